```python
import math
import jax, jax.numpy as jnp
from jax import lax
import numpy as np

D_MODEL = 2048
BATCH = 16
SEQ = 2048
DEPTH = 2

N_META = 16
CHUNK = 128
NORM_EPS = 1e-6
HEAD_NORM_EPS = 1e-5
S5_WIDTH = D_MODEL // 2
S5_GROUP_SIZE = 16
S5_GROUPS = S5_WIDTH // S5_GROUP_SIZE
S5_STATE = 64
MLSTM_WIDTH = 3 * D_MODEL // 2
MLSTM_HEADS = 8
MLSTM_HEAD_DIM = MLSTM_WIDTH // MLSTM_HEADS
MLSTM_CONV = 4
QKV_BLOCK = 4
AB_INNER = S5_WIDTH + MLSTM_WIDTH
AB_IN = 2 * AB_INNER
SSD_INNER = 2 * D_MODEL
SSD_HEAD_DIM = 64
SSD_HEADS = SSD_INNER // SSD_HEAD_DIM
SSD_STATE = 128
SSD_GROUPS = 8
SSD_HPG = SSD_HEADS // SSD_GROUPS
SSD_CONV = 4
SSD_CONV_DIM = SSD_INNER + 2 * SSD_GROUPS * SSD_STATE
SSD_IN = SSD_INNER + SSD_CONV_DIM + SSD_HEADS
N_EVEN = (DEPTH + 1) // 2
N_ODD = DEPTH // 2

kernel_name = 'hybrid_s5_mlstm_ssd_meta'

F32 = jnp.float32


def rmsnorm(x, g):
    xf = x.astype(F32)
    y = xf * lax.rsqrt(jnp.mean(xf * xf, axis=-1, keepdims=True) + NORM_EPS)
    return (y * g.astype(F32)).astype(x.dtype)


def causal_dwconv(x, w, b):
    k, c = w.shape
    y = lax.conv_general_dilated(x, w[:, None, :].astype(x.dtype), window_strides=(1,),
                                 padding=[(k - 1, 0)], dimension_numbers=('NWC', 'WIO', 'NWC'),
                                 feature_group_count=c)
    return y + b.astype(x.dtype)


def run_chunked(step, carry, seqs):
    bsz = seqs[0].shape[0]
    carry, y_meta = step(carry, tuple(a[:, :N_META] for a in seqs))

    def to_chunks(a):
        r = a[:, N_META:]
        r = r.reshape(bsz, r.shape[1] // CHUNK, CHUNK, *r.shape[2:])
        return jnp.moveaxis(r, 1, 0)

    _, y_real = lax.scan(step, carry, tuple(to_chunks(a) for a in seqs))
    y_real = jnp.moveaxis(y_real, 0, 1)
    y_real = y_real.reshape(bsz, -1, *y_real.shape[3:])
    return jnp.concatenate([y_meta, y_real], axis=1)


def cmul(ar, ai, br, bi):
    return ar * br - ai * bi, ar * bi + ai * br


def s5_mixer(u, lam_re, lam_im, log_dt, b_re, b_im, c_re, c_im, d, glu_w, glu_b):
    bsz, t_len, _ = u.shape
    uf = u.astype(F32).reshape(bsz, t_len, S5_GROUPS, S5_GROUP_SIZE)
    dt = jnp.exp(log_dt.astype(F32))[:, None]
    lr, li = lam_re.astype(F32), lam_im.astype(F32)
    mag = jnp.exp(lr * dt)
    ar, ai = mag * jnp.cos(li * dt), mag * jnp.sin(li * dt)
    den = lr * lr + li * li
    qr = ((ar - 1.0) * lr + ai * li) / den
    qi = (ai * lr - (ar - 1.0) * li) / den
    bbr, bbi = cmul(qr[..., None], qi[..., None], b_re.astype(F32), b_im.astype(F32))
    cr, ci = c_re.astype(F32), c_im.astype(F32)

    def combine(e1, e2):
        a1r, a1i, s1r, s1i = e1
        a2r, a2i, s2r, s2i = e2
        pr, pi_ = cmul(a1r, a1i, a2r, a2i)
        tr, ti = cmul(a2r, a2i, s1r, s1i)
        return pr, pi_, tr + s2r, ti + s2i

    def step(carry, inp):
        sr, si = carry
        (uc,) = inp
        bur = jnp.einsum('bcgh,gph->bcgp', uc, bbr)
        bui = jnp.einsum('bcgh,gph->bcgp', uc, bbi)
        shp = bur.shape
        pr, pi_, xr, xi = lax.associative_scan(
            combine, (jnp.broadcast_to(ar, shp), jnp.broadcast_to(ai, shp), bur, bui), axis=1)
        hr, hi = cmul(pr, pi_, sr[:, None], si[:, None])
        xr, xi = xr + hr, xi + hi
        y = jnp.einsum('ghp,bcgp->bcgh', cr, xr) - jnp.einsum('ghp,bcgp->bcgh', ci, xi)
        return (xr[:, -1], xi[:, -1]), y

    zeros = jnp.zeros((bsz, S5_GROUPS, S5_STATE), F32)
    y = run_chunked(step, (zeros, zeros), (uf,))
    y = (y + d.astype(F32).reshape(S5_GROUPS, S5_GROUP_SIZE) * uf).reshape(bsz, t_len, S5_WIDTH)
    g = jax.nn.gelu(y)
    return g * jax.nn.sigmoid(g @ glu_w.astype(F32) + glu_b.astype(F32))


def mlstm_mixer(xm, conv_w, conv_b, wq, wk, wv, w_gate, b_gate, norm_w, skip):
    bsz, t_len, _ = xm.shape
    xf = xm.astype(F32)
    xc = jax.nn.silu(causal_dwconv(xf, conv_w.astype(F32), conv_b.astype(F32)))

    def headwise(a, w):
        nb = w.shape[0]
        return jnp.einsum('btni,nio->btno', a.reshape(bsz, t_len, nb, QKV_BLOCK),
                          w.astype(F32)).reshape(bsz, t_len, MLSTM_WIDTH)

    q = headwise(xc, wq)
    k = headwise(xc, wk)
    v = headwise(xf, wv)
    wg = w_gate.astype(F32)
    gates = (q @ wg[:MLSTM_WIDTH] + k @ wg[MLSTM_WIDTH:2 * MLSTM_WIDTH]
             + v @ wg[2 * MLSTM_WIDTH:] + b_gate.astype(F32))
    ig = gates[..., :MLSTM_HEADS]
    lf = jax.nn.log_sigmoid(gates[..., MLSTM_HEADS:])
    hs = (bsz, t_len, MLSTM_HEADS, MLSTM_HEAD_DIM)
    qh = q.reshape(hs) * (MLSTM_HEAD_DIM ** -0.5)
    kh = k.reshape(hs)
    vh = v.reshape(hs)

    def step(carry, inp):
        cmat, nvec, m = carry
        qc, kc, vc, igc, lfc = inp
        c = qc.shape[1]
        bcum = jnp.cumsum(lfc, axis=1)
        causal = jnp.tril(jnp.ones((c, c), bool))[None, :, :, None]
        dmat = bcum[:, :, None, :] - bcum[:, None, :, :] + igc[:, None, :, :]
        dmat = jnp.where(causal, dmat, -jnp.inf)
        inter = bcum + m[:, None, :]
        mt = jnp.maximum(inter, dmat.max(axis=2))
        wt = jnp.exp(dmat - mt[:, :, None, :])
        w_prev = jnp.exp(inter - mt)
        s = jnp.einsum('bthd,bshd->btsh', qc, kc) * wt
        num = (jnp.einsum('btsh,bshe->bthe', s, vc)
               + w_prev[..., None] * jnp.einsum('bthd,bhde->bthe', qc, cmat))
        den = s.sum(axis=2) + w_prev * jnp.einsum('bthd,bhd->bth', qc, nvec)
        h = num / jnp.maximum(jnp.abs(den), jnp.exp(-mt))[..., None]
        blast = bcum[:, -1]
        g = blast[:, None, :] - bcum + igc
        m_new = jnp.maximum(blast + m, g.max(axis=1))
        decay = jnp.exp(blast + m - m_new)
        wkc = jnp.exp(g - m_new[:, None, :])[..., None] * kc
        c_new = decay[..., None, None] * cmat + jnp.einsum('bshd,bshe->bhde', wkc, vc)
        n_new = decay[..., None] * nvec + wkc.sum(axis=1)
        return (c_new, n_new, m_new), h

    carry0 = (jnp.zeros((bsz, MLSTM_HEADS, MLSTM_HEAD_DIM, MLSTM_HEAD_DIM), F32),
              jnp.zeros((bsz, MLSTM_HEADS, MLSTM_HEAD_DIM), F32),
              jnp.zeros((bsz, MLSTM_HEADS), F32))
    h = run_chunked(step, carry0, (qh, kh, vh, ig, lf))
    mu = jnp.mean(h, axis=-1, keepdims=True)
    var = jnp.mean(jnp.square(h - mu), axis=-1, keepdims=True)
    hn = (h - mu) * lax.rsqrt(var + HEAD_NORM_EPS) * norm_w.astype(F32).reshape(MLSTM_HEADS, MLSTM_HEAD_DIM)
    return hn.reshape(bsz, t_len, MLSTM_WIDTH) + skip.astype(F32) * xc


def ab_layer(x, norm_g, w_in, s5_lambda_re, s5_lambda_im, s5_log_dt, s5_b_re, s5_b_im, s5_c_re,
             s5_c_im, s5_d, s5_glu_w, s5_glu_b, ml_conv_w, ml_conv_b, ml_wq, ml_wk, ml_wv,
             ml_w_gate, ml_b_gate, ml_norm, ml_skip, w_out):
    p = rmsnorm(x, norm_g) @ w_in
    u_a, z_a, x_b, z_b = jnp.split(p, [S5_WIDTH, 2 * S5_WIDTH, 2 * S5_WIDTH + MLSTM_WIDTH], axis=-1)
    y_a = s5_mixer(u_a, s5_lambda_re, s5_lambda_im, s5_log_dt, s5_b_re, s5_b_im, s5_c_re, s5_c_im,
                   s5_d, s5_glu_w, s5_glu_b) * jax.nn.silu(z_a.astype(F32))
    y_b = mlstm_mixer(x_b, ml_conv_w, ml_conv_b, ml_wq, ml_wk, ml_wv, ml_w_gate, ml_b_gate,
                      ml_norm, ml_skip) * jax.nn.silu(z_b.astype(F32))
    y = jnp.concatenate([y_a, y_b], axis=-1).astype(x.dtype)
    return x + y @ w_out


def ssd_layer(x, norm_g, w_in, conv_w, conv_b, dt_bias, a_log, d, gnorm, w_out):
    bsz, t_len, _ = x.shape
    p = (rmsnorm(x, norm_g) @ w_in).astype(F32)
    z, xbc, dt = jnp.split(p, [SSD_INNER, SSD_INNER + SSD_CONV_DIM], axis=-1)
    xbc = jax.nn.silu(causal_dwconv(xbc, conv_w.astype(F32), conv_b.astype(F32)))
    xs, bm, cm = jnp.split(xbc, [SSD_INNER, SSD_INNER + SSD_GROUPS * SSD_STATE], axis=-1)
    xs = xs.reshape(bsz, t_len, SSD_GROUPS, SSD_HPG, SSD_HEAD_DIM)
    bm = bm.reshape(bsz, t_len, SSD_GROUPS, SSD_STATE)
    cm = cm.reshape(bsz, t_len, SSD_GROUPS, SSD_STATE)
    dt = jax.nn.softplus(dt + dt_bias.astype(F32)).reshape(bsz, t_len, SSD_GROUPS, SSD_HPG)
    a = -jnp.exp(a_log.astype(F32)).reshape(SSD_GROUPS, SSD_HPG)

    def step(state, inp):
        xc, dtc, bc, cc = inp
        c = xc.shape[1]
        cum = jnp.cumsum(dtc * a, axis=1)
        causal = jnp.tril(jnp.ones((c, c), bool))[None, :, :, None, None]
        seg = jnp.exp(jnp.where(causal, cum[:, :, None] - cum[:, None], -jnp.inf))
        cb = jnp.einsum('btgn,bsgn->btsg', cc, bc)
        w = cb[..., None] * seg * dtc[:, None]
        y = (jnp.einsum('btsgr,bsgrp->btgrp', w, xc)
             + jnp.exp(cum)[..., None] * jnp.einsum('btgn,bgrpn->btgrp', cc, state))
        last = cum[:, -1]
        dec = jnp.exp(last[:, None] - cum) * dtc
        state = (jnp.exp(last)[..., None, None] * state
                 + jnp.einsum('bsgr,bsgn,bsgrp->bgrpn', dec, bc, xc))
        return state, y

    state0 = jnp.zeros((bsz, SSD_GROUPS, SSD_HPG, SSD_HEAD_DIM, SSD_STATE), F32)
    y = run_chunked(step, state0, (xs, dt, bm, cm))
    y = y + d.astype(F32).reshape(SSD_GROUPS, SSD_HPG, 1) * xs
    y = y.reshape(bsz, t_len, SSD_INNER) * jax.nn.silu(z)
    yg = y.reshape(bsz, t_len, SSD_GROUPS, -1)
    yg = yg * lax.rsqrt(jnp.mean(yg * yg, axis=-1, keepdims=True) + NORM_EPS)
    y = yg.reshape(bsz, t_len, SSD_INNER) * gnorm.astype(F32)
    return x + y.astype(x.dtype) @ w_out


def _fwd_setup_inputs(seed: int = 0) -> dict:
    key = jax.random.key(seed)
    ks = iter(jax.random.split(key, 48))
    nrm = lambda shape, s: jax.random.normal(next(ks), shape, F32) * s
    ne, no = N_EVEN, N_ODD
    lam_im = jnp.pi * jnp.arange(S5_STATE, dtype=F32)
    gate_b = jnp.concatenate([
        nrm((ne, MLSTM_HEADS), 0.1),
        jnp.linspace(3.0, 6.0, MLSTM_HEADS, dtype=F32)[None] + nrm((ne, MLSTM_HEADS), 0.01)], axis=-1)
    dt0 = jnp.exp(jax.random.uniform(next(ks), (no, SSD_HEADS), F32, math.log(1e-3), math.log(1e-1)))
    return {
        'x': nrm((BATCH, SEQ, D_MODEL), 1.0),
        'meta_tokens': nrm((N_META, D_MODEL), 1.0),
        'ab_norm': 1.0 + nrm((ne, D_MODEL), 0.02),
        'ab_w_in': nrm((ne, D_MODEL, AB_IN), D_MODEL ** -0.5),
        's5_lambda_re': -0.5 + nrm((ne, S5_GROUPS, S5_STATE), 0.01),
        's5_lambda_im': lam_im + nrm((ne, S5_GROUPS, S5_STATE), 0.01),
        's5_log_dt': jax.random.uniform(next(ks), (ne, S5_GROUPS), F32, math.log(1e-3), math.log(1e-1)),
        's5_b_re': nrm((ne, S5_GROUPS, S5_STATE, S5_GROUP_SIZE), (2 * S5_GROUP_SIZE) ** -0.5),
        's5_b_im': nrm((ne, S5_GROUPS, S5_STATE, S5_GROUP_SIZE), (2 * S5_GROUP_SIZE) ** -0.5),
        's5_c_re': nrm((ne, S5_GROUPS, S5_GROUP_SIZE, S5_STATE), (2 * S5_STATE) ** -0.5),
        's5_c_im': nrm((ne, S5_GROUPS, S5_GROUP_SIZE, S5_STATE), (2 * S5_STATE) ** -0.5),
        's5_d': nrm((ne, S5_WIDTH), 1.0),
        's5_glu_w': nrm((ne, S5_WIDTH, S5_WIDTH), S5_WIDTH ** -0.5),
        's5_glu_b': nrm((ne, S5_WIDTH), 0.01),
        'ml_conv_w': nrm((ne, MLSTM_CONV, MLSTM_WIDTH), MLSTM_CONV ** -0.5),
        'ml_conv_b': nrm((ne, MLSTM_WIDTH), 0.01),
        'ml_wq': nrm((ne, MLSTM_WIDTH // QKV_BLOCK, QKV_BLOCK, QKV_BLOCK), QKV_BLOCK ** -0.5),
        'ml_wk': nrm((ne, MLSTM_WIDTH // QKV_BLOCK, QKV_BLOCK, QKV_BLOCK), QKV_BLOCK ** -0.5),
        'ml_wv': nrm((ne, MLSTM_WIDTH // QKV_BLOCK, QKV_BLOCK, QKV_BLOCK), QKV_BLOCK ** -0.5),
        'ml_w_gate': nrm((ne, 3 * MLSTM_WIDTH, 2 * MLSTM_HEADS), (3 * MLSTM_WIDTH) ** -0.5),
        'ml_b_gate': gate_b,
        'ml_norm': 1.0 + nrm((ne, MLSTM_WIDTH), 0.02),
        'ml_skip': 1.0 + nrm((ne, MLSTM_WIDTH), 0.02),
        'ab_w_out': nrm((ne, AB_INNER, D_MODEL), AB_INNER ** -0.5),
        'ssd_norm': 1.0 + nrm((no, D_MODEL), 0.02),
        'ssd_w_in': nrm((no, D_MODEL, SSD_IN), D_MODEL ** -0.5),
        'ssd_conv_w': nrm((no, SSD_CONV, SSD_CONV_DIM), SSD_CONV ** -0.5),
        'ssd_conv_b': nrm((no, SSD_CONV_DIM), 0.01),
        'ssd_dt_bias': dt0 + jnp.log(-jnp.expm1(-dt0)),
        'ssd_a_log': jnp.log(jax.random.uniform(next(ks), (no, SSD_HEADS), F32, 1.0, 16.0)),
        'ssd_d': 1.0 + nrm((no, SSD_HEADS), 0.02),
        'ssd_gnorm': 1.0 + nrm((no, SSD_INNER), 0.02),
        'ssd_w_out': nrm((no, SSD_INNER, D_MODEL), SSD_INNER ** -0.5),
        'final_norm': 1.0 + nrm((D_MODEL,), 0.02),
    }


def _fwd_reference(x, meta_tokens, ab_norm, ab_w_in, s5_lambda_re, s5_lambda_im, s5_log_dt, s5_b_re,
              s5_b_im, s5_c_re, s5_c_im, s5_d, s5_glu_w, s5_glu_b, ml_conv_w, ml_conv_b, ml_wq,
              ml_wk, ml_wv, ml_w_gate, ml_b_gate, ml_norm, ml_skip, ab_w_out, ssd_norm, ssd_w_in,
              ssd_conv_w, ssd_conv_b, ssd_dt_bias, ssd_a_log, ssd_d, ssd_gnorm, ssd_w_out, final_norm):
    bsz = x.shape[0]
    meta = jnp.broadcast_to(meta_tokens[None].astype(x.dtype), (bsz, N_META, x.shape[-1]))
    h = jnp.concatenate([meta, x], axis=1)
    for layer in range(DEPTH):
        i = layer // 2
        if layer % 2 == 0:
            h = ab_layer(h, ab_norm[i], ab_w_in[i], s5_lambda_re[i], s5_lambda_im[i], s5_log_dt[i],
                         s5_b_re[i], s5_b_im[i], s5_c_re[i], s5_c_im[i], s5_d[i], s5_glu_w[i],
                         s5_glu_b[i], ml_conv_w[i], ml_conv_b[i], ml_wq[i], ml_wk[i], ml_wv[i],
                         ml_w_gate[i], ml_b_gate[i], ml_norm[i], ml_skip[i], ab_w_out[i])
        else:
            h = ssd_layer(h, ssd_norm[i], ssd_w_in[i], ssd_conv_w[i], ssd_conv_b[i], ssd_dt_bias[i],
                          ssd_a_log[i], ssd_d[i], ssd_gnorm[i], ssd_w_out[i])
    return rmsnorm(h, final_norm)[:, N_META:]


import jax as _jax
import jax.numpy as _jnp

TWIN_FORMAT = 'train_step'
FWD_PARAMS = ['x', 'meta_tokens', 'ab_norm', 'ab_w_in', 's5_lambda_re', 's5_lambda_im', 's5_log_dt', 's5_b_re', 's5_b_im', 's5_c_re', 's5_c_im', 's5_d', 's5_glu_w', 's5_glu_b', 'ml_conv_w', 'ml_conv_b', 'ml_wq', 'ml_wk', 'ml_wv', 'ml_w_gate', 'ml_b_gate', 'ml_norm', 'ml_skip', 'ab_w_out', 'ssd_norm', 'ssd_w_in', 'ssd_conv_w', 'ssd_conv_b', 'ssd_dt_bias', 'ssd_a_log', 'ssd_d', 'ssd_gnorm', 'ssd_w_out', 'final_norm']
TWIN_WEIGHTS = ['meta_tokens', 'ab_norm', 'ab_w_in', 's5_lambda_re', 's5_lambda_im', 's5_log_dt', 's5_b_re', 's5_b_im', 's5_c_re', 's5_c_im', 's5_d', 's5_glu_w', 's5_glu_b', 'ml_conv_w', 'ml_conv_b', 'ml_wq', 'ml_wk', 'ml_wv', 'ml_w_gate', 'ml_b_gate', 'ml_norm', 'ml_skip', 'ab_w_out', 'ssd_norm', 'ssd_w_in', 'ssd_conv_w', 'ssd_conv_b', 'ssd_dt_bias', 'ssd_a_log', 'ssd_d', 'ssd_gnorm', 'ssd_w_out', 'final_norm']
TWIN_DIFF_INPUT = 'x'
TWIN_INPUTS = ['x', 'meta_tokens', 'ab_norm', 'ab_w_in', 's5_lambda_re', 's5_lambda_im', 's5_log_dt', 's5_b_re', 's5_b_im', 's5_c_re', 's5_c_im', 's5_d', 's5_glu_w', 's5_glu_b', 'ml_conv_w', 'ml_conv_b', 'ml_wq', 'ml_wk', 'ml_wv', 'ml_w_gate', 'ml_b_gate', 'ml_norm', 'ml_skip', 'ab_w_out', 'ssd_norm', 'ssd_w_in', 'ssd_conv_w', 'ssd_conv_b', 'ssd_dt_bias', 'ssd_a_log', 'ssd_d', 'ssd_gnorm', 'ssd_w_out', 'final_norm', 'loss_target', 'm_meta_tokens', 'm_ab_norm', 'm_ab_w_in', 'm_s5_lambda_re', 'm_s5_lambda_im', 'm_s5_log_dt', 'm_s5_b_re', 'm_s5_b_im', 'm_s5_c_re', 'm_s5_c_im', 'm_s5_d', 'm_s5_glu_w', 'm_s5_glu_b', 'm_ml_conv_w', 'm_ml_conv_b', 'm_ml_wq', 'm_ml_wk', 'm_ml_wv', 'm_ml_w_gate', 'm_ml_b_gate', 'm_ml_norm', 'm_ml_skip', 'm_ab_w_out', 'm_ssd_norm', 'm_ssd_w_in', 'm_ssd_conv_w', 'm_ssd_conv_b', 'm_ssd_dt_bias', 'm_ssd_a_log', 'm_ssd_d', 'm_ssd_gnorm', 'm_ssd_w_out', 'm_final_norm', 'v_meta_tokens', 'v_ab_norm', 'v_ab_w_in', 'v_s5_lambda_re', 'v_s5_lambda_im', 'v_s5_log_dt', 'v_s5_b_re', 'v_s5_b_im', 'v_s5_c_re', 'v_s5_c_im', 'v_s5_d', 'v_s5_glu_w', 'v_s5_glu_b', 'v_ml_conv_w', 'v_ml_conv_b', 'v_ml_wq', 'v_ml_wk', 'v_ml_wv', 'v_ml_w_gate', 'v_ml_b_gate', 'v_ml_norm', 'v_ml_skip', 'v_ab_w_out', 'v_ssd_norm', 'v_ssd_w_in', 'v_ssd_conv_w', 'v_ssd_conv_b', 'v_ssd_dt_bias', 'v_ssd_a_log', 'v_ssd_d', 'v_ssd_gnorm', 'v_ssd_w_out', 'v_final_norm']
TWIN_OUTPUTS = ['loss', 'grad_x', 'grad_meta_tokens', 'grad_ab_norm', 'grad_ab_w_in', 'grad_s5_lambda_re', 'grad_s5_lambda_im', 'grad_s5_log_dt', 'grad_s5_b_re', 'grad_s5_b_im', 'grad_s5_c_re', 'grad_s5_c_im', 'grad_s5_d', 'grad_s5_glu_w', 'grad_s5_glu_b', 'grad_ml_conv_w', 'grad_ml_conv_b', 'grad_ml_wq', 'grad_ml_wk', 'grad_ml_wv', 'grad_ml_w_gate', 'grad_ml_b_gate', 'grad_ml_norm', 'grad_ml_skip', 'grad_ab_w_out', 'grad_ssd_norm', 'grad_ssd_w_in', 'grad_ssd_conv_w', 'grad_ssd_conv_b', 'grad_ssd_dt_bias', 'grad_ssd_a_log', 'grad_ssd_d', 'grad_ssd_gnorm', 'grad_ssd_w_out', 'grad_final_norm', 'delta_meta_tokens', 'delta_ab_norm', 'delta_ab_w_in', 'delta_s5_lambda_re', 'delta_s5_lambda_im', 'delta_s5_log_dt', 'delta_s5_b_re', 'delta_s5_b_im', 'delta_s5_c_re', 'delta_s5_c_im', 'delta_s5_d', 'delta_s5_glu_w', 'delta_s5_glu_b', 'delta_ml_conv_w', 'delta_ml_conv_b', 'delta_ml_wq', 'delta_ml_wk', 'delta_ml_wv', 'delta_ml_w_gate', 'delta_ml_b_gate', 'delta_ml_norm', 'delta_ml_skip', 'delta_ab_w_out', 'delta_ssd_norm', 'delta_ssd_w_in', 'delta_ssd_conv_w', 'delta_ssd_conv_b', 'delta_ssd_dt_bias', 'delta_ssd_a_log', 'delta_ssd_d', 'delta_ssd_gnorm', 'delta_ssd_w_out', 'delta_final_norm', 'new_m_meta_tokens', 'new_m_ab_norm', 'new_m_ab_w_in', 'new_m_s5_lambda_re', 'new_m_s5_lambda_im', 'new_m_s5_log_dt', 'new_m_s5_b_re', 'new_m_s5_b_im', 'new_m_s5_c_re', 'new_m_s5_c_im', 'new_m_s5_d', 'new_m_s5_glu_w', 'new_m_s5_glu_b', 'new_m_ml_conv_w', 'new_m_ml_conv_b', 'new_m_ml_wq', 'new_m_ml_wk', 'new_m_ml_wv', 'new_m_ml_w_gate', 'new_m_ml_b_gate', 'new_m_ml_norm', 'new_m_ml_skip', 'new_m_ab_w_out', 'new_m_ssd_norm', 'new_m_ssd_w_in', 'new_m_ssd_conv_w', 'new_m_ssd_conv_b', 'new_m_ssd_dt_bias', 'new_m_ssd_a_log', 'new_m_ssd_d', 'new_m_ssd_gnorm', 'new_m_ssd_w_out', 'new_m_final_norm', 'new_v_meta_tokens', 'new_v_ab_norm', 'new_v_ab_w_in', 'new_v_s5_lambda_re', 'new_v_s5_lambda_im', 'new_v_s5_log_dt', 'new_v_s5_b_re', 'new_v_s5_b_im', 'new_v_s5_c_re', 'new_v_s5_c_im', 'new_v_s5_d', 'new_v_s5_glu_w', 'new_v_s5_glu_b', 'new_v_ml_conv_w', 'new_v_ml_conv_b', 'new_v_ml_wq', 'new_v_ml_wk', 'new_v_ml_wv', 'new_v_ml_w_gate', 'new_v_ml_b_gate', 'new_v_ml_norm', 'new_v_ml_skip', 'new_v_ab_w_out', 'new_v_ssd_norm', 'new_v_ssd_w_in', 'new_v_ssd_conv_w', 'new_v_ssd_conv_b', 'new_v_ssd_dt_bias', 'new_v_ssd_a_log', 'new_v_ssd_d', 'new_v_ssd_gnorm', 'new_v_ssd_w_out', 'new_v_final_norm']
TWIN_LEAF_KINDS = {'loss': 'loss', 'grad_x': 'grad_x', 'grad_meta_tokens': 'grad_w', 'grad_ab_norm': 'grad_w', 'grad_ab_w_in': 'grad_w', 'grad_s5_lambda_re': 'grad_w', 'grad_s5_lambda_im': 'grad_w', 'grad_s5_log_dt': 'grad_w', 'grad_s5_b_re': 'grad_w', 'grad_s5_b_im': 'grad_w', 'grad_s5_c_re': 'grad_w', 'grad_s5_c_im': 'grad_w', 'grad_s5_d': 'grad_w', 'grad_s5_glu_w': 'grad_w', 'grad_s5_glu_b': 'grad_w', 'grad_ml_conv_w': 'grad_w', 'grad_ml_conv_b': 'grad_w', 'grad_ml_wq': 'grad_w', 'grad_ml_wk': 'grad_w', 'grad_ml_wv': 'grad_w', 'grad_ml_w_gate': 'grad_w', 'grad_ml_b_gate': 'grad_w', 'grad_ml_norm': 'grad_w', 'grad_ml_skip': 'grad_w', 'grad_ab_w_out': 'grad_w', 'grad_ssd_norm': 'grad_w', 'grad_ssd_w_in': 'grad_w', 'grad_ssd_conv_w': 'grad_w', 'grad_ssd_conv_b': 'grad_w', 'grad_ssd_dt_bias': 'grad_w', 'grad_ssd_a_log': 'grad_w', 'grad_ssd_d': 'grad_w', 'grad_ssd_gnorm': 'grad_w', 'grad_ssd_w_out': 'grad_w', 'grad_final_norm': 'grad_w', 'delta_meta_tokens': 'delta_w', 'delta_ab_norm': 'delta_w', 'delta_ab_w_in': 'delta_w', 'delta_s5_lambda_re': 'delta_w', 'delta_s5_lambda_im': 'delta_w', 'delta_s5_log_dt': 'delta_w', 'delta_s5_b_re': 'delta_w', 'delta_s5_b_im': 'delta_w', 'delta_s5_c_re': 'delta_w', 'delta_s5_c_im': 'delta_w', 'delta_s5_d': 'delta_w', 'delta_s5_glu_w': 'delta_w', 'delta_s5_glu_b': 'delta_w', 'delta_ml_conv_w': 'delta_w', 'delta_ml_conv_b': 'delta_w', 'delta_ml_wq': 'delta_w', 'delta_ml_wk': 'delta_w', 'delta_ml_wv': 'delta_w', 'delta_ml_w_gate': 'delta_w', 'delta_ml_b_gate': 'delta_w', 'delta_ml_norm': 'delta_w', 'delta_ml_skip': 'delta_w', 'delta_ab_w_out': 'delta_w', 'delta_ssd_norm': 'delta_w', 'delta_ssd_w_in': 'delta_w', 'delta_ssd_conv_w': 'delta_w', 'delta_ssd_conv_b': 'delta_w', 'delta_ssd_dt_bias': 'delta_w', 'delta_ssd_a_log': 'delta_w', 'delta_ssd_d': 'delta_w', 'delta_ssd_gnorm': 'delta_w', 'delta_ssd_w_out': 'delta_w', 'delta_final_norm': 'delta_w', 'new_m_meta_tokens': 'new_m', 'new_m_ab_norm': 'new_m', 'new_m_ab_w_in': 'new_m', 'new_m_s5_lambda_re': 'new_m', 'new_m_s5_lambda_im': 'new_m', 'new_m_s5_log_dt': 'new_m', 'new_m_s5_b_re': 'new_m', 'new_m_s5_b_im': 'new_m', 'new_m_s5_c_re': 'new_m', 'new_m_s5_c_im': 'new_m', 'new_m_s5_d': 'new_m', 'new_m_s5_glu_w': 'new_m', 'new_m_s5_glu_b': 'new_m', 'new_m_ml_conv_w': 'new_m', 'new_m_ml_conv_b': 'new_m', 'new_m_ml_wq': 'new_m', 'new_m_ml_wk': 'new_m', 'new_m_ml_wv': 'new_m', 'new_m_ml_w_gate': 'new_m', 'new_m_ml_b_gate': 'new_m', 'new_m_ml_norm': 'new_m', 'new_m_ml_skip': 'new_m', 'new_m_ab_w_out': 'new_m', 'new_m_ssd_norm': 'new_m', 'new_m_ssd_w_in': 'new_m', 'new_m_ssd_conv_w': 'new_m', 'new_m_ssd_conv_b': 'new_m', 'new_m_ssd_dt_bias': 'new_m', 'new_m_ssd_a_log': 'new_m', 'new_m_ssd_d': 'new_m', 'new_m_ssd_gnorm': 'new_m', 'new_m_ssd_w_out': 'new_m', 'new_m_final_norm': 'new_m', 'new_v_meta_tokens': 'new_v', 'new_v_ab_norm': 'new_v', 'new_v_ab_w_in': 'new_v', 'new_v_s5_lambda_re': 'new_v', 'new_v_s5_lambda_im': 'new_v', 'new_v_s5_log_dt': 'new_v', 'new_v_s5_b_re': 'new_v', 'new_v_s5_b_im': 'new_v', 'new_v_s5_c_re': 'new_v', 'new_v_s5_c_im': 'new_v', 'new_v_s5_d': 'new_v', 'new_v_s5_glu_w': 'new_v', 'new_v_s5_glu_b': 'new_v', 'new_v_ml_conv_w': 'new_v', 'new_v_ml_conv_b': 'new_v', 'new_v_ml_wq': 'new_v', 'new_v_ml_wk': 'new_v', 'new_v_ml_wv': 'new_v', 'new_v_ml_w_gate': 'new_v', 'new_v_ml_b_gate': 'new_v', 'new_v_ml_norm': 'new_v', 'new_v_ml_skip': 'new_v', 'new_v_ab_w_out': 'new_v', 'new_v_ssd_norm': 'new_v', 'new_v_ssd_w_in': 'new_v', 'new_v_ssd_conv_w': 'new_v', 'new_v_ssd_conv_b': 'new_v', 'new_v_ssd_dt_bias': 'new_v', 'new_v_ssd_a_log': 'new_v', 'new_v_ssd_d': 'new_v', 'new_v_ssd_gnorm': 'new_v', 'new_v_ssd_w_out': 'new_v', 'new_v_final_norm': 'new_v'}


def _forward(args):
    return _fwd_reference(*[args[k] for k in FWD_PARAMS])


def _output_shape():
    out = _jax.eval_shape(lambda: _forward(_fwd_setup_inputs(0)))
    return out.shape, out.dtype

N_MICROBATCH = 1
ADAM_LR = 0.001
ADAM_B1 = 0.9
ADAM_B2 = 0.999
ADAM_EPS = 1e-08
ADAM_WD = 0.01
ADAM_STEP = 10
PER_EXAMPLE_BATCH_AXIS = {'x': 0, 'loss_target': 0}
SHARED_INPUTS = []
_WEIGHT_DTYPES = {'meta_tokens': _jnp.float32, 'ab_norm': _jnp.float32, 'ab_w_in': _jnp.float32, 's5_lambda_re': _jnp.float32, 's5_lambda_im': _jnp.float32, 's5_log_dt': _jnp.float32, 's5_b_re': _jnp.float32, 's5_b_im': _jnp.float32, 's5_c_re': _jnp.float32, 's5_c_im': _jnp.float32, 's5_d': _jnp.float32, 's5_glu_w': _jnp.float32, 's5_glu_b': _jnp.float32, 'ml_conv_w': _jnp.float32, 'ml_conv_b': _jnp.float32, 'ml_wq': _jnp.float32, 'ml_wk': _jnp.float32, 'ml_wv': _jnp.float32, 'ml_w_gate': _jnp.float32, 'ml_b_gate': _jnp.float32, 'ml_norm': _jnp.float32, 'ml_skip': _jnp.float32, 'ab_w_out': _jnp.float32, 'ssd_norm': _jnp.float32, 'ssd_w_in': _jnp.float32, 'ssd_conv_w': _jnp.float32, 'ssd_conv_b': _jnp.float32, 'ssd_dt_bias': _jnp.float32, 'ssd_a_log': _jnp.float32, 'ssd_d': _jnp.float32, 'ssd_gnorm': _jnp.float32, 'ssd_w_out': _jnp.float32, 'final_norm': _jnp.float32}
MOMENT_SCALE = {'meta_tokens': 5.802023e-03, 'ab_norm': 1.177518e-01, 'ab_w_in': 5.840638e-02, 's5_lambda_re': 7.594457e-04, 's5_lambda_im': 7.369353e-04, 's5_log_dt': 4.986728e-01, 's5_b_re': 4.773515e-04, 's5_b_im': 4.720899e-04, 's5_c_re': 9.438781e-04, 's5_c_im': 9.430893e-04, 's5_d': 1.766913e-02, 's5_glu_w': 4.174752e-03, 's5_glu_b': 6.988508e-03, 'ml_conv_w': 6.828199e-02, 'ml_conv_b': 6.521836e-02, 'ml_wq': 4.565207e-02, 'ml_wk': 4.485631e-02, 'ml_wv': 4.439894e-02, 'ml_w_gate': 3.667534e-01, 'ml_b_gate': 1.356292e-01, 'ml_norm': 4.088240e-02, 'ml_skip': 2.353380e-02, 'ab_w_out': 5.720932e-02, 'ssd_norm': 9.252867e-02, 'ssd_w_in': 3.846916e-02, 'ssd_conv_w': 3.482755e-02, 'ssd_conv_b': 4.954253e-02, 'ssd_dt_bias': 8.461687e-02, 'ssd_a_log': 1.095774e-01, 'ssd_d': 1.759747e-01, 'ssd_gnorm': 4.024246e-02, 'ssd_w_out': 5.804009e-02, 'final_norm': 1.596730e+01}


def _to_microbatches(a, axis):
    t = _jnp.moveaxis(a, axis, 0)
    t = t.reshape((N_MICROBATCH, t.shape[0] // N_MICROBATCH) + t.shape[1:])
    return _jnp.moveaxis(t, 1, axis + 1)


def setup_inputs(seed: int = 0) -> dict:
    inp = _fwd_setup_inputs(seed)
    key = _jax.random.fold_in(_jax.random.key(seed), 7919)
    shape, _ = _output_shape()
    out = dict(inp)
    out["loss_target"] = _jax.random.normal(_jax.random.fold_in(key, 0), shape, _jnp.float32)
    for i, name in enumerate(TWIN_WEIGHTS):
        w = inp[name].astype(_jnp.float32)
        if MOMENT_SCALE is None:
            s = _jnp.sqrt(_jnp.mean(_jnp.square(w)) + 1e-30)
        else:
            s = MOMENT_SCALE[name]
        km, kv = _jax.random.split(_jax.random.fold_in(key, i + 1))
        out[name] = w
        out["m_" + name] = s * _jax.random.normal(km, w.shape, _jnp.float32)
        out["v_" + name] = (s * s) * _jax.random.uniform(kv, w.shape, _jnp.float32, 0.5, 1.5)
    if N_MICROBATCH > 1:
        for name, axis in PER_EXAMPLE_BATCH_AXIS.items():
            out[name] = _to_microbatches(out[name], axis)
    return {'x': out['x'], 'meta_tokens': out['meta_tokens'], 'ab_norm': out['ab_norm'], 'ab_w_in': out['ab_w_in'], 's5_lambda_re': out['s5_lambda_re'], 's5_lambda_im': out['s5_lambda_im'], 's5_log_dt': out['s5_log_dt'], 's5_b_re': out['s5_b_re'], 's5_b_im': out['s5_b_im'], 's5_c_re': out['s5_c_re'], 's5_c_im': out['s5_c_im'], 's5_d': out['s5_d'], 's5_glu_w': out['s5_glu_w'], 's5_glu_b': out['s5_glu_b'], 'ml_conv_w': out['ml_conv_w'], 'ml_conv_b': out['ml_conv_b'], 'ml_wq': out['ml_wq'], 'ml_wk': out['ml_wk'], 'ml_wv': out['ml_wv'], 'ml_w_gate': out['ml_w_gate'], 'ml_b_gate': out['ml_b_gate'], 'ml_norm': out['ml_norm'], 'ml_skip': out['ml_skip'], 'ab_w_out': out['ab_w_out'], 'ssd_norm': out['ssd_norm'], 'ssd_w_in': out['ssd_w_in'], 'ssd_conv_w': out['ssd_conv_w'], 'ssd_conv_b': out['ssd_conv_b'], 'ssd_dt_bias': out['ssd_dt_bias'], 'ssd_a_log': out['ssd_a_log'], 'ssd_d': out['ssd_d'], 'ssd_gnorm': out['ssd_gnorm'], 'ssd_w_out': out['ssd_w_out'], 'final_norm': out['final_norm'], 'loss_target': out['loss_target'], 'm_meta_tokens': out['m_meta_tokens'], 'm_ab_norm': out['m_ab_norm'], 'm_ab_w_in': out['m_ab_w_in'], 'm_s5_lambda_re': out['m_s5_lambda_re'], 'm_s5_lambda_im': out['m_s5_lambda_im'], 'm_s5_log_dt': out['m_s5_log_dt'], 'm_s5_b_re': out['m_s5_b_re'], 'm_s5_b_im': out['m_s5_b_im'], 'm_s5_c_re': out['m_s5_c_re'], 'm_s5_c_im': out['m_s5_c_im'], 'm_s5_d': out['m_s5_d'], 'm_s5_glu_w': out['m_s5_glu_w'], 'm_s5_glu_b': out['m_s5_glu_b'], 'm_ml_conv_w': out['m_ml_conv_w'], 'm_ml_conv_b': out['m_ml_conv_b'], 'm_ml_wq': out['m_ml_wq'], 'm_ml_wk': out['m_ml_wk'], 'm_ml_wv': out['m_ml_wv'], 'm_ml_w_gate': out['m_ml_w_gate'], 'm_ml_b_gate': out['m_ml_b_gate'], 'm_ml_norm': out['m_ml_norm'], 'm_ml_skip': out['m_ml_skip'], 'm_ab_w_out': out['m_ab_w_out'], 'm_ssd_norm': out['m_ssd_norm'], 'm_ssd_w_in': out['m_ssd_w_in'], 'm_ssd_conv_w': out['m_ssd_conv_w'], 'm_ssd_conv_b': out['m_ssd_conv_b'], 'm_ssd_dt_bias': out['m_ssd_dt_bias'], 'm_ssd_a_log': out['m_ssd_a_log'], 'm_ssd_d': out['m_ssd_d'], 'm_ssd_gnorm': out['m_ssd_gnorm'], 'm_ssd_w_out': out['m_ssd_w_out'], 'm_final_norm': out['m_final_norm'], 'v_meta_tokens': out['v_meta_tokens'], 'v_ab_norm': out['v_ab_norm'], 'v_ab_w_in': out['v_ab_w_in'], 'v_s5_lambda_re': out['v_s5_lambda_re'], 'v_s5_lambda_im': out['v_s5_lambda_im'], 'v_s5_log_dt': out['v_s5_log_dt'], 'v_s5_b_re': out['v_s5_b_re'], 'v_s5_b_im': out['v_s5_b_im'], 'v_s5_c_re': out['v_s5_c_re'], 'v_s5_c_im': out['v_s5_c_im'], 'v_s5_d': out['v_s5_d'], 'v_s5_glu_w': out['v_s5_glu_w'], 'v_s5_glu_b': out['v_s5_glu_b'], 'v_ml_conv_w': out['v_ml_conv_w'], 'v_ml_conv_b': out['v_ml_conv_b'], 'v_ml_wq': out['v_ml_wq'], 'v_ml_wk': out['v_ml_wk'], 'v_ml_wv': out['v_ml_wv'], 'v_ml_w_gate': out['v_ml_w_gate'], 'v_ml_b_gate': out['v_ml_b_gate'], 'v_ml_norm': out['v_ml_norm'], 'v_ml_skip': out['v_ml_skip'], 'v_ab_w_out': out['v_ab_w_out'], 'v_ssd_norm': out['v_ssd_norm'], 'v_ssd_w_in': out['v_ssd_w_in'], 'v_ssd_conv_w': out['v_ssd_conv_w'], 'v_ssd_conv_b': out['v_ssd_conv_b'], 'v_ssd_dt_bias': out['v_ssd_dt_bias'], 'v_ssd_a_log': out['v_ssd_a_log'], 'v_ssd_d': out['v_ssd_d'], 'v_ssd_gnorm': out['v_ssd_gnorm'], 'v_ssd_w_out': out['v_ssd_w_out'], 'v_final_norm': out['v_final_norm']}


def _loss(weights, diff, rest, loss_target):
    with _jax.named_scope("forward"):
        args = {**rest, TWIN_DIFF_INPUT: diff, **{k: w.astype(_WEIGHT_DTYPES[k]) for k, w in weights.items()}}
        y = _forward(args)
    with _jax.named_scope("loss_head"):
        err = _jnp.square(y.astype(_jnp.float32) - loss_target)
        return 0.5 * _jnp.sum(_jnp.mean(err, axis=-1)) if err.ndim else 0.5 * err


def _adamw(w, g, m, v):
    m = ADAM_B1 * m + (1.0 - ADAM_B1) * g
    v = ADAM_B2 * v + (1.0 - ADAM_B2) * _jnp.square(g)
    m_hat = m / (1.0 - ADAM_B1 ** ADAM_STEP)
    v_hat = v / (1.0 - ADAM_B2 ** ADAM_STEP)
    delta = -ADAM_LR * (m_hat / (_jnp.sqrt(v_hat) + ADAM_EPS) + ADAM_WD * w)
    return delta, m, v


def reference(x, meta_tokens, ab_norm, ab_w_in, s5_lambda_re, s5_lambda_im, s5_log_dt, s5_b_re, s5_b_im, s5_c_re, s5_c_im, s5_d, s5_glu_w, s5_glu_b, ml_conv_w, ml_conv_b, ml_wq, ml_wk, ml_wv, ml_w_gate, ml_b_gate, ml_norm, ml_skip, ab_w_out, ssd_norm, ssd_w_in, ssd_conv_w, ssd_conv_b, ssd_dt_bias, ssd_a_log, ssd_d, ssd_gnorm, ssd_w_out, final_norm, loss_target, m_meta_tokens, m_ab_norm, m_ab_w_in, m_s5_lambda_re, m_s5_lambda_im, m_s5_log_dt, m_s5_b_re, m_s5_b_im, m_s5_c_re, m_s5_c_im, m_s5_d, m_s5_glu_w, m_s5_glu_b, m_ml_conv_w, m_ml_conv_b, m_ml_wq, m_ml_wk, m_ml_wv, m_ml_w_gate, m_ml_b_gate, m_ml_norm, m_ml_skip, m_ab_w_out, m_ssd_norm, m_ssd_w_in, m_ssd_conv_w, m_ssd_conv_b, m_ssd_dt_bias, m_ssd_a_log, m_ssd_d, m_ssd_gnorm, m_ssd_w_out, m_final_norm, v_meta_tokens, v_ab_norm, v_ab_w_in, v_s5_lambda_re, v_s5_lambda_im, v_s5_log_dt, v_s5_b_re, v_s5_b_im, v_s5_c_re, v_s5_c_im, v_s5_d, v_s5_glu_w, v_s5_glu_b, v_ml_conv_w, v_ml_conv_b, v_ml_wq, v_ml_wk, v_ml_wv, v_ml_w_gate, v_ml_b_gate, v_ml_norm, v_ml_skip, v_ab_w_out, v_ssd_norm, v_ssd_w_in, v_ssd_conv_w, v_ssd_conv_b, v_ssd_dt_bias, v_ssd_a_log, v_ssd_d, v_ssd_gnorm, v_ssd_w_out, v_final_norm):
    given = dict(x=x, meta_tokens=meta_tokens, ab_norm=ab_norm, ab_w_in=ab_w_in, s5_lambda_re=s5_lambda_re, s5_lambda_im=s5_lambda_im, s5_log_dt=s5_log_dt, s5_b_re=s5_b_re, s5_b_im=s5_b_im, s5_c_re=s5_c_re, s5_c_im=s5_c_im, s5_d=s5_d, s5_glu_w=s5_glu_w, s5_glu_b=s5_glu_b, ml_conv_w=ml_conv_w, ml_conv_b=ml_conv_b, ml_wq=ml_wq, ml_wk=ml_wk, ml_wv=ml_wv, ml_w_gate=ml_w_gate, ml_b_gate=ml_b_gate, ml_norm=ml_norm, ml_skip=ml_skip, ab_w_out=ab_w_out, ssd_norm=ssd_norm, ssd_w_in=ssd_w_in, ssd_conv_w=ssd_conv_w, ssd_conv_b=ssd_conv_b, ssd_dt_bias=ssd_dt_bias, ssd_a_log=ssd_a_log, ssd_d=ssd_d, ssd_gnorm=ssd_gnorm, ssd_w_out=ssd_w_out, final_norm=final_norm, loss_target=loss_target, m_meta_tokens=m_meta_tokens, m_ab_norm=m_ab_norm, m_ab_w_in=m_ab_w_in, m_s5_lambda_re=m_s5_lambda_re, m_s5_lambda_im=m_s5_lambda_im, m_s5_log_dt=m_s5_log_dt, m_s5_b_re=m_s5_b_re, m_s5_b_im=m_s5_b_im, m_s5_c_re=m_s5_c_re, m_s5_c_im=m_s5_c_im, m_s5_d=m_s5_d, m_s5_glu_w=m_s5_glu_w, m_s5_glu_b=m_s5_glu_b, m_ml_conv_w=m_ml_conv_w, m_ml_conv_b=m_ml_conv_b, m_ml_wq=m_ml_wq, m_ml_wk=m_ml_wk, m_ml_wv=m_ml_wv, m_ml_w_gate=m_ml_w_gate, m_ml_b_gate=m_ml_b_gate, m_ml_norm=m_ml_norm, m_ml_skip=m_ml_skip, m_ab_w_out=m_ab_w_out, m_ssd_norm=m_ssd_norm, m_ssd_w_in=m_ssd_w_in, m_ssd_conv_w=m_ssd_conv_w, m_ssd_conv_b=m_ssd_conv_b, m_ssd_dt_bias=m_ssd_dt_bias, m_ssd_a_log=m_ssd_a_log, m_ssd_d=m_ssd_d, m_ssd_gnorm=m_ssd_gnorm, m_ssd_w_out=m_ssd_w_out, m_final_norm=m_final_norm, v_meta_tokens=v_meta_tokens, v_ab_norm=v_ab_norm, v_ab_w_in=v_ab_w_in, v_s5_lambda_re=v_s5_lambda_re, v_s5_lambda_im=v_s5_lambda_im, v_s5_log_dt=v_s5_log_dt, v_s5_b_re=v_s5_b_re, v_s5_b_im=v_s5_b_im, v_s5_c_re=v_s5_c_re, v_s5_c_im=v_s5_c_im, v_s5_d=v_s5_d, v_s5_glu_w=v_s5_glu_w, v_s5_glu_b=v_s5_glu_b, v_ml_conv_w=v_ml_conv_w, v_ml_conv_b=v_ml_conv_b, v_ml_wq=v_ml_wq, v_ml_wk=v_ml_wk, v_ml_wv=v_ml_wv, v_ml_w_gate=v_ml_w_gate, v_ml_b_gate=v_ml_b_gate, v_ml_norm=v_ml_norm, v_ml_skip=v_ml_skip, v_ab_w_out=v_ab_w_out, v_ssd_norm=v_ssd_norm, v_ssd_w_in=v_ssd_w_in, v_ssd_conv_w=v_ssd_conv_w, v_ssd_conv_b=v_ssd_conv_b, v_ssd_dt_bias=v_ssd_dt_bias, v_ssd_a_log=v_ssd_a_log, v_ssd_d=v_ssd_d, v_ssd_gnorm=v_ssd_gnorm, v_ssd_w_out=v_ssd_w_out, v_final_norm=v_final_norm)
    weights = {n: given[n] for n in TWIN_WEIGHTS}
    shared = {n: given[n] for n in SHARED_INPUTS}
    per_example = {n: given[n] for n in ['x']}
    grad_fn = _jax.value_and_grad(_loss, argnums=(0, 1))

    def one_microbatch(ex, loss_target):
        ex = dict(ex)
        diff = ex.pop(TWIN_DIFF_INPUT)
        return grad_fn(weights, diff, {**shared, **ex}, loss_target)

    if N_MICROBATCH == 1:
        loss, (grad_w, grad_x) = one_microbatch(per_example, given["loss_target"])
    else:
        def body(carry, xs):
            loss_sum, grad_sum = carry
            l_k, (gw_k, gx_k) = one_microbatch(xs[0], xs[1])
            with _jax.named_scope("update"):
                return (loss_sum + l_k, _jax.tree.map(_jnp.add, grad_sum, gw_k)), gx_k

        init = (_jnp.zeros((), _jnp.float32), _jax.tree.map(_jnp.zeros_like, weights))
        (loss, grad_w), grad_x = _jax.lax.scan(body, init, (per_example, given["loss_target"]))
    with _jax.named_scope("update"):
        delta_w, new_m, new_v = {}, {}, {}
        for n in TWIN_WEIGHTS:
            delta_w[n], new_m[n], new_v[n] = _adamw(weights[n], grad_w[n], given["m_" + n], given["v_" + n])
    return (loss, grad_x, *[grad_w[n] for n in TWIN_WEIGHTS], *[delta_w[n] for n in TWIN_WEIGHTS],
            *[new_m[n] for n in TWIN_WEIGHTS], *[new_v[n] for n in TWIN_WEIGHTS])
```

```python
import functools
import math

import jax
import jax.numpy as jnp
from jax import lax
from jax.experimental import pallas as pl
from jax.experimental.pallas import tpu as pltpu

F32 = jnp.float32
BF16 = jnp.bfloat16
HI = lax.Precision.HIGHEST

D_MODEL = 2048
SEQ = 2048
N_META = 16
CHUNK = 128
NORM_EPS = 1e-6
HEAD_NORM_EPS = 1e-5
S5_GROUP_SIZE = 16
S5_STATE = 64
MLSTM_HEADS = 8
QKV_BLOCK = 4
SSD_HEAD_DIM = 64
SSD_STATE = 128
SSD_HPG = 8
ADAM_LR = 0.001
ADAM_B1 = 0.9
ADAM_B2 = 0.999
ADAM_EPS = 1e-08
ADAM_WD = 0.01
ADAM_STEP = 10

LANES = 128
SUBLANES = 8
VMEM_LIMIT = 56 * 1024 * 1024


def _sigmoid(x):
    return 1.0 / (1.0 + jnp.exp(-x))


def _silu(x):
    return x * _sigmoid(x)


def _softplus(x):
    return jnp.maximum(x, 0.0) + jnp.log(1.0 + jnp.exp(-jnp.abs(x)))


def _log_sigmoid(x):
    return jnp.minimum(x, 0.0) - jnp.log(1.0 + jnp.exp(-jnp.abs(x)))


def _gelu(x):
    return 0.5 * x * (1.0 + jnp.tanh(math.sqrt(2.0 / math.pi) * (x + 0.044715 * (x * x * x))))


def _dot(a, b, dims, precision=None):
    return lax.dot_general(a, b, (dims, ((), ())), preferred_element_type=F32, precision=precision)


def _dot_nn(a, b):
    return _dot(a.astype(BF16), b.astype(BF16), ((1,), (0,)))


def _dot_nt(a, b):
    return _dot(a.astype(BF16), b.astype(BF16), ((1,), (1,)))


def _dot_tn(a, b):
    return _dot(a.astype(BF16), b.astype(BF16), ((0,), (0,)))


def _lane_pick(a, idx):
    sel = (lax.broadcasted_iota(jnp.int32, (1, a.shape[1]), 1) == idx).astype(a.dtype)
    return jnp.sum(a * sel, axis=1, keepdims=True)


def _row_pick(a, idx):
    sel = (lax.broadcasted_iota(jnp.int32, (a.shape[0], 1), 0) == idx).astype(a.dtype)
    return jnp.sum(a * sel, axis=0, keepdims=True)


def _tri(n, upper=False):
    r = lax.broadcasted_iota(jnp.int32, (n, n), 0)
    c = lax.broadcasted_iota(jnp.int32, (n, n), 1)
    return ((r <= c) if upper else (r >= c)).astype(F32)


def _tile(n, target, align):
    if n <= target:
        return n
    t = (target // align) * align
    while t >= align:
        if n % t == 0:
            return t
        t -= align
    return n


def _params(sem=None):
    return pltpu.CompilerParams(dimension_semantics=sem, vmem_limit_bytes=VMEM_LIMIT)


def mm(a, b, mode, name, resid=None, out_dtype=F32):
    if mode == "nn":
        (m, k), (k2, n) = a.shape, b.shape
    elif mode == "nt":
        (m, k), (n, k2) = a.shape, b.shape
    else:
        (k, m), (k2, n) = a.shape, b.shape
    assert k == k2, (a.shape, b.shape, mode)
    if mode == "tn":
        tm, tn, tk = _tile(m, 1024, LANES), _tile(n, 1024, LANES), _tile(k, 640, 16)
    else:
        tm, tn, tk = _tile(m, 1088, 16), _tile(n, 512, LANES), _tile(k, 512, LANES)
    nk = k // tk
    dims = {"nn": ((1,), (0,)), "nt": ((1,), (1,)), "tn": ((0,), (0,))}[mode]
    has_resid = resid is not None

    def body(*refs):
        if has_resid:
            a_ref, b_ref, r_ref, o_ref, acc_ref = refs
        else:
            a_ref, b_ref, o_ref, acc_ref = refs
        kk = pl.program_id(2)

        @pl.when(kk == 0)
        def _():
            acc_ref[...] = jnp.zeros_like(acc_ref)

        acc_ref[...] += _dot(a_ref[...].astype(BF16), b_ref[...].astype(BF16), dims)

        @pl.when(kk == nk - 1)
        def _():
            res = acc_ref[...]
            if has_resid:
                res = res + r_ref[...].astype(F32)
            o_ref[...] = res.astype(o_ref.dtype)

    if mode == "tn":
        a_spec = pl.BlockSpec((tk, tm), lambda i, j, kk: (kk, i))
    else:
        a_spec = pl.BlockSpec((tm, tk), lambda i, j, kk: (i, kk))
    if mode == "nt":
        b_spec = pl.BlockSpec((tn, tk), lambda i, j, kk: (j, kk))
    else:
        b_spec = pl.BlockSpec((tk, tn), lambda i, j, kk: (kk, j))
    o_spec = pl.BlockSpec((tm, tn), lambda i, j, kk: (i, j))
    in_specs = [a_spec, b_spec] + ([o_spec] if has_resid else [])
    args = (a, b) + ((resid,) if has_resid else ())
    return pl.pallas_call(
        body, name=name, grid=(m // tm, n // tn, nk), in_specs=in_specs, out_specs=o_spec,
        out_shape=jax.ShapeDtypeStruct((m, n), out_dtype), scratch_shapes=[pltpu.VMEM((tm, tn), F32)],
        compiler_params=_params(("parallel", "parallel", "arbitrary")))(*args)


def rowwise(name, f, rows, params, outs, accs=(), tr=128):
    n_rows = rows[0].shape[0]
    assert n_rows % tr == 0
    n_r, n_p, n_o, n_a = len(rows), len(params), len(outs), len(accs)

    def body(*refs):
        i = pl.program_id(0)
        r_vals = [r[...] for r in refs[:n_r]]
        p_vals = [r[...] for r in refs[n_r:n_r + n_p]]
        o_refs = refs[n_r + n_p:n_r + n_p + n_o]
        a_refs = refs[n_r + n_p + n_o:]
        res = f(i, *r_vals, *p_vals)
        if not isinstance(res, (tuple, list)):
            res = (res,)
        assert len(res) == n_o + n_a, (name, len(res))
        for o_ref, val in zip(o_refs, res[:n_o]):
            o_ref[...] = val.astype(o_ref.dtype)
        if n_a:
            @pl.when(i == 0)
            def _():
                for a_ref in a_refs:
                    a_ref[...] = jnp.zeros_like(a_ref)

            for a_ref, val in zip(a_refs, res[n_o:]):
                a_ref[...] += val.astype(F32)

    in_specs = [pl.BlockSpec((tr, r.shape[1]), lambda i: (i, 0)) for r in rows]
    in_specs += [pl.BlockSpec(p.shape, lambda i: (0, 0)) for p in params]
    out_specs = [pl.BlockSpec((tr, w), lambda i: (i, 0)) for w, _ in outs]
    out_specs += [pl.BlockSpec(s, lambda i: (0, 0)) for s in accs]
    out_shape = [jax.ShapeDtypeStruct((n_rows, w), dt) for w, dt in outs]
    out_shape += [jax.ShapeDtypeStruct(s, F32) for s in accs]
    res = pl.pallas_call(
        body, name=name, grid=(n_rows // tr,), in_specs=in_specs, out_specs=out_specs, out_shape=out_shape,
        compiler_params=_params(("arbitrary",)))(*rows, *params)
    return res


def _rms(x, g, eps=NORM_EPS):
    return x * lax.rsqrt(jnp.mean(x * x, axis=-1, keepdims=True) + eps) * g


def norm_fwd(x, g, name):
    return rowwise(name, lambda i, xb, gb: _rms(xb, gb), [x], [g], [(x.shape[1], BF16)], tr=_tile(x.shape[0], 256, 16))[0]


def norm_bwd(x, g, dn, resid, name):
    def f(i, xb, dnb, rb, gb):
        _, vjp = jax.vjp(_rms, xb, gb)
        dx, dg = vjp(dnb)
        return dx + rb, dg

    return rowwise(name, f, [x, dn, resid], [g], [(x.shape[1], F32)], [g.shape], tr=_tile(x.shape[0], 256, 16))


def conv_fwd(x, w, b, nb, name):
    rows, width = x.shape
    nc = rows // nb // CHUNK
    tw = _tile(width, 1024, LANES)
    ksz = w.shape[0]

    def body(x_ref, w_ref, b_ref, o_ref, ext_ref):
        c = pl.program_id(2)

        @pl.when(c == 0)
        def _():
            ext_ref[0:SUBLANES, :] = jnp.zeros((SUBLANES, tw), F32)

        xv = x_ref[...]
        ext_ref[SUBLANES:SUBLANES + CHUNK, :] = xv
        acc = jnp.broadcast_to(b_ref[...], (CHUNK, tw))
        for j in range(ksz):
            off = SUBLANES - (ksz - 1) + j
            acc = acc + w_ref[j:j + 1, :] * ext_ref[off:off + CHUNK, :]
        o_ref[...] = acc
        ext_ref[0:SUBLANES, :] = xv[CHUNK - SUBLANES:CHUNK, :]

    return pl.pallas_call(
        body, name=name, grid=(width // tw, nb, nc),
        in_specs=[pl.BlockSpec((CHUNK, tw), lambda j, bb, c: (bb * nc + c, j)),
                  pl.BlockSpec((ksz, tw), lambda j, bb, c: (0, j)),
                  pl.BlockSpec((1, tw), lambda j, bb, c: (0, j))],
        out_specs=pl.BlockSpec((CHUNK, tw), lambda j, bb, c: (bb * nc + c, j)),
        out_shape=jax.ShapeDtypeStruct((rows, width), F32),
        scratch_shapes=[pltpu.VMEM((CHUNK + 2 * SUBLANES, tw), F32)],
        compiler_params=_params(("arbitrary", "arbitrary", "arbitrary")))(x, w, b)


def conv_bwd(dc, x, w, nb, name, resid=None):
    rows, width = x.shape
    nc = rows // nb // CHUNK
    tw = _tile(width, 1024, LANES)
    ksz = w.shape[0]
    per = CHUNK // SUBLANES
    has_resid = resid is not None

    def body(*refs):
        if has_resid:
            dc_ref, x_ref, halo_ref, w_ref, r_ref, dx_ref, dw_ref, db_ref, extd_ref, extx_ref = refs
        else:
            dc_ref, x_ref, halo_ref, w_ref, dx_ref, dw_ref, db_ref, extd_ref, extx_ref = refs
        bb = pl.program_id(1)
        step = pl.program_id(2)
        c = nc - 1 - step

        @pl.when(jnp.logical_and(bb == 0, step == 0))
        def _():
            dw_ref[...] = jnp.zeros_like(dw_ref)
            db_ref[...] = jnp.zeros_like(db_ref)

        @pl.when(step == 0)
        def _():
            extd_ref[CHUNK:CHUNK + SUBLANES, :] = jnp.zeros((SUBLANES, tw), F32)

        dcv = dc_ref[...]
        extd_ref[0:CHUNK, :] = dcv
        extx_ref[0:SUBLANES, :] = jnp.where(c == 0, 0.0, halo_ref[...])
        extx_ref[SUBLANES:SUBLANES + CHUNK, :] = x_ref[...]
        dx = jnp.zeros((CHUNK, tw), F32)
        for j in range(ksz):
            up = ksz - 1 - j
            dx = dx + w_ref[j:j + 1, :] * extd_ref[up:up + CHUNK, :]
            off = SUBLANES - (ksz - 1) + j
            dw_ref[j:j + 1, :] += jnp.sum(dcv * extx_ref[off:off + CHUNK, :], axis=0, keepdims=True)
        if has_resid:
            dx = dx + r_ref[...]
        dx_ref[...] = dx
        db_ref[...] += jnp.sum(dcv, axis=0, keepdims=True)
        extd_ref[CHUNK:CHUNK + SUBLANES, :] = dcv[0:SUBLANES, :]

    def blk(j, bb, step):
        return (bb * nc + nc - 1 - step, j)

    def halo(j, bb, step):
        return (jnp.maximum((bb * nc + nc - 1 - step) * per - 1, 0), j)

    in_specs = [pl.BlockSpec((CHUNK, tw), blk), pl.BlockSpec((CHUNK, tw), blk), pl.BlockSpec((SUBLANES, tw), halo),
                pl.BlockSpec((ksz, tw), lambda j, bb, step: (0, j))]
    args = [dc, x, x, w]
    if has_resid:
        in_specs.append(pl.BlockSpec((CHUNK, tw), blk))
        args.append(resid)
    return pl.pallas_call(
        body, name=name, grid=(width // tw, nb, nc), in_specs=in_specs,
        out_specs=[pl.BlockSpec((CHUNK, tw), blk), pl.BlockSpec((SUBLANES, tw), lambda j, bb, step: (0, j)),
                   pl.BlockSpec((1, tw), lambda j, bb, step: (0, j))],
        out_shape=[jax.ShapeDtypeStruct((rows, width), F32), jax.ShapeDtypeStruct((SUBLANES, width), F32),
                   jax.ShapeDtypeStruct((1, width), F32)],
        scratch_shapes=[pltpu.VMEM((CHUNK + 2 * SUBLANES, tw), F32), pltpu.VMEM((CHUNK + 2 * SUBLANES, tw), F32)],
        compiler_params=_params(("arbitrary", "arbitrary", "arbitrary")))(*args)


S5_Q = 4


def _s5_fill_bu(u, bre_ref, bim_ref, xr_ref, xi_ref, ns):
    for s in range(ns):
        ub = u[:, s * LANES:(s + 1) * LANES].astype(BF16)
        bur = _dot(ub, bre_ref[s], ((1,), (0,)))
        bui = _dot(ub, bim_ref[s], ((1,), (0,)))
        for q in range(S5_Q):
            xr_ref[q, pl.ds(s, CHUNK, stride=ns), :] = bur[:, q * LANES:(q + 1) * LANES]
            xi_ref[q, pl.ds(s, CHUNK, stride=ns), :] = bui[:, q * LANES:(q + 1) * LANES]


def _s5_scan(xr_ref, xi_ref, ar_ref, ai_ref, st_ref, ns):
    ar = [ar_ref[q] for q in range(S5_Q)]
    ai = [ai_ref[q] for q in range(S5_Q)]

    def step(t, carry):
        rows = pl.ds(pl.multiple_of(t * ns, ns), ns)
        out = []
        for q in range(S5_Q):
            pr, pi_ = carry[2 * q], carry[2 * q + 1]
            nr = ar[q] * pr - ai[q] * pi_ + xr_ref[q, rows, :]
            ni = ar[q] * pi_ + ai[q] * pr + xi_ref[q, rows, :]
            xr_ref[q, rows, :] = nr
            xi_ref[q, rows, :] = ni
            out += [nr, ni]
        return tuple(out)

    init = []
    for q in range(S5_Q):
        init += [st_ref[0, q], st_ref[1, q]]
    fin = lax.fori_loop(0, CHUNK, step, tuple(init), unroll=2)
    for q in range(S5_Q):
        st_ref[0, q] = fin[2 * q]
        st_ref[1, q] = fin[2 * q + 1]


def s5_fwd(pa, bre, bim, cre, cim, ar, ai, dvec, nb, name):
    rows = pa.shape[0]
    width = pa.shape[1] // 2
    ns = width // LANES
    nc = rows // nb // CHUNK

    def body(u_ref, bre_ref, bim_ref, cre_ref, cim_ref, ar_ref, ai_ref, d_ref, y_ref, g_ref, so_ref, xr_ref, xi_ref, st_ref):
        c = pl.program_id(1)

        @pl.when(c == 0)
        def _():
            st_ref[...] = jnp.zeros_like(st_ref)

        so_ref[...] = st_ref[...]
        u = u_ref[...]
        _s5_fill_bu(u, bre_ref, bim_ref, xr_ref, xi_ref, ns)
        _s5_scan(xr_ref, xi_ref, ar_ref, ai_ref, st_ref, ns)
        for s in range(ns):
            acc = jnp.zeros((CHUNK, LANES), F32)
            for q in range(S5_Q):
                xr = xr_ref[q, pl.ds(s, CHUNK, stride=ns), :].astype(BF16)
                xi = xi_ref[q, pl.ds(s, CHUNK, stride=ns), :].astype(BF16)
                acc = acc + _dot(xr, cre_ref[s, q * LANES:(q + 1) * LANES, :], ((1,), (0,)))
                acc = acc - _dot(xi, cim_ref[s, q * LANES:(q + 1) * LANES, :], ((1,), (0,)))
            cols = slice(s * LANES, (s + 1) * LANES)
            y = acc + d_ref[:, cols] * u[:, cols]
            y_ref[:, cols] = y
            g_ref[:, cols] = _gelu(y).astype(BF16)

    whole3 = lambda a: pl.BlockSpec(a.shape, lambda b_, c: (0, 0, 0))
    return pl.pallas_call(
        body, name=name, grid=(nb, nc),
        in_specs=[pl.BlockSpec((CHUNK, width), lambda b_, c: (b_ * nc + c, 0)), whole3(bre), whole3(bim), whole3(cre),
                  whole3(cim), whole3(ar), whole3(ai), pl.BlockSpec((1, width), lambda b_, c: (0, 0))],
        out_specs=[pl.BlockSpec((CHUNK, width), lambda b_, c: (b_ * nc + c, 0)),
                   pl.BlockSpec((CHUNK, width), lambda b_, c: (b_ * nc + c, 0)),
                   pl.BlockSpec((None, 2, S5_Q, ns, LANES), lambda b_, c: (b_ * nc + c, 0, 0, 0, 0))],
        out_shape=[jax.ShapeDtypeStruct((rows, width), F32), jax.ShapeDtypeStruct((rows, width), BF16),
                   jax.ShapeDtypeStruct((nb * nc, 2, S5_Q, ns, LANES), F32)],
        scratch_shapes=[pltpu.VMEM((S5_Q, CHUNK * ns, LANES), F32), pltpu.VMEM((S5_Q, CHUNK * ns, LANES), F32),
                        pltpu.VMEM((2, S5_Q, ns, LANES), F32)],
        compiler_params=_params(("arbitrary", "arbitrary")))(pa, bre, bim, cre, cim, ar, ai, dvec)


def s5_bwd(pa, dys, states, bre, bim, cre, cim, ar, ai, dvec, nb, name):
    rows = pa.shape[0]
    width = pa.shape[1] // 2
    ns = width // LANES
    nc = rows // nb // CHUNK

    def body(u_ref, dy_ref, sin_ref, bre_ref, bim_ref, cre_ref, cim_ref, ar_ref, ai_ref, d_ref,
             du_ref, dbre_ref, dbim_ref, dcre_ref, dcim_ref, dar_ref, dai_ref, dd_ref,
             xr_ref, xi_ref, lr_ref, li_ref, st_ref, lam_ref):
        bb = pl.program_id(0)
        step_i = pl.program_id(1)

        @pl.when(jnp.logical_and(bb == 0, step_i == 0))
        def _():
            for r in (dbre_ref, dbim_ref, dcre_ref, dcim_ref, dar_ref, dai_ref, dd_ref):
                r[...] = jnp.zeros_like(r)

        @pl.when(step_i == 0)
        def _():
            lam_ref[...] = jnp.zeros_like(lam_ref)

        u = u_ref[...]
        dy = dy_ref[...]
        st_ref[...] = sin_ref[...]
        _s5_fill_bu(u, bre_ref, bim_ref, xr_ref, xi_ref, ns)
        _s5_scan(xr_ref, xi_ref, ar_ref, ai_ref, st_ref, ns)
        dd_ref[...] += jnp.sum(dy * u, axis=0, keepdims=True)
        for s in range(ns):
            dyb = dy[:, s * LANES:(s + 1) * LANES].astype(BF16)
            gr = _dot(dyb, cre_ref[s], ((1,), (1,)))
            gi = -_dot(dyb, cim_ref[s], ((1,), (1,)))
            for q in range(S5_Q):
                lr_ref[q, pl.ds(s, CHUNK, stride=ns), :] = gr[:, q * LANES:(q + 1) * LANES]
                li_ref[q, pl.ds(s, CHUNK, stride=ns), :] = gi[:, q * LANES:(q + 1) * LANES]
                xr = xr_ref[q, pl.ds(s, CHUNK, stride=ns), :].astype(BF16)
                xi = xi_ref[q, pl.ds(s, CHUNK, stride=ns), :].astype(BF16)
                dcre_ref[s, q * LANES:(q + 1) * LANES, :] += _dot(xr, dyb, ((0,), (0,)))
                dcim_ref[s, q * LANES:(q + 1) * LANES, :] -= _dot(xi, dyb, ((0,), (0,)))
        ar = [ar_ref[q] for q in range(S5_Q)]
        ai = [ai_ref[q] for q in range(S5_Q)]

        def one(t_rows, p_r, p_i, carry):
            out = []
            for q in range(S5_Q):
                l_r, l_i, da_r, da_i = carry[4 * q:4 * q + 4]
                n_r = lr_ref[q, t_rows, :] + ar[q] * l_r + ai[q] * l_i
                n_i = li_ref[q, t_rows, :] + ar[q] * l_i - ai[q] * l_r
                lr_ref[q, t_rows, :] = n_r
                li_ref[q, t_rows, :] = n_i
                xpr, xpi = p_r(q), p_i(q)
                out += [n_r, n_i, da_r + n_r * xpr + n_i * xpi, da_i + n_i * xpr - n_r * xpi]
            return tuple(out)

        def step(k, carry):
            t = CHUNK - 1 - k
            t_rows = pl.ds(pl.multiple_of(t * ns, ns), ns)
            p_rows = pl.ds(pl.multiple_of((t - 1) * ns, ns), ns)
            return one(t_rows, lambda q: xr_ref[q, p_rows, :], lambda q: xi_ref[q, p_rows, :], carry)

        init = []
        zero = jnp.zeros((ns, LANES), F32)
        for q in range(S5_Q):
            init += [lam_ref[0, q], lam_ref[1, q], zero, zero]
        carry = lax.fori_loop(0, CHUNK - 1, step, tuple(init), unroll=2)
        carry = one(pl.ds(0, ns), lambda q: sin_ref[0, q], lambda q: sin_ref[1, q], carry)
        for q in range(S5_Q):
            lam_ref[0, q] = carry[4 * q]
            lam_ref[1, q] = carry[4 * q + 1]
            dar_ref[q] += carry[4 * q + 2]
            dai_ref[q] += carry[4 * q + 3]
        for s in range(ns):
            cols = slice(s * LANES, (s + 1) * LANES)
            ub = u[:, cols].astype(BF16)
            acc = d_ref[:, cols] * dy[:, cols]
            for q in range(S5_Q):
                qs = slice(q * LANES, (q + 1) * LANES)
                lr = lr_ref[q, pl.ds(s, CHUNK, stride=ns), :].astype(BF16)
                li = li_ref[q, pl.ds(s, CHUNK, stride=ns), :].astype(BF16)
                dbre_ref[s, :, qs] += _dot(ub, lr, ((0,), (0,)))
                dbim_ref[s, :, qs] += _dot(ub, li, ((0,), (0,)))
                acc = acc + _dot(lr, bre_ref[s, :, qs], ((1,), (1,))) + _dot(li, bim_ref[s, :, qs], ((1,), (1,)))
            du_ref[:, cols] = acc

    whole3 = lambda a: pl.BlockSpec(a.shape, lambda b_, c: (0, 0, 0))
    rowblk = pl.BlockSpec((CHUNK, width), lambda b_, c: (b_ * nc + nc - 1 - c, 0))
    scr = pltpu.VMEM((S5_Q, CHUNK * ns, LANES), F32)
    return pl.pallas_call(
        body, name=name, grid=(nb, nc),
        in_specs=[rowblk, rowblk,
                  pl.BlockSpec((None, 2, S5_Q, ns, LANES), lambda b_, c: (b_ * nc + nc - 1 - c, 0, 0, 0, 0)),
                  whole3(bre), whole3(bim), whole3(cre), whole3(cim), whole3(ar), whole3(ai),
                  pl.BlockSpec((1, width), lambda b_, c: (0, 0))],
        out_specs=[rowblk, whole3(bre), whole3(bim), whole3(cre), whole3(cim), whole3(ar), whole3(ai),
                   pl.BlockSpec((1, width), lambda b_, c: (0, 0))],
        out_shape=[jax.ShapeDtypeStruct((rows, width), F32), jax.ShapeDtypeStruct(bre.shape, F32),
                   jax.ShapeDtypeStruct(bim.shape, F32), jax.ShapeDtypeStruct(cre.shape, F32),
                   jax.ShapeDtypeStruct(cim.shape, F32), jax.ShapeDtypeStruct(ar.shape, F32),
                   jax.ShapeDtypeStruct(ai.shape, F32), jax.ShapeDtypeStruct((1, width), F32)],
        scratch_shapes=[scr, scr, scr, scr, pltpu.VMEM((2, S5_Q, ns, LANES), F32), pltpu.VMEM((2, S5_Q, ns, LANES), F32)],
        compiler_params=_params(("arbitrary", "arbitrary")))(pa, dys, states, bre, bim, cre, cim, ar, ai, dvec)


def _s5_discretize(lam_re, lam_im, log_dt, b_re, b_im):
    dt = jnp.exp(log_dt)[:, None]
    mag = jnp.exp(lam_re * dt)
    ar, ai = mag * jnp.cos(lam_im * dt), mag * jnp.sin(lam_im * dt)
    den = lam_re * lam_re + lam_im * lam_im
    qr = ((ar - 1.0) * lam_re + ai * lam_im) / den
    qi = (ai * lam_re - (ar - 1.0) * lam_im) / den
    bbr = qr[..., None] * b_re - qi[..., None] * b_im
    bbi = qr[..., None] * b_im + qi[..., None] * b_re
    return ar, ai, bbr, bbi


def _s5_expand(ar, ai, bbr, bbi, c_re, c_im):
    g, p, h = bbr.shape
    gps = LANES // h
    ns = g // gps
    eye = jnp.eye(gps, dtype=F32)

    def bexp(b):
        return jnp.einsum("sgph,gk->sghkp", b.reshape(ns, gps, p, h), eye).reshape(ns, gps * h, gps * p)

    def cexp(c):
        return jnp.einsum("sghp,gk->sgpkh", c.reshape(ns, gps, h, p), eye).reshape(ns, gps * p, gps * h)

    def aexp(a):
        return a.reshape(ns, S5_Q, LANES).transpose(1, 0, 2)

    return (bexp(bbr).astype(BF16), bexp(bbi).astype(BF16), cexp(c_re).astype(BF16), cexp(c_im).astype(BF16),
            aexp(ar), aexp(ai))


def _s5_contract(dbre, dbim, dcre, dcim, dar, dai, g, p, h):
    gps = LANES // h
    ns = g // gps
    eye = jnp.eye(gps, dtype=F32)
    bcon = lambda d: jnp.einsum("sghkp,gk->sgph", d.reshape(ns, gps, h, gps, p), eye).reshape(g, p, h)
    ccon = lambda d: jnp.einsum("sgpkh,gk->sghp", d.reshape(ns, gps, p, gps, h), eye).reshape(g, h, p)
    acon = lambda d: d.transpose(1, 0, 2).reshape(g, p)
    return bcon(dbre), bcon(dbim), ccon(dcre), ccon(dcim), acon(dar), acon(dai)


def _ml_proj_tile(cpre, xb, wq, wk, wv, gq, gk, gv):
    xc = _silu(cpre)
    q = _dot_nn(xc, wq)
    k = _dot_nn(xc, wk)
    v = _dot_nn(xb, wv)
    return q, k, v, _dot_nn(q, gq) + _dot_nn(k, gk) + _dot_nn(v, gv)


def ml_proj_fwd(cpre, xb, wq, wk, wv, gq, gk, gv, name):
    rows, width = cpre.shape
    nblk = width // LANES
    tr = _tile(rows, 1088, 16)

    def body(c_ref, x_ref, wq_ref, wk_ref, wv_ref, gq_ref, gk_ref, gv_ref, q_ref, k_ref, v_ref, g_ref):
        j = pl.program_id(1)
        q, k, v, g = _ml_proj_tile(c_ref[...], x_ref[...], wq_ref[...], wk_ref[...], wv_ref[...],
                                   gq_ref[...], gk_ref[...], gv_ref[...])
        q_ref[...] = q
        k_ref[...] = k
        v_ref[...] = v

        @pl.when(j == 0)
        def _():
            g_ref[...] = jnp.zeros_like(g_ref)

        g_ref[...] += g

    rb = pl.BlockSpec((tr, LANES), lambda i, j: (i, j))
    wb = pl.BlockSpec((None, LANES, LANES), lambda i, j: (j, 0, 0))
    return pl.pallas_call(
        body, name=name, grid=(rows // tr, nblk), in_specs=[rb, rb, wb, wb, wb, wb, wb, wb],
        out_specs=[rb, rb, rb, pl.BlockSpec((tr, LANES), lambda i, j: (i, 0))],
        out_shape=[jax.ShapeDtypeStruct((rows, width), F32)] * 3 + [jax.ShapeDtypeStruct((rows, LANES), F32)],
        compiler_params=_params(("arbitrary", "arbitrary")))(cpre, xb, wq, wk, wv, gq, gk, gv)


def ml_proj_bwd(cpre, xb, wq, wk, wv, gq, gk, gv, dq, dk, dv, dg, dcp_extra, name):
    rows, width = cpre.shape
    nblk = width // LANES
    tr = _tile(rows, 1088, 16)

    def body(c_ref, x_ref, wq_ref, wk_ref, wv_ref, gq_ref, gk_ref, gv_ref, dq_ref, dk_ref, dv_ref, dg_ref, e_ref,
             dc_ref, dx_ref, *dw_refs):
        i = pl.program_id(1)
        _, vjp = jax.vjp(_ml_proj_tile, c_ref[...], x_ref[...], wq_ref[...], wk_ref[...], wv_ref[...],
                         gq_ref[...], gk_ref[...], gv_ref[...])
        grads = vjp((dq_ref[...], dk_ref[...], dv_ref[...], dg_ref[...]))
        dc_ref[...] = grads[0] + e_ref[...]
        dx_ref[...] = grads[1]

        @pl.when(i == 0)
        def _():
            for r in dw_refs:
                r[...] = jnp.zeros_like(r)

        for r, gval in zip(dw_refs, grads[2:]):
            r[...] += gval

    rb = pl.BlockSpec((tr, LANES), lambda j, i: (i, j))
    wb = pl.BlockSpec((None, LANES, LANES), lambda j, i: (j, 0, 0))
    gb = pl.BlockSpec((tr, LANES), lambda j, i: (i, 0))
    wshape = jax.ShapeDtypeStruct((nblk, LANES, LANES), F32)
    return pl.pallas_call(
        body, name=name, grid=(nblk, rows // tr), in_specs=[rb, rb, wb, wb, wb, wb, wb, wb, rb, rb, rb, gb, rb],
        out_specs=[rb, rb] + [wb] * 6,
        out_shape=[jax.ShapeDtypeStruct((rows, width), F32)] * 2 + [wshape] * 6,
        compiler_params=_params(("arbitrary", "arbitrary")))(cpre, xb, wq, wk, wv, gq, gk, gv, dq, dk, dv, dg, dcp_extra)


def _ml_gates_tile(gl, bg, nh):
    x = gl + bg
    bcum = _dot(_tri(CHUNK), _log_sigmoid(x), ((1,), (0,)), precision=HI)
    lane = lax.broadcasted_iota(jnp.int32, x.shape, 1)
    return jnp.where(lane < nh, x, jnp.where(lane < 2 * nh, bcum, 0.0))


def _ml_core_tile(q, k, v, colg, rowg, cpre, zb, nw, sk, cst, nst, m_prev):
    c, dh = q.shape
    igc, bc = _lane_pick(colg, 0), _lane_pick(colg, 1)
    igr, br = _row_pick(rowg, 0), _row_pick(rowg, 1)
    causal = _tri(c) > 0
    dmat = jnp.where(causal, bc - br + igr, -jnp.inf)
    inter = bc + m_prev
    mt = lax.stop_gradient(jnp.maximum(inter, jnp.max(dmat, axis=1, keepdims=True)))
    wt = jnp.exp(dmat - mt)
    w_prev = jnp.exp(inter - mt)
    qs = q * (dh ** -0.5)
    s = _dot_nt(qs, k) * wt
    num = _dot_nn(s, v) + w_prev * _dot_nn(qs, cst)
    den = jnp.sum(s, axis=1, keepdims=True) + w_prev * jnp.sum(qs * nst, axis=1, keepdims=True)
    h = num / jnp.maximum(jnp.abs(den), jnp.exp(-mt))
    last = (lax.broadcasted_iota(jnp.int32, (c, 1), 0) == c - 1).astype(F32)
    blast = jnp.sum(bc * last, axis=0, keepdims=True)
    g = blast - bc + igc
    m_new = lax.stop_gradient(jnp.maximum(blast + m_prev, jnp.max(g, axis=0, keepdims=True)))
    decay = jnp.exp(blast + m_prev - m_new)
    wk = jnp.exp(g - m_new) * k
    c_new = decay * cst + _dot_tn(wk, v)
    n_new = decay * nst + jnp.sum(wk, axis=0, keepdims=True)
    mu = jnp.mean(h, axis=1, keepdims=True)
    hc = h - mu
    var = jnp.mean(hc * hc, axis=1, keepdims=True)
    out = hc * lax.rsqrt(var + HEAD_NORM_EPS) * nw + sk * _silu(cpre)
    return out * _silu(zb), c_new, n_new, m_new


def _ml_core_specs(nc, dh, rev):
    ch = (lambda c: nc - 1 - c) if rev else (lambda c: c)
    rb = pl.BlockSpec((CHUNK, dh), lambda b_, c, h: (b_ * nc + ch(c), h))
    colb = pl.BlockSpec((None, CHUNK, 2), lambda b_, c, h: (h, b_ * nc + ch(c), 0))
    rowb = pl.BlockSpec((None, None, 2, CHUNK), lambda b_, c, h: (b_ * nc + ch(c), h, 0, 0))
    pb = pl.BlockSpec((1, dh), lambda b_, c, h: (0, h))
    cb = pl.BlockSpec((None, None, dh, dh), lambda b_, c, h: (b_ * nc + ch(c), h, 0, 0))
    nb_ = pl.BlockSpec((None, None, 1, dh), lambda b_, c, h: (b_ * nc + ch(c), h, 0, 0))
    mb = pl.BlockSpec((None, None, 1, 1), lambda b_, c, h: (b_ * nc + ch(c), h, 0, 0))
    return rb, colb, rowb, pb, cb, nb_, mb


def ml_core_fwd(q, k, v, colg, rowg, cpre, zb, nw, sk, nb, nh, name):
    rows, width = q.shape
    dh = width // nh
    nc = rows // nb // CHUNK
    rb, colb, rowb, pb, cb, nb_, mb = _ml_core_specs(nc, dh, False)

    def body(q_ref, k_ref, v_ref, col_ref, row_ref, c_ref, z_ref, nw_ref, sk_ref, y_ref, cs_ref, ns_ref, ms_ref,
             cst_ref, nst_ref, mst_ref):
        c = pl.program_id(1)
        h = pl.program_id(2)

        @pl.when(c == 0)
        def _():
            cst_ref[h] = jnp.zeros((dh, dh), F32)
            nst_ref[h] = jnp.zeros((1, dh), F32)
            mst_ref[h] = jnp.zeros((1, 1), F32)

        cst, nst, m_prev = cst_ref[h], nst_ref[h], mst_ref[h]
        cs_ref[...] = cst
        ns_ref[...] = nst
        ms_ref[...] = m_prev
        y, c_new, n_new, m_new = _ml_core_tile(q_ref[...], k_ref[...], v_ref[...], col_ref[...], row_ref[...],
                                               c_ref[...], z_ref[...], nw_ref[...], sk_ref[...], cst, nst, m_prev)
        y_ref[...] = y.astype(BF16)
        cst_ref[h] = c_new
        nst_ref[h] = n_new
        mst_ref[h] = m_new

    nbc = nb * nc
    return pl.pallas_call(
        body, name=name, grid=(nb, nc, nh), in_specs=[rb, rb, rb, colb, rowb, rb, rb, pb, pb],
        out_specs=[rb, cb, nb_, mb],
        out_shape=[jax.ShapeDtypeStruct((rows, width), BF16), jax.ShapeDtypeStruct((nbc, nh, dh, dh), F32),
                   jax.ShapeDtypeStruct((nbc, nh, 1, dh), F32), jax.ShapeDtypeStruct((nbc, nh, 1, 1), F32)],
        scratch_shapes=[pltpu.VMEM((nh, dh, dh), F32), pltpu.VMEM((nh, 1, dh), F32), pltpu.VMEM((nh, 1, 1), F32)],
        compiler_params=_params(("arbitrary", "arbitrary", "arbitrary")))(q, k, v, colg, rowg, cpre, zb, nw, sk)


def ml_core_bwd(q, k, v, colg, rowg, cpre, zb, nw, sk, cs, ns, ms, dy, nb, nh, name):
    rows, width = q.shape
    dh = width // nh
    nc = rows // nb // CHUNK
    rb, colb, rowb, pb, cb, nb_, mb = _ml_core_specs(nc, dh, True)

    def body(q_ref, k_ref, v_ref, col_ref, row_ref, c_ref, z_ref, nw_ref, sk_ref, cs_ref, ns_ref, ms_ref, dy_ref,
             dq_ref, dk_ref, dv_ref, dc_ref, dz_ref, dcol_ref, drow_ref, dnw_ref, dsk_ref, dcst_ref, dnst_ref):
        bb = pl.program_id(0)
        step = pl.program_id(1)
        h = pl.program_id(2)

        @pl.when(jnp.logical_and(bb == 0, jnp.logical_and(step == 0, h == 0)))
        def _():
            dnw_ref[...] = jnp.zeros_like(dnw_ref)
            dsk_ref[...] = jnp.zeros_like(dsk_ref)

        @pl.when(step == 0)
        def _():
            dcst_ref[h] = jnp.zeros((dh, dh), F32)
            dnst_ref[h] = jnp.zeros((1, dh), F32)

        m_prev = ms_ref[...]

        def f(*a):
            return _ml_core_tile(*a, m_prev)[:3]

        _, vjp = jax.vjp(f, q_ref[...], k_ref[...], v_ref[...], col_ref[...], row_ref[...], c_ref[...], z_ref[...],
                         nw_ref[...], sk_ref[...], cs_ref[...], ns_ref[...])
        g = vjp((dy_ref[...], dcst_ref[h], dnst_ref[h]))
        dq_ref[...] = g[0]
        dk_ref[...] = g[1]
        dv_ref[...] = g[2]
        dcol_ref[...] = g[3]
        drow_ref[...] = g[4]
        dc_ref[...] = g[5]
        dz_ref[...] = g[6]
        dnw_ref[h] += g[7]
        dsk_ref[h] += g[8]
        dcst_ref[h] = g[9]
        dnst_ref[h] = g[10]

    nbc = nb * nc
    accb = pl.BlockSpec((nh, 1, dh), lambda b_, c, h: (0, 0, 0))
    return pl.pallas_call(
        body, name=name, grid=(nb, nc, nh), in_specs=[rb, rb, rb, colb, rowb, rb, rb, pb, pb, cb, nb_, mb, rb],
        out_specs=[rb, rb, rb, rb, rb, colb, rowb, accb, accb],
        out_shape=[jax.ShapeDtypeStruct((rows, width), F32)] * 5
        + [jax.ShapeDtypeStruct(colg.shape, F32), jax.ShapeDtypeStruct(rowg.shape, F32),
           jax.ShapeDtypeStruct((nh, 1, dh), F32), jax.ShapeDtypeStruct((nh, 1, dh), F32)],
        scratch_shapes=[pltpu.VMEM((nh, dh, dh), F32), pltpu.VMEM((nh, 1, dh), F32)],
        compiler_params=_params(("arbitrary", "arbitrary", "arbitrary")))(
            q, k, v, colg, rowg, cpre, zb, nw, sk, cs, ns, ms, dy)


def _ssd_dt_tile(dtr, bias, alog):
    dt = _softplus(dtr + bias)
    cum = _dot(_tri(CHUNK), dt * (-jnp.exp(alog)), ((1,), (0,)), precision=HI)
    return dt, cum


def _ssd_tile(xcs, bmc, cmc, cols, rows_, z, dvec, gn, states, hpg):
    npair = hpg // 2
    hd = SSD_HEAD_DIM
    xs = [_silu(x) for x in xcs]
    bm, cm = _silu(bmc), _silu(cmc)
    cb = _dot_nt(cm, bm)
    causal = _tri(CHUNK) > 0
    lane_lo = lax.broadcasted_iota(jnp.int32, (1, 2 * hd), 1) < hd
    row_lo = lax.broadcasted_iota(jnp.int32, (2 * hd, 1), 0) < hd
    lastsel = (lax.broadcasted_iota(jnp.int32, (CHUNK, 1), 0) == CHUNK - 1).astype(F32)
    heads = []
    for r in range(hpg):
        dtc, cumc = _lane_pick(cols, r), _lane_pick(cols, hpg + r)
        dtrow, cumr = _row_pick(rows_, r), _row_pick(rows_, hpg + r)
        w = cb * jnp.exp(jnp.where(causal, cumc - cumr, -jnp.inf)) * dtrow
        last = jnp.sum(cumc * lastsel, axis=0, keepdims=True)
        heads.append((w, jnp.exp(cumc), jnp.exp(last - cumc) * dtc, jnp.exp(last)))
    ys, new_states = [], []
    for j in range(npair):
        (wa, ea, da, la), (wb, eb, db, lb) = heads[2 * j], heads[2 * j + 1]
        yi = jnp.where(lane_lo, _dot_nn(wa, xs[j]), _dot_nn(wb, xs[j]))
        yst = jnp.where(lane_lo, ea, eb) * _dot_nt(cm, states[j])
        ys.append(yi + yst)
        xd = xs[j] * jnp.where(lane_lo, da, db)
        new_states.append(jnp.where(row_lo, la, lb) * states[j] + _dot_tn(xd, bm))
    y = jnp.concatenate(ys, axis=1) + dvec * jnp.concatenate(xs, axis=1)
    yg = y * _silu(z)
    yn = yg * lax.rsqrt(jnp.mean(yg * yg, axis=1, keepdims=True) + NORM_EPS) * gn
    return yn, new_states


def _ssd_specs(nc, hpg, ng, rev):
    npair = hpg // 2
    gw = hpg * SSD_HEAD_DIM
    xblocks = ng * npair
    ch = (lambda c: nc - 1 - c) if rev else (lambda c: c)
    xs = [pl.BlockSpec((CHUNK, LANES), functools.partial(lambda b_, c, g, jj: (b_ * nc + ch(c), g * npair + jj), jj=j))
          for j in range(npair)]
    bmb = pl.BlockSpec((CHUNK, SSD_STATE), lambda b_, c, g: (b_ * nc + ch(c), xblocks + g))
    cmb = pl.BlockSpec((CHUNK, SSD_STATE), lambda b_, c, g: (b_ * nc + ch(c), xblocks + ng + g))
    colb = pl.BlockSpec((None, CHUNK, 2 * hpg), lambda b_, c, g: (g, b_ * nc + ch(c), 0))
    rowb = pl.BlockSpec((None, None, 2 * hpg, CHUNK), lambda b_, c, g: (b_ * nc + ch(c), g, 0, 0))
    zb = pl.BlockSpec((CHUNK, gw), lambda b_, c, g: (b_ * nc + ch(c), g))
    pb = pl.BlockSpec((1, gw), lambda b_, c, g: (0, g))
    sb = pl.BlockSpec((None, None, npair, 2 * SSD_HEAD_DIM, SSD_STATE), lambda b_, c, g: (b_ * nc + ch(c), g, 0, 0, 0))
    return xs, bmb, cmb, colb, rowb, zb, pb, sb


def ssd_core_fwd(cpre, cols, rows_, z, dvec, gn, nb, hpg, name):
    rows = cpre.shape[0]
    inner = z.shape[1]
    ng = inner // (hpg * SSD_HEAD_DIM)
    npair = hpg // 2
    nc = rows // nb // CHUNK
    xs, bmb, cmb, colb, rowb, zb, pb, sb = _ssd_specs(nc, hpg, ng, False)

    def body(*refs):
        x_refs = refs[:npair]
        bm_ref, cm_ref, col_ref, row_ref, z_ref, d_ref, gn_ref, y_ref, so_ref, st_ref = refs[npair:]
        c = pl.program_id(1)
        g = pl.program_id(2)

        @pl.when(c == 0)
        def _():
            st_ref[g] = jnp.zeros((npair, 2 * SSD_HEAD_DIM, SSD_STATE), F32)

        so_ref[...] = st_ref[g]
        states = [st_ref[g, j] for j in range(npair)]
        yn, new_states = _ssd_tile([r[...] for r in x_refs], bm_ref[...], cm_ref[...], col_ref[...], row_ref[...],
                                   z_ref[...], d_ref[...], gn_ref[...], states, hpg)
        y_ref[...] = yn.astype(BF16)
        for j in range(npair):
            st_ref[g, j] = new_states[j]

    return pl.pallas_call(
        body, name=name, grid=(nb, nc, ng), in_specs=xs + [bmb, cmb, colb, rowb, zb, pb, pb],
        out_specs=[zb, sb],
        out_shape=[jax.ShapeDtypeStruct((rows, inner), BF16),
                   jax.ShapeDtypeStruct((nb * nc, ng, npair, 2 * SSD_HEAD_DIM, SSD_STATE), F32)],
        scratch_shapes=[pltpu.VMEM((ng, npair, 2 * SSD_HEAD_DIM, SSD_STATE), F32)],
        compiler_params=_params(("arbitrary", "arbitrary", "arbitrary")))(
            *([cpre] * npair), cpre, cpre, cols, rows_, z, dvec, gn)


def ssd_core_bwd(cpre, cols, rows_, z, dvec, gn, states, dyn, nb, hpg, name):
    rows = cpre.shape[0]
    inner = z.shape[1]
    gw = hpg * SSD_HEAD_DIM
    ng = inner // gw
    npair = hpg // 2
    nc = rows // nb // CHUNK
    xs, bmb, cmb, colb, rowb, zb, pb, sb = _ssd_specs(nc, hpg, ng, True)

    def body(*refs):
        x_refs = refs[:npair]
        (bm_ref, cm_ref, col_ref, row_ref, z_ref, d_ref, gn_ref, s_ref, dy_ref,
         dx_ref, dbm_ref, dcm_ref, dcol_ref, drow_ref, dz_ref, dd_ref, dgn_ref, dst_ref) = refs[npair:]
        bb = pl.program_id(0)
        step = pl.program_id(1)
        g = pl.program_id(2)

        @pl.when(jnp.logical_and(bb == 0, jnp.logical_and(step == 0, g == 0)))
        def _():
            dd_ref[...] = jnp.zeros_like(dd_ref)
            dgn_ref[...] = jnp.zeros_like(dgn_ref)

        @pl.when(step == 0)
        def _():
            dst_ref[g] = jnp.zeros((npair, 2 * SSD_HEAD_DIM, SSD_STATE), F32)

        def f(xcs, bmc, cmc, cv, rv, zv, dv_, gv, sts):
            return _ssd_tile(xcs, bmc, cmc, cv, rv, zv, dv_, gv, sts, hpg)

        _, vjp = jax.vjp(f, [r[...] for r in x_refs], bm_ref[...], cm_ref[...], col_ref[...], row_ref[...], z_ref[...],
                         d_ref[...], gn_ref[...], [s_ref[j] for j in range(npair)])
        gr = vjp((dy_ref[...], [dst_ref[g, j] for j in range(npair)]))
        dx_ref[...] = jnp.concatenate(gr[0], axis=1)
        dbm_ref[...] = gr[1]
        dcm_ref[...] = gr[2]
        dcol_ref[...] = gr[3]
        drow_ref[...] = gr[4]
        dz_ref[...] = gr[5]
        dd_ref[g] += gr[6]
        dgn_ref[g] += gr[7]
        for j in range(npair):
            dst_ref[g, j] = gr[8][j]

    ch = lambda c: nc - 1 - c
    nblk = pl.BlockSpec((CHUNK, SSD_STATE), lambda b_, c, g: (b_ * nc + ch(c), g))
    accb = pl.BlockSpec((ng, 1, gw), lambda b_, c, g: (0, 0, 0))
    return pl.pallas_call(
        body, name=name, grid=(nb, nc, ng), in_specs=xs + [bmb, cmb, colb, rowb, zb, pb, pb, sb, zb],
        out_specs=[zb, nblk, nblk, colb, rowb, zb, accb, accb],
        out_shape=[jax.ShapeDtypeStruct((rows, inner), F32), jax.ShapeDtypeStruct((rows, ng * SSD_STATE), F32),
                   jax.ShapeDtypeStruct((rows, ng * SSD_STATE), F32), jax.ShapeDtypeStruct(cols.shape, F32),
                   jax.ShapeDtypeStruct(rows_.shape, F32), jax.ShapeDtypeStruct((rows, inner), F32),
                   jax.ShapeDtypeStruct((ng, 1, gw), F32), jax.ShapeDtypeStruct((ng, 1, gw), F32)],
        scratch_shapes=[pltpu.VMEM((ng, npair, 2 * SSD_HEAD_DIM, SSD_STATE), F32)],
        compiler_params=_params(("arbitrary", "arbitrary", "arbitrary")))(
            *([cpre] * npair), cpre, cpre, cols, rows_, z, dvec, gn, states, dyn)


def _hw_expand(w):
    n, bi, _ = w.shape
    per = LANES // bi
    eye = jnp.eye(per, dtype=F32)
    return jnp.einsum("jbio,bc->jbico", w.reshape(n // per, per, bi, bi), eye).reshape(n // per, LANES, LANES)


def _hw_contract(d, bi=QKV_BLOCK):
    per = LANES // bi
    eye = jnp.eye(per, dtype=F32)
    return jnp.einsum("jbico,bc->jbio", d.reshape(d.shape[0], per, bi, per, bi), eye).reshape(-1, bi, bi)


def _wg_expand(wg, width):
    pad = jnp.pad(wg, ((0, 0), (0, LANES - wg.shape[1])))
    return [pad[i * width:(i + 1) * width].reshape(width // LANES, LANES, LANES) for i in range(3)]


def _wg_contract(dgs, ngate):
    return jnp.concatenate([d[:, :, :ngate].reshape(-1, ngate) for d in dgs], axis=0)


def _pad_lanes(a):
    return jnp.pad(a, ((0, 0), (0, LANES - a.shape[1])))


def _pairs_to_layouts(first, second, ngrp, per, nbc):
    rows = first.shape[0]
    both = jnp.concatenate([first.reshape(rows, ngrp, per), second.reshape(rows, ngrp, per)], axis=2)
    return both.transpose(1, 0, 2), both.reshape(nbc, CHUNK, ngrp, 2 * per).transpose(0, 2, 3, 1)


def _layouts_to_pairs(dcols, drows, ngrp, per):
    rows = dcols.shape[1]
    both = dcols.transpose(1, 0, 2) + drows.transpose(0, 3, 1, 2).reshape(rows, ngrp, 2 * per)
    return both[:, :, :per].reshape(rows, ngrp * per), both[:, :, per:].reshape(rows, ngrp * per)


def _local_step(x, target, bw, sp):
    nb, seq, d = x.shape
    nh, hpg = MLSTM_HEADS, SSD_HPG
    t_len = N_META + seq
    nc = -(-t_len // CHUNK)
    tp = nc * CHUNK
    rows = nb * tp
    nbc = nb * nc
    meta = sp["meta_tokens"]
    h0 = jnp.concatenate([jnp.broadcast_to(meta[None], (nb, N_META, d)), x, jnp.zeros((nb, tp - t_len, d), F32)], axis=1)
    h0 = h0.reshape(rows, d)
    tgt = jnp.pad(target, ((0, 0), (N_META, tp - t_len), (0, 0))).reshape(rows, d)

    n0 = norm_fwd(h0, sp["ab_norm"], "norm0")
    pa = mm(n0, bw["W0a"], "nn", "mm_pa")
    xb = mm(n0, bw["W0xb"], "nn", "mm_xb")
    zb = mm(n0, bw["W0zb"], "nn", "mm_zb")
    s5w = pa.shape[1] // 2
    mlw = xb.shape[1]
    s5_args = (sp["s5_lambda_re"], sp["s5_lambda_im"], sp["s5_log_dt"].reshape(-1), sp["s5_b_re"], sp["s5_b_im"])
    (ar, ai, bbr, bbi), s5_disc_vjp = jax.vjp(_s5_discretize, *s5_args)
    sg, spn, shh = bbr.shape
    bre, bim, cre, cim, are, aie = _s5_expand(ar, ai, bbr, bbi, sp["s5_c_re"], sp["s5_c_im"])
    ys5, gb, s5st = s5_fwd(pa, bre, bim, cre, cim, are, aie, sp["s5_d"], nb, "s5_fwd")
    tglu = mm(gb, bw["glu"], "nn", "mm_glu")

    def glu_tile(ys, tt, za, gbias):
        return _gelu(ys) * _sigmoid(tt + gbias) * _silu(za)

    ya = rowwise("glu_fwd", lambda i, ys, tt, pab, gbias: glu_tile(ys, tt, pab[:, s5w:], gbias),
                 [ys5, tglu, pa], [sp["s5_glu_b"]], [(s5w, BF16)], tr=_tile(rows, 256, 16))[0]

    cpre0 = conv_fwd(xb, sp["ml_conv_w"], sp["ml_conv_b"], nb, "ml_conv_fwd")
    wq_e, wk_e, wv_e = _hw_expand(sp["ml_wq"]), _hw_expand(sp["ml_wk"]), _hw_expand(sp["ml_wv"])
    gq, gk, gv = _wg_expand(sp["ml_w_gate"], mlw)
    q, k, v, gl = ml_proj_fwd(cpre0, xb, wq_e, wk_e, wv_e, gq, gk, gv, "ml_proj_fwd")
    bgate = _pad_lanes(sp["ml_b_gate"])
    gout = rowwise("ml_gates_fwd", lambda i, g_, b_: _ml_gates_tile(g_, b_, nh), [gl], [bgate], [(LANES, F32)], tr=CHUNK)[0]
    colg, rowg = _pairs_to_layouts(gout[:, :nh], gout[:, nh:2 * nh], nh, 1, nbc)
    yb, ml_cs, ml_ns, ml_ms = ml_core_fwd(q, k, v, colg, rowg, cpre0, zb, sp["ml_norm"], sp["ml_skip"], nb, nh, "ml_core_fwd")
    h1 = mm(ya, bw["Wo0a"], "nn", "mm_out0a", resid=h0)
    h1 = mm(yb, bw["Wo0b"], "nn", "mm_out0b", resid=h1)

    n1 = norm_fwd(h1, sp["ssd_norm"], "norm1")
    z1 = mm(n1, bw["W1z"], "nn", "mm_z1")
    xbc = mm(n1, bw["W1x"], "nn", "mm_xbc")
    dtr = mm(n1, bw["W1dt"], "nn", "mm_dt")
    inner = z1.shape[1]
    ng = inner // (hpg * SSD_HEAD_DIM)
    nhd = ng * hpg
    cpre1 = conv_fwd(xbc, sp["ssd_conv_w"], sp["ssd_conv_b"], nb, "ssd_conv_fwd")
    dt_bias, a_log = _pad_lanes(sp["ssd_dt_bias"]), _pad_lanes(sp["ssd_a_log"])
    dt, cum = rowwise("ssd_dt_fwd", lambda i, r_, b_, a_: _ssd_dt_tile(r_, b_, a_), [dtr], [dt_bias, a_log],
                      [(LANES, F32), (LANES, F32)], tr=CHUNK)
    cols, rws = _pairs_to_layouts(dt[:, :nhd], cum[:, :nhd], ng, hpg, nbc)
    dvec = jnp.repeat(sp["ssd_d"], SSD_HEAD_DIM, axis=1)
    yn, ssd_st = ssd_core_fwd(cpre1, cols, rws, z1, dvec, sp["ssd_gnorm"], nb, hpg, "ssd_core_fwd")
    h2 = mm(yn, bw["Wo1"], "nn", "mm_out1", resid=h1)

    tr_l = _tile(tp, 256, 16)
    per_ex = tp // tr_l

    def loss_tile(i, hb, tb, gfn):
        tpos = (i % per_ex) * tr_l + lax.broadcasted_iota(jnp.int32, (tr_l, 1), 0)
        mask = jnp.logical_and(tpos >= N_META, tpos < t_len).astype(F32)

        def lf(hh, gg):
            e = (_rms(hh, gg) - tb) * mask
            return 0.5 * jnp.sum(e * e) / d

        lval, (dh, dg) = jax.value_and_grad(lf, (0, 1))(hb, gfn)
        return dh, jnp.full((1, LANES), lval, F32), dg

    fn = sp["final_norm"].reshape(1, d)
    dh2, loss_acc, dfn = rowwise("loss", loss_tile, [h2, tgt], [fn], [(d, F32)], [(1, LANES), (1, d)], tr=tr_l)

    gbig, gs = {}, {}
    gs["final_norm"] = dfn.reshape(sp["final_norm"].shape)
    dyn = mm(dh2, bw["Wo1"], "nt", "mm_dyn")
    gbig["Wo1"] = mm(yn, dh2, "tn", "mm_dWo1", out_dtype=BF16)
    dxs, dbm, dcm, dcols, drws, dz1, ddvec, dgn = ssd_core_bwd(cpre1, cols, rws, z1, dvec, sp["ssd_gnorm"], ssd_st, dyn,
                                                              nb, hpg, "ssd_core_bwd")
    gs["ssd_d"] = ddvec.reshape(1, nhd, SSD_HEAD_DIM).sum(axis=2)
    gs["ssd_gnorm"] = dgn.reshape(1, inner)
    ddt, dcum = _layouts_to_pairs(dcols, drws, ng, hpg)

    def ssd_dt_bwd_tile(i, r_, ddt_, dcum_, b_, a_):
        _, vjp = jax.vjp(_ssd_dt_tile, r_, b_, a_)
        return vjp((ddt_, dcum_))

    ddtr, dbias, dalog = rowwise("ssd_dt_bwd", ssd_dt_bwd_tile, [dtr, _pad_lanes(ddt), _pad_lanes(dcum)], [dt_bias, a_log],
                                 [(LANES, F32)], [(1, LANES), (1, LANES)], tr=CHUNK)
    gs["ssd_dt_bias"] = dbias[:, :nhd]
    gs["ssd_a_log"] = dalog[:, :nhd]
    dcpre1 = jnp.concatenate([dxs, dbm, dcm], axis=1)
    dxbc, dcw1, dcb1 = conv_bwd(dcpre1, xbc, sp["ssd_conv_w"], nb, "ssd_conv_bwd")
    gs["ssd_conv_w"] = dcw1[:sp["ssd_conv_w"].shape[0]]
    gs["ssd_conv_b"] = dcb1
    dn1 = mm(dz1, bw["W1z"], "nt", "mm_dn1z")
    dn1 = mm(dxbc, bw["W1x"], "nt", "mm_dn1x", resid=dn1)
    dn1 = mm(ddtr, bw["W1dt"], "nt", "mm_dn1dt", resid=dn1)
    gbig["W1z"] = mm(n1, dz1, "tn", "mm_dW1z", out_dtype=BF16)
    gbig["W1x"] = mm(n1, dxbc, "tn", "mm_dW1x", out_dtype=BF16)
    gbig["W1dt"] = mm(n1, ddtr, "tn", "mm_dW1dt", out_dtype=BF16)
    dh1, dg1 = norm_bwd(h1, sp["ssd_norm"], dn1, dh2, "norm1_bwd")
    gs["ssd_norm"] = dg1

    dya = mm(dh1, bw["Wo0a"], "nt", "mm_dya")
    dyb = mm(dh1, bw["Wo0b"], "nt", "mm_dyb")
    gbig["Wo0a"] = mm(ya, dh1, "tn", "mm_dWo0a", out_dtype=BF16)
    gbig["Wo0b"] = mm(yb, dh1, "tn", "mm_dWo0b", out_dtype=BF16)
    (dq, dk, dv, dcp_skip, dzb, dcolg, drowg, dnw, dsk) = ml_core_bwd(
        q, k, v, colg, rowg, cpre0, zb, sp["ml_norm"], sp["ml_skip"], ml_cs, ml_ns, ml_ms, dyb, nb, nh, "ml_core_bwd")
    gs["ml_norm"] = dnw.reshape(1, mlw)
    gs["ml_skip"] = dsk.reshape(1, mlw)
    dig, dbcum = _layouts_to_pairs(dcolg, drowg, nh, 1)
    dgout = _pad_lanes(jnp.concatenate([dig, dbcum], axis=1))

    def ml_gates_bwd_tile(i, g_, dgo, b_):
        _, vjp = jax.vjp(lambda a, b: _ml_gates_tile(a, b, nh), g_, b_)
        return vjp(dgo)

    dgl, dbg = rowwise("ml_gates_bwd", ml_gates_bwd_tile, [gl, dgout], [bgate], [(LANES, F32)], [(1, LANES)], tr=CHUNK)
    gs["ml_b_gate"] = dbg[:, :2 * nh]
    dcpre0, dxb_v, dwq, dwk, dwv, dgq, dgk, dgv = ml_proj_bwd(cpre0, xb, wq_e, wk_e, wv_e, gq, gk, gv, dq, dk, dv, dgl,
                                                            dcp_skip, "ml_proj_bwd")
    gs["ml_wq"], gs["ml_wk"], gs["ml_wv"] = _hw_contract(dwq), _hw_contract(dwk), _hw_contract(dwv)
    gs["ml_w_gate"] = _wg_contract([dgq, dgk, dgv], 2 * nh)
    dxb, dcw0, dcb0 = conv_bwd(dcpre0, xb, sp["ml_conv_w"], nb, "ml_conv_bwd", resid=dxb_v)
    gs["ml_conv_w"] = dcw0[:sp["ml_conv_w"].shape[0]]
    gs["ml_conv_b"] = dcb0

    def glu_bwd_tile(i, ys, tt, pab, dy_, gbias):
        _, vjp = jax.vjp(glu_tile, ys, tt, pab[:, s5w:], gbias)
        return vjp(dy_)

    dys_direct, dtglu, dza, dglub = rowwise("glu_bwd", glu_bwd_tile, [ys5, tglu, pa, dya], [sp["s5_glu_b"]],
                                            [(s5w, F32)] * 3, [(1, s5w)], tr=_tile(rows, 256, 16))
    gs["s5_glu_b"] = dglub
    dgb = mm(dtglu, bw["glu"], "nt", "mm_dgb")
    gbig["glu"] = mm(gb, dtglu, "tn", "mm_dglu", out_dtype=BF16)

    def gelu_bwd_tile(i, ys, dg_, direct):
        _, vjp = jax.vjp(_gelu, ys)
        return vjp(dg_)[0] + direct

    dys5 = rowwise("gelu_bwd", gelu_bwd_tile, [ys5, dgb, dys_direct], [], [(s5w, F32)], tr=_tile(rows, 256, 16))[0]
    du, dbre, dbim, dcre, dcim, dare, daie, dd5 = s5_bwd(pa, dys5, s5st, bre, bim, cre, cim, are, aie, sp["s5_d"], nb, "s5_bwd")
    gs["s5_d"] = dd5
    dbbr, dbbi, dcr, dci, dar, dai = _s5_contract(dbre, dbim, dcre, dcim, dare, daie, sg, spn, shh)
    gs["s5_c_re"], gs["s5_c_im"] = dcr, dci
    (gs["s5_lambda_re"], gs["s5_lambda_im"], dlogdt, gs["s5_b_re"], gs["s5_b_im"]) = s5_disc_vjp((dar, dai, dbbr, dbbi))
    gs["s5_log_dt"] = dlogdt.reshape(1, -1)
    dpa = jnp.concatenate([du, dza], axis=1)
    dn0 = mm(dpa, bw["W0a"], "nt", "mm_dn0a")
    dn0 = mm(dxb, bw["W0xb"], "nt", "mm_dn0xb", resid=dn0)
    dn0 = mm(dzb, bw["W0zb"], "nt", "mm_dn0zb", resid=dn0)
    gbig["W0a"] = mm(n0, dpa, "tn", "mm_dW0a", out_dtype=BF16)
    gbig["W0xb"] = mm(n0, dxb, "tn", "mm_dW0xb", out_dtype=BF16)
    gbig["W0zb"] = mm(n0, dzb, "tn", "mm_dW0zb", out_dtype=BF16)
    dh0, dg0 = norm_bwd(h0, sp["ab_norm"], dn0, dh1, "norm0_bwd")
    gs["ab_norm"] = dg0
    dh0 = dh0.reshape(nb, tp, d)
    gs["meta_tokens"] = jnp.sum(dh0[:, :N_META], axis=0)
    return loss_acc[0, 0], dh0, gbig, gs


N_DEV = 8
N_CHIP = 4
MESH = pl.DeviceIdType.MESH
_HBM = pl.BlockSpec(memory_space=pltpu.HBM)


def _place():
    x, y, c = lax.axis_index("x"), lax.axis_index("y"), lax.axis_index("c")
    return x, y, c, [(1 - x, y), (x, 1 - y), (1 - x, 1 - y)]


def all_gather8(v, name):
    m_per, n = v.shape

    def body(x_ref, out_ref, send_sems, recv_sems, local_sem):
        x, y, c, chips = _place()
        me, sibling = (x, y, c), (x, y, 1 - c)

        def rows(px, py, pc):
            return out_ref.at[pl.ds((4 * px + 2 * py + pc) * m_per, m_per), :]

        def copy(kk, block, to, src=None):
            return pltpu.make_async_remote_copy(
                src_ref=rows(*block) if src is None else src, dst_ref=rows(*block), send_sem=send_sems.at[kk],
                recv_sem=recv_sems.at[kk], device_id=to, device_id_type=MESH)

        mine = pltpu.make_async_copy(x_ref, rows(*me), local_sem)
        mine.start()
        first = [copy(0, me, sibling, src=x_ref)]
        first += [copy(1 + j, me, (*chip, c), src=x_ref) for j, chip in enumerate(chips)]
        for cp in first:
            cp.start()
        passed = [copy(4 + j, (*chip, c), sibling) for j, chip in enumerate(chips)]
        for j, chip in enumerate(chips):
            copy(1 + j, (*chip, c), me).wait_recv()
            passed[j].start()
        copy(0, sibling, me).wait_recv()
        for j, chip in enumerate(chips):
            copy(4 + j, (*chip, 1 - c), me).wait_recv()
        for cp in first + passed:
            cp.wait_send()
        mine.wait()

    return pl.pallas_call(
        body, name=name, out_shape=jax.ShapeDtypeStruct((N_DEV * m_per, n), v.dtype),
        in_specs=[pl.BlockSpec(memory_space=pltpu.VMEM)], out_specs=pl.BlockSpec(memory_space=pltpu.VMEM),
        scratch_shapes=[pltpu.SemaphoreType.DMA((7,)), pltpu.SemaphoreType.DMA((7,)), pltpu.SemaphoreType.DMA],
        compiler_params=pltpu.CompilerParams(vmem_limit_bytes=VMEM_LIMIT))(v)


def gather_chips(v, name):
    _, m, n = v.shape

    def body(x_ref, out_ref, send_sems, recv_sems, local_sem):
        x, y, c, chips = _place()
        k = 2 * x + y
        sibling = (x, y, 1 - c)

        def copy(kk, src, chip_k, half, to):
            return pltpu.make_async_remote_copy(src_ref=src, dst_ref=out_ref.at[chip_k, half], send_sem=send_sems.at[kk],
                                                recv_sem=recv_sems.at[kk], device_id=to, device_id_type=MESH)

        mine = pltpu.make_async_copy(x_ref, out_ref.at[k], local_sem)
        mine.start()
        first = [copy(j, x_ref.at[c], k, c, (*chip, c)) for j, chip in enumerate(chips)]
        for cp in first:
            cp.start()
        passed = []
        for j, (cx, cy) in enumerate(chips):
            kj = 2 * cx + cy
            copy(j, out_ref.at[kj, c], kj, c, (cx, cy, c)).wait_recv()
            fwd = copy(3 + j, out_ref.at[kj, c], kj, c, sibling)
            fwd.start()
            passed.append(fwd)
        for j, (cx, cy) in enumerate(chips):
            kj = 2 * cx + cy
            copy(3 + j, out_ref.at[kj, 1 - c], kj, 1 - c, sibling).wait_recv()
        for cp in first + passed:
            cp.wait_send()
        mine.wait()

    return pl.pallas_call(
        body, name=name, out_shape=jax.ShapeDtypeStruct((N_CHIP, 2, m, n), v.dtype), in_specs=[_HBM], out_specs=_HBM,
        scratch_shapes=[pltpu.SemaphoreType.DMA((6,)), pltpu.SemaphoreType.DMA((6,)), pltpu.SemaphoreType.DMA])(v)


def scatter_chips(p, name):
    _, m, n = p.shape

    def body(p_ref, out_ref, send_sems, recv_sems, local_sem):
        x, y, c, chips = _place()
        k = 2 * x + y
        sibling = (x, y, 1 - c)

        def copy(kk, src, chip_k, half, to):
            return pltpu.make_async_remote_copy(src_ref=src, dst_ref=out_ref.at[chip_k, half], send_sem=send_sems.at[kk],
                                                recv_sem=recv_sems.at[kk], device_id=to, device_id_type=MESH)

        mine = pltpu.make_async_copy(p_ref.at[k], out_ref.at[k, c], local_sem)
        mine.start()
        first = [copy(0, p_ref.at[k], k, c, sibling)]
        first += [copy(1 + j, p_ref.at[2 * cx + cy], k, c, (cx, cy, c)) for j, (cx, cy) in enumerate(chips)]
        for cp in first:
            cp.start()
        passed = []
        for j, (cx, cy) in enumerate(chips):
            kj = 2 * cx + cy
            copy(1 + j, out_ref.at[kj, c], kj, c, (cx, cy, c)).wait_recv()
            fwd = copy(4 + j, out_ref.at[kj, c], kj, c, sibling)
            fwd.start()
            passed.append(fwd)
        copy(0, out_ref.at[k, 1 - c], k, 1 - c, sibling).wait_recv()
        for j, (cx, cy) in enumerate(chips):
            kj = 2 * cx + cy
            copy(4 + j, out_ref.at[kj, 1 - c], kj, 1 - c, sibling).wait_recv()
        for cp in first + passed:
            cp.wait_send()
        mine.wait()

    return pl.pallas_call(
        body, name=name, out_shape=jax.ShapeDtypeStruct((N_CHIP, 2, m, n), p.dtype), in_specs=[_HBM], out_specs=_HBM,
        scratch_shapes=[pltpu.SemaphoreType.DMA((7,)), pltpu.SemaphoreType.DMA((7,)), pltpu.SemaphoreType.DMA])(p)


def swap_sibling(v, name):
    def body(v_ref, out_ref, send_sem, recv_sem):
        x, y, c, _ = _place()
        cp = pltpu.make_async_remote_copy(src_ref=v_ref, dst_ref=out_ref, send_sem=send_sem, recv_sem=recv_sem,
                                          device_id=(x, y, 1 - c), device_id_type=MESH)
        cp.start()
        cp.wait()

    return pl.pallas_call(
        body, name=name, out_shape=jax.ShapeDtypeStruct(v.shape, v.dtype), in_specs=[_HBM], out_specs=_HBM,
        scratch_shapes=[pltpu.SemaphoreType.DMA, pltpu.SemaphoreType.DMA])(v)


PACK_LANES = 512


def _pack(arrs, dtype, lanes, row_align):
    flat = jnp.concatenate([a.reshape(-1).astype(dtype) for a in arrs])
    unit = lanes * row_align
    total = -(-flat.shape[0] // unit) * unit
    return jnp.pad(flat, (0, total - flat.shape[0])).reshape(total // lanes, lanes)


def _unpack(flat, shapes):
    flat = flat.reshape(-1)
    out, off = [], 0
    for s in shapes:
        n = math.prod(s)
        out.append(flat[off:off + n].reshape(s))
        off += n
    return out


def _adam_tile(w, m, v, g):
    m2 = ADAM_B1 * m + (1.0 - ADAM_B1) * g
    v2 = ADAM_B2 * v + (1.0 - ADAM_B2) * (g * g)
    m_hat = m2 / (1.0 - ADAM_B1 ** ADAM_STEP)
    v_hat = v2 / (1.0 - ADAM_B2 ** ADAM_STEP)
    delta = -ADAM_LR * (m_hat / (jnp.sqrt(v_hat) + ADAM_EPS) + ADAM_WD * w)
    return delta, m2, v2


def adam_big(w, m, v, pieces, name):
    def f(i, wb, mb, vb, p0, p1, p2, p3):
        g = ((p0.astype(F32) + p1.astype(F32)) + p2.astype(F32)) + p3.astype(F32)
        return (g,) + _adam_tile(wb, mb, vb, g)

    width = w.shape[1]
    return rowwise(name, f, [w, m, v] + list(pieces), [], [(width, F32)] * 4, tr=_tile(w.shape[0], 128, 16))


_WEIGHTS = (
    ("meta_tokens", "small", 1), ("ab_norm", "small", None), ("ab_w_in", "big", 2), ("s5_lambda_re", "small", None),
    ("s5_lambda_im", "small", None), ("s5_log_dt", "small", None), ("s5_b_re", "small", None), ("s5_b_im", "small", None),
    ("s5_c_re", "small", None), ("s5_c_im", "small", None), ("s5_d", "small", None), ("s5_glu_w", "big", 1),
    ("s5_glu_b", "small", None), ("ml_conv_w", "small", 2), ("ml_conv_b", "small", None), ("ml_wq", "small", 1),
    ("ml_wk", "small", 1), ("ml_wv", "small", 1), ("ml_w_gate", "small", 1), ("ml_b_gate", "small", None),
    ("ml_norm", "small", None), ("ml_skip", "small", None), ("ab_w_out", "big", 1), ("ssd_norm", "small", 1),
    ("ssd_w_in", "big", 2), ("ssd_conv_w", "small", 2), ("ssd_conv_b", "small", 1), ("ssd_dt_bias", "small", None),
    ("ssd_a_log", "small", None), ("ssd_d", "small", None), ("ssd_gnorm", "small", 1), ("ssd_w_out", "big", 1),
    ("final_norm", "small", None),
)


def _squeeze(a):
    return a[0] if a.ndim >= 3 else a


def kernel(x, meta_tokens, ab_norm, ab_w_in, s5_lambda_re, s5_lambda_im, s5_log_dt, s5_b_re, s5_b_im, s5_c_re, s5_c_im, s5_d, s5_glu_w, s5_glu_b, ml_conv_w, ml_conv_b, ml_wq, ml_wk, ml_wv, ml_w_gate, ml_b_gate, ml_norm, ml_skip, ab_w_out, ssd_norm, ssd_w_in, ssd_conv_w, ssd_conv_b, ssd_dt_bias, ssd_a_log, ssd_d, ssd_gnorm, ssd_w_out, final_norm, loss_target, m_meta_tokens, m_ab_norm, m_ab_w_in, m_s5_lambda_re, m_s5_lambda_im, m_s5_log_dt, m_s5_b_re, m_s5_b_im, m_s5_c_re, m_s5_c_im, m_s5_d, m_s5_glu_w, m_s5_glu_b, m_ml_conv_w, m_ml_conv_b, m_ml_wq, m_ml_wk, m_ml_wv, m_ml_w_gate, m_ml_b_gate, m_ml_norm, m_ml_skip, m_ab_w_out, m_ssd_norm, m_ssd_w_in, m_ssd_conv_w, m_ssd_conv_b, m_ssd_dt_bias, m_ssd_a_log, m_ssd_d, m_ssd_gnorm, m_ssd_w_out, m_final_norm, v_meta_tokens, v_ab_norm, v_ab_w_in, v_s5_lambda_re, v_s5_lambda_im, v_s5_log_dt, v_s5_b_re, v_s5_b_im, v_s5_c_re, v_s5_c_im, v_s5_d, v_s5_glu_w, v_s5_glu_b, v_ml_conv_w, v_ml_conv_b, v_ml_wq, v_ml_wk, v_ml_wv, v_ml_w_gate, v_ml_b_gate, v_ml_norm, v_ml_skip, v_ab_w_out, v_ssd_norm, v_ssd_w_in, v_ssd_conv_w, v_ssd_conv_b, v_ssd_dt_bias, v_ssd_a_log, v_ssd_d, v_ssd_gnorm, v_ssd_w_out, v_final_norm):
    args = (meta_tokens, ab_norm, ab_w_in, s5_lambda_re, s5_lambda_im, s5_log_dt, s5_b_re, s5_b_im, s5_c_re, s5_c_im, s5_d, s5_glu_w, s5_glu_b, ml_conv_w, ml_conv_b, ml_wq, ml_wk, ml_wv, ml_w_gate, ml_b_gate, ml_norm, ml_skip, ab_w_out, ssd_norm, ssd_w_in, ssd_conv_w, ssd_conv_b, ssd_dt_bias, ssd_a_log, ssd_d, ssd_gnorm, ssd_w_out, final_norm)
    m_args = (m_meta_tokens, m_ab_norm, m_ab_w_in, m_s5_lambda_re, m_s5_lambda_im, m_s5_log_dt, m_s5_b_re, m_s5_b_im, m_s5_c_re, m_s5_c_im, m_s5_d, m_s5_glu_w, m_s5_glu_b, m_ml_conv_w, m_ml_conv_b, m_ml_wq, m_ml_wk, m_ml_wv, m_ml_w_gate, m_ml_b_gate, m_ml_norm, m_ml_skip, m_ab_w_out, m_ssd_norm, m_ssd_w_in, m_ssd_conv_w, m_ssd_conv_b, m_ssd_dt_bias, m_ssd_a_log, m_ssd_d, m_ssd_gnorm, m_ssd_w_out, m_final_norm)
    v_args = (v_meta_tokens, v_ab_norm, v_ab_w_in, v_s5_lambda_re, v_s5_lambda_im, v_s5_log_dt, v_s5_b_re, v_s5_b_im, v_s5_c_re, v_s5_c_im, v_s5_d, v_s5_glu_w, v_s5_glu_b, v_ml_conv_w, v_ml_conv_b, v_ml_wq, v_ml_wk, v_ml_wv, v_ml_w_gate, v_ml_b_gate, v_ml_norm, v_ml_skip, v_ab_w_out, v_ssd_norm, v_ssd_w_in, v_ssd_conv_w, v_ssd_conv_b, v_ssd_dt_bias, v_ssd_a_log, v_ssd_d, v_ssd_gnorm, v_ssd_w_out, v_final_norm)
    names = [w[0] for w in _WEIGHTS]
    kind = {w[0]: w[1] for w in _WEIGHTS}
    axis = {w[0]: w[2] for w in _WEIGHTS}
    w_loc = dict(zip(names, args))
    m_loc = dict(zip(names, m_args))
    v_loc = dict(zip(names, v_args))
    chip = 2 * lax.axis_index("x") + lax.axis_index("y")
    core = lax.axis_index("c")
    big = [n for n in names if kind[n] == "big"]
    small = [n for n in names if kind[n] == "small"]
    small_sh = [n for n in small if axis[n] is not None]

    packed_w = _pack([w_loc[n] for n in big], BF16, PACK_LANES, 32)
    half_rows = packed_w.shape[0] // 2
    gathered = gather_chips(packed_w.reshape(2, half_rows, PACK_LANES), "gather_big_w")
    gathered = gathered.reshape(N_CHIP, -1)
    full_big = {}
    off = 0
    for n in big:
        sz = math.prod(w_loc[n].shape)
        shards = [gathered[kk, off:off + sz].reshape(w_loc[n].shape) for kk in range(N_CHIP)]
        full_big[n] = jnp.concatenate(shards, axis=axis[n])[0]
        off += sz
    small_sh_shapes = [w_loc[n].shape for n in small_sh]
    packed_s = _pack([w_loc[n] for n in small_sh], F32, LANES, SUBLANES)
    g8 = all_gather8(packed_s, "gather_small_w").reshape(N_CHIP, 2, -1)
    sp = {}
    for n in small:
        if axis[n] is None:
            sp[n] = _squeeze(w_loc[n])
    per_chip = [_unpack(g8[kk, 0], small_sh_shapes) for kk in range(N_CHIP)]
    for i, n in enumerate(small_sh):
        sp[n] = _squeeze(jnp.concatenate([per_chip[kk][i] for kk in range(N_CHIP)], axis=axis[n]))

    s5w = full_big["s5_glu_w"].shape[0]
    mlw = full_big["ab_w_out"].shape[0] - s5w
    inner = full_big["ssd_w_out"].shape[0]
    w_in0, w_in1 = full_big["ab_w_in"], full_big["ssd_w_in"]
    n_heads1 = sp["ssd_d"].shape[1]
    cdim = w_in1.shape[1] - inner - n_heads1
    bw = dict(W0a=w_in0[:, :2 * s5w], W0xb=w_in0[:, 2 * s5w:2 * s5w + mlw], W0zb=w_in0[:, 2 * s5w + mlw:],
              glu=full_big["s5_glu_w"], Wo0a=full_big["ab_w_out"][:s5w], Wo0b=full_big["ab_w_out"][s5w:],
              W1z=w_in1[:, :inner], W1x=w_in1[:, inner:inner + cdim], W1dt=_pad_lanes(w_in1[:, inner + cdim:]),
              Wo1=full_big["ssd_w_out"])

    loss_local, dh0, gbig, gs = _local_step(x, loss_target, bw, sp)
    loss = lax.psum(loss_local, ("x", "y", "c"))
    grad_x = dh0[:, N_META:N_META + x.shape[1]]

    gfull = {
        "ab_w_in": jnp.concatenate([gbig["W0a"], gbig["W0xb"], gbig["W0zb"]], axis=1),
        "s5_glu_w": gbig["glu"],
        "ab_w_out": jnp.concatenate([gbig["Wo0a"], gbig["Wo0b"]], axis=0),
        "ssd_w_in": jnp.concatenate([gbig["W1z"], gbig["W1x"], gbig["W1dt"][:, :n_heads1]], axis=1),
        "ssd_w_out": gbig["Wo1"],
    }
    per_chip_g = []
    for kk in range(N_CHIP):
        parts = []
        for n in big:
            ax = axis[n] - 1
            size = w_loc[n].shape[axis[n]]
            parts.append(lax.slice_in_dim(gfull[n], kk * size, (kk + 1) * size, axis=ax))
        per_chip_g.append(_pack(parts, BF16, PACK_LANES, 32).reshape(2, half_rows, PACK_LANES))
    gp = jnp.stack(per_chip_g)
    g_mine = lax.dynamic_index_in_dim(gp, core, axis=1, keepdims=False).reshape(N_CHIP * half_rows, PACK_LANES)
    g_other = lax.dynamic_index_in_dim(gp, 1 - core, axis=1, keepdims=False).reshape(N_CHIP * half_rows, PACK_LANES)
    g_sib = swap_sibling(g_other, "swap_big_g")
    partial = rowwise("add_big_g", lambda i, a, b: a.astype(F32) + b.astype(F32), [g_mine, g_sib], [],
                      [(PACK_LANES, BF16)], tr=_tile(N_CHIP * half_rows, 512, 16))[0]
    pieces = scatter_chips(partial.reshape(N_CHIP, half_rows, PACK_LANES), "scatter_big_g").reshape(N_CHIP, -1)

    out_g, out_d, out_m, out_v = {}, {}, {}, {}
    off = 0
    for n in big:
        shp = w_loc[n].shape
        sz = math.prod(shp)
        r2 = (math.prod(shp[:-1]), shp[-1])
        pcs = [pieces[kk, off:off + sz].reshape(r2) for kk in range(N_CHIP)]
        off += sz
        g, dl, mn, vn = adam_big(w_loc[n].reshape(r2), m_loc[n].reshape(r2), v_loc[n].reshape(r2), pcs, "adam_" + n)
        out_g[n], out_d[n], out_m[n], out_v[n] = (a.reshape(shp) for a in (g, dl, mn, vn))

    small_full_shapes = [sp[n].shape for n in small]
    packed_gs = _pack([gs[n] for n in small], F32, LANES, SUBLANES)
    rows_s = packed_gs.shape[0]
    all_gs = all_gather8(packed_gs, "gather_small_g")
    blocks = [all_gs[i * rows_s:(i + 1) * rows_s] for i in range(N_DEV)]

    def sum8(i, *b):
        acc = b[0]
        for t in b[1:]:
            acc = acc + t
        return acc

    gsum = rowwise("sum_small_g", sum8, blocks, [], [(LANES, F32)], tr=_tile(rows_s, 512, 8))[0]
    g_small = dict(zip(small, _unpack(gsum, small_full_shapes)))
    g_loc = {}
    for n in small:
        g = g_small[n].reshape((1,) + g_small[n].shape) if w_loc[n].ndim >= 3 else g_small[n]
        if axis[n] is not None:
            size = w_loc[n].shape[axis[n]]
            g = lax.dynamic_slice_in_dim(g, chip * size, size, axis=axis[n])
        g_loc[n] = g.reshape(w_loc[n].shape)
    loc_shapes = [w_loc[n].shape for n in small]
    pw, pm, pv, pg = (_pack([d[n] for n in small], F32, LANES, SUBLANES) for d in (w_loc, m_loc, v_loc, g_loc))
    dl, mn, vn = rowwise("adam_small", lambda i, a, b, c_, d_: _adam_tile(a, b, c_, d_), [pw, pm, pv, pg], [],
                         [(LANES, F32)] * 3, tr=_tile(pw.shape[0], 512, 8))
    for d_out, flat in ((out_d, dl), (out_m, mn), (out_v, vn)):
        for n, a in zip(small, _unpack(flat, loc_shapes)):
            d_out[n] = a
    for n in small:
        out_g[n] = g_loc[n]

    return (loss, grad_x, *[out_g[n] for n in names], *[out_d[n] for n in names], *[out_m[n] for n in names],
            *[out_v[n] for n in names])
```

```python
import functools
import math

import jax
import jax.numpy as jnp
from jax import lax
from jax.experimental import pallas as pl
from jax.experimental.pallas import tpu as pltpu

F32 = jnp.float32
BF16 = jnp.bfloat16
HI = lax.Precision.HIGHEST

D_MODEL = 2048
SEQ = 2048
N_META = 16
CHUNK = 128
NORM_EPS = 1e-6
HEAD_NORM_EPS = 1e-5
S5_GROUP_SIZE = 16
S5_STATE = 64
MLSTM_HEADS = 8
QKV_BLOCK = 4
SSD_HEAD_DIM = 64
SSD_STATE = 128
SSD_HPG = 8
ADAM_LR = 0.001
ADAM_B1 = 0.9
ADAM_B2 = 0.999
ADAM_EPS = 1e-08
ADAM_WD = 0.01
ADAM_STEP = 10

LANES = 128
SUBLANES = 8
VMEM_LIMIT = 56 * 1024 * 1024


def _sigmoid(x):
    return 1.0 / (1.0 + jnp.exp(-x))


def _silu(x):
    return x * _sigmoid(x)


def _softplus(x):
    return jnp.maximum(x, 0.0) + jnp.log(1.0 + jnp.exp(-jnp.abs(x)))


def _log_sigmoid(x):
    return jnp.minimum(x, 0.0) - jnp.log(1.0 + jnp.exp(-jnp.abs(x)))


def _gelu(x):
    return 0.5 * x * (1.0 + jnp.tanh(math.sqrt(2.0 / math.pi) * (x + 0.044715 * (x * x * x))))


def _dot(a, b, dims, precision=None):
    return lax.dot_general(a, b, (dims, ((), ())), preferred_element_type=F32, precision=precision)


def _dot_nn(a, b):
    return _dot(a.astype(BF16), b.astype(BF16), ((1,), (0,)))


def _dot_nt(a, b):
    return _dot(a.astype(BF16), b.astype(BF16), ((1,), (1,)))


def _dot_tn(a, b):
    return _dot(a.astype(BF16), b.astype(BF16), ((0,), (0,)))


def _lane_pick(a, idx):
    sel = (lax.broadcasted_iota(jnp.int32, (1, a.shape[1]), 1) == idx).astype(a.dtype)
    return jnp.sum(a * sel, axis=1, keepdims=True)


def _row_pick(a, idx):
    sel = (lax.broadcasted_iota(jnp.int32, (a.shape[0], 1), 0) == idx).astype(a.dtype)
    return jnp.sum(a * sel, axis=0, keepdims=True)


def _tri(n, upper=False):
    r = lax.broadcasted_iota(jnp.int32, (n, n), 0)
    c = lax.broadcasted_iota(jnp.int32, (n, n), 1)
    return ((r <= c) if upper else (r >= c)).astype(F32)


def _tile(n, target, align):
    if n <= target:
        return n
    t = (target // align) * align
    while t >= align:
        if n % t == 0:
            return t
        t -= align
    return n


def _params(sem=None):
    return pltpu.CompilerParams(dimension_semantics=sem, vmem_limit_bytes=VMEM_LIMIT)


def mm(a, b, mode, name, resid=None, out_dtype=F32):
    if mode == "nn":
        (m, k), (k2, n) = a.shape, b.shape
    elif mode == "nt":
        (m, k), (n, k2) = a.shape, b.shape
    else:
        (k, m), (k2, n) = a.shape, b.shape
    assert k == k2, (a.shape, b.shape, mode)
    if mode == "tn":
        tm, tn, tk = _tile(m, 1024, LANES), _tile(n, 1024, LANES), _tile(k, 640, 16)
    else:
        tm, tn, tk = _tile(m, 1088, 16), _tile(n, 512, LANES), _tile(k, 512, LANES)
    nk = k // tk
    dims = {"nn": ((1,), (0,)), "nt": ((1,), (1,)), "tn": ((0,), (0,))}[mode]
    has_resid = resid is not None

    def body(*refs):
        if has_resid:
            a_ref, b_ref, r_ref, o_ref, acc_ref = refs
        else:
            a_ref, b_ref, o_ref, acc_ref = refs
        kk = pl.program_id(2)

        @pl.when(kk == 0)
        def _():
            acc_ref[...] = jnp.zeros_like(acc_ref)

        acc_ref[...] += _dot(a_ref[...].astype(BF16), b_ref[...].astype(BF16), dims)

        @pl.when(kk == nk - 1)
        def _():
            res = acc_ref[...]
            if has_resid:
                res = res + r_ref[...].astype(F32)
            o_ref[...] = res.astype(o_ref.dtype)

    if mode == "tn":
        a_spec = pl.BlockSpec((tk, tm), lambda i, j, kk: (kk, i))
    else:
        a_spec = pl.BlockSpec((tm, tk), lambda i, j, kk: (i, kk))
    if mode == "nt":
        b_spec = pl.BlockSpec((tn, tk), lambda i, j, kk: (j, kk))
    else:
        b_spec = pl.BlockSpec((tk, tn), lambda i, j, kk: (kk, j))
    o_spec = pl.BlockSpec((tm, tn), lambda i, j, kk: (i, j))
    in_specs = [a_spec, b_spec] + ([o_spec] if has_resid else [])
    args = (a, b) + ((resid,) if has_resid else ())
    return pl.pallas_call(
        body, name=name, grid=(m // tm, n // tn, nk), in_specs=in_specs, out_specs=o_spec,
        out_shape=jax.ShapeDtypeStruct((m, n), out_dtype), scratch_shapes=[pltpu.VMEM((tm, tn), F32)],
        compiler_params=_params(("parallel", "parallel", "arbitrary")))(*args)


def rowwise(name, f, rows, params, outs, accs=(), tr=128):
    n_rows = rows[0].shape[0]
    assert n_rows % tr == 0
    n_r, n_p, n_o, n_a = len(rows), len(params), len(outs), len(accs)

    def body(*refs):
        i = pl.program_id(0)
        r_vals = [r[...] for r in refs[:n_r]]
        p_vals = [r[...] for r in refs[n_r:n_r + n_p]]
        o_refs = refs[n_r + n_p:n_r + n_p + n_o]
        a_refs = refs[n_r + n_p + n_o:]
        res = f(i, *r_vals, *p_vals)
        if not isinstance(res, (tuple, list)):
            res = (res,)
        assert len(res) == n_o + n_a, (name, len(res))
        for o_ref, val in zip(o_refs, res[:n_o]):
            o_ref[...] = val.astype(o_ref.dtype)
        if n_a:
            @pl.when(i == 0)
            def _():
                for a_ref in a_refs:
                    a_ref[...] = jnp.zeros_like(a_ref)

            for a_ref, val in zip(a_refs, res[n_o:]):
                a_ref[...] += val.astype(F32)

    in_specs = [pl.BlockSpec((tr, r.shape[1]), lambda i: (i, 0)) for r in rows]
    in_specs += [pl.BlockSpec(p.shape, lambda i: (0, 0)) for p in params]
    out_specs = [pl.BlockSpec((tr, w), lambda i: (i, 0)) for w, _ in outs]
    out_specs += [pl.BlockSpec(s, lambda i: (0, 0)) for s in accs]
    out_shape = [jax.ShapeDtypeStruct((n_rows, w), dt) for w, dt in outs]
    out_shape += [jax.ShapeDtypeStruct(s, F32) for s in accs]
    res = pl.pallas_call(
        body, name=name, grid=(n_rows // tr,), in_specs=in_specs, out_specs=out_specs, out_shape=out_shape,
        compiler_params=_params(("arbitrary",)))(*rows, *params)
    return res


def _rms(x, g, eps=NORM_EPS):
    return x * lax.rsqrt(jnp.mean(x * x, axis=-1, keepdims=True) + eps) * g


def norm_fwd(x, g, name):
    return rowwise(name, lambda i, xb, gb: _rms(xb, gb), [x], [g], [(x.shape[1], BF16)], tr=_tile(x.shape[0], 256, 16))[0]


def norm_bwd(x, g, dn, resid, name):
    def f(i, xb, dnb, rb, gb):
        _, vjp = jax.vjp(_rms, xb, gb)
        dx, dg = vjp(dnb)
        return dx + rb, dg

    return rowwise(name, f, [x, dn, resid], [g], [(x.shape[1], F32)], [g.shape], tr=_tile(x.shape[0], 256, 16))


def conv_fwd(x, w, b, nb, name):
    rows, width = x.shape
    nc = rows // nb // CHUNK
    tw = _tile(width, 1024, LANES)
    ksz = w.shape[0]

    def body(x_ref, w_ref, b_ref, o_ref, ext_ref):
        c = pl.program_id(2)

        @pl.when(c == 0)
        def _():
            ext_ref[0:SUBLANES, :] = jnp.zeros((SUBLANES, tw), F32)

        xv = x_ref[...]
        ext_ref[SUBLANES:SUBLANES + CHUNK, :] = xv
        acc = jnp.broadcast_to(b_ref[...], (CHUNK, tw))
        for j in range(ksz):
            off = SUBLANES - (ksz - 1) + j
            acc = acc + w_ref[j:j + 1, :] * ext_ref[off:off + CHUNK, :]
        o_ref[...] = acc
        ext_ref[0:SUBLANES, :] = xv[CHUNK - SUBLANES:CHUNK, :]

    return pl.pallas_call(
        body, name=name, grid=(width // tw, nb, nc),
        in_specs=[pl.BlockSpec((CHUNK, tw), lambda j, bb, c: (bb * nc + c, j)),
                  pl.BlockSpec((ksz, tw), lambda j, bb, c: (0, j)),
                  pl.BlockSpec((1, tw), lambda j, bb, c: (0, j))],
        out_specs=pl.BlockSpec((CHUNK, tw), lambda j, bb, c: (bb * nc + c, j)),
        out_shape=jax.ShapeDtypeStruct((rows, width), F32),
        scratch_shapes=[pltpu.VMEM((CHUNK + 2 * SUBLANES, tw), F32)],
        compiler_params=_params(("arbitrary", "arbitrary", "arbitrary")))(x, w, b)


def conv_bwd(dc, x, w, nb, name, resid=None):
    rows, width = x.shape
    nc = rows // nb // CHUNK
    tw = _tile(width, 1024, LANES)
    ksz = w.shape[0]
    per = CHUNK // SUBLANES
    has_resid = resid is not None

    def body(*refs):
        if has_resid:
            dc_ref, x_ref, halo_ref, w_ref, r_ref, dx_ref, dw_ref, db_ref, extd_ref, extx_ref = refs
        else:
            dc_ref, x_ref, halo_ref, w_ref, dx_ref, dw_ref, db_ref, extd_ref, extx_ref = refs
        bb = pl.program_id(1)
        step = pl.program_id(2)
        c = nc - 1 - step

        @pl.when(jnp.logical_and(bb == 0, step == 0))
        def _():
            dw_ref[...] = jnp.zeros_like(dw_ref)
            db_ref[...] = jnp.zeros_like(db_ref)

        @pl.when(step == 0)
        def _():
            extd_ref[CHUNK:CHUNK + SUBLANES, :] = jnp.zeros((SUBLANES, tw), F32)

        dcv = dc_ref[...]
        extd_ref[0:CHUNK, :] = dcv
        extx_ref[0:SUBLANES, :] = jnp.where(c == 0, 0.0, halo_ref[...])
        extx_ref[SUBLANES:SUBLANES + CHUNK, :] = x_ref[...]
        dx = jnp.zeros((CHUNK, tw), F32)
        for j in range(ksz):
            up = ksz - 1 - j
            dx = dx + w_ref[j:j + 1, :] * extd_ref[up:up + CHUNK, :]
            off = SUBLANES - (ksz - 1) + j
            dw_ref[j:j + 1, :] += jnp.sum(dcv * extx_ref[off:off + CHUNK, :], axis=0, keepdims=True)
        if has_resid:
            dx = dx + r_ref[...]
        dx_ref[...] = dx
        db_ref[...] += jnp.sum(dcv, axis=0, keepdims=True)
        extd_ref[CHUNK:CHUNK + SUBLANES, :] = dcv[0:SUBLANES, :]

    def blk(j, bb, step):
        return (bb * nc + nc - 1 - step, j)

    def halo(j, bb, step):
        return (jnp.maximum((bb * nc + nc - 1 - step) * per - 1, 0), j)

    in_specs = [pl.BlockSpec((CHUNK, tw), blk), pl.BlockSpec((CHUNK, tw), blk), pl.BlockSpec((SUBLANES, tw), halo),
                pl.BlockSpec((ksz, tw), lambda j, bb, step: (0, j))]
    args = [dc, x, x, w]
    if has_resid:
        in_specs.append(pl.BlockSpec((CHUNK, tw), blk))
        args.append(resid)
    return pl.pallas_call(
        body, name=name, grid=(width // tw, nb, nc), in_specs=in_specs,
        out_specs=[pl.BlockSpec((CHUNK, tw), blk), pl.BlockSpec((SUBLANES, tw), lambda j, bb, step: (0, j)),
                   pl.BlockSpec((1, tw), lambda j, bb, step: (0, j))],
        out_shape=[jax.ShapeDtypeStruct((rows, width), F32), jax.ShapeDtypeStruct((SUBLANES, width), F32),
                   jax.ShapeDtypeStruct((1, width), F32)],
        scratch_shapes=[pltpu.VMEM((CHUNK + 2 * SUBLANES, tw), F32), pltpu.VMEM((CHUNK + 2 * SUBLANES, tw), F32)],
        compiler_params=_params(("arbitrary", "arbitrary", "arbitrary")))(*args)


S5_Q = 4


def _s5_fill_bu(u, bre_ref, bim_ref, xr_ref, xi_ref, ns):
    for s in range(ns):
        ub = u[:, s * LANES:(s + 1) * LANES].astype(BF16)
        bur = _dot(ub, bre_ref[s], ((1,), (0,)))
        bui = _dot(ub, bim_ref[s], ((1,), (0,)))
        for q in range(S5_Q):
            xr_ref[q, pl.ds(s, CHUNK, stride=ns), :] = bur[:, q * LANES:(q + 1) * LANES]
            xi_ref[q, pl.ds(s, CHUNK, stride=ns), :] = bui[:, q * LANES:(q + 1) * LANES]


def _s5_scan(xr_ref, xi_ref, ar_ref, ai_ref, st_ref, ns):
    ar = [ar_ref[q] for q in range(S5_Q)]
    ai = [ai_ref[q] for q in range(S5_Q)]

    def step(t, carry):
        rows = pl.ds(pl.multiple_of(t * ns, ns), ns)
        out = []
        for q in range(S5_Q):
            pr, pi_ = carry[2 * q], carry[2 * q + 1]
            nr = ar[q] * pr - ai[q] * pi_ + xr_ref[q, rows, :]
            ni = ar[q] * pi_ + ai[q] * pr + xi_ref[q, rows, :]
            xr_ref[q, rows, :] = nr
            xi_ref[q, rows, :] = ni
            out += [nr, ni]
        return tuple(out)

    init = []
    for q in range(S5_Q):
        init += [st_ref[0, q], st_ref[1, q]]
    fin = lax.fori_loop(0, CHUNK, step, tuple(init), unroll=2)
    for q in range(S5_Q):
        st_ref[0, q] = fin[2 * q]
        st_ref[1, q] = fin[2 * q + 1]


def s5_fwd(pa, bre, bim, cre, cim, ar, ai, dvec, nb, name):
    rows = pa.shape[0]
    width = pa.shape[1] // 2
    ns = width // LANES
    nc = rows // nb // CHUNK

    def body(u_ref, bre_ref, bim_ref, cre_ref, cim_ref, ar_ref, ai_ref, d_ref, y_ref, g_ref, so_ref, xr_ref, xi_ref, st_ref):
        c = pl.program_id(1)

        @pl.when(c == 0)
        def _():
            st_ref[...] = jnp.zeros_like(st_ref)

        so_ref[...] = st_ref[...]
        u = u_ref[...]
        _s5_fill_bu(u, bre_ref, bim_ref, xr_ref, xi_ref, ns)
        _s5_scan(xr_ref, xi_ref, ar_ref, ai_ref, st_ref, ns)
        for s in range(ns):
            acc = jnp.zeros((CHUNK, LANES), F32)
            for q in range(S5_Q):
                xr = xr_ref[q, pl.ds(s, CHUNK, stride=ns), :].astype(BF16)
                xi = xi_ref[q, pl.ds(s, CHUNK, stride=ns), :].astype(BF16)
                acc = acc + _dot(xr, cre_ref[s, q * LANES:(q + 1) * LANES, :], ((1,), (0,)))
                acc = acc - _dot(xi, cim_ref[s, q * LANES:(q + 1) * LANES, :], ((1,), (0,)))
            cols = slice(s * LANES, (s + 1) * LANES)
            y = acc + d_ref[:, cols] * u[:, cols]
            y_ref[:, cols] = y
            g_ref[:, cols] = _gelu(y).astype(BF16)

    whole3 = lambda a: pl.BlockSpec(a.shape, lambda b_, c: (0, 0, 0))
    return pl.pallas_call(
        body, name=name, grid=(nb, nc),
        in_specs=[pl.BlockSpec((CHUNK, width), lambda b_, c: (b_ * nc + c, 0)), whole3(bre), whole3(bim), whole3(cre),
                  whole3(cim), whole3(ar), whole3(ai), pl.BlockSpec((1, width), lambda b_, c: (0, 0))],
        out_specs=[pl.BlockSpec((CHUNK, width), lambda b_, c: (b_ * nc + c, 0)),
                   pl.BlockSpec((CHUNK, width), lambda b_, c: (b_ * nc + c, 0)),
                   pl.BlockSpec((None, 2, S5_Q, ns, LANES), lambda b_, c: (b_ * nc + c, 0, 0, 0, 0))],
        out_shape=[jax.ShapeDtypeStruct((rows, width), F32), jax.ShapeDtypeStruct((rows, width), BF16),
                   jax.ShapeDtypeStruct((nb * nc, 2, S5_Q, ns, LANES), F32)],
        scratch_shapes=[pltpu.VMEM((S5_Q, CHUNK * ns, LANES), F32), pltpu.VMEM((S5_Q, CHUNK * ns, LANES), F32),
                        pltpu.VMEM((2, S5_Q, ns, LANES), F32)],
        compiler_params=_params(("arbitrary", "arbitrary")))(pa, bre, bim, cre, cim, ar, ai, dvec)


def s5_bwd(pa, dys, states, bre, bim, cre, cim, ar, ai, dvec, nb, name):
    rows = pa.shape[0]
    width = pa.shape[1] // 2
    ns = width // LANES
    nc = rows // nb // CHUNK

    def body(u_ref, dy_ref, sin_ref, bre_ref, bim_ref, cre_ref, cim_ref, ar_ref, ai_ref, d_ref,
             du_ref, dbre_ref, dbim_ref, dcre_ref, dcim_ref, dar_ref, dai_ref, dd_ref,
             xr_ref, xi_ref, lr_ref, li_ref, st_ref, lam_ref):
        bb = pl.program_id(0)
        step_i = pl.program_id(1)

        @pl.when(jnp.logical_and(bb == 0, step_i == 0))
        def _():
            for r in (dbre_ref, dbim_ref, dcre_ref, dcim_ref, dar_ref, dai_ref, dd_ref):
                r[...] = jnp.zeros_like(r)

        @pl.when(step_i == 0)
        def _():
            lam_ref[...] = jnp.zeros_like(lam_ref)

        u = u_ref[...]
        dy = dy_ref[...]
        st_ref[...] = sin_ref[...]
        _s5_fill_bu(u, bre_ref, bim_ref, xr_ref, xi_ref, ns)
        _s5_scan(xr_ref, xi_ref, ar_ref, ai_ref, st_ref, ns)
        dd_ref[...] += jnp.sum(dy * u, axis=0, keepdims=True)
        for s in range(ns):
            dyb = dy[:, s * LANES:(s + 1) * LANES].astype(BF16)
            gr = _dot(dyb, cre_ref[s], ((1,), (1,)))
            gi = -_dot(dyb, cim_ref[s], ((1,), (1,)))
            for q in range(S5_Q):
                lr_ref[q, pl.ds(s, CHUNK, stride=ns), :] = gr[:, q * LANES:(q + 1) * LANES]
                li_ref[q, pl.ds(s, CHUNK, stride=ns), :] = gi[:, q * LANES:(q + 1) * LANES]
                xr = xr_ref[q, pl.ds(s, CHUNK, stride=ns), :].astype(BF16)
                xi = xi_ref[q, pl.ds(s, CHUNK, stride=ns), :].astype(BF16)
                dcre_ref[s, q * LANES:(q + 1) * LANES, :] += _dot(xr, dyb, ((0,), (0,)))
                dcim_ref[s, q * LANES:(q + 1) * LANES, :] -= _dot(xi, dyb, ((0,), (0,)))
        ar = [ar_ref[q] for q in range(S5_Q)]
        ai = [ai_ref[q] for q in range(S5_Q)]

        def one(t_rows, p_r, p_i, carry):
            out = []
            for q in range(S5_Q):
                l_r, l_i, da_r, da_i = carry[4 * q:4 * q + 4]
                n_r = lr_ref[q, t_rows, :] + ar[q] * l_r + ai[q] * l_i
                n_i = li_ref[q, t_rows, :] + ar[q] * l_i - ai[q] * l_r
                lr_ref[q, t_rows, :] = n_r
                li_ref[q, t_rows, :] = n_i
                xpr, xpi = p_r(q), p_i(q)
                out += [n_r, n_i, da_r + n_r * xpr + n_i * xpi, da_i + n_i * xpr - n_r * xpi]
            return tuple(out)

        def step(k, carry):
            t = CHUNK - 1 - k
            t_rows = pl.ds(pl.multiple_of(t * ns, ns), ns)
            p_rows = pl.ds(pl.multiple_of((t - 1) * ns, ns), ns)
            return one(t_rows, lambda q: xr_ref[q, p_rows, :], lambda q: xi_ref[q, p_rows, :], carry)

        init = []
        zero = jnp.zeros((ns, LANES), F32)
        for q in range(S5_Q):
            init += [lam_ref[0, q], lam_ref[1, q], zero, zero]
        carry = lax.fori_loop(0, CHUNK - 1, step, tuple(init), unroll=2)
        carry = one(pl.ds(0, ns), lambda q: sin_ref[0, q], lambda q: sin_ref[1, q], carry)
        for q in range(S5_Q):
            lam_ref[0, q] = carry[4 * q]
            lam_ref[1, q] = carry[4 * q + 1]
            dar_ref[q] += carry[4 * q + 2]
            dai_ref[q] += carry[4 * q + 3]
        for s in range(ns):
            cols = slice(s * LANES, (s + 1) * LANES)
            ub = u[:, cols].astype(BF16)
            acc = d_ref[:, cols] * dy[:, cols]
            for q in range(S5_Q):
                qs = slice(q * LANES, (q + 1) * LANES)
                lr = lr_ref[q, pl.ds(s, CHUNK, stride=ns), :].astype(BF16)
                li = li_ref[q, pl.ds(s, CHUNK, stride=ns), :].astype(BF16)
                dbre_ref[s, :, qs] += _dot(ub, lr, ((0,), (0,)))
                dbim_ref[s, :, qs] += _dot(ub, li, ((0,), (0,)))
                acc = acc + _dot(lr, bre_ref[s, :, qs], ((1,), (1,))) + _dot(li, bim_ref[s, :, qs], ((1,), (1,)))
            du_ref[:, cols] = acc

    whole3 = lambda a: pl.BlockSpec(a.shape, lambda b_, c: (0, 0, 0))
    rowblk = pl.BlockSpec((CHUNK, width), lambda b_, c: (b_ * nc + nc - 1 - c, 0))
    scr = pltpu.VMEM((S5_Q, CHUNK * ns, LANES), F32)
    return pl.pallas_call(
        body, name=name, grid=(nb, nc),
        in_specs=[rowblk, rowblk,
                  pl.BlockSpec((None, 2, S5_Q, ns, LANES), lambda b_, c: (b_ * nc + nc - 1 - c, 0, 0, 0, 0)),
                  whole3(bre), whole3(bim), whole3(cre), whole3(cim), whole3(ar), whole3(ai),
                  pl.BlockSpec((1, width), lambda b_, c: (0, 0))],
        out_specs=[rowblk, whole3(bre), whole3(bim), whole3(cre), whole3(cim), whole3(ar), whole3(ai),
                   pl.BlockSpec((1, width), lambda b_, c: (0, 0))],
        out_shape=[jax.ShapeDtypeStruct((rows, width), F32), jax.ShapeDtypeStruct(bre.shape, F32),
                   jax.ShapeDtypeStruct(bim.shape, F32), jax.ShapeDtypeStruct(cre.shape, F32),
                   jax.ShapeDtypeStruct(cim.shape, F32), jax.ShapeDtypeStruct(ar.shape, F32),
                   jax.ShapeDtypeStruct(ai.shape, F32), jax.ShapeDtypeStruct((1, width), F32)],
        scratch_shapes=[scr, scr, scr, scr, pltpu.VMEM((2, S5_Q, ns, LANES), F32), pltpu.VMEM((2, S5_Q, ns, LANES), F32)],
        compiler_params=_params(("arbitrary", "arbitrary")))(pa, dys, states, bre, bim, cre, cim, ar, ai, dvec)


def _s5_discretize(lam_re, lam_im, log_dt, b_re, b_im):
    dt = jnp.exp(log_dt)[:, None]
    mag = jnp.exp(lam_re * dt)
    ar, ai = mag * jnp.cos(lam_im * dt), mag * jnp.sin(lam_im * dt)
    den = lam_re * lam_re + lam_im * lam_im
    qr = ((ar - 1.0) * lam_re + ai * lam_im) / den
    qi = (ai * lam_re - (ar - 1.0) * lam_im) / den
    bbr = qr[..., None] * b_re - qi[..., None] * b_im
    bbi = qr[..., None] * b_im + qi[..., None] * b_re
    return ar, ai, bbr, bbi


def _s5_expand(ar, ai, bbr, bbi, c_re, c_im):
    g, p, h = bbr.shape
    gps = LANES // h
    ns = g // gps
    eye = jnp.eye(gps, dtype=F32)

    def bexp(b):
        return jnp.einsum("sgph,gk->sghkp", b.reshape(ns, gps, p, h), eye).reshape(ns, gps * h, gps * p)

    def cexp(c):
        return jnp.einsum("sghp,gk->sgpkh", c.reshape(ns, gps, h, p), eye).reshape(ns, gps * p, gps * h)

    def aexp(a):
        return a.reshape(ns, S5_Q, LANES).transpose(1, 0, 2)

    return (bexp(bbr).astype(BF16), bexp(bbi).astype(BF16), cexp(c_re).astype(BF16), cexp(c_im).astype(BF16),
            aexp(ar), aexp(ai))


def _s5_contract(dbre, dbim, dcre, dcim, dar, dai, g, p, h):
    gps = LANES // h
    ns = g // gps
    eye = jnp.eye(gps, dtype=F32)
    bcon = lambda d: jnp.einsum("sghkp,gk->sgph", d.reshape(ns, gps, h, gps, p), eye).reshape(g, p, h)
    ccon = lambda d: jnp.einsum("sgpkh,gk->sghp", d.reshape(ns, gps, p, gps, h), eye).reshape(g, h, p)
    acon = lambda d: d.transpose(1, 0, 2).reshape(g, p)
    return bcon(dbre), bcon(dbim), ccon(dcre), ccon(dcim), acon(dar), acon(dai)


def _ml_proj_tile(cpre, xb, wq, wk, wv, gq, gk, gv):
    xc = _silu(cpre)
    q = _dot_nn(xc, wq)
    k = _dot_nn(xc, wk)
    v = _dot_nn(xb, wv)
    return q, k, v, _dot_nn(q, gq) + _dot_nn(k, gk) + _dot_nn(v, gv)


def ml_proj_fwd(cpre, xb, wq, wk, wv, gq, gk, gv, name):
    rows, width = cpre.shape
    nblk = width // LANES
    tr = _tile(rows, 1088, 16)

    def body(c_ref, x_ref, wq_ref, wk_ref, wv_ref, gq_ref, gk_ref, gv_ref, q_ref, k_ref, v_ref, g_ref):
        j = pl.program_id(1)
        q, k, v, g = _ml_proj_tile(c_ref[...], x_ref[...], wq_ref[...], wk_ref[...], wv_ref[...],
                                   gq_ref[...], gk_ref[...], gv_ref[...])
        q_ref[...] = q
        k_ref[...] = k
        v_ref[...] = v

        @pl.when(j == 0)
        def _():
            g_ref[...] = jnp.zeros_like(g_ref)

        g_ref[...] += g

    rb = pl.BlockSpec((tr, LANES), lambda i, j: (i, j))
    wb = pl.BlockSpec((None, LANES, LANES), lambda i, j: (j, 0, 0))
    return pl.pallas_call(
        body, name=name, grid=(rows // tr, nblk), in_specs=[rb, rb, wb, wb, wb, wb, wb, wb],
        out_specs=[rb, rb, rb, pl.BlockSpec((tr, LANES), lambda i, j: (i, 0))],
        out_shape=[jax.ShapeDtypeStruct((rows, width), F32)] * 3 + [jax.ShapeDtypeStruct((rows, LANES), F32)],
        compiler_params=_params(("arbitrary", "arbitrary")))(cpre, xb, wq, wk, wv, gq, gk, gv)


def ml_proj_bwd(cpre, xb, wq, wk, wv, gq, gk, gv, dq, dk, dv, dg, dcp_extra, name):
    rows, width = cpre.shape
    nblk = width // LANES
    tr = _tile(rows, 1088, 16)

    def body(c_ref, x_ref, wq_ref, wk_ref, wv_ref, gq_ref, gk_ref, gv_ref, dq_ref, dk_ref, dv_ref, dg_ref, e_ref,
             dc_ref, dx_ref, *dw_refs):
        i = pl.program_id(1)
        _, vjp = jax.vjp(_ml_proj_tile, c_ref[...], x_ref[...], wq_ref[...], wk_ref[...], wv_ref[...],
                         gq_ref[...], gk_ref[...], gv_ref[...])
        grads = vjp((dq_ref[...], dk_ref[...], dv_ref[...], dg_ref[...]))
        dc_ref[...] = grads[0] + e_ref[...]
        dx_ref[...] = grads[1]

        @pl.when(i == 0)
        def _():
            for r in dw_refs:
                r[...] = jnp.zeros_like(r)

        for r, gval in zip(dw_refs, grads[2:]):
            r[...] += gval

    rb = pl.BlockSpec((tr, LANES), lambda j, i: (i, j))
    wb = pl.BlockSpec((None, LANES, LANES), lambda j, i: (j, 0, 0))
    gb = pl.BlockSpec((tr, LANES), lambda j, i: (i, 0))
    wshape = jax.ShapeDtypeStruct((nblk, LANES, LANES), F32)
    return pl.pallas_call(
        body, name=name, grid=(nblk, rows // tr), in_specs=[rb, rb, wb, wb, wb, wb, wb, wb, rb, rb, rb, gb, rb],
        out_specs=[rb, rb] + [wb] * 6,
        out_shape=[jax.ShapeDtypeStruct((rows, width), F32)] * 2 + [wshape] * 6,
        compiler_params=_params(("arbitrary", "arbitrary")))(cpre, xb, wq, wk, wv, gq, gk, gv, dq, dk, dv, dg, dcp_extra)


def _ml_gates_tile(gl, bg, nh):
    x = gl + bg
    bcum = _dot(_tri(CHUNK), _log_sigmoid(x), ((1,), (0,)), precision=HI)
    lane = lax.broadcasted_iota(jnp.int32, x.shape, 1)
    return jnp.where(lane < nh, x, jnp.where(lane < 2 * nh, bcum, 0.0))


def _ml_core_tile(q, k, v, colg, rowg, cpre, zb, nw, sk, cst, nst, m_prev):
    c, dh = q.shape
    igc, bc = _lane_pick(colg, 0), _lane_pick(colg, 1)
    igr, br = _row_pick(rowg, 0), _row_pick(rowg, 1)
    causal = _tri(c) > 0
    dmat = jnp.where(causal, bc - br + igr, -jnp.inf)
    inter = bc + m_prev
    mt = lax.stop_gradient(jnp.maximum(inter, jnp.max(dmat, axis=1, keepdims=True)))
    wt = jnp.exp(dmat - mt)
    w_prev = jnp.exp(inter - mt)
    qs = q * (dh ** -0.5)
    s = _dot_nt(qs, k) * wt
    num = _dot_nn(s, v) + w_prev * _dot_nn(qs, cst)
    den = jnp.sum(s, axis=1, keepdims=True) + w_prev * jnp.sum(qs * nst, axis=1, keepdims=True)
    h = num / jnp.maximum(jnp.abs(den), jnp.exp(-mt))
    last = (lax.broadcasted_iota(jnp.int32, (c, 1), 0) == c - 1).astype(F32)
    blast = jnp.sum(bc * last, axis=0, keepdims=True)
    g = blast - bc + igc
    m_new = lax.stop_gradient(jnp.maximum(blast + m_prev, jnp.max(g, axis=0, keepdims=True)))
    decay = jnp.exp(blast + m_prev - m_new)
    wk = jnp.exp(g - m_new) * k
    c_new = decay * cst + _dot_tn(wk, v)
    n_new = decay * nst + jnp.sum(wk, axis=0, keepdims=True)
    mu = jnp.mean(h, axis=1, keepdims=True)
    hc = h - mu
    var = jnp.mean(hc * hc, axis=1, keepdims=True)
    out = hc * lax.rsqrt(var + HEAD_NORM_EPS) * nw + sk * _silu(cpre)
    return out * _silu(zb), c_new, n_new, m_new


def _ml_core_specs(nc, dh, rev):
    ch = (lambda c: nc - 1 - c) if rev else (lambda c: c)
    rb = pl.BlockSpec((CHUNK, dh), lambda b_, c, h: (b_ * nc + ch(c), h))
    colb = pl.BlockSpec((None, CHUNK, 2), lambda b_, c, h: (h, b_ * nc + ch(c), 0))
    rowb = pl.BlockSpec((None, None, 2, CHUNK), lambda b_, c, h: (b_ * nc + ch(c), h, 0, 0))
    pb = pl.BlockSpec((1, dh), lambda b_, c, h: (0, h))
    cb = pl.BlockSpec((None, None, dh, dh), lambda b_, c, h: (b_ * nc + ch(c), h, 0, 0))
    nb_ = pl.BlockSpec((None, None, 1, dh), lambda b_, c, h: (b_ * nc + ch(c), h, 0, 0))
    mb = pl.BlockSpec((None, None, 1, 1), lambda b_, c, h: (b_ * nc + ch(c), h, 0, 0))
    return rb, colb, rowb, pb, cb, nb_, mb


def ml_core_fwd(q, k, v, colg, rowg, cpre, zb, nw, sk, nb, nh, name):
    rows, width = q.shape
    dh = width // nh
    nc = rows // nb // CHUNK
    rb, colb, rowb, pb, cb, nb_, mb = _ml_core_specs(nc, dh, False)

    def body(q_ref, k_ref, v_ref, col_ref, row_ref, c_ref, z_ref, nw_ref, sk_ref, y_ref, cs_ref, ns_ref, ms_ref,
             cst_ref, nst_ref, mst_ref):
        c = pl.program_id(1)
        h = pl.program_id(2)

        @pl.when(c == 0)
        def _():
            cst_ref[h] = jnp.zeros((dh, dh), F32)
            nst_ref[h] = jnp.zeros((1, dh), F32)
            mst_ref[h] = jnp.zeros((1, 1), F32)

        cst, nst, m_prev = cst_ref[h], nst_ref[h], mst_ref[h]
        cs_ref[...] = cst
        ns_ref[...] = nst
        ms_ref[...] = m_prev
        y, c_new, n_new, m_new = _ml_core_tile(q_ref[...], k_ref[...], v_ref[...], col_ref[...], row_ref[...],
                                               c_ref[...], z_ref[...], nw_ref[...], sk_ref[...], cst, nst, m_prev)
        y_ref[...] = y.astype(BF16)
        cst_ref[h] = c_new
        nst_ref[h] = n_new
        mst_ref[h] = m_new

    nbc = nb * nc
    return pl.pallas_call(
        body, name=name, grid=(nb, nc, nh), in_specs=[rb, rb, rb, colb, rowb, rb, rb, pb, pb],
        out_specs=[rb, cb, nb_, mb],
        out_shape=[jax.ShapeDtypeStruct((rows, width), BF16), jax.ShapeDtypeStruct((nbc, nh, dh, dh), F32),
                   jax.ShapeDtypeStruct((nbc, nh, 1, dh), F32), jax.ShapeDtypeStruct((nbc, nh, 1, 1), F32)],
        scratch_shapes=[pltpu.VMEM((nh, dh, dh), F32), pltpu.VMEM((nh, 1, dh), F32), pltpu.VMEM((nh, 1, 1), F32)],
        compiler_params=_params(("arbitrary", "arbitrary", "arbitrary")))(q, k, v, colg, rowg, cpre, zb, nw, sk)


def ml_core_bwd(q, k, v, colg, rowg, cpre, zb, nw, sk, cs, ns, ms, dy, nb, nh, name):
    rows, width = q.shape
    dh = width // nh
    nc = rows // nb // CHUNK
    rb, colb, rowb, pb, cb, nb_, mb = _ml_core_specs(nc, dh, True)

    def body(q_ref, k_ref, v_ref, col_ref, row_ref, c_ref, z_ref, nw_ref, sk_ref, cs_ref, ns_ref, ms_ref, dy_ref,
             dq_ref, dk_ref, dv_ref, dc_ref, dz_ref, dcol_ref, drow_ref, dnw_ref, dsk_ref, dcst_ref, dnst_ref):
        bb = pl.program_id(0)
        step = pl.program_id(1)
        h = pl.program_id(2)

        @pl.when(jnp.logical_and(bb == 0, jnp.logical_and(step == 0, h == 0)))
        def _():
            dnw_ref[...] = jnp.zeros_like(dnw_ref)
            dsk_ref[...] = jnp.zeros_like(dsk_ref)

        @pl.when(step == 0)
        def _():
            dcst_ref[h] = jnp.zeros((dh, dh), F32)
            dnst_ref[h] = jnp.zeros((1, dh), F32)

        m_prev = ms_ref[...]

        def f(*a):
            return _ml_core_tile(*a, m_prev)[:3]

        _, vjp = jax.vjp(f, q_ref[...], k_ref[...], v_ref[...], col_ref[...], row_ref[...], c_ref[...], z_ref[...],
                         nw_ref[...], sk_ref[...], cs_ref[...], ns_ref[...])
        g = vjp((dy_ref[...], dcst_ref[h], dnst_ref[h]))
        dq_ref[...] = g[0]
        dk_ref[...] = g[1]
        dv_ref[...] = g[2]
        dcol_ref[...] = g[3]
        drow_ref[...] = g[4]
        dc_ref[...] = g[5]
        dz_ref[...] = g[6]
        dnw_ref[h] += g[7]
        dsk_ref[h] += g[8]
        dcst_ref[h] = g[9]
        dnst_ref[h] = g[10]

    nbc = nb * nc
    accb = pl.BlockSpec((nh, 1, dh), lambda b_, c, h: (0, 0, 0))
    return pl.pallas_call(
        body, name=name, grid=(nb, nc, nh), in_specs=[rb, rb, rb, colb, rowb, rb, rb, pb, pb, cb, nb_, mb, rb],
        out_specs=[rb, rb, rb, rb, rb, colb, rowb, accb, accb],
        out_shape=[jax.ShapeDtypeStruct((rows, width), F32)] * 5
        + [jax.ShapeDtypeStruct(colg.shape, F32), jax.ShapeDtypeStruct(rowg.shape, F32),
           jax.ShapeDtypeStruct((nh, 1, dh), F32), jax.ShapeDtypeStruct((nh, 1, dh), F32)],
        scratch_shapes=[pltpu.VMEM((nh, dh, dh), F32), pltpu.VMEM((nh, 1, dh), F32)],
        compiler_params=_params(("arbitrary", "arbitrary", "arbitrary")))(
            q, k, v, colg, rowg, cpre, zb, nw, sk, cs, ns, ms, dy)


def _ssd_dt_tile(dtr, bias, alog):
    dt = _softplus(dtr + bias)
    cum = _dot(_tri(CHUNK), dt * (-jnp.exp(alog)), ((1,), (0,)), precision=HI)
    return dt, cum


def _ssd_tile(xcs, bmc, cmc, cols, rows_, z, dvec, gn, states, hpg):
    npair = hpg // 2
    hd = SSD_HEAD_DIM
    xs = [_silu(x) for x in xcs]
    bm, cm = _silu(bmc), _silu(cmc)
    cb = _dot_nt(cm, bm)
    causal = _tri(CHUNK) > 0
    lane_lo = lax.broadcasted_iota(jnp.int32, (1, 2 * hd), 1) < hd
    row_lo = lax.broadcasted_iota(jnp.int32, (2 * hd, 1), 0) < hd
    lastsel = (lax.broadcasted_iota(jnp.int32, (CHUNK, 1), 0) == CHUNK - 1).astype(F32)
    heads = []
    for r in range(hpg):
        dtc, cumc = _lane_pick(cols, r), _lane_pick(cols, hpg + r)
        dtrow, cumr = _row_pick(rows_, r), _row_pick(rows_, hpg + r)
        w = cb * jnp.exp(jnp.where(causal, cumc - cumr, -jnp.inf)) * dtrow
        last = jnp.sum(cumc * lastsel, axis=0, keepdims=True)
        heads.append((w, jnp.exp(cumc), jnp.exp(last - cumc) * dtc, jnp.exp(last)))
    ys, new_states = [], []
    for j in range(npair):
        (wa, ea, da, la), (wb, eb, db, lb) = heads[2 * j], heads[2 * j + 1]
        yi = jnp.where(lane_lo, _dot_nn(wa, xs[j]), _dot_nn(wb, xs[j]))
        yst = jnp.where(lane_lo, ea, eb) * _dot_nt(cm, states[j])
        ys.append(yi + yst)
        xd = xs[j] * jnp.where(lane_lo, da, db)
        new_states.append(jnp.where(row_lo, la, lb) * states[j] + _dot_tn(xd, bm))
    y = jnp.concatenate(ys, axis=1) + dvec * jnp.concatenate(xs, axis=1)
    yg = y * _silu(z)
    yn = yg * lax.rsqrt(jnp.mean(yg * yg, axis=1, keepdims=True) + NORM_EPS) * gn
    return yn, new_states


def _ssd_specs(nc, hpg, ng, rev):
    npair = hpg // 2
    gw = hpg * SSD_HEAD_DIM
    xblocks = ng * npair
    ch = (lambda c: nc - 1 - c) if rev else (lambda c: c)
    xs = [pl.BlockSpec((CHUNK, LANES), functools.partial(lambda b_, c, g, jj: (b_ * nc + ch(c), g * npair + jj), jj=j))
          for j in range(npair)]
    bmb = pl.BlockSpec((CHUNK, SSD_STATE), lambda b_, c, g: (b_ * nc + ch(c), xblocks + g))
    cmb = pl.BlockSpec((CHUNK, SSD_STATE), lambda b_, c, g: (b_ * nc + ch(c), xblocks + ng + g))
    colb = pl.BlockSpec((None, CHUNK, 2 * hpg), lambda b_, c, g: (g, b_ * nc + ch(c), 0))
    rowb = pl.BlockSpec((None, None, 2 * hpg, CHUNK), lambda b_, c, g: (b_ * nc + ch(c), g, 0, 0))
    zb = pl.BlockSpec((CHUNK, gw), lambda b_, c, g: (b_ * nc + ch(c), g))
    pb = pl.BlockSpec((1, gw), lambda b_, c, g: (0, g))
    sb = pl.BlockSpec((None, None, npair, 2 * SSD_HEAD_DIM, SSD_STATE), lambda b_, c, g: (b_ * nc + ch(c), g, 0, 0, 0))
    return xs, bmb, cmb, colb, rowb, zb, pb, sb


def ssd_core_fwd(cpre, cols, rows_, z, dvec, gn, nb, hpg, name):
    rows = cpre.shape[0]
    inner = z.shape[1]
    ng = inner // (hpg * SSD_HEAD_DIM)
    npair = hpg // 2
    nc = rows // nb // CHUNK
    xs, bmb, cmb, colb, rowb, zb, pb, sb = _ssd_specs(nc, hpg, ng, False)

    def body(*refs):
        x_refs = refs[:npair]
        bm_ref, cm_ref, col_ref, row_ref, z_ref, d_ref, gn_ref, y_ref, so_ref, st_ref = refs[npair:]
        c = pl.program_id(1)
        g = pl.program_id(2)

        @pl.when(c == 0)
        def _():
            st_ref[g] = jnp.zeros((npair, 2 * SSD_HEAD_DIM, SSD_STATE), F32)

        so_ref[...] = st_ref[g]
        states = [st_ref[g, j] for j in range(npair)]
        yn, new_states = _ssd_tile([r[...] for r in x_refs], bm_ref[...], cm_ref[...], col_ref[...], row_ref[...],
                                   z_ref[...], d_ref[...], gn_ref[...], states, hpg)
        y_ref[...] = yn.astype(BF16)
        for j in range(npair):
            st_ref[g, j] = new_states[j]

    return pl.pallas_call(
        body, name=name, grid=(nb, nc, ng), in_specs=xs + [bmb, cmb, colb, rowb, zb, pb, pb],
        out_specs=[zb, sb],
        out_shape=[jax.ShapeDtypeStruct((rows, inner), BF16),
                   jax.ShapeDtypeStruct((nb * nc, ng, npair, 2 * SSD_HEAD_DIM, SSD_STATE), F32)],
        scratch_shapes=[pltpu.VMEM((ng, npair, 2 * SSD_HEAD_DIM, SSD_STATE), F32)],
        compiler_params=_params(("arbitrary", "arbitrary", "arbitrary")))(
            *([cpre] * npair), cpre, cpre, cols, rows_, z, dvec, gn)


def ssd_core_bwd(cpre, cols, rows_, z, dvec, gn, states, dyn, nb, hpg, name):
    rows = cpre.shape[0]
    inner = z.shape[1]
    gw = hpg * SSD_HEAD_DIM
    ng = inner // gw
    npair = hpg // 2
    nc = rows // nb // CHUNK
    xs, bmb, cmb, colb, rowb, zb, pb, sb = _ssd_specs(nc, hpg, ng, True)

    def body(*refs):
        x_refs = refs[:npair]
        (bm_ref, cm_ref, col_ref, row_ref, z_ref, d_ref, gn_ref, s_ref, dy_ref,
         dx_ref, dbm_ref, dcm_ref, dcol_ref, drow_ref, dz_ref, dd_ref, dgn_ref, dst_ref) = refs[npair:]
        bb = pl.program_id(0)
        step = pl.program_id(1)
        g = pl.program_id(2)

        @pl.when(jnp.logical_and(bb == 0, jnp.logical_and(step == 0, g == 0)))
        def _():
            dd_ref[...] = jnp.zeros_like(dd_ref)
            dgn_ref[...] = jnp.zeros_like(dgn_ref)

        @pl.when(step == 0)
        def _():
            dst_ref[g] = jnp.zeros((npair, 2 * SSD_HEAD_DIM, SSD_STATE), F32)

        def f(xcs, bmc, cmc, cv, rv, zv, dv_, gv, sts):
            return _ssd_tile(xcs, bmc, cmc, cv, rv, zv, dv_, gv, sts, hpg)

        _, vjp = jax.vjp(f, [r[...] for r in x_refs], bm_ref[...], cm_ref[...], col_ref[...], row_ref[...], z_ref[...],
                         d_ref[...], gn_ref[...], [s_ref[j] for j in range(npair)])
        gr = vjp((dy_ref[...], [dst_ref[g, j] for j in range(npair)]))
        dx_ref[...] = jnp.concatenate(gr[0], axis=1)
        dbm_ref[...] = gr[1]
        dcm_ref[...] = gr[2]
        dcol_ref[...] = gr[3]
        drow_ref[...] = gr[4]
        dz_ref[...] = gr[5]
        dd_ref[g] += gr[6]
        dgn_ref[g] += gr[7]
        for j in range(npair):
            dst_ref[g, j] = gr[8][j]

    ch = lambda c: nc - 1 - c
    nblk = pl.BlockSpec((CHUNK, SSD_STATE), lambda b_, c, g: (b_ * nc + ch(c), g))
    accb = pl.BlockSpec((ng, 1, gw), lambda b_, c, g: (0, 0, 0))
    return pl.pallas_call(
        body, name=name, grid=(nb, nc, ng), in_specs=xs + [bmb, cmb, colb, rowb, zb, pb, pb, sb, zb],
        out_specs=[zb, nblk, nblk, colb, rowb, zb, accb, accb],
        out_shape=[jax.ShapeDtypeStruct((rows, inner), F32), jax.ShapeDtypeStruct((rows, ng * SSD_STATE), F32),
                   jax.ShapeDtypeStruct((rows, ng * SSD_STATE), F32), jax.ShapeDtypeStruct(cols.shape, F32),
                   jax.ShapeDtypeStruct(rows_.shape, F32), jax.ShapeDtypeStruct((rows, inner), F32),
                   jax.ShapeDtypeStruct((ng, 1, gw), F32), jax.ShapeDtypeStruct((ng, 1, gw), F32)],
        scratch_shapes=[pltpu.VMEM((ng, npair, 2 * SSD_HEAD_DIM, SSD_STATE), F32)],
        compiler_params=_params(("arbitrary", "arbitrary", "arbitrary")))(
            *([cpre] * npair), cpre, cpre, cols, rows_, z, dvec, gn, states, dyn)


def _hw_expand(w):
    n, bi, _ = w.shape
    per = LANES // bi
    eye = jnp.eye(per, dtype=F32)
    return jnp.einsum("jbio,bc->jbico", w.reshape(n // per, per, bi, bi), eye).reshape(n // per, LANES, LANES)


def _hw_contract(d, bi=QKV_BLOCK):
    per = LANES // bi
    eye = jnp.eye(per, dtype=F32)
    return jnp.einsum("jbico,bc->jbio", d.reshape(d.shape[0], per, bi, per, bi), eye).reshape(-1, bi, bi)


def _wg_expand(wg, width):
    pad = jnp.pad(wg, ((0, 0), (0, LANES - wg.shape[1])))
    return [pad[i * width:(i + 1) * width].reshape(width // LANES, LANES, LANES) for i in range(3)]


def _wg_contract(dgs, ngate):
    return jnp.concatenate([d[:, :, :ngate].reshape(-1, ngate) for d in dgs], axis=0)


def _pad_lanes(a):
    return jnp.pad(a, ((0, 0), (0, LANES - a.shape[1])))


def _pairs_to_layouts(first, second, ngrp, per, nbc):
    rows = first.shape[0]
    both = jnp.concatenate([first.reshape(rows, ngrp, per), second.reshape(rows, ngrp, per)], axis=2)
    return both.transpose(1, 0, 2), both.reshape(nbc, CHUNK, ngrp, 2 * per).transpose(0, 2, 3, 1)


def _layouts_to_pairs(dcols, drows, ngrp, per):
    rows = dcols.shape[1]
    both = dcols.transpose(1, 0, 2) + drows.transpose(0, 3, 1, 2).reshape(rows, ngrp, 2 * per)
    return both[:, :, :per].reshape(rows, ngrp * per), both[:, :, per:].reshape(rows, ngrp * per)


def _local_step(x, target, bw, sp):
    nb, seq, d = x.shape
    nh, hpg = MLSTM_HEADS, SSD_HPG
    t_len = N_META + seq
    nc = -(-t_len // CHUNK)
    tp = nc * CHUNK
    rows = nb * tp
    nbc = nb * nc
    meta = sp["meta_tokens"]
    h0 = jnp.concatenate([jnp.broadcast_to(meta[None], (nb, N_META, d)), x, jnp.zeros((nb, tp - t_len, d), F32)], axis=1)
    h0 = h0.reshape(rows, d)
    tgt = jnp.pad(target, ((0, 0), (N_META, tp - t_len), (0, 0))).reshape(rows, d)

    n0 = norm_fwd(h0, sp["ab_norm"], "norm0")
    pa = mm(n0, bw["W0a"], "nn", "mm_pa")
    xb = mm(n0, bw["W0xb"], "nn", "mm_xb")
    zb = mm(n0, bw["W0zb"], "nn", "mm_zb")
    s5w = pa.shape[1] // 2
    mlw = xb.shape[1]
    s5_args = (sp["s5_lambda_re"], sp["s5_lambda_im"], sp["s5_log_dt"].reshape(-1), sp["s5_b_re"], sp["s5_b_im"])
    (ar, ai, bbr, bbi), s5_disc_vjp = jax.vjp(_s5_discretize, *s5_args)
    sg, spn, shh = bbr.shape
    bre, bim, cre, cim, are, aie = _s5_expand(ar, ai, bbr, bbi, sp["s5_c_re"], sp["s5_c_im"])
    ys5, gb, s5st = s5_fwd(pa, bre, bim, cre, cim, are, aie, sp["s5_d"], nb, "s5_fwd")
    tglu = mm(gb, bw["glu"], "nn", "mm_glu")

    def glu_tile(ys, tt, za, gbias):
        return _gelu(ys) * _sigmoid(tt + gbias) * _silu(za)

    ya = rowwise("glu_fwd", lambda i, ys, tt, pab, gbias: glu_tile(ys, tt, pab[:, s5w:], gbias),
                 [ys5, tglu, pa], [sp["s5_glu_b"]], [(s5w, BF16)], tr=_tile(rows, 256, 16))[0]

    cpre0 = conv_fwd(xb, sp["ml_conv_w"], sp["ml_conv_b"], nb, "ml_conv_fwd")
    wq_e, wk_e, wv_e = _hw_expand(sp["ml_wq"]), _hw_expand(sp["ml_wk"]), _hw_expand(sp["ml_wv"])
    gq, gk, gv = _wg_expand(sp["ml_w_gate"], mlw)
    q, k, v, gl = ml_proj_fwd(cpre0, xb, wq_e, wk_e, wv_e, gq, gk, gv, "ml_proj_fwd")
    bgate = _pad_lanes(sp["ml_b_gate"])
    gout = rowwise("ml_gates_fwd", lambda i, g_, b_: _ml_gates_tile(g_, b_, nh), [gl], [bgate], [(LANES, F32)], tr=CHUNK)[0]
    colg, rowg = _pairs_to_layouts(gout[:, :nh], gout[:, nh:2 * nh], nh, 1, nbc)
    yb, ml_cs, ml_ns, ml_ms = ml_core_fwd(q, k, v, colg, rowg, cpre0, zb, sp["ml_norm"], sp["ml_skip"], nb, nh, "ml_core_fwd")
    h1 = mm(ya, bw["Wo0a"], "nn", "mm_out0a", resid=h0)
    h1 = mm(yb, bw["Wo0b"], "nn", "mm_out0b", resid=h1)

    n1 = norm_fwd(h1, sp["ssd_norm"], "norm1")
    z1 = mm(n1, bw["W1z"], "nn", "mm_z1")
    xbc = mm(n1, bw["W1x"], "nn", "mm_xbc")
    dtr = mm(n1, bw["W1dt"], "nn", "mm_dt")
    inner = z1.shape[1]
    ng = inner // (hpg * SSD_HEAD_DIM)
    nhd = ng * hpg
    cpre1 = conv_fwd(xbc, sp["ssd_conv_w"], sp["ssd_conv_b"], nb, "ssd_conv_fwd")
    dt_bias, a_log = _pad_lanes(sp["ssd_dt_bias"]), _pad_lanes(sp["ssd_a_log"])
    dt, cum = rowwise("ssd_dt_fwd", lambda i, r_, b_, a_: _ssd_dt_tile(r_, b_, a_), [dtr], [dt_bias, a_log],
                      [(LANES, F32), (LANES, F32)], tr=CHUNK)
    cols, rws = _pairs_to_layouts(dt[:, :nhd], cum[:, :nhd], ng, hpg, nbc)
    dvec = jnp.repeat(sp["ssd_d"], SSD_HEAD_DIM, axis=1)
    yn, ssd_st = ssd_core_fwd(cpre1, cols, rws, z1, dvec, sp["ssd_gnorm"], nb, hpg, "ssd_core_fwd")
    h2 = mm(yn, bw["Wo1"], "nn", "mm_out1", resid=h1)

    tr_l = _tile(tp, 256, 16)
    per_ex = tp // tr_l

    def loss_tile(i, hb, tb, gfn):
        tpos = (i % per_ex) * tr_l + lax.broadcasted_iota(jnp.int32, (tr_l, 1), 0)
        mask = jnp.logical_and(tpos >= N_META, tpos < t_len).astype(F32)

        def lf(hh, gg):
            e = (_rms(hh, gg) - tb) * mask
            return 0.5 * jnp.sum(e * e) / d

        lval, (dh, dg) = jax.value_and_grad(lf, (0, 1))(hb, gfn)
        return dh, jnp.full((1, LANES), lval, F32), dg

    fn = sp["final_norm"].reshape(1, d)
    dh2, loss_acc, dfn = rowwise("loss", loss_tile, [h2, tgt], [fn], [(d, F32)], [(1, LANES), (1, d)], tr=tr_l)

    gbig, gs = {}, {}
    gs["final_norm"] = dfn.reshape(sp["final_norm"].shape)
    dyn = mm(dh2, bw["Wo1"], "nt", "mm_dyn")
    gbig["Wo1"] = mm(yn, dh2, "tn", "mm_dWo1", out_dtype=BF16)
    dxs, dbm, dcm, dcols, drws, dz1, ddvec, dgn = ssd_core_bwd(cpre1, cols, rws, z1, dvec, sp["ssd_gnorm"], ssd_st, dyn,
                                                              nb, hpg, "ssd_core_bwd")
    gs["ssd_d"] = ddvec.reshape(1, nhd, SSD_HEAD_DIM).sum(axis=2)
    gs["ssd_gnorm"] = dgn.reshape(1, inner)
    ddt, dcum = _layouts_to_pairs(dcols, drws, ng, hpg)

    def ssd_dt_bwd_tile(i, r_, ddt_, dcum_, b_, a_):
        _, vjp = jax.vjp(_ssd_dt_tile, r_, b_, a_)
        return vjp((ddt_, dcum_))

    ddtr, dbias, dalog = rowwise("ssd_dt_bwd", ssd_dt_bwd_tile, [dtr, _pad_lanes(ddt), _pad_lanes(dcum)], [dt_bias, a_log],
                                 [(LANES, F32)], [(1, LANES), (1, LANES)], tr=CHUNK)
    gs["ssd_dt_bias"] = dbias[:, :nhd]
    gs["ssd_a_log"] = dalog[:, :nhd]
    dcpre1 = jnp.concatenate([dxs, dbm, dcm], axis=1)
    dxbc, dcw1, dcb1 = conv_bwd(dcpre1, xbc, sp["ssd_conv_w"], nb, "ssd_conv_bwd")
    gs["ssd_conv_w"] = dcw1[:sp["ssd_conv_w"].shape[0]]
    gs["ssd_conv_b"] = dcb1
    dn1 = mm(dz1, bw["W1z"], "nt", "mm_dn1z")
    dn1 = mm(dxbc, bw["W1x"], "nt", "mm_dn1x", resid=dn1)
    dn1 = mm(ddtr, bw["W1dt"], "nt", "mm_dn1dt", resid=dn1)
    gbig["W1z"] = mm(n1, dz1, "tn", "mm_dW1z", out_dtype=BF16)
    gbig["W1x"] = mm(n1, dxbc, "tn", "mm_dW1x", out_dtype=BF16)
    gbig["W1dt"] = mm(n1, ddtr, "tn", "mm_dW1dt", out_dtype=BF16)
    dh1, dg1 = norm_bwd(h1, sp["ssd_norm"], dn1, dh2, "norm1_bwd")
    gs["ssd_norm"] = dg1

    dya = mm(dh1, bw["Wo0a"], "nt", "mm_dya")
    dyb = mm(dh1, bw["Wo0b"], "nt", "mm_dyb")
    gbig["Wo0a"] = mm(ya, dh1, "tn", "mm_dWo0a", out_dtype=BF16)
    gbig["Wo0b"] = mm(yb, dh1, "tn", "mm_dWo0b", out_dtype=BF16)
    (dq, dk, dv, dcp_skip, dzb, dcolg, drowg, dnw, dsk) = ml_core_bwd(
        q, k, v, colg, rowg, cpre0, zb, sp["ml_norm"], sp["ml_skip"], ml_cs, ml_ns, ml_ms, dyb, nb, nh, "ml_core_bwd")
    gs["ml_norm"] = dnw.reshape(1, mlw)
    gs["ml_skip"] = dsk.reshape(1, mlw)
    dig, dbcum = _layouts_to_pairs(dcolg, drowg, nh, 1)
    dgout = _pad_lanes(jnp.concatenate([dig, dbcum], axis=1))

    def ml_gates_bwd_tile(i, g_, dgo, b_):
        _, vjp = jax.vjp(lambda a, b: _ml_gates_tile(a, b, nh), g_, b_)
        return vjp(dgo)

    dgl, dbg = rowwise("ml_gates_bwd", ml_gates_bwd_tile, [gl, dgout], [bgate], [(LANES, F32)], [(1, LANES)], tr=CHUNK)
    gs["ml_b_gate"] = dbg[:, :2 * nh]
    dcpre0, dxb_v, dwq, dwk, dwv, dgq, dgk, dgv = ml_proj_bwd(cpre0, xb, wq_e, wk_e, wv_e, gq, gk, gv, dq, dk, dv, dgl,
                                                            dcp_skip, "ml_proj_bwd")
    gs["ml_wq"], gs["ml_wk"], gs["ml_wv"] = _hw_contract(dwq), _hw_contract(dwk), _hw_contract(dwv)
    gs["ml_w_gate"] = _wg_contract([dgq, dgk, dgv], 2 * nh)
    dxb, dcw0, dcb0 = conv_bwd(dcpre0, xb, sp["ml_conv_w"], nb, "ml_conv_bwd", resid=dxb_v)
    gs["ml_conv_w"] = dcw0[:sp["ml_conv_w"].shape[0]]
    gs["ml_conv_b"] = dcb0

    def glu_bwd_tile(i, ys, tt, pab, dy_, gbias):
        _, vjp = jax.vjp(glu_tile, ys, tt, pab[:, s5w:], gbias)
        return vjp(dy_)

    dys_direct, dtglu, dza, dglub = rowwise("glu_bwd", glu_bwd_tile, [ys5, tglu, pa, dya], [sp["s5_glu_b"]],
                                            [(s5w, F32)] * 3, [(1, s5w)], tr=_tile(rows, 256, 16))
    gs["s5_glu_b"] = dglub
    dgb = mm(dtglu, bw["glu"], "nt", "mm_dgb")
    gbig["glu"] = mm(gb, dtglu, "tn", "mm_dglu", out_dtype=BF16)

    def gelu_bwd_tile(i, ys, dg_, direct):
        _, vjp = jax.vjp(_gelu, ys)
        return vjp(dg_)[0] + direct

    dys5 = rowwise("gelu_bwd", gelu_bwd_tile, [ys5, dgb, dys_direct], [], [(s5w, F32)], tr=_tile(rows, 256, 16))[0]
    du, dbre, dbim, dcre, dcim, dare, daie, dd5 = s5_bwd(pa, dys5, s5st, bre, bim, cre, cim, are, aie, sp["s5_d"], nb, "s5_bwd")
    gs["s5_d"] = dd5
    dbbr, dbbi, dcr, dci, dar, dai = _s5_contract(dbre, dbim, dcre, dcim, dare, daie, sg, spn, shh)
    gs["s5_c_re"], gs["s5_c_im"] = dcr, dci
    (gs["s5_lambda_re"], gs["s5_lambda_im"], dlogdt, gs["s5_b_re"], gs["s5_b_im"]) = s5_disc_vjp((dar, dai, dbbr, dbbi))
    gs["s5_log_dt"] = dlogdt.reshape(1, -1)
    dpa = jnp.concatenate([du, dza], axis=1)
    dn0 = mm(dpa, bw["W0a"], "nt", "mm_dn0a")
    dn0 = mm(dxb, bw["W0xb"], "nt", "mm_dn0xb", resid=dn0)
    dn0 = mm(dzb, bw["W0zb"], "nt", "mm_dn0zb", resid=dn0)
    gbig["W0a"] = mm(n0, dpa, "tn", "mm_dW0a", out_dtype=BF16)
    gbig["W0xb"] = mm(n0, dxb, "tn", "mm_dW0xb", out_dtype=BF16)
    gbig["W0zb"] = mm(n0, dzb, "tn", "mm_dW0zb", out_dtype=BF16)
    dh0, dg0 = norm_bwd(h0, sp["ab_norm"], dn0, dh1, "norm0_bwd")
    gs["ab_norm"] = dg0
    dh0 = dh0.reshape(nb, tp, d)
    gs["meta_tokens"] = jnp.sum(dh0[:, :N_META], axis=0)
    return loss_acc[0, 0], dh0, gbig, gs


N_DEV = 8
N_CHIP = 4
MESH = pl.DeviceIdType.MESH
_HBM = pl.BlockSpec(memory_space=pltpu.HBM)


def _place():
    x, y, c = lax.axis_index("x"), lax.axis_index("y"), lax.axis_index("c")
    return x, y, c, [(1 - x, y), (x, 1 - y), (1 - x, 1 - y)]


def all_gather8(v, name):
    m_per, n = v.shape

    def body(x_ref, out_ref, send_sems, recv_sems, local_sem):
        x, y, c, chips = _place()
        me, sibling = (x, y, c), (x, y, 1 - c)

        def rows(px, py, pc):
            return out_ref.at[pl.ds((4 * px + 2 * py + pc) * m_per, m_per), :]

        def copy(kk, block, to, src=None):
            return pltpu.make_async_remote_copy(
                src_ref=rows(*block) if src is None else src, dst_ref=rows(*block), send_sem=send_sems.at[kk],
                recv_sem=recv_sems.at[kk], device_id=to, device_id_type=MESH)

        mine = pltpu.make_async_copy(x_ref, rows(*me), local_sem)
        mine.start()
        first = [copy(0, me, sibling, src=x_ref)]
        first += [copy(1 + j, me, (*chip, c), src=x_ref) for j, chip in enumerate(chips)]
        for cp in first:
            cp.start()
        passed = [copy(4 + j, (*chip, c), sibling) for j, chip in enumerate(chips)]
        for j, chip in enumerate(chips):
            copy(1 + j, (*chip, c), me).wait_recv()
            passed[j].start()
        copy(0, sibling, me).wait_recv()
        for j, chip in enumerate(chips):
            copy(4 + j, (*chip, 1 - c), me).wait_recv()
        for cp in first + passed:
            cp.wait_send()
        mine.wait()

    return pl.pallas_call(
        body, name=name, out_shape=jax.ShapeDtypeStruct((N_DEV * m_per, n), v.dtype),
        in_specs=[pl.BlockSpec(memory_space=pltpu.VMEM)], out_specs=pl.BlockSpec(memory_space=pltpu.VMEM),
        scratch_shapes=[pltpu.SemaphoreType.DMA((7,)), pltpu.SemaphoreType.DMA((7,)), pltpu.SemaphoreType.DMA],
        compiler_params=pltpu.CompilerParams(vmem_limit_bytes=VMEM_LIMIT))(v)


def gather_chips(vs, name):
    na = len(vs)

    def body(*refs):
        x_refs, out_refs = refs[:na], refs[na:2 * na]
        send_sems, recv_sems, local_sems = refs[2 * na:]
        x, y, c, chips = _place()
        k = 2 * x + y
        sibling = (x, y, 1 - c)

        def copy(i, kk, src, chip_k, half, to):
            return pltpu.make_async_remote_copy(
                src_ref=src, dst_ref=out_refs[i].at[chip_k, half], send_sem=send_sems.at[6 * i + kk],
                recv_sem=recv_sems.at[6 * i + kk], device_id=to, device_id_type=MESH)

        mine = [pltpu.make_async_copy(x_refs[i], out_refs[i].at[k], local_sems.at[i]) for i in range(na)]
        for cp in mine:
            cp.start()
        first = [copy(i, j, x_refs[i].at[c], k, c, (*chip, c)) for j, chip in enumerate(chips) for i in range(na)]
        for cp in first:
            cp.start()
        passed = []
        for j, (cx, cy) in enumerate(chips):
            kj = 2 * cx + cy
            for i in range(na):
                copy(i, j, out_refs[i].at[kj, c], kj, c, (cx, cy, c)).wait_recv()
                fwd = copy(i, 3 + j, out_refs[i].at[kj, c], kj, c, sibling)
                fwd.start()
                passed.append(fwd)
        for j, (cx, cy) in enumerate(chips):
            kj = 2 * cx + cy
            for i in range(na):
                copy(i, 3 + j, out_refs[i].at[kj, 1 - c], kj, 1 - c, sibling).wait_recv()
        for cp in first + passed:
            cp.wait_send()
        for cp in mine:
            cp.wait()

    return pl.pallas_call(
        body, name=name, out_shape=[jax.ShapeDtypeStruct((N_CHIP,) + v.shape, v.dtype) for v in vs],
        in_specs=[_HBM] * na, out_specs=[_HBM] * na,
        scratch_shapes=[pltpu.SemaphoreType.DMA((6 * na,)), pltpu.SemaphoreType.DMA((6 * na,)),
                        pltpu.SemaphoreType.DMA((na,))])(*vs)


def scatter_chips(ps, name):
    na = len(ps)

    def body(*refs):
        p_refs, out_refs = refs[:na], refs[na:2 * na]
        send_sems, recv_sems, local_sems = refs[2 * na:]
        x, y, c, chips = _place()
        k = 2 * x + y
        sibling = (x, y, 1 - c)

        def copy(i, kk, src, chip_k, half, to):
            return pltpu.make_async_remote_copy(
                src_ref=src, dst_ref=out_refs[i].at[chip_k, half], send_sem=send_sems.at[7 * i + kk],
                recv_sem=recv_sems.at[7 * i + kk], device_id=to, device_id_type=MESH)

        mine = [pltpu.make_async_copy(p_refs[i].at[k], out_refs[i].at[k, c], local_sems.at[i]) for i in range(na)]
        for cp in mine:
            cp.start()
        first = [copy(i, 1 + j, p_refs[i].at[2 * cx + cy], k, c, (cx, cy, c))
                 for j, (cx, cy) in enumerate(chips) for i in range(na)]
        first += [copy(i, 0, p_refs[i].at[k], k, c, sibling) for i in range(na)]
        for cp in first:
            cp.start()
        passed = []
        for j, (cx, cy) in enumerate(chips):
            kj = 2 * cx + cy
            for i in range(na):
                copy(i, 1 + j, out_refs[i].at[kj, c], kj, c, (cx, cy, c)).wait_recv()
                fwd = copy(i, 4 + j, out_refs[i].at[kj, c], kj, c, sibling)
                fwd.start()
                passed.append(fwd)
        for i in range(na):
            copy(i, 0, out_refs[i].at[k, 1 - c], k, 1 - c, sibling).wait_recv()
        for j, (cx, cy) in enumerate(chips):
            kj = 2 * cx + cy
            for i in range(na):
                copy(i, 4 + j, out_refs[i].at[kj, 1 - c], kj, 1 - c, sibling).wait_recv()
        for cp in first + passed:
            cp.wait_send()
        for cp in mine:
            cp.wait()

    return pl.pallas_call(
        body, name=name, out_shape=[jax.ShapeDtypeStruct((N_CHIP, 2) + p.shape[1:], p.dtype) for p in ps],
        in_specs=[_HBM] * na, out_specs=[_HBM] * na,
        scratch_shapes=[pltpu.SemaphoreType.DMA((7 * na,)), pltpu.SemaphoreType.DMA((7 * na,)),
                        pltpu.SemaphoreType.DMA((na,))])(*ps)


def swap_halves(gs_, name):
    na = len(gs_)

    def body(*refs):
        g_refs, out_refs = refs[:na], refs[na:2 * na]
        send_sems, recv_sems = refs[2 * na:]
        x, y, c, _ = _place()
        cps = [pltpu.make_async_remote_copy(
            src_ref=g_refs[i].at[kk, 1 - c], dst_ref=out_refs[i].at[kk], send_sem=send_sems.at[N_CHIP * i + kk],
            recv_sem=recv_sems.at[N_CHIP * i + kk], device_id=(x, y, 1 - c), device_id_type=MESH)
            for i in range(na) for kk in range(N_CHIP)]
        for cp in cps:
            cp.start()
        for cp in cps:
            cp.wait()

    return pl.pallas_call(
        body, name=name, out_shape=[jax.ShapeDtypeStruct((N_CHIP,) + g.shape[2:], g.dtype) for g in gs_],
        in_specs=[_HBM] * na, out_specs=[_HBM] * na,
        scratch_shapes=[pltpu.SemaphoreType.DMA((N_CHIP * na,)), pltpu.SemaphoreType.DMA((N_CHIP * na,))])(*gs_)


def add_halves(g, other, core, name):
    _, _, m, n = g.shape
    tr = _tile(m, 256, 16)

    def body(core_ref, g_ref, o_ref, out_ref):
        out_ref[...] = (g_ref[...].astype(F32) + o_ref[...].astype(F32)).astype(out_ref.dtype)

    grid_spec = pltpu.PrefetchScalarGridSpec(
        num_scalar_prefetch=1, grid=(N_CHIP, m // tr),
        in_specs=[pl.BlockSpec((None, None, tr, n), lambda kk, i, core_ref: (kk, core_ref[0], i, 0)),
                  pl.BlockSpec((None, tr, n), lambda kk, i, core_ref: (kk, i, 0))],
        out_specs=pl.BlockSpec((None, tr, n), lambda kk, i, core_ref: (kk, i, 0)))
    return pl.pallas_call(body, name=name, grid_spec=grid_spec, out_shape=jax.ShapeDtypeStruct((N_CHIP, m, n), g.dtype),
                          compiler_params=_params(("arbitrary", "arbitrary")))(core.reshape(1).astype(jnp.int32), g, other)


PACK_LANES = 512


def _pack(arrs, dtype, lanes, row_align):
    flat = jnp.concatenate([a.reshape(-1).astype(dtype) for a in arrs])
    unit = lanes * row_align
    total = -(-flat.shape[0] // unit) * unit
    return jnp.pad(flat, (0, total - flat.shape[0])).reshape(total // lanes, lanes)


def _unpack(flat, shapes):
    flat = flat.reshape(-1)
    out, off = [], 0
    for s in shapes:
        n = math.prod(s)
        out.append(flat[off:off + n].reshape(s))
        off += n
    return out


def _adam_tile(w, m, v, g):
    m2 = ADAM_B1 * m + (1.0 - ADAM_B1) * g
    v2 = ADAM_B2 * v + (1.0 - ADAM_B2) * (g * g)
    m_hat = m2 / (1.0 - ADAM_B1 ** ADAM_STEP)
    v_hat = v2 / (1.0 - ADAM_B2 ** ADAM_STEP)
    delta = -ADAM_LR * (m_hat / (jnp.sqrt(v_hat) + ADAM_EPS) + ADAM_WD * w)
    return delta, m2, v2


def adam_big(w, m, v, pieces, name):
    _, r, c = w.shape
    tr = _tile(r, 128, 16)

    def body(w_ref, m_ref, v_ref, p0, p1, p2, p3, g_ref, d_ref, mo_ref, vo_ref):
        g = ((p0[...].astype(F32) + p1[...].astype(F32)) + p2[...].astype(F32)) + p3[...].astype(F32)
        delta, m2, v2 = _adam_tile(w_ref[...], m_ref[...], v_ref[...], g)
        g_ref[...] = g
        d_ref[...] = delta
        mo_ref[...] = m2
        vo_ref[...] = v2

    wspec = pl.BlockSpec((None, tr, c), lambda i: (0, i, 0))
    pspecs = [pl.BlockSpec((None, tr, c), functools.partial(lambda i, kk: (kk, i, 0), kk=kk)) for kk in range(N_CHIP)]
    return pl.pallas_call(
        body, name=name, grid=(r // tr,), in_specs=[wspec] * 3 + pspecs, out_specs=[wspec] * 4,
        out_shape=[jax.ShapeDtypeStruct(w.shape, F32)] * 4, compiler_params=_params(("parallel",)))(
            w, m, v, pieces, pieces, pieces, pieces)


_WEIGHTS = (
    ("meta_tokens", "small", 1), ("ab_norm", "small", None), ("ab_w_in", "big", 2), ("s5_lambda_re", "small", None),
    ("s5_lambda_im", "small", None), ("s5_log_dt", "small", None), ("s5_b_re", "small", None), ("s5_b_im", "small", None),
    ("s5_c_re", "small", None), ("s5_c_im", "small", None), ("s5_d", "small", None), ("s5_glu_w", "big", 1),
    ("s5_glu_b", "small", None), ("ml_conv_w", "small", 2), ("ml_conv_b", "small", None), ("ml_wq", "small", 1),
    ("ml_wk", "small", 1), ("ml_wv", "small", 1), ("ml_w_gate", "small", 1), ("ml_b_gate", "small", None),
    ("ml_norm", "small", None), ("ml_skip", "small", None), ("ab_w_out", "big", 1), ("ssd_norm", "small", 1),
    ("ssd_w_in", "big", 2), ("ssd_conv_w", "small", 2), ("ssd_conv_b", "small", 1), ("ssd_dt_bias", "small", None),
    ("ssd_a_log", "small", None), ("ssd_d", "small", None), ("ssd_gnorm", "small", 1), ("ssd_w_out", "big", 1),
    ("final_norm", "small", None),
)


def _squeeze(a):
    return a[0] if a.ndim >= 3 else a


def kernel(x, meta_tokens, ab_norm, ab_w_in, s5_lambda_re, s5_lambda_im, s5_log_dt, s5_b_re, s5_b_im, s5_c_re, s5_c_im, s5_d, s5_glu_w, s5_glu_b, ml_conv_w, ml_conv_b, ml_wq, ml_wk, ml_wv, ml_w_gate, ml_b_gate, ml_norm, ml_skip, ab_w_out, ssd_norm, ssd_w_in, ssd_conv_w, ssd_conv_b, ssd_dt_bias, ssd_a_log, ssd_d, ssd_gnorm, ssd_w_out, final_norm, loss_target, m_meta_tokens, m_ab_norm, m_ab_w_in, m_s5_lambda_re, m_s5_lambda_im, m_s5_log_dt, m_s5_b_re, m_s5_b_im, m_s5_c_re, m_s5_c_im, m_s5_d, m_s5_glu_w, m_s5_glu_b, m_ml_conv_w, m_ml_conv_b, m_ml_wq, m_ml_wk, m_ml_wv, m_ml_w_gate, m_ml_b_gate, m_ml_norm, m_ml_skip, m_ab_w_out, m_ssd_norm, m_ssd_w_in, m_ssd_conv_w, m_ssd_conv_b, m_ssd_dt_bias, m_ssd_a_log, m_ssd_d, m_ssd_gnorm, m_ssd_w_out, m_final_norm, v_meta_tokens, v_ab_norm, v_ab_w_in, v_s5_lambda_re, v_s5_lambda_im, v_s5_log_dt, v_s5_b_re, v_s5_b_im, v_s5_c_re, v_s5_c_im, v_s5_d, v_s5_glu_w, v_s5_glu_b, v_ml_conv_w, v_ml_conv_b, v_ml_wq, v_ml_wk, v_ml_wv, v_ml_w_gate, v_ml_b_gate, v_ml_norm, v_ml_skip, v_ab_w_out, v_ssd_norm, v_ssd_w_in, v_ssd_conv_w, v_ssd_conv_b, v_ssd_dt_bias, v_ssd_a_log, v_ssd_d, v_ssd_gnorm, v_ssd_w_out, v_final_norm):
    args = (meta_tokens, ab_norm, ab_w_in, s5_lambda_re, s5_lambda_im, s5_log_dt, s5_b_re, s5_b_im, s5_c_re, s5_c_im, s5_d, s5_glu_w, s5_glu_b, ml_conv_w, ml_conv_b, ml_wq, ml_wk, ml_wv, ml_w_gate, ml_b_gate, ml_norm, ml_skip, ab_w_out, ssd_norm, ssd_w_in, ssd_conv_w, ssd_conv_b, ssd_dt_bias, ssd_a_log, ssd_d, ssd_gnorm, ssd_w_out, final_norm)
    m_args = (m_meta_tokens, m_ab_norm, m_ab_w_in, m_s5_lambda_re, m_s5_lambda_im, m_s5_log_dt, m_s5_b_re, m_s5_b_im, m_s5_c_re, m_s5_c_im, m_s5_d, m_s5_glu_w, m_s5_glu_b, m_ml_conv_w, m_ml_conv_b, m_ml_wq, m_ml_wk, m_ml_wv, m_ml_w_gate, m_ml_b_gate, m_ml_norm, m_ml_skip, m_ab_w_out, m_ssd_norm, m_ssd_w_in, m_ssd_conv_w, m_ssd_conv_b, m_ssd_dt_bias, m_ssd_a_log, m_ssd_d, m_ssd_gnorm, m_ssd_w_out, m_final_norm)
    v_args = (v_meta_tokens, v_ab_norm, v_ab_w_in, v_s5_lambda_re, v_s5_lambda_im, v_s5_log_dt, v_s5_b_re, v_s5_b_im, v_s5_c_re, v_s5_c_im, v_s5_d, v_s5_glu_w, v_s5_glu_b, v_ml_conv_w, v_ml_conv_b, v_ml_wq, v_ml_wk, v_ml_wv, v_ml_w_gate, v_ml_b_gate, v_ml_norm, v_ml_skip, v_ab_w_out, v_ssd_norm, v_ssd_w_in, v_ssd_conv_w, v_ssd_conv_b, v_ssd_dt_bias, v_ssd_a_log, v_ssd_d, v_ssd_gnorm, v_ssd_w_out, v_final_norm)
    names = [w[0] for w in _WEIGHTS]
    kind = {w[0]: w[1] for w in _WEIGHTS}
    axis = {w[0]: w[2] for w in _WEIGHTS}
    w_loc = dict(zip(names, args))
    m_loc = dict(zip(names, m_args))
    v_loc = dict(zip(names, v_args))
    chip = 2 * lax.axis_index("x") + lax.axis_index("y")
    core = lax.axis_index("c")
    big = [n for n in names if kind[n] == "big"]
    small = [n for n in names if kind[n] == "small"]
    small_sh = [n for n in small if axis[n] is not None]

    def halves(a):
        return a.astype(BF16).reshape(2, a.shape[1] // 2, a.shape[2])

    gathered = gather_chips([halves(w_loc[n]) for n in big], "gather_big_w")
    full_big = {}
    for n, gth in zip(big, gathered):
        shard = gth.reshape((N_CHIP,) + w_loc[n].shape[1:])
        if axis[n] == 1:
            full_big[n] = shard.reshape(-1, shard.shape[2])
        else:
            full_big[n] = jnp.concatenate([shard[kk] for kk in range(N_CHIP)], axis=1)
    small_sh_shapes = [w_loc[n].shape for n in small_sh]
    packed_s = _pack([w_loc[n] for n in small_sh], F32, LANES, SUBLANES)
    g8 = all_gather8(packed_s, "gather_small_w").reshape(N_CHIP, 2, -1)
    sp = {}
    for n in small:
        if axis[n] is None:
            sp[n] = _squeeze(w_loc[n])
    per_chip = [_unpack(g8[kk, 0], small_sh_shapes) for kk in range(N_CHIP)]
    for i, n in enumerate(small_sh):
        sp[n] = _squeeze(jnp.concatenate([per_chip[kk][i] for kk in range(N_CHIP)], axis=axis[n]))

    s5w = full_big["s5_glu_w"].shape[0]
    mlw = full_big["ab_w_out"].shape[0] - s5w
    inner = full_big["ssd_w_out"].shape[0]
    w_in0, w_in1 = full_big["ab_w_in"], full_big["ssd_w_in"]
    n_heads1 = sp["ssd_d"].shape[1]
    cdim = w_in1.shape[1] - inner - n_heads1
    bw = dict(W0a=w_in0[:, :2 * s5w], W0xb=w_in0[:, 2 * s5w:2 * s5w + mlw], W0zb=w_in0[:, 2 * s5w + mlw:],
              glu=full_big["s5_glu_w"], Wo0a=full_big["ab_w_out"][:s5w], Wo0b=full_big["ab_w_out"][s5w:],
              W1z=w_in1[:, :inner], W1x=w_in1[:, inner:inner + cdim], W1dt=_pad_lanes(w_in1[:, inner + cdim:]),
              Wo1=full_big["ssd_w_out"])

    loss_local, dh0, gbig, gs = _local_step(x, loss_target, bw, sp)
    loss = lax.psum(loss_local, ("x", "y", "c"))
    grad_x = dh0[:, N_META:N_META + x.shape[1]]

    gfull = {
        "ab_w_in": jnp.concatenate([gbig["W0a"], gbig["W0xb"], gbig["W0zb"]], axis=1),
        "s5_glu_w": gbig["glu"],
        "ab_w_out": jnp.concatenate([gbig["Wo0a"], gbig["Wo0b"]], axis=0),
        "ssd_w_in": jnp.concatenate([gbig["W1z"], gbig["W1x"], gbig["W1dt"][:, :n_heads1]], axis=1),
        "ssd_w_out": gbig["Wo1"],
    }
    gps = []
    for n in big:
        _, r, c_ = w_loc[n].shape
        gf = gfull[n]
        if axis[n] == 1:
            gp = gf.reshape(N_CHIP, 2, r // 2, c_)
        else:
            gp = jnp.stack([gf[:, kk * c_:(kk + 1) * c_] for kk in range(N_CHIP)]).reshape(N_CHIP, 2, r // 2, c_)
        gps.append(gp)
    from_sibling = swap_halves(gps, "swap_big_g")
    partials = [add_halves(gp, oth, core, "add_" + n) for n, gp, oth in zip(big, gps, from_sibling)]
    pieces = scatter_chips(partials, "scatter_big_g")

    out_g, out_d, out_m, out_v = {}, {}, {}, {}
    for n, pc in zip(big, pieces):
        pcs = pc.reshape((N_CHIP,) + w_loc[n].shape[1:])
        out_g[n], out_d[n], out_m[n], out_v[n] = adam_big(w_loc[n], m_loc[n], v_loc[n], pcs, "adam_" + n)

    small_full_shapes = [sp[n].shape for n in small]
    packed_gs = _pack([gs[n] for n in small], F32, LANES, SUBLANES)
    rows_s = packed_gs.shape[0]
    all_gs = all_gather8(packed_gs, "gather_small_g")
    blocks = [all_gs[i * rows_s:(i + 1) * rows_s] for i in range(N_DEV)]

    def sum8(i, *b):
        acc = b[0]
        for t in b[1:]:
            acc = acc + t
        return acc

    gsum = rowwise("sum_small_g", sum8, blocks, [], [(LANES, F32)], tr=_tile(rows_s, 512, 8))[0]
    g_small = dict(zip(small, _unpack(gsum, small_full_shapes)))
    g_loc = {}
    for n in small:
        g = g_small[n].reshape((1,) + g_small[n].shape) if w_loc[n].ndim >= 3 else g_small[n]
        if axis[n] is not None:
            size = w_loc[n].shape[axis[n]]
            g = lax.dynamic_slice_in_dim(g, chip * size, size, axis=axis[n])
        g_loc[n] = g.reshape(w_loc[n].shape)
    loc_shapes = [w_loc[n].shape for n in small]
    pw, pm, pv, pg = (_pack([d[n] for n in small], F32, LANES, SUBLANES) for d in (w_loc, m_loc, v_loc, g_loc))
    dl, mn, vn = rowwise("adam_small", lambda i, a, b, c_, d_: _adam_tile(a, b, c_, d_), [pw, pm, pv, pg], [],
                         [(LANES, F32)] * 3, tr=_tile(pw.shape[0], 512, 8))
    for d_out, flat in ((out_d, dl), (out_m, mn), (out_v, vn)):
        for n, a in zip(small, _unpack(flat, loc_shapes)):
            d_out[n] = a
    for n in small:
        out_g[n] = g_loc[n]

    return (loss, grad_x, *[out_g[n] for n in names], *[out_d[n] for n in names], *[out_m[n] for n in names],
            *[out_v[n] for n in names])
```

```python
import functools
import math

import jax
import jax.numpy as jnp
from jax import lax
from jax.experimental import pallas as pl
from jax.experimental.pallas import tpu as pltpu

F32 = jnp.float32
BF16 = jnp.bfloat16
HI = lax.Precision.HIGHEST

D_MODEL = 2048
SEQ = 2048
N_META = 16
CHUNK = 128
NORM_EPS = 1e-6
HEAD_NORM_EPS = 1e-5
S5_GROUP_SIZE = 16
S5_STATE = 64
MLSTM_HEADS = 8
QKV_BLOCK = 4
SSD_HEAD_DIM = 64
SSD_STATE = 128
SSD_HPG = 8
ADAM_LR = 0.001
ADAM_B1 = 0.9
ADAM_B2 = 0.999
ADAM_EPS = 1e-08
ADAM_WD = 0.01
ADAM_STEP = 10

LANES = 128
SUBLANES = 8
VMEM_LIMIT = 56 * 1024 * 1024
MM_OPERAND_VMEM = 34 * 1024 * 1024


def _sigmoid(x):
    return 1.0 / (1.0 + jnp.exp(-x))


def _silu(x):
    return x * _sigmoid(x)


def _softplus(x):
    return jnp.maximum(x, 0.0) + jnp.log(1.0 + jnp.exp(-jnp.abs(x)))


def _log_sigmoid(x):
    return jnp.minimum(x, 0.0) - jnp.log(1.0 + jnp.exp(-jnp.abs(x)))


def _gelu(x):
    return 0.5 * x * (1.0 + jnp.tanh(math.sqrt(2.0 / math.pi) * (x + 0.044715 * (x * x * x))))


def _dot(a, b, dims, precision=None):
    return lax.dot_general(a, b, (dims, ((), ())), preferred_element_type=F32, precision=precision)


def _dot_nn(a, b):
    return _dot(a.astype(BF16), b.astype(BF16), ((1,), (0,)))


def _dot_nt(a, b):
    return _dot(a.astype(BF16), b.astype(BF16), ((1,), (1,)))


def _dot_tn(a, b):
    return _dot(a.astype(BF16), b.astype(BF16), ((0,), (0,)))


def _lane_pick(a, idx):
    sel = (lax.broadcasted_iota(jnp.int32, (1, a.shape[1]), 1) == idx).astype(a.dtype)
    return jnp.sum(a * sel, axis=1, keepdims=True)


def _row_pick(a, idx):
    sel = (lax.broadcasted_iota(jnp.int32, (a.shape[0], 1), 0) == idx).astype(a.dtype)
    return jnp.sum(a * sel, axis=0, keepdims=True)


def _tri(n, upper=False):
    r = lax.broadcasted_iota(jnp.int32, (n, n), 0)
    c = lax.broadcasted_iota(jnp.int32, (n, n), 1)
    return ((r <= c) if upper else (r >= c)).astype(F32)


def _tile(n, target, align):
    if n <= target:
        return n
    t = (target // align) * align
    while t >= align:
        if n % t == 0:
            return t
        t -= align
    return n


def _params(sem=None):
    return pltpu.CompilerParams(dimension_semantics=sem, vmem_limit_bytes=VMEM_LIMIT)


def mm(a, b, mode, name, resid=None, out_dtype=F32):
    if mode == "nn":
        (m, k), (k2, n) = a.shape, b.shape
    elif mode == "nt":
        (m, k), (n, k2) = a.shape, b.shape
    else:
        (k, m), (k2, n) = a.shape, b.shape
    assert k == k2, (a.shape, b.shape, mode)
    a_sz, b_sz = a.dtype.itemsize, b.dtype.itemsize
    if mode == "tn":
        tm, tn = _tile(m, 1024, LANES), _tile(n, 1024, LANES)
        tk = _tile(k, MM_OPERAND_VMEM // (2 * (tm * a_sz + tn * b_sz)), 16)
    else:
        tm, tn = _tile(m, 1088, 16), _tile(n, 512, LANES)
        tk = _tile(k, MM_OPERAND_VMEM // (2 * (tm * a_sz + tn * b_sz)), LANES)
    nk = k // tk
    dims = {"nn": ((1,), (0,)), "nt": ((1,), (1,)), "tn": ((0,), (0,))}[mode]
    has_resid = resid is not None

    def body(*refs):
        if has_resid:
            a_ref, b_ref, r_ref, o_ref = refs[:4]
        else:
            a_ref, b_ref, o_ref = refs[:3]
        part = _dot(a_ref[...].astype(BF16), b_ref[...].astype(BF16), dims)

        def finish(res):
            if has_resid:
                res = res + r_ref[...].astype(F32)
            o_ref[...] = res.astype(o_ref.dtype)

        if nk == 1:
            finish(part)
            return
        acc_ref = refs[-1]
        kk = pl.program_id(2)

        @pl.when(kk == 0)
        def _():
            acc_ref[...] = part

        @pl.when(jnp.logical_and(kk > 0, kk < nk - 1))
        def _():
            acc_ref[...] += part

        @pl.when(kk == nk - 1)
        def _():
            finish(acc_ref[...] + part)

    if mode == "tn":
        a_spec = pl.BlockSpec((tk, tm), lambda i, j, kk: (kk, i))
    else:
        a_spec = pl.BlockSpec((tm, tk), lambda i, j, kk: (i, kk))
    if mode == "nt":
        b_spec = pl.BlockSpec((tn, tk), lambda i, j, kk: (j, kk))
    else:
        b_spec = pl.BlockSpec((tk, tn), lambda i, j, kk: (kk, j))
    o_spec = pl.BlockSpec((tm, tn), lambda i, j, kk: (i, j))
    in_specs = [a_spec, b_spec] + ([o_spec] if has_resid else [])
    args = (a, b) + ((resid,) if has_resid else ())
    return pl.pallas_call(
        body, name=name, grid=(m // tm, n // tn, nk), in_specs=in_specs, out_specs=o_spec,
        out_shape=jax.ShapeDtypeStruct((m, n), out_dtype), scratch_shapes=[pltpu.VMEM((tm, tn), F32)] if nk > 1 else [],
        compiler_params=_params(("parallel", "parallel", "arbitrary")))(*args)


def rowwise(name, f, rows, params, outs, accs=(), tr=128):
    n_rows = rows[0].shape[0]
    assert n_rows % tr == 0
    n_r, n_p, n_o, n_a = len(rows), len(params), len(outs), len(accs)

    def body(*refs):
        i = pl.program_id(0)
        r_vals = [r[...] for r in refs[:n_r]]
        p_vals = [r[...] for r in refs[n_r:n_r + n_p]]
        o_refs = refs[n_r + n_p:n_r + n_p + n_o]
        a_refs = refs[n_r + n_p + n_o:]
        res = f(i, *r_vals, *p_vals)
        if not isinstance(res, (tuple, list)):
            res = (res,)
        assert len(res) == n_o + n_a, (name, len(res))
        for o_ref, val in zip(o_refs, res[:n_o]):
            o_ref[...] = val.astype(o_ref.dtype)
        if n_a:
            @pl.when(i == 0)
            def _():
                for a_ref in a_refs:
                    a_ref[...] = jnp.zeros_like(a_ref)

            for a_ref, val in zip(a_refs, res[n_o:]):
                a_ref[...] += val.astype(F32)

    in_specs = [pl.BlockSpec((tr, r.shape[1]), lambda i: (i, 0)) for r in rows]
    in_specs += [pl.BlockSpec(p.shape, lambda i: (0, 0)) for p in params]
    out_specs = [pl.BlockSpec((tr, w), lambda i: (i, 0)) for w, _ in outs]
    out_specs += [pl.BlockSpec(s, lambda i: (0, 0)) for s in accs]
    out_shape = [jax.ShapeDtypeStruct((n_rows, w), dt) for w, dt in outs]
    out_shape += [jax.ShapeDtypeStruct(s, F32) for s in accs]
    res = pl.pallas_call(
        body, name=name, grid=(n_rows // tr,), in_specs=in_specs, out_specs=out_specs, out_shape=out_shape,
        compiler_params=_params(("arbitrary",)))(*rows, *params)
    return res


def _rms(x, g, eps=NORM_EPS):
    return x * lax.rsqrt(jnp.mean(x * x, axis=-1, keepdims=True) + eps) * g


def norm_fwd(x, g, name):
    return rowwise(name, lambda i, xb, gb: _rms(xb, gb), [x], [g], [(x.shape[1], BF16)], tr=_tile(x.shape[0], 256, 16))[0]


def norm_bwd(x, g, dn, resid, name):
    def f(i, xb, dnb, rb, gb):
        _, vjp = jax.vjp(_rms, xb, gb)
        dx, dg = vjp(dnb)
        return dx + rb, dx + rb, dg

    return rowwise(name, f, [x, dn, resid], [g], [(x.shape[1], F32), (x.shape[1], BF16)], [g.shape],
                   tr=_tile(x.shape[0], 256, 16))


def conv_fwd(x, w, b, nb, name):
    rows, width = x.shape
    nc = rows // nb // CHUNK
    tw = _tile(width, 1024, LANES)
    ksz = w.shape[0]

    def body(x_ref, w_ref, b_ref, o_ref, ext_ref):
        c = pl.program_id(2)

        @pl.when(c == 0)
        def _():
            ext_ref[0:SUBLANES, :] = jnp.zeros((SUBLANES, tw), F32)

        xv = x_ref[...]
        ext_ref[SUBLANES:SUBLANES + CHUNK, :] = xv
        acc = jnp.broadcast_to(b_ref[...], (CHUNK, tw))
        for j in range(ksz):
            off = SUBLANES - (ksz - 1) + j
            acc = acc + w_ref[j:j + 1, :] * ext_ref[off:off + CHUNK, :]
        o_ref[...] = acc
        ext_ref[0:SUBLANES, :] = xv[CHUNK - SUBLANES:CHUNK, :]

    return pl.pallas_call(
        body, name=name, grid=(width // tw, nb, nc),
        in_specs=[pl.BlockSpec((CHUNK, tw), lambda j, bb, c: (bb * nc + c, j)),
                  pl.BlockSpec((ksz, tw), lambda j, bb, c: (0, j)),
                  pl.BlockSpec((1, tw), lambda j, bb, c: (0, j))],
        out_specs=pl.BlockSpec((CHUNK, tw), lambda j, bb, c: (bb * nc + c, j)),
        out_shape=jax.ShapeDtypeStruct((rows, width), F32),
        scratch_shapes=[pltpu.VMEM((CHUNK + 2 * SUBLANES, tw), F32)],
        compiler_params=_params(("arbitrary", "arbitrary", "arbitrary")))(x, w, b)


def conv_bwd(dc, x, w, nb, name, resid=None, dx_dtype=BF16):
    rows, width = x.shape
    nc = rows // nb // CHUNK
    tw = _tile(width, 1024, LANES)
    ksz = w.shape[0]
    per = CHUNK // SUBLANES
    has_resid = resid is not None

    def body(*refs):
        if has_resid:
            dc_ref, x_ref, halo_ref, w_ref, r_ref, dx_ref, dw_ref, db_ref, extd_ref, extx_ref = refs
        else:
            dc_ref, x_ref, halo_ref, w_ref, dx_ref, dw_ref, db_ref, extd_ref, extx_ref = refs
        bb = pl.program_id(1)
        step = pl.program_id(2)
        c = nc - 1 - step

        @pl.when(jnp.logical_and(bb == 0, step == 0))
        def _():
            dw_ref[...] = jnp.zeros_like(dw_ref)
            db_ref[...] = jnp.zeros_like(db_ref)

        @pl.when(step == 0)
        def _():
            extd_ref[CHUNK:CHUNK + SUBLANES, :] = jnp.zeros((SUBLANES, tw), F32)

        dcv = dc_ref[...]
        extd_ref[0:CHUNK, :] = dcv
        extx_ref[0:SUBLANES, :] = jnp.where(c == 0, 0.0, halo_ref[...])
        extx_ref[SUBLANES:SUBLANES + CHUNK, :] = x_ref[...]
        dx = jnp.zeros((CHUNK, tw), F32)
        for j in range(ksz):
            up = ksz - 1 - j
            dx = dx + w_ref[j:j + 1, :] * extd_ref[up:up + CHUNK, :]
            off = SUBLANES - (ksz - 1) + j
            dw_ref[j:j + 1, :] += jnp.sum(dcv * extx_ref[off:off + CHUNK, :], axis=0, keepdims=True)
        if has_resid:
            dx = dx + r_ref[...]
        dx_ref[...] = dx.astype(dx_ref.dtype)
        db_ref[...] += jnp.sum(dcv, axis=0, keepdims=True)
        extd_ref[CHUNK:CHUNK + SUBLANES, :] = dcv[0:SUBLANES, :]

    def blk(j, bb, step):
        return (bb * nc + nc - 1 - step, j)

    def halo(j, bb, step):
        return (jnp.maximum((bb * nc + nc - 1 - step) * per - 1, 0), j)

    in_specs = [pl.BlockSpec((CHUNK, tw), blk), pl.BlockSpec((CHUNK, tw), blk), pl.BlockSpec((SUBLANES, tw), halo),
                pl.BlockSpec((ksz, tw), lambda j, bb, step: (0, j))]
    args = [dc, x, x, w]
    if has_resid:
        in_specs.append(pl.BlockSpec((CHUNK, tw), blk))
        args.append(resid)
    return pl.pallas_call(
        body, name=name, grid=(width // tw, nb, nc), in_specs=in_specs,
        out_specs=[pl.BlockSpec((CHUNK, tw), blk), pl.BlockSpec((SUBLANES, tw), lambda j, bb, step: (0, j)),
                   pl.BlockSpec((1, tw), lambda j, bb, step: (0, j))],
        out_shape=[jax.ShapeDtypeStruct((rows, width), dx_dtype), jax.ShapeDtypeStruct((SUBLANES, width), F32),
                   jax.ShapeDtypeStruct((1, width), F32)],
        scratch_shapes=[pltpu.VMEM((CHUNK + 2 * SUBLANES, tw), F32), pltpu.VMEM((CHUNK + 2 * SUBLANES, tw), F32)],
        compiler_params=_params(("arbitrary", "arbitrary", "arbitrary")))(*args)


S5_Q = 4


def _s5_fill_bu(u, bre_ref, bim_ref, xr_ref, xi_ref, ns):
    for s in range(ns):
        ub = u[:, s * LANES:(s + 1) * LANES].astype(BF16)
        bur = _dot(ub, bre_ref[s], ((1,), (0,)))
        bui = _dot(ub, bim_ref[s], ((1,), (0,)))
        for q in range(S5_Q):
            xr_ref[q, pl.ds(s, CHUNK, stride=ns), :] = bur[:, q * LANES:(q + 1) * LANES]
            xi_ref[q, pl.ds(s, CHUNK, stride=ns), :] = bui[:, q * LANES:(q + 1) * LANES]


def _s5_scan(xr_ref, xi_ref, ar_ref, ai_ref, st_ref, ns):
    ar = [ar_ref[q] for q in range(S5_Q)]
    ai = [ai_ref[q] for q in range(S5_Q)]

    def step(t, carry):
        rows = pl.ds(pl.multiple_of(t * ns, ns), ns)
        out = []
        for q in range(S5_Q):
            pr, pi_ = carry[2 * q], carry[2 * q + 1]
            nr = ar[q] * pr - ai[q] * pi_ + xr_ref[q, rows, :]
            ni = ar[q] * pi_ + ai[q] * pr + xi_ref[q, rows, :]
            xr_ref[q, rows, :] = nr
            xi_ref[q, rows, :] = ni
            out += [nr, ni]
        return tuple(out)

    init = []
    for q in range(S5_Q):
        init += [st_ref[0, q], st_ref[1, q]]
    fin = lax.fori_loop(0, CHUNK, step, tuple(init), unroll=2)
    for q in range(S5_Q):
        st_ref[0, q] = fin[2 * q]
        st_ref[1, q] = fin[2 * q + 1]


def s5_fwd(pa, bre, bim, cre, cim, ar, ai, dvec, nb, name):
    rows = pa.shape[0]
    width = pa.shape[1] // 2
    ns = width // LANES
    nc = rows // nb // CHUNK

    def body(u_ref, bre_ref, bim_ref, cre_ref, cim_ref, ar_ref, ai_ref, d_ref, y_ref, g_ref, so_ref, xr_ref, xi_ref, st_ref):
        c = pl.program_id(1)

        @pl.when(c == 0)
        def _():
            st_ref[...] = jnp.zeros_like(st_ref)

        so_ref[...] = st_ref[...]
        u = u_ref[...]
        _s5_fill_bu(u, bre_ref, bim_ref, xr_ref, xi_ref, ns)
        _s5_scan(xr_ref, xi_ref, ar_ref, ai_ref, st_ref, ns)
        for s in range(ns):
            acc = jnp.zeros((CHUNK, LANES), F32)
            for q in range(S5_Q):
                xr = xr_ref[q, pl.ds(s, CHUNK, stride=ns), :].astype(BF16)
                xi = xi_ref[q, pl.ds(s, CHUNK, stride=ns), :].astype(BF16)
                acc = acc + _dot(xr, cre_ref[s, q * LANES:(q + 1) * LANES, :], ((1,), (0,)))
                acc = acc - _dot(xi, cim_ref[s, q * LANES:(q + 1) * LANES, :], ((1,), (0,)))
            cols = slice(s * LANES, (s + 1) * LANES)
            y = acc + d_ref[:, cols] * u[:, cols]
            y_ref[:, cols] = y
            g_ref[:, cols] = _gelu(y).astype(BF16)

    whole3 = lambda a: pl.BlockSpec(a.shape, lambda b_, c: (0, 0, 0))
    return pl.pallas_call(
        body, name=name, grid=(nb, nc),
        in_specs=[pl.BlockSpec((CHUNK, width), lambda b_, c: (b_ * nc + c, 0)), whole3(bre), whole3(bim), whole3(cre),
                  whole3(cim), whole3(ar), whole3(ai), pl.BlockSpec((1, width), lambda b_, c: (0, 0))],
        out_specs=[pl.BlockSpec((CHUNK, width), lambda b_, c: (b_ * nc + c, 0)),
                   pl.BlockSpec((CHUNK, width), lambda b_, c: (b_ * nc + c, 0)),
                   pl.BlockSpec((None, 2, S5_Q, ns, LANES), lambda b_, c: (b_ * nc + c, 0, 0, 0, 0))],
        out_shape=[jax.ShapeDtypeStruct((rows, width), F32), jax.ShapeDtypeStruct((rows, width), BF16),
                   jax.ShapeDtypeStruct((nb * nc, 2, S5_Q, ns, LANES), F32)],
        scratch_shapes=[pltpu.VMEM((S5_Q, CHUNK * ns, LANES), F32), pltpu.VMEM((S5_Q, CHUNK * ns, LANES), F32),
                        pltpu.VMEM((2, S5_Q, ns, LANES), F32)],
        compiler_params=_params(("arbitrary", "arbitrary")))(pa, bre, bim, cre, cim, ar, ai, dvec)


def s5_bwd(pa, dys, states, bre, bim, cre, cim, ar, ai, dvec, nb, name):
    rows = pa.shape[0]
    width = pa.shape[1] // 2
    ns = width // LANES
    nc = rows // nb // CHUNK

    def body(u_ref, dy_ref, sin_ref, bre_ref, bim_ref, cre_ref, cim_ref, ar_ref, ai_ref, d_ref,
             du_ref, dbre_ref, dbim_ref, dcre_ref, dcim_ref, dar_ref, dai_ref, dd_ref,
             xr_ref, xi_ref, lr_ref, li_ref, st_ref, lam_ref):
        bb = pl.program_id(0)
        step_i = pl.program_id(1)

        @pl.when(jnp.logical_and(bb == 0, step_i == 0))
        def _():
            for r in (dbre_ref, dbim_ref, dcre_ref, dcim_ref, dar_ref, dai_ref, dd_ref):
                r[...] = jnp.zeros_like(r)

        @pl.when(step_i == 0)
        def _():
            lam_ref[...] = jnp.zeros_like(lam_ref)

        u = u_ref[...]
        dy = dy_ref[...]
        st_ref[...] = sin_ref[...]
        _s5_fill_bu(u, bre_ref, bim_ref, xr_ref, xi_ref, ns)
        _s5_scan(xr_ref, xi_ref, ar_ref, ai_ref, st_ref, ns)
        dd_ref[...] += jnp.sum(dy * u, axis=0, keepdims=True)
        for s in range(ns):
            dyb = dy[:, s * LANES:(s + 1) * LANES].astype(BF16)
            gr = _dot(dyb, cre_ref[s], ((1,), (1,)))
            gi = -_dot(dyb, cim_ref[s], ((1,), (1,)))
            for q in range(S5_Q):
                lr_ref[q, pl.ds(s, CHUNK, stride=ns), :] = gr[:, q * LANES:(q + 1) * LANES]
                li_ref[q, pl.ds(s, CHUNK, stride=ns), :] = gi[:, q * LANES:(q + 1) * LANES]
                xr = xr_ref[q, pl.ds(s, CHUNK, stride=ns), :].astype(BF16)
                xi = xi_ref[q, pl.ds(s, CHUNK, stride=ns), :].astype(BF16)
                dcre_ref[s, q * LANES:(q + 1) * LANES, :] += _dot(xr, dyb, ((0,), (0,)))
                dcim_ref[s, q * LANES:(q + 1) * LANES, :] -= _dot(xi, dyb, ((0,), (0,)))
        ar = [ar_ref[q] for q in range(S5_Q)]
        ai = [ai_ref[q] for q in range(S5_Q)]

        def one(t_rows, p_r, p_i, carry):
            out = []
            for q in range(S5_Q):
                l_r, l_i, da_r, da_i = carry[4 * q:4 * q + 4]
                n_r = lr_ref[q, t_rows, :] + ar[q] * l_r + ai[q] * l_i
                n_i = li_ref[q, t_rows, :] + ar[q] * l_i - ai[q] * l_r
                lr_ref[q, t_rows, :] = n_r
                li_ref[q, t_rows, :] = n_i
                xpr, xpi = p_r(q), p_i(q)
                out += [n_r, n_i, da_r + n_r * xpr + n_i * xpi, da_i + n_i * xpr - n_r * xpi]
            return tuple(out)

        def step(k, carry):
            t = CHUNK - 1 - k
            t_rows = pl.ds(pl.multiple_of(t * ns, ns), ns)
            p_rows = pl.ds(pl.multiple_of((t - 1) * ns, ns), ns)
            return one(t_rows, lambda q: xr_ref[q, p_rows, :], lambda q: xi_ref[q, p_rows, :], carry)

        init = []
        zero = jnp.zeros((ns, LANES), F32)
        for q in range(S5_Q):
            init += [lam_ref[0, q], lam_ref[1, q], zero, zero]
        carry = lax.fori_loop(0, CHUNK - 1, step, tuple(init), unroll=2)
        carry = one(pl.ds(0, ns), lambda q: sin_ref[0, q], lambda q: sin_ref[1, q], carry)
        for q in range(S5_Q):
            lam_ref[0, q] = carry[4 * q]
            lam_ref[1, q] = carry[4 * q + 1]
            dar_ref[q] += carry[4 * q + 2]
            dai_ref[q] += carry[4 * q + 3]
        for s in range(ns):
            cols = slice(s * LANES, (s + 1) * LANES)
            ub = u[:, cols].astype(BF16)
            acc = d_ref[:, cols] * dy[:, cols]
            for q in range(S5_Q):
                qs = slice(q * LANES, (q + 1) * LANES)
                lr = lr_ref[q, pl.ds(s, CHUNK, stride=ns), :].astype(BF16)
                li = li_ref[q, pl.ds(s, CHUNK, stride=ns), :].astype(BF16)
                dbre_ref[s, :, qs] += _dot(ub, lr, ((0,), (0,)))
                dbim_ref[s, :, qs] += _dot(ub, li, ((0,), (0,)))
                acc = acc + _dot(lr, bre_ref[s, :, qs], ((1,), (1,))) + _dot(li, bim_ref[s, :, qs], ((1,), (1,)))
            du_ref[:, cols] = acc.astype(du_ref.dtype)

    whole3 = lambda a: pl.BlockSpec(a.shape, lambda b_, c: (0, 0, 0))
    rowblk = pl.BlockSpec((CHUNK, width), lambda b_, c: (b_ * nc + nc - 1 - c, 0))
    scr = pltpu.VMEM((S5_Q, CHUNK * ns, LANES), F32)
    return pl.pallas_call(
        body, name=name, grid=(nb, nc),
        in_specs=[rowblk, rowblk,
                  pl.BlockSpec((None, 2, S5_Q, ns, LANES), lambda b_, c: (b_ * nc + nc - 1 - c, 0, 0, 0, 0)),
                  whole3(bre), whole3(bim), whole3(cre), whole3(cim), whole3(ar), whole3(ai),
                  pl.BlockSpec((1, width), lambda b_, c: (0, 0))],
        out_specs=[rowblk, whole3(bre), whole3(bim), whole3(cre), whole3(cim), whole3(ar), whole3(ai),
                   pl.BlockSpec((1, width), lambda b_, c: (0, 0))],
        out_shape=[jax.ShapeDtypeStruct((rows, width), BF16), jax.ShapeDtypeStruct(bre.shape, F32),
                   jax.ShapeDtypeStruct(bim.shape, F32), jax.ShapeDtypeStruct(cre.shape, F32),
                   jax.ShapeDtypeStruct(cim.shape, F32), jax.ShapeDtypeStruct(ar.shape, F32),
                   jax.ShapeDtypeStruct(ai.shape, F32), jax.ShapeDtypeStruct((1, width), F32)],
        scratch_shapes=[scr, scr, scr, scr, pltpu.VMEM((2, S5_Q, ns, LANES), F32), pltpu.VMEM((2, S5_Q, ns, LANES), F32)],
        compiler_params=_params(("arbitrary", "arbitrary")))(pa, dys, states, bre, bim, cre, cim, ar, ai, dvec)


def _s5_discretize(lam_re, lam_im, log_dt, b_re, b_im):
    dt = jnp.exp(log_dt)[:, None]
    mag = jnp.exp(lam_re * dt)
    ar, ai = mag * jnp.cos(lam_im * dt), mag * jnp.sin(lam_im * dt)
    den = lam_re * lam_re + lam_im * lam_im
    qr = ((ar - 1.0) * lam_re + ai * lam_im) / den
    qi = (ai * lam_re - (ar - 1.0) * lam_im) / den
    bbr = qr[..., None] * b_re - qi[..., None] * b_im
    bbi = qr[..., None] * b_im + qi[..., None] * b_re
    return ar, ai, bbr, bbi


def _s5_expand(ar, ai, bbr, bbi, c_re, c_im):
    g, p, h = bbr.shape
    gps = LANES // h
    ns = g // gps
    eye = jnp.eye(gps, dtype=F32)

    def bexp(b):
        return jnp.einsum("sgph,gk->sghkp", b.reshape(ns, gps, p, h), eye).reshape(ns, gps * h, gps * p)

    def cexp(c):
        return jnp.einsum("sghp,gk->sgpkh", c.reshape(ns, gps, h, p), eye).reshape(ns, gps * p, gps * h)

    def aexp(a):
        return a.reshape(ns, S5_Q, LANES).transpose(1, 0, 2)

    return (bexp(bbr).astype(BF16), bexp(bbi).astype(BF16), cexp(c_re).astype(BF16), cexp(c_im).astype(BF16),
            aexp(ar), aexp(ai))


def _s5_contract(dbre, dbim, dcre, dcim, dar, dai, g, p, h):
    gps = LANES // h
    ns = g // gps
    eye = jnp.eye(gps, dtype=F32)
    bcon = lambda d: jnp.einsum("sghkp,gk->sgph", d.reshape(ns, gps, h, gps, p), eye).reshape(g, p, h)
    ccon = lambda d: jnp.einsum("sgpkh,gk->sghp", d.reshape(ns, gps, p, gps, h), eye).reshape(g, h, p)
    acon = lambda d: d.transpose(1, 0, 2).reshape(g, p)
    return bcon(dbre), bcon(dbim), ccon(dcre), ccon(dcim), acon(dar), acon(dai)


def _ml_proj_tile(cpre, xb, wq, wk, wv, gq, gk, gv):
    xc = _silu(cpre)
    q = _dot_nn(xc, wq)
    k = _dot_nn(xc, wk)
    v = _dot_nn(xb, wv)
    return q, k, v, _dot_nn(q, gq) + _dot_nn(k, gk) + _dot_nn(v, gv)


def ml_proj_fwd(cpre, xb, wq, wk, wv, gq, gk, gv, name):
    rows, width = cpre.shape
    nblk = width // LANES
    tr = _tile(rows, 1088, 16)

    def body(c_ref, x_ref, wq_ref, wk_ref, wv_ref, gq_ref, gk_ref, gv_ref, q_ref, k_ref, v_ref, g_ref):
        j = pl.program_id(1)
        q, k, v, g = _ml_proj_tile(c_ref[...], x_ref[...], wq_ref[...], wk_ref[...], wv_ref[...],
                                   gq_ref[...], gk_ref[...], gv_ref[...])
        q_ref[...] = q
        k_ref[...] = k
        v_ref[...] = v

        @pl.when(j == 0)
        def _():
            g_ref[...] = jnp.zeros_like(g_ref)

        g_ref[...] += g

    rb = pl.BlockSpec((tr, LANES), lambda i, j: (i, j))
    wb = pl.BlockSpec((None, LANES, LANES), lambda i, j: (j, 0, 0))
    return pl.pallas_call(
        body, name=name, grid=(rows // tr, nblk), in_specs=[rb, rb, wb, wb, wb, wb, wb, wb],
        out_specs=[rb, rb, rb, pl.BlockSpec((tr, LANES), lambda i, j: (i, 0))],
        out_shape=[jax.ShapeDtypeStruct((rows, width), F32)] * 3 + [jax.ShapeDtypeStruct((rows, LANES), F32)],
        compiler_params=_params(("arbitrary", "arbitrary")))(cpre, xb, wq, wk, wv, gq, gk, gv)


def ml_proj_bwd(cpre, xb, wq, wk, wv, gq, gk, gv, dq, dk, dv, dg, dcp_extra, name):
    rows, width = cpre.shape
    nblk = width // LANES
    tr = _tile(rows, 1088, 16)

    def body(c_ref, x_ref, wq_ref, wk_ref, wv_ref, gq_ref, gk_ref, gv_ref, dq_ref, dk_ref, dv_ref, dg_ref, e_ref,
             dc_ref, dx_ref, *dw_refs):
        i = pl.program_id(1)
        _, vjp = jax.vjp(_ml_proj_tile, c_ref[...], x_ref[...], wq_ref[...], wk_ref[...], wv_ref[...],
                         gq_ref[...], gk_ref[...], gv_ref[...])
        grads = vjp((dq_ref[...], dk_ref[...], dv_ref[...], dg_ref[...]))
        dc_ref[...] = grads[0] + e_ref[...]
        dx_ref[...] = grads[1]

        @pl.when(i == 0)
        def _():
            for r in dw_refs:
                r[...] = jnp.zeros_like(r)

        for r, gval in zip(dw_refs, grads[2:]):
            r[...] += gval

    rb = pl.BlockSpec((tr, LANES), lambda j, i: (i, j))
    wb = pl.BlockSpec((None, LANES, LANES), lambda j, i: (j, 0, 0))
    gb = pl.BlockSpec((tr, LANES), lambda j, i: (i, 0))
    wshape = jax.ShapeDtypeStruct((nblk, LANES, LANES), F32)
    return pl.pallas_call(
        body, name=name, grid=(nblk, rows // tr), in_specs=[rb, rb, wb, wb, wb, wb, wb, wb, rb, rb, rb, gb, rb],
        out_specs=[rb, rb] + [wb] * 6,
        out_shape=[jax.ShapeDtypeStruct((rows, width), F32)] * 2 + [wshape] * 6,
        compiler_params=_params(("arbitrary", "arbitrary")))(cpre, xb, wq, wk, wv, gq, gk, gv, dq, dk, dv, dg, dcp_extra)


def _ml_gates_tile(gl, bg, nh):
    x = gl + bg
    bcum = _dot(_tri(CHUNK), _log_sigmoid(x), ((1,), (0,)), precision=HI)
    lane = lax.broadcasted_iota(jnp.int32, x.shape, 1)
    return jnp.where(lane < nh, x, jnp.where(lane < 2 * nh, bcum, 0.0))


def _ml_core_tile(q, k, v, colg, rowg, cpre, zb, nw, sk, cst, nst, m_prev):
    c, dh = q.shape
    igc, bc = _lane_pick(colg, 0), _lane_pick(colg, 1)
    igr, br = _row_pick(rowg, 0), _row_pick(rowg, 1)
    causal = _tri(c) > 0
    dmat = jnp.where(causal, bc - br + igr, -jnp.inf)
    inter = bc + m_prev
    mt = lax.stop_gradient(jnp.maximum(inter, jnp.max(dmat, axis=1, keepdims=True)))
    wt = jnp.exp(dmat - mt)
    w_prev = jnp.exp(inter - mt)
    qs = q * (dh ** -0.5)
    s = _dot_nt(qs, k) * wt
    num = _dot_nn(s, v) + w_prev * _dot_nn(qs, cst)
    den = jnp.sum(s, axis=1, keepdims=True) + w_prev * jnp.sum(qs * nst, axis=1, keepdims=True)
    h = num / jnp.maximum(jnp.abs(den), jnp.exp(-mt))
    last = (lax.broadcasted_iota(jnp.int32, (c, 1), 0) == c - 1).astype(F32)
    blast = jnp.sum(bc * last, axis=0, keepdims=True)
    g = blast - bc + igc
    m_new = lax.stop_gradient(jnp.maximum(blast + m_prev, jnp.max(g, axis=0, keepdims=True)))
    decay = jnp.exp(blast + m_prev - m_new)
    wk = jnp.exp(g - m_new) * k
    c_new = decay * cst + _dot_tn(wk, v)
    n_new = decay * nst + jnp.sum(wk, axis=0, keepdims=True)
    mu = jnp.mean(h, axis=1, keepdims=True)
    hc = h - mu
    var = jnp.mean(hc * hc, axis=1, keepdims=True)
    out = hc * lax.rsqrt(var + HEAD_NORM_EPS) * nw + sk * _silu(cpre)
    return out * _silu(zb), c_new, n_new, m_new


def _ml_core_specs(nc, dh, rev):
    ch = (lambda c: nc - 1 - c) if rev else (lambda c: c)
    rb = pl.BlockSpec((CHUNK, dh), lambda b_, c, h: (b_ * nc + ch(c), h))
    colb = pl.BlockSpec((None, CHUNK, 2), lambda b_, c, h: (h, b_ * nc + ch(c), 0))
    rowb = pl.BlockSpec((None, None, 2, CHUNK), lambda b_, c, h: (b_ * nc + ch(c), h, 0, 0))
    pb = pl.BlockSpec((1, dh), lambda b_, c, h: (0, h))
    cb = pl.BlockSpec((None, None, dh, dh), lambda b_, c, h: (b_ * nc + ch(c), h, 0, 0))
    nb_ = pl.BlockSpec((None, None, 1, dh), lambda b_, c, h: (b_ * nc + ch(c), h, 0, 0))
    mb = pl.BlockSpec((None, None, 1, 1), lambda b_, c, h: (b_ * nc + ch(c), h, 0, 0))
    return rb, colb, rowb, pb, cb, nb_, mb


def ml_core_fwd(q, k, v, colg, rowg, cpre, zb, nw, sk, nb, nh, name):
    rows, width = q.shape
    dh = width // nh
    nc = rows // nb // CHUNK
    rb, colb, rowb, pb, cb, nb_, mb = _ml_core_specs(nc, dh, False)

    def body(q_ref, k_ref, v_ref, col_ref, row_ref, c_ref, z_ref, nw_ref, sk_ref, y_ref, cs_ref, ns_ref, ms_ref,
             cst_ref, nst_ref, mst_ref):
        c = pl.program_id(1)
        h = pl.program_id(2)

        @pl.when(c == 0)
        def _():
            cst_ref[h] = jnp.zeros((dh, dh), F32)
            nst_ref[h] = jnp.zeros((1, dh), F32)
            mst_ref[h] = jnp.zeros((1, 1), F32)

        cst, nst, m_prev = cst_ref[h], nst_ref[h], mst_ref[h]
        cs_ref[...] = cst
        ns_ref[...] = nst
        ms_ref[...] = m_prev
        y, c_new, n_new, m_new = _ml_core_tile(q_ref[...], k_ref[...], v_ref[...], col_ref[...], row_ref[...],
                                               c_ref[...], z_ref[...], nw_ref[...], sk_ref[...], cst, nst, m_prev)
        y_ref[...] = y.astype(BF16)
        cst_ref[h] = c_new
        nst_ref[h] = n_new
        mst_ref[h] = m_new

    nbc = nb * nc
    return pl.pallas_call(
        body, name=name, grid=(nb, nc, nh), in_specs=[rb, rb, rb, colb, rowb, rb, rb, pb, pb],
        out_specs=[rb, cb, nb_, mb],
        out_shape=[jax.ShapeDtypeStruct((rows, width), BF16), jax.ShapeDtypeStruct((nbc, nh, dh, dh), F32),
                   jax.ShapeDtypeStruct((nbc, nh, 1, dh), F32), jax.ShapeDtypeStruct((nbc, nh, 1, 1), F32)],
        scratch_shapes=[pltpu.VMEM((nh, dh, dh), F32), pltpu.VMEM((nh, 1, dh), F32), pltpu.VMEM((nh, 1, 1), F32)],
        compiler_params=_params(("arbitrary", "arbitrary", "arbitrary")))(q, k, v, colg, rowg, cpre, zb, nw, sk)


def ml_core_bwd(q, k, v, colg, rowg, cpre, zb, nw, sk, cs, ns, ms, dy, nb, nh, name):
    rows, width = q.shape
    dh = width // nh
    nc = rows // nb // CHUNK
    rb, colb, rowb, pb, cb, nb_, mb = _ml_core_specs(nc, dh, True)

    def body(q_ref, k_ref, v_ref, col_ref, row_ref, c_ref, z_ref, nw_ref, sk_ref, cs_ref, ns_ref, ms_ref, dy_ref,
             dq_ref, dk_ref, dv_ref, dc_ref, dz_ref, dcol_ref, drow_ref, dnw_ref, dsk_ref, dcst_ref, dnst_ref):
        bb = pl.program_id(0)
        step = pl.program_id(1)
        h = pl.program_id(2)

        @pl.when(jnp.logical_and(bb == 0, jnp.logical_and(step == 0, h == 0)))
        def _():
            dnw_ref[...] = jnp.zeros_like(dnw_ref)
            dsk_ref[...] = jnp.zeros_like(dsk_ref)

        @pl.when(step == 0)
        def _():
            dcst_ref[h] = jnp.zeros((dh, dh), F32)
            dnst_ref[h] = jnp.zeros((1, dh), F32)

        m_prev = ms_ref[...]

        def f(*a):
            return _ml_core_tile(*a, m_prev)[:3]

        _, vjp = jax.vjp(f, q_ref[...], k_ref[...], v_ref[...], col_ref[...], row_ref[...], c_ref[...], z_ref[...],
                         nw_ref[...], sk_ref[...], cs_ref[...], ns_ref[...])
        g = vjp((dy_ref[...], dcst_ref[h], dnst_ref[h]))
        dq_ref[...] = g[0]
        dk_ref[...] = g[1]
        dv_ref[...] = g[2]
        dcol_ref[...] = g[3]
        drow_ref[...] = g[4]
        dc_ref[...] = g[5]
        dz_ref[...] = g[6].astype(dz_ref.dtype)
        dnw_ref[h] += g[7]
        dsk_ref[h] += g[8]
        dcst_ref[h] = g[9]
        dnst_ref[h] = g[10]

    nbc = nb * nc
    accb = pl.BlockSpec((nh, 1, dh), lambda b_, c, h: (0, 0, 0))
    return pl.pallas_call(
        body, name=name, grid=(nb, nc, nh), in_specs=[rb, rb, rb, colb, rowb, rb, rb, pb, pb, cb, nb_, mb, rb],
        out_specs=[rb, rb, rb, rb, rb, colb, rowb, accb, accb],
        out_shape=[jax.ShapeDtypeStruct((rows, width), F32)] * 4 + [jax.ShapeDtypeStruct((rows, width), BF16)]
        + [jax.ShapeDtypeStruct(colg.shape, F32), jax.ShapeDtypeStruct(rowg.shape, F32),
           jax.ShapeDtypeStruct((nh, 1, dh), F32), jax.ShapeDtypeStruct((nh, 1, dh), F32)],
        scratch_shapes=[pltpu.VMEM((nh, dh, dh), F32), pltpu.VMEM((nh, 1, dh), F32)],
        compiler_params=_params(("arbitrary", "arbitrary", "arbitrary")))(
            q, k, v, colg, rowg, cpre, zb, nw, sk, cs, ns, ms, dy)


def _ssd_dt_tile(dtr, bias, alog):
    dt = _softplus(dtr + bias)
    cum = _dot(_tri(CHUNK), dt * (-jnp.exp(alog)), ((1,), (0,)), precision=HI)
    return dt, cum


def _ssd_tile(xcs, bmc, cmc, cols, rows_, z, dvec, gn, states, hpg):
    npair = hpg // 2
    hd = SSD_HEAD_DIM
    xs = [_silu(x) for x in xcs]
    bm, cm = _silu(bmc), _silu(cmc)
    cb = _dot_nt(cm, bm)
    causal = _tri(CHUNK) > 0
    lane_lo = lax.broadcasted_iota(jnp.int32, (1, 2 * hd), 1) < hd
    row_lo = lax.broadcasted_iota(jnp.int32, (2 * hd, 1), 0) < hd
    lastsel = (lax.broadcasted_iota(jnp.int32, (CHUNK, 1), 0) == CHUNK - 1).astype(F32)
    heads = []
    for r in range(hpg):
        dtc, cumc = _lane_pick(cols, r), _lane_pick(cols, hpg + r)
        dtrow, cumr = _row_pick(rows_, r), _row_pick(rows_, hpg + r)
        w = cb * jnp.exp(jnp.where(causal, cumc - cumr, -jnp.inf)) * dtrow
        last = jnp.sum(cumc * lastsel, axis=0, keepdims=True)
        heads.append((w, jnp.exp(cumc), jnp.exp(last - cumc) * dtc, jnp.exp(last)))
    ys, new_states = [], []
    for j in range(npair):
        (wa, ea, da, la), (wb, eb, db, lb) = heads[2 * j], heads[2 * j + 1]
        yi = jnp.where(lane_lo, _dot_nn(wa, xs[j]), _dot_nn(wb, xs[j]))
        yst = jnp.where(lane_lo, ea, eb) * _dot_nt(cm, states[j])
        ys.append(yi + yst)
        xd = xs[j] * jnp.where(lane_lo, da, db)
        new_states.append(jnp.where(row_lo, la, lb) * states[j] + _dot_tn(xd, bm))
    y = jnp.concatenate(ys, axis=1) + dvec * jnp.concatenate(xs, axis=1)
    yg = y * _silu(z)
    yn = yg * lax.rsqrt(jnp.mean(yg * yg, axis=1, keepdims=True) + NORM_EPS) * gn
    return yn, new_states


def _ssd_specs(nc, hpg, ng, rev):
    npair = hpg // 2
    gw = hpg * SSD_HEAD_DIM
    xblocks = ng * npair
    ch = (lambda c: nc - 1 - c) if rev else (lambda c: c)
    xs = [pl.BlockSpec((CHUNK, LANES), functools.partial(lambda b_, c, g, jj: (b_ * nc + ch(c), g * npair + jj), jj=j))
          for j in range(npair)]
    bmb = pl.BlockSpec((CHUNK, SSD_STATE), lambda b_, c, g: (b_ * nc + ch(c), xblocks + g))
    cmb = pl.BlockSpec((CHUNK, SSD_STATE), lambda b_, c, g: (b_ * nc + ch(c), xblocks + ng + g))
    colb = pl.BlockSpec((None, CHUNK, 2 * hpg), lambda b_, c, g: (g, b_ * nc + ch(c), 0))
    rowb = pl.BlockSpec((None, None, 2 * hpg, CHUNK), lambda b_, c, g: (b_ * nc + ch(c), g, 0, 0))
    zb = pl.BlockSpec((CHUNK, gw), lambda b_, c, g: (b_ * nc + ch(c), g))
    pb = pl.BlockSpec((1, gw), lambda b_, c, g: (0, g))
    sb = pl.BlockSpec((None, None, npair, 2 * SSD_HEAD_DIM, SSD_STATE), lambda b_, c, g: (b_ * nc + ch(c), g, 0, 0, 0))
    return xs, bmb, cmb, colb, rowb, zb, pb, sb


def ssd_core_fwd(cpre, cols, rows_, z, dvec, gn, nb, hpg, name):
    rows = cpre.shape[0]
    inner = z.shape[1]
    ng = inner // (hpg * SSD_HEAD_DIM)
    npair = hpg // 2
    nc = rows // nb // CHUNK
    xs, bmb, cmb, colb, rowb, zb, pb, sb = _ssd_specs(nc, hpg, ng, False)

    def body(*refs):
        x_refs = refs[:npair]
        bm_ref, cm_ref, col_ref, row_ref, z_ref, d_ref, gn_ref, y_ref, so_ref, st_ref = refs[npair:]
        c = pl.program_id(1)
        g = pl.program_id(2)

        @pl.when(c == 0)
        def _():
            st_ref[g] = jnp.zeros((npair, 2 * SSD_HEAD_DIM, SSD_STATE), F32)

        so_ref[...] = st_ref[g]
        states = [st_ref[g, j] for j in range(npair)]
        yn, new_states = _ssd_tile([r[...] for r in x_refs], bm_ref[...], cm_ref[...], col_ref[...], row_ref[...],
                                   z_ref[...], d_ref[...], gn_ref[...], states, hpg)
        y_ref[...] = yn.astype(BF16)
        for j in range(npair):
            st_ref[g, j] = new_states[j]

    return pl.pallas_call(
        body, name=name, grid=(nb, nc, ng), in_specs=xs + [bmb, cmb, colb, rowb, zb, pb, pb],
        out_specs=[zb, sb],
        out_shape=[jax.ShapeDtypeStruct((rows, inner), BF16),
                   jax.ShapeDtypeStruct((nb * nc, ng, npair, 2 * SSD_HEAD_DIM, SSD_STATE), F32)],
        scratch_shapes=[pltpu.VMEM((ng, npair, 2 * SSD_HEAD_DIM, SSD_STATE), F32)],
        compiler_params=_params(("arbitrary", "arbitrary", "arbitrary")))(
            *([cpre] * npair), cpre, cpre, cols, rows_, z, dvec, gn)


def ssd_core_bwd(cpre, cols, rows_, z, dvec, gn, states, dyn, nb, hpg, name):
    rows = cpre.shape[0]
    inner = z.shape[1]
    gw = hpg * SSD_HEAD_DIM
    ng = inner // gw
    npair = hpg // 2
    nc = rows // nb // CHUNK
    xs, bmb, cmb, colb, rowb, zb, pb, sb = _ssd_specs(nc, hpg, ng, True)

    def body(*refs):
        x_refs = refs[:npair]
        (bm_ref, cm_ref, col_ref, row_ref, z_ref, d_ref, gn_ref, s_ref, dy_ref,
         dx_ref, dbm_ref, dcm_ref, dcol_ref, drow_ref, dz_ref, dd_ref, dgn_ref, dst_ref) = refs[npair:]
        bb = pl.program_id(0)
        step = pl.program_id(1)
        g = pl.program_id(2)

        @pl.when(jnp.logical_and(bb == 0, jnp.logical_and(step == 0, g == 0)))
        def _():
            dd_ref[...] = jnp.zeros_like(dd_ref)
            dgn_ref[...] = jnp.zeros_like(dgn_ref)

        @pl.when(step == 0)
        def _():
            dst_ref[g] = jnp.zeros((npair, 2 * SSD_HEAD_DIM, SSD_STATE), F32)

        def f(xcs, bmc, cmc, cv, rv, zv, dv_, gv, sts):
            return _ssd_tile(xcs, bmc, cmc, cv, rv, zv, dv_, gv, sts, hpg)

        _, vjp = jax.vjp(f, [r[...] for r in x_refs], bm_ref[...], cm_ref[...], col_ref[...], row_ref[...], z_ref[...],
                         d_ref[...], gn_ref[...], [s_ref[j] for j in range(npair)])
        gr = vjp((dy_ref[...], [dst_ref[g, j] for j in range(npair)]))
        dx_ref[...] = jnp.concatenate(gr[0], axis=1)
        dbm_ref[...] = gr[1]
        dcm_ref[...] = gr[2]
        dcol_ref[...] = gr[3]
        drow_ref[...] = gr[4]
        dz_ref[...] = gr[5].astype(dz_ref.dtype)
        dd_ref[g] += gr[6]
        dgn_ref[g] += gr[7]
        for j in range(npair):
            dst_ref[g, j] = gr[8][j]

    ch = lambda c: nc - 1 - c
    nblk = pl.BlockSpec((CHUNK, SSD_STATE), lambda b_, c, g: (b_ * nc + ch(c), g))
    accb = pl.BlockSpec((ng, 1, gw), lambda b_, c, g: (0, 0, 0))
    return pl.pallas_call(
        body, name=name, grid=(nb, nc, ng), in_specs=xs + [bmb, cmb, colb, rowb, zb, pb, pb, sb, zb],
        out_specs=[zb, nblk, nblk, colb, rowb, zb, accb, accb],
        out_shape=[jax.ShapeDtypeStruct((rows, inner), F32), jax.ShapeDtypeStruct((rows, ng * SSD_STATE), F32),
                   jax.ShapeDtypeStruct((rows, ng * SSD_STATE), F32), jax.ShapeDtypeStruct(cols.shape, F32),
                   jax.ShapeDtypeStruct(rows_.shape, F32), jax.ShapeDtypeStruct((rows, inner), BF16),
                   jax.ShapeDtypeStruct((ng, 1, gw), F32), jax.ShapeDtypeStruct((ng, 1, gw), F32)],
        scratch_shapes=[pltpu.VMEM((ng, npair, 2 * SSD_HEAD_DIM, SSD_STATE), F32)],
        compiler_params=_params(("arbitrary", "arbitrary", "arbitrary")))(
            *([cpre] * npair), cpre, cpre, cols, rows_, z, dvec, gn, states, dyn)


def _hw_expand(w):
    n, bi, _ = w.shape
    per = LANES // bi
    eye = jnp.eye(per, dtype=F32)
    return jnp.einsum("jbio,bc->jbico", w.reshape(n // per, per, bi, bi), eye).reshape(n // per, LANES, LANES)


def _hw_contract(d, bi=QKV_BLOCK):
    per = LANES // bi
    eye = jnp.eye(per, dtype=F32)
    return jnp.einsum("jbico,bc->jbio", d.reshape(d.shape[0], per, bi, per, bi), eye).reshape(-1, bi, bi)


def _wg_expand(wg, width):
    pad = jnp.pad(wg, ((0, 0), (0, LANES - wg.shape[1])))
    return [pad[i * width:(i + 1) * width].reshape(width // LANES, LANES, LANES) for i in range(3)]


def _wg_contract(dgs, ngate):
    return jnp.concatenate([d[:, :, :ngate].reshape(-1, ngate) for d in dgs], axis=0)


def _pad_lanes(a):
    return jnp.pad(a, ((0, 0), (0, LANES - a.shape[1])))


def _pairs_to_layouts(first, second, ngrp, per, nbc):
    rows = first.shape[0]
    both = jnp.concatenate([first.reshape(rows, ngrp, per), second.reshape(rows, ngrp, per)], axis=2)
    return both.transpose(1, 0, 2), both.reshape(nbc, CHUNK, ngrp, 2 * per).transpose(0, 2, 3, 1)


def _layouts_to_pairs(dcols, drows, ngrp, per):
    rows = dcols.shape[1]
    both = dcols.transpose(1, 0, 2) + drows.transpose(0, 3, 1, 2).reshape(rows, ngrp, 2 * per)
    return both[:, :, :per].reshape(rows, ngrp * per), both[:, :, per:].reshape(rows, ngrp * per)


def _local_step(x, target, bw, sp):
    nb, seq, d = x.shape
    nh, hpg = MLSTM_HEADS, SSD_HPG
    t_len = N_META + seq
    nc = -(-t_len // CHUNK)
    tp = nc * CHUNK
    rows = nb * tp
    nbc = nb * nc
    meta = sp["meta_tokens"]
    h0 = jnp.concatenate([jnp.broadcast_to(meta[None], (nb, N_META, d)), x, jnp.zeros((nb, tp - t_len, d), F32)], axis=1)
    h0 = h0.reshape(rows, d)
    tgt = jnp.pad(target, ((0, 0), (N_META, tp - t_len), (0, 0))).reshape(rows, d)

    n0 = norm_fwd(h0, sp["ab_norm"], "norm0")
    pa = mm(n0, bw["W0a"], "nn", "mm_pa")
    xb = mm(n0, bw["W0xb"], "nn", "mm_xb")
    zb = mm(n0, bw["W0zb"], "nn", "mm_zb")
    s5w = pa.shape[1] // 2
    mlw = xb.shape[1]
    s5_args = (sp["s5_lambda_re"], sp["s5_lambda_im"], sp["s5_log_dt"].reshape(-1), sp["s5_b_re"], sp["s5_b_im"])
    (ar, ai, bbr, bbi), s5_disc_vjp = jax.vjp(_s5_discretize, *s5_args)
    sg, spn, shh = bbr.shape
    bre, bim, cre, cim, are, aie = _s5_expand(ar, ai, bbr, bbi, sp["s5_c_re"], sp["s5_c_im"])
    ys5, gb, s5st = s5_fwd(pa, bre, bim, cre, cim, are, aie, sp["s5_d"], nb, "s5_fwd")
    tglu = mm(gb, bw["glu"], "nn", "mm_glu")

    def glu_tile(ys, tt, za, gbias):
        return _gelu(ys) * _sigmoid(tt + gbias) * _silu(za)

    ya = rowwise("glu_fwd", lambda i, ys, tt, pab, gbias: glu_tile(ys, tt, pab[:, s5w:], gbias),
                 [ys5, tglu, pa], [sp["s5_glu_b"]], [(s5w, BF16)], tr=_tile(rows, 256, 16))[0]

    cpre0 = conv_fwd(xb, sp["ml_conv_w"], sp["ml_conv_b"], nb, "ml_conv_fwd")
    wq_e, wk_e, wv_e = _hw_expand(sp["ml_wq"]), _hw_expand(sp["ml_wk"]), _hw_expand(sp["ml_wv"])
    gq, gk, gv = _wg_expand(sp["ml_w_gate"], mlw)
    q, k, v, gl = ml_proj_fwd(cpre0, xb, wq_e, wk_e, wv_e, gq, gk, gv, "ml_proj_fwd")
    bgate = _pad_lanes(sp["ml_b_gate"])
    gout = rowwise("ml_gates_fwd", lambda i, g_, b_: _ml_gates_tile(g_, b_, nh), [gl], [bgate], [(LANES, F32)], tr=CHUNK)[0]
    colg, rowg = _pairs_to_layouts(gout[:, :nh], gout[:, nh:2 * nh], nh, 1, nbc)
    yb, ml_cs, ml_ns, ml_ms = ml_core_fwd(q, k, v, colg, rowg, cpre0, zb, sp["ml_norm"], sp["ml_skip"], nb, nh, "ml_core_fwd")
    h1 = mm(ya, bw["Wo0a"], "nn", "mm_out0a", resid=h0)
    h1 = mm(yb, bw["Wo0b"], "nn", "mm_out0b", resid=h1)

    n1 = norm_fwd(h1, sp["ssd_norm"], "norm1")
    z1 = mm(n1, bw["W1z"], "nn", "mm_z1")
    xbc = mm(n1, bw["W1x"], "nn", "mm_xbc")
    dtr = mm(n1, bw["W1dt"], "nn", "mm_dt")
    inner = z1.shape[1]
    ng = inner // (hpg * SSD_HEAD_DIM)
    nhd = ng * hpg
    cpre1 = conv_fwd(xbc, sp["ssd_conv_w"], sp["ssd_conv_b"], nb, "ssd_conv_fwd")
    dt_bias, a_log = _pad_lanes(sp["ssd_dt_bias"]), _pad_lanes(sp["ssd_a_log"])
    dt, cum = rowwise("ssd_dt_fwd", lambda i, r_, b_, a_: _ssd_dt_tile(r_, b_, a_), [dtr], [dt_bias, a_log],
                      [(LANES, F32), (LANES, F32)], tr=CHUNK)
    cols, rws = _pairs_to_layouts(dt[:, :nhd], cum[:, :nhd], ng, hpg, nbc)
    dvec = jnp.repeat(sp["ssd_d"], SSD_HEAD_DIM, axis=1)
    yn, ssd_st = ssd_core_fwd(cpre1, cols, rws, z1, dvec, sp["ssd_gnorm"], nb, hpg, "ssd_core_fwd")
    h2 = mm(yn, bw["Wo1"], "nn", "mm_out1", resid=h1)

    tr_l = _tile(tp, 256, 16)
    per_ex = tp // tr_l

    def loss_tile(i, hb, tb, gfn):
        tpos = (i % per_ex) * tr_l + lax.broadcasted_iota(jnp.int32, (tr_l, 1), 0)
        mask = jnp.logical_and(tpos >= N_META, tpos < t_len).astype(F32)

        def lf(hh, gg):
            e = (_rms(hh, gg) - tb) * mask
            return 0.5 * jnp.sum(e * e) / d

        lval, (dh, dg) = jax.value_and_grad(lf, (0, 1))(hb, gfn)
        return dh, dh, jnp.full((1, LANES), lval, F32), dg

    fn = sp["final_norm"].reshape(1, d)
    dh2, dh2b, loss_acc, dfn = rowwise("loss", loss_tile, [h2, tgt], [fn], [(d, F32), (d, BF16)], [(1, LANES), (1, d)], tr=tr_l)

    gbig, gs = {}, {}
    gs["final_norm"] = dfn.reshape(sp["final_norm"].shape)
    dyn = mm(dh2b, bw["Wo1"], "nt", "mm_dyn")
    gbig["Wo1"] = mm(yn, dh2b, "tn", "mm_dWo1", out_dtype=BF16)
    dxs, dbm, dcm, dcols, drws, dz1, ddvec, dgn = ssd_core_bwd(cpre1, cols, rws, z1, dvec, sp["ssd_gnorm"], ssd_st, dyn,
                                                              nb, hpg, "ssd_core_bwd")
    gs["ssd_d"] = ddvec.reshape(1, nhd, SSD_HEAD_DIM).sum(axis=2)
    gs["ssd_gnorm"] = dgn.reshape(1, inner)
    ddt, dcum = _layouts_to_pairs(dcols, drws, ng, hpg)

    def ssd_dt_bwd_tile(i, r_, ddt_, dcum_, b_, a_):
        _, vjp = jax.vjp(_ssd_dt_tile, r_, b_, a_)
        return vjp((ddt_, dcum_))

    ddtr, dbias, dalog = rowwise("ssd_dt_bwd", ssd_dt_bwd_tile, [dtr, _pad_lanes(ddt), _pad_lanes(dcum)], [dt_bias, a_log],
                                 [(LANES, BF16)], [(1, LANES), (1, LANES)], tr=CHUNK)
    gs["ssd_dt_bias"] = dbias[:, :nhd]
    gs["ssd_a_log"] = dalog[:, :nhd]
    dcpre1 = jnp.concatenate([dxs, dbm, dcm], axis=1)
    dxbc, dcw1, dcb1 = conv_bwd(dcpre1, xbc, sp["ssd_conv_w"], nb, "ssd_conv_bwd")
    gs["ssd_conv_w"] = dcw1[:sp["ssd_conv_w"].shape[0]]
    gs["ssd_conv_b"] = dcb1
    dn1 = mm(dz1, bw["W1z"], "nt", "mm_dn1z")
    dn1 = mm(dxbc, bw["W1x"], "nt", "mm_dn1x", resid=dn1)
    dn1 = mm(ddtr, bw["W1dt"], "nt", "mm_dn1dt", resid=dn1)
    gbig["W1z"] = mm(n1, dz1, "tn", "mm_dW1z", out_dtype=BF16)
    gbig["W1x"] = mm(n1, dxbc, "tn", "mm_dW1x", out_dtype=BF16)
    gbig["W1dt"] = mm(n1, ddtr, "tn", "mm_dW1dt", out_dtype=BF16)
    dh1, dh1b, dg1 = norm_bwd(h1, sp["ssd_norm"], dn1, dh2, "norm1_bwd")
    gs["ssd_norm"] = dg1

    dya = mm(dh1b, bw["Wo0a"], "nt", "mm_dya")
    dyb = mm(dh1b, bw["Wo0b"], "nt", "mm_dyb")
    gbig["Wo0a"] = mm(ya, dh1b, "tn", "mm_dWo0a", out_dtype=BF16)
    gbig["Wo0b"] = mm(yb, dh1b, "tn", "mm_dWo0b", out_dtype=BF16)
    (dq, dk, dv, dcp_skip, dzb, dcolg, drowg, dnw, dsk) = ml_core_bwd(
        q, k, v, colg, rowg, cpre0, zb, sp["ml_norm"], sp["ml_skip"], ml_cs, ml_ns, ml_ms, dyb, nb, nh, "ml_core_bwd")
    gs["ml_norm"] = dnw.reshape(1, mlw)
    gs["ml_skip"] = dsk.reshape(1, mlw)
    dig, dbcum = _layouts_to_pairs(dcolg, drowg, nh, 1)
    dgout = _pad_lanes(jnp.concatenate([dig, dbcum], axis=1))

    def ml_gates_bwd_tile(i, g_, dgo, b_):
        _, vjp = jax.vjp(lambda a, b: _ml_gates_tile(a, b, nh), g_, b_)
        return vjp(dgo)

    dgl, dbg = rowwise("ml_gates_bwd", ml_gates_bwd_tile, [gl, dgout], [bgate], [(LANES, F32)], [(1, LANES)], tr=CHUNK)
    gs["ml_b_gate"] = dbg[:, :2 * nh]
    dcpre0, dxb_v, dwq, dwk, dwv, dgq, dgk, dgv = ml_proj_bwd(cpre0, xb, wq_e, wk_e, wv_e, gq, gk, gv, dq, dk, dv, dgl,
                                                            dcp_skip, "ml_proj_bwd")
    gs["ml_wq"], gs["ml_wk"], gs["ml_wv"] = _hw_contract(dwq), _hw_contract(dwk), _hw_contract(dwv)
    gs["ml_w_gate"] = _wg_contract([dgq, dgk, dgv], 2 * nh)
    dxb, dcw0, dcb0 = conv_bwd(dcpre0, xb, sp["ml_conv_w"], nb, "ml_conv_bwd", resid=dxb_v)
    gs["ml_conv_w"] = dcw0[:sp["ml_conv_w"].shape[0]]
    gs["ml_conv_b"] = dcb0

    def glu_bwd_tile(i, ys, tt, pab, dy_, gbias):
        _, vjp = jax.vjp(glu_tile, ys, tt, pab[:, s5w:], gbias)
        return vjp(dy_)

    dys_direct, dtglu, dza, dglub = rowwise("glu_bwd", glu_bwd_tile, [ys5, tglu, pa, dya], [sp["s5_glu_b"]],
                                            [(s5w, F32), (s5w, BF16), (s5w, BF16)], [(1, s5w)], tr=_tile(rows, 256, 16))
    gs["s5_glu_b"] = dglub
    dgb = mm(dtglu, bw["glu"], "nt", "mm_dgb")
    gbig["glu"] = mm(gb, dtglu, "tn", "mm_dglu", out_dtype=BF16)

    def gelu_bwd_tile(i, ys, dg_, direct):
        _, vjp = jax.vjp(_gelu, ys)
        return vjp(dg_)[0] + direct

    dys5 = rowwise("gelu_bwd", gelu_bwd_tile, [ys5, dgb, dys_direct], [], [(s5w, F32)], tr=_tile(rows, 256, 16))[0]
    du, dbre, dbim, dcre, dcim, dare, daie, dd5 = s5_bwd(pa, dys5, s5st, bre, bim, cre, cim, are, aie, sp["s5_d"], nb, "s5_bwd")
    gs["s5_d"] = dd5
    dbbr, dbbi, dcr, dci, dar, dai = _s5_contract(dbre, dbim, dcre, dcim, dare, daie, sg, spn, shh)
    gs["s5_c_re"], gs["s5_c_im"] = dcr, dci
    (gs["s5_lambda_re"], gs["s5_lambda_im"], dlogdt, gs["s5_b_re"], gs["s5_b_im"]) = s5_disc_vjp((dar, dai, dbbr, dbbi))
    gs["s5_log_dt"] = dlogdt.reshape(1, -1)
    dpa = jnp.concatenate([du, dza], axis=1)
    dn0 = mm(dpa, bw["W0a"], "nt", "mm_dn0a")
    dn0 = mm(dxb, bw["W0xb"], "nt", "mm_dn0xb", resid=dn0)
    dn0 = mm(dzb, bw["W0zb"], "nt", "mm_dn0zb", resid=dn0)
    gbig["W0a"] = mm(n0, dpa, "tn", "mm_dW0a", out_dtype=BF16)
    gbig["W0xb"] = mm(n0, dxb, "tn", "mm_dW0xb", out_dtype=BF16)
    gbig["W0zb"] = mm(n0, dzb, "tn", "mm_dW0zb", out_dtype=BF16)
    dh0, _, dg0 = norm_bwd(h0, sp["ab_norm"], dn0, dh1, "norm0_bwd")
    gs["ab_norm"] = dg0
    dh0 = dh0.reshape(nb, tp, d)
    gs["meta_tokens"] = jnp.sum(dh0[:, :N_META], axis=0)
    return loss_acc[0, 0], dh0, gbig, gs


N_DEV = 8
N_CHIP = 4
MESH = pl.DeviceIdType.MESH
_HBM = pl.BlockSpec(memory_space=pltpu.HBM)


def _place():
    x, y, c = lax.axis_index("x"), lax.axis_index("y"), lax.axis_index("c")
    return x, y, c, [(1 - x, y), (x, 1 - y), (1 - x, 1 - y)]


def all_gather8(v, name):
    m_per, n = v.shape

    def body(x_ref, out_ref, send_sems, recv_sems, local_sem):
        x, y, c, chips = _place()
        me, sibling = (x, y, c), (x, y, 1 - c)

        def rows(px, py, pc):
            return out_ref.at[pl.ds((4 * px + 2 * py + pc) * m_per, m_per), :]

        def copy(kk, block, to, src=None):
            return pltpu.make_async_remote_copy(
                src_ref=rows(*block) if src is None else src, dst_ref=rows(*block), send_sem=send_sems.at[kk],
                recv_sem=recv_sems.at[kk], device_id=to, device_id_type=MESH)

        mine = pltpu.make_async_copy(x_ref, rows(*me), local_sem)
        mine.start()
        first = [copy(0, me, sibling, src=x_ref)]
        first += [copy(1 + j, me, (*chip, c), src=x_ref) for j, chip in enumerate(chips)]
        for cp in first:
            cp.start()
        passed = [copy(4 + j, (*chip, c), sibling) for j, chip in enumerate(chips)]
        for j, chip in enumerate(chips):
            copy(1 + j, (*chip, c), me).wait_recv()
            passed[j].start()
        copy(0, sibling, me).wait_recv()
        for j, chip in enumerate(chips):
            copy(4 + j, (*chip, 1 - c), me).wait_recv()
        for cp in first + passed:
            cp.wait_send()
        mine.wait()

    return pl.pallas_call(
        body, name=name, out_shape=jax.ShapeDtypeStruct((N_DEV * m_per, n), v.dtype),
        in_specs=[pl.BlockSpec(memory_space=pltpu.VMEM)], out_specs=pl.BlockSpec(memory_space=pltpu.VMEM),
        scratch_shapes=[pltpu.SemaphoreType.DMA((7,)), pltpu.SemaphoreType.DMA((7,)), pltpu.SemaphoreType.DMA],
        compiler_params=pltpu.CompilerParams(vmem_limit_bytes=VMEM_LIMIT))(v)


def gather_chips(vs, name):
    na = len(vs)

    def body(*refs):
        x_refs, out_refs = refs[:na], refs[na:2 * na]
        send_sems, recv_sems, local_sems = refs[2 * na:]
        x, y, c, chips = _place()
        k = 2 * x + y
        sibling = (x, y, 1 - c)

        def copy(i, kk, src, chip_k, half, to):
            return pltpu.make_async_remote_copy(
                src_ref=src, dst_ref=out_refs[i].at[chip_k, half], send_sem=send_sems.at[6 * i + kk],
                recv_sem=recv_sems.at[6 * i + kk], device_id=to, device_id_type=MESH)

        mine = [pltpu.make_async_copy(x_refs[i], out_refs[i].at[k], local_sems.at[i]) for i in range(na)]
        for cp in mine:
            cp.start()
        first = [copy(i, j, x_refs[i].at[c], k, c, (*chip, c)) for j, chip in enumerate(chips) for i in range(na)]
        for cp in first:
            cp.start()
        passed = []
        for j, (cx, cy) in enumerate(chips):
            kj = 2 * cx + cy
            for i in range(na):
                copy(i, j, out_refs[i].at[kj, c], kj, c, (cx, cy, c)).wait_recv()
                fwd = copy(i, 3 + j, out_refs[i].at[kj, c], kj, c, sibling)
                fwd.start()
                passed.append(fwd)
        for j, (cx, cy) in enumerate(chips):
            kj = 2 * cx + cy
            for i in range(na):
                copy(i, 3 + j, out_refs[i].at[kj, 1 - c], kj, 1 - c, sibling).wait_recv()
        for cp in first + passed:
            cp.wait_send()
        for cp in mine:
            cp.wait()

    return pl.pallas_call(
        body, name=name, out_shape=[jax.ShapeDtypeStruct((N_CHIP,) + v.shape, v.dtype) for v in vs],
        in_specs=[_HBM] * na, out_specs=[_HBM] * na,
        scratch_shapes=[pltpu.SemaphoreType.DMA((6 * na,)), pltpu.SemaphoreType.DMA((6 * na,)),
                        pltpu.SemaphoreType.DMA((na,))])(*vs)


def scatter_chips(ps, name):
    na = len(ps)

    def body(*refs):
        p_refs, out_refs = refs[:na], refs[na:2 * na]
        send_sems, recv_sems, local_sems = refs[2 * na:]
        x, y, c, chips = _place()
        k = 2 * x + y
        sibling = (x, y, 1 - c)

        def copy(i, kk, src, chip_k, half, to):
            return pltpu.make_async_remote_copy(
                src_ref=src, dst_ref=out_refs[i].at[chip_k, half], send_sem=send_sems.at[7 * i + kk],
                recv_sem=recv_sems.at[7 * i + kk], device_id=to, device_id_type=MESH)

        mine = [pltpu.make_async_copy(p_refs[i].at[k], out_refs[i].at[k, c], local_sems.at[i]) for i in range(na)]
        for cp in mine:
            cp.start()
        first = [copy(i, 1 + j, p_refs[i].at[2 * cx + cy], k, c, (cx, cy, c))
                 for j, (cx, cy) in enumerate(chips) for i in range(na)]
        first += [copy(i, 0, p_refs[i].at[k], k, c, sibling) for i in range(na)]
        for cp in first:
            cp.start()
        passed = []
        for j, (cx, cy) in enumerate(chips):
            kj = 2 * cx + cy
            for i in range(na):
                copy(i, 1 + j, out_refs[i].at[kj, c], kj, c, (cx, cy, c)).wait_recv()
                fwd = copy(i, 4 + j, out_refs[i].at[kj, c], kj, c, sibling)
                fwd.start()
                passed.append(fwd)
        for i in range(na):
            copy(i, 0, out_refs[i].at[k, 1 - c], k, 1 - c, sibling).wait_recv()
        for j, (cx, cy) in enumerate(chips):
            kj = 2 * cx + cy
            for i in range(na):
                copy(i, 4 + j, out_refs[i].at[kj, 1 - c], kj, 1 - c, sibling).wait_recv()
        for cp in first + passed:
            cp.wait_send()
        for cp in mine:
            cp.wait()

    return pl.pallas_call(
        body, name=name, out_shape=[jax.ShapeDtypeStruct((N_CHIP, 2) + p.shape[1:], p.dtype) for p in ps],
        in_specs=[_HBM] * na, out_specs=[_HBM] * na,
        scratch_shapes=[pltpu.SemaphoreType.DMA((7 * na,)), pltpu.SemaphoreType.DMA((7 * na,)),
                        pltpu.SemaphoreType.DMA((na,))])(*ps)


def swap_halves(gs_, name):
    na = len(gs_)

    def body(*refs):
        g_refs, out_refs = refs[:na], refs[na:2 * na]
        send_sems, recv_sems = refs[2 * na:]
        x, y, c, _ = _place()
        cps = [pltpu.make_async_remote_copy(
            src_ref=g_refs[i].at[kk, 1 - c], dst_ref=out_refs[i].at[kk], send_sem=send_sems.at[N_CHIP * i + kk],
            recv_sem=recv_sems.at[N_CHIP * i + kk], device_id=(x, y, 1 - c), device_id_type=MESH)
            for i in range(na) for kk in range(N_CHIP)]
        for cp in cps:
            cp.start()
        for cp in cps:
            cp.wait()

    return pl.pallas_call(
        body, name=name, out_shape=[jax.ShapeDtypeStruct((N_CHIP,) + g.shape[2:], g.dtype) for g in gs_],
        in_specs=[_HBM] * na, out_specs=[_HBM] * na,
        scratch_shapes=[pltpu.SemaphoreType.DMA((N_CHIP * na,)), pltpu.SemaphoreType.DMA((N_CHIP * na,))])(*gs_)


def add_halves(g, other, core, name):
    _, _, m, n = g.shape
    tr = _tile(m, 256, 16)

    def body(core_ref, g_ref, o_ref, out_ref):
        out_ref[...] = (g_ref[...].astype(F32) + o_ref[...].astype(F32)).astype(out_ref.dtype)

    grid_spec = pltpu.PrefetchScalarGridSpec(
        num_scalar_prefetch=1, grid=(N_CHIP, m // tr),
        in_specs=[pl.BlockSpec((None, None, tr, n), lambda kk, i, core_ref: (kk, core_ref[0], i, 0)),
                  pl.BlockSpec((None, tr, n), lambda kk, i, core_ref: (kk, i, 0))],
        out_specs=pl.BlockSpec((None, tr, n), lambda kk, i, core_ref: (kk, i, 0)))
    return pl.pallas_call(body, name=name, grid_spec=grid_spec, out_shape=jax.ShapeDtypeStruct((N_CHIP, m, n), g.dtype),
                          compiler_params=_params(("arbitrary", "arbitrary")))(core.reshape(1).astype(jnp.int32), g, other)


PACK_LANES = 512


def _pack(arrs, dtype, lanes, row_align):
    flat = jnp.concatenate([a.reshape(-1).astype(dtype) for a in arrs])
    unit = lanes * row_align
    total = -(-flat.shape[0] // unit) * unit
    return jnp.pad(flat, (0, total - flat.shape[0])).reshape(total // lanes, lanes)


def _unpack(flat, shapes):
    flat = flat.reshape(-1)
    out, off = [], 0
    for s in shapes:
        n = math.prod(s)
        out.append(flat[off:off + n].reshape(s))
        off += n
    return out


def _adam_tile(w, m, v, g):
    m2 = ADAM_B1 * m + (1.0 - ADAM_B1) * g
    v2 = ADAM_B2 * v + (1.0 - ADAM_B2) * (g * g)
    m_hat = m2 / (1.0 - ADAM_B1 ** ADAM_STEP)
    v_hat = v2 / (1.0 - ADAM_B2 ** ADAM_STEP)
    delta = -ADAM_LR * (m_hat / (jnp.sqrt(v_hat) + ADAM_EPS) + ADAM_WD * w)
    return delta, m2, v2


def adam_big(w, m, v, pieces, name):
    _, r, c = w.shape
    tr = _tile(r, 128, 16)

    def body(w_ref, m_ref, v_ref, p0, p1, p2, p3, g_ref, d_ref, mo_ref, vo_ref):
        g = ((p0[...].astype(F32) + p1[...].astype(F32)) + p2[...].astype(F32)) + p3[...].astype(F32)
        delta, m2, v2 = _adam_tile(w_ref[...], m_ref[...], v_ref[...], g)
        g_ref[...] = g
        d_ref[...] = delta
        mo_ref[...] = m2
        vo_ref[...] = v2

    wspec = pl.BlockSpec((None, tr, c), lambda i: (0, i, 0))
    pspecs = [pl.BlockSpec((None, tr, c), functools.partial(lambda i, kk: (kk, i, 0), kk=kk)) for kk in range(N_CHIP)]
    return pl.pallas_call(
        body, name=name, grid=(r // tr,), in_specs=[wspec] * 3 + pspecs, out_specs=[wspec] * 4,
        out_shape=[jax.ShapeDtypeStruct(w.shape, F32)] * 4, compiler_params=_params(("parallel",)))(
            w, m, v, pieces, pieces, pieces, pieces)


_WEIGHTS = (
    ("meta_tokens", "small", 1), ("ab_norm", "small", None), ("ab_w_in", "big", 2), ("s5_lambda_re", "small", None),
    ("s5_lambda_im", "small", None), ("s5_log_dt", "small", None), ("s5_b_re", "small", None), ("s5_b_im", "small", None),
    ("s5_c_re", "small", None), ("s5_c_im", "small", None), ("s5_d", "small", None), ("s5_glu_w", "big", 1),
    ("s5_glu_b", "small", None), ("ml_conv_w", "small", 2), ("ml_conv_b", "small", None), ("ml_wq", "small", 1),
    ("ml_wk", "small", 1), ("ml_wv", "small", 1), ("ml_w_gate", "small", 1), ("ml_b_gate", "small", None),
    ("ml_norm", "small", None), ("ml_skip", "small", None), ("ab_w_out", "big", 1), ("ssd_norm", "small", 1),
    ("ssd_w_in", "big", 2), ("ssd_conv_w", "small", 2), ("ssd_conv_b", "small", 1), ("ssd_dt_bias", "small", None),
    ("ssd_a_log", "small", None), ("ssd_d", "small", None), ("ssd_gnorm", "small", 1), ("ssd_w_out", "big", 1),
    ("final_norm", "small", None),
)


def _squeeze(a):
    return a[0] if a.ndim >= 3 else a


def kernel(x, meta_tokens, ab_norm, ab_w_in, s5_lambda_re, s5_lambda_im, s5_log_dt, s5_b_re, s5_b_im, s5_c_re, s5_c_im, s5_d, s5_glu_w, s5_glu_b, ml_conv_w, ml_conv_b, ml_wq, ml_wk, ml_wv, ml_w_gate, ml_b_gate, ml_norm, ml_skip, ab_w_out, ssd_norm, ssd_w_in, ssd_conv_w, ssd_conv_b, ssd_dt_bias, ssd_a_log, ssd_d, ssd_gnorm, ssd_w_out, final_norm, loss_target, m_meta_tokens, m_ab_norm, m_ab_w_in, m_s5_lambda_re, m_s5_lambda_im, m_s5_log_dt, m_s5_b_re, m_s5_b_im, m_s5_c_re, m_s5_c_im, m_s5_d, m_s5_glu_w, m_s5_glu_b, m_ml_conv_w, m_ml_conv_b, m_ml_wq, m_ml_wk, m_ml_wv, m_ml_w_gate, m_ml_b_gate, m_ml_norm, m_ml_skip, m_ab_w_out, m_ssd_norm, m_ssd_w_in, m_ssd_conv_w, m_ssd_conv_b, m_ssd_dt_bias, m_ssd_a_log, m_ssd_d, m_ssd_gnorm, m_ssd_w_out, m_final_norm, v_meta_tokens, v_ab_norm, v_ab_w_in, v_s5_lambda_re, v_s5_lambda_im, v_s5_log_dt, v_s5_b_re, v_s5_b_im, v_s5_c_re, v_s5_c_im, v_s5_d, v_s5_glu_w, v_s5_glu_b, v_ml_conv_w, v_ml_conv_b, v_ml_wq, v_ml_wk, v_ml_wv, v_ml_w_gate, v_ml_b_gate, v_ml_norm, v_ml_skip, v_ab_w_out, v_ssd_norm, v_ssd_w_in, v_ssd_conv_w, v_ssd_conv_b, v_ssd_dt_bias, v_ssd_a_log, v_ssd_d, v_ssd_gnorm, v_ssd_w_out, v_final_norm):
    args = (meta_tokens, ab_norm, ab_w_in, s5_lambda_re, s5_lambda_im, s5_log_dt, s5_b_re, s5_b_im, s5_c_re, s5_c_im, s5_d, s5_glu_w, s5_glu_b, ml_conv_w, ml_conv_b, ml_wq, ml_wk, ml_wv, ml_w_gate, ml_b_gate, ml_norm, ml_skip, ab_w_out, ssd_norm, ssd_w_in, ssd_conv_w, ssd_conv_b, ssd_dt_bias, ssd_a_log, ssd_d, ssd_gnorm, ssd_w_out, final_norm)
    m_args = (m_meta_tokens, m_ab_norm, m_ab_w_in, m_s5_lambda_re, m_s5_lambda_im, m_s5_log_dt, m_s5_b_re, m_s5_b_im, m_s5_c_re, m_s5_c_im, m_s5_d, m_s5_glu_w, m_s5_glu_b, m_ml_conv_w, m_ml_conv_b, m_ml_wq, m_ml_wk, m_ml_wv, m_ml_w_gate, m_ml_b_gate, m_ml_norm, m_ml_skip, m_ab_w_out, m_ssd_norm, m_ssd_w_in, m_ssd_conv_w, m_ssd_conv_b, m_ssd_dt_bias, m_ssd_a_log, m_ssd_d, m_ssd_gnorm, m_ssd_w_out, m_final_norm)
    v_args = (v_meta_tokens, v_ab_norm, v_ab_w_in, v_s5_lambda_re, v_s5_lambda_im, v_s5_log_dt, v_s5_b_re, v_s5_b_im, v_s5_c_re, v_s5_c_im, v_s5_d, v_s5_glu_w, v_s5_glu_b, v_ml_conv_w, v_ml_conv_b, v_ml_wq, v_ml_wk, v_ml_wv, v_ml_w_gate, v_ml_b_gate, v_ml_norm, v_ml_skip, v_ab_w_out, v_ssd_norm, v_ssd_w_in, v_ssd_conv_w, v_ssd_conv_b, v_ssd_dt_bias, v_ssd_a_log, v_ssd_d, v_ssd_gnorm, v_ssd_w_out, v_final_norm)
    names = [w[0] for w in _WEIGHTS]
    kind = {w[0]: w[1] for w in _WEIGHTS}
    axis = {w[0]: w[2] for w in _WEIGHTS}
    w_loc = dict(zip(names, args))
    m_loc = dict(zip(names, m_args))
    v_loc = dict(zip(names, v_args))
    chip = 2 * lax.axis_index("x") + lax.axis_index("y")
    core = lax.axis_index("c")
    big = [n for n in names if kind[n] == "big"]
    small = [n for n in names if kind[n] == "small"]
    small_sh = [n for n in small if axis[n] is not None]

    def halves(a):
        return a.astype(BF16).reshape(2, a.shape[1] // 2, a.shape[2])

    gathered = gather_chips([halves(w_loc[n]) for n in big], "gather_big_w")
    full_big = {}
    for n, gth in zip(big, gathered):
        shard = gth.reshape((N_CHIP,) + w_loc[n].shape[1:])
        if axis[n] == 1:
            full_big[n] = shard.reshape(-1, shard.shape[2])
        else:
            full_big[n] = jnp.concatenate([shard[kk] for kk in range(N_CHIP)], axis=1)
    small_sh_shapes = [w_loc[n].shape for n in small_sh]
    packed_s = _pack([w_loc[n] for n in small_sh], F32, LANES, SUBLANES)
    g8 = all_gather8(packed_s, "gather_small_w").reshape(N_CHIP, 2, -1)
    sp = {}
    for n in small:
        if axis[n] is None:
            sp[n] = _squeeze(w_loc[n])
    per_chip = [_unpack(g8[kk, 0], small_sh_shapes) for kk in range(N_CHIP)]
    for i, n in enumerate(small_sh):
        sp[n] = _squeeze(jnp.concatenate([per_chip[kk][i] for kk in range(N_CHIP)], axis=axis[n]))

    s5w = full_big["s5_glu_w"].shape[0]
    mlw = full_big["ab_w_out"].shape[0] - s5w
    inner = full_big["ssd_w_out"].shape[0]
    w_in0, w_in1 = full_big["ab_w_in"], full_big["ssd_w_in"]
    n_heads1 = sp["ssd_d"].shape[1]
    cdim = w_in1.shape[1] - inner - n_heads1
    bw = dict(W0a=w_in0[:, :2 * s5w], W0xb=w_in0[:, 2 * s5w:2 * s5w + mlw], W0zb=w_in0[:, 2 * s5w + mlw:],
              glu=full_big["s5_glu_w"], Wo0a=full_big["ab_w_out"][:s5w], Wo0b=full_big["ab_w_out"][s5w:],
              W1z=w_in1[:, :inner], W1x=w_in1[:, inner:inner + cdim], W1dt=_pad_lanes(w_in1[:, inner + cdim:]),
              Wo1=full_big["ssd_w_out"])

    loss_local, dh0, gbig, gs = _local_step(x, loss_target, bw, sp)
    loss = lax.psum(loss_local, ("x", "y", "c"))
    grad_x = dh0[:, N_META:N_META + x.shape[1]]

    gfull = {
        "ab_w_in": jnp.concatenate([gbig["W0a"], gbig["W0xb"], gbig["W0zb"]], axis=1),
        "s5_glu_w": gbig["glu"],
        "ab_w_out": jnp.concatenate([gbig["Wo0a"], gbig["Wo0b"]], axis=0),
        "ssd_w_in": jnp.concatenate([gbig["W1z"], gbig["W1x"], gbig["W1dt"][:, :n_heads1]], axis=1),
        "ssd_w_out": gbig["Wo1"],
    }
    gps = []
    for n in big:
        _, r, c_ = w_loc[n].shape
        gf = gfull[n]
        if axis[n] == 1:
            gp = gf.reshape(N_CHIP, 2, r // 2, c_)
        else:
            gp = jnp.stack([gf[:, kk * c_:(kk + 1) * c_] for kk in range(N_CHIP)]).reshape(N_CHIP, 2, r // 2, c_)
        gps.append(gp)
    from_sibling = swap_halves(gps, "swap_big_g")
    partials = [add_halves(gp, oth, core, "add_" + n) for n, gp, oth in zip(big, gps, from_sibling)]
    pieces = scatter_chips(partials, "scatter_big_g")

    out_g, out_d, out_m, out_v = {}, {}, {}, {}
    for n, pc in zip(big, pieces):
        pcs = pc.reshape((N_CHIP,) + w_loc[n].shape[1:])
        out_g[n], out_d[n], out_m[n], out_v[n] = adam_big(w_loc[n], m_loc[n], v_loc[n], pcs, "adam_" + n)

    small_full_shapes = [sp[n].shape for n in small]
    packed_gs = _pack([gs[n] for n in small], F32, LANES, SUBLANES)
    rows_s = packed_gs.shape[0]
    all_gs = all_gather8(packed_gs, "gather_small_g")
    blocks = [all_gs[i * rows_s:(i + 1) * rows_s] for i in range(N_DEV)]

    def sum8(i, *b):
        acc = b[0]
        for t in b[1:]:
            acc = acc + t
        return acc

    gsum = rowwise("sum_small_g", sum8, blocks, [], [(LANES, F32)], tr=_tile(rows_s, 512, 8))[0]
    g_small = dict(zip(small, _unpack(gsum, small_full_shapes)))
    g_loc = {}
    for n in small:
        g = g_small[n].reshape((1,) + g_small[n].shape) if w_loc[n].ndim >= 3 else g_small[n]
        if axis[n] is not None:
            size = w_loc[n].shape[axis[n]]
            g = lax.dynamic_slice_in_dim(g, chip * size, size, axis=axis[n])
        g_loc[n] = g.reshape(w_loc[n].shape)
    loc_shapes = [w_loc[n].shape for n in small]
    pw, pm, pv, pg = (_pack([d[n] for n in small], F32, LANES, SUBLANES) for d in (w_loc, m_loc, v_loc, g_loc))
    dl, mn, vn = rowwise("adam_small", lambda i, a, b, c_, d_: _adam_tile(a, b, c_, d_), [pw, pm, pv, pg], [],
                         [(LANES, F32)] * 3, tr=_tile(pw.shape[0], 512, 8))
    for d_out, flat in ((out_d, dl), (out_m, mn), (out_v, vn)):
        for n, a in zip(small, _unpack(flat, loc_shapes)):
            d_out[n] = a
    for n in small:
        out_g[n] = g_loc[n]

    return (loss, grad_x, *[out_g[n] for n in names], *[out_d[n] for n in names], *[out_m[n] for n in names],
            *[out_v[n] for n in names])
```

```python
import functools
import math

import jax
import jax.numpy as jnp
from jax import lax
from jax.experimental import pallas as pl
from jax.experimental.pallas import tpu as pltpu

F32 = jnp.float32
BF16 = jnp.bfloat16
HI = lax.Precision.HIGHEST

D_MODEL = 2048
SEQ = 2048
N_META = 16
CHUNK = 128
NORM_EPS = 1e-6
HEAD_NORM_EPS = 1e-5
S5_GROUP_SIZE = 16
S5_STATE = 64
MLSTM_HEADS = 8
QKV_BLOCK = 4
SSD_HEAD_DIM = 64
SSD_STATE = 128
SSD_HPG = 8
ADAM_LR = 0.001
ADAM_B1 = 0.9
ADAM_B2 = 0.999
ADAM_EPS = 1e-08
ADAM_WD = 0.01
ADAM_STEP = 10

LANES = 128
SUBLANES = 8
VMEM_LIMIT = 56 * 1024 * 1024
MM_OPERAND_VMEM = 34 * 1024 * 1024


def _sigmoid(x):
    return 0.5 * jnp.tanh(0.5 * x) + 0.5


@jax.custom_vjp
def _silu(x):
    return x * _sigmoid(x)


def _silu_fwd(x):
    return x * _sigmoid(x), x


def _silu_bwd(x, ct):
    s = _sigmoid(x)
    return (ct * (s * (1.0 + x * (1.0 - s))),)


_silu.defvjp(_silu_fwd, _silu_bwd)


def _softplus(x):
    return jnp.maximum(x, 0.0) + jnp.log(1.0 + jnp.exp(-jnp.abs(x)))


def _log_sigmoid(x):
    return jnp.minimum(x, 0.0) - jnp.log(1.0 + jnp.exp(-jnp.abs(x)))


def _gelu(x):
    return 0.5 * x * (1.0 + jnp.tanh(math.sqrt(2.0 / math.pi) * (x + 0.044715 * (x * x * x))))


def _dot(a, b, dims, precision=None):
    return lax.dot_general(a, b, (dims, ((), ())), preferred_element_type=F32, precision=precision)


_NN, _NT, _TN = ((1,), (0,)), ((1,), (1,)), ((0,), (0,))


def _bf16_dot(dims, da_rule, db_rule):
    @jax.custom_vjp
    def f(a, b):
        return _dot(a.astype(BF16), b.astype(BF16), dims)

    def fwd(a, b):
        ab, bb = a.astype(BF16), b.astype(BF16)
        return _dot(ab, bb, dims), (ab, bb, jnp.zeros((), a.dtype), jnp.zeros((), b.dtype))

    def bwd(res, ct):
        ab, bb, a_like, b_like = res
        cb = ct.astype(BF16)
        return da_rule(ab, bb, cb).astype(a_like.dtype), db_rule(ab, bb, cb).astype(b_like.dtype)

    f.defvjp(fwd, bwd)
    return f


_dot_nn = _bf16_dot(_NN, lambda a, b, c: _dot(c, b, _NT), lambda a, b, c: _dot(a, c, _TN))
_dot_nt = _bf16_dot(_NT, lambda a, b, c: _dot(c, b, _NN), lambda a, b, c: _dot(c, a, _TN))
_dot_tn = _bf16_dot(_TN, lambda a, b, c: _dot(b, c, _NT), lambda a, b, c: _dot(a, c, _NN))


def _lane_pick(a, idx):
    sel = (lax.broadcasted_iota(jnp.int32, (1, a.shape[1]), 1) == idx).astype(a.dtype)
    return jnp.sum(a * sel, axis=1, keepdims=True)


def _row_pick(a, idx):
    sel = (lax.broadcasted_iota(jnp.int32, (a.shape[0], 1), 0) == idx).astype(a.dtype)
    return jnp.sum(a * sel, axis=0, keepdims=True)


def _tri(n, upper=False):
    r = lax.broadcasted_iota(jnp.int32, (n, n), 0)
    c = lax.broadcasted_iota(jnp.int32, (n, n), 1)
    return ((r <= c) if upper else (r >= c)).astype(F32)


def _tile(n, target, align):
    if n <= target:
        return n
    t = (target // align) * align
    while t >= align:
        if n % t == 0:
            return t
        t -= align
    return n


def _params(sem=None):
    return pltpu.CompilerParams(dimension_semantics=sem, vmem_limit_bytes=VMEM_LIMIT)


def mm(a, b, mode, name, resid=None, out_dtype=F32):
    if mode == "nn":
        (m, k), (k2, n) = a.shape, b.shape
    elif mode == "nt":
        (m, k), (n, k2) = a.shape, b.shape
    else:
        (k, m), (k2, n) = a.shape, b.shape
    assert k == k2, (a.shape, b.shape, mode)
    a_sz, b_sz = a.dtype.itemsize, b.dtype.itemsize
    if mode == "tn":
        tm, tn = _tile(m, 1024, LANES), _tile(n, 1024, LANES)
        tk = _tile(k, MM_OPERAND_VMEM // (2 * (tm * a_sz + tn * b_sz)), 16)
    else:
        tm, tn = _tile(m, 1088, 16), _tile(n, 512, LANES)
        tk = _tile(k, MM_OPERAND_VMEM // (2 * (tm * a_sz + tn * b_sz)), LANES)
    nk = k // tk
    dims = {"nn": ((1,), (0,)), "nt": ((1,), (1,)), "tn": ((0,), (0,))}[mode]
    has_resid = resid is not None

    def body(*refs):
        if has_resid:
            a_ref, b_ref, r_ref, o_ref = refs[:4]
        else:
            a_ref, b_ref, o_ref = refs[:3]
        part = _dot(a_ref[...].astype(BF16), b_ref[...].astype(BF16), dims)

        def finish(res):
            if has_resid:
                res = res + r_ref[...].astype(F32)
            o_ref[...] = res.astype(o_ref.dtype)

        if nk == 1:
            finish(part)
            return
        acc_ref = refs[-1]
        kk = pl.program_id(2)

        @pl.when(kk == 0)
        def _():
            acc_ref[...] = part

        @pl.when(jnp.logical_and(kk > 0, kk < nk - 1))
        def _():
            acc_ref[...] += part

        @pl.when(kk == nk - 1)
        def _():
            finish(acc_ref[...] + part)

    if mode == "tn":
        a_spec = pl.BlockSpec((tk, tm), lambda i, j, kk: (kk, i))
    else:
        a_spec = pl.BlockSpec((tm, tk), lambda i, j, kk: (i, kk))
    if mode == "nt":
        b_spec = pl.BlockSpec((tn, tk), lambda i, j, kk: (j, kk))
    else:
        b_spec = pl.BlockSpec((tk, tn), lambda i, j, kk: (kk, j))
    o_spec = pl.BlockSpec((tm, tn), lambda i, j, kk: (i, j))
    in_specs = [a_spec, b_spec] + ([o_spec] if has_resid else [])
    args = (a, b) + ((resid,) if has_resid else ())
    return pl.pallas_call(
        body, name=name, grid=(m // tm, n // tn, nk), in_specs=in_specs, out_specs=o_spec,
        out_shape=jax.ShapeDtypeStruct((m, n), out_dtype), scratch_shapes=[pltpu.VMEM((tm, tn), F32)] if nk > 1 else [],
        compiler_params=_params(("parallel", "parallel", "arbitrary")))(*args)


def rowwise(name, f, rows, params, outs, accs=(), tr=128):
    n_rows = rows[0].shape[0]
    assert n_rows % tr == 0
    n_r, n_p, n_o, n_a = len(rows), len(params), len(outs), len(accs)

    def body(*refs):
        i = pl.program_id(0)
        r_vals = [r[...] for r in refs[:n_r]]
        p_vals = [r[...] for r in refs[n_r:n_r + n_p]]
        o_refs = refs[n_r + n_p:n_r + n_p + n_o]
        a_refs = refs[n_r + n_p + n_o:]
        res = f(i, *r_vals, *p_vals)
        if not isinstance(res, (tuple, list)):
            res = (res,)
        assert len(res) == n_o + n_a, (name, len(res))
        for o_ref, val in zip(o_refs, res[:n_o]):
            o_ref[...] = val.astype(o_ref.dtype)
        if n_a:
            @pl.when(i == 0)
            def _():
                for a_ref in a_refs:
                    a_ref[...] = jnp.zeros_like(a_ref)

            for a_ref, val in zip(a_refs, res[n_o:]):
                a_ref[...] += val.astype(F32)

    in_specs = [pl.BlockSpec((tr, r.shape[1]), lambda i: (i, 0)) for r in rows]
    in_specs += [pl.BlockSpec(p.shape, lambda i: (0, 0)) for p in params]
    out_specs = [pl.BlockSpec((tr, w), lambda i: (i, 0)) for w, _ in outs]
    out_specs += [pl.BlockSpec(s, lambda i: (0, 0)) for s in accs]
    out_shape = [jax.ShapeDtypeStruct((n_rows, w), dt) for w, dt in outs]
    out_shape += [jax.ShapeDtypeStruct(s, F32) for s in accs]
    res = pl.pallas_call(
        body, name=name, grid=(n_rows // tr,), in_specs=in_specs, out_specs=out_specs, out_shape=out_shape,
        compiler_params=_params(("arbitrary",)))(*rows, *params)
    return res


def _rms(x, g, eps=NORM_EPS):
    return x * lax.rsqrt(jnp.mean(x * x, axis=-1, keepdims=True) + eps) * g


def norm_fwd(x, g, name):
    return rowwise(name, lambda i, xb, gb: _rms(xb, gb), [x], [g], [(x.shape[1], BF16)], tr=_tile(x.shape[0], 256, 16))[0]


def norm_bwd(x, g, dn, resid, name):
    def f(i, xb, dnb, rb, gb):
        _, vjp = jax.vjp(_rms, xb, gb)
        dx, dg = vjp(dnb)
        return dx + rb, dx + rb, dg

    return rowwise(name, f, [x, dn, resid], [g], [(x.shape[1], F32), (x.shape[1], BF16)], [g.shape],
                   tr=_tile(x.shape[0], 256, 16))


def conv_fwd(x, w, b, nb, name):
    rows, width = x.shape
    nc = rows // nb // CHUNK
    tw = _tile(width, 1024, LANES)
    ksz = w.shape[0]

    def body(x_ref, w_ref, b_ref, o_ref, ext_ref):
        c = pl.program_id(2)

        @pl.when(c == 0)
        def _():
            ext_ref[0:SUBLANES, :] = jnp.zeros((SUBLANES, tw), F32)

        xv = x_ref[...]
        ext_ref[SUBLANES:SUBLANES + CHUNK, :] = xv
        acc = jnp.broadcast_to(b_ref[...], (CHUNK, tw))
        for j in range(ksz):
            off = SUBLANES - (ksz - 1) + j
            acc = acc + w_ref[j:j + 1, :] * ext_ref[off:off + CHUNK, :]
        o_ref[...] = acc
        ext_ref[0:SUBLANES, :] = xv[CHUNK - SUBLANES:CHUNK, :]

    return pl.pallas_call(
        body, name=name, grid=(width // tw, nb, nc),
        in_specs=[pl.BlockSpec((CHUNK, tw), lambda j, bb, c: (bb * nc + c, j)),
                  pl.BlockSpec((ksz, tw), lambda j, bb, c: (0, j)),
                  pl.BlockSpec((1, tw), lambda j, bb, c: (0, j))],
        out_specs=pl.BlockSpec((CHUNK, tw), lambda j, bb, c: (bb * nc + c, j)),
        out_shape=jax.ShapeDtypeStruct((rows, width), F32),
        scratch_shapes=[pltpu.VMEM((CHUNK + 2 * SUBLANES, tw), F32)],
        compiler_params=_params(("arbitrary", "arbitrary", "arbitrary")))(x, w, b)


def conv_bwd(dc, x, w, nb, name, resid=None, dx_dtype=BF16):
    rows, width = x.shape
    nc = rows // nb // CHUNK
    tw = _tile(width, 1024, LANES)
    ksz = w.shape[0]
    per = CHUNK // SUBLANES
    has_resid = resid is not None

    def body(*refs):
        if has_resid:
            dc_ref, x_ref, halo_ref, w_ref, r_ref, dx_ref, dw_ref, db_ref, extd_ref, extx_ref = refs
        else:
            dc_ref, x_ref, halo_ref, w_ref, dx_ref, dw_ref, db_ref, extd_ref, extx_ref = refs
        bb = pl.program_id(1)
        step = pl.program_id(2)
        c = nc - 1 - step

        @pl.when(jnp.logical_and(bb == 0, step == 0))
        def _():
            dw_ref[...] = jnp.zeros_like(dw_ref)
            db_ref[...] = jnp.zeros_like(db_ref)

        @pl.when(step == 0)
        def _():
            extd_ref[CHUNK:CHUNK + SUBLANES, :] = jnp.zeros((SUBLANES, tw), F32)

        dcv = dc_ref[...]
        extd_ref[0:CHUNK, :] = dcv
        extx_ref[0:SUBLANES, :] = jnp.where(c == 0, 0.0, halo_ref[...])
        extx_ref[SUBLANES:SUBLANES + CHUNK, :] = x_ref[...]
        dx = jnp.zeros((CHUNK, tw), F32)
        for j in range(ksz):
            up = ksz - 1 - j
            dx = dx + w_ref[j:j + 1, :] * extd_ref[up:up + CHUNK, :]
            off = SUBLANES - (ksz - 1) + j
            dw_ref[j:j + 1, :] += jnp.sum(dcv * extx_ref[off:off + CHUNK, :], axis=0, keepdims=True)
        if has_resid:
            dx = dx + r_ref[...]
        dx_ref[...] = dx.astype(dx_ref.dtype)
        db_ref[...] += jnp.sum(dcv, axis=0, keepdims=True)
        extd_ref[CHUNK:CHUNK + SUBLANES, :] = dcv[0:SUBLANES, :]

    def blk(j, bb, step):
        return (bb * nc + nc - 1 - step, j)

    def halo(j, bb, step):
        return (jnp.maximum((bb * nc + nc - 1 - step) * per - 1, 0), j)

    in_specs = [pl.BlockSpec((CHUNK, tw), blk), pl.BlockSpec((CHUNK, tw), blk), pl.BlockSpec((SUBLANES, tw), halo),
                pl.BlockSpec((ksz, tw), lambda j, bb, step: (0, j))]
    args = [dc, x, x, w]
    if has_resid:
        in_specs.append(pl.BlockSpec((CHUNK, tw), blk))
        args.append(resid)
    return pl.pallas_call(
        body, name=name, grid=(width // tw, nb, nc), in_specs=in_specs,
        out_specs=[pl.BlockSpec((CHUNK, tw), blk), pl.BlockSpec((SUBLANES, tw), lambda j, bb, step: (0, j)),
                   pl.BlockSpec((1, tw), lambda j, bb, step: (0, j))],
        out_shape=[jax.ShapeDtypeStruct((rows, width), dx_dtype), jax.ShapeDtypeStruct((SUBLANES, width), F32),
                   jax.ShapeDtypeStruct((1, width), F32)],
        scratch_shapes=[pltpu.VMEM((CHUNK + 2 * SUBLANES, tw), F32), pltpu.VMEM((CHUNK + 2 * SUBLANES, tw), F32)],
        compiler_params=_params(("arbitrary", "arbitrary", "arbitrary")))(*args)


S5_Q = 4


def _s5_fill_bu(u, bre_ref, bim_ref, xr_ref, xi_ref, ns):
    for s in range(ns):
        ub = u[:, s * LANES:(s + 1) * LANES].astype(BF16)
        bur = _dot(ub, bre_ref[s], ((1,), (0,)))
        bui = _dot(ub, bim_ref[s], ((1,), (0,)))
        for q in range(S5_Q):
            xr_ref[q, pl.ds(s, CHUNK, stride=ns), :] = bur[:, q * LANES:(q + 1) * LANES]
            xi_ref[q, pl.ds(s, CHUNK, stride=ns), :] = bui[:, q * LANES:(q + 1) * LANES]


def _s5_scan(xr_ref, xi_ref, ar_ref, ai_ref, st_ref, ns):
    ar = [ar_ref[q] for q in range(S5_Q)]
    ai = [ai_ref[q] for q in range(S5_Q)]

    def step(t, carry):
        rows = pl.ds(pl.multiple_of(t * ns, ns), ns)
        out = []
        for q in range(S5_Q):
            pr, pi_ = carry[2 * q], carry[2 * q + 1]
            nr = ar[q] * pr - ai[q] * pi_ + xr_ref[q, rows, :]
            ni = ar[q] * pi_ + ai[q] * pr + xi_ref[q, rows, :]
            xr_ref[q, rows, :] = nr
            xi_ref[q, rows, :] = ni
            out += [nr, ni]
        return tuple(out)

    init = []
    for q in range(S5_Q):
        init += [st_ref[0, q], st_ref[1, q]]
    fin = lax.fori_loop(0, CHUNK, step, tuple(init), unroll=2)
    for q in range(S5_Q):
        st_ref[0, q] = fin[2 * q]
        st_ref[1, q] = fin[2 * q + 1]


def s5_fwd(pa, bre, bim, cre, cim, ar, ai, dvec, nb, name):
    rows = pa.shape[0]
    width = pa.shape[1] // 2
    ns = width // LANES
    nc = rows // nb // CHUNK

    def body(u_ref, bre_ref, bim_ref, cre_ref, cim_ref, ar_ref, ai_ref, d_ref, y_ref, g_ref, so_ref, xr_ref, xi_ref, st_ref):
        c = pl.program_id(1)

        @pl.when(c == 0)
        def _():
            st_ref[...] = jnp.zeros_like(st_ref)

        so_ref[...] = st_ref[...]
        u = u_ref[...]
        _s5_fill_bu(u, bre_ref, bim_ref, xr_ref, xi_ref, ns)
        _s5_scan(xr_ref, xi_ref, ar_ref, ai_ref, st_ref, ns)
        for s in range(ns):
            acc = jnp.zeros((CHUNK, LANES), F32)
            for q in range(S5_Q):
                xr = xr_ref[q, pl.ds(s, CHUNK, stride=ns), :].astype(BF16)
                xi = xi_ref[q, pl.ds(s, CHUNK, stride=ns), :].astype(BF16)
                acc = acc + _dot(xr, cre_ref[s, q * LANES:(q + 1) * LANES, :], ((1,), (0,)))
                acc = acc - _dot(xi, cim_ref[s, q * LANES:(q + 1) * LANES, :], ((1,), (0,)))
            cols = slice(s * LANES, (s + 1) * LANES)
            y = acc + d_ref[:, cols] * u[:, cols]
            y_ref[:, cols] = y
            g_ref[:, cols] = _gelu(y).astype(BF16)

    whole3 = lambda a: pl.BlockSpec(a.shape, lambda b_, c: (0, 0, 0))
    return pl.pallas_call(
        body, name=name, grid=(nb, nc),
        in_specs=[pl.BlockSpec((CHUNK, width), lambda b_, c: (b_ * nc + c, 0)), whole3(bre), whole3(bim), whole3(cre),
                  whole3(cim), whole3(ar), whole3(ai), pl.BlockSpec((1, width), lambda b_, c: (0, 0))],
        out_specs=[pl.BlockSpec((CHUNK, width), lambda b_, c: (b_ * nc + c, 0)),
                   pl.BlockSpec((CHUNK, width), lambda b_, c: (b_ * nc + c, 0)),
                   pl.BlockSpec((None, 2, S5_Q, ns, LANES), lambda b_, c: (b_ * nc + c, 0, 0, 0, 0))],
        out_shape=[jax.ShapeDtypeStruct((rows, width), F32), jax.ShapeDtypeStruct((rows, width), BF16),
                   jax.ShapeDtypeStruct((nb * nc, 2, S5_Q, ns, LANES), F32)],
        scratch_shapes=[pltpu.VMEM((S5_Q, CHUNK * ns, LANES), F32), pltpu.VMEM((S5_Q, CHUNK * ns, LANES), F32),
                        pltpu.VMEM((2, S5_Q, ns, LANES), F32)],
        compiler_params=_params(("arbitrary", "arbitrary")))(pa, bre, bim, cre, cim, ar, ai, dvec)


def s5_bwd(pa, dys, states, bre, bim, cre, cim, ar, ai, dvec, nb, name):
    rows = pa.shape[0]
    width = pa.shape[1] // 2
    ns = width // LANES
    nc = rows // nb // CHUNK

    def body(u_ref, dy_ref, sin_ref, bre_ref, bim_ref, cre_ref, cim_ref, ar_ref, ai_ref, d_ref,
             du_ref, dbre_ref, dbim_ref, dcre_ref, dcim_ref, dar_ref, dai_ref, dd_ref,
             xr_ref, xi_ref, lr_ref, li_ref, st_ref, lam_ref):
        bb = pl.program_id(0)
        step_i = pl.program_id(1)

        @pl.when(jnp.logical_and(bb == 0, step_i == 0))
        def _():
            for r in (dbre_ref, dbim_ref, dcre_ref, dcim_ref, dar_ref, dai_ref, dd_ref):
                r[...] = jnp.zeros_like(r)

        @pl.when(step_i == 0)
        def _():
            lam_ref[...] = jnp.zeros_like(lam_ref)

        u = u_ref[...]
        dy = dy_ref[...]
        st_ref[...] = sin_ref[...]
        _s5_fill_bu(u, bre_ref, bim_ref, xr_ref, xi_ref, ns)
        _s5_scan(xr_ref, xi_ref, ar_ref, ai_ref, st_ref, ns)
        dd_ref[...] += jnp.sum(dy * u, axis=0, keepdims=True)
        for s in range(ns):
            dyb = dy[:, s * LANES:(s + 1) * LANES].astype(BF16)
            gr = _dot(dyb, cre_ref[s], ((1,), (1,)))
            gi = -_dot(dyb, cim_ref[s], ((1,), (1,)))
            for q in range(S5_Q):
                lr_ref[q, pl.ds(s, CHUNK, stride=ns), :] = gr[:, q * LANES:(q + 1) * LANES]
                li_ref[q, pl.ds(s, CHUNK, stride=ns), :] = gi[:, q * LANES:(q + 1) * LANES]
                xr = xr_ref[q, pl.ds(s, CHUNK, stride=ns), :].astype(BF16)
                xi = xi_ref[q, pl.ds(s, CHUNK, stride=ns), :].astype(BF16)
                dcre_ref[s, q * LANES:(q + 1) * LANES, :] += _dot(xr, dyb, ((0,), (0,)))
                dcim_ref[s, q * LANES:(q + 1) * LANES, :] -= _dot(xi, dyb, ((0,), (0,)))
        ar = [ar_ref[q] for q in range(S5_Q)]
        ai = [ai_ref[q] for q in range(S5_Q)]

        def one(t_rows, p_r, p_i, carry):
            out = []
            for q in range(S5_Q):
                l_r, l_i, da_r, da_i = carry[4 * q:4 * q + 4]
                n_r = lr_ref[q, t_rows, :] + ar[q] * l_r + ai[q] * l_i
                n_i = li_ref[q, t_rows, :] + ar[q] * l_i - ai[q] * l_r
                lr_ref[q, t_rows, :] = n_r
                li_ref[q, t_rows, :] = n_i
                xpr, xpi = p_r(q), p_i(q)
                out += [n_r, n_i, da_r + n_r * xpr + n_i * xpi, da_i + n_i * xpr - n_r * xpi]
            return tuple(out)

        def step(k, carry):
            t = CHUNK - 1 - k
            t_rows = pl.ds(pl.multiple_of(t * ns, ns), ns)
            p_rows = pl.ds(pl.multiple_of((t - 1) * ns, ns), ns)
            return one(t_rows, lambda q: xr_ref[q, p_rows, :], lambda q: xi_ref[q, p_rows, :], carry)

        init = []
        zero = jnp.zeros((ns, LANES), F32)
        for q in range(S5_Q):
            init += [lam_ref[0, q], lam_ref[1, q], zero, zero]
        carry = lax.fori_loop(0, CHUNK - 1, step, tuple(init), unroll=2)
        carry = one(pl.ds(0, ns), lambda q: sin_ref[0, q], lambda q: sin_ref[1, q], carry)
        for q in range(S5_Q):
            lam_ref[0, q] = carry[4 * q]
            lam_ref[1, q] = carry[4 * q + 1]
            dar_ref[q] += carry[4 * q + 2]
            dai_ref[q] += carry[4 * q + 3]
        for s in range(ns):
            cols = slice(s * LANES, (s + 1) * LANES)
            ub = u[:, cols].astype(BF16)
            acc = d_ref[:, cols] * dy[:, cols]
            for q in range(S5_Q):
                qs = slice(q * LANES, (q + 1) * LANES)
                lr = lr_ref[q, pl.ds(s, CHUNK, stride=ns), :].astype(BF16)
                li = li_ref[q, pl.ds(s, CHUNK, stride=ns), :].astype(BF16)
                dbre_ref[s, :, qs] += _dot(ub, lr, ((0,), (0,)))
                dbim_ref[s, :, qs] += _dot(ub, li, ((0,), (0,)))
                acc = acc + _dot(lr, bre_ref[s, :, qs], ((1,), (1,))) + _dot(li, bim_ref[s, :, qs], ((1,), (1,)))
            du_ref[:, cols] = acc.astype(du_ref.dtype)

    whole3 = lambda a: pl.BlockSpec(a.shape, lambda b_, c: (0, 0, 0))
    rowblk = pl.BlockSpec((CHUNK, width), lambda b_, c: (b_ * nc + nc - 1 - c, 0))
    scr = pltpu.VMEM((S5_Q, CHUNK * ns, LANES), F32)
    return pl.pallas_call(
        body, name=name, grid=(nb, nc),
        in_specs=[rowblk, rowblk,
                  pl.BlockSpec((None, 2, S5_Q, ns, LANES), lambda b_, c: (b_ * nc + nc - 1 - c, 0, 0, 0, 0)),
                  whole3(bre), whole3(bim), whole3(cre), whole3(cim), whole3(ar), whole3(ai),
                  pl.BlockSpec((1, width), lambda b_, c: (0, 0))],
        out_specs=[rowblk, whole3(bre), whole3(bim), whole3(cre), whole3(cim), whole3(ar), whole3(ai),
                   pl.BlockSpec((1, width), lambda b_, c: (0, 0))],
        out_shape=[jax.ShapeDtypeStruct((rows, width), BF16), jax.ShapeDtypeStruct(bre.shape, F32),
                   jax.ShapeDtypeStruct(bim.shape, F32), jax.ShapeDtypeStruct(cre.shape, F32),
                   jax.ShapeDtypeStruct(cim.shape, F32), jax.ShapeDtypeStruct(ar.shape, F32),
                   jax.ShapeDtypeStruct(ai.shape, F32), jax.ShapeDtypeStruct((1, width), F32)],
        scratch_shapes=[scr, scr, scr, scr, pltpu.VMEM((2, S5_Q, ns, LANES), F32), pltpu.VMEM((2, S5_Q, ns, LANES), F32)],
        compiler_params=_params(("arbitrary", "arbitrary")))(pa, dys, states, bre, bim, cre, cim, ar, ai, dvec)


def _s5_discretize(lam_re, lam_im, log_dt, b_re, b_im):
    dt = jnp.exp(log_dt)[:, None]
    mag = jnp.exp(lam_re * dt)
    ar, ai = mag * jnp.cos(lam_im * dt), mag * jnp.sin(lam_im * dt)
    den = lam_re * lam_re + lam_im * lam_im
    qr = ((ar - 1.0) * lam_re + ai * lam_im) / den
    qi = (ai * lam_re - (ar - 1.0) * lam_im) / den
    bbr = qr[..., None] * b_re - qi[..., None] * b_im
    bbi = qr[..., None] * b_im + qi[..., None] * b_re
    return ar, ai, bbr, bbi


def _s5_expand(ar, ai, bbr, bbi, c_re, c_im):
    g, p, h = bbr.shape
    gps = LANES // h
    ns = g // gps
    eye = jnp.eye(gps, dtype=F32)

    def bexp(b):
        return jnp.einsum("sgph,gk->sghkp", b.reshape(ns, gps, p, h), eye).reshape(ns, gps * h, gps * p)

    def cexp(c):
        return jnp.einsum("sghp,gk->sgpkh", c.reshape(ns, gps, h, p), eye).reshape(ns, gps * p, gps * h)

    def aexp(a):
        return a.reshape(ns, S5_Q, LANES).transpose(1, 0, 2)

    return (bexp(bbr).astype(BF16), bexp(bbi).astype(BF16), cexp(c_re).astype(BF16), cexp(c_im).astype(BF16),
            aexp(ar), aexp(ai))


def _s5_contract(dbre, dbim, dcre, dcim, dar, dai, g, p, h):
    gps = LANES // h
    ns = g // gps
    eye = jnp.eye(gps, dtype=F32)
    bcon = lambda d: jnp.einsum("sghkp,gk->sgph", d.reshape(ns, gps, h, gps, p), eye).reshape(g, p, h)
    ccon = lambda d: jnp.einsum("sgpkh,gk->sghp", d.reshape(ns, gps, p, gps, h), eye).reshape(g, h, p)
    acon = lambda d: d.transpose(1, 0, 2).reshape(g, p)
    return bcon(dbre), bcon(dbim), ccon(dcre), ccon(dcim), acon(dar), acon(dai)


def _ml_proj_tile(cpre, xb, wq, wk, wv, gq, gk, gv):
    xc = _silu(cpre)
    q = _dot_nn(xc, wq)
    k = _dot_nn(xc, wk)
    v = _dot_nn(xb, wv)
    return q, k, v, _dot_nn(q, gq) + _dot_nn(k, gk) + _dot_nn(v, gv)


def ml_proj_fwd(cpre, xb, wq, wk, wv, gq, gk, gv, name):
    rows, width = cpre.shape
    nblk = width // LANES
    tr = _tile(rows, 1088, 16)

    def body(c_ref, x_ref, wq_ref, wk_ref, wv_ref, gq_ref, gk_ref, gv_ref, q_ref, k_ref, v_ref, g_ref):
        j = pl.program_id(1)
        q, k, v, g = _ml_proj_tile(c_ref[...], x_ref[...], wq_ref[...], wk_ref[...], wv_ref[...],
                                   gq_ref[...], gk_ref[...], gv_ref[...])
        q_ref[...] = q
        k_ref[...] = k
        v_ref[...] = v

        @pl.when(j == 0)
        def _():
            g_ref[...] = jnp.zeros_like(g_ref)

        g_ref[...] += g

    rb = pl.BlockSpec((tr, LANES), lambda i, j: (i, j))
    wb = pl.BlockSpec((None, LANES, LANES), lambda i, j: (j, 0, 0))
    return pl.pallas_call(
        body, name=name, grid=(rows // tr, nblk), in_specs=[rb, rb, wb, wb, wb, wb, wb, wb],
        out_specs=[rb, rb, rb, pl.BlockSpec((tr, LANES), lambda i, j: (i, 0))],
        out_shape=[jax.ShapeDtypeStruct((rows, width), F32)] * 3 + [jax.ShapeDtypeStruct((rows, LANES), F32)],
        compiler_params=_params(("arbitrary", "arbitrary")))(cpre, xb, wq, wk, wv, gq, gk, gv)


def ml_proj_bwd(cpre, xb, wq, wk, wv, gq, gk, gv, dq, dk, dv, dg, dcp_extra, name):
    rows, width = cpre.shape
    nblk = width // LANES
    tr = _tile(rows, 1088, 16)

    def body(c_ref, x_ref, wq_ref, wk_ref, wv_ref, gq_ref, gk_ref, gv_ref, dq_ref, dk_ref, dv_ref, dg_ref, e_ref,
             dc_ref, dx_ref, *dw_refs):
        i = pl.program_id(1)
        _, vjp = jax.vjp(_ml_proj_tile, c_ref[...], x_ref[...], wq_ref[...], wk_ref[...], wv_ref[...],
                         gq_ref[...], gk_ref[...], gv_ref[...])
        grads = vjp((dq_ref[...], dk_ref[...], dv_ref[...], dg_ref[...]))
        dc_ref[...] = grads[0] + e_ref[...]
        dx_ref[...] = grads[1]

        @pl.when(i == 0)
        def _():
            for r in dw_refs:
                r[...] = jnp.zeros_like(r)

        for r, gval in zip(dw_refs, grads[2:]):
            r[...] += gval

    rb = pl.BlockSpec((tr, LANES), lambda j, i: (i, j))
    wb = pl.BlockSpec((None, LANES, LANES), lambda j, i: (j, 0, 0))
    gb = pl.BlockSpec((tr, LANES), lambda j, i: (i, 0))
    wshape = jax.ShapeDtypeStruct((nblk, LANES, LANES), F32)
    return pl.pallas_call(
        body, name=name, grid=(nblk, rows // tr), in_specs=[rb, rb, wb, wb, wb, wb, wb, wb, rb, rb, rb, gb, rb],
        out_specs=[rb, rb] + [wb] * 6,
        out_shape=[jax.ShapeDtypeStruct((rows, width), F32)] * 2 + [wshape] * 6,
        compiler_params=_params(("arbitrary", "arbitrary")))(cpre, xb, wq, wk, wv, gq, gk, gv, dq, dk, dv, dg, dcp_extra)


def _ml_gates_tile(gl, bg, nh):
    x = gl + bg
    bcum = _dot(_tri(CHUNK), _log_sigmoid(x), ((1,), (0,)), precision=HI)
    lane = lax.broadcasted_iota(jnp.int32, x.shape, 1)
    return jnp.where(lane < nh, x, jnp.where(lane < 2 * nh, bcum, 0.0))


def _ml_core_tile(q, k, v, colg, rowg, cpre, zb, nw, sk, cst, nst, m_prev):
    c, dh = q.shape
    igc, bc = _lane_pick(colg, 0), _lane_pick(colg, 1)
    igr, br = _row_pick(rowg, 0), _row_pick(rowg, 1)
    causal = _tri(c) > 0
    dmat = jnp.where(causal, bc - br + igr, -jnp.inf)
    inter = bc + m_prev
    mt = lax.stop_gradient(jnp.maximum(inter, jnp.max(dmat, axis=1, keepdims=True)))
    wt = jnp.exp(dmat - mt)
    w_prev = jnp.exp(inter - mt)
    qs = q * (dh ** -0.5)
    s = _dot_nt(qs, k) * wt
    num = _dot_nn(s, v) + w_prev * _dot_nn(qs, cst)
    den = jnp.sum(s, axis=1, keepdims=True) + w_prev * jnp.sum(qs * nst, axis=1, keepdims=True)
    h = num * (1.0 / jnp.maximum(jnp.abs(den), jnp.exp(-mt)))
    last = (lax.broadcasted_iota(jnp.int32, (c, 1), 0) == c - 1).astype(F32)
    blast = jnp.sum(bc * last, axis=0, keepdims=True)
    g = blast - bc + igc
    m_new = lax.stop_gradient(jnp.maximum(blast + m_prev, jnp.max(g, axis=0, keepdims=True)))
    decay = jnp.exp(blast + m_prev - m_new)
    wk = jnp.exp(g - m_new) * k
    c_new = decay * cst + _dot_tn(wk, v)
    n_new = decay * nst + jnp.sum(wk, axis=0, keepdims=True)
    mu = jnp.mean(h, axis=1, keepdims=True)
    hc = h - mu
    var = jnp.mean(hc * hc, axis=1, keepdims=True)
    out = hc * lax.rsqrt(var + HEAD_NORM_EPS) * nw + sk * _silu(cpre)
    return out * _silu(zb), c_new, n_new, m_new


def _ml_core_specs(nc, dh, rev):
    ch = (lambda c: nc - 1 - c) if rev else (lambda c: c)
    rb = pl.BlockSpec((CHUNK, dh), lambda b_, c, h: (b_ * nc + ch(c), h))
    colb = pl.BlockSpec((None, CHUNK, 2), lambda b_, c, h: (h, b_ * nc + ch(c), 0))
    rowb = pl.BlockSpec((None, None, 2, CHUNK), lambda b_, c, h: (b_ * nc + ch(c), h, 0, 0))
    pb = pl.BlockSpec((1, dh), lambda b_, c, h: (0, h))
    cb = pl.BlockSpec((None, None, dh, dh), lambda b_, c, h: (b_ * nc + ch(c), h, 0, 0))
    nb_ = pl.BlockSpec((None, None, 1, dh), lambda b_, c, h: (b_ * nc + ch(c), h, 0, 0))
    mb = pl.BlockSpec((None, None, 1, 1), lambda b_, c, h: (b_ * nc + ch(c), h, 0, 0))
    return rb, colb, rowb, pb, cb, nb_, mb


def ml_core_fwd(q, k, v, colg, rowg, cpre, zb, nw, sk, nb, nh, name):
    rows, width = q.shape
    dh = width // nh
    nc = rows // nb // CHUNK
    rb, colb, rowb, pb, cb, nb_, mb = _ml_core_specs(nc, dh, False)

    def body(q_ref, k_ref, v_ref, col_ref, row_ref, c_ref, z_ref, nw_ref, sk_ref, y_ref, cs_ref, ns_ref, ms_ref,
             cst_ref, nst_ref, mst_ref):
        c = pl.program_id(1)
        h = pl.program_id(2)

        @pl.when(c == 0)
        def _():
            cst_ref[h] = jnp.zeros((dh, dh), F32)
            nst_ref[h] = jnp.zeros((1, dh), F32)
            mst_ref[h] = jnp.zeros((1, 1), F32)

        cst, nst, m_prev = cst_ref[h], nst_ref[h], mst_ref[h]
        cs_ref[...] = cst
        ns_ref[...] = nst
        ms_ref[...] = m_prev
        y, c_new, n_new, m_new = _ml_core_tile(q_ref[...], k_ref[...], v_ref[...], col_ref[...], row_ref[...],
                                               c_ref[...], z_ref[...], nw_ref[...], sk_ref[...], cst, nst, m_prev)
        y_ref[...] = y.astype(BF16)
        cst_ref[h] = c_new
        nst_ref[h] = n_new
        mst_ref[h] = m_new

    nbc = nb * nc
    return pl.pallas_call(
        body, name=name, grid=(nb, nc, nh), in_specs=[rb, rb, rb, colb, rowb, rb, rb, pb, pb],
        out_specs=[rb, cb, nb_, mb],
        out_shape=[jax.ShapeDtypeStruct((rows, width), BF16), jax.ShapeDtypeStruct((nbc, nh, dh, dh), F32),
                   jax.ShapeDtypeStruct((nbc, nh, 1, dh), F32), jax.ShapeDtypeStruct((nbc, nh, 1, 1), F32)],
        scratch_shapes=[pltpu.VMEM((nh, dh, dh), F32), pltpu.VMEM((nh, 1, dh), F32), pltpu.VMEM((nh, 1, 1), F32)],
        compiler_params=_params(("arbitrary", "arbitrary", "arbitrary")))(q, k, v, colg, rowg, cpre, zb, nw, sk)


def ml_core_bwd(q, k, v, colg, rowg, cpre, zb, nw, sk, cs, ns, ms, dy, nb, nh, name):
    rows, width = q.shape
    dh = width // nh
    nc = rows // nb // CHUNK
    rb, colb, rowb, pb, cb, nb_, mb = _ml_core_specs(nc, dh, True)

    def body(q_ref, k_ref, v_ref, col_ref, row_ref, c_ref, z_ref, nw_ref, sk_ref, cs_ref, ns_ref, ms_ref, dy_ref,
             dq_ref, dk_ref, dv_ref, dc_ref, dz_ref, dcol_ref, drow_ref, dnw_ref, dsk_ref, dcst_ref, dnst_ref):
        bb = pl.program_id(0)
        step = pl.program_id(1)
        h = pl.program_id(2)

        @pl.when(jnp.logical_and(bb == 0, jnp.logical_and(step == 0, h == 0)))
        def _():
            dnw_ref[...] = jnp.zeros_like(dnw_ref)
            dsk_ref[...] = jnp.zeros_like(dsk_ref)

        @pl.when(step == 0)
        def _():
            dcst_ref[h] = jnp.zeros((dh, dh), F32)
            dnst_ref[h] = jnp.zeros((1, dh), F32)

        m_prev = ms_ref[...]

        def f(*a):
            return _ml_core_tile(*a, m_prev)[:3]

        _, vjp = jax.vjp(f, q_ref[...], k_ref[...], v_ref[...], col_ref[...], row_ref[...], c_ref[...], z_ref[...],
                         nw_ref[...], sk_ref[...], cs_ref[...], ns_ref[...])
        g = vjp((dy_ref[...], dcst_ref[h], dnst_ref[h]))
        dq_ref[...] = g[0]
        dk_ref[...] = g[1]
        dv_ref[...] = g[2]
        dcol_ref[...] = g[3]
        drow_ref[...] = g[4]
        dc_ref[...] = g[5]
        dz_ref[...] = g[6].astype(dz_ref.dtype)
        dnw_ref[h] += g[7]
        dsk_ref[h] += g[8]
        dcst_ref[h] = g[9]
        dnst_ref[h] = g[10]

    nbc = nb * nc
    accb = pl.BlockSpec((nh, 1, dh), lambda b_, c, h: (0, 0, 0))
    return pl.pallas_call(
        body, name=name, grid=(nb, nc, nh), in_specs=[rb, rb, rb, colb, rowb, rb, rb, pb, pb, cb, nb_, mb, rb],
        out_specs=[rb, rb, rb, rb, rb, colb, rowb, accb, accb],
        out_shape=[jax.ShapeDtypeStruct((rows, width), F32)] * 4 + [jax.ShapeDtypeStruct((rows, width), BF16)]
        + [jax.ShapeDtypeStruct(colg.shape, F32), jax.ShapeDtypeStruct(rowg.shape, F32),
           jax.ShapeDtypeStruct((nh, 1, dh), F32), jax.ShapeDtypeStruct((nh, 1, dh), F32)],
        scratch_shapes=[pltpu.VMEM((nh, dh, dh), F32), pltpu.VMEM((nh, 1, dh), F32)],
        compiler_params=_params(("arbitrary", "arbitrary", "arbitrary")))(
            q, k, v, colg, rowg, cpre, zb, nw, sk, cs, ns, ms, dy)


def _ssd_dt_tile(dtr, bias, alog):
    dt = _softplus(dtr + bias)
    cum = _dot(_tri(CHUNK), dt * (-jnp.exp(alog)), ((1,), (0,)), precision=HI)
    return dt, cum


def _ssd_tile(xcs, bmc, cmc, cols, rows_, z, dvec, gn, states, hpg):
    npair = hpg // 2
    hd = SSD_HEAD_DIM
    xs = [_silu(x) for x in xcs]
    bm, cm = _silu(bmc), _silu(cmc)
    cb = _dot_nt(cm, bm)
    causal = _tri(CHUNK) > 0
    lane_lo = lax.broadcasted_iota(jnp.int32, (1, 2 * hd), 1) < hd
    lastsel = (lax.broadcasted_iota(jnp.int32, (CHUNK, 1), 0) == CHUNK - 1).astype(F32)
    heads = []
    for r in range(hpg):
        dtc, cumc = _lane_pick(cols, r), _lane_pick(cols, hpg + r)
        dtrow, cumr = _row_pick(rows_, r), _row_pick(rows_, hpg + r)
        w = cb * jnp.exp(jnp.where(causal, cumc - cumr, -jnp.inf)) * dtrow
        last = jnp.sum(cumc * lastsel, axis=0, keepdims=True)
        heads.append((w, jnp.exp(cumc), jnp.exp(last - cumc) * dtc, jnp.exp(last)))
    ys, new_states = [], []
    for j in range(npair):
        (wa, ea, da, la), (wb, eb, db, lb) = heads[2 * j], heads[2 * j + 1]
        yi = jnp.where(lane_lo, _dot_nn(wa, xs[j]), _dot_nn(wb, xs[j]))
        ys.append(yi + jnp.where(lane_lo, ea, eb) * _dot_nn(cm, states[j]))
        xd = xs[j] * jnp.where(lane_lo, da, db)
        new_states.append(jnp.where(lane_lo, la, lb) * states[j] + _dot_tn(bm, xd))
    y = jnp.concatenate(ys, axis=1) + dvec * jnp.concatenate(xs, axis=1)
    yg = y * _silu(z)
    yn = yg * lax.rsqrt(jnp.mean(yg * yg, axis=1, keepdims=True) + NORM_EPS) * gn
    return yn, new_states


def _ssd_specs(nc, hpg, ng, rev):
    npair = hpg // 2
    gw = hpg * SSD_HEAD_DIM
    xblocks = ng * npair
    ch = (lambda c: nc - 1 - c) if rev else (lambda c: c)
    xs = [pl.BlockSpec((CHUNK, LANES), functools.partial(lambda b_, c, g, jj: (b_ * nc + ch(c), g * npair + jj), jj=j))
          for j in range(npair)]
    bmb = pl.BlockSpec((CHUNK, SSD_STATE), lambda b_, c, g: (b_ * nc + ch(c), xblocks + g))
    cmb = pl.BlockSpec((CHUNK, SSD_STATE), lambda b_, c, g: (b_ * nc + ch(c), xblocks + ng + g))
    colb = pl.BlockSpec((None, CHUNK, 2 * hpg), lambda b_, c, g: (g, b_ * nc + ch(c), 0))
    rowb = pl.BlockSpec((None, None, 2 * hpg, CHUNK), lambda b_, c, g: (b_ * nc + ch(c), g, 0, 0))
    zb = pl.BlockSpec((CHUNK, gw), lambda b_, c, g: (b_ * nc + ch(c), g))
    pb = pl.BlockSpec((1, gw), lambda b_, c, g: (0, g))
    sb = pl.BlockSpec((None, None, npair, SSD_STATE, 2 * SSD_HEAD_DIM), lambda b_, c, g: (b_ * nc + ch(c), g, 0, 0, 0))
    return xs, bmb, cmb, colb, rowb, zb, pb, sb


def ssd_core_fwd(cpre, cols, rows_, z, dvec, gn, nb, hpg, name):
    rows = cpre.shape[0]
    inner = z.shape[1]
    ng = inner // (hpg * SSD_HEAD_DIM)
    npair = hpg // 2
    nc = rows // nb // CHUNK
    xs, bmb, cmb, colb, rowb, zb, pb, sb = _ssd_specs(nc, hpg, ng, False)

    def body(*refs):
        x_refs = refs[:npair]
        bm_ref, cm_ref, col_ref, row_ref, z_ref, d_ref, gn_ref, y_ref, so_ref, st_ref = refs[npair:]
        c = pl.program_id(1)
        g = pl.program_id(2)

        @pl.when(c == 0)
        def _():
            st_ref[g] = jnp.zeros((npair, SSD_STATE, 2 * SSD_HEAD_DIM), F32)

        so_ref[...] = st_ref[g]
        states = [st_ref[g, j] for j in range(npair)]
        yn, new_states = _ssd_tile([r[...] for r in x_refs], bm_ref[...], cm_ref[...], col_ref[...], row_ref[...],
                                   z_ref[...], d_ref[...], gn_ref[...], states, hpg)
        y_ref[...] = yn.astype(BF16)
        for j in range(npair):
            st_ref[g, j] = new_states[j]

    return pl.pallas_call(
        body, name=name, grid=(nb, nc, ng), in_specs=xs + [bmb, cmb, colb, rowb, zb, pb, pb],
        out_specs=[zb, sb],
        out_shape=[jax.ShapeDtypeStruct((rows, inner), BF16),
                   jax.ShapeDtypeStruct((nb * nc, ng, npair, SSD_STATE, 2 * SSD_HEAD_DIM), F32)],
        scratch_shapes=[pltpu.VMEM((ng, npair, SSD_STATE, 2 * SSD_HEAD_DIM), F32)],
        compiler_params=_params(("arbitrary", "arbitrary", "arbitrary")))(
            *([cpre] * npair), cpre, cpre, cols, rows_, z, dvec, gn)


def ssd_core_bwd(cpre, cols, rows_, z, dvec, gn, states, dyn, nb, hpg, name):
    rows = cpre.shape[0]
    inner = z.shape[1]
    gw = hpg * SSD_HEAD_DIM
    ng = inner // gw
    npair = hpg // 2
    nc = rows // nb // CHUNK
    xs, bmb, cmb, colb, rowb, zb, pb, sb = _ssd_specs(nc, hpg, ng, True)

    def body(*refs):
        x_refs = refs[:npair]
        (bm_ref, cm_ref, col_ref, row_ref, z_ref, d_ref, gn_ref, s_ref, dy_ref,
         dx_ref, dbm_ref, dcm_ref, dcol_ref, drow_ref, dz_ref, dd_ref, dgn_ref, dst_ref) = refs[npair:]
        bb = pl.program_id(0)
        step = pl.program_id(1)
        g = pl.program_id(2)

        @pl.when(jnp.logical_and(bb == 0, jnp.logical_and(step == 0, g == 0)))
        def _():
            dd_ref[...] = jnp.zeros_like(dd_ref)
            dgn_ref[...] = jnp.zeros_like(dgn_ref)

        @pl.when(step == 0)
        def _():
            dst_ref[g] = jnp.zeros((npair, SSD_STATE, 2 * SSD_HEAD_DIM), F32)

        def f(xcs, bmc, cmc, cv, rv, zv, dv_, gv, sts):
            return _ssd_tile(xcs, bmc, cmc, cv, rv, zv, dv_, gv, sts, hpg)

        _, vjp = jax.vjp(f, [r[...] for r in x_refs], bm_ref[...], cm_ref[...], col_ref[...], row_ref[...], z_ref[...],
                         d_ref[...], gn_ref[...], [s_ref[j] for j in range(npair)])
        gr = vjp((dy_ref[...], [dst_ref[g, j] for j in range(npair)]))
        dx_ref[...] = jnp.concatenate(gr[0], axis=1)
        dbm_ref[...] = gr[1]
        dcm_ref[...] = gr[2]
        dcol_ref[...] = gr[3]
        drow_ref[...] = gr[4]
        dz_ref[...] = gr[5].astype(dz_ref.dtype)
        dd_ref[g] += gr[6]
        dgn_ref[g] += gr[7]
        for j in range(npair):
            dst_ref[g, j] = gr[8][j]

    ch = lambda c: nc - 1 - c
    nblk = pl.BlockSpec((CHUNK, SSD_STATE), lambda b_, c, g: (b_ * nc + ch(c), g))
    accb = pl.BlockSpec((ng, 1, gw), lambda b_, c, g: (0, 0, 0))
    return pl.pallas_call(
        body, name=name, grid=(nb, nc, ng), in_specs=xs + [bmb, cmb, colb, rowb, zb, pb, pb, sb, zb],
        out_specs=[zb, nblk, nblk, colb, rowb, zb, accb, accb],
        out_shape=[jax.ShapeDtypeStruct((rows, inner), F32), jax.ShapeDtypeStruct((rows, ng * SSD_STATE), F32),
                   jax.ShapeDtypeStruct((rows, ng * SSD_STATE), F32), jax.ShapeDtypeStruct(cols.shape, F32),
                   jax.ShapeDtypeStruct(rows_.shape, F32), jax.ShapeDtypeStruct((rows, inner), BF16),
                   jax.ShapeDtypeStruct((ng, 1, gw), F32), jax.ShapeDtypeStruct((ng, 1, gw), F32)],
        scratch_shapes=[pltpu.VMEM((ng, npair, SSD_STATE, 2 * SSD_HEAD_DIM), F32)],
        compiler_params=_params(("arbitrary", "arbitrary", "arbitrary")))(
            *([cpre] * npair), cpre, cpre, cols, rows_, z, dvec, gn, states, dyn)


def _hw_expand(w):
    n, bi, _ = w.shape
    per = LANES // bi
    eye = jnp.eye(per, dtype=F32)
    return jnp.einsum("jbio,bc->jbico", w.reshape(n // per, per, bi, bi), eye).reshape(n // per, LANES, LANES)


def _hw_contract(d, bi=QKV_BLOCK):
    per = LANES // bi
    eye = jnp.eye(per, dtype=F32)
    return jnp.einsum("jbico,bc->jbio", d.reshape(d.shape[0], per, bi, per, bi), eye).reshape(-1, bi, bi)


def _wg_expand(wg, width):
    pad = jnp.pad(wg, ((0, 0), (0, LANES - wg.shape[1])))
    return [pad[i * width:(i + 1) * width].reshape(width // LANES, LANES, LANES) for i in range(3)]


def _wg_contract(dgs, ngate):
    return jnp.concatenate([d[:, :, :ngate].reshape(-1, ngate) for d in dgs], axis=0)


def _pad_lanes(a):
    return jnp.pad(a, ((0, 0), (0, LANES - a.shape[1])))


def _pairs_to_layouts(first, second, ngrp, per, nbc):
    rows = first.shape[0]
    both = jnp.concatenate([first.reshape(rows, ngrp, per), second.reshape(rows, ngrp, per)], axis=2)
    return both.transpose(1, 0, 2), both.reshape(nbc, CHUNK, ngrp, 2 * per).transpose(0, 2, 3, 1)


def _layouts_to_pairs(dcols, drows, ngrp, per):
    rows = dcols.shape[1]
    both = dcols.transpose(1, 0, 2) + drows.transpose(0, 3, 1, 2).reshape(rows, ngrp, 2 * per)
    return both[:, :, :per].reshape(rows, ngrp * per), both[:, :, per:].reshape(rows, ngrp * per)


def _local_step(x, target, bw, sp):
    nb, seq, d = x.shape
    nh, hpg = MLSTM_HEADS, SSD_HPG
    t_len = N_META + seq
    nc = -(-t_len // CHUNK)
    tp = nc * CHUNK
    rows = nb * tp
    nbc = nb * nc
    meta = sp["meta_tokens"]
    h0 = jnp.concatenate([jnp.broadcast_to(meta[None], (nb, N_META, d)), x, jnp.zeros((nb, tp - t_len, d), F32)], axis=1)
    h0 = h0.reshape(rows, d)
    tgt = jnp.pad(target, ((0, 0), (N_META, tp - t_len), (0, 0))).reshape(rows, d)

    n0 = norm_fwd(h0, sp["ab_norm"], "norm0")
    pa = mm(n0, bw["W0a"], "nn", "mm_pa")
    xb = mm(n0, bw["W0xb"], "nn", "mm_xb")
    zb = mm(n0, bw["W0zb"], "nn", "mm_zb")
    s5w = pa.shape[1] // 2
    mlw = xb.shape[1]
    s5_args = (sp["s5_lambda_re"], sp["s5_lambda_im"], sp["s5_log_dt"].reshape(-1), sp["s5_b_re"], sp["s5_b_im"])
    (ar, ai, bbr, bbi), s5_disc_vjp = jax.vjp(_s5_discretize, *s5_args)
    sg, spn, shh = bbr.shape
    bre, bim, cre, cim, are, aie = _s5_expand(ar, ai, bbr, bbi, sp["s5_c_re"], sp["s5_c_im"])
    ys5, gb, s5st = s5_fwd(pa, bre, bim, cre, cim, are, aie, sp["s5_d"], nb, "s5_fwd")
    tglu = mm(gb, bw["glu"], "nn", "mm_glu")

    def glu_tile(ys, tt, za, gbias):
        return _gelu(ys) * _sigmoid(tt + gbias) * _silu(za)

    ya = rowwise("glu_fwd", lambda i, ys, tt, pab, gbias: glu_tile(ys, tt, pab[:, s5w:], gbias),
                 [ys5, tglu, pa], [sp["s5_glu_b"]], [(s5w, BF16)], tr=_tile(rows, 256, 16))[0]

    cpre0 = conv_fwd(xb, sp["ml_conv_w"], sp["ml_conv_b"], nb, "ml_conv_fwd")
    wq_e, wk_e, wv_e = _hw_expand(sp["ml_wq"]), _hw_expand(sp["ml_wk"]), _hw_expand(sp["ml_wv"])
    gq, gk, gv = _wg_expand(sp["ml_w_gate"], mlw)
    q, k, v, gl = ml_proj_fwd(cpre0, xb, wq_e, wk_e, wv_e, gq, gk, gv, "ml_proj_fwd")
    bgate = _pad_lanes(sp["ml_b_gate"])
    gout = rowwise("ml_gates_fwd", lambda i, g_, b_: _ml_gates_tile(g_, b_, nh), [gl], [bgate], [(LANES, F32)], tr=CHUNK)[0]
    colg, rowg = _pairs_to_layouts(gout[:, :nh], gout[:, nh:2 * nh], nh, 1, nbc)
    yb, ml_cs, ml_ns, ml_ms = ml_core_fwd(q, k, v, colg, rowg, cpre0, zb, sp["ml_norm"], sp["ml_skip"], nb, nh, "ml_core_fwd")
    h1 = mm(ya, bw["Wo0a"], "nn", "mm_out0a", resid=h0)
    h1 = mm(yb, bw["Wo0b"], "nn", "mm_out0b", resid=h1)

    n1 = norm_fwd(h1, sp["ssd_norm"], "norm1")
    z1 = mm(n1, bw["W1z"], "nn", "mm_z1")
    xbc = mm(n1, bw["W1x"], "nn", "mm_xbc")
    dtr = mm(n1, bw["W1dt"], "nn", "mm_dt")
    inner = z1.shape[1]
    ng = inner // (hpg * SSD_HEAD_DIM)
    nhd = ng * hpg
    cpre1 = conv_fwd(xbc, sp["ssd_conv_w"], sp["ssd_conv_b"], nb, "ssd_conv_fwd")
    dt_bias, a_log = _pad_lanes(sp["ssd_dt_bias"]), _pad_lanes(sp["ssd_a_log"])
    dt, cum = rowwise("ssd_dt_fwd", lambda i, r_, b_, a_: _ssd_dt_tile(r_, b_, a_), [dtr], [dt_bias, a_log],
                      [(LANES, F32), (LANES, F32)], tr=CHUNK)
    cols, rws = _pairs_to_layouts(dt[:, :nhd], cum[:, :nhd], ng, hpg, nbc)
    dvec = jnp.repeat(sp["ssd_d"], SSD_HEAD_DIM, axis=1)
    yn, ssd_st = ssd_core_fwd(cpre1, cols, rws, z1, dvec, sp["ssd_gnorm"], nb, hpg, "ssd_core_fwd")
    h2 = mm(yn, bw["Wo1"], "nn", "mm_out1", resid=h1)

    tr_l = _tile(tp, 256, 16)
    per_ex = tp // tr_l

    def loss_tile(i, hb, tb, gfn):
        tpos = (i % per_ex) * tr_l + lax.broadcasted_iota(jnp.int32, (tr_l, 1), 0)
        mask = jnp.logical_and(tpos >= N_META, tpos < t_len).astype(F32)

        def lf(hh, gg):
            e = (_rms(hh, gg) - tb) * mask
            return 0.5 * jnp.sum(e * e) / d

        lval, (dh, dg) = jax.value_and_grad(lf, (0, 1))(hb, gfn)
        return dh, dh, jnp.full((1, LANES), lval, F32), dg

    fn = sp["final_norm"].reshape(1, d)
    dh2, dh2b, loss_acc, dfn = rowwise("loss", loss_tile, [h2, tgt], [fn], [(d, F32), (d, BF16)], [(1, LANES), (1, d)], tr=tr_l)

    gbig, gs = {}, {}
    gs["final_norm"] = dfn.reshape(sp["final_norm"].shape)
    dyn = mm(dh2b, bw["Wo1"], "nt", "mm_dyn")
    gbig["Wo1"] = mm(yn, dh2b, "tn", "mm_dWo1", out_dtype=BF16)
    dxs, dbm, dcm, dcols, drws, dz1, ddvec, dgn = ssd_core_bwd(cpre1, cols, rws, z1, dvec, sp["ssd_gnorm"], ssd_st, dyn,
                                                              nb, hpg, "ssd_core_bwd")
    gs["ssd_d"] = ddvec.reshape(1, nhd, SSD_HEAD_DIM).sum(axis=2)
    gs["ssd_gnorm"] = dgn.reshape(1, inner)
    ddt, dcum = _layouts_to_pairs(dcols, drws, ng, hpg)

    def ssd_dt_bwd_tile(i, r_, ddt_, dcum_, b_, a_):
        _, vjp = jax.vjp(_ssd_dt_tile, r_, b_, a_)
        return vjp((ddt_, dcum_))

    ddtr, dbias, dalog = rowwise("ssd_dt_bwd", ssd_dt_bwd_tile, [dtr, _pad_lanes(ddt), _pad_lanes(dcum)], [dt_bias, a_log],
                                 [(LANES, BF16)], [(1, LANES), (1, LANES)], tr=CHUNK)
    gs["ssd_dt_bias"] = dbias[:, :nhd]
    gs["ssd_a_log"] = dalog[:, :nhd]
    dcpre1 = jnp.concatenate([dxs, dbm, dcm], axis=1)
    dxbc, dcw1, dcb1 = conv_bwd(dcpre1, xbc, sp["ssd_conv_w"], nb, "ssd_conv_bwd")
    gs["ssd_conv_w"] = dcw1[:sp["ssd_conv_w"].shape[0]]
    gs["ssd_conv_b"] = dcb1
    dn1 = mm(dz1, bw["W1z"], "nt", "mm_dn1z")
    dn1 = mm(dxbc, bw["W1x"], "nt", "mm_dn1x", resid=dn1)
    dn1 = mm(ddtr, bw["W1dt"], "nt", "mm_dn1dt", resid=dn1)
    gbig["W1z"] = mm(n1, dz1, "tn", "mm_dW1z", out_dtype=BF16)
    gbig["W1x"] = mm(n1, dxbc, "tn", "mm_dW1x", out_dtype=BF16)
    gbig["W1dt"] = mm(n1, ddtr, "tn", "mm_dW1dt", out_dtype=BF16)
    dh1, dh1b, dg1 = norm_bwd(h1, sp["ssd_norm"], dn1, dh2, "norm1_bwd")
    gs["ssd_norm"] = dg1

    dya = mm(dh1b, bw["Wo0a"], "nt", "mm_dya")
    dyb = mm(dh1b, bw["Wo0b"], "nt", "mm_dyb")
    gbig["Wo0a"] = mm(ya, dh1b, "tn", "mm_dWo0a", out_dtype=BF16)
    gbig["Wo0b"] = mm(yb, dh1b, "tn", "mm_dWo0b", out_dtype=BF16)
    (dq, dk, dv, dcp_skip, dzb, dcolg, drowg, dnw, dsk) = ml_core_bwd(
        q, k, v, colg, rowg, cpre0, zb, sp["ml_norm"], sp["ml_skip"], ml_cs, ml_ns, ml_ms, dyb, nb, nh, "ml_core_bwd")
    gs["ml_norm"] = dnw.reshape(1, mlw)
    gs["ml_skip"] = dsk.reshape(1, mlw)
    dig, dbcum = _layouts_to_pairs(dcolg, drowg, nh, 1)
    dgout = _pad_lanes(jnp.concatenate([dig, dbcum], axis=1))

    def ml_gates_bwd_tile(i, g_, dgo, b_):
        _, vjp = jax.vjp(lambda a, b: _ml_gates_tile(a, b, nh), g_, b_)
        return vjp(dgo)

    dgl, dbg = rowwise("ml_gates_bwd", ml_gates_bwd_tile, [gl, dgout], [bgate], [(LANES, F32)], [(1, LANES)], tr=CHUNK)
    gs["ml_b_gate"] = dbg[:, :2 * nh]
    dcpre0, dxb_v, dwq, dwk, dwv, dgq, dgk, dgv = ml_proj_bwd(cpre0, xb, wq_e, wk_e, wv_e, gq, gk, gv, dq, dk, dv, dgl,
                                                            dcp_skip, "ml_proj_bwd")
    gs["ml_wq"], gs["ml_wk"], gs["ml_wv"] = _hw_contract(dwq), _hw_contract(dwk), _hw_contract(dwv)
    gs["ml_w_gate"] = _wg_contract([dgq, dgk, dgv], 2 * nh)
    dxb, dcw0, dcb0 = conv_bwd(dcpre0, xb, sp["ml_conv_w"], nb, "ml_conv_bwd", resid=dxb_v)
    gs["ml_conv_w"] = dcw0[:sp["ml_conv_w"].shape[0]]
    gs["ml_conv_b"] = dcb0

    def glu_bwd_tile(i, ys, tt, pab, dy_, gbias):
        _, vjp = jax.vjp(glu_tile, ys, tt, pab[:, s5w:], gbias)
        return vjp(dy_)

    dys_direct, dtglu, dza, dglub = rowwise("glu_bwd", glu_bwd_tile, [ys5, tglu, pa, dya], [sp["s5_glu_b"]],
                                            [(s5w, F32), (s5w, BF16), (s5w, BF16)], [(1, s5w)], tr=_tile(rows, 256, 16))
    gs["s5_glu_b"] = dglub
    dgb = mm(dtglu, bw["glu"], "nt", "mm_dgb")
    gbig["glu"] = mm(gb, dtglu, "tn", "mm_dglu", out_dtype=BF16)

    def gelu_bwd_tile(i, ys, dg_, direct):
        _, vjp = jax.vjp(_gelu, ys)
        return vjp(dg_)[0] + direct

    dys5 = rowwise("gelu_bwd", gelu_bwd_tile, [ys5, dgb, dys_direct], [], [(s5w, F32)], tr=_tile(rows, 256, 16))[0]
    du, dbre, dbim, dcre, dcim, dare, daie, dd5 = s5_bwd(pa, dys5, s5st, bre, bim, cre, cim, are, aie, sp["s5_d"], nb, "s5_bwd")
    gs["s5_d"] = dd5
    dbbr, dbbi, dcr, dci, dar, dai = _s5_contract(dbre, dbim, dcre, dcim, dare, daie, sg, spn, shh)
    gs["s5_c_re"], gs["s5_c_im"] = dcr, dci
    (gs["s5_lambda_re"], gs["s5_lambda_im"], dlogdt, gs["s5_b_re"], gs["s5_b_im"]) = s5_disc_vjp((dar, dai, dbbr, dbbi))
    gs["s5_log_dt"] = dlogdt.reshape(1, -1)
    dpa = jnp.concatenate([du, dza], axis=1)
    dn0 = mm(dpa, bw["W0a"], "nt", "mm_dn0a")
    dn0 = mm(dxb, bw["W0xb"], "nt", "mm_dn0xb", resid=dn0)
    dn0 = mm(dzb, bw["W0zb"], "nt", "mm_dn0zb", resid=dn0)
    gbig["W0a"] = mm(n0, dpa, "tn", "mm_dW0a", out_dtype=BF16)
    gbig["W0xb"] = mm(n0, dxb, "tn", "mm_dW0xb", out_dtype=BF16)
    gbig["W0zb"] = mm(n0, dzb, "tn", "mm_dW0zb", out_dtype=BF16)
    dh0, _, dg0 = norm_bwd(h0, sp["ab_norm"], dn0, dh1, "norm0_bwd")
    gs["ab_norm"] = dg0
    dh0 = dh0.reshape(nb, tp, d)
    gs["meta_tokens"] = jnp.sum(dh0[:, :N_META], axis=0)
    return loss_acc[0, 0], dh0, gbig, gs


N_DEV = 8
N_CHIP = 4
MESH = pl.DeviceIdType.MESH
_HBM = pl.BlockSpec(memory_space=pltpu.HBM)


def _place():
    x, y, c = lax.axis_index("x"), lax.axis_index("y"), lax.axis_index("c")
    return x, y, c, [(1 - x, y), (x, 1 - y), (1 - x, 1 - y)]


def all_gather8(v, name):
    m_per, n = v.shape

    def body(x_ref, out_ref, send_sems, recv_sems, local_sem):
        x, y, c, chips = _place()
        me, sibling = (x, y, c), (x, y, 1 - c)

        def rows(px, py, pc):
            return out_ref.at[pl.ds((4 * px + 2 * py + pc) * m_per, m_per), :]

        def copy(kk, block, to, src=None):
            return pltpu.make_async_remote_copy(
                src_ref=rows(*block) if src is None else src, dst_ref=rows(*block), send_sem=send_sems.at[kk],
                recv_sem=recv_sems.at[kk], device_id=to, device_id_type=MESH)

        mine = pltpu.make_async_copy(x_ref, rows(*me), local_sem)
        mine.start()
        first = [copy(0, me, sibling, src=x_ref)]
        first += [copy(1 + j, me, (*chip, c), src=x_ref) for j, chip in enumerate(chips)]
        for cp in first:
            cp.start()
        passed = [copy(4 + j, (*chip, c), sibling) for j, chip in enumerate(chips)]
        for j, chip in enumerate(chips):
            copy(1 + j, (*chip, c), me).wait_recv()
            passed[j].start()
        copy(0, sibling, me).wait_recv()
        for j, chip in enumerate(chips):
            copy(4 + j, (*chip, 1 - c), me).wait_recv()
        for cp in first + passed:
            cp.wait_send()
        mine.wait()

    return pl.pallas_call(
        body, name=name, out_shape=jax.ShapeDtypeStruct((N_DEV * m_per, n), v.dtype),
        in_specs=[pl.BlockSpec(memory_space=pltpu.VMEM)], out_specs=pl.BlockSpec(memory_space=pltpu.VMEM),
        scratch_shapes=[pltpu.SemaphoreType.DMA((7,)), pltpu.SemaphoreType.DMA((7,)), pltpu.SemaphoreType.DMA],
        compiler_params=pltpu.CompilerParams(vmem_limit_bytes=VMEM_LIMIT))(v)


def gather_chips(vs, name):
    na = len(vs)

    def body(*refs):
        x_refs, out_refs = refs[:na], refs[na:2 * na]
        send_sems, recv_sems, local_sems = refs[2 * na:]
        x, y, c, chips = _place()
        k = 2 * x + y
        sibling = (x, y, 1 - c)

        def copy(i, kk, src, chip_k, half, to):
            return pltpu.make_async_remote_copy(
                src_ref=src, dst_ref=out_refs[i].at[chip_k, half], send_sem=send_sems.at[6 * i + kk],
                recv_sem=recv_sems.at[6 * i + kk], device_id=to, device_id_type=MESH)

        mine = [pltpu.make_async_copy(x_refs[i], out_refs[i].at[k], local_sems.at[i]) for i in range(na)]
        for cp in mine:
            cp.start()
        first = [copy(i, j, x_refs[i].at[c], k, c, (*chip, c)) for j, chip in enumerate(chips) for i in range(na)]
        for cp in first:
            cp.start()
        passed = []
        for j, (cx, cy) in enumerate(chips):
            kj = 2 * cx + cy
            for i in range(na):
                copy(i, j, out_refs[i].at[kj, c], kj, c, (cx, cy, c)).wait_recv()
                fwd = copy(i, 3 + j, out_refs[i].at[kj, c], kj, c, sibling)
                fwd.start()
                passed.append(fwd)
        for j, (cx, cy) in enumerate(chips):
            kj = 2 * cx + cy
            for i in range(na):
                copy(i, 3 + j, out_refs[i].at[kj, 1 - c], kj, 1 - c, sibling).wait_recv()
        for cp in first + passed:
            cp.wait_send()
        for cp in mine:
            cp.wait()

    return pl.pallas_call(
        body, name=name, out_shape=[jax.ShapeDtypeStruct((N_CHIP,) + v.shape, v.dtype) for v in vs],
        in_specs=[_HBM] * na, out_specs=[_HBM] * na,
        scratch_shapes=[pltpu.SemaphoreType.DMA((6 * na,)), pltpu.SemaphoreType.DMA((6 * na,)),
                        pltpu.SemaphoreType.DMA((na,))])(*vs)


def scatter_chips(ps, name):
    na = len(ps)

    def body(*refs):
        p_refs, out_refs = refs[:na], refs[na:2 * na]
        send_sems, recv_sems, local_sems = refs[2 * na:]
        x, y, c, chips = _place()
        k = 2 * x + y
        sibling = (x, y, 1 - c)

        def copy(i, kk, src, chip_k, half, to):
            return pltpu.make_async_remote_copy(
                src_ref=src, dst_ref=out_refs[i].at[chip_k, half], send_sem=send_sems.at[7 * i + kk],
                recv_sem=recv_sems.at[7 * i + kk], device_id=to, device_id_type=MESH)

        mine = [pltpu.make_async_copy(p_refs[i].at[k], out_refs[i].at[k, c], local_sems.at[i]) for i in range(na)]
        for cp in mine:
            cp.start()
        first = [copy(i, 1 + j, p_refs[i].at[2 * cx + cy], k, c, (cx, cy, c))
                 for j, (cx, cy) in enumerate(chips) for i in range(na)]
        first += [copy(i, 0, p_refs[i].at[k], k, c, sibling) for i in range(na)]
        for cp in first:
            cp.start()
        passed = []
        for j, (cx, cy) in enumerate(chips):
            kj = 2 * cx + cy
            for i in range(na):
                copy(i, 1 + j, out_refs[i].at[kj, c], kj, c, (cx, cy, c)).wait_recv()
                fwd = copy(i, 4 + j, out_refs[i].at[kj, c], kj, c, sibling)
                fwd.start()
                passed.append(fwd)
        for i in range(na):
            copy(i, 0, out_refs[i].at[k, 1 - c], k, 1 - c, sibling).wait_recv()
        for j, (cx, cy) in enumerate(chips):
            kj = 2 * cx + cy
            for i in range(na):
                copy(i, 4 + j, out_refs[i].at[kj, 1 - c], kj, 1 - c, sibling).wait_recv()
        for cp in first + passed:
            cp.wait_send()
        for cp in mine:
            cp.wait()

    return pl.pallas_call(
        body, name=name, out_shape=[jax.ShapeDtypeStruct((N_CHIP, 2) + p.shape[1:], p.dtype) for p in ps],
        in_specs=[_HBM] * na, out_specs=[_HBM] * na,
        scratch_shapes=[pltpu.SemaphoreType.DMA((7 * na,)), pltpu.SemaphoreType.DMA((7 * na,)),
                        pltpu.SemaphoreType.DMA((na,))])(*ps)


def swap_halves(gs_, name):
    na = len(gs_)

    def body(*refs):
        g_refs, out_refs = refs[:na], refs[na:2 * na]
        send_sems, recv_sems = refs[2 * na:]
        x, y, c, _ = _place()
        cps = [pltpu.make_async_remote_copy(
            src_ref=g_refs[i].at[kk, 1 - c], dst_ref=out_refs[i].at[kk], send_sem=send_sems.at[N_CHIP * i + kk],
            recv_sem=recv_sems.at[N_CHIP * i + kk], device_id=(x, y, 1 - c), device_id_type=MESH)
            for i in range(na) for kk in range(N_CHIP)]
        for cp in cps:
            cp.start()
        for cp in cps:
            cp.wait()

    return pl.pallas_call(
        body, name=name, out_shape=[jax.ShapeDtypeStruct((N_CHIP,) + g.shape[2:], g.dtype) for g in gs_],
        in_specs=[_HBM] * na, out_specs=[_HBM] * na,
        scratch_shapes=[pltpu.SemaphoreType.DMA((N_CHIP * na,)), pltpu.SemaphoreType.DMA((N_CHIP * na,))])(*gs_)


def add_halves(g, other, core, name):
    _, _, m, n = g.shape
    tr = _tile(m, 256, 16)

    def body(core_ref, g_ref, o_ref, out_ref):
        out_ref[...] = (g_ref[...].astype(F32) + o_ref[...].astype(F32)).astype(out_ref.dtype)

    grid_spec = pltpu.PrefetchScalarGridSpec(
        num_scalar_prefetch=1, grid=(N_CHIP, m // tr),
        in_specs=[pl.BlockSpec((None, None, tr, n), lambda kk, i, core_ref: (kk, core_ref[0], i, 0)),
                  pl.BlockSpec((None, tr, n), lambda kk, i, core_ref: (kk, i, 0))],
        out_specs=pl.BlockSpec((None, tr, n), lambda kk, i, core_ref: (kk, i, 0)))
    return pl.pallas_call(body, name=name, grid_spec=grid_spec, out_shape=jax.ShapeDtypeStruct((N_CHIP, m, n), g.dtype),
                          compiler_params=_params(("arbitrary", "arbitrary")))(core.reshape(1).astype(jnp.int32), g, other)


PACK_LANES = 512


def _pack(arrs, dtype, lanes, row_align):
    flat = jnp.concatenate([a.reshape(-1).astype(dtype) for a in arrs])
    unit = lanes * row_align
    total = -(-flat.shape[0] // unit) * unit
    return jnp.pad(flat, (0, total - flat.shape[0])).reshape(total // lanes, lanes)


def _unpack(flat, shapes):
    flat = flat.reshape(-1)
    out, off = [], 0
    for s in shapes:
        n = math.prod(s)
        out.append(flat[off:off + n].reshape(s))
        off += n
    return out


def _adam_tile(w, m, v, g):
    m2 = ADAM_B1 * m + (1.0 - ADAM_B1) * g
    v2 = ADAM_B2 * v + (1.0 - ADAM_B2) * (g * g)
    m_hat = m2 / (1.0 - ADAM_B1 ** ADAM_STEP)
    v_hat = v2 / (1.0 - ADAM_B2 ** ADAM_STEP)
    delta = -ADAM_LR * (m_hat / (jnp.sqrt(v_hat) + ADAM_EPS) + ADAM_WD * w)
    return delta, m2, v2


def adam_big(w, m, v, pieces, name):
    _, r, c = w.shape
    tr = _tile(r, 128, 16)

    def body(w_ref, m_ref, v_ref, p0, p1, p2, p3, g_ref, d_ref, mo_ref, vo_ref):
        g = ((p0[...].astype(F32) + p1[...].astype(F32)) + p2[...].astype(F32)) + p3[...].astype(F32)
        delta, m2, v2 = _adam_tile(w_ref[...], m_ref[...], v_ref[...], g)
        g_ref[...] = g
        d_ref[...] = delta
        mo_ref[...] = m2
        vo_ref[...] = v2

    wspec = pl.BlockSpec((None, tr, c), lambda i: (0, i, 0))
    pspecs = [pl.BlockSpec((None, tr, c), functools.partial(lambda i, kk: (kk, i, 0), kk=kk)) for kk in range(N_CHIP)]
    return pl.pallas_call(
        body, name=name, grid=(r // tr,), in_specs=[wspec] * 3 + pspecs, out_specs=[wspec] * 4,
        out_shape=[jax.ShapeDtypeStruct(w.shape, F32)] * 4, compiler_params=_params(("parallel",)))(
            w, m, v, pieces, pieces, pieces, pieces)


_WEIGHTS = (
    ("meta_tokens", "small", 1), ("ab_norm", "small", None), ("ab_w_in", "big", 2), ("s5_lambda_re", "small", None),
    ("s5_lambda_im", "small", None), ("s5_log_dt", "small", None), ("s5_b_re", "small", None), ("s5_b_im", "small", None),
    ("s5_c_re", "small", None), ("s5_c_im", "small", None), ("s5_d", "small", None), ("s5_glu_w", "big", 1),
    ("s5_glu_b", "small", None), ("ml_conv_w", "small", 2), ("ml_conv_b", "small", None), ("ml_wq", "small", 1),
    ("ml_wk", "small", 1), ("ml_wv", "small", 1), ("ml_w_gate", "small", 1), ("ml_b_gate", "small", None),
    ("ml_norm", "small", None), ("ml_skip", "small", None), ("ab_w_out", "big", 1), ("ssd_norm", "small", 1),
    ("ssd_w_in", "big", 2), ("ssd_conv_w", "small", 2), ("ssd_conv_b", "small", 1), ("ssd_dt_bias", "small", None),
    ("ssd_a_log", "small", None), ("ssd_d", "small", None), ("ssd_gnorm", "small", 1), ("ssd_w_out", "big", 1),
    ("final_norm", "small", None),
)


def _squeeze(a):
    return a[0] if a.ndim >= 3 else a


def kernel(x, meta_tokens, ab_norm, ab_w_in, s5_lambda_re, s5_lambda_im, s5_log_dt, s5_b_re, s5_b_im, s5_c_re, s5_c_im, s5_d, s5_glu_w, s5_glu_b, ml_conv_w, ml_conv_b, ml_wq, ml_wk, ml_wv, ml_w_gate, ml_b_gate, ml_norm, ml_skip, ab_w_out, ssd_norm, ssd_w_in, ssd_conv_w, ssd_conv_b, ssd_dt_bias, ssd_a_log, ssd_d, ssd_gnorm, ssd_w_out, final_norm, loss_target, m_meta_tokens, m_ab_norm, m_ab_w_in, m_s5_lambda_re, m_s5_lambda_im, m_s5_log_dt, m_s5_b_re, m_s5_b_im, m_s5_c_re, m_s5_c_im, m_s5_d, m_s5_glu_w, m_s5_glu_b, m_ml_conv_w, m_ml_conv_b, m_ml_wq, m_ml_wk, m_ml_wv, m_ml_w_gate, m_ml_b_gate, m_ml_norm, m_ml_skip, m_ab_w_out, m_ssd_norm, m_ssd_w_in, m_ssd_conv_w, m_ssd_conv_b, m_ssd_dt_bias, m_ssd_a_log, m_ssd_d, m_ssd_gnorm, m_ssd_w_out, m_final_norm, v_meta_tokens, v_ab_norm, v_ab_w_in, v_s5_lambda_re, v_s5_lambda_im, v_s5_log_dt, v_s5_b_re, v_s5_b_im, v_s5_c_re, v_s5_c_im, v_s5_d, v_s5_glu_w, v_s5_glu_b, v_ml_conv_w, v_ml_conv_b, v_ml_wq, v_ml_wk, v_ml_wv, v_ml_w_gate, v_ml_b_gate, v_ml_norm, v_ml_skip, v_ab_w_out, v_ssd_norm, v_ssd_w_in, v_ssd_conv_w, v_ssd_conv_b, v_ssd_dt_bias, v_ssd_a_log, v_ssd_d, v_ssd_gnorm, v_ssd_w_out, v_final_norm):
    args = (meta_tokens, ab_norm, ab_w_in, s5_lambda_re, s5_lambda_im, s5_log_dt, s5_b_re, s5_b_im, s5_c_re, s5_c_im, s5_d, s5_glu_w, s5_glu_b, ml_conv_w, ml_conv_b, ml_wq, ml_wk, ml_wv, ml_w_gate, ml_b_gate, ml_norm, ml_skip, ab_w_out, ssd_norm, ssd_w_in, ssd_conv_w, ssd_conv_b, ssd_dt_bias, ssd_a_log, ssd_d, ssd_gnorm, ssd_w_out, final_norm)
    m_args = (m_meta_tokens, m_ab_norm, m_ab_w_in, m_s5_lambda_re, m_s5_lambda_im, m_s5_log_dt, m_s5_b_re, m_s5_b_im, m_s5_c_re, m_s5_c_im, m_s5_d, m_s5_glu_w, m_s5_glu_b, m_ml_conv_w, m_ml_conv_b, m_ml_wq, m_ml_wk, m_ml_wv, m_ml_w_gate, m_ml_b_gate, m_ml_norm, m_ml_skip, m_ab_w_out, m_ssd_norm, m_ssd_w_in, m_ssd_conv_w, m_ssd_conv_b, m_ssd_dt_bias, m_ssd_a_log, m_ssd_d, m_ssd_gnorm, m_ssd_w_out, m_final_norm)
    v_args = (v_meta_tokens, v_ab_norm, v_ab_w_in, v_s5_lambda_re, v_s5_lambda_im, v_s5_log_dt, v_s5_b_re, v_s5_b_im, v_s5_c_re, v_s5_c_im, v_s5_d, v_s5_glu_w, v_s5_glu_b, v_ml_conv_w, v_ml_conv_b, v_ml_wq, v_ml_wk, v_ml_wv, v_ml_w_gate, v_ml_b_gate, v_ml_norm, v_ml_skip, v_ab_w_out, v_ssd_norm, v_ssd_w_in, v_ssd_conv_w, v_ssd_conv_b, v_ssd_dt_bias, v_ssd_a_log, v_ssd_d, v_ssd_gnorm, v_ssd_w_out, v_final_norm)
    names = [w[0] for w in _WEIGHTS]
    kind = {w[0]: w[1] for w in _WEIGHTS}
    axis = {w[0]: w[2] for w in _WEIGHTS}
    w_loc = dict(zip(names, args))
    m_loc = dict(zip(names, m_args))
    v_loc = dict(zip(names, v_args))
    chip = 2 * lax.axis_index("x") + lax.axis_index("y")
    core = lax.axis_index("c")
    big = [n for n in names if kind[n] == "big"]
    small = [n for n in names if kind[n] == "small"]
    small_sh = [n for n in small if axis[n] is not None]

    def halves(a):
        return a.astype(BF16).reshape(2, a.shape[1] // 2, a.shape[2])

    gathered = gather_chips([halves(w_loc[n]) for n in big], "gather_big_w")
    full_big = {}
    for n, gth in zip(big, gathered):
        shard = gth.reshape((N_CHIP,) + w_loc[n].shape[1:])
        if axis[n] == 1:
            full_big[n] = shard.reshape(-1, shard.shape[2])
        else:
            full_big[n] = jnp.concatenate([shard[kk] for kk in range(N_CHIP)], axis=1)
    small_sh_shapes = [w_loc[n].shape for n in small_sh]
    packed_s = _pack([w_loc[n] for n in small_sh], F32, LANES, SUBLANES)
    g8 = all_gather8(packed_s, "gather_small_w").reshape(N_CHIP, 2, -1)
    sp = {}
    for n in small:
        if axis[n] is None:
            sp[n] = _squeeze(w_loc[n])
    per_chip = [_unpack(g8[kk, 0], small_sh_shapes) for kk in range(N_CHIP)]
    for i, n in enumerate(small_sh):
        sp[n] = _squeeze(jnp.concatenate([per_chip[kk][i] for kk in range(N_CHIP)], axis=axis[n]))

    s5w = full_big["s5_glu_w"].shape[0]
    mlw = full_big["ab_w_out"].shape[0] - s5w
    inner = full_big["ssd_w_out"].shape[0]
    w_in0, w_in1 = full_big["ab_w_in"], full_big["ssd_w_in"]
    n_heads1 = sp["ssd_d"].shape[1]
    cdim = w_in1.shape[1] - inner - n_heads1
    bw = dict(W0a=w_in0[:, :2 * s5w], W0xb=w_in0[:, 2 * s5w:2 * s5w + mlw], W0zb=w_in0[:, 2 * s5w + mlw:],
              glu=full_big["s5_glu_w"], Wo0a=full_big["ab_w_out"][:s5w], Wo0b=full_big["ab_w_out"][s5w:],
              W1z=w_in1[:, :inner], W1x=w_in1[:, inner:inner + cdim], W1dt=_pad_lanes(w_in1[:, inner + cdim:]),
              Wo1=full_big["ssd_w_out"])

    loss_local, dh0, gbig, gs = _local_step(x, loss_target, bw, sp)
    loss = lax.psum(loss_local, ("x", "y", "c"))
    grad_x = dh0[:, N_META:N_META + x.shape[1]]

    gfull = {
        "ab_w_in": jnp.concatenate([gbig["W0a"], gbig["W0xb"], gbig["W0zb"]], axis=1),
        "s5_glu_w": gbig["glu"],
        "ab_w_out": jnp.concatenate([gbig["Wo0a"], gbig["Wo0b"]], axis=0),
        "ssd_w_in": jnp.concatenate([gbig["W1z"], gbig["W1x"], gbig["W1dt"][:, :n_heads1]], axis=1),
        "ssd_w_out": gbig["Wo1"],
    }
    gps = []
    for n in big:
        _, r, c_ = w_loc[n].shape
        gf = gfull[n]
        if axis[n] == 1:
            gp = gf.reshape(N_CHIP, 2, r // 2, c_)
        else:
            gp = jnp.stack([gf[:, kk * c_:(kk + 1) * c_] for kk in range(N_CHIP)]).reshape(N_CHIP, 2, r // 2, c_)
        gps.append(gp)
    from_sibling = swap_halves(gps, "swap_big_g")
    partials = [add_halves(gp, oth, core, "add_" + n) for n, gp, oth in zip(big, gps, from_sibling)]
    pieces = scatter_chips(partials, "scatter_big_g")

    out_g, out_d, out_m, out_v = {}, {}, {}, {}
    for n, pc in zip(big, pieces):
        pcs = pc.reshape((N_CHIP,) + w_loc[n].shape[1:])
        out_g[n], out_d[n], out_m[n], out_v[n] = adam_big(w_loc[n], m_loc[n], v_loc[n], pcs, "adam_" + n)

    small_full_shapes = [sp[n].shape for n in small]
    packed_gs = _pack([gs[n] for n in small], F32, LANES, SUBLANES)
    rows_s = packed_gs.shape[0]
    all_gs = all_gather8(packed_gs, "gather_small_g")
    blocks = [all_gs[i * rows_s:(i + 1) * rows_s] for i in range(N_DEV)]

    def sum8(i, *b):
        acc = b[0]
        for t in b[1:]:
            acc = acc + t
        return acc

    gsum = rowwise("sum_small_g", sum8, blocks, [], [(LANES, F32)], tr=_tile(rows_s, 512, 8))[0]
    g_small = dict(zip(small, _unpack(gsum, small_full_shapes)))
    g_loc = {}
    for n in small:
        g = g_small[n].reshape((1,) + g_small[n].shape) if w_loc[n].ndim >= 3 else g_small[n]
        if axis[n] is not None:
            size = w_loc[n].shape[axis[n]]
            g = lax.dynamic_slice_in_dim(g, chip * size, size, axis=axis[n])
        g_loc[n] = g.reshape(w_loc[n].shape)
    loc_shapes = [w_loc[n].shape for n in small]
    pw, pm, pv, pg = (_pack([d[n] for n in small], F32, LANES, SUBLANES) for d in (w_loc, m_loc, v_loc, g_loc))
    dl, mn, vn = rowwise("adam_small", lambda i, a, b, c_, d_: _adam_tile(a, b, c_, d_), [pw, pm, pv, pg], [],
                         [(LANES, F32)] * 3, tr=_tile(pw.shape[0], 512, 8))
    for d_out, flat in ((out_d, dl), (out_m, mn), (out_v, vn)):
        for n, a in zip(small, _unpack(flat, loc_shapes)):
            d_out[n] = a
    for n in small:
        out_g[n] = g_loc[n]

    return (loss, grad_x, *[out_g[n] for n in names], *[out_d[n] for n in names], *[out_m[n] for n in names],
            *[out_v[n] for n in names])
```

```python
import functools
import math

import jax
import jax.numpy as jnp
from jax import lax
from jax.experimental import pallas as pl
from jax.experimental.pallas import tpu as pltpu

F32 = jnp.float32
BF16 = jnp.bfloat16
HI = lax.Precision.HIGHEST

D_MODEL = 2048
SEQ = 2048
N_META = 16
CHUNK = 128
NORM_EPS = 1e-6
HEAD_NORM_EPS = 1e-5
S5_GROUP_SIZE = 16
S5_STATE = 64
MLSTM_HEADS = 8
QKV_BLOCK = 4
SSD_HEAD_DIM = 64
SSD_STATE = 128
SSD_HPG = 8
ADAM_LR = 0.001
ADAM_B1 = 0.9
ADAM_B2 = 0.999
ADAM_EPS = 1e-08
ADAM_WD = 0.01
ADAM_STEP = 10

LANES = 128
SUBLANES = 8
VMEM_LIMIT = 56 * 1024 * 1024
MM_OPERAND_VMEM = 34 * 1024 * 1024


def _sigmoid(x):
    return 0.5 * jnp.tanh(0.5 * x) + 0.5


@jax.custom_vjp
def _silu(x):
    return x * _sigmoid(x)


def _silu_fwd(x):
    return x * _sigmoid(x), x


def _silu_bwd(x, ct):
    s = _sigmoid(x)
    return (ct * (s * (1.0 + x * (1.0 - s))),)


_silu.defvjp(_silu_fwd, _silu_bwd)


def _softplus(x):
    return jnp.maximum(x, 0.0) + jnp.log(1.0 + jnp.exp(-jnp.abs(x)))


def _log_sigmoid(x):
    return jnp.minimum(x, 0.0) - jnp.log(1.0 + jnp.exp(-jnp.abs(x)))


def _gelu(x):
    return 0.5 * x * (1.0 + jnp.tanh(math.sqrt(2.0 / math.pi) * (x + 0.044715 * (x * x * x))))


def _dot(a, b, dims, precision=None):
    return lax.dot_general(a, b, (dims, ((), ())), preferred_element_type=F32, precision=precision)


_NN, _NT, _TN = ((1,), (0,)), ((1,), (1,)), ((0,), (0,))


def _bf16_dot(dims, da_rule, db_rule):
    @jax.custom_vjp
    def f(a, b):
        return _dot(a.astype(BF16), b.astype(BF16), dims)

    def fwd(a, b):
        ab, bb = a.astype(BF16), b.astype(BF16)
        return _dot(ab, bb, dims), (ab, bb, jnp.zeros((), a.dtype), jnp.zeros((), b.dtype))

    def bwd(res, ct):
        ab, bb, a_like, b_like = res
        cb = ct.astype(BF16)
        return da_rule(ab, bb, cb).astype(a_like.dtype), db_rule(ab, bb, cb).astype(b_like.dtype)

    f.defvjp(fwd, bwd)
    return f


_dot_nn = _bf16_dot(_NN, lambda a, b, c: _dot(c, b, _NT), lambda a, b, c: _dot(a, c, _TN))
_dot_nt = _bf16_dot(_NT, lambda a, b, c: _dot(c, b, _NN), lambda a, b, c: _dot(c, a, _TN))
_dot_tn = _bf16_dot(_TN, lambda a, b, c: _dot(b, c, _NT), lambda a, b, c: _dot(a, c, _NN))


def _lane_pick(a, idx):
    sel = (lax.broadcasted_iota(jnp.int32, (1, a.shape[1]), 1) == idx).astype(a.dtype)
    return jnp.sum(a * sel, axis=1, keepdims=True)


def _row_pick(a, idx):
    sel = (lax.broadcasted_iota(jnp.int32, (a.shape[0], 1), 0) == idx).astype(a.dtype)
    return jnp.sum(a * sel, axis=0, keepdims=True)


def _tri(n, upper=False):
    r = lax.broadcasted_iota(jnp.int32, (n, n), 0)
    c = lax.broadcasted_iota(jnp.int32, (n, n), 1)
    return ((r <= c) if upper else (r >= c)).astype(F32)


def _tile(n, target, align):
    if n <= target:
        return n
    t = (target // align) * align
    while t >= align:
        if n % t == 0:
            return t
        t -= align
    return n


def _params(sem=None):
    return pltpu.CompilerParams(dimension_semantics=sem, vmem_limit_bytes=VMEM_LIMIT)


def mm(a, b, mode, name, resid=None, out_dtype=F32):
    if mode == "nn":
        (m, k), (k2, n) = a.shape, b.shape
    elif mode == "nt":
        (m, k), (n, k2) = a.shape, b.shape
    else:
        (k, m), (k2, n) = a.shape, b.shape
    assert k == k2, (a.shape, b.shape, mode)
    a_sz, b_sz = a.dtype.itemsize, b.dtype.itemsize
    if mode == "tn":
        tm, tn = _tile(m, 1024, LANES), _tile(n, 1024, LANES)
        tk = _tile(k, MM_OPERAND_VMEM // (2 * (tm * a_sz + tn * b_sz)), 16)
    else:
        tm, tn = _tile(m, 1088, 16), _tile(n, 512, LANES)
        tk = _tile(k, MM_OPERAND_VMEM // (2 * (tm * a_sz + tn * b_sz)), LANES)
    nk = k // tk
    dims = {"nn": ((1,), (0,)), "nt": ((1,), (1,)), "tn": ((0,), (0,))}[mode]
    has_resid = resid is not None

    def body(*refs):
        if has_resid:
            a_ref, b_ref, r_ref, o_ref = refs[:4]
        else:
            a_ref, b_ref, o_ref = refs[:3]
        part = _dot(a_ref[...].astype(BF16), b_ref[...].astype(BF16), dims)

        def finish(res):
            if has_resid:
                res = res + r_ref[...].astype(F32)
            o_ref[...] = res.astype(o_ref.dtype)

        if nk == 1:
            finish(part)
            return
        acc_ref = refs[-1]
        kk = pl.program_id(2)

        @pl.when(kk == 0)
        def _():
            acc_ref[...] = part

        @pl.when(jnp.logical_and(kk > 0, kk < nk - 1))
        def _():
            acc_ref[...] += part

        @pl.when(kk == nk - 1)
        def _():
            finish(acc_ref[...] + part)

    if mode == "tn":
        a_spec = pl.BlockSpec((tk, tm), lambda i, j, kk: (kk, i))
    else:
        a_spec = pl.BlockSpec((tm, tk), lambda i, j, kk: (i, kk))
    if mode == "nt":
        b_spec = pl.BlockSpec((tn, tk), lambda i, j, kk: (j, kk))
    else:
        b_spec = pl.BlockSpec((tk, tn), lambda i, j, kk: (kk, j))
    o_spec = pl.BlockSpec((tm, tn), lambda i, j, kk: (i, j))
    in_specs = [a_spec, b_spec] + ([o_spec] if has_resid else [])
    args = (a, b) + ((resid,) if has_resid else ())
    return pl.pallas_call(
        body, name=name, grid=(m // tm, n // tn, nk), in_specs=in_specs, out_specs=o_spec,
        out_shape=jax.ShapeDtypeStruct((m, n), out_dtype), scratch_shapes=[pltpu.VMEM((tm, tn), F32)] if nk > 1 else [],
        compiler_params=_params(("parallel", "parallel", "arbitrary")))(*args)


def rowwise(name, f, rows, params, outs, accs=(), tr=128):
    n_rows = rows[0].shape[0]
    assert n_rows % tr == 0
    n_r, n_p, n_o, n_a = len(rows), len(params), len(outs), len(accs)

    def body(*refs):
        i = pl.program_id(0)
        r_vals = [r[...] for r in refs[:n_r]]
        p_vals = [r[...] for r in refs[n_r:n_r + n_p]]
        o_refs = refs[n_r + n_p:n_r + n_p + n_o]
        a_refs = refs[n_r + n_p + n_o:]
        res = f(i, *r_vals, *p_vals)
        if not isinstance(res, (tuple, list)):
            res = (res,)
        assert len(res) == n_o + n_a, (name, len(res))
        for o_ref, val in zip(o_refs, res[:n_o]):
            o_ref[...] = val.astype(o_ref.dtype)
        if n_a:
            @pl.when(i == 0)
            def _():
                for a_ref in a_refs:
                    a_ref[...] = jnp.zeros_like(a_ref)

            for a_ref, val in zip(a_refs, res[n_o:]):
                a_ref[...] += val.astype(F32)

    in_specs = [pl.BlockSpec((tr, r.shape[1]), lambda i: (i, 0)) for r in rows]
    in_specs += [pl.BlockSpec(p.shape, lambda i: (0, 0)) for p in params]
    out_specs = [pl.BlockSpec((tr, w), lambda i: (i, 0)) for w, _ in outs]
    out_specs += [pl.BlockSpec(s, lambda i: (0, 0)) for s in accs]
    out_shape = [jax.ShapeDtypeStruct((n_rows, w), dt) for w, dt in outs]
    out_shape += [jax.ShapeDtypeStruct(s, F32) for s in accs]
    res = pl.pallas_call(
        body, name=name, grid=(n_rows // tr,), in_specs=in_specs, out_specs=out_specs, out_shape=out_shape,
        compiler_params=_params(("arbitrary",)))(*rows, *params)
    return res


def _rms(x, g, eps=NORM_EPS):
    return x * lax.rsqrt(jnp.mean(x * x, axis=-1, keepdims=True) + eps) * g


def norm_fwd(x, g, name):
    return rowwise(name, lambda i, xb, gb: _rms(xb, gb), [x], [g], [(x.shape[1], BF16)], tr=_tile(x.shape[0], 256, 16))[0]


def norm_bwd(x, g, dn, resid, name):
    def f(i, xb, dnb, rb, gb):
        _, vjp = jax.vjp(_rms, xb, gb)
        dx, dg = vjp(dnb)
        return dx + rb, dx + rb, dg

    return rowwise(name, f, [x, dn, resid], [g], [(x.shape[1], F32), (x.shape[1], BF16)], [g.shape],
                   tr=_tile(x.shape[0], 256, 16))


def conv_fwd(x, w, b, nb, name):
    rows, width = x.shape
    nc = rows // nb // CHUNK
    tw = _tile(width, 1024, LANES)
    ksz = w.shape[0]

    def body(x_ref, w_ref, b_ref, o_ref, ext_ref):
        c = pl.program_id(2)

        @pl.when(c == 0)
        def _():
            ext_ref[0:SUBLANES, :] = jnp.zeros((SUBLANES, tw), F32)

        xv = x_ref[...]
        ext_ref[SUBLANES:SUBLANES + CHUNK, :] = xv
        acc = jnp.broadcast_to(b_ref[...], (CHUNK, tw))
        for j in range(ksz):
            off = SUBLANES - (ksz - 1) + j
            acc = acc + w_ref[j:j + 1, :] * ext_ref[off:off + CHUNK, :]
        o_ref[...] = acc
        ext_ref[0:SUBLANES, :] = xv[CHUNK - SUBLANES:CHUNK, :]

    return pl.pallas_call(
        body, name=name, grid=(width // tw, nb, nc),
        in_specs=[pl.BlockSpec((CHUNK, tw), lambda j, bb, c: (bb * nc + c, j)),
                  pl.BlockSpec((ksz, tw), lambda j, bb, c: (0, j)),
                  pl.BlockSpec((1, tw), lambda j, bb, c: (0, j))],
        out_specs=pl.BlockSpec((CHUNK, tw), lambda j, bb, c: (bb * nc + c, j)),
        out_shape=jax.ShapeDtypeStruct((rows, width), F32),
        scratch_shapes=[pltpu.VMEM((CHUNK + 2 * SUBLANES, tw), F32)],
        compiler_params=_params(("arbitrary", "arbitrary", "arbitrary")))(x, w, b)


def conv_bwd(dc, x, w, nb, name, resid=None, dx_dtype=BF16):
    rows, width = x.shape
    nc = rows // nb // CHUNK
    tw = _tile(width, 1024, LANES)
    ksz = w.shape[0]
    per = CHUNK // SUBLANES
    has_resid = resid is not None

    def body(*refs):
        if has_resid:
            dc_ref, x_ref, halo_ref, w_ref, r_ref, dx_ref, dw_ref, db_ref, extd_ref, extx_ref = refs
        else:
            dc_ref, x_ref, halo_ref, w_ref, dx_ref, dw_ref, db_ref, extd_ref, extx_ref = refs
        bb = pl.program_id(1)
        step = pl.program_id(2)
        c = nc - 1 - step

        @pl.when(jnp.logical_and(bb == 0, step == 0))
        def _():
            dw_ref[...] = jnp.zeros_like(dw_ref)
            db_ref[...] = jnp.zeros_like(db_ref)

        @pl.when(step == 0)
        def _():
            extd_ref[CHUNK:CHUNK + SUBLANES, :] = jnp.zeros((SUBLANES, tw), F32)

        dcv = dc_ref[...]
        extd_ref[0:CHUNK, :] = dcv
        extx_ref[0:SUBLANES, :] = jnp.where(c == 0, 0.0, halo_ref[...])
        extx_ref[SUBLANES:SUBLANES + CHUNK, :] = x_ref[...]
        dx = jnp.zeros((CHUNK, tw), F32)
        for j in range(ksz):
            up = ksz - 1 - j
            dx = dx + w_ref[j:j + 1, :] * extd_ref[up:up + CHUNK, :]
            off = SUBLANES - (ksz - 1) + j
            dw_ref[j:j + 1, :] += jnp.sum(dcv * extx_ref[off:off + CHUNK, :], axis=0, keepdims=True)
        if has_resid:
            dx = dx + r_ref[...]
        dx_ref[...] = dx.astype(dx_ref.dtype)
        db_ref[...] += jnp.sum(dcv, axis=0, keepdims=True)
        extd_ref[CHUNK:CHUNK + SUBLANES, :] = dcv[0:SUBLANES, :]

    def blk(j, bb, step):
        return (bb * nc + nc - 1 - step, j)

    def halo(j, bb, step):
        return (jnp.maximum((bb * nc + nc - 1 - step) * per - 1, 0), j)

    in_specs = [pl.BlockSpec((CHUNK, tw), blk), pl.BlockSpec((CHUNK, tw), blk), pl.BlockSpec((SUBLANES, tw), halo),
                pl.BlockSpec((ksz, tw), lambda j, bb, step: (0, j))]
    args = [dc, x, x, w]
    if has_resid:
        in_specs.append(pl.BlockSpec((CHUNK, tw), blk))
        args.append(resid)
    return pl.pallas_call(
        body, name=name, grid=(width // tw, nb, nc), in_specs=in_specs,
        out_specs=[pl.BlockSpec((CHUNK, tw), blk), pl.BlockSpec((SUBLANES, tw), lambda j, bb, step: (0, j)),
                   pl.BlockSpec((1, tw), lambda j, bb, step: (0, j))],
        out_shape=[jax.ShapeDtypeStruct((rows, width), dx_dtype), jax.ShapeDtypeStruct((SUBLANES, width), F32),
                   jax.ShapeDtypeStruct((1, width), F32)],
        scratch_shapes=[pltpu.VMEM((CHUNK + 2 * SUBLANES, tw), F32), pltpu.VMEM((CHUNK + 2 * SUBLANES, tw), F32)],
        compiler_params=_params(("arbitrary", "arbitrary", "arbitrary")))(*args)


S5_Q = 4


def _s5_fill_bu(u, bre_ref, bim_ref, xr_ref, xi_ref, ns):
    for s in range(ns):
        ub = u[:, s * LANES:(s + 1) * LANES].astype(BF16)
        bur = _dot(ub, bre_ref[s], ((1,), (0,)))
        bui = _dot(ub, bim_ref[s], ((1,), (0,)))
        for q in range(S5_Q):
            xr_ref[q, pl.ds(s, CHUNK, stride=ns), :] = bur[:, q * LANES:(q + 1) * LANES]
            xi_ref[q, pl.ds(s, CHUNK, stride=ns), :] = bui[:, q * LANES:(q + 1) * LANES]


def _s5_scan(xr_ref, xi_ref, ar_ref, ai_ref, st_ref, ns):
    ar = [ar_ref[q] for q in range(S5_Q)]
    ai = [ai_ref[q] for q in range(S5_Q)]

    def step(t, carry):
        rows = pl.ds(pl.multiple_of(t * ns, ns), ns)
        out = []
        for q in range(S5_Q):
            pr, pi_ = carry[2 * q], carry[2 * q + 1]
            nr = ar[q] * pr - ai[q] * pi_ + xr_ref[q, rows, :]
            ni = ar[q] * pi_ + ai[q] * pr + xi_ref[q, rows, :]
            xr_ref[q, rows, :] = nr
            xi_ref[q, rows, :] = ni
            out += [nr, ni]
        return tuple(out)

    init = []
    for q in range(S5_Q):
        init += [st_ref[0, q], st_ref[1, q]]
    fin = lax.fori_loop(0, CHUNK, step, tuple(init), unroll=2)
    for q in range(S5_Q):
        st_ref[0, q] = fin[2 * q]
        st_ref[1, q] = fin[2 * q + 1]


def s5_fwd(pa, bre, bim, cre, cim, ar, ai, dvec, nb, name):
    rows = pa.shape[0]
    width = pa.shape[1] // 2
    ns = width // LANES
    nc = rows // nb // CHUNK

    def body(u_ref, bre_ref, bim_ref, cre_ref, cim_ref, ar_ref, ai_ref, d_ref, y_ref, g_ref, so_ref, xr_ref, xi_ref, st_ref):
        c = pl.program_id(1)

        @pl.when(c == 0)
        def _():
            st_ref[...] = jnp.zeros_like(st_ref)

        so_ref[...] = st_ref[...]
        u = u_ref[...]
        _s5_fill_bu(u, bre_ref, bim_ref, xr_ref, xi_ref, ns)
        _s5_scan(xr_ref, xi_ref, ar_ref, ai_ref, st_ref, ns)
        for s in range(ns):
            acc = jnp.zeros((CHUNK, LANES), F32)
            for q in range(S5_Q):
                xr = xr_ref[q, pl.ds(s, CHUNK, stride=ns), :].astype(BF16)
                xi = xi_ref[q, pl.ds(s, CHUNK, stride=ns), :].astype(BF16)
                acc = acc + _dot(xr, cre_ref[s, q * LANES:(q + 1) * LANES, :], ((1,), (0,)))
                acc = acc - _dot(xi, cim_ref[s, q * LANES:(q + 1) * LANES, :], ((1,), (0,)))
            cols = slice(s * LANES, (s + 1) * LANES)
            y = acc + d_ref[:, cols] * u[:, cols]
            y_ref[:, cols] = y
            g_ref[:, cols] = _gelu(y).astype(BF16)

    whole3 = lambda a: pl.BlockSpec(a.shape, lambda b_, c: (0, 0, 0))
    return pl.pallas_call(
        body, name=name, grid=(nb, nc),
        in_specs=[pl.BlockSpec((CHUNK, width), lambda b_, c: (b_ * nc + c, 0)), whole3(bre), whole3(bim), whole3(cre),
                  whole3(cim), whole3(ar), whole3(ai), pl.BlockSpec((1, width), lambda b_, c: (0, 0))],
        out_specs=[pl.BlockSpec((CHUNK, width), lambda b_, c: (b_ * nc + c, 0)),
                   pl.BlockSpec((CHUNK, width), lambda b_, c: (b_ * nc + c, 0)),
                   pl.BlockSpec((None, 2, S5_Q, ns, LANES), lambda b_, c: (b_ * nc + c, 0, 0, 0, 0))],
        out_shape=[jax.ShapeDtypeStruct((rows, width), F32), jax.ShapeDtypeStruct((rows, width), BF16),
                   jax.ShapeDtypeStruct((nb * nc, 2, S5_Q, ns, LANES), F32)],
        scratch_shapes=[pltpu.VMEM((S5_Q, CHUNK * ns, LANES), F32), pltpu.VMEM((S5_Q, CHUNK * ns, LANES), F32),
                        pltpu.VMEM((2, S5_Q, ns, LANES), F32)],
        compiler_params=_params(("arbitrary", "arbitrary")))(pa, bre, bim, cre, cim, ar, ai, dvec)


def s5_bwd(pa, dys, states, bre, bim, cre, cim, ar, ai, dvec, nb, name):
    rows = pa.shape[0]
    width = pa.shape[1] // 2
    ns = width // LANES
    nc = rows // nb // CHUNK

    def body(u_ref, dy_ref, sin_ref, bre_ref, bim_ref, cre_ref, cim_ref, ar_ref, ai_ref, d_ref,
             du_ref, dbre_ref, dbim_ref, dcre_ref, dcim_ref, dar_ref, dai_ref, dd_ref,
             xr_ref, xi_ref, lr_ref, li_ref, st_ref, lam_ref):
        bb = pl.program_id(0)
        step_i = pl.program_id(1)

        @pl.when(jnp.logical_and(bb == 0, step_i == 0))
        def _():
            for r in (dbre_ref, dbim_ref, dcre_ref, dcim_ref, dar_ref, dai_ref, dd_ref):
                r[...] = jnp.zeros_like(r)

        @pl.when(step_i == 0)
        def _():
            lam_ref[...] = jnp.zeros_like(lam_ref)

        u = u_ref[...]
        dy = dy_ref[...]
        st_ref[...] = sin_ref[...]
        _s5_fill_bu(u, bre_ref, bim_ref, xr_ref, xi_ref, ns)
        _s5_scan(xr_ref, xi_ref, ar_ref, ai_ref, st_ref, ns)
        dd_ref[...] += jnp.sum(dy * u, axis=0, keepdims=True)
        for s in range(ns):
            dyb = dy[:, s * LANES:(s + 1) * LANES].astype(BF16)
            gr = _dot(dyb, cre_ref[s], ((1,), (1,)))
            gi = -_dot(dyb, cim_ref[s], ((1,), (1,)))
            for q in range(S5_Q):
                lr_ref[q, pl.ds(s, CHUNK, stride=ns), :] = gr[:, q * LANES:(q + 1) * LANES]
                li_ref[q, pl.ds(s, CHUNK, stride=ns), :] = gi[:, q * LANES:(q + 1) * LANES]
                xr = xr_ref[q, pl.ds(s, CHUNK, stride=ns), :].astype(BF16)
                xi = xi_ref[q, pl.ds(s, CHUNK, stride=ns), :].astype(BF16)
                dcre_ref[s, q * LANES:(q + 1) * LANES, :] += _dot(xr, dyb, ((0,), (0,)))
                dcim_ref[s, q * LANES:(q + 1) * LANES, :] -= _dot(xi, dyb, ((0,), (0,)))
        ar = [ar_ref[q] for q in range(S5_Q)]
        ai = [ai_ref[q] for q in range(S5_Q)]

        def one(t_rows, p_r, p_i, carry):
            out = []
            for q in range(S5_Q):
                l_r, l_i, da_r, da_i = carry[4 * q:4 * q + 4]
                n_r = lr_ref[q, t_rows, :] + ar[q] * l_r + ai[q] * l_i
                n_i = li_ref[q, t_rows, :] + ar[q] * l_i - ai[q] * l_r
                lr_ref[q, t_rows, :] = n_r
                li_ref[q, t_rows, :] = n_i
                xpr, xpi = p_r(q), p_i(q)
                out += [n_r, n_i, da_r + n_r * xpr + n_i * xpi, da_i + n_i * xpr - n_r * xpi]
            return tuple(out)

        def step(k, carry):
            t = CHUNK - 1 - k
            t_rows = pl.ds(pl.multiple_of(t * ns, ns), ns)
            p_rows = pl.ds(pl.multiple_of((t - 1) * ns, ns), ns)
            return one(t_rows, lambda q: xr_ref[q, p_rows, :], lambda q: xi_ref[q, p_rows, :], carry)

        init = []
        zero = jnp.zeros((ns, LANES), F32)
        for q in range(S5_Q):
            init += [lam_ref[0, q], lam_ref[1, q], zero, zero]
        carry = lax.fori_loop(0, CHUNK - 1, step, tuple(init), unroll=2)
        carry = one(pl.ds(0, ns), lambda q: sin_ref[0, q], lambda q: sin_ref[1, q], carry)
        for q in range(S5_Q):
            lam_ref[0, q] = carry[4 * q]
            lam_ref[1, q] = carry[4 * q + 1]
            dar_ref[q] += carry[4 * q + 2]
            dai_ref[q] += carry[4 * q + 3]
        for s in range(ns):
            cols = slice(s * LANES, (s + 1) * LANES)
            ub = u[:, cols].astype(BF16)
            acc = d_ref[:, cols] * dy[:, cols]
            for q in range(S5_Q):
                qs = slice(q * LANES, (q + 1) * LANES)
                lr = lr_ref[q, pl.ds(s, CHUNK, stride=ns), :].astype(BF16)
                li = li_ref[q, pl.ds(s, CHUNK, stride=ns), :].astype(BF16)
                dbre_ref[s, :, qs] += _dot(ub, lr, ((0,), (0,)))
                dbim_ref[s, :, qs] += _dot(ub, li, ((0,), (0,)))
                acc = acc + _dot(lr, bre_ref[s, :, qs], ((1,), (1,))) + _dot(li, bim_ref[s, :, qs], ((1,), (1,)))
            du_ref[:, cols] = acc.astype(du_ref.dtype)

    whole3 = lambda a: pl.BlockSpec(a.shape, lambda b_, c: (0, 0, 0))
    rowblk = pl.BlockSpec((CHUNK, width), lambda b_, c: (b_ * nc + nc - 1 - c, 0))
    scr = pltpu.VMEM((S5_Q, CHUNK * ns, LANES), F32)
    return pl.pallas_call(
        body, name=name, grid=(nb, nc),
        in_specs=[rowblk, rowblk,
                  pl.BlockSpec((None, 2, S5_Q, ns, LANES), lambda b_, c: (b_ * nc + nc - 1 - c, 0, 0, 0, 0)),
                  whole3(bre), whole3(bim), whole3(cre), whole3(cim), whole3(ar), whole3(ai),
                  pl.BlockSpec((1, width), lambda b_, c: (0, 0))],
        out_specs=[rowblk, whole3(bre), whole3(bim), whole3(cre), whole3(cim), whole3(ar), whole3(ai),
                   pl.BlockSpec((1, width), lambda b_, c: (0, 0))],
        out_shape=[jax.ShapeDtypeStruct((rows, width), BF16), jax.ShapeDtypeStruct(bre.shape, F32),
                   jax.ShapeDtypeStruct(bim.shape, F32), jax.ShapeDtypeStruct(cre.shape, F32),
                   jax.ShapeDtypeStruct(cim.shape, F32), jax.ShapeDtypeStruct(ar.shape, F32),
                   jax.ShapeDtypeStruct(ai.shape, F32), jax.ShapeDtypeStruct((1, width), F32)],
        scratch_shapes=[scr, scr, scr, scr, pltpu.VMEM((2, S5_Q, ns, LANES), F32), pltpu.VMEM((2, S5_Q, ns, LANES), F32)],
        compiler_params=_params(("arbitrary", "arbitrary")))(pa, dys, states, bre, bim, cre, cim, ar, ai, dvec)


def _s5_discretize(lam_re, lam_im, log_dt, b_re, b_im):
    dt = jnp.exp(log_dt)[:, None]
    mag = jnp.exp(lam_re * dt)
    ar, ai = mag * jnp.cos(lam_im * dt), mag * jnp.sin(lam_im * dt)
    den = lam_re * lam_re + lam_im * lam_im
    qr = ((ar - 1.0) * lam_re + ai * lam_im) / den
    qi = (ai * lam_re - (ar - 1.0) * lam_im) / den
    bbr = qr[..., None] * b_re - qi[..., None] * b_im
    bbi = qr[..., None] * b_im + qi[..., None] * b_re
    return ar, ai, bbr, bbi


def _s5_expand(ar, ai, bbr, bbi, c_re, c_im):
    g, p, h = bbr.shape
    gps = LANES // h
    ns = g // gps
    eye = jnp.eye(gps, dtype=F32)

    def bexp(b):
        return jnp.einsum("sgph,gk->sghkp", b.reshape(ns, gps, p, h), eye).reshape(ns, gps * h, gps * p)

    def cexp(c):
        return jnp.einsum("sghp,gk->sgpkh", c.reshape(ns, gps, h, p), eye).reshape(ns, gps * p, gps * h)

    def aexp(a):
        return a.reshape(ns, S5_Q, LANES).transpose(1, 0, 2)

    return (bexp(bbr).astype(BF16), bexp(bbi).astype(BF16), cexp(c_re).astype(BF16), cexp(c_im).astype(BF16),
            aexp(ar), aexp(ai))


def _s5_contract(dbre, dbim, dcre, dcim, dar, dai, g, p, h):
    gps = LANES // h
    ns = g // gps
    eye = jnp.eye(gps, dtype=F32)
    bcon = lambda d: jnp.einsum("sghkp,gk->sgph", d.reshape(ns, gps, h, gps, p), eye).reshape(g, p, h)
    ccon = lambda d: jnp.einsum("sgpkh,gk->sghp", d.reshape(ns, gps, p, gps, h), eye).reshape(g, h, p)
    acon = lambda d: d.transpose(1, 0, 2).reshape(g, p)
    return bcon(dbre), bcon(dbim), ccon(dcre), ccon(dcim), acon(dar), acon(dai)


def _ml_proj_tile(cpre, xb, wq, wk, wv, gq, gk, gv):
    xc = _silu(cpre)
    q = _dot_nn(xc, wq)
    k = _dot_nn(xc, wk)
    v = _dot_nn(xb, wv)
    return q, k, v, _dot_nn(q, gq) + _dot_nn(k, gk) + _dot_nn(v, gv)


def ml_proj_fwd(cpre, xb, wq, wk, wv, gq, gk, gv, name):
    rows, width = cpre.shape
    nblk = width // LANES
    tr = _tile(rows, 1088, 16)

    def body(c_ref, x_ref, wq_ref, wk_ref, wv_ref, gq_ref, gk_ref, gv_ref, q_ref, k_ref, v_ref, g_ref):
        j = pl.program_id(1)
        q, k, v, g = _ml_proj_tile(c_ref[...], x_ref[...], wq_ref[...], wk_ref[...], wv_ref[...],
                                   gq_ref[...], gk_ref[...], gv_ref[...])
        q_ref[...] = q
        k_ref[...] = k
        v_ref[...] = v

        @pl.when(j == 0)
        def _():
            g_ref[...] = jnp.zeros_like(g_ref)

        g_ref[...] += g

    rb = pl.BlockSpec((tr, LANES), lambda i, j: (i, j))
    wb = pl.BlockSpec((None, LANES, LANES), lambda i, j: (j, 0, 0))
    return pl.pallas_call(
        body, name=name, grid=(rows // tr, nblk), in_specs=[rb, rb, wb, wb, wb, wb, wb, wb],
        out_specs=[rb, rb, rb, pl.BlockSpec((tr, LANES), lambda i, j: (i, 0))],
        out_shape=[jax.ShapeDtypeStruct((rows, width), F32)] * 3 + [jax.ShapeDtypeStruct((rows, LANES), F32)],
        compiler_params=_params(("arbitrary", "arbitrary")))(cpre, xb, wq, wk, wv, gq, gk, gv)


def ml_proj_bwd(cpre, xb, wq, wk, wv, gq, gk, gv, dq, dk, dv, dg, dcp_extra, name):
    rows, width = cpre.shape
    nblk = width // LANES
    tr = _tile(rows, 1088, 16)

    def body(c_ref, x_ref, wq_ref, wk_ref, wv_ref, gq_ref, gk_ref, gv_ref, dq_ref, dk_ref, dv_ref, dg_ref, e_ref,
             dc_ref, dx_ref, *dw_refs):
        i = pl.program_id(1)
        _, vjp = jax.vjp(_ml_proj_tile, c_ref[...], x_ref[...], wq_ref[...], wk_ref[...], wv_ref[...],
                         gq_ref[...], gk_ref[...], gv_ref[...])
        grads = vjp((dq_ref[...], dk_ref[...], dv_ref[...], dg_ref[...]))
        dc_ref[...] = grads[0] + e_ref[...]
        dx_ref[...] = grads[1]

        @pl.when(i == 0)
        def _():
            for r in dw_refs:
                r[...] = jnp.zeros_like(r)

        for r, gval in zip(dw_refs, grads[2:]):
            r[...] += gval

    rb = pl.BlockSpec((tr, LANES), lambda j, i: (i, j))
    wb = pl.BlockSpec((None, LANES, LANES), lambda j, i: (j, 0, 0))
    gb = pl.BlockSpec((tr, LANES), lambda j, i: (i, 0))
    wshape = jax.ShapeDtypeStruct((nblk, LANES, LANES), F32)
    return pl.pallas_call(
        body, name=name, grid=(nblk, rows // tr), in_specs=[rb, rb, wb, wb, wb, wb, wb, wb, rb, rb, rb, gb, rb],
        out_specs=[rb, rb] + [wb] * 6,
        out_shape=[jax.ShapeDtypeStruct((rows, width), F32)] * 2 + [wshape] * 6,
        compiler_params=_params(("arbitrary", "arbitrary")))(cpre, xb, wq, wk, wv, gq, gk, gv, dq, dk, dv, dg, dcp_extra)


def _ml_gates_tile(gl, bg, nh):
    x = gl + bg
    bcum = _dot(_tri(CHUNK), _log_sigmoid(x), ((1,), (0,)), precision=HI)
    lane = lax.broadcasted_iota(jnp.int32, x.shape, 1)
    return jnp.where(lane < nh, x, jnp.where(lane < 2 * nh, bcum, 0.0))


def _ml_core_tile(q, k, v, colg, rowg, cpre, zb, nw, sk, cst, nst, m_prev):
    c, dh = q.shape
    igc, bc = _lane_pick(colg, 0), _lane_pick(colg, 1)
    igr, br = _row_pick(rowg, 0), _row_pick(rowg, 1)
    causal = _tri(c) > 0
    dmat = jnp.where(causal, bc - br + igr, -jnp.inf)
    inter = bc + m_prev
    mt = lax.stop_gradient(jnp.maximum(inter, jnp.max(dmat, axis=1, keepdims=True)))
    wt = jnp.exp(dmat - mt)
    w_prev = jnp.exp(inter - mt)
    qs = q * (dh ** -0.5)
    s = _dot_nt(qs, k) * wt
    num = _dot_nn(s, v) + w_prev * _dot_nn(qs, cst)
    den = jnp.sum(s, axis=1, keepdims=True) + w_prev * jnp.sum(qs * nst, axis=1, keepdims=True)
    h = num * (1.0 / jnp.maximum(jnp.abs(den), jnp.exp(-mt)))
    last = (lax.broadcasted_iota(jnp.int32, (c, 1), 0) == c - 1).astype(F32)
    blast = jnp.sum(bc * last, axis=0, keepdims=True)
    g = blast - bc + igc
    m_new = lax.stop_gradient(jnp.maximum(blast + m_prev, jnp.max(g, axis=0, keepdims=True)))
    decay = jnp.exp(blast + m_prev - m_new)
    wk = jnp.exp(g - m_new) * k
    c_new = decay * cst + _dot_tn(wk, v)
    n_new = decay * nst + jnp.sum(wk, axis=0, keepdims=True)
    mu = jnp.mean(h, axis=1, keepdims=True)
    hc = h - mu
    var = jnp.mean(hc * hc, axis=1, keepdims=True)
    out = hc * lax.rsqrt(var + HEAD_NORM_EPS) * nw + sk * _silu(cpre)
    return out * _silu(zb), c_new, n_new, m_new


def _ml_core_specs(nc, dh, rev):
    ch = (lambda c: nc - 1 - c) if rev else (lambda c: c)
    rb = pl.BlockSpec((CHUNK, dh), lambda b_, c, h: (b_ * nc + ch(c), h))
    colb = pl.BlockSpec((None, CHUNK, 2), lambda b_, c, h: (h, b_ * nc + ch(c), 0))
    rowb = pl.BlockSpec((None, None, 2, CHUNK), lambda b_, c, h: (b_ * nc + ch(c), h, 0, 0))
    pb = pl.BlockSpec((1, dh), lambda b_, c, h: (0, h))
    cb = pl.BlockSpec((None, None, dh, dh), lambda b_, c, h: (b_ * nc + ch(c), h, 0, 0))
    nb_ = pl.BlockSpec((None, None, 1, dh), lambda b_, c, h: (b_ * nc + ch(c), h, 0, 0))
    mb = pl.BlockSpec((None, None, 1, 1), lambda b_, c, h: (b_ * nc + ch(c), h, 0, 0))
    return rb, colb, rowb, pb, cb, nb_, mb


def ml_core_fwd(q, k, v, colg, rowg, cpre, zb, nw, sk, nb, nh, name):
    rows, width = q.shape
    dh = width // nh
    nc = rows // nb // CHUNK
    rb, colb, rowb, pb, cb, nb_, mb = _ml_core_specs(nc, dh, False)

    def body(q_ref, k_ref, v_ref, col_ref, row_ref, c_ref, z_ref, nw_ref, sk_ref, y_ref, cs_ref, ns_ref, ms_ref,
             cst_ref, nst_ref, mst_ref):
        c = pl.program_id(1)
        h = pl.program_id(2)

        @pl.when(c == 0)
        def _():
            cst_ref[h] = jnp.zeros((dh, dh), F32)
            nst_ref[h] = jnp.zeros((1, dh), F32)
            mst_ref[h] = jnp.zeros((1, 1), F32)

        cst, nst, m_prev = cst_ref[h], nst_ref[h], mst_ref[h]
        cs_ref[...] = cst
        ns_ref[...] = nst
        ms_ref[...] = m_prev
        y, c_new, n_new, m_new = _ml_core_tile(q_ref[...], k_ref[...], v_ref[...], col_ref[...], row_ref[...],
                                               c_ref[...], z_ref[...], nw_ref[...], sk_ref[...], cst, nst, m_prev)
        y_ref[...] = y.astype(BF16)
        cst_ref[h] = c_new
        nst_ref[h] = n_new
        mst_ref[h] = m_new

    nbc = nb * nc
    return pl.pallas_call(
        body, name=name, grid=(nb, nc, nh), in_specs=[rb, rb, rb, colb, rowb, rb, rb, pb, pb],
        out_specs=[rb, cb, nb_, mb],
        out_shape=[jax.ShapeDtypeStruct((rows, width), BF16), jax.ShapeDtypeStruct((nbc, nh, dh, dh), F32),
                   jax.ShapeDtypeStruct((nbc, nh, 1, dh), F32), jax.ShapeDtypeStruct((nbc, nh, 1, 1), F32)],
        scratch_shapes=[pltpu.VMEM((nh, dh, dh), F32), pltpu.VMEM((nh, 1, dh), F32), pltpu.VMEM((nh, 1, 1), F32)],
        compiler_params=_params(("arbitrary", "arbitrary", "arbitrary")))(q, k, v, colg, rowg, cpre, zb, nw, sk)


def ml_core_bwd(q, k, v, colg, rowg, cpre, zb, nw, sk, cs, ns, ms, dy, nb, nh, name):
    rows, width = q.shape
    dh = width // nh
    nc = rows // nb // CHUNK
    rb, colb, rowb, pb, cb, nb_, mb = _ml_core_specs(nc, dh, True)

    def body(q_ref, k_ref, v_ref, col_ref, row_ref, c_ref, z_ref, nw_ref, sk_ref, cs_ref, ns_ref, ms_ref, dy_ref,
             dq_ref, dk_ref, dv_ref, dc_ref, dz_ref, dcol_ref, drow_ref, dnw_ref, dsk_ref, dcst_ref, dnst_ref):
        bb = pl.program_id(0)
        step = pl.program_id(1)
        h = pl.program_id(2)

        @pl.when(jnp.logical_and(bb == 0, jnp.logical_and(step == 0, h == 0)))
        def _():
            dnw_ref[...] = jnp.zeros_like(dnw_ref)
            dsk_ref[...] = jnp.zeros_like(dsk_ref)

        @pl.when(step == 0)
        def _():
            dcst_ref[h] = jnp.zeros((dh, dh), F32)
            dnst_ref[h] = jnp.zeros((1, dh), F32)

        m_prev = ms_ref[...]

        def f(*a):
            return _ml_core_tile(*a, m_prev)[:3]

        _, vjp = jax.vjp(f, q_ref[...], k_ref[...], v_ref[...], col_ref[...], row_ref[...], c_ref[...], z_ref[...],
                         nw_ref[...], sk_ref[...], cs_ref[...], ns_ref[...])
        g = vjp((dy_ref[...], dcst_ref[h], dnst_ref[h]))
        dq_ref[...] = g[0]
        dk_ref[...] = g[1]
        dv_ref[...] = g[2]
        dcol_ref[...] = g[3]
        drow_ref[...] = g[4]
        dc_ref[...] = g[5]
        dz_ref[...] = g[6].astype(dz_ref.dtype)
        dnw_ref[h] += g[7]
        dsk_ref[h] += g[8]
        dcst_ref[h] = g[9]
        dnst_ref[h] = g[10]

    nbc = nb * nc
    accb = pl.BlockSpec((nh, 1, dh), lambda b_, c, h: (0, 0, 0))
    return pl.pallas_call(
        body, name=name, grid=(nb, nc, nh), in_specs=[rb, rb, rb, colb, rowb, rb, rb, pb, pb, cb, nb_, mb, rb],
        out_specs=[rb, rb, rb, rb, rb, colb, rowb, accb, accb],
        out_shape=[jax.ShapeDtypeStruct((rows, width), F32)] * 4 + [jax.ShapeDtypeStruct((rows, width), BF16)]
        + [jax.ShapeDtypeStruct(colg.shape, F32), jax.ShapeDtypeStruct(rowg.shape, F32),
           jax.ShapeDtypeStruct((nh, 1, dh), F32), jax.ShapeDtypeStruct((nh, 1, dh), F32)],
        scratch_shapes=[pltpu.VMEM((nh, dh, dh), F32), pltpu.VMEM((nh, 1, dh), F32)],
        compiler_params=_params(("arbitrary", "arbitrary", "arbitrary")))(
            q, k, v, colg, rowg, cpre, zb, nw, sk, cs, ns, ms, dy)


def _ssd_dt_tile(dtr, bias, alog):
    dt = _softplus(dtr + bias)
    cum = _dot(_tri(CHUNK), dt * (-jnp.exp(alog)), ((1,), (0,)), precision=HI)
    return dt, cum


def _ssd_tile(xcs, bmc, cmc, cols, rows_, z, dvec, gn, states, hpg):
    npair = hpg // 2
    hd = SSD_HEAD_DIM
    xs = [_silu(x) for x in xcs]
    bm, cm = _silu(bmc), _silu(cmc)
    cb = _dot_nt(cm, bm)
    causal = _tri(CHUNK) > 0
    lane_lo = lax.broadcasted_iota(jnp.int32, (1, 2 * hd), 1) < hd
    lastsel = (lax.broadcasted_iota(jnp.int32, (CHUNK, 1), 0) == CHUNK - 1).astype(F32)
    heads = []
    for r in range(hpg):
        dtc, cumc = _lane_pick(cols, r), _lane_pick(cols, hpg + r)
        dtrow, cumr = _row_pick(rows_, r), _row_pick(rows_, hpg + r)
        w = cb * jnp.exp(jnp.where(causal, cumc - cumr, -jnp.inf)) * dtrow
        last = jnp.sum(cumc * lastsel, axis=0, keepdims=True)
        heads.append((w, jnp.exp(cumc), jnp.exp(last - cumc) * dtc, jnp.exp(last)))
    ys, new_states = [], []
    for j in range(npair):
        (wa, ea, da, la), (wb, eb, db, lb) = heads[2 * j], heads[2 * j + 1]
        yi = jnp.where(lane_lo, _dot_nn(wa, xs[j]), _dot_nn(wb, xs[j]))
        ys.append(yi + jnp.where(lane_lo, ea, eb) * _dot_nn(cm, states[j]))
        xd = xs[j] * jnp.where(lane_lo, da, db)
        new_states.append(jnp.where(lane_lo, la, lb) * states[j] + _dot_tn(bm, xd))
    y = jnp.concatenate(ys, axis=1) + dvec * jnp.concatenate(xs, axis=1)
    yg = y * _silu(z)
    yn = yg * lax.rsqrt(jnp.mean(yg * yg, axis=1, keepdims=True) + NORM_EPS) * gn
    return yn, new_states


def _ssd_specs(nc, hpg, ng, rev):
    npair = hpg // 2
    gw = hpg * SSD_HEAD_DIM
    xblocks = ng * npair
    ch = (lambda c: nc - 1 - c) if rev else (lambda c: c)
    xs = [pl.BlockSpec((CHUNK, LANES), functools.partial(lambda b_, c, g, jj: (b_ * nc + ch(c), g * npair + jj), jj=j))
          for j in range(npair)]
    bmb = pl.BlockSpec((CHUNK, SSD_STATE), lambda b_, c, g: (b_ * nc + ch(c), xblocks + g))
    cmb = pl.BlockSpec((CHUNK, SSD_STATE), lambda b_, c, g: (b_ * nc + ch(c), xblocks + ng + g))
    colb = pl.BlockSpec((None, CHUNK, 2 * hpg), lambda b_, c, g: (g, b_ * nc + ch(c), 0))
    rowb = pl.BlockSpec((None, None, 2 * hpg, CHUNK), lambda b_, c, g: (b_ * nc + ch(c), g, 0, 0))
    zb = pl.BlockSpec((CHUNK, gw), lambda b_, c, g: (b_ * nc + ch(c), g))
    pb = pl.BlockSpec((1, gw), lambda b_, c, g: (0, g))
    sb = pl.BlockSpec((None, None, npair, SSD_STATE, 2 * SSD_HEAD_DIM), lambda b_, c, g: (b_ * nc + ch(c), g, 0, 0, 0))
    return xs, bmb, cmb, colb, rowb, zb, pb, sb


def ssd_core_fwd(cpre, cols, rows_, z, dvec, gn, nb, hpg, name):
    rows = cpre.shape[0]
    inner = z.shape[1]
    ng = inner // (hpg * SSD_HEAD_DIM)
    npair = hpg // 2
    nc = rows // nb // CHUNK
    xs, bmb, cmb, colb, rowb, zb, pb, sb = _ssd_specs(nc, hpg, ng, False)

    def body(*refs):
        x_refs = refs[:npair]
        bm_ref, cm_ref, col_ref, row_ref, z_ref, d_ref, gn_ref, y_ref, so_ref, st_ref = refs[npair:]
        c = pl.program_id(1)
        g = pl.program_id(2)

        @pl.when(c == 0)
        def _():
            st_ref[g] = jnp.zeros((npair, SSD_STATE, 2 * SSD_HEAD_DIM), F32)

        so_ref[...] = st_ref[g]
        states = [st_ref[g, j] for j in range(npair)]
        yn, new_states = _ssd_tile([r[...] for r in x_refs], bm_ref[...], cm_ref[...], col_ref[...], row_ref[...],
                                   z_ref[...], d_ref[...], gn_ref[...], states, hpg)
        y_ref[...] = yn.astype(BF16)
        for j in range(npair):
            st_ref[g, j] = new_states[j]

    return pl.pallas_call(
        body, name=name, grid=(nb, nc, ng), in_specs=xs + [bmb, cmb, colb, rowb, zb, pb, pb],
        out_specs=[zb, sb],
        out_shape=[jax.ShapeDtypeStruct((rows, inner), BF16),
                   jax.ShapeDtypeStruct((nb * nc, ng, npair, SSD_STATE, 2 * SSD_HEAD_DIM), F32)],
        scratch_shapes=[pltpu.VMEM((ng, npair, SSD_STATE, 2 * SSD_HEAD_DIM), F32)],
        compiler_params=_params(("arbitrary", "arbitrary", "arbitrary")))(
            *([cpre] * npair), cpre, cpre, cols, rows_, z, dvec, gn)


def ssd_core_bwd(cpre, cols, rows_, z, dvec, gn, states, dyn, nb, hpg, name):
    rows = cpre.shape[0]
    inner = z.shape[1]
    gw = hpg * SSD_HEAD_DIM
    ng = inner // gw
    npair = hpg // 2
    nc = rows // nb // CHUNK
    xs, bmb, cmb, colb, rowb, zb, pb, sb = _ssd_specs(nc, hpg, ng, True)

    def body(*refs):
        x_refs = refs[:npair]
        (bm_ref, cm_ref, col_ref, row_ref, z_ref, d_ref, gn_ref, s_ref, dy_ref,
         dx_ref, dbm_ref, dcm_ref, dcol_ref, drow_ref, dz_ref, dd_ref, dgn_ref, dst_ref) = refs[npair:]
        bb = pl.program_id(0)
        step = pl.program_id(1)
        g = pl.program_id(2)

        @pl.when(jnp.logical_and(bb == 0, jnp.logical_and(step == 0, g == 0)))
        def _():
            dd_ref[...] = jnp.zeros_like(dd_ref)
            dgn_ref[...] = jnp.zeros_like(dgn_ref)

        @pl.when(step == 0)
        def _():
            dst_ref[g] = jnp.zeros((npair, SSD_STATE, 2 * SSD_HEAD_DIM), F32)

        def f(xcs, bmc, cmc, cv, rv, zv, dv_, gv, sts):
            return _ssd_tile(xcs, bmc, cmc, cv, rv, zv, dv_, gv, sts, hpg)

        _, vjp = jax.vjp(f, [r[...] for r in x_refs], bm_ref[...], cm_ref[...], col_ref[...], row_ref[...], z_ref[...],
                         d_ref[...], gn_ref[...], [s_ref[j] for j in range(npair)])
        gr = vjp((dy_ref[...], [dst_ref[g, j] for j in range(npair)]))
        dx_ref[...] = jnp.concatenate(gr[0], axis=1)
        dbm_ref[...] = gr[1]
        dcm_ref[...] = gr[2]
        dcol_ref[...] = gr[3]
        drow_ref[...] = gr[4]
        dz_ref[...] = gr[5].astype(dz_ref.dtype)
        dd_ref[g] += gr[6]
        dgn_ref[g] += gr[7]
        for j in range(npair):
            dst_ref[g, j] = gr[8][j]

    ch = lambda c: nc - 1 - c
    nblk = pl.BlockSpec((CHUNK, SSD_STATE), lambda b_, c, g: (b_ * nc + ch(c), g))
    accb = pl.BlockSpec((ng, 1, gw), lambda b_, c, g: (0, 0, 0))
    return pl.pallas_call(
        body, name=name, grid=(nb, nc, ng), in_specs=xs + [bmb, cmb, colb, rowb, zb, pb, pb, sb, zb],
        out_specs=[zb, nblk, nblk, colb, rowb, zb, accb, accb],
        out_shape=[jax.ShapeDtypeStruct((rows, inner), F32), jax.ShapeDtypeStruct((rows, ng * SSD_STATE), F32),
                   jax.ShapeDtypeStruct((rows, ng * SSD_STATE), F32), jax.ShapeDtypeStruct(cols.shape, F32),
                   jax.ShapeDtypeStruct(rows_.shape, F32), jax.ShapeDtypeStruct((rows, inner), BF16),
                   jax.ShapeDtypeStruct((ng, 1, gw), F32), jax.ShapeDtypeStruct((ng, 1, gw), F32)],
        scratch_shapes=[pltpu.VMEM((ng, npair, SSD_STATE, 2 * SSD_HEAD_DIM), F32)],
        compiler_params=_params(("arbitrary", "arbitrary", "arbitrary")))(
            *([cpre] * npair), cpre, cpre, cols, rows_, z, dvec, gn, states, dyn)


def _hw_expand(w):
    n, bi, _ = w.shape
    per = LANES // bi
    eye = jnp.eye(per, dtype=F32)
    return jnp.einsum("jbio,bc->jbico", w.reshape(n // per, per, bi, bi), eye).reshape(n // per, LANES, LANES)


def _hw_contract(d, bi=QKV_BLOCK):
    per = LANES // bi
    eye = jnp.eye(per, dtype=F32)
    return jnp.einsum("jbico,bc->jbio", d.reshape(d.shape[0], per, bi, per, bi), eye).reshape(-1, bi, bi)


def _wg_expand(wg, width):
    pad = jnp.pad(wg, ((0, 0), (0, LANES - wg.shape[1])))
    return [pad[i * width:(i + 1) * width].reshape(width // LANES, LANES, LANES) for i in range(3)]


def _wg_contract(dgs, ngate):
    return jnp.concatenate([d[:, :, :ngate].reshape(-1, ngate) for d in dgs], axis=0)


def _pad_lanes(a):
    return jnp.pad(a, ((0, 0), (0, LANES - a.shape[1])))


def _pairs_to_layouts(first, second, ngrp, per, nbc):
    rows = first.shape[0]
    both = jnp.concatenate([first.reshape(rows, ngrp, per), second.reshape(rows, ngrp, per)], axis=2)
    return both.transpose(1, 0, 2), both.reshape(nbc, CHUNK, ngrp, 2 * per).transpose(0, 2, 3, 1)


def _layouts_to_pairs(dcols, drows, ngrp, per):
    rows = dcols.shape[1]
    both = dcols.transpose(1, 0, 2) + drows.transpose(0, 3, 1, 2).reshape(rows, ngrp, 2 * per)
    return both[:, :, :per].reshape(rows, ngrp * per), both[:, :, per:].reshape(rows, ngrp * per)


_LATE = ("Wo0a", "Wo0b", "W1z", "W1x", "W1dt", "Wo1")


def _local_step(x, target, bw, sp, late_weights=None, late_grads=None, start_token=None):
    nb, seq, d = x.shape
    nh, hpg = MLSTM_HEADS, SSD_HPG
    t_len = N_META + seq
    nc = -(-t_len // CHUNK)
    tp = nc * CHUNK
    rows = nb * tp
    nbc = nb * nc
    meta = sp["meta_tokens"]
    h0 = jnp.concatenate([jnp.broadcast_to(meta[None], (nb, N_META, d)), x, jnp.zeros((nb, tp - t_len, d), F32)], axis=1)
    h0 = h0.reshape(rows, d)
    if start_token is not None:
        h0 = h0 + start_token
    tgt = jnp.pad(target, ((0, 0), (N_META, tp - t_len), (0, 0))).reshape(rows, d)

    n0 = norm_fwd(h0, sp["ab_norm"], "norm0")
    pa = mm(n0, bw["W0a"], "nn", "mm_pa")
    xb = mm(n0, bw["W0xb"], "nn", "mm_xb")
    zb = mm(n0, bw["W0zb"], "nn", "mm_zb")
    s5w = pa.shape[1] // 2
    mlw = xb.shape[1]
    s5_args = (sp["s5_lambda_re"], sp["s5_lambda_im"], sp["s5_log_dt"].reshape(-1), sp["s5_b_re"], sp["s5_b_im"])
    (ar, ai, bbr, bbi), s5_disc_vjp = jax.vjp(_s5_discretize, *s5_args)
    sg, spn, shh = bbr.shape
    bre, bim, cre, cim, are, aie = _s5_expand(ar, ai, bbr, bbi, sp["s5_c_re"], sp["s5_c_im"])
    ys5, gb, s5st = s5_fwd(pa, bre, bim, cre, cim, are, aie, sp["s5_d"], nb, "s5_fwd")
    tglu = mm(gb, bw["glu"], "nn", "mm_glu")

    def glu_tile(ys, tt, za, gbias):
        return _gelu(ys) * _sigmoid(tt + gbias) * _silu(za)

    ya = rowwise("glu_fwd", lambda i, ys, tt, pab, gbias: glu_tile(ys, tt, pab[:, s5w:], gbias),
                 [ys5, tglu, pa], [sp["s5_glu_b"]], [(s5w, BF16)], tr=_tile(rows, 256, 16))[0]

    cpre0 = conv_fwd(xb, sp["ml_conv_w"], sp["ml_conv_b"], nb, "ml_conv_fwd")
    wq_e, wk_e, wv_e = _hw_expand(sp["ml_wq"]), _hw_expand(sp["ml_wk"]), _hw_expand(sp["ml_wv"])
    gq, gk, gv = _wg_expand(sp["ml_w_gate"], mlw)
    q, k, v, gl = ml_proj_fwd(cpre0, xb, wq_e, wk_e, wv_e, gq, gk, gv, "ml_proj_fwd")
    bgate = _pad_lanes(sp["ml_b_gate"])
    gout = rowwise("ml_gates_fwd", lambda i, g_, b_: _ml_gates_tile(g_, b_, nh), [gl], [bgate], [(LANES, F32)], tr=CHUNK)[0]
    colg, rowg = _pairs_to_layouts(gout[:, :nh], gout[:, nh:2 * nh], nh, 1, nbc)
    yb, ml_cs, ml_ns, ml_ms = ml_core_fwd(q, k, v, colg, rowg, cpre0, zb, sp["ml_norm"], sp["ml_skip"], nb, nh, "ml_core_fwd")
    if late_weights is not None:
        bw = {**bw, **late_weights(yb)}
    h1 = mm(ya, bw["Wo0a"], "nn", "mm_out0a", resid=h0)
    h1 = mm(yb, bw["Wo0b"], "nn", "mm_out0b", resid=h1)

    n1 = norm_fwd(h1, sp["ssd_norm"], "norm1")
    z1 = mm(n1, bw["W1z"], "nn", "mm_z1")
    xbc = mm(n1, bw["W1x"], "nn", "mm_xbc")
    dtr = mm(n1, bw["W1dt"], "nn", "mm_dt")
    inner = z1.shape[1]
    ng = inner // (hpg * SSD_HEAD_DIM)
    nhd = ng * hpg
    cpre1 = conv_fwd(xbc, sp["ssd_conv_w"], sp["ssd_conv_b"], nb, "ssd_conv_fwd")
    dt_bias, a_log = _pad_lanes(sp["ssd_dt_bias"]), _pad_lanes(sp["ssd_a_log"])
    dt, cum = rowwise("ssd_dt_fwd", lambda i, r_, b_, a_: _ssd_dt_tile(r_, b_, a_), [dtr], [dt_bias, a_log],
                      [(LANES, F32), (LANES, F32)], tr=CHUNK)
    cols, rws = _pairs_to_layouts(dt[:, :nhd], cum[:, :nhd], ng, hpg, nbc)
    dvec = jnp.repeat(sp["ssd_d"], SSD_HEAD_DIM, axis=1)
    yn, ssd_st = ssd_core_fwd(cpre1, cols, rws, z1, dvec, sp["ssd_gnorm"], nb, hpg, "ssd_core_fwd")
    h2 = mm(yn, bw["Wo1"], "nn", "mm_out1", resid=h1)

    tr_l = _tile(tp, 256, 16)
    per_ex = tp // tr_l

    def loss_tile(i, hb, tb, gfn):
        tpos = (i % per_ex) * tr_l + lax.broadcasted_iota(jnp.int32, (tr_l, 1), 0)
        mask = jnp.logical_and(tpos >= N_META, tpos < t_len).astype(F32)

        def lf(hh, gg):
            e = (_rms(hh, gg) - tb) * mask
            return 0.5 * jnp.sum(e * e) / d

        lval, (dh, dg) = jax.value_and_grad(lf, (0, 1))(hb, gfn)
        return dh, dh, jnp.full((1, LANES), lval, F32), dg

    fn = sp["final_norm"].reshape(1, d)
    dh2, dh2b, loss_acc, dfn = rowwise("loss", loss_tile, [h2, tgt], [fn], [(d, F32), (d, BF16)], [(1, LANES), (1, d)], tr=tr_l)

    gbig, gs = {}, {}
    gs["final_norm"] = dfn.reshape(sp["final_norm"].shape)
    dyn = mm(dh2b, bw["Wo1"], "nt", "mm_dyn")
    gbig["Wo1"] = mm(yn, dh2b, "tn", "mm_dWo1", out_dtype=BF16)
    dxs, dbm, dcm, dcols, drws, dz1, ddvec, dgn = ssd_core_bwd(cpre1, cols, rws, z1, dvec, sp["ssd_gnorm"], ssd_st, dyn,
                                                              nb, hpg, "ssd_core_bwd")
    gs["ssd_d"] = ddvec.reshape(1, nhd, SSD_HEAD_DIM).sum(axis=2)
    gs["ssd_gnorm"] = dgn.reshape(1, inner)
    ddt, dcum = _layouts_to_pairs(dcols, drws, ng, hpg)

    def ssd_dt_bwd_tile(i, r_, ddt_, dcum_, b_, a_):
        _, vjp = jax.vjp(_ssd_dt_tile, r_, b_, a_)
        return vjp((ddt_, dcum_))

    ddtr, dbias, dalog = rowwise("ssd_dt_bwd", ssd_dt_bwd_tile, [dtr, _pad_lanes(ddt), _pad_lanes(dcum)], [dt_bias, a_log],
                                 [(LANES, BF16)], [(1, LANES), (1, LANES)], tr=CHUNK)
    gs["ssd_dt_bias"] = dbias[:, :nhd]
    gs["ssd_a_log"] = dalog[:, :nhd]
    dcpre1 = jnp.concatenate([dxs, dbm, dcm], axis=1)
    dxbc, dcw1, dcb1 = conv_bwd(dcpre1, xbc, sp["ssd_conv_w"], nb, "ssd_conv_bwd")
    gs["ssd_conv_w"] = dcw1[:sp["ssd_conv_w"].shape[0]]
    gs["ssd_conv_b"] = dcb1
    dn1 = mm(dz1, bw["W1z"], "nt", "mm_dn1z")
    dn1 = mm(dxbc, bw["W1x"], "nt", "mm_dn1x", resid=dn1)
    dn1 = mm(ddtr, bw["W1dt"], "nt", "mm_dn1dt", resid=dn1)
    gbig["W1z"] = mm(n1, dz1, "tn", "mm_dW1z", out_dtype=BF16)
    gbig["W1x"] = mm(n1, dxbc, "tn", "mm_dW1x", out_dtype=BF16)
    gbig["W1dt"] = mm(n1, ddtr, "tn", "mm_dW1dt", out_dtype=BF16)
    dh1, dh1b, dg1 = norm_bwd(h1, sp["ssd_norm"], dn1, dh2, "norm1_bwd")
    gs["ssd_norm"] = dg1

    gbig["Wo0a"] = mm(ya, dh1b, "tn", "mm_dWo0a", out_dtype=BF16)
    gbig["Wo0b"] = mm(yb, dh1b, "tn", "mm_dWo0b", out_dtype=BF16)
    wo0a = bw["Wo0a"]
    if late_grads is not None:
        wo0a = wo0a + late_grads({n: gbig[n] for n in _LATE}).astype(wo0a.dtype)
    dya = mm(dh1b, wo0a, "nt", "mm_dya")
    dyb = mm(dh1b, bw["Wo0b"], "nt", "mm_dyb")
    (dq, dk, dv, dcp_skip, dzb, dcolg, drowg, dnw, dsk) = ml_core_bwd(
        q, k, v, colg, rowg, cpre0, zb, sp["ml_norm"], sp["ml_skip"], ml_cs, ml_ns, ml_ms, dyb, nb, nh, "ml_core_bwd")
    gs["ml_norm"] = dnw.reshape(1, mlw)
    gs["ml_skip"] = dsk.reshape(1, mlw)
    dig, dbcum = _layouts_to_pairs(dcolg, drowg, nh, 1)
    dgout = _pad_lanes(jnp.concatenate([dig, dbcum], axis=1))

    def ml_gates_bwd_tile(i, g_, dgo, b_):
        _, vjp = jax.vjp(lambda a, b: _ml_gates_tile(a, b, nh), g_, b_)
        return vjp(dgo)

    dgl, dbg = rowwise("ml_gates_bwd", ml_gates_bwd_tile, [gl, dgout], [bgate], [(LANES, F32)], [(1, LANES)], tr=CHUNK)
    gs["ml_b_gate"] = dbg[:, :2 * nh]
    dcpre0, dxb_v, dwq, dwk, dwv, dgq, dgk, dgv = ml_proj_bwd(cpre0, xb, wq_e, wk_e, wv_e, gq, gk, gv, dq, dk, dv, dgl,
                                                            dcp_skip, "ml_proj_bwd")
    gs["ml_wq"], gs["ml_wk"], gs["ml_wv"] = _hw_contract(dwq), _hw_contract(dwk), _hw_contract(dwv)
    gs["ml_w_gate"] = _wg_contract([dgq, dgk, dgv], 2 * nh)
    dxb, dcw0, dcb0 = conv_bwd(dcpre0, xb, sp["ml_conv_w"], nb, "ml_conv_bwd", resid=dxb_v)
    gs["ml_conv_w"] = dcw0[:sp["ml_conv_w"].shape[0]]
    gs["ml_conv_b"] = dcb0

    def glu_bwd_tile(i, ys, tt, pab, dy_, gbias):
        _, vjp = jax.vjp(glu_tile, ys, tt, pab[:, s5w:], gbias)
        return vjp(dy_)

    dys_direct, dtglu, dza, dglub = rowwise("glu_bwd", glu_bwd_tile, [ys5, tglu, pa, dya], [sp["s5_glu_b"]],
                                            [(s5w, F32), (s5w, BF16), (s5w, BF16)], [(1, s5w)], tr=_tile(rows, 256, 16))
    gs["s5_glu_b"] = dglub
    dgb = mm(dtglu, bw["glu"], "nt", "mm_dgb")
    gbig["glu"] = mm(gb, dtglu, "tn", "mm_dglu", out_dtype=BF16)

    def gelu_bwd_tile(i, ys, dg_, direct):
        _, vjp = jax.vjp(_gelu, ys)
        return vjp(dg_)[0] + direct

    dys5 = rowwise("gelu_bwd", gelu_bwd_tile, [ys5, dgb, dys_direct], [], [(s5w, F32)], tr=_tile(rows, 256, 16))[0]
    du, dbre, dbim, dcre, dcim, dare, daie, dd5 = s5_bwd(pa, dys5, s5st, bre, bim, cre, cim, are, aie, sp["s5_d"], nb, "s5_bwd")
    gs["s5_d"] = dd5
    dbbr, dbbi, dcr, dci, dar, dai = _s5_contract(dbre, dbim, dcre, dcim, dare, daie, sg, spn, shh)
    gs["s5_c_re"], gs["s5_c_im"] = dcr, dci
    (gs["s5_lambda_re"], gs["s5_lambda_im"], dlogdt, gs["s5_b_re"], gs["s5_b_im"]) = s5_disc_vjp((dar, dai, dbbr, dbbi))
    gs["s5_log_dt"] = dlogdt.reshape(1, -1)
    dpa = jnp.concatenate([du, dza], axis=1)
    dn0 = mm(dpa, bw["W0a"], "nt", "mm_dn0a")
    dn0 = mm(dxb, bw["W0xb"], "nt", "mm_dn0xb", resid=dn0)
    dn0 = mm(dzb, bw["W0zb"], "nt", "mm_dn0zb", resid=dn0)
    gbig["W0a"] = mm(n0, dpa, "tn", "mm_dW0a", out_dtype=BF16)
    gbig["W0xb"] = mm(n0, dxb, "tn", "mm_dW0xb", out_dtype=BF16)
    gbig["W0zb"] = mm(n0, dzb, "tn", "mm_dW0zb", out_dtype=BF16)
    dh0, _, dg0 = norm_bwd(h0, sp["ab_norm"], dn0, dh1, "norm0_bwd")
    gs["ab_norm"] = dg0
    dh0 = dh0.reshape(nb, tp, d)
    gs["meta_tokens"] = jnp.sum(dh0[:, :N_META], axis=0)
    return loss_acc[0, 0], dh0, gbig, gs


N_DEV = 8
N_CHIP = 4
MESH = pl.DeviceIdType.MESH
_HBM = pl.BlockSpec(memory_space=pltpu.HBM)


def _place():
    x, y, c = lax.axis_index("x"), lax.axis_index("y"), lax.axis_index("c")
    return x, y, c, [(1 - x, y), (x, 1 - y), (1 - x, 1 - y)]


def all_gather8(v, name):
    m_per, n = v.shape

    def body(x_ref, out_ref, send_sems, recv_sems, local_sem):
        x, y, c, chips = _place()
        me, sibling = (x, y, c), (x, y, 1 - c)

        def rows(px, py, pc):
            return out_ref.at[pl.ds((4 * px + 2 * py + pc) * m_per, m_per), :]

        def copy(kk, block, to, src=None):
            return pltpu.make_async_remote_copy(
                src_ref=rows(*block) if src is None else src, dst_ref=rows(*block), send_sem=send_sems.at[kk],
                recv_sem=recv_sems.at[kk], device_id=to, device_id_type=MESH)

        mine = pltpu.make_async_copy(x_ref, rows(*me), local_sem)
        mine.start()
        first = [copy(0, me, sibling, src=x_ref)]
        first += [copy(1 + j, me, (*chip, c), src=x_ref) for j, chip in enumerate(chips)]
        for cp in first:
            cp.start()
        passed = [copy(4 + j, (*chip, c), sibling) for j, chip in enumerate(chips)]
        for j, chip in enumerate(chips):
            copy(1 + j, (*chip, c), me).wait_recv()
            passed[j].start()
        copy(0, sibling, me).wait_recv()
        for j, chip in enumerate(chips):
            copy(4 + j, (*chip, 1 - c), me).wait_recv()
        for cp in first + passed:
            cp.wait_send()
        mine.wait()

    return pl.pallas_call(
        body, name=name, out_shape=jax.ShapeDtypeStruct((N_DEV * m_per, n), v.dtype),
        in_specs=[pl.BlockSpec(memory_space=pltpu.VMEM)], out_specs=pl.BlockSpec(memory_space=pltpu.VMEM),
        scratch_shapes=[pltpu.SemaphoreType.DMA((7,)), pltpu.SemaphoreType.DMA((7,)), pltpu.SemaphoreType.DMA],
        compiler_params=pltpu.CompilerParams(vmem_limit_bytes=VMEM_LIMIT))(v)


def gather_chips(vs, name):
    na = len(vs)

    def body(*refs):
        x_refs, out_refs = refs[:na], refs[na:2 * na]
        send_sems, recv_sems, local_sems = refs[2 * na:]
        x, y, c, chips = _place()
        k = 2 * x + y
        sibling = (x, y, 1 - c)

        def copy(i, kk, src, chip_k, half, to):
            return pltpu.make_async_remote_copy(
                src_ref=src, dst_ref=out_refs[i].at[chip_k, half], send_sem=send_sems.at[6 * i + kk],
                recv_sem=recv_sems.at[6 * i + kk], device_id=to, device_id_type=MESH)

        mine = [pltpu.make_async_copy(x_refs[i], out_refs[i].at[k], local_sems.at[i]) for i in range(na)]
        for cp in mine:
            cp.start()
        first = [copy(i, j, x_refs[i].at[c], k, c, (*chip, c)) for j, chip in enumerate(chips) for i in range(na)]
        for cp in first:
            cp.start()
        passed = []
        for j, (cx, cy) in enumerate(chips):
            kj = 2 * cx + cy
            for i in range(na):
                copy(i, j, out_refs[i].at[kj, c], kj, c, (cx, cy, c)).wait_recv()
                fwd = copy(i, 3 + j, out_refs[i].at[kj, c], kj, c, sibling)
                fwd.start()
                passed.append(fwd)
        for j, (cx, cy) in enumerate(chips):
            kj = 2 * cx + cy
            for i in range(na):
                copy(i, 3 + j, out_refs[i].at[kj, 1 - c], kj, 1 - c, sibling).wait_recv()
        for cp in first + passed:
            cp.wait_send()
        for cp in mine:
            cp.wait()

    return pl.pallas_call(
        body, name=name, out_shape=[jax.ShapeDtypeStruct((N_CHIP,) + v.shape, v.dtype) for v in vs],
        in_specs=[_HBM] * na, out_specs=[_HBM] * na,
        scratch_shapes=[pltpu.SemaphoreType.DMA((6 * na,)), pltpu.SemaphoreType.DMA((6 * na,)),
                        pltpu.SemaphoreType.DMA((na,))])(*vs)


def scatter_chips(ps, name):
    na = len(ps)

    def body(*refs):
        p_refs, out_refs = refs[:na], refs[na:2 * na]
        send_sems, recv_sems, local_sems = refs[2 * na:]
        x, y, c, chips = _place()
        k = 2 * x + y
        sibling = (x, y, 1 - c)

        def copy(i, kk, src, chip_k, half, to):
            return pltpu.make_async_remote_copy(
                src_ref=src, dst_ref=out_refs[i].at[chip_k, half], send_sem=send_sems.at[7 * i + kk],
                recv_sem=recv_sems.at[7 * i + kk], device_id=to, device_id_type=MESH)

        mine = [pltpu.make_async_copy(p_refs[i].at[k], out_refs[i].at[k, c], local_sems.at[i]) for i in range(na)]
        for cp in mine:
            cp.start()
        first = [copy(i, 1 + j, p_refs[i].at[2 * cx + cy], k, c, (cx, cy, c))
                 for j, (cx, cy) in enumerate(chips) for i in range(na)]
        first += [copy(i, 0, p_refs[i].at[k], k, c, sibling) for i in range(na)]
        for cp in first:
            cp.start()
        passed = []
        for j, (cx, cy) in enumerate(chips):
            kj = 2 * cx + cy
            for i in range(na):
                copy(i, 1 + j, out_refs[i].at[kj, c], kj, c, (cx, cy, c)).wait_recv()
                fwd = copy(i, 4 + j, out_refs[i].at[kj, c], kj, c, sibling)
                fwd.start()
                passed.append(fwd)
        for i in range(na):
            copy(i, 0, out_refs[i].at[k, 1 - c], k, 1 - c, sibling).wait_recv()
        for j, (cx, cy) in enumerate(chips):
            kj = 2 * cx + cy
            for i in range(na):
                copy(i, 4 + j, out_refs[i].at[kj, 1 - c], kj, 1 - c, sibling).wait_recv()
        for cp in first + passed:
            cp.wait_send()
        for cp in mine:
            cp.wait()

    return pl.pallas_call(
        body, name=name, out_shape=[jax.ShapeDtypeStruct((N_CHIP, 2) + p.shape[1:], p.dtype) for p in ps],
        in_specs=[_HBM] * na, out_specs=[_HBM] * na,
        scratch_shapes=[pltpu.SemaphoreType.DMA((7 * na,)), pltpu.SemaphoreType.DMA((7 * na,)),
                        pltpu.SemaphoreType.DMA((na,))])(*ps)


def swap_halves(gs_, name):
    na = len(gs_)

    def body(*refs):
        g_refs, out_refs = refs[:na], refs[na:2 * na]
        send_sems, recv_sems = refs[2 * na:]
        x, y, c, _ = _place()
        cps = [pltpu.make_async_remote_copy(
            src_ref=g_refs[i].at[kk, 1 - c], dst_ref=out_refs[i].at[kk], send_sem=send_sems.at[N_CHIP * i + kk],
            recv_sem=recv_sems.at[N_CHIP * i + kk], device_id=(x, y, 1 - c), device_id_type=MESH)
            for i in range(na) for kk in range(N_CHIP)]
        for cp in cps:
            cp.start()
        for cp in cps:
            cp.wait()

    return pl.pallas_call(
        body, name=name, out_shape=[jax.ShapeDtypeStruct((N_CHIP,) + g.shape[2:], g.dtype) for g in gs_],
        in_specs=[_HBM] * na, out_specs=[_HBM] * na,
        scratch_shapes=[pltpu.SemaphoreType.DMA((N_CHIP * na,)), pltpu.SemaphoreType.DMA((N_CHIP * na,))])(*gs_)


def add_halves(g, other, core, name):
    _, _, m, n = g.shape
    tr = _tile(m, 256, 16)

    def body(core_ref, g_ref, o_ref, out_ref):
        out_ref[...] = (g_ref[...].astype(F32) + o_ref[...].astype(F32)).astype(out_ref.dtype)

    grid_spec = pltpu.PrefetchScalarGridSpec(
        num_scalar_prefetch=1, grid=(N_CHIP, m // tr),
        in_specs=[pl.BlockSpec((None, None, tr, n), lambda kk, i, core_ref: (kk, core_ref[0], i, 0)),
                  pl.BlockSpec((None, tr, n), lambda kk, i, core_ref: (kk, i, 0))],
        out_specs=pl.BlockSpec((None, tr, n), lambda kk, i, core_ref: (kk, i, 0)))
    return pl.pallas_call(body, name=name, grid_spec=grid_spec, out_shape=jax.ShapeDtypeStruct((N_CHIP, m, n), g.dtype),
                          compiler_params=_params(("arbitrary", "arbitrary")))(core.reshape(1).astype(jnp.int32), g, other)


_SEM = pl.BlockSpec(memory_space=pltpu.SEMAPHORE)
_EFFECT = pltpu.SideEffectType.DATAFLOW_SIDE_EFFECTING
N_PEER_CHIPS = N_CHIP - 1


def _ici_copy(scatter, src_refs, land_refs, send_sems, recv_sems, i, j, chips, k, c, landed):
    cx, cy = chips[j]
    kj = 2 * cx + cy
    src = src_refs[i].at[kj] if scatter else src_refs[i].at[c]
    dst = land_refs[i].at[kj, c] if landed else land_refs[i].at[k, c]
    return pltpu.make_async_remote_copy(src_ref=src, dst_ref=dst, send_sem=send_sems.at[N_PEER_CHIPS * i + j],
                                        recv_sem=recv_sems.at[N_PEER_CHIPS * i + j], device_id=(cx, cy, c),
                                        device_id_type=MESH)


def ici_start(srcs, lands, scatter, collective_id, name):
    na = len(srcs)

    def body(*refs):
        src_refs, land_refs = refs[:na], refs[na:2 * na]
        send_sems, recv_sems = refs[2 * na], refs[2 * na + 1]
        token = refs[-1]
        x, y, c, chips = _place()
        barrier = pltpu.get_barrier_semaphore()
        for cx, cy in chips:
            pl.semaphore_signal(barrier, inc=1, device_id=(cx, cy, c), device_id_type=MESH)
        pl.semaphore_wait(barrier, N_PEER_CHIPS)
        for j in range(N_PEER_CHIPS):
            for i in range(na):
                _ici_copy(scatter, src_refs, land_refs, send_sems, recv_sems, i, j, chips, 2 * x + y, c, False).start()
        token[...] = jnp.zeros_like(token)

    nsem = N_PEER_CHIPS * na
    hbm = lambda a: pltpu.HBM(a.shape, a.dtype)
    res = pl.pallas_call(
        body, name=name,
        out_shape=(pltpu.SemaphoreType.DMA((nsem,)), pltpu.SemaphoreType.DMA((nsem,)), *[hbm(a) for a in srcs],
                   *[hbm(a) for a in lands], jax.ShapeDtypeStruct((SUBLANES, LANES), F32)),
        in_specs=[_HBM] * (2 * na),
        out_specs=(_SEM, _SEM, *([_HBM] * (2 * na)), pl.BlockSpec(memory_space=pltpu.VMEM)),
        input_output_aliases={i: 2 + i for i in range(2 * na)},
        compiler_params=pltpu.CompilerParams(has_side_effects=_EFFECT, collective_id=collective_id))(
            *[pltpu.with_memory_space_constraint(a, pltpu.HBM) for a in srcs],
            *[pltpu.with_memory_space_constraint(a, pltpu.HBM) for a in lands])
    return res[0], res[1], list(res[2:2 + na]), list(res[2 + na:2 + 2 * na]), res[-1]


def ici_wait(send_sems, recv_sems, srcs, lands, after, scatter, name):
    na = len(srcs)

    def body(*refs):
        src_refs, land_refs = refs[:na], refs[na:2 * na]
        s_sems, r_sems = refs[2 * na], refs[2 * na + 1]
        x, y, c, chips = _place()
        for j in range(N_PEER_CHIPS):
            for i in range(na):
                _ici_copy(scatter, src_refs, land_refs, s_sems, r_sems, i, j, chips, 2 * x + y, c, False).wait_send()
                _ici_copy(scatter, src_refs, land_refs, s_sems, r_sems, i, j, chips, 2 * x + y, c, True).wait_recv()

    hbm = lambda a: pltpu.HBM(a.shape, a.dtype)
    res = pl.pallas_call(
        body, name=name, out_shape=tuple(hbm(a) for a in list(srcs) + list(lands)),
        in_specs=[_HBM] * (2 * na) + [_SEM, _SEM, pl.BlockSpec(memory_space=pl.ANY)], out_specs=tuple([_HBM] * (2 * na)),
        input_output_aliases={i: i for i in range(2 * na)},
        compiler_params=pltpu.CompilerParams(has_side_effects=_EFFECT))(*srcs, *lands, send_sems, recv_sems, after)
    return list(res[:na]), list(res[na:])


def sibling_finish(srcs, lands, scatter, name):
    na = len(srcs)
    per = N_CHIP if scatter else N_PEER_CHIPS

    def body(*refs):
        src_refs, land_in = refs[:na], refs[na:2 * na]
        land_refs = refs[2 * na:3 * na]
        send_sems, recv_sems, local_sems = refs[3 * na:]
        x, y, c, chips = _place()
        k = 2 * x + y
        sibling = (x, y, 1 - c)

        def copy(i, kk, src, chip_k, half):
            return pltpu.make_async_remote_copy(
                src_ref=src, dst_ref=land_refs[i].at[chip_k, half], send_sem=send_sems.at[per * i + kk],
                recv_sem=recv_sems.at[per * i + kk], device_id=sibling, device_id_type=MESH)

        if scatter:
            mine = [pltpu.make_async_copy(src_refs[i].at[k], land_refs[i].at[k, c], local_sems.at[i]) for i in range(na)]
        else:
            mine = [pltpu.make_async_copy(src_refs[i], land_refs[i].at[k], local_sems.at[i]) for i in range(na)]
        for cp in mine:
            cp.start()
        sent = []
        for j, (cx, cy) in enumerate(chips):
            kj = 2 * cx + cy
            sent += [copy(i, j, land_in[i].at[kj, c], kj, c) for i in range(na)]
        if scatter:
            sent += [copy(i, N_PEER_CHIPS, src_refs[i].at[k], k, c) for i in range(na)]
        for cp in sent:
            cp.start()
        for j, (cx, cy) in enumerate(chips):
            kj = 2 * cx + cy
            for i in range(na):
                copy(i, j, land_refs[i].at[kj, 1 - c], kj, 1 - c).wait_recv()
        if scatter:
            for i in range(na):
                copy(i, N_PEER_CHIPS, land_refs[i].at[k, 1 - c], k, 1 - c).wait_recv()
        for cp in sent:
            cp.wait_send()
        for cp in mine:
            cp.wait()

    return pl.pallas_call(
        body, name=name, out_shape=[jax.ShapeDtypeStruct(a.shape, a.dtype) for a in lands],
        in_specs=[_HBM] * (2 * na), out_specs=[_HBM] * na, input_output_aliases={na + i: i for i in range(na)},
        scratch_shapes=[pltpu.SemaphoreType.DMA((per * na,)), pltpu.SemaphoreType.DMA((per * na,)),
                        pltpu.SemaphoreType.DMA((na,))])(*srcs, *lands)


PACK_LANES = 512


def _pack(arrs, dtype, lanes, row_align):
    flat = jnp.concatenate([a.reshape(-1).astype(dtype) for a in arrs])
    unit = lanes * row_align
    total = -(-flat.shape[0] // unit) * unit
    return jnp.pad(flat, (0, total - flat.shape[0])).reshape(total // lanes, lanes)


def _unpack(flat, shapes):
    flat = flat.reshape(-1)
    out, off = [], 0
    for s in shapes:
        n = math.prod(s)
        out.append(flat[off:off + n].reshape(s))
        off += n
    return out


def _adam_tile(w, m, v, g):
    m2 = ADAM_B1 * m + (1.0 - ADAM_B1) * g
    v2 = ADAM_B2 * v + (1.0 - ADAM_B2) * (g * g)
    m_hat = m2 / (1.0 - ADAM_B1 ** ADAM_STEP)
    v_hat = v2 / (1.0 - ADAM_B2 ** ADAM_STEP)
    delta = -ADAM_LR * (m_hat / (jnp.sqrt(v_hat) + ADAM_EPS) + ADAM_WD * w)
    return delta, m2, v2


def adam_big(w, m, v, pieces, name):
    _, r, c = w.shape
    tr = _tile(r, 128, 16)

    def body(w_ref, m_ref, v_ref, p0, p1, p2, p3, g_ref, d_ref, mo_ref, vo_ref):
        g = ((p0[...].astype(F32) + p1[...].astype(F32)) + p2[...].astype(F32)) + p3[...].astype(F32)
        delta, m2, v2 = _adam_tile(w_ref[...], m_ref[...], v_ref[...], g)
        g_ref[...] = g
        d_ref[...] = delta
        mo_ref[...] = m2
        vo_ref[...] = v2

    wspec = pl.BlockSpec((None, tr, c), lambda i: (0, i, 0))
    pspecs = [pl.BlockSpec((None, tr, c), functools.partial(lambda i, kk: (kk, i, 0), kk=kk)) for kk in range(N_CHIP)]
    return pl.pallas_call(
        body, name=name, grid=(r // tr,), in_specs=[wspec] * 3 + pspecs, out_specs=[wspec] * 4,
        out_shape=[jax.ShapeDtypeStruct(w.shape, F32)] * 4, compiler_params=_params(("parallel",)))(
            w, m, v, pieces, pieces, pieces, pieces)


_WEIGHTS = (
    ("meta_tokens", "small", 1), ("ab_norm", "small", None), ("ab_w_in", "big", 2), ("s5_lambda_re", "small", None),
    ("s5_lambda_im", "small", None), ("s5_log_dt", "small", None), ("s5_b_re", "small", None), ("s5_b_im", "small", None),
    ("s5_c_re", "small", None), ("s5_c_im", "small", None), ("s5_d", "small", None), ("s5_glu_w", "big", 1),
    ("s5_glu_b", "small", None), ("ml_conv_w", "small", 2), ("ml_conv_b", "small", None), ("ml_wq", "small", 1),
    ("ml_wk", "small", 1), ("ml_wv", "small", 1), ("ml_w_gate", "small", 1), ("ml_b_gate", "small", None),
    ("ml_norm", "small", None), ("ml_skip", "small", None), ("ab_w_out", "big", 1), ("ssd_norm", "small", 1),
    ("ssd_w_in", "big", 2), ("ssd_conv_w", "small", 2), ("ssd_conv_b", "small", 1), ("ssd_dt_bias", "small", None),
    ("ssd_a_log", "small", None), ("ssd_d", "small", None), ("ssd_gnorm", "small", 1), ("ssd_w_out", "big", 1),
    ("final_norm", "small", None),
)


def _squeeze(a):
    return a[0] if a.ndim >= 3 else a


def kernel(x, meta_tokens, ab_norm, ab_w_in, s5_lambda_re, s5_lambda_im, s5_log_dt, s5_b_re, s5_b_im, s5_c_re, s5_c_im, s5_d, s5_glu_w, s5_glu_b, ml_conv_w, ml_conv_b, ml_wq, ml_wk, ml_wv, ml_w_gate, ml_b_gate, ml_norm, ml_skip, ab_w_out, ssd_norm, ssd_w_in, ssd_conv_w, ssd_conv_b, ssd_dt_bias, ssd_a_log, ssd_d, ssd_gnorm, ssd_w_out, final_norm, loss_target, m_meta_tokens, m_ab_norm, m_ab_w_in, m_s5_lambda_re, m_s5_lambda_im, m_s5_log_dt, m_s5_b_re, m_s5_b_im, m_s5_c_re, m_s5_c_im, m_s5_d, m_s5_glu_w, m_s5_glu_b, m_ml_conv_w, m_ml_conv_b, m_ml_wq, m_ml_wk, m_ml_wv, m_ml_w_gate, m_ml_b_gate, m_ml_norm, m_ml_skip, m_ab_w_out, m_ssd_norm, m_ssd_w_in, m_ssd_conv_w, m_ssd_conv_b, m_ssd_dt_bias, m_ssd_a_log, m_ssd_d, m_ssd_gnorm, m_ssd_w_out, m_final_norm, v_meta_tokens, v_ab_norm, v_ab_w_in, v_s5_lambda_re, v_s5_lambda_im, v_s5_log_dt, v_s5_b_re, v_s5_b_im, v_s5_c_re, v_s5_c_im, v_s5_d, v_s5_glu_w, v_s5_glu_b, v_ml_conv_w, v_ml_conv_b, v_ml_wq, v_ml_wk, v_ml_wv, v_ml_w_gate, v_ml_b_gate, v_ml_norm, v_ml_skip, v_ab_w_out, v_ssd_norm, v_ssd_w_in, v_ssd_conv_w, v_ssd_conv_b, v_ssd_dt_bias, v_ssd_a_log, v_ssd_d, v_ssd_gnorm, v_ssd_w_out, v_final_norm):
    args = (meta_tokens, ab_norm, ab_w_in, s5_lambda_re, s5_lambda_im, s5_log_dt, s5_b_re, s5_b_im, s5_c_re, s5_c_im, s5_d, s5_glu_w, s5_glu_b, ml_conv_w, ml_conv_b, ml_wq, ml_wk, ml_wv, ml_w_gate, ml_b_gate, ml_norm, ml_skip, ab_w_out, ssd_norm, ssd_w_in, ssd_conv_w, ssd_conv_b, ssd_dt_bias, ssd_a_log, ssd_d, ssd_gnorm, ssd_w_out, final_norm)
    m_args = (m_meta_tokens, m_ab_norm, m_ab_w_in, m_s5_lambda_re, m_s5_lambda_im, m_s5_log_dt, m_s5_b_re, m_s5_b_im, m_s5_c_re, m_s5_c_im, m_s5_d, m_s5_glu_w, m_s5_glu_b, m_ml_conv_w, m_ml_conv_b, m_ml_wq, m_ml_wk, m_ml_wv, m_ml_w_gate, m_ml_b_gate, m_ml_norm, m_ml_skip, m_ab_w_out, m_ssd_norm, m_ssd_w_in, m_ssd_conv_w, m_ssd_conv_b, m_ssd_dt_bias, m_ssd_a_log, m_ssd_d, m_ssd_gnorm, m_ssd_w_out, m_final_norm)
    v_args = (v_meta_tokens, v_ab_norm, v_ab_w_in, v_s5_lambda_re, v_s5_lambda_im, v_s5_log_dt, v_s5_b_re, v_s5_b_im, v_s5_c_re, v_s5_c_im, v_s5_d, v_s5_glu_w, v_s5_glu_b, v_ml_conv_w, v_ml_conv_b, v_ml_wq, v_ml_wk, v_ml_wv, v_ml_w_gate, v_ml_b_gate, v_ml_norm, v_ml_skip, v_ab_w_out, v_ssd_norm, v_ssd_w_in, v_ssd_conv_w, v_ssd_conv_b, v_ssd_dt_bias, v_ssd_a_log, v_ssd_d, v_ssd_gnorm, v_ssd_w_out, v_final_norm)
    names = [w[0] for w in _WEIGHTS]
    kind = {w[0]: w[1] for w in _WEIGHTS}
    axis = {w[0]: w[2] for w in _WEIGHTS}
    w_loc = dict(zip(names, args))
    m_loc = dict(zip(names, m_args))
    v_loc = dict(zip(names, v_args))
    chip = 2 * lax.axis_index("x") + lax.axis_index("y")
    core = lax.axis_index("c")
    big = [n for n in names if kind[n] == "big"]
    small = [n for n in names if kind[n] == "small"]
    small_sh = [n for n in small if axis[n] is not None]

    def halves(a):
        return a.astype(BF16).reshape(2, a.shape[1] // 2, a.shape[2])

    def assemble(n, gth):
        shard = gth.reshape((N_CHIP,) + w_loc[n].shape[1:])
        if axis[n] == 1:
            return shard.reshape(-1, shard.shape[2])
        return jnp.concatenate([shard[kk] for kk in range(N_CHIP)], axis=1)

    early = ["ab_w_in", "s5_glu_w"]
    late = ["ab_w_out", "ssd_w_in", "ssd_w_out"]
    gathered = gather_chips([halves(w_loc[n]) for n in early], "gather_early_w")
    full_big = {n: assemble(n, gth) for n, gth in zip(early, gathered)}
    after_early = (gathered[0][0, 0, 0, 0] * 0).astype(BF16)
    late_src = [halves(w_loc[n]) + after_early for n in late]
    late_land = [lax.empty((N_CHIP,) + s.shape, BF16) for s in late_src]
    w_send, w_recv, late_src, late_land, w_token = ici_start(late_src, late_land, False, 1, "gather_late_start")
    small_sh_shapes = [w_loc[n].shape for n in small_sh]
    packed_s = _pack([w_loc[n] for n in small_sh], F32, LANES, SUBLANES)
    g8 = all_gather8(packed_s, "gather_small_w").reshape(N_CHIP, 2, -1)
    sp = {}
    for n in small:
        if axis[n] is None:
            sp[n] = _squeeze(w_loc[n])
    per_chip = [_unpack(g8[kk, 0], small_sh_shapes) for kk in range(N_CHIP)]
    for i, n in enumerate(small_sh):
        sp[n] = _squeeze(jnp.concatenate([per_chip[kk][i] for kk in range(N_CHIP)], axis=axis[n]))

    s5w = full_big["s5_glu_w"].shape[0]
    mlw = w_loc["ab_w_out"].shape[1] * N_CHIP - s5w
    inner = w_loc["ssd_w_out"].shape[1] * N_CHIP
    w_in0 = full_big["ab_w_in"]
    n_heads1 = sp["ssd_d"].shape[1]
    cdim = w_loc["ssd_w_in"].shape[2] * N_CHIP - inner - n_heads1
    bw = dict(W0a=w_in0[:, :2 * s5w], W0xb=w_in0[:, 2 * s5w:2 * s5w + mlw], W0zb=w_in0[:, 2 * s5w + mlw:],
              glu=full_big["s5_glu_w"])

    def late_weights(after):
        src, land = ici_wait(w_send, w_recv, late_src, late_land, after, False, "gather_late_wait")
        fb = {n: assemble(n, gth) for n, gth in zip(late, sibling_finish(src, land, False, "gather_late_finish"))}
        w_in1 = fb["ssd_w_in"]
        return dict(Wo0a=fb["ab_w_out"][:s5w], Wo0b=fb["ab_w_out"][s5w:], W1z=w_in1[:, :inner],
                    W1x=w_in1[:, inner:inner + cdim], W1dt=_pad_lanes(w_in1[:, inner + cdim:]), Wo1=fb["ssd_w_out"])

    def chip_halves(n, gf):
        _, r, c_ = w_loc[n].shape
        if axis[n] == 1:
            return gf.reshape(N_CHIP, 2, r // 2, c_)
        return jnp.stack([gf[:, kk * c_:(kk + 1) * c_] for kk in range(N_CHIP)]).reshape(N_CHIP, 2, r // 2, c_)

    def chip_partials(ns, gfull, tag):
        gps = [chip_halves(n, gfull[n]) for n in ns]
        from_sibling = swap_halves(gps, "swap_" + tag)
        return [add_halves(gp, oth, core, "add_" + n) for n, gp, oth in zip(ns, gps, from_sibling)]

    late_state = {}

    def late_grads(g):
        gfull = {"ab_w_out": jnp.concatenate([g["Wo0a"], g["Wo0b"]], axis=0),
                 "ssd_w_in": jnp.concatenate([g["W1z"], g["W1x"], g["W1dt"][:, :n_heads1]], axis=1),
                 "ssd_w_out": g["Wo1"]}
        partials = chip_partials(late, gfull, "late_g")
        land = [lax.empty((N_CHIP, 2) + p.shape[1:], BF16) for p in partials]
        late_state["copy"] = ici_start(partials, land, True, 2, "scatter_late_start")
        return late_state["copy"][4][0, 0]

    loss_local, dh0, gbig, gs = _local_step(x, loss_target, bw, sp, late_weights, late_grads, w_token[0, 0])
    loss = lax.psum(loss_local, ("x", "y", "c"))
    grad_x = dh0[:, N_META:N_META + x.shape[1]]

    gfull = {"ab_w_in": jnp.concatenate([gbig["W0a"], gbig["W0xb"], gbig["W0zb"]], axis=1), "s5_glu_w": gbig["glu"]}
    pieces = dict(zip(early, scatter_chips(chip_partials(early, gfull, "early_g"), "scatter_early_g")))
    g_send, g_recv, g_src, g_land, _ = late_state["copy"]
    g_src, g_land = ici_wait(g_send, g_recv, g_src, g_land, dh0, True, "scatter_late_wait")
    pieces.update(zip(late, sibling_finish(g_src, g_land, True, "scatter_late_finish")))

    out_g, out_d, out_m, out_v = {}, {}, {}, {}
    for n in big:
        pcs = pieces[n].reshape((N_CHIP,) + w_loc[n].shape[1:])
        out_g[n], out_d[n], out_m[n], out_v[n] = adam_big(w_loc[n], m_loc[n], v_loc[n], pcs, "adam_" + n)

    small_full_shapes = [sp[n].shape for n in small]
    packed_gs = _pack([gs[n] for n in small], F32, LANES, SUBLANES)
    rows_s = packed_gs.shape[0]
    all_gs = all_gather8(packed_gs, "gather_small_g")
    blocks = [all_gs[i * rows_s:(i + 1) * rows_s] for i in range(N_DEV)]

    def sum8(i, *b):
        acc = b[0]
        for t in b[1:]:
            acc = acc + t
        return acc

    gsum = rowwise("sum_small_g", sum8, blocks, [], [(LANES, F32)], tr=_tile(rows_s, 512, 8))[0]
    g_small = dict(zip(small, _unpack(gsum, small_full_shapes)))
    g_loc = {}
    for n in small:
        g = g_small[n].reshape((1,) + g_small[n].shape) if w_loc[n].ndim >= 3 else g_small[n]
        if axis[n] is not None:
            size = w_loc[n].shape[axis[n]]
            g = lax.dynamic_slice_in_dim(g, chip * size, size, axis=axis[n])
        g_loc[n] = g.reshape(w_loc[n].shape)
    loc_shapes = [w_loc[n].shape for n in small]
    pw, pm, pv, pg = (_pack([d[n] for n in small], F32, LANES, SUBLANES) for d in (w_loc, m_loc, v_loc, g_loc))
    dl, mn, vn = rowwise("adam_small", lambda i, a, b, c_, d_: _adam_tile(a, b, c_, d_), [pw, pm, pv, pg], [],
                         [(LANES, F32)] * 3, tr=_tile(pw.shape[0], 512, 8))
    for d_out, flat in ((out_d, dl), (out_m, mn), (out_v, vn)):
        for n, a in zip(small, _unpack(flat, loc_shapes)):
            d_out[n] = a
    for n in small:
        out_g[n] = g_loc[n]

    return (loss, grad_x, *[out_g[n] for n in names], *[out_d[n] for n in names], *[out_m[n] for n in names],
            *[out_v[n] for n in names])
```

```python
import functools
import math

import jax
import jax.numpy as jnp
from jax import lax
from jax.experimental import pallas as pl
from jax.experimental.pallas import tpu as pltpu

F32 = jnp.float32
BF16 = jnp.bfloat16
HI = lax.Precision.HIGHEST

D_MODEL = 2048
SEQ = 2048
N_META = 16
CHUNK = 128
NORM_EPS = 1e-6
HEAD_NORM_EPS = 1e-5
S5_GROUP_SIZE = 16
S5_STATE = 64
MLSTM_HEADS = 8
QKV_BLOCK = 4
SSD_HEAD_DIM = 64
SSD_STATE = 128
SSD_HPG = 8
ADAM_LR = 0.001
ADAM_B1 = 0.9
ADAM_B2 = 0.999
ADAM_EPS = 1e-08
ADAM_WD = 0.01
ADAM_STEP = 10

LANES = 128
SUBLANES = 8
VMEM_LIMIT = 56 * 1024 * 1024
MM_OPERAND_VMEM = 34 * 1024 * 1024


def _sigmoid(x):
    return 0.5 * jnp.tanh(0.5 * x) + 0.5


@jax.custom_vjp
def _silu(x):
    return x * _sigmoid(x)


def _silu_fwd(x):
    return x * _sigmoid(x), x


def _silu_bwd(x, ct):
    s = _sigmoid(x)
    return (ct * (s * (1.0 + x * (1.0 - s))),)


_silu.defvjp(_silu_fwd, _silu_bwd)


def _softplus(x):
    return jnp.maximum(x, 0.0) + jnp.log(1.0 + jnp.exp(-jnp.abs(x)))


def _log_sigmoid(x):
    return jnp.minimum(x, 0.0) - jnp.log(1.0 + jnp.exp(-jnp.abs(x)))


def _gelu(x):
    return 0.5 * x * (1.0 + jnp.tanh(math.sqrt(2.0 / math.pi) * (x + 0.044715 * (x * x * x))))


def _dot(a, b, dims, precision=None):
    return lax.dot_general(a, b, (dims, ((), ())), preferred_element_type=F32, precision=precision)


_NN, _NT, _TN = ((1,), (0,)), ((1,), (1,)), ((0,), (0,))


def _bf16_dot(dims, da_rule, db_rule):
    @jax.custom_vjp
    def f(a, b):
        return _dot(a.astype(BF16), b.astype(BF16), dims)

    def fwd(a, b):
        ab, bb = a.astype(BF16), b.astype(BF16)
        return _dot(ab, bb, dims), (ab, bb, jnp.zeros((), a.dtype), jnp.zeros((), b.dtype))

    def bwd(res, ct):
        ab, bb, a_like, b_like = res
        cb = ct.astype(BF16)
        return da_rule(ab, bb, cb).astype(a_like.dtype), db_rule(ab, bb, cb).astype(b_like.dtype)

    f.defvjp(fwd, bwd)
    return f


_dot_nn = _bf16_dot(_NN, lambda a, b, c: _dot(c, b, _NT), lambda a, b, c: _dot(a, c, _TN))
_dot_nt = _bf16_dot(_NT, lambda a, b, c: _dot(c, b, _NN), lambda a, b, c: _dot(c, a, _TN))
_dot_tn = _bf16_dot(_TN, lambda a, b, c: _dot(b, c, _NT), lambda a, b, c: _dot(a, c, _NN))


def _lane_pick(a, idx):
    sel = (lax.broadcasted_iota(jnp.int32, (1, a.shape[1]), 1) == idx).astype(a.dtype)
    return jnp.sum(a * sel, axis=1, keepdims=True)


def _row_pick(a, idx):
    sel = (lax.broadcasted_iota(jnp.int32, (a.shape[0], 1), 0) == idx).astype(a.dtype)
    return jnp.sum(a * sel, axis=0, keepdims=True)


def _tri(n, upper=False):
    r = lax.broadcasted_iota(jnp.int32, (n, n), 0)
    c = lax.broadcasted_iota(jnp.int32, (n, n), 1)
    return ((r <= c) if upper else (r >= c)).astype(F32)


def _tile(n, target, align):
    if n <= target:
        return n
    t = (target // align) * align
    while t >= align:
        if n % t == 0:
            return t
        t -= align
    return n


def _params(sem=None):
    return pltpu.CompilerParams(dimension_semantics=sem, vmem_limit_bytes=VMEM_LIMIT)


def mm(a, b, mode, name, resid=None, out_dtype=F32):
    if mode == "nn":
        (m, k), (k2, n) = a.shape, b.shape
    elif mode == "nt":
        (m, k), (n, k2) = a.shape, b.shape
    else:
        (k, m), (k2, n) = a.shape, b.shape
    assert k == k2, (a.shape, b.shape, mode)
    a_sz, b_sz = a.dtype.itemsize, b.dtype.itemsize
    if mode == "tn":
        tm, tn = _tile(m, 1024, LANES), _tile(n, 1024, LANES)
        tk = _tile(k, MM_OPERAND_VMEM // (2 * (tm * a_sz + tn * b_sz)), 16)
    else:
        tm, tn = _tile(m, 1088, 16), _tile(n, 512, LANES)
        tk = _tile(k, MM_OPERAND_VMEM // (2 * (tm * a_sz + tn * b_sz)), LANES)
    nk = k // tk
    dims = {"nn": ((1,), (0,)), "nt": ((1,), (1,)), "tn": ((0,), (0,))}[mode]
    has_resid = resid is not None

    def body(*refs):
        if has_resid:
            a_ref, b_ref, r_ref, o_ref = refs[:4]
        else:
            a_ref, b_ref, o_ref = refs[:3]
        part = _dot(a_ref[...].astype(BF16), b_ref[...].astype(BF16), dims)

        def finish(res):
            if has_resid:
                res = res + r_ref[...].astype(F32)
            o_ref[...] = res.astype(o_ref.dtype)

        if nk == 1:
            finish(part)
            return
        acc_ref = refs[-1]
        kk = pl.program_id(2)

        @pl.when(kk == 0)
        def _():
            acc_ref[...] = part

        @pl.when(jnp.logical_and(kk > 0, kk < nk - 1))
        def _():
            acc_ref[...] += part

        @pl.when(kk == nk - 1)
        def _():
            finish(acc_ref[...] + part)

    if mode == "tn":
        a_spec = pl.BlockSpec((tk, tm), lambda i, j, kk: (kk, i))
    else:
        a_spec = pl.BlockSpec((tm, tk), lambda i, j, kk: (i, kk))
    if mode == "nt":
        b_spec = pl.BlockSpec((tn, tk), lambda i, j, kk: (j, kk))
    else:
        b_spec = pl.BlockSpec((tk, tn), lambda i, j, kk: (kk, j))
    o_spec = pl.BlockSpec((tm, tn), lambda i, j, kk: (i, j))
    in_specs = [a_spec, b_spec] + ([o_spec] if has_resid else [])
    args = (a, b) + ((resid,) if has_resid else ())
    return pl.pallas_call(
        body, name=name, grid=(m // tm, n // tn, nk), in_specs=in_specs, out_specs=o_spec,
        out_shape=jax.ShapeDtypeStruct((m, n), out_dtype), scratch_shapes=[pltpu.VMEM((tm, tn), F32)] if nk > 1 else [],
        compiler_params=_params(("parallel", "parallel", "arbitrary")))(*args)


def rowwise(name, f, rows, params, outs, accs=(), tr=128):
    n_rows = rows[0].shape[0]
    assert n_rows % tr == 0
    n_r, n_p, n_o, n_a = len(rows), len(params), len(outs), len(accs)

    def body(*refs):
        i = pl.program_id(0)
        r_vals = [r[...] for r in refs[:n_r]]
        p_vals = [r[...] for r in refs[n_r:n_r + n_p]]
        o_refs = refs[n_r + n_p:n_r + n_p + n_o]
        a_refs = refs[n_r + n_p + n_o:]
        res = f(i, *r_vals, *p_vals)
        if not isinstance(res, (tuple, list)):
            res = (res,)
        assert len(res) == n_o + n_a, (name, len(res))
        for o_ref, val in zip(o_refs, res[:n_o]):
            o_ref[...] = val.astype(o_ref.dtype)
        if n_a:
            @pl.when(i == 0)
            def _():
                for a_ref in a_refs:
                    a_ref[...] = jnp.zeros_like(a_ref)

            for a_ref, val in zip(a_refs, res[n_o:]):
                a_ref[...] += val.astype(F32)

    in_specs = [pl.BlockSpec((tr, r.shape[1]), lambda i: (i, 0)) for r in rows]
    in_specs += [pl.BlockSpec(p.shape, lambda i: (0, 0)) for p in params]
    out_specs = [pl.BlockSpec((tr, w), lambda i: (i, 0)) for w, _ in outs]
    out_specs += [pl.BlockSpec(s, lambda i: (0, 0)) for s in accs]
    out_shape = [jax.ShapeDtypeStruct((n_rows, w), dt) for w, dt in outs]
    out_shape += [jax.ShapeDtypeStruct(s, F32) for s in accs]
    res = pl.pallas_call(
        body, name=name, grid=(n_rows // tr,), in_specs=in_specs, out_specs=out_specs, out_shape=out_shape,
        compiler_params=_params(("arbitrary",)))(*rows, *params)
    return res


def _rms(x, g, eps=NORM_EPS):
    return x * lax.rsqrt(jnp.mean(x * x, axis=-1, keepdims=True) + eps) * g


def norm_fwd(x, g, name):
    return rowwise(name, lambda i, xb, gb: _rms(xb, gb), [x], [g], [(x.shape[1], BF16)], tr=_tile(x.shape[0], 256, 16))[0]


def norm_bwd(x, g, dn, resid, name):
    def f(i, xb, dnb, rb, gb):
        _, vjp = jax.vjp(_rms, xb, gb)
        dx, dg = vjp(dnb)
        return dx + rb, dx + rb, dg

    return rowwise(name, f, [x, dn, resid], [g], [(x.shape[1], F32), (x.shape[1], BF16)], [g.shape],
                   tr=_tile(x.shape[0], 256, 16))


def conv_fwd(x, w, b, nb, name):
    rows, width = x.shape
    nc = rows // nb // CHUNK
    tw = _tile(width, 1024, LANES)
    ksz = w.shape[0]

    def body(x_ref, w_ref, b_ref, o_ref, ext_ref):
        c = pl.program_id(2)

        @pl.when(c == 0)
        def _():
            ext_ref[0:SUBLANES, :] = jnp.zeros((SUBLANES, tw), F32)

        xv = x_ref[...]
        ext_ref[SUBLANES:SUBLANES + CHUNK, :] = xv
        acc = jnp.broadcast_to(b_ref[...], (CHUNK, tw))
        for j in range(ksz):
            off = SUBLANES - (ksz - 1) + j
            acc = acc + w_ref[j:j + 1, :] * ext_ref[off:off + CHUNK, :]
        o_ref[...] = acc
        ext_ref[0:SUBLANES, :] = xv[CHUNK - SUBLANES:CHUNK, :]

    return pl.pallas_call(
        body, name=name, grid=(width // tw, nb, nc),
        in_specs=[pl.BlockSpec((CHUNK, tw), lambda j, bb, c: (bb * nc + c, j)),
                  pl.BlockSpec((ksz, tw), lambda j, bb, c: (0, j)),
                  pl.BlockSpec((1, tw), lambda j, bb, c: (0, j))],
        out_specs=pl.BlockSpec((CHUNK, tw), lambda j, bb, c: (bb * nc + c, j)),
        out_shape=jax.ShapeDtypeStruct((rows, width), F32),
        scratch_shapes=[pltpu.VMEM((CHUNK + 2 * SUBLANES, tw), F32)],
        compiler_params=_params(("arbitrary", "arbitrary", "arbitrary")))(x, w, b)


def conv_bwd(dc, x, w, nb, name, resid=None, dx_dtype=BF16):
    rows, width = x.shape
    nc = rows // nb // CHUNK
    tw = _tile(width, 1024, LANES)
    ksz = w.shape[0]
    per = CHUNK // SUBLANES
    has_resid = resid is not None

    def body(*refs):
        if has_resid:
            dc_ref, x_ref, halo_ref, w_ref, r_ref, dx_ref, dw_ref, db_ref, extd_ref, extx_ref = refs
        else:
            dc_ref, x_ref, halo_ref, w_ref, dx_ref, dw_ref, db_ref, extd_ref, extx_ref = refs
        bb = pl.program_id(1)
        step = pl.program_id(2)
        c = nc - 1 - step

        @pl.when(jnp.logical_and(bb == 0, step == 0))
        def _():
            dw_ref[...] = jnp.zeros_like(dw_ref)
            db_ref[...] = jnp.zeros_like(db_ref)

        @pl.when(step == 0)
        def _():
            extd_ref[CHUNK:CHUNK + SUBLANES, :] = jnp.zeros((SUBLANES, tw), F32)

        dcv = dc_ref[...]
        extd_ref[0:CHUNK, :] = dcv
        extx_ref[0:SUBLANES, :] = jnp.where(c == 0, 0.0, halo_ref[...])
        extx_ref[SUBLANES:SUBLANES + CHUNK, :] = x_ref[...]
        dx = jnp.zeros((CHUNK, tw), F32)
        for j in range(ksz):
            up = ksz - 1 - j
            dx = dx + w_ref[j:j + 1, :] * extd_ref[up:up + CHUNK, :]
            off = SUBLANES - (ksz - 1) + j
            dw_ref[j:j + 1, :] += jnp.sum(dcv * extx_ref[off:off + CHUNK, :], axis=0, keepdims=True)
        if has_resid:
            dx = dx + r_ref[...]
        dx_ref[...] = dx.astype(dx_ref.dtype)
        db_ref[...] += jnp.sum(dcv, axis=0, keepdims=True)
        extd_ref[CHUNK:CHUNK + SUBLANES, :] = dcv[0:SUBLANES, :]

    def blk(j, bb, step):
        return (bb * nc + nc - 1 - step, j)

    def halo(j, bb, step):
        return (jnp.maximum((bb * nc + nc - 1 - step) * per - 1, 0), j)

    in_specs = [pl.BlockSpec((CHUNK, tw), blk), pl.BlockSpec((CHUNK, tw), blk), pl.BlockSpec((SUBLANES, tw), halo),
                pl.BlockSpec((ksz, tw), lambda j, bb, step: (0, j))]
    args = [dc, x, x, w]
    if has_resid:
        in_specs.append(pl.BlockSpec((CHUNK, tw), blk))
        args.append(resid)
    return pl.pallas_call(
        body, name=name, grid=(width // tw, nb, nc), in_specs=in_specs,
        out_specs=[pl.BlockSpec((CHUNK, tw), blk), pl.BlockSpec((SUBLANES, tw), lambda j, bb, step: (0, j)),
                   pl.BlockSpec((1, tw), lambda j, bb, step: (0, j))],
        out_shape=[jax.ShapeDtypeStruct((rows, width), dx_dtype), jax.ShapeDtypeStruct((SUBLANES, width), F32),
                   jax.ShapeDtypeStruct((1, width), F32)],
        scratch_shapes=[pltpu.VMEM((CHUNK + 2 * SUBLANES, tw), F32), pltpu.VMEM((CHUNK + 2 * SUBLANES, tw), F32)],
        compiler_params=_params(("arbitrary", "arbitrary", "arbitrary")))(*args)


S5_Q = 4


def _s5_fill_bu(u, bre_ref, bim_ref, xr_ref, xi_ref, ns):
    for s in range(ns):
        ub = u[:, s * LANES:(s + 1) * LANES].astype(BF16)
        bur = _dot(ub, bre_ref[s], ((1,), (0,)))
        bui = _dot(ub, bim_ref[s], ((1,), (0,)))
        for q in range(S5_Q):
            xr_ref[q, pl.ds(s, CHUNK, stride=ns), :] = bur[:, q * LANES:(q + 1) * LANES]
            xi_ref[q, pl.ds(s, CHUNK, stride=ns), :] = bui[:, q * LANES:(q + 1) * LANES]


def _s5_scan(xr_ref, xi_ref, ar_ref, ai_ref, st_ref, ns):
    ar = [ar_ref[q] for q in range(S5_Q)]
    ai = [ai_ref[q] for q in range(S5_Q)]

    def step(t, carry):
        rows = pl.ds(pl.multiple_of(t * ns, ns), ns)
        out = []
        for q in range(S5_Q):
            pr, pi_ = carry[2 * q], carry[2 * q + 1]
            nr = ar[q] * pr - ai[q] * pi_ + xr_ref[q, rows, :]
            ni = ar[q] * pi_ + ai[q] * pr + xi_ref[q, rows, :]
            xr_ref[q, rows, :] = nr
            xi_ref[q, rows, :] = ni
            out += [nr, ni]
        return tuple(out)

    init = []
    for q in range(S5_Q):
        init += [st_ref[0, q], st_ref[1, q]]
    fin = lax.fori_loop(0, CHUNK, step, tuple(init), unroll=2)
    for q in range(S5_Q):
        st_ref[0, q] = fin[2 * q]
        st_ref[1, q] = fin[2 * q + 1]


def s5_fwd(pa, bre, bim, cre, cim, ar, ai, dvec, nb, name):
    rows = pa.shape[0]
    width = pa.shape[1] // 2
    ns = width // LANES
    nc = rows // nb // CHUNK

    def body(u_ref, bre_ref, bim_ref, cre_ref, cim_ref, ar_ref, ai_ref, d_ref, y_ref, g_ref, so_ref, xr_ref, xi_ref, st_ref):
        c = pl.program_id(1)

        @pl.when(c == 0)
        def _():
            st_ref[...] = jnp.zeros_like(st_ref)

        so_ref[...] = st_ref[...]
        u = u_ref[...]
        _s5_fill_bu(u, bre_ref, bim_ref, xr_ref, xi_ref, ns)
        _s5_scan(xr_ref, xi_ref, ar_ref, ai_ref, st_ref, ns)
        for s in range(ns):
            acc = jnp.zeros((CHUNK, LANES), F32)
            for q in range(S5_Q):
                xr = xr_ref[q, pl.ds(s, CHUNK, stride=ns), :].astype(BF16)
                xi = xi_ref[q, pl.ds(s, CHUNK, stride=ns), :].astype(BF16)
                acc = acc + _dot(xr, cre_ref[s, q * LANES:(q + 1) * LANES, :], ((1,), (0,)))
                acc = acc - _dot(xi, cim_ref[s, q * LANES:(q + 1) * LANES, :], ((1,), (0,)))
            cols = slice(s * LANES, (s + 1) * LANES)
            y = acc + d_ref[:, cols] * u[:, cols]
            y_ref[:, cols] = y
            g_ref[:, cols] = _gelu(y).astype(BF16)

    whole3 = lambda a: pl.BlockSpec(a.shape, lambda b_, c: (0, 0, 0))
    return pl.pallas_call(
        body, name=name, grid=(nb, nc),
        in_specs=[pl.BlockSpec((CHUNK, width), lambda b_, c: (b_ * nc + c, 0)), whole3(bre), whole3(bim), whole3(cre),
                  whole3(cim), whole3(ar), whole3(ai), pl.BlockSpec((1, width), lambda b_, c: (0, 0))],
        out_specs=[pl.BlockSpec((CHUNK, width), lambda b_, c: (b_ * nc + c, 0)),
                   pl.BlockSpec((CHUNK, width), lambda b_, c: (b_ * nc + c, 0)),
                   pl.BlockSpec((None, 2, S5_Q, ns, LANES), lambda b_, c: (b_ * nc + c, 0, 0, 0, 0))],
        out_shape=[jax.ShapeDtypeStruct((rows, width), F32), jax.ShapeDtypeStruct((rows, width), BF16),
                   jax.ShapeDtypeStruct((nb * nc, 2, S5_Q, ns, LANES), F32)],
        scratch_shapes=[pltpu.VMEM((S5_Q, CHUNK * ns, LANES), F32), pltpu.VMEM((S5_Q, CHUNK * ns, LANES), F32),
                        pltpu.VMEM((2, S5_Q, ns, LANES), F32)],
        compiler_params=_params(("arbitrary", "arbitrary")))(pa, bre, bim, cre, cim, ar, ai, dvec)


def s5_bwd(pa, dys, states, bre, bim, cre, cim, ar, ai, dvec, nb, name):
    rows = pa.shape[0]
    width = pa.shape[1] // 2
    ns = width // LANES
    nc = rows // nb // CHUNK

    def body(u_ref, dy_ref, sin_ref, bre_ref, bim_ref, cre_ref, cim_ref, ar_ref, ai_ref, d_ref,
             du_ref, dbre_ref, dbim_ref, dcre_ref, dcim_ref, dar_ref, dai_ref, dd_ref,
             xr_ref, xi_ref, lr_ref, li_ref, st_ref, lam_ref):
        bb = pl.program_id(0)
        step_i = pl.program_id(1)

        @pl.when(jnp.logical_and(bb == 0, step_i == 0))
        def _():
            for r in (dbre_ref, dbim_ref, dcre_ref, dcim_ref, dar_ref, dai_ref, dd_ref):
                r[...] = jnp.zeros_like(r)

        @pl.when(step_i == 0)
        def _():
            lam_ref[...] = jnp.zeros_like(lam_ref)

        u = u_ref[...]
        dy = dy_ref[...]
        st_ref[...] = sin_ref[...]
        _s5_fill_bu(u, bre_ref, bim_ref, xr_ref, xi_ref, ns)
        _s5_scan(xr_ref, xi_ref, ar_ref, ai_ref, st_ref, ns)
        dd_ref[...] += jnp.sum(dy * u, axis=0, keepdims=True)
        for s in range(ns):
            dyb = dy[:, s * LANES:(s + 1) * LANES].astype(BF16)
            gr = _dot(dyb, cre_ref[s], ((1,), (1,)))
            gi = -_dot(dyb, cim_ref[s], ((1,), (1,)))
            for q in range(S5_Q):
                lr_ref[q, pl.ds(s, CHUNK, stride=ns), :] = gr[:, q * LANES:(q + 1) * LANES]
                li_ref[q, pl.ds(s, CHUNK, stride=ns), :] = gi[:, q * LANES:(q + 1) * LANES]
                xr = xr_ref[q, pl.ds(s, CHUNK, stride=ns), :].astype(BF16)
                xi = xi_ref[q, pl.ds(s, CHUNK, stride=ns), :].astype(BF16)
                dcre_ref[s, q * LANES:(q + 1) * LANES, :] += _dot(xr, dyb, ((0,), (0,)))
                dcim_ref[s, q * LANES:(q + 1) * LANES, :] -= _dot(xi, dyb, ((0,), (0,)))
        ar = [ar_ref[q] for q in range(S5_Q)]
        ai = [ai_ref[q] for q in range(S5_Q)]

        def one(t_rows, p_r, p_i, carry):
            out = []
            for q in range(S5_Q):
                l_r, l_i, da_r, da_i = carry[4 * q:4 * q + 4]
                n_r = lr_ref[q, t_rows, :] + ar[q] * l_r + ai[q] * l_i
                n_i = li_ref[q, t_rows, :] + ar[q] * l_i - ai[q] * l_r
                lr_ref[q, t_rows, :] = n_r
                li_ref[q, t_rows, :] = n_i
                xpr, xpi = p_r(q), p_i(q)
                out += [n_r, n_i, da_r + n_r * xpr + n_i * xpi, da_i + n_i * xpr - n_r * xpi]
            return tuple(out)

        def step(k, carry):
            t = CHUNK - 1 - k
            t_rows = pl.ds(pl.multiple_of(t * ns, ns), ns)
            p_rows = pl.ds(pl.multiple_of((t - 1) * ns, ns), ns)
            return one(t_rows, lambda q: xr_ref[q, p_rows, :], lambda q: xi_ref[q, p_rows, :], carry)

        init = []
        zero = jnp.zeros((ns, LANES), F32)
        for q in range(S5_Q):
            init += [lam_ref[0, q], lam_ref[1, q], zero, zero]
        carry = lax.fori_loop(0, CHUNK - 1, step, tuple(init), unroll=2)
        carry = one(pl.ds(0, ns), lambda q: sin_ref[0, q], lambda q: sin_ref[1, q], carry)
        for q in range(S5_Q):
            lam_ref[0, q] = carry[4 * q]
            lam_ref[1, q] = carry[4 * q + 1]
            dar_ref[q] += carry[4 * q + 2]
            dai_ref[q] += carry[4 * q + 3]
        for s in range(ns):
            cols = slice(s * LANES, (s + 1) * LANES)
            ub = u[:, cols].astype(BF16)
            acc = d_ref[:, cols] * dy[:, cols]
            for q in range(S5_Q):
                qs = slice(q * LANES, (q + 1) * LANES)
                lr = lr_ref[q, pl.ds(s, CHUNK, stride=ns), :].astype(BF16)
                li = li_ref[q, pl.ds(s, CHUNK, stride=ns), :].astype(BF16)
                dbre_ref[s, :, qs] += _dot(ub, lr, ((0,), (0,)))
                dbim_ref[s, :, qs] += _dot(ub, li, ((0,), (0,)))
                acc = acc + _dot(lr, bre_ref[s, :, qs], ((1,), (1,))) + _dot(li, bim_ref[s, :, qs], ((1,), (1,)))
            du_ref[:, cols] = acc.astype(du_ref.dtype)

    whole3 = lambda a: pl.BlockSpec(a.shape, lambda b_, c: (0, 0, 0))
    rowblk = pl.BlockSpec((CHUNK, width), lambda b_, c: (b_ * nc + nc - 1 - c, 0))
    scr = pltpu.VMEM((S5_Q, CHUNK * ns, LANES), F32)
    return pl.pallas_call(
        body, name=name, grid=(nb, nc),
        in_specs=[rowblk, rowblk,
                  pl.BlockSpec((None, 2, S5_Q, ns, LANES), lambda b_, c: (b_ * nc + nc - 1 - c, 0, 0, 0, 0)),
                  whole3(bre), whole3(bim), whole3(cre), whole3(cim), whole3(ar), whole3(ai),
                  pl.BlockSpec((1, width), lambda b_, c: (0, 0))],
        out_specs=[rowblk, whole3(bre), whole3(bim), whole3(cre), whole3(cim), whole3(ar), whole3(ai),
                   pl.BlockSpec((1, width), lambda b_, c: (0, 0))],
        out_shape=[jax.ShapeDtypeStruct((rows, width), BF16), jax.ShapeDtypeStruct(bre.shape, F32),
                   jax.ShapeDtypeStruct(bim.shape, F32), jax.ShapeDtypeStruct(cre.shape, F32),
                   jax.ShapeDtypeStruct(cim.shape, F32), jax.ShapeDtypeStruct(ar.shape, F32),
                   jax.ShapeDtypeStruct(ai.shape, F32), jax.ShapeDtypeStruct((1, width), F32)],
        scratch_shapes=[scr, scr, scr, scr, pltpu.VMEM((2, S5_Q, ns, LANES), F32), pltpu.VMEM((2, S5_Q, ns, LANES), F32)],
        compiler_params=_params(("arbitrary", "arbitrary")))(pa, dys, states, bre, bim, cre, cim, ar, ai, dvec)


def _s5_discretize(lam_re, lam_im, log_dt, b_re, b_im):
    dt = jnp.exp(log_dt)[:, None]
    mag = jnp.exp(lam_re * dt)
    ar, ai = mag * jnp.cos(lam_im * dt), mag * jnp.sin(lam_im * dt)
    den = lam_re * lam_re + lam_im * lam_im
    qr = ((ar - 1.0) * lam_re + ai * lam_im) / den
    qi = (ai * lam_re - (ar - 1.0) * lam_im) / den
    bbr = qr[..., None] * b_re - qi[..., None] * b_im
    bbi = qr[..., None] * b_im + qi[..., None] * b_re
    return ar, ai, bbr, bbi


def _s5_expand(ar, ai, bbr, bbi, c_re, c_im):
    g, p, h = bbr.shape
    gps = LANES // h
    ns = g // gps
    eye = jnp.eye(gps, dtype=F32)

    def bexp(b):
        return jnp.einsum("sgph,gk->sghkp", b.reshape(ns, gps, p, h), eye).reshape(ns, gps * h, gps * p)

    def cexp(c):
        return jnp.einsum("sghp,gk->sgpkh", c.reshape(ns, gps, h, p), eye).reshape(ns, gps * p, gps * h)

    def aexp(a):
        return a.reshape(ns, S5_Q, LANES).transpose(1, 0, 2)

    return (bexp(bbr).astype(BF16), bexp(bbi).astype(BF16), cexp(c_re).astype(BF16), cexp(c_im).astype(BF16),
            aexp(ar), aexp(ai))


def _s5_contract(dbre, dbim, dcre, dcim, dar, dai, g, p, h):
    gps = LANES // h
    ns = g // gps
    eye = jnp.eye(gps, dtype=F32)
    bcon = lambda d: jnp.einsum("sghkp,gk->sgph", d.reshape(ns, gps, h, gps, p), eye).reshape(g, p, h)
    ccon = lambda d: jnp.einsum("sgpkh,gk->sghp", d.reshape(ns, gps, p, gps, h), eye).reshape(g, h, p)
    acon = lambda d: d.transpose(1, 0, 2).reshape(g, p)
    return bcon(dbre), bcon(dbim), ccon(dcre), ccon(dcim), acon(dar), acon(dai)


def _ml_proj_tile(cpre, xb, wq, wk, wv, gq, gk, gv):
    xc = _silu(cpre)
    q = _dot_nn(xc, wq)
    k = _dot_nn(xc, wk)
    v = _dot_nn(xb, wv)
    return q, k, v, _dot_nn(q, gq) + _dot_nn(k, gk) + _dot_nn(v, gv)


def ml_proj_fwd(cpre, xb, wq, wk, wv, gq, gk, gv, name):
    rows, width = cpre.shape
    nblk = width // LANES
    tr = _tile(rows, 1088, 16)

    def body(c_ref, x_ref, wq_ref, wk_ref, wv_ref, gq_ref, gk_ref, gv_ref, q_ref, k_ref, v_ref, g_ref):
        j = pl.program_id(1)
        q, k, v, g = _ml_proj_tile(c_ref[...], x_ref[...], wq_ref[...], wk_ref[...], wv_ref[...],
                                   gq_ref[...], gk_ref[...], gv_ref[...])
        q_ref[...] = q
        k_ref[...] = k
        v_ref[...] = v

        @pl.when(j == 0)
        def _():
            g_ref[...] = jnp.zeros_like(g_ref)

        g_ref[...] += g

    rb = pl.BlockSpec((tr, LANES), lambda i, j: (i, j))
    wb = pl.BlockSpec((None, LANES, LANES), lambda i, j: (j, 0, 0))
    return pl.pallas_call(
        body, name=name, grid=(rows // tr, nblk), in_specs=[rb, rb, wb, wb, wb, wb, wb, wb],
        out_specs=[rb, rb, rb, pl.BlockSpec((tr, LANES), lambda i, j: (i, 0))],
        out_shape=[jax.ShapeDtypeStruct((rows, width), F32)] * 3 + [jax.ShapeDtypeStruct((rows, LANES), F32)],
        compiler_params=_params(("arbitrary", "arbitrary")))(cpre, xb, wq, wk, wv, gq, gk, gv)


def ml_proj_bwd(cpre, xb, wq, wk, wv, gq, gk, gv, dq, dk, dv, dg, dcp_extra, name):
    rows, width = cpre.shape
    nblk = width // LANES
    tr = _tile(rows, 1088, 16)

    def body(c_ref, x_ref, wq_ref, wk_ref, wv_ref, gq_ref, gk_ref, gv_ref, dq_ref, dk_ref, dv_ref, dg_ref, e_ref,
             dc_ref, dx_ref, *dw_refs):
        i = pl.program_id(1)
        _, vjp = jax.vjp(_ml_proj_tile, c_ref[...], x_ref[...], wq_ref[...], wk_ref[...], wv_ref[...],
                         gq_ref[...], gk_ref[...], gv_ref[...])
        grads = vjp((dq_ref[...], dk_ref[...], dv_ref[...], dg_ref[...]))
        dc_ref[...] = grads[0] + e_ref[...]
        dx_ref[...] = grads[1]

        @pl.when(i == 0)
        def _():
            for r in dw_refs:
                r[...] = jnp.zeros_like(r)

        for r, gval in zip(dw_refs, grads[2:]):
            r[...] += gval

    rb = pl.BlockSpec((tr, LANES), lambda j, i: (i, j))
    wb = pl.BlockSpec((None, LANES, LANES), lambda j, i: (j, 0, 0))
    gb = pl.BlockSpec((tr, LANES), lambda j, i: (i, 0))
    wshape = jax.ShapeDtypeStruct((nblk, LANES, LANES), F32)
    return pl.pallas_call(
        body, name=name, grid=(nblk, rows // tr), in_specs=[rb, rb, wb, wb, wb, wb, wb, wb, rb, rb, rb, gb, rb],
        out_specs=[rb, rb] + [wb] * 6,
        out_shape=[jax.ShapeDtypeStruct((rows, width), F32)] * 2 + [wshape] * 6,
        compiler_params=_params(("arbitrary", "arbitrary")))(cpre, xb, wq, wk, wv, gq, gk, gv, dq, dk, dv, dg, dcp_extra)


def _ml_gates_tile(gl, bg, nh):
    x = gl + bg
    bcum = _dot(_tri(CHUNK), _log_sigmoid(x), ((1,), (0,)), precision=HI)
    lane = lax.broadcasted_iota(jnp.int32, x.shape, 1)
    return jnp.where(lane < nh, x, jnp.where(lane < 2 * nh, bcum, 0.0))


def _ml_core_tile(q, k, v, colg, rowg, cpre, zb, nw, sk, cst, nst, m_prev):
    c, dh = q.shape
    igc, bc = _lane_pick(colg, 0), _lane_pick(colg, 1)
    igr, br = _row_pick(rowg, 0), _row_pick(rowg, 1)
    causal = _tri(c) > 0
    dmat = jnp.where(causal, bc - br + igr, -jnp.inf)
    inter = bc + m_prev
    mt = lax.stop_gradient(jnp.maximum(inter, jnp.max(dmat, axis=1, keepdims=True)))
    wt = jnp.exp(dmat - mt)
    w_prev = jnp.exp(inter - mt)
    qs = q * (dh ** -0.5)
    s = _dot_nt(qs, k) * wt
    num = _dot_nn(s, v) + w_prev * _dot_nn(qs, cst)
    den = jnp.sum(s, axis=1, keepdims=True) + w_prev * jnp.sum(qs * nst, axis=1, keepdims=True)
    h = num * (1.0 / jnp.maximum(jnp.abs(den), jnp.exp(-mt)))
    last = (lax.broadcasted_iota(jnp.int32, (c, 1), 0) == c - 1).astype(F32)
    blast = jnp.sum(bc * last, axis=0, keepdims=True)
    g = blast - bc + igc
    m_new = lax.stop_gradient(jnp.maximum(blast + m_prev, jnp.max(g, axis=0, keepdims=True)))
    decay = jnp.exp(blast + m_prev - m_new)
    wk = jnp.exp(g - m_new) * k
    c_new = decay * cst + _dot_tn(wk, v)
    n_new = decay * nst + jnp.sum(wk, axis=0, keepdims=True)
    mu = jnp.mean(h, axis=1, keepdims=True)
    hc = h - mu
    var = jnp.mean(hc * hc, axis=1, keepdims=True)
    out = hc * lax.rsqrt(var + HEAD_NORM_EPS) * nw + sk * _silu(cpre)
    return out * _silu(zb), c_new, n_new, m_new


def _ml_core_specs(nc, dh, rev):
    ch = (lambda c: nc - 1 - c) if rev else (lambda c: c)
    rb = pl.BlockSpec((CHUNK, dh), lambda b_, c, h: (b_ * nc + ch(c), h))
    colb = pl.BlockSpec((None, CHUNK, 2), lambda b_, c, h: (h, b_ * nc + ch(c), 0))
    rowb = pl.BlockSpec((None, None, 2, CHUNK), lambda b_, c, h: (b_ * nc + ch(c), h, 0, 0))
    pb = pl.BlockSpec((1, dh), lambda b_, c, h: (0, h))
    cb = pl.BlockSpec((None, None, dh, dh), lambda b_, c, h: (b_ * nc + ch(c), h, 0, 0))
    nb_ = pl.BlockSpec((None, None, 1, dh), lambda b_, c, h: (b_ * nc + ch(c), h, 0, 0))
    mb = pl.BlockSpec((None, None, 1, 1), lambda b_, c, h: (b_ * nc + ch(c), h, 0, 0))
    return rb, colb, rowb, pb, cb, nb_, mb


def ml_core_fwd(q, k, v, colg, rowg, cpre, zb, nw, sk, nb, nh, name):
    rows, width = q.shape
    dh = width // nh
    nc = rows // nb // CHUNK
    rb, colb, rowb, pb, cb, nb_, mb = _ml_core_specs(nc, dh, False)

    def body(q_ref, k_ref, v_ref, col_ref, row_ref, c_ref, z_ref, nw_ref, sk_ref, y_ref, cs_ref, ns_ref, ms_ref,
             cst_ref, nst_ref, mst_ref):
        c = pl.program_id(1)
        h = pl.program_id(2)

        @pl.when(c == 0)
        def _():
            cst_ref[h] = jnp.zeros((dh, dh), F32)
            nst_ref[h] = jnp.zeros((1, dh), F32)
            mst_ref[h] = jnp.zeros((1, 1), F32)

        cst, nst, m_prev = cst_ref[h], nst_ref[h], mst_ref[h]
        cs_ref[...] = cst
        ns_ref[...] = nst
        ms_ref[...] = m_prev
        y, c_new, n_new, m_new = _ml_core_tile(q_ref[...], k_ref[...], v_ref[...], col_ref[...], row_ref[...],
                                               c_ref[...], z_ref[...], nw_ref[...], sk_ref[...], cst, nst, m_prev)
        y_ref[...] = y.astype(BF16)
        cst_ref[h] = c_new
        nst_ref[h] = n_new
        mst_ref[h] = m_new

    nbc = nb * nc
    return pl.pallas_call(
        body, name=name, grid=(nb, nc, nh), in_specs=[rb, rb, rb, colb, rowb, rb, rb, pb, pb],
        out_specs=[rb, cb, nb_, mb],
        out_shape=[jax.ShapeDtypeStruct((rows, width), BF16), jax.ShapeDtypeStruct((nbc, nh, dh, dh), F32),
                   jax.ShapeDtypeStruct((nbc, nh, 1, dh), F32), jax.ShapeDtypeStruct((nbc, nh, 1, 1), F32)],
        scratch_shapes=[pltpu.VMEM((nh, dh, dh), F32), pltpu.VMEM((nh, 1, dh), F32), pltpu.VMEM((nh, 1, 1), F32)],
        compiler_params=_params(("arbitrary", "arbitrary", "arbitrary")))(q, k, v, colg, rowg, cpre, zb, nw, sk)


def ml_core_bwd(q, k, v, colg, rowg, cpre, zb, nw, sk, cs, ns, ms, dy, nb, nh, name):
    rows, width = q.shape
    dh = width // nh
    nc = rows // nb // CHUNK
    rb, colb, rowb, pb, cb, nb_, mb = _ml_core_specs(nc, dh, True)

    def body(q_ref, k_ref, v_ref, col_ref, row_ref, c_ref, z_ref, nw_ref, sk_ref, cs_ref, ns_ref, ms_ref, dy_ref,
             dq_ref, dk_ref, dv_ref, dc_ref, dz_ref, dcol_ref, drow_ref, dnw_ref, dsk_ref, dcst_ref, dnst_ref):
        bb = pl.program_id(0)
        step = pl.program_id(1)
        h = pl.program_id(2)

        @pl.when(jnp.logical_and(bb == 0, jnp.logical_and(step == 0, h == 0)))
        def _():
            dnw_ref[...] = jnp.zeros_like(dnw_ref)
            dsk_ref[...] = jnp.zeros_like(dsk_ref)

        @pl.when(step == 0)
        def _():
            dcst_ref[h] = jnp.zeros((dh, dh), F32)
            dnst_ref[h] = jnp.zeros((1, dh), F32)

        m_prev = ms_ref[...]

        def f(*a):
            return _ml_core_tile(*a, m_prev)[:3]

        _, vjp = jax.vjp(f, q_ref[...], k_ref[...], v_ref[...], col_ref[...], row_ref[...], c_ref[...], z_ref[...],
                         nw_ref[...], sk_ref[...], cs_ref[...], ns_ref[...])
        g = vjp((dy_ref[...], dcst_ref[h], dnst_ref[h]))
        dq_ref[...] = g[0]
        dk_ref[...] = g[1]
        dv_ref[...] = g[2]
        dcol_ref[...] = g[3]
        drow_ref[...] = g[4]
        dc_ref[...] = g[5]
        dz_ref[...] = g[6].astype(dz_ref.dtype)
        dnw_ref[h] += g[7]
        dsk_ref[h] += g[8]
        dcst_ref[h] = g[9]
        dnst_ref[h] = g[10]

    nbc = nb * nc
    accb = pl.BlockSpec((nh, 1, dh), lambda b_, c, h: (0, 0, 0))
    return pl.pallas_call(
        body, name=name, grid=(nb, nc, nh), in_specs=[rb, rb, rb, colb, rowb, rb, rb, pb, pb, cb, nb_, mb, rb],
        out_specs=[rb, rb, rb, rb, rb, colb, rowb, accb, accb],
        out_shape=[jax.ShapeDtypeStruct((rows, width), F32)] * 4 + [jax.ShapeDtypeStruct((rows, width), BF16)]
        + [jax.ShapeDtypeStruct(colg.shape, F32), jax.ShapeDtypeStruct(rowg.shape, F32),
           jax.ShapeDtypeStruct((nh, 1, dh), F32), jax.ShapeDtypeStruct((nh, 1, dh), F32)],
        scratch_shapes=[pltpu.VMEM((nh, dh, dh), F32), pltpu.VMEM((nh, 1, dh), F32)],
        compiler_params=_params(("arbitrary", "arbitrary", "arbitrary")))(
            q, k, v, colg, rowg, cpre, zb, nw, sk, cs, ns, ms, dy)


def _ssd_dt_tile(dtr, bias, alog):
    dt = _softplus(dtr + bias)
    cum = _dot(_tri(CHUNK), dt * (-jnp.exp(alog)), ((1,), (0,)), precision=HI)
    return dt, cum


def _ssd_tile(xcs, bmc, cmc, cols, rows_, z, dvec, gn, states, hpg):
    npair = hpg // 2
    hd = SSD_HEAD_DIM
    xs = [_silu(x) for x in xcs]
    bm, cm = _silu(bmc), _silu(cmc)
    cb = _dot_nt(cm, bm)
    causal = _tri(CHUNK) > 0
    lane_lo = lax.broadcasted_iota(jnp.int32, (1, 2 * hd), 1) < hd
    lastsel = (lax.broadcasted_iota(jnp.int32, (CHUNK, 1), 0) == CHUNK - 1).astype(F32)
    heads = []
    for r in range(hpg):
        dtc, cumc = _lane_pick(cols, r), _lane_pick(cols, hpg + r)
        dtrow, cumr = _row_pick(rows_, r), _row_pick(rows_, hpg + r)
        w = cb * jnp.exp(jnp.where(causal, cumc - cumr, -jnp.inf)) * dtrow
        last = jnp.sum(cumc * lastsel, axis=0, keepdims=True)
        heads.append((w, jnp.exp(cumc), jnp.exp(last - cumc) * dtc, jnp.exp(last)))
    ys, new_states = [], []
    for j in range(npair):
        (wa, ea, da, la), (wb, eb, db, lb) = heads[2 * j], heads[2 * j + 1]
        yi = jnp.where(lane_lo, _dot_nn(wa, xs[j]), _dot_nn(wb, xs[j]))
        ys.append(yi + jnp.where(lane_lo, ea, eb) * _dot_nn(cm, states[j]))
        xd = xs[j] * jnp.where(lane_lo, da, db)
        new_states.append(jnp.where(lane_lo, la, lb) * states[j] + _dot_tn(bm, xd))
    y = jnp.concatenate(ys, axis=1) + dvec * jnp.concatenate(xs, axis=1)
    yg = y * _silu(z)
    yn = yg * lax.rsqrt(jnp.mean(yg * yg, axis=1, keepdims=True) + NORM_EPS) * gn
    return yn, new_states


def _ssd_specs(nc, hpg, ng, rev):
    npair = hpg // 2
    gw = hpg * SSD_HEAD_DIM
    xblocks = ng * npair
    ch = (lambda c: nc - 1 - c) if rev else (lambda c: c)
    xs = [pl.BlockSpec((CHUNK, LANES), functools.partial(lambda b_, c, g, jj: (b_ * nc + ch(c), g * npair + jj), jj=j))
          for j in range(npair)]
    bmb = pl.BlockSpec((CHUNK, SSD_STATE), lambda b_, c, g: (b_ * nc + ch(c), xblocks + g))
    cmb = pl.BlockSpec((CHUNK, SSD_STATE), lambda b_, c, g: (b_ * nc + ch(c), xblocks + ng + g))
    colb = pl.BlockSpec((None, CHUNK, 2 * hpg), lambda b_, c, g: (g, b_ * nc + ch(c), 0))
    rowb = pl.BlockSpec((None, None, 2 * hpg, CHUNK), lambda b_, c, g: (b_ * nc + ch(c), g, 0, 0))
    zb = pl.BlockSpec((CHUNK, gw), lambda b_, c, g: (b_ * nc + ch(c), g))
    pb = pl.BlockSpec((1, gw), lambda b_, c, g: (0, g))
    sb = pl.BlockSpec((None, None, npair, SSD_STATE, 2 * SSD_HEAD_DIM), lambda b_, c, g: (b_ * nc + ch(c), g, 0, 0, 0))
    return xs, bmb, cmb, colb, rowb, zb, pb, sb


def ssd_core_fwd(cpre, cols, rows_, z, dvec, gn, nb, hpg, name):
    rows = cpre.shape[0]
    inner = z.shape[1]
    ng = inner // (hpg * SSD_HEAD_DIM)
    npair = hpg // 2
    nc = rows // nb // CHUNK
    xs, bmb, cmb, colb, rowb, zb, pb, sb = _ssd_specs(nc, hpg, ng, False)

    def body(*refs):
        x_refs = refs[:npair]
        bm_ref, cm_ref, col_ref, row_ref, z_ref, d_ref, gn_ref, y_ref, so_ref, st_ref = refs[npair:]
        c = pl.program_id(1)
        g = pl.program_id(2)

        @pl.when(c == 0)
        def _():
            st_ref[g] = jnp.zeros((npair, SSD_STATE, 2 * SSD_HEAD_DIM), F32)

        so_ref[...] = st_ref[g]
        states = [st_ref[g, j] for j in range(npair)]
        yn, new_states = _ssd_tile([r[...] for r in x_refs], bm_ref[...], cm_ref[...], col_ref[...], row_ref[...],
                                   z_ref[...], d_ref[...], gn_ref[...], states, hpg)
        y_ref[...] = yn.astype(BF16)
        for j in range(npair):
            st_ref[g, j] = new_states[j]

    return pl.pallas_call(
        body, name=name, grid=(nb, nc, ng), in_specs=xs + [bmb, cmb, colb, rowb, zb, pb, pb],
        out_specs=[zb, sb],
        out_shape=[jax.ShapeDtypeStruct((rows, inner), BF16),
                   jax.ShapeDtypeStruct((nb * nc, ng, npair, SSD_STATE, 2 * SSD_HEAD_DIM), F32)],
        scratch_shapes=[pltpu.VMEM((ng, npair, SSD_STATE, 2 * SSD_HEAD_DIM), F32)],
        compiler_params=_params(("arbitrary", "arbitrary", "arbitrary")))(
            *([cpre] * npair), cpre, cpre, cols, rows_, z, dvec, gn)


def ssd_core_bwd(cpre, cols, rows_, z, dvec, gn, states, dyn, nb, hpg, name):
    rows = cpre.shape[0]
    inner = z.shape[1]
    gw = hpg * SSD_HEAD_DIM
    ng = inner // gw
    npair = hpg // 2
    nc = rows // nb // CHUNK
    xs, bmb, cmb, colb, rowb, zb, pb, sb = _ssd_specs(nc, hpg, ng, True)

    def body(*refs):
        x_refs = refs[:npair]
        (bm_ref, cm_ref, col_ref, row_ref, z_ref, d_ref, gn_ref, s_ref, dy_ref,
         dx_ref, dbm_ref, dcm_ref, dcol_ref, drow_ref, dz_ref, dd_ref, dgn_ref, dst_ref) = refs[npair:]
        bb = pl.program_id(0)
        step = pl.program_id(1)
        g = pl.program_id(2)

        @pl.when(jnp.logical_and(bb == 0, jnp.logical_and(step == 0, g == 0)))
        def _():
            dd_ref[...] = jnp.zeros_like(dd_ref)
            dgn_ref[...] = jnp.zeros_like(dgn_ref)

        @pl.when(step == 0)
        def _():
            dst_ref[g] = jnp.zeros((npair, SSD_STATE, 2 * SSD_HEAD_DIM), F32)

        def f(xcs, bmc, cmc, cv, rv, zv, dv_, gv, sts):
            return _ssd_tile(xcs, bmc, cmc, cv, rv, zv, dv_, gv, sts, hpg)

        _, vjp = jax.vjp(f, [r[...] for r in x_refs], bm_ref[...], cm_ref[...], col_ref[...], row_ref[...], z_ref[...],
                         d_ref[...], gn_ref[...], [s_ref[j] for j in range(npair)])
        gr = vjp((dy_ref[...], [dst_ref[g, j] for j in range(npair)]))
        dx_ref[...] = jnp.concatenate(gr[0], axis=1)
        dbm_ref[...] = gr[1]
        dcm_ref[...] = gr[2]
        dcol_ref[...] = gr[3]
        drow_ref[...] = gr[4]
        dz_ref[...] = gr[5].astype(dz_ref.dtype)
        dd_ref[g] += gr[6]
        dgn_ref[g] += gr[7]
        for j in range(npair):
            dst_ref[g, j] = gr[8][j]

    ch = lambda c: nc - 1 - c
    nblk = pl.BlockSpec((CHUNK, SSD_STATE), lambda b_, c, g: (b_ * nc + ch(c), g))
    accb = pl.BlockSpec((ng, 1, gw), lambda b_, c, g: (0, 0, 0))
    return pl.pallas_call(
        body, name=name, grid=(nb, nc, ng), in_specs=xs + [bmb, cmb, colb, rowb, zb, pb, pb, sb, zb],
        out_specs=[zb, nblk, nblk, colb, rowb, zb, accb, accb],
        out_shape=[jax.ShapeDtypeStruct((rows, inner), F32), jax.ShapeDtypeStruct((rows, ng * SSD_STATE), F32),
                   jax.ShapeDtypeStruct((rows, ng * SSD_STATE), F32), jax.ShapeDtypeStruct(cols.shape, F32),
                   jax.ShapeDtypeStruct(rows_.shape, F32), jax.ShapeDtypeStruct((rows, inner), BF16),
                   jax.ShapeDtypeStruct((ng, 1, gw), F32), jax.ShapeDtypeStruct((ng, 1, gw), F32)],
        scratch_shapes=[pltpu.VMEM((ng, npair, SSD_STATE, 2 * SSD_HEAD_DIM), F32)],
        compiler_params=_params(("arbitrary", "arbitrary", "arbitrary")))(
            *([cpre] * npair), cpre, cpre, cols, rows_, z, dvec, gn, states, dyn)


def _hw_expand(w):
    n, bi, _ = w.shape
    per = LANES // bi
    eye = jnp.eye(per, dtype=F32)
    return jnp.einsum("jbio,bc->jbico", w.reshape(n // per, per, bi, bi), eye).reshape(n // per, LANES, LANES)


def _hw_contract(d, bi=QKV_BLOCK):
    per = LANES // bi
    eye = jnp.eye(per, dtype=F32)
    return jnp.einsum("jbico,bc->jbio", d.reshape(d.shape[0], per, bi, per, bi), eye).reshape(-1, bi, bi)


def _wg_expand(wg, width):
    pad = jnp.pad(wg, ((0, 0), (0, LANES - wg.shape[1])))
    return [pad[i * width:(i + 1) * width].reshape(width // LANES, LANES, LANES) for i in range(3)]


def _wg_contract(dgs, ngate):
    return jnp.concatenate([d[:, :, :ngate].reshape(-1, ngate) for d in dgs], axis=0)


def _pad_lanes(a):
    return jnp.pad(a, ((0, 0), (0, LANES - a.shape[1])))


def _pairs_to_layouts(first, second, ngrp, per, nbc):
    rows = first.shape[0]
    both = jnp.concatenate([first.reshape(rows, ngrp, per), second.reshape(rows, ngrp, per)], axis=2)
    return both.transpose(1, 0, 2), both.reshape(nbc, CHUNK, ngrp, 2 * per).transpose(0, 2, 3, 1)


def _layouts_to_pairs(dcols, drows, ngrp, per):
    rows = dcols.shape[1]
    both = dcols.transpose(1, 0, 2) + drows.transpose(0, 3, 1, 2).reshape(rows, ngrp, 2 * per)
    return both[:, :, :per].reshape(rows, ngrp * per), both[:, :, per:].reshape(rows, ngrp * per)


_LATE = ("Wo0a", "Wo0b", "W1z", "W1x", "W1dt", "Wo1")


def _local_step(x, target, bw, sp, late_weights=None, late_grads=None, start_token=None):
    nb, seq, d = x.shape
    nh, hpg = MLSTM_HEADS, SSD_HPG
    t_len = N_META + seq
    nc = -(-t_len // CHUNK)
    tp = nc * CHUNK
    rows = nb * tp
    nbc = nb * nc
    meta = sp["meta_tokens"]
    h0 = jnp.concatenate([jnp.broadcast_to(meta[None], (nb, N_META, d)), x, jnp.zeros((nb, tp - t_len, d), F32)], axis=1)
    h0 = h0.reshape(rows, d)
    if start_token is not None:
        h0 = h0 + start_token
    tgt = jnp.pad(target, ((0, 0), (N_META, tp - t_len), (0, 0))).reshape(rows, d)

    n0 = norm_fwd(h0, sp["ab_norm"], "norm0")
    pa = mm(n0, bw["W0a"], "nn", "mm_pa")
    xb = mm(n0, bw["W0xb"], "nn", "mm_xb")
    zb = mm(n0, bw["W0zb"], "nn", "mm_zb")
    s5w = pa.shape[1] // 2
    mlw = xb.shape[1]
    s5_args = (sp["s5_lambda_re"], sp["s5_lambda_im"], sp["s5_log_dt"].reshape(-1), sp["s5_b_re"], sp["s5_b_im"])
    (ar, ai, bbr, bbi), s5_disc_vjp = jax.vjp(_s5_discretize, *s5_args)
    sg, spn, shh = bbr.shape
    bre, bim, cre, cim, are, aie = _s5_expand(ar, ai, bbr, bbi, sp["s5_c_re"], sp["s5_c_im"])
    ys5, gb, s5st = s5_fwd(pa, bre, bim, cre, cim, are, aie, sp["s5_d"], nb, "s5_fwd")
    tglu = mm(gb, bw["glu"], "nn", "mm_glu")

    def glu_tile(ys, tt, za, gbias):
        return _gelu(ys) * _sigmoid(tt + gbias) * _silu(za)

    ya = rowwise("glu_fwd", lambda i, ys, tt, pab, gbias: glu_tile(ys, tt, pab[:, s5w:], gbias),
                 [ys5, tglu, pa], [sp["s5_glu_b"]], [(s5w, BF16)], tr=_tile(rows, 256, 16))[0]

    cpre0 = conv_fwd(xb, sp["ml_conv_w"], sp["ml_conv_b"], nb, "ml_conv_fwd")
    wq_e, wk_e, wv_e = _hw_expand(sp["ml_wq"]), _hw_expand(sp["ml_wk"]), _hw_expand(sp["ml_wv"])
    gq, gk, gv = _wg_expand(sp["ml_w_gate"], mlw)
    q, k, v, gl = ml_proj_fwd(cpre0, xb, wq_e, wk_e, wv_e, gq, gk, gv, "ml_proj_fwd")
    bgate = _pad_lanes(sp["ml_b_gate"])
    gout = rowwise("ml_gates_fwd", lambda i, g_, b_: _ml_gates_tile(g_, b_, nh), [gl], [bgate], [(LANES, F32)], tr=CHUNK)[0]
    colg, rowg = _pairs_to_layouts(gout[:, :nh], gout[:, nh:2 * nh], nh, 1, nbc)
    yb, ml_cs, ml_ns, ml_ms = ml_core_fwd(q, k, v, colg, rowg, cpre0, zb, sp["ml_norm"], sp["ml_skip"], nb, nh, "ml_core_fwd")
    if late_weights is not None:
        bw = {**bw, **late_weights((ya, yb))}
    h1 = mm(ya, bw["Wo0a"], "nn", "mm_out0a", resid=h0)
    h1 = mm(yb, bw["Wo0b"], "nn", "mm_out0b", resid=h1)

    n1 = norm_fwd(h1, sp["ssd_norm"], "norm1")
    z1 = mm(n1, bw["W1z"], "nn", "mm_z1")
    xbc = mm(n1, bw["W1x"], "nn", "mm_xbc")
    dtr = mm(n1, bw["W1dt"], "nn", "mm_dt")
    inner = z1.shape[1]
    ng = inner // (hpg * SSD_HEAD_DIM)
    nhd = ng * hpg
    cpre1 = conv_fwd(xbc, sp["ssd_conv_w"], sp["ssd_conv_b"], nb, "ssd_conv_fwd")
    dt_bias, a_log = _pad_lanes(sp["ssd_dt_bias"]), _pad_lanes(sp["ssd_a_log"])
    dt, cum = rowwise("ssd_dt_fwd", lambda i, r_, b_, a_: _ssd_dt_tile(r_, b_, a_), [dtr], [dt_bias, a_log],
                      [(LANES, F32), (LANES, F32)], tr=CHUNK)
    cols, rws = _pairs_to_layouts(dt[:, :nhd], cum[:, :nhd], ng, hpg, nbc)
    dvec = jnp.repeat(sp["ssd_d"], SSD_HEAD_DIM, axis=1)
    yn, ssd_st = ssd_core_fwd(cpre1, cols, rws, z1, dvec, sp["ssd_gnorm"], nb, hpg, "ssd_core_fwd")
    h2 = mm(yn, bw["Wo1"], "nn", "mm_out1", resid=h1)

    tr_l = _tile(tp, 256, 16)
    per_ex = tp // tr_l

    def loss_tile(i, hb, tb, gfn):
        tpos = (i % per_ex) * tr_l + lax.broadcasted_iota(jnp.int32, (tr_l, 1), 0)
        mask = jnp.logical_and(tpos >= N_META, tpos < t_len).astype(F32)

        def lf(hh, gg):
            e = (_rms(hh, gg) - tb) * mask
            return 0.5 * jnp.sum(e * e) / d

        lval, (dh, dg) = jax.value_and_grad(lf, (0, 1))(hb, gfn)
        return dh, dh, jnp.full((1, LANES), lval, F32), dg

    fn = sp["final_norm"].reshape(1, d)
    dh2, dh2b, loss_acc, dfn = rowwise("loss", loss_tile, [h2, tgt], [fn], [(d, F32), (d, BF16)], [(1, LANES), (1, d)], tr=tr_l)

    gbig, gs = {}, {}
    gs["final_norm"] = dfn.reshape(sp["final_norm"].shape)
    dyn = mm(dh2b, bw["Wo1"], "nt", "mm_dyn")
    gbig["Wo1"] = mm(yn, dh2b, "tn", "mm_dWo1", out_dtype=BF16)
    dxs, dbm, dcm, dcols, drws, dz1, ddvec, dgn = ssd_core_bwd(cpre1, cols, rws, z1, dvec, sp["ssd_gnorm"], ssd_st, dyn,
                                                              nb, hpg, "ssd_core_bwd")
    gs["ssd_d"] = ddvec.reshape(1, nhd, SSD_HEAD_DIM).sum(axis=2)
    gs["ssd_gnorm"] = dgn.reshape(1, inner)
    ddt, dcum = _layouts_to_pairs(dcols, drws, ng, hpg)

    def ssd_dt_bwd_tile(i, r_, ddt_, dcum_, b_, a_):
        _, vjp = jax.vjp(_ssd_dt_tile, r_, b_, a_)
        return vjp((ddt_, dcum_))

    ddtr, dbias, dalog = rowwise("ssd_dt_bwd", ssd_dt_bwd_tile, [dtr, _pad_lanes(ddt), _pad_lanes(dcum)], [dt_bias, a_log],
                                 [(LANES, BF16)], [(1, LANES), (1, LANES)], tr=CHUNK)
    gs["ssd_dt_bias"] = dbias[:, :nhd]
    gs["ssd_a_log"] = dalog[:, :nhd]
    dcpre1 = jnp.concatenate([dxs, dbm, dcm], axis=1)
    dxbc, dcw1, dcb1 = conv_bwd(dcpre1, xbc, sp["ssd_conv_w"], nb, "ssd_conv_bwd")
    gs["ssd_conv_w"] = dcw1[:sp["ssd_conv_w"].shape[0]]
    gs["ssd_conv_b"] = dcb1
    dn1 = mm(dz1, bw["W1z"], "nt", "mm_dn1z")
    dn1 = mm(dxbc, bw["W1x"], "nt", "mm_dn1x", resid=dn1)
    dn1 = mm(ddtr, bw["W1dt"], "nt", "mm_dn1dt", resid=dn1)
    gbig["W1z"] = mm(n1, dz1, "tn", "mm_dW1z", out_dtype=BF16)
    gbig["W1x"] = mm(n1, dxbc, "tn", "mm_dW1x", out_dtype=BF16)
    gbig["W1dt"] = mm(n1, ddtr, "tn", "mm_dW1dt", out_dtype=BF16)
    dh1, dh1b, dg1 = norm_bwd(h1, sp["ssd_norm"], dn1, dh2, "norm1_bwd")
    gs["ssd_norm"] = dg1

    gbig["Wo0a"] = mm(ya, dh1b, "tn", "mm_dWo0a", out_dtype=BF16)
    gbig["Wo0b"] = mm(yb, dh1b, "tn", "mm_dWo0b", out_dtype=BF16)
    wo0a, wo0b = bw["Wo0a"], bw["Wo0b"]
    if late_grads is not None:
        follow = late_grads({n: gbig[n] for n in _LATE}).astype(wo0a.dtype)
        wo0a, wo0b = wo0a + follow, wo0b + follow
    dya = mm(dh1b, wo0a, "nt", "mm_dya")
    dyb = mm(dh1b, wo0b, "nt", "mm_dyb")
    (dq, dk, dv, dcp_skip, dzb, dcolg, drowg, dnw, dsk) = ml_core_bwd(
        q, k, v, colg, rowg, cpre0, zb, sp["ml_norm"], sp["ml_skip"], ml_cs, ml_ns, ml_ms, dyb, nb, nh, "ml_core_bwd")
    gs["ml_norm"] = dnw.reshape(1, mlw)
    gs["ml_skip"] = dsk.reshape(1, mlw)
    dig, dbcum = _layouts_to_pairs(dcolg, drowg, nh, 1)
    dgout = _pad_lanes(jnp.concatenate([dig, dbcum], axis=1))

    def ml_gates_bwd_tile(i, g_, dgo, b_):
        _, vjp = jax.vjp(lambda a, b: _ml_gates_tile(a, b, nh), g_, b_)
        return vjp(dgo)

    dgl, dbg = rowwise("ml_gates_bwd", ml_gates_bwd_tile, [gl, dgout], [bgate], [(LANES, F32)], [(1, LANES)], tr=CHUNK)
    gs["ml_b_gate"] = dbg[:, :2 * nh]
    dcpre0, dxb_v, dwq, dwk, dwv, dgq, dgk, dgv = ml_proj_bwd(cpre0, xb, wq_e, wk_e, wv_e, gq, gk, gv, dq, dk, dv, dgl,
                                                            dcp_skip, "ml_proj_bwd")
    gs["ml_wq"], gs["ml_wk"], gs["ml_wv"] = _hw_contract(dwq), _hw_contract(dwk), _hw_contract(dwv)
    gs["ml_w_gate"] = _wg_contract([dgq, dgk, dgv], 2 * nh)
    dxb, dcw0, dcb0 = conv_bwd(dcpre0, xb, sp["ml_conv_w"], nb, "ml_conv_bwd", resid=dxb_v)
    gs["ml_conv_w"] = dcw0[:sp["ml_conv_w"].shape[0]]
    gs["ml_conv_b"] = dcb0

    def glu_bwd_tile(i, ys, tt, pab, dy_, gbias):
        _, vjp = jax.vjp(glu_tile, ys, tt, pab[:, s5w:], gbias)
        return vjp(dy_)

    dys_direct, dtglu, dza, dglub = rowwise("glu_bwd", glu_bwd_tile, [ys5, tglu, pa, dya], [sp["s5_glu_b"]],
                                            [(s5w, F32), (s5w, BF16), (s5w, BF16)], [(1, s5w)], tr=_tile(rows, 256, 16))
    gs["s5_glu_b"] = dglub
    dgb = mm(dtglu, bw["glu"], "nt", "mm_dgb")
    gbig["glu"] = mm(gb, dtglu, "tn", "mm_dglu", out_dtype=BF16)

    def gelu_bwd_tile(i, ys, dg_, direct):
        _, vjp = jax.vjp(_gelu, ys)
        return vjp(dg_)[0] + direct

    dys5 = rowwise("gelu_bwd", gelu_bwd_tile, [ys5, dgb, dys_direct], [], [(s5w, F32)], tr=_tile(rows, 256, 16))[0]
    du, dbre, dbim, dcre, dcim, dare, daie, dd5 = s5_bwd(pa, dys5, s5st, bre, bim, cre, cim, are, aie, sp["s5_d"], nb, "s5_bwd")
    gs["s5_d"] = dd5
    dbbr, dbbi, dcr, dci, dar, dai = _s5_contract(dbre, dbim, dcre, dcim, dare, daie, sg, spn, shh)
    gs["s5_c_re"], gs["s5_c_im"] = dcr, dci
    (gs["s5_lambda_re"], gs["s5_lambda_im"], dlogdt, gs["s5_b_re"], gs["s5_b_im"]) = s5_disc_vjp((dar, dai, dbbr, dbbi))
    gs["s5_log_dt"] = dlogdt.reshape(1, -1)
    dpa = jnp.concatenate([du, dza], axis=1)
    dn0 = mm(dpa, bw["W0a"], "nt", "mm_dn0a")
    dn0 = mm(dxb, bw["W0xb"], "nt", "mm_dn0xb", resid=dn0)
    dn0 = mm(dzb, bw["W0zb"], "nt", "mm_dn0zb", resid=dn0)
    gbig["W0a"] = mm(n0, dpa, "tn", "mm_dW0a", out_dtype=BF16)
    gbig["W0xb"] = mm(n0, dxb, "tn", "mm_dW0xb", out_dtype=BF16)
    gbig["W0zb"] = mm(n0, dzb, "tn", "mm_dW0zb", out_dtype=BF16)
    dh0, _, dg0 = norm_bwd(h0, sp["ab_norm"], dn0, dh1, "norm0_bwd")
    gs["ab_norm"] = dg0
    dh0 = dh0.reshape(nb, tp, d)
    gs["meta_tokens"] = jnp.sum(dh0[:, :N_META], axis=0)
    return loss_acc[0, 0], dh0, gbig, gs


N_DEV = 8
N_CHIP = 4
MESH = pl.DeviceIdType.MESH
_HBM = pl.BlockSpec(memory_space=pltpu.HBM)


def _place():
    x, y, c = lax.axis_index("x"), lax.axis_index("y"), lax.axis_index("c")
    return x, y, c, [(1 - x, y), (x, 1 - y), (1 - x, 1 - y)]


def all_gather8(v, name):
    m_per, n = v.shape

    def body(x_ref, out_ref, send_sems, recv_sems, local_sem):
        x, y, c, chips = _place()
        me, sibling = (x, y, c), (x, y, 1 - c)

        def rows(px, py, pc):
            return out_ref.at[pl.ds((4 * px + 2 * py + pc) * m_per, m_per), :]

        def copy(kk, block, to, src=None):
            return pltpu.make_async_remote_copy(
                src_ref=rows(*block) if src is None else src, dst_ref=rows(*block), send_sem=send_sems.at[kk],
                recv_sem=recv_sems.at[kk], device_id=to, device_id_type=MESH)

        mine = pltpu.make_async_copy(x_ref, rows(*me), local_sem)
        mine.start()
        first = [copy(0, me, sibling, src=x_ref)]
        first += [copy(1 + j, me, (*chip, c), src=x_ref) for j, chip in enumerate(chips)]
        for cp in first:
            cp.start()
        passed = [copy(4 + j, (*chip, c), sibling) for j, chip in enumerate(chips)]
        for j, chip in enumerate(chips):
            copy(1 + j, (*chip, c), me).wait_recv()
            passed[j].start()
        copy(0, sibling, me).wait_recv()
        for j, chip in enumerate(chips):
            copy(4 + j, (*chip, 1 - c), me).wait_recv()
        for cp in first + passed:
            cp.wait_send()
        mine.wait()

    return pl.pallas_call(
        body, name=name, out_shape=jax.ShapeDtypeStruct((N_DEV * m_per, n), v.dtype),
        in_specs=[pl.BlockSpec(memory_space=pltpu.VMEM)], out_specs=pl.BlockSpec(memory_space=pltpu.VMEM),
        scratch_shapes=[pltpu.SemaphoreType.DMA((7,)), pltpu.SemaphoreType.DMA((7,)), pltpu.SemaphoreType.DMA],
        compiler_params=pltpu.CompilerParams(vmem_limit_bytes=VMEM_LIMIT))(v)


def gather_chips(vs, name):
    na = len(vs)

    def body(*refs):
        x_refs, out_refs = refs[:na], refs[na:2 * na]
        send_sems, recv_sems, local_sems = refs[2 * na:]
        x, y, c, chips = _place()
        k = 2 * x + y
        sibling = (x, y, 1 - c)

        def copy(i, kk, src, chip_k, half, to):
            return pltpu.make_async_remote_copy(
                src_ref=src, dst_ref=out_refs[i].at[chip_k, half], send_sem=send_sems.at[6 * i + kk],
                recv_sem=recv_sems.at[6 * i + kk], device_id=to, device_id_type=MESH)

        mine = [pltpu.make_async_copy(x_refs[i], out_refs[i].at[k], local_sems.at[i]) for i in range(na)]
        for cp in mine:
            cp.start()
        first = [copy(i, j, x_refs[i].at[c], k, c, (*chip, c)) for j, chip in enumerate(chips) for i in range(na)]
        for cp in first:
            cp.start()
        passed = []
        for j, (cx, cy) in enumerate(chips):
            kj = 2 * cx + cy
            for i in range(na):
                copy(i, j, out_refs[i].at[kj, c], kj, c, (cx, cy, c)).wait_recv()
                fwd = copy(i, 3 + j, out_refs[i].at[kj, c], kj, c, sibling)
                fwd.start()
                passed.append(fwd)
        for j, (cx, cy) in enumerate(chips):
            kj = 2 * cx + cy
            for i in range(na):
                copy(i, 3 + j, out_refs[i].at[kj, 1 - c], kj, 1 - c, sibling).wait_recv()
        for cp in first + passed:
            cp.wait_send()
        for cp in mine:
            cp.wait()

    return pl.pallas_call(
        body, name=name, out_shape=[jax.ShapeDtypeStruct((N_CHIP,) + v.shape, v.dtype) for v in vs],
        in_specs=[_HBM] * na, out_specs=[_HBM] * na,
        scratch_shapes=[pltpu.SemaphoreType.DMA((6 * na,)), pltpu.SemaphoreType.DMA((6 * na,)),
                        pltpu.SemaphoreType.DMA((na,))])(*vs)


def scatter_chips(ps, name):
    na = len(ps)

    def body(*refs):
        p_refs, out_refs = refs[:na], refs[na:2 * na]
        send_sems, recv_sems, local_sems = refs[2 * na:]
        x, y, c, chips = _place()
        k = 2 * x + y
        sibling = (x, y, 1 - c)

        def copy(i, kk, src, chip_k, half, to):
            return pltpu.make_async_remote_copy(
                src_ref=src, dst_ref=out_refs[i].at[chip_k, half], send_sem=send_sems.at[7 * i + kk],
                recv_sem=recv_sems.at[7 * i + kk], device_id=to, device_id_type=MESH)

        mine = [pltpu.make_async_copy(p_refs[i].at[k], out_refs[i].at[k, c], local_sems.at[i]) for i in range(na)]
        for cp in mine:
            cp.start()
        first = [copy(i, 1 + j, p_refs[i].at[2 * cx + cy], k, c, (cx, cy, c))
                 for j, (cx, cy) in enumerate(chips) for i in range(na)]
        first += [copy(i, 0, p_refs[i].at[k], k, c, sibling) for i in range(na)]
        for cp in first:
            cp.start()
        passed = []
        for j, (cx, cy) in enumerate(chips):
            kj = 2 * cx + cy
            for i in range(na):
                copy(i, 1 + j, out_refs[i].at[kj, c], kj, c, (cx, cy, c)).wait_recv()
                fwd = copy(i, 4 + j, out_refs[i].at[kj, c], kj, c, sibling)
                fwd.start()
                passed.append(fwd)
        for i in range(na):
            copy(i, 0, out_refs[i].at[k, 1 - c], k, 1 - c, sibling).wait_recv()
        for j, (cx, cy) in enumerate(chips):
            kj = 2 * cx + cy
            for i in range(na):
                copy(i, 4 + j, out_refs[i].at[kj, 1 - c], kj, 1 - c, sibling).wait_recv()
        for cp in first + passed:
            cp.wait_send()
        for cp in mine:
            cp.wait()

    return pl.pallas_call(
        body, name=name, out_shape=[jax.ShapeDtypeStruct((N_CHIP, 2) + p.shape[1:], p.dtype) for p in ps],
        in_specs=[_HBM] * na, out_specs=[_HBM] * na,
        scratch_shapes=[pltpu.SemaphoreType.DMA((7 * na,)), pltpu.SemaphoreType.DMA((7 * na,)),
                        pltpu.SemaphoreType.DMA((na,))])(*ps)


def swap_halves(gs_, name):
    na = len(gs_)

    def body(*refs):
        g_refs, out_refs = refs[:na], refs[na:2 * na]
        send_sems, recv_sems = refs[2 * na:]
        x, y, c, _ = _place()
        cps = [pltpu.make_async_remote_copy(
            src_ref=g_refs[i].at[kk, 1 - c], dst_ref=out_refs[i].at[kk], send_sem=send_sems.at[N_CHIP * i + kk],
            recv_sem=recv_sems.at[N_CHIP * i + kk], device_id=(x, y, 1 - c), device_id_type=MESH)
            for i in range(na) for kk in range(N_CHIP)]
        for cp in cps:
            cp.start()
        for cp in cps:
            cp.wait()

    return pl.pallas_call(
        body, name=name, out_shape=[jax.ShapeDtypeStruct((N_CHIP,) + g.shape[2:], g.dtype) for g in gs_],
        in_specs=[_HBM] * na, out_specs=[_HBM] * na,
        scratch_shapes=[pltpu.SemaphoreType.DMA((N_CHIP * na,)), pltpu.SemaphoreType.DMA((N_CHIP * na,))])(*gs_)


def add_halves(g, other, core, name):
    _, _, m, n = g.shape
    tr = _tile(m, 256, 16)

    def body(core_ref, g_ref, o_ref, out_ref):
        out_ref[...] = (g_ref[...].astype(F32) + o_ref[...].astype(F32)).astype(out_ref.dtype)

    grid_spec = pltpu.PrefetchScalarGridSpec(
        num_scalar_prefetch=1, grid=(N_CHIP, m // tr),
        in_specs=[pl.BlockSpec((None, None, tr, n), lambda kk, i, core_ref: (kk, core_ref[0], i, 0)),
                  pl.BlockSpec((None, tr, n), lambda kk, i, core_ref: (kk, i, 0))],
        out_specs=pl.BlockSpec((None, tr, n), lambda kk, i, core_ref: (kk, i, 0)))
    return pl.pallas_call(body, name=name, grid_spec=grid_spec, out_shape=jax.ShapeDtypeStruct((N_CHIP, m, n), g.dtype),
                          compiler_params=_params(("arbitrary", "arbitrary")))(core.reshape(1).astype(jnp.int32), g, other)


_SEM = pl.BlockSpec(memory_space=pltpu.SEMAPHORE)
_EFFECT = pltpu.SideEffectType.DATAFLOW_SIDE_EFFECTING
N_PEER_CHIPS = N_CHIP - 1


def _ici_copy(scatter, src_refs, land_refs, send_sems, recv_sems, i, j, chips, k, c, landed):
    cx, cy = chips[j]
    kj = 2 * cx + cy
    src = src_refs[i].at[kj] if scatter else src_refs[i].at[c]
    dst = land_refs[i].at[kj, c] if landed else land_refs[i].at[k, c]
    return pltpu.make_async_remote_copy(src_ref=src, dst_ref=dst, send_sem=send_sems.at[N_PEER_CHIPS * i + j],
                                        recv_sem=recv_sems.at[N_PEER_CHIPS * i + j], device_id=(cx, cy, c),
                                        device_id_type=MESH)


def ici_start(srcs, lands, scatter, collective_id, name):
    na = len(srcs)

    def body(*refs):
        src_refs, land_refs = refs[:na], refs[na:2 * na]
        send_sems, recv_sems = refs[2 * na], refs[2 * na + 1]
        token = refs[-1]
        x, y, c, chips = _place()
        barrier = pltpu.get_barrier_semaphore()
        for cx, cy in chips:
            pl.semaphore_signal(barrier, inc=1, device_id=(cx, cy, c), device_id_type=MESH)
        pl.semaphore_wait(barrier, N_PEER_CHIPS)
        for j in range(N_PEER_CHIPS):
            for i in range(na):
                _ici_copy(scatter, src_refs, land_refs, send_sems, recv_sems, i, j, chips, 2 * x + y, c, False).start()
        token[...] = jnp.zeros_like(token)

    nsem = N_PEER_CHIPS * na
    hbm = lambda a: pltpu.HBM(a.shape, a.dtype)
    res = pl.pallas_call(
        body, name=name,
        out_shape=(pltpu.SemaphoreType.DMA((nsem,)), pltpu.SemaphoreType.DMA((nsem,)), *[hbm(a) for a in srcs],
                   *[hbm(a) for a in lands], jax.ShapeDtypeStruct((SUBLANES, LANES), F32)),
        in_specs=[_HBM] * (2 * na),
        out_specs=(_SEM, _SEM, *([_HBM] * (2 * na)), pl.BlockSpec(memory_space=pltpu.VMEM)),
        input_output_aliases={i: 2 + i for i in range(2 * na)},
        compiler_params=pltpu.CompilerParams(has_side_effects=_EFFECT, collective_id=collective_id))(
            *[pltpu.with_memory_space_constraint(a, pltpu.HBM) for a in srcs],
            *[pltpu.with_memory_space_constraint(a, pltpu.HBM) for a in lands])
    return res[0], res[1], list(res[2:2 + na]), list(res[2 + na:2 + 2 * na]), res[-1]


def ici_wait(send_sems, recv_sems, srcs, lands, after, scatter, name):
    na = len(srcs)
    after = list(after)

    def body(*refs):
        src_refs, land_refs = refs[:na], refs[na:2 * na]
        s_sems, r_sems = refs[2 * na], refs[2 * na + 1]
        x, y, c, chips = _place()
        for j in range(N_PEER_CHIPS):
            for i in range(na):
                _ici_copy(scatter, src_refs, land_refs, s_sems, r_sems, i, j, chips, 2 * x + y, c, False).wait_send()
                _ici_copy(scatter, src_refs, land_refs, s_sems, r_sems, i, j, chips, 2 * x + y, c, True).wait_recv()

    hbm = lambda a: pltpu.HBM(a.shape, a.dtype)
    res = pl.pallas_call(
        body, name=name, out_shape=tuple(hbm(a) for a in list(srcs) + list(lands)),
        in_specs=[_HBM] * (2 * na) + [_SEM, _SEM] + [pl.BlockSpec(memory_space=pl.ANY)] * len(after),
        out_specs=tuple([_HBM] * (2 * na)), input_output_aliases={i: i for i in range(2 * na)},
        compiler_params=pltpu.CompilerParams(has_side_effects=_EFFECT))(*srcs, *lands, send_sems, recv_sems, *after)
    return list(res[:na]), list(res[na:])


def sibling_finish(srcs, lands, scatter, name):
    na = len(srcs)
    per = N_CHIP if scatter else N_PEER_CHIPS

    def body(*refs):
        src_refs, land_in = refs[:na], refs[na:2 * na]
        land_refs = refs[2 * na:3 * na]
        send_sems, recv_sems, local_sems = refs[3 * na:]
        x, y, c, chips = _place()
        k = 2 * x + y
        sibling = (x, y, 1 - c)

        def copy(i, kk, src, chip_k, half):
            return pltpu.make_async_remote_copy(
                src_ref=src, dst_ref=land_refs[i].at[chip_k, half], send_sem=send_sems.at[per * i + kk],
                recv_sem=recv_sems.at[per * i + kk], device_id=sibling, device_id_type=MESH)

        if scatter:
            mine = [pltpu.make_async_copy(src_refs[i].at[k], land_refs[i].at[k, c], local_sems.at[i]) for i in range(na)]
        else:
            mine = [pltpu.make_async_copy(src_refs[i], land_refs[i].at[k], local_sems.at[i]) for i in range(na)]
        for cp in mine:
            cp.start()
        sent = []
        for j, (cx, cy) in enumerate(chips):
            kj = 2 * cx + cy
            sent += [copy(i, j, land_in[i].at[kj, c], kj, c) for i in range(na)]
        if scatter:
            sent += [copy(i, N_PEER_CHIPS, src_refs[i].at[k], k, c) for i in range(na)]
        for cp in sent:
            cp.start()
        for j, (cx, cy) in enumerate(chips):
            kj = 2 * cx + cy
            for i in range(na):
                copy(i, j, land_refs[i].at[kj, 1 - c], kj, 1 - c).wait_recv()
        if scatter:
            for i in range(na):
                copy(i, N_PEER_CHIPS, land_refs[i].at[k, 1 - c], k, 1 - c).wait_recv()
        for cp in sent:
            cp.wait_send()
        for cp in mine:
            cp.wait()

    return pl.pallas_call(
        body, name=name, out_shape=[jax.ShapeDtypeStruct(a.shape, a.dtype) for a in lands],
        in_specs=[_HBM] * (2 * na), out_specs=[_HBM] * na, input_output_aliases={na + i: i for i in range(na)},
        scratch_shapes=[pltpu.SemaphoreType.DMA((per * na,)), pltpu.SemaphoreType.DMA((per * na,)),
                        pltpu.SemaphoreType.DMA((na,))])(*srcs, *lands)


PACK_LANES = 512


def _pack(arrs, dtype, lanes, row_align):
    flat = jnp.concatenate([a.reshape(-1).astype(dtype) for a in arrs])
    unit = lanes * row_align
    total = -(-flat.shape[0] // unit) * unit
    return jnp.pad(flat, (0, total - flat.shape[0])).reshape(total // lanes, lanes)


def _unpack(flat, shapes):
    flat = flat.reshape(-1)
    out, off = [], 0
    for s in shapes:
        n = math.prod(s)
        out.append(flat[off:off + n].reshape(s))
        off += n
    return out


def _adam_tile(w, m, v, g):
    m2 = ADAM_B1 * m + (1.0 - ADAM_B1) * g
    v2 = ADAM_B2 * v + (1.0 - ADAM_B2) * (g * g)
    m_hat = m2 / (1.0 - ADAM_B1 ** ADAM_STEP)
    v_hat = v2 / (1.0 - ADAM_B2 ** ADAM_STEP)
    delta = -ADAM_LR * (m_hat / (jnp.sqrt(v_hat) + ADAM_EPS) + ADAM_WD * w)
    return delta, m2, v2


def adam_big(w, m, v, pieces, name):
    _, r, c = w.shape
    tr = _tile(r, 128, 16)

    def body(w_ref, m_ref, v_ref, p0, p1, p2, p3, g_ref, d_ref, mo_ref, vo_ref):
        g = ((p0[...].astype(F32) + p1[...].astype(F32)) + p2[...].astype(F32)) + p3[...].astype(F32)
        delta, m2, v2 = _adam_tile(w_ref[...], m_ref[...], v_ref[...], g)
        g_ref[...] = g
        d_ref[...] = delta
        mo_ref[...] = m2
        vo_ref[...] = v2

    wspec = pl.BlockSpec((None, tr, c), lambda i: (0, i, 0))
    pspecs = [pl.BlockSpec((None, tr, c), functools.partial(lambda i, kk: (kk, i, 0), kk=kk)) for kk in range(N_CHIP)]
    return pl.pallas_call(
        body, name=name, grid=(r // tr,), in_specs=[wspec] * 3 + pspecs, out_specs=[wspec] * 4,
        out_shape=[jax.ShapeDtypeStruct(w.shape, F32)] * 4, compiler_params=_params(("parallel",)))(
            w, m, v, pieces, pieces, pieces, pieces)


_WEIGHTS = (
    ("meta_tokens", "small", 1), ("ab_norm", "small", None), ("ab_w_in", "big", 2), ("s5_lambda_re", "small", None),
    ("s5_lambda_im", "small", None), ("s5_log_dt", "small", None), ("s5_b_re", "small", None), ("s5_b_im", "small", None),
    ("s5_c_re", "small", None), ("s5_c_im", "small", None), ("s5_d", "small", None), ("s5_glu_w", "big", 1),
    ("s5_glu_b", "small", None), ("ml_conv_w", "small", 2), ("ml_conv_b", "small", None), ("ml_wq", "small", 1),
    ("ml_wk", "small", 1), ("ml_wv", "small", 1), ("ml_w_gate", "small", 1), ("ml_b_gate", "small", None),
    ("ml_norm", "small", None), ("ml_skip", "small", None), ("ab_w_out", "big", 1), ("ssd_norm", "small", 1),
    ("ssd_w_in", "big", 2), ("ssd_conv_w", "small", 2), ("ssd_conv_b", "small", 1), ("ssd_dt_bias", "small", None),
    ("ssd_a_log", "small", None), ("ssd_d", "small", None), ("ssd_gnorm", "small", 1), ("ssd_w_out", "big", 1),
    ("final_norm", "small", None),
)


def _squeeze(a):
    return a[0] if a.ndim >= 3 else a


def kernel(x, meta_tokens, ab_norm, ab_w_in, s5_lambda_re, s5_lambda_im, s5_log_dt, s5_b_re, s5_b_im, s5_c_re, s5_c_im, s5_d, s5_glu_w, s5_glu_b, ml_conv_w, ml_conv_b, ml_wq, ml_wk, ml_wv, ml_w_gate, ml_b_gate, ml_norm, ml_skip, ab_w_out, ssd_norm, ssd_w_in, ssd_conv_w, ssd_conv_b, ssd_dt_bias, ssd_a_log, ssd_d, ssd_gnorm, ssd_w_out, final_norm, loss_target, m_meta_tokens, m_ab_norm, m_ab_w_in, m_s5_lambda_re, m_s5_lambda_im, m_s5_log_dt, m_s5_b_re, m_s5_b_im, m_s5_c_re, m_s5_c_im, m_s5_d, m_s5_glu_w, m_s5_glu_b, m_ml_conv_w, m_ml_conv_b, m_ml_wq, m_ml_wk, m_ml_wv, m_ml_w_gate, m_ml_b_gate, m_ml_norm, m_ml_skip, m_ab_w_out, m_ssd_norm, m_ssd_w_in, m_ssd_conv_w, m_ssd_conv_b, m_ssd_dt_bias, m_ssd_a_log, m_ssd_d, m_ssd_gnorm, m_ssd_w_out, m_final_norm, v_meta_tokens, v_ab_norm, v_ab_w_in, v_s5_lambda_re, v_s5_lambda_im, v_s5_log_dt, v_s5_b_re, v_s5_b_im, v_s5_c_re, v_s5_c_im, v_s5_d, v_s5_glu_w, v_s5_glu_b, v_ml_conv_w, v_ml_conv_b, v_ml_wq, v_ml_wk, v_ml_wv, v_ml_w_gate, v_ml_b_gate, v_ml_norm, v_ml_skip, v_ab_w_out, v_ssd_norm, v_ssd_w_in, v_ssd_conv_w, v_ssd_conv_b, v_ssd_dt_bias, v_ssd_a_log, v_ssd_d, v_ssd_gnorm, v_ssd_w_out, v_final_norm):
    args = (meta_tokens, ab_norm, ab_w_in, s5_lambda_re, s5_lambda_im, s5_log_dt, s5_b_re, s5_b_im, s5_c_re, s5_c_im, s5_d, s5_glu_w, s5_glu_b, ml_conv_w, ml_conv_b, ml_wq, ml_wk, ml_wv, ml_w_gate, ml_b_gate, ml_norm, ml_skip, ab_w_out, ssd_norm, ssd_w_in, ssd_conv_w, ssd_conv_b, ssd_dt_bias, ssd_a_log, ssd_d, ssd_gnorm, ssd_w_out, final_norm)
    m_args = (m_meta_tokens, m_ab_norm, m_ab_w_in, m_s5_lambda_re, m_s5_lambda_im, m_s5_log_dt, m_s5_b_re, m_s5_b_im, m_s5_c_re, m_s5_c_im, m_s5_d, m_s5_glu_w, m_s5_glu_b, m_ml_conv_w, m_ml_conv_b, m_ml_wq, m_ml_wk, m_ml_wv, m_ml_w_gate, m_ml_b_gate, m_ml_norm, m_ml_skip, m_ab_w_out, m_ssd_norm, m_ssd_w_in, m_ssd_conv_w, m_ssd_conv_b, m_ssd_dt_bias, m_ssd_a_log, m_ssd_d, m_ssd_gnorm, m_ssd_w_out, m_final_norm)
    v_args = (v_meta_tokens, v_ab_norm, v_ab_w_in, v_s5_lambda_re, v_s5_lambda_im, v_s5_log_dt, v_s5_b_re, v_s5_b_im, v_s5_c_re, v_s5_c_im, v_s5_d, v_s5_glu_w, v_s5_glu_b, v_ml_conv_w, v_ml_conv_b, v_ml_wq, v_ml_wk, v_ml_wv, v_ml_w_gate, v_ml_b_gate, v_ml_norm, v_ml_skip, v_ab_w_out, v_ssd_norm, v_ssd_w_in, v_ssd_conv_w, v_ssd_conv_b, v_ssd_dt_bias, v_ssd_a_log, v_ssd_d, v_ssd_gnorm, v_ssd_w_out, v_final_norm)
    names = [w[0] for w in _WEIGHTS]
    kind = {w[0]: w[1] for w in _WEIGHTS}
    axis = {w[0]: w[2] for w in _WEIGHTS}
    w_loc = dict(zip(names, args))
    m_loc = dict(zip(names, m_args))
    v_loc = dict(zip(names, v_args))
    chip = 2 * lax.axis_index("x") + lax.axis_index("y")
    core = lax.axis_index("c")
    big = [n for n in names if kind[n] == "big"]
    small = [n for n in names if kind[n] == "small"]
    small_sh = [n for n in small if axis[n] is not None]

    def halves(a):
        return a.astype(BF16).reshape(2, a.shape[1] // 2, a.shape[2])

    def assemble(n, gth):
        shard = gth.reshape((N_CHIP,) + w_loc[n].shape[1:])
        if axis[n] == 1:
            return shard.reshape(-1, shard.shape[2])
        return jnp.concatenate([shard[kk] for kk in range(N_CHIP)], axis=1)

    early = ["ab_w_in", "s5_glu_w"]
    late = ["ab_w_out", "ssd_w_in", "ssd_w_out"]
    gathered = gather_chips([halves(w_loc[n]) for n in early], "gather_early_w")
    after_early = (gathered[0][0, 0, 0, 0] * 0).astype(BF16)
    late_src = [halves(w_loc[n]) + after_early for n in late]
    late_land = [lax.empty((N_CHIP,) + s.shape, BF16) for s in late_src]
    w_send, w_recv, late_src, late_land, w_token = ici_start(late_src, late_land, False, 1, "gather_late_start")
    started = w_token[0, 0].astype(BF16)
    w_in0_shards = gathered[0].reshape((N_CHIP,) + w_loc["ab_w_in"].shape[1:]) + started
    glu_full = assemble("s5_glu_w", gathered[1])

    def columns(lo, hi):
        cw = w_in0_shards.shape[2]
        parts = [w_in0_shards[kk][:, max(lo - kk * cw, 0):min(hi - kk * cw, cw)]
                 for kk in range(N_CHIP) if lo < (kk + 1) * cw and hi > kk * cw]
        return parts[0] if len(parts) == 1 else jnp.concatenate(parts, axis=1)

    small_sh_shapes = [w_loc[n].shape for n in small_sh]
    packed_s = _pack([w_loc[n] for n in small_sh], F32, LANES, SUBLANES)
    g8 = all_gather8(packed_s, "gather_small_w").reshape(N_CHIP, 2, -1)
    sp = {}
    for n in small:
        if axis[n] is None:
            sp[n] = _squeeze(w_loc[n])
    per_chip = [_unpack(g8[kk, 0], small_sh_shapes) for kk in range(N_CHIP)]
    for i, n in enumerate(small_sh):
        sp[n] = _squeeze(jnp.concatenate([per_chip[kk][i] for kk in range(N_CHIP)], axis=axis[n]))

    s5w = glu_full.shape[0]
    mlw = w_loc["ab_w_out"].shape[1] * N_CHIP - s5w
    inner = w_loc["ssd_w_out"].shape[1] * N_CHIP
    n_heads1 = sp["ssd_d"].shape[1]
    cdim = w_loc["ssd_w_in"].shape[2] * N_CHIP - inner - n_heads1
    bw = dict(W0a=columns(0, 2 * s5w), W0xb=columns(2 * s5w, 2 * s5w + mlw),
              W0zb=columns(2 * s5w + mlw, 2 * (s5w + mlw)), glu=glu_full)

    def late_weights(after):
        src, land = ici_wait(w_send, w_recv, late_src, late_land, after, False, "gather_late_wait")
        fb = {n: assemble(n, gth) for n, gth in zip(late, sibling_finish(src, land, False, "gather_late_finish"))}
        w_in1 = fb["ssd_w_in"]
        return dict(Wo0a=fb["ab_w_out"][:s5w], Wo0b=fb["ab_w_out"][s5w:], W1z=w_in1[:, :inner],
                    W1x=w_in1[:, inner:inner + cdim], W1dt=_pad_lanes(w_in1[:, inner + cdim:]), Wo1=fb["ssd_w_out"])

    def chip_halves(n, gf):
        _, r, c_ = w_loc[n].shape
        if axis[n] == 1:
            return gf.reshape(N_CHIP, 2, r // 2, c_)
        return jnp.stack([gf[:, kk * c_:(kk + 1) * c_] for kk in range(N_CHIP)]).reshape(N_CHIP, 2, r // 2, c_)

    def chip_partials(ns, gfull, tag):
        gps = [chip_halves(n, gfull[n]) for n in ns]
        from_sibling = swap_halves(gps, "swap_" + tag)
        return [add_halves(gp, oth, core, "add_" + n) for n, gp, oth in zip(ns, gps, from_sibling)]

    late_state = {}

    def late_grads(g):
        gfull = {"ab_w_out": jnp.concatenate([g["Wo0a"], g["Wo0b"]], axis=0),
                 "ssd_w_in": jnp.concatenate([g["W1z"], g["W1x"], g["W1dt"][:, :n_heads1]], axis=1),
                 "ssd_w_out": g["Wo1"]}
        partials = chip_partials(late, gfull, "late_g")
        land = [lax.empty((N_CHIP, 2) + p.shape[1:], BF16) for p in partials]
        late_state["copy"] = ici_start(partials, land, True, 2, "scatter_late_start")
        return late_state["copy"][4][0, 0]

    loss_local, dh0, gbig, gs = _local_step(x, loss_target, bw, sp, late_weights, late_grads, w_token[0, 0])
    loss = lax.psum(loss_local, ("x", "y", "c"))
    grad_x = dh0[:, N_META:N_META + x.shape[1]]

    gfull = {"ab_w_in": jnp.concatenate([gbig["W0a"], gbig["W0xb"], gbig["W0zb"]], axis=1), "s5_glu_w": gbig["glu"]}
    pieces = dict(zip(early, scatter_chips(chip_partials(early, gfull, "early_g"), "scatter_early_g")))
    g_send, g_recv, g_src, g_land, _ = late_state["copy"]
    g_src, g_land = ici_wait(g_send, g_recv, g_src, g_land, [dh0], True, "scatter_late_wait")
    pieces.update(zip(late, sibling_finish(g_src, g_land, True, "scatter_late_finish")))

    out_g, out_d, out_m, out_v = {}, {}, {}, {}
    for n in big:
        pcs = pieces[n].reshape((N_CHIP,) + w_loc[n].shape[1:])
        out_g[n], out_d[n], out_m[n], out_v[n] = adam_big(w_loc[n], m_loc[n], v_loc[n], pcs, "adam_" + n)

    small_full_shapes = [sp[n].shape for n in small]
    packed_gs = _pack([gs[n] for n in small], F32, LANES, SUBLANES)
    rows_s = packed_gs.shape[0]
    all_gs = all_gather8(packed_gs, "gather_small_g")
    blocks = [all_gs[i * rows_s:(i + 1) * rows_s] for i in range(N_DEV)]

    def sum8(i, *b):
        acc = b[0]
        for t in b[1:]:
            acc = acc + t
        return acc

    gsum = rowwise("sum_small_g", sum8, blocks, [], [(LANES, F32)], tr=_tile(rows_s, 512, 8))[0]
    g_small = dict(zip(small, _unpack(gsum, small_full_shapes)))
    g_loc = {}
    for n in small:
        g = g_small[n].reshape((1,) + g_small[n].shape) if w_loc[n].ndim >= 3 else g_small[n]
        if axis[n] is not None:
            size = w_loc[n].shape[axis[n]]
            g = lax.dynamic_slice_in_dim(g, chip * size, size, axis=axis[n])
        g_loc[n] = g.reshape(w_loc[n].shape)
    loc_shapes = [w_loc[n].shape for n in small]
    pw, pm, pv, pg = (_pack([d[n] for n in small], F32, LANES, SUBLANES) for d in (w_loc, m_loc, v_loc, g_loc))
    dl, mn, vn = rowwise("adam_small", lambda i, a, b, c_, d_: _adam_tile(a, b, c_, d_), [pw, pm, pv, pg], [],
                         [(LANES, F32)] * 3, tr=_tile(pw.shape[0], 512, 8))
    for d_out, flat in ((out_d, dl), (out_m, mn), (out_v, vn)):
        for n, a in zip(small, _unpack(flat, loc_shapes)):
            d_out[n] = a
    for n in small:
        out_g[n] = g_loc[n]

    return (loss, grad_x, *[out_g[n] for n in names], *[out_d[n] for n in names], *[out_m[n] for n in names],
            *[out_v[n] for n in names])
```

```python
import functools
import math

import jax
import jax.numpy as jnp
from jax import lax
from jax.experimental import pallas as pl
from jax.experimental.pallas import tpu as pltpu

F32 = jnp.float32
BF16 = jnp.bfloat16
HI = lax.Precision.HIGHEST

D_MODEL = 2048
SEQ = 2048
N_META = 16
CHUNK = 128
NORM_EPS = 1e-6
HEAD_NORM_EPS = 1e-5
S5_GROUP_SIZE = 16
S5_STATE = 64
MLSTM_HEADS = 8
QKV_BLOCK = 4
SSD_HEAD_DIM = 64
SSD_STATE = 128
SSD_HPG = 8
ADAM_LR = 0.001
ADAM_B1 = 0.9
ADAM_B2 = 0.999
ADAM_EPS = 1e-08
ADAM_WD = 0.01
ADAM_STEP = 10

LANES = 128
SUBLANES = 8
VMEM_LIMIT = 56 * 1024 * 1024
MM_OPERAND_VMEM = 34 * 1024 * 1024


def _sigmoid(x):
    return 0.5 * jnp.tanh(0.5 * x) + 0.5


@jax.custom_vjp
def _silu(x):
    return x * _sigmoid(x)


def _silu_fwd(x):
    return x * _sigmoid(x), x


def _silu_bwd(x, ct):
    s = _sigmoid(x)
    return (ct * (s * (1.0 + x * (1.0 - s))),)


_silu.defvjp(_silu_fwd, _silu_bwd)


def _softplus(x):
    return jnp.maximum(x, 0.0) + jnp.log(1.0 + jnp.exp(-jnp.abs(x)))


def _log_sigmoid(x):
    return jnp.minimum(x, 0.0) - jnp.log(1.0 + jnp.exp(-jnp.abs(x)))


def _gelu(x):
    return 0.5 * x * (1.0 + jnp.tanh(math.sqrt(2.0 / math.pi) * (x + 0.044715 * (x * x * x))))


def _dot(a, b, dims, precision=None):
    return lax.dot_general(a, b, (dims, ((), ())), preferred_element_type=F32, precision=precision)


_NN, _NT, _TN = ((1,), (0,)), ((1,), (1,)), ((0,), (0,))


def _bf16_dot(dims, da_rule, db_rule):
    @jax.custom_vjp
    def f(a, b):
        return _dot(a.astype(BF16), b.astype(BF16), dims)

    def fwd(a, b):
        ab, bb = a.astype(BF16), b.astype(BF16)
        return _dot(ab, bb, dims), (ab, bb, jnp.zeros((), a.dtype), jnp.zeros((), b.dtype))

    def bwd(res, ct):
        ab, bb, a_like, b_like = res
        cb = ct.astype(BF16)
        return da_rule(ab, bb, cb).astype(a_like.dtype), db_rule(ab, bb, cb).astype(b_like.dtype)

    f.defvjp(fwd, bwd)
    return f


_dot_nn = _bf16_dot(_NN, lambda a, b, c: _dot(c, b, _NT), lambda a, b, c: _dot(a, c, _TN))
_dot_nt = _bf16_dot(_NT, lambda a, b, c: _dot(c, b, _NN), lambda a, b, c: _dot(c, a, _TN))
_dot_tn = _bf16_dot(_TN, lambda a, b, c: _dot(b, c, _NT), lambda a, b, c: _dot(a, c, _NN))


def _lane_pick(a, idx):
    sel = (lax.broadcasted_iota(jnp.int32, (1, a.shape[1]), 1) == idx).astype(a.dtype)
    return jnp.sum(a * sel, axis=1, keepdims=True)


def _row_pick(a, idx):
    sel = (lax.broadcasted_iota(jnp.int32, (a.shape[0], 1), 0) == idx).astype(a.dtype)
    return jnp.sum(a * sel, axis=0, keepdims=True)


def _tri(n, upper=False):
    r = lax.broadcasted_iota(jnp.int32, (n, n), 0)
    c = lax.broadcasted_iota(jnp.int32, (n, n), 1)
    return ((r <= c) if upper else (r >= c)).astype(F32)


def _tile(n, target, align):
    if n <= target:
        return n
    t = (target // align) * align
    while t >= align:
        if n % t == 0:
            return t
        t -= align
    return n


def _params(sem=None):
    return pltpu.CompilerParams(dimension_semantics=sem, vmem_limit_bytes=VMEM_LIMIT)


def mm(a, b, mode, name, resid=None, out_dtype=F32):
    if mode == "nn":
        (m, k), (k2, n) = a.shape, b.shape
    elif mode == "nt":
        (m, k), (n, k2) = a.shape, b.shape
    else:
        (k, m), (k2, n) = a.shape, b.shape
    assert k == k2, (a.shape, b.shape, mode)
    a_sz, b_sz = a.dtype.itemsize, b.dtype.itemsize
    if mode == "tn":
        tm, tn = _tile(m, 1024, LANES), _tile(n, 1024, LANES)
        tk = _tile(k, MM_OPERAND_VMEM // (2 * (tm * a_sz + tn * b_sz)), 16)
    else:
        tm, tn = _tile(m, 1088, 16), _tile(n, 512, LANES)
        tk = _tile(k, MM_OPERAND_VMEM // (2 * (tm * a_sz + tn * b_sz)), LANES)
    nk = k // tk
    dims = {"nn": ((1,), (0,)), "nt": ((1,), (1,)), "tn": ((0,), (0,))}[mode]
    has_resid = resid is not None

    def body(*refs):
        if has_resid:
            a_ref, b_ref, r_ref, o_ref = refs[:4]
        else:
            a_ref, b_ref, o_ref = refs[:3]
        part = _dot(a_ref[...].astype(BF16), b_ref[...].astype(BF16), dims)

        def finish(res):
            if has_resid:
                res = res + r_ref[...].astype(F32)
            o_ref[...] = res.astype(o_ref.dtype)

        if nk == 1:
            finish(part)
            return
        acc_ref = refs[-1]
        kk = pl.program_id(2)

        @pl.when(kk == 0)
        def _():
            acc_ref[...] = part

        @pl.when(jnp.logical_and(kk > 0, kk < nk - 1))
        def _():
            acc_ref[...] += part

        @pl.when(kk == nk - 1)
        def _():
            finish(acc_ref[...] + part)

    if mode == "tn":
        a_spec = pl.BlockSpec((tk, tm), lambda i, j, kk: (kk, i))
    else:
        a_spec = pl.BlockSpec((tm, tk), lambda i, j, kk: (i, kk))
    if mode == "nt":
        b_spec = pl.BlockSpec((tn, tk), lambda i, j, kk: (j, kk))
    else:
        b_spec = pl.BlockSpec((tk, tn), lambda i, j, kk: (kk, j))
    o_spec = pl.BlockSpec((tm, tn), lambda i, j, kk: (i, j))
    in_specs = [a_spec, b_spec] + ([o_spec] if has_resid else [])
    args = (a, b) + ((resid,) if has_resid else ())
    return pl.pallas_call(
        body, name=name, grid=(m // tm, n // tn, nk), in_specs=in_specs, out_specs=o_spec,
        out_shape=jax.ShapeDtypeStruct((m, n), out_dtype), scratch_shapes=[pltpu.VMEM((tm, tn), F32)] if nk > 1 else [],
        compiler_params=_params(("parallel", "parallel", "arbitrary")))(*args)


def rowwise(name, f, rows, params, outs, accs=(), tr=128):
    n_rows = rows[0].shape[0]
    assert n_rows % tr == 0
    n_r, n_p, n_o, n_a = len(rows), len(params), len(outs), len(accs)

    def body(*refs):
        i = pl.program_id(0)
        r_vals = [r[...] for r in refs[:n_r]]
        p_vals = [r[...] for r in refs[n_r:n_r + n_p]]
        o_refs = refs[n_r + n_p:n_r + n_p + n_o]
        a_refs = refs[n_r + n_p + n_o:]
        res = f(i, *r_vals, *p_vals)
        if not isinstance(res, (tuple, list)):
            res = (res,)
        assert len(res) == n_o + n_a, (name, len(res))
        for o_ref, val in zip(o_refs, res[:n_o]):
            o_ref[...] = val.astype(o_ref.dtype)
        if n_a:
            @pl.when(i == 0)
            def _():
                for a_ref in a_refs:
                    a_ref[...] = jnp.zeros_like(a_ref)

            for a_ref, val in zip(a_refs, res[n_o:]):
                a_ref[...] += val.astype(F32)

    in_specs = [pl.BlockSpec((tr, r.shape[1]), lambda i: (i, 0)) for r in rows]
    in_specs += [pl.BlockSpec(p.shape, lambda i: (0, 0)) for p in params]
    out_specs = [pl.BlockSpec((tr, w), lambda i: (i, 0)) for w, _ in outs]
    out_specs += [pl.BlockSpec(s, lambda i: (0, 0)) for s in accs]
    out_shape = [jax.ShapeDtypeStruct((n_rows, w), dt) for w, dt in outs]
    out_shape += [jax.ShapeDtypeStruct(s, F32) for s in accs]
    res = pl.pallas_call(
        body, name=name, grid=(n_rows // tr,), in_specs=in_specs, out_specs=out_specs, out_shape=out_shape,
        compiler_params=_params(("arbitrary",)))(*rows, *params)
    return res


def _rms(x, g, eps=NORM_EPS):
    return x * lax.rsqrt(jnp.mean(x * x, axis=-1, keepdims=True) + eps) * g


def norm_fwd(x, g, name):
    return rowwise(name, lambda i, xb, gb: _rms(xb, gb), [x], [g], [(x.shape[1], BF16)], tr=_tile(x.shape[0], 256, 16))[0]


def norm_bwd(x, g, dn, resid, name):
    def f(i, xb, dnb, rb, gb):
        _, vjp = jax.vjp(_rms, xb, gb)
        dx, dg = vjp(dnb)
        return dx + rb, dx + rb, dg

    return rowwise(name, f, [x, dn, resid], [g], [(x.shape[1], F32), (x.shape[1], BF16)], [g.shape],
                   tr=_tile(x.shape[0], 256, 16))


def conv_fwd(x, w, b, nb, name):
    rows, width = x.shape
    nc = rows // nb // CHUNK
    tw = _tile(width, 1024, LANES)
    ksz = w.shape[0]

    def body(x_ref, w_ref, b_ref, o_ref, ext_ref):
        c = pl.program_id(2)

        @pl.when(c == 0)
        def _():
            ext_ref[0:SUBLANES, :] = jnp.zeros((SUBLANES, tw), F32)

        ext_ref[SUBLANES:2 * SUBLANES, :] = x_ref[0:SUBLANES, :]
        taps = [w_ref[j:j + 1, :] for j in range(ksz)]
        bias = b_ref[...]
        for s in range(CHUNK // SUBLANES):
            r0 = s * SUBLANES
            acc = jnp.broadcast_to(bias, (SUBLANES, tw))
            for j in range(ksz):
                back = ksz - 1 - j
                if s == 0:
                    src = ext_ref[SUBLANES - back:2 * SUBLANES - back, :]
                else:
                    src = x_ref[r0 - back:r0 - back + SUBLANES, :]
                acc = acc + taps[j] * src
            o_ref[r0:r0 + SUBLANES, :] = acc
        ext_ref[0:SUBLANES, :] = x_ref[CHUNK - SUBLANES:CHUNK, :]

    return pl.pallas_call(
        body, name=name, grid=(width // tw, nb, nc),
        in_specs=[pl.BlockSpec((CHUNK, tw), lambda j, bb, c: (bb * nc + c, j)),
                  pl.BlockSpec((ksz, tw), lambda j, bb, c: (0, j)),
                  pl.BlockSpec((1, tw), lambda j, bb, c: (0, j))],
        out_specs=pl.BlockSpec((CHUNK, tw), lambda j, bb, c: (bb * nc + c, j)),
        out_shape=jax.ShapeDtypeStruct((rows, width), F32),
        scratch_shapes=[pltpu.VMEM((2 * SUBLANES, tw), F32)],
        compiler_params=_params(("arbitrary", "arbitrary", "arbitrary")))(x, w, b)


def conv_bwd(dc, x, w, nb, name, resid=None, dx_dtype=BF16):
    rows, width = x.shape
    nc = rows // nb // CHUNK
    tw = _tile(width, 1024, LANES)
    ksz = w.shape[0]
    per = CHUNK // SUBLANES
    has_resid = resid is not None

    def body(*refs):
        if has_resid:
            dc_ref, x_ref, halo_ref, w_ref, r_ref, dx_ref, dw_ref, db_ref, extd_ref, extx_ref = refs
        else:
            dc_ref, x_ref, halo_ref, w_ref, dx_ref, dw_ref, db_ref, extd_ref, extx_ref = refs
        bb = pl.program_id(1)
        step = pl.program_id(2)
        c = nc - 1 - step

        @pl.when(jnp.logical_and(bb == 0, step == 0))
        def _():
            dw_ref[...] = jnp.zeros_like(dw_ref)
            db_ref[...] = jnp.zeros_like(db_ref)

        @pl.when(step == 0)
        def _():
            extd_ref[SUBLANES:2 * SUBLANES, :] = jnp.zeros((SUBLANES, tw), F32)

        nstrip = CHUNK // SUBLANES
        extd_ref[0:SUBLANES, :] = dc_ref[CHUNK - SUBLANES:CHUNK, :]
        extx_ref[0:SUBLANES, :] = jnp.where(c == 0, 0.0, halo_ref[...])
        extx_ref[SUBLANES:2 * SUBLANES, :] = x_ref[0:SUBLANES, :]
        taps = [w_ref[j:j + 1, :] for j in range(ksz)]
        for s in range(nstrip):
            r0 = s * SUBLANES
            dcs = dc_ref[r0:r0 + SUBLANES, :]
            dx = r_ref[r0:r0 + SUBLANES, :].astype(F32) if has_resid else jnp.zeros((SUBLANES, tw), F32)
            for j in range(ksz):
                up = ksz - 1 - j
                if up == 0:
                    fwd = dcs
                elif s == nstrip - 1:
                    fwd = extd_ref[up:up + SUBLANES, :]
                else:
                    fwd = dc_ref[r0 + up:r0 + up + SUBLANES, :]
                dx = dx + taps[j] * fwd
                if up == 0:
                    xs = x_ref[r0:r0 + SUBLANES, :]
                elif s == 0:
                    xs = extx_ref[SUBLANES - up:2 * SUBLANES - up, :]
                else:
                    xs = x_ref[r0 - up:r0 - up + SUBLANES, :]
                dw_ref[j * SUBLANES:(j + 1) * SUBLANES, :] += dcs * xs
            if s % 2 == 0:
                held = dx
            else:
                dx_ref[r0 - SUBLANES:r0 + SUBLANES, :] = jnp.concatenate([held, dx], axis=0).astype(dx_ref.dtype)
            db_ref[...] += dcs
        extd_ref[SUBLANES:2 * SUBLANES, :] = dc_ref[0:SUBLANES, :]

    def blk(j, bb, step):
        return (bb * nc + nc - 1 - step, j)

    def halo(j, bb, step):
        return (jnp.maximum((bb * nc + nc - 1 - step) * per - 1, 0), j)

    in_specs = [pl.BlockSpec((CHUNK, tw), blk), pl.BlockSpec((CHUNK, tw), blk), pl.BlockSpec((SUBLANES, tw), halo),
                pl.BlockSpec((ksz, tw), lambda j, bb, step: (0, j))]
    args = [dc, x, x, w]
    if has_resid:
        in_specs.append(pl.BlockSpec((CHUNK, tw), blk))
        args.append(resid)
    dx, dw_raw, db_raw = pl.pallas_call(
        body, name=name, grid=(width // tw, nb, nc), in_specs=in_specs,
        out_specs=[pl.BlockSpec((CHUNK, tw), blk), pl.BlockSpec((ksz * SUBLANES, tw), lambda j, bb, step: (0, j)),
                   pl.BlockSpec((SUBLANES, tw), lambda j, bb, step: (0, j))],
        out_shape=[jax.ShapeDtypeStruct((rows, width), dx_dtype), jax.ShapeDtypeStruct((ksz * SUBLANES, width), F32),
                   jax.ShapeDtypeStruct((SUBLANES, width), F32)],
        scratch_shapes=[pltpu.VMEM((2 * SUBLANES, tw), F32), pltpu.VMEM((2 * SUBLANES, tw), F32)],
        compiler_params=_params(("arbitrary", "arbitrary", "arbitrary")))(*args)
    return dx, dw_raw.reshape(ksz, SUBLANES, width).sum(axis=1), db_raw.sum(axis=0, keepdims=True)


S5_Q = 4


def _s5_fill_bu(u, bre_ref, bim_ref, xr_ref, xi_ref, ns):
    for s in range(ns):
        ub = u[:, s * LANES:(s + 1) * LANES].astype(BF16)
        bur = _dot(ub, bre_ref[s], ((1,), (0,)))
        bui = _dot(ub, bim_ref[s], ((1,), (0,)))
        for q in range(S5_Q):
            xr_ref[q, pl.ds(s, CHUNK, stride=ns), :] = bur[:, q * LANES:(q + 1) * LANES]
            xi_ref[q, pl.ds(s, CHUNK, stride=ns), :] = bui[:, q * LANES:(q + 1) * LANES]


def _s5_scan(xr_ref, xi_ref, ar_ref, ai_ref, st_ref, ns):
    ar = [ar_ref[q] for q in range(S5_Q)]
    ai = [ai_ref[q] for q in range(S5_Q)]

    def step(t, carry):
        rows = pl.ds(pl.multiple_of(t * ns, ns), ns)
        out = []
        for q in range(S5_Q):
            pr, pi_ = carry[2 * q], carry[2 * q + 1]
            nr = ar[q] * pr - ai[q] * pi_ + xr_ref[q, rows, :]
            ni = ar[q] * pi_ + ai[q] * pr + xi_ref[q, rows, :]
            xr_ref[q, rows, :] = nr
            xi_ref[q, rows, :] = ni
            out += [nr, ni]
        return tuple(out)

    init = []
    for q in range(S5_Q):
        init += [st_ref[0, q], st_ref[1, q]]
    fin = lax.fori_loop(0, CHUNK, step, tuple(init), unroll=2)
    for q in range(S5_Q):
        st_ref[0, q] = fin[2 * q]
        st_ref[1, q] = fin[2 * q + 1]


def s5_fwd(pa, bre, bim, cre, cim, ar, ai, dvec, nb, name):
    rows = pa.shape[0]
    width = pa.shape[1] // 2
    ns = width // LANES
    nc = rows // nb // CHUNK

    def body(u_ref, bre_ref, bim_ref, cre_ref, cim_ref, ar_ref, ai_ref, d_ref, y_ref, g_ref, so_ref, xr_ref, xi_ref, st_ref):
        c = pl.program_id(1)

        @pl.when(c == 0)
        def _():
            st_ref[...] = jnp.zeros_like(st_ref)

        so_ref[...] = st_ref[...]
        u = u_ref[...]
        _s5_fill_bu(u, bre_ref, bim_ref, xr_ref, xi_ref, ns)
        _s5_scan(xr_ref, xi_ref, ar_ref, ai_ref, st_ref, ns)
        for s in range(ns):
            acc = jnp.zeros((CHUNK, LANES), F32)
            for q in range(S5_Q):
                xr = xr_ref[q, pl.ds(s, CHUNK, stride=ns), :].astype(BF16)
                xi = xi_ref[q, pl.ds(s, CHUNK, stride=ns), :].astype(BF16)
                acc = acc + _dot(xr, cre_ref[s, q * LANES:(q + 1) * LANES, :], ((1,), (0,)))
                acc = acc - _dot(xi, cim_ref[s, q * LANES:(q + 1) * LANES, :], ((1,), (0,)))
            cols = slice(s * LANES, (s + 1) * LANES)
            y = acc + d_ref[:, cols] * u[:, cols]
            y_ref[:, cols] = y
            g_ref[:, cols] = _gelu(y).astype(BF16)

    whole3 = lambda a: pl.BlockSpec(a.shape, lambda b_, c: (0, 0, 0))
    return pl.pallas_call(
        body, name=name, grid=(nb, nc),
        in_specs=[pl.BlockSpec((CHUNK, width), lambda b_, c: (b_ * nc + c, 0)), whole3(bre), whole3(bim), whole3(cre),
                  whole3(cim), whole3(ar), whole3(ai), pl.BlockSpec((1, width), lambda b_, c: (0, 0))],
        out_specs=[pl.BlockSpec((CHUNK, width), lambda b_, c: (b_ * nc + c, 0)),
                   pl.BlockSpec((CHUNK, width), lambda b_, c: (b_ * nc + c, 0)),
                   pl.BlockSpec((None, 2, S5_Q, ns, LANES), lambda b_, c: (b_ * nc + c, 0, 0, 0, 0))],
        out_shape=[jax.ShapeDtypeStruct((rows, width), F32), jax.ShapeDtypeStruct((rows, width), BF16),
                   jax.ShapeDtypeStruct((nb * nc, 2, S5_Q, ns, LANES), F32)],
        scratch_shapes=[pltpu.VMEM((S5_Q, CHUNK * ns, LANES), F32), pltpu.VMEM((S5_Q, CHUNK * ns, LANES), F32),
                        pltpu.VMEM((2, S5_Q, ns, LANES), F32)],
        compiler_params=_params(("arbitrary", "arbitrary")))(pa, bre, bim, cre, cim, ar, ai, dvec)


def s5_bwd(pa, dys, states, bre, bim, cre, cim, ar, ai, dvec, nb, name):
    rows = pa.shape[0]
    width = pa.shape[1] // 2
    ns = width // LANES
    nc = rows // nb // CHUNK

    def body(u_ref, dy_ref, sin_ref, bre_ref, bim_ref, cre_ref, cim_ref, ar_ref, ai_ref, d_ref,
             du_ref, dbre_ref, dbim_ref, dcre_ref, dcim_ref, dar_ref, dai_ref, dd_ref,
             xr_ref, xi_ref, lr_ref, li_ref, st_ref, lam_ref):
        bb = pl.program_id(0)
        step_i = pl.program_id(1)

        @pl.when(jnp.logical_and(bb == 0, step_i == 0))
        def _():
            for r in (dbre_ref, dbim_ref, dcre_ref, dcim_ref, dar_ref, dai_ref, dd_ref):
                r[...] = jnp.zeros_like(r)

        @pl.when(step_i == 0)
        def _():
            lam_ref[...] = jnp.zeros_like(lam_ref)

        u = u_ref[...]
        dy = dy_ref[...]
        st_ref[...] = sin_ref[...]
        _s5_fill_bu(u, bre_ref, bim_ref, xr_ref, xi_ref, ns)
        _s5_scan(xr_ref, xi_ref, ar_ref, ai_ref, st_ref, ns)
        dd_ref[...] += jnp.sum(dy * u, axis=0, keepdims=True)
        for s in range(ns):
            dyb = dy[:, s * LANES:(s + 1) * LANES].astype(BF16)
            gr = _dot(dyb, cre_ref[s], ((1,), (1,)))
            gi = -_dot(dyb, cim_ref[s], ((1,), (1,)))
            for q in range(S5_Q):
                lr_ref[q, pl.ds(s, CHUNK, stride=ns), :] = gr[:, q * LANES:(q + 1) * LANES]
                li_ref[q, pl.ds(s, CHUNK, stride=ns), :] = gi[:, q * LANES:(q + 1) * LANES]
                xr = xr_ref[q, pl.ds(s, CHUNK, stride=ns), :].astype(BF16)
                xi = xi_ref[q, pl.ds(s, CHUNK, stride=ns), :].astype(BF16)
                dcre_ref[s, q * LANES:(q + 1) * LANES, :] += _dot(xr, dyb, ((0,), (0,)))
                dcim_ref[s, q * LANES:(q + 1) * LANES, :] -= _dot(xi, dyb, ((0,), (0,)))
        ar = [ar_ref[q] for q in range(S5_Q)]
        ai = [ai_ref[q] for q in range(S5_Q)]

        def one(t_rows, p_r, p_i, carry):
            out = []
            for q in range(S5_Q):
                l_r, l_i, da_r, da_i = carry[4 * q:4 * q + 4]
                n_r = lr_ref[q, t_rows, :] + ar[q] * l_r + ai[q] * l_i
                n_i = li_ref[q, t_rows, :] + ar[q] * l_i - ai[q] * l_r
                lr_ref[q, t_rows, :] = n_r
                li_ref[q, t_rows, :] = n_i
                xpr, xpi = p_r(q), p_i(q)
                out += [n_r, n_i, da_r + n_r * xpr + n_i * xpi, da_i + n_i * xpr - n_r * xpi]
            return tuple(out)

        def step(k, carry):
            t = CHUNK - 1 - k
            t_rows = pl.ds(pl.multiple_of(t * ns, ns), ns)
            p_rows = pl.ds(pl.multiple_of((t - 1) * ns, ns), ns)
            return one(t_rows, lambda q: xr_ref[q, p_rows, :], lambda q: xi_ref[q, p_rows, :], carry)

        init = []
        zero = jnp.zeros((ns, LANES), F32)
        for q in range(S5_Q):
            init += [lam_ref[0, q], lam_ref[1, q], zero, zero]
        carry = lax.fori_loop(0, CHUNK - 1, step, tuple(init), unroll=2)
        carry = one(pl.ds(0, ns), lambda q: sin_ref[0, q], lambda q: sin_ref[1, q], carry)
        for q in range(S5_Q):
            lam_ref[0, q] = carry[4 * q]
            lam_ref[1, q] = carry[4 * q + 1]
            dar_ref[q] += carry[4 * q + 2]
            dai_ref[q] += carry[4 * q + 3]
        for s in range(ns):
            cols = slice(s * LANES, (s + 1) * LANES)
            ub = u[:, cols].astype(BF16)
            acc = d_ref[:, cols] * dy[:, cols]
            for q in range(S5_Q):
                qs = slice(q * LANES, (q + 1) * LANES)
                lr = lr_ref[q, pl.ds(s, CHUNK, stride=ns), :].astype(BF16)
                li = li_ref[q, pl.ds(s, CHUNK, stride=ns), :].astype(BF16)
                dbre_ref[s, :, qs] += _dot(ub, lr, ((0,), (0,)))
                dbim_ref[s, :, qs] += _dot(ub, li, ((0,), (0,)))
                acc = acc + _dot(lr, bre_ref[s, :, qs], ((1,), (1,))) + _dot(li, bim_ref[s, :, qs], ((1,), (1,)))
            du_ref[:, cols] = acc.astype(du_ref.dtype)

    whole3 = lambda a: pl.BlockSpec(a.shape, lambda b_, c: (0, 0, 0))
    rowblk = pl.BlockSpec((CHUNK, width), lambda b_, c: (b_ * nc + nc - 1 - c, 0))
    scr = pltpu.VMEM((S5_Q, CHUNK * ns, LANES), F32)
    return pl.pallas_call(
        body, name=name, grid=(nb, nc),
        in_specs=[rowblk, rowblk,
                  pl.BlockSpec((None, 2, S5_Q, ns, LANES), lambda b_, c: (b_ * nc + nc - 1 - c, 0, 0, 0, 0)),
                  whole3(bre), whole3(bim), whole3(cre), whole3(cim), whole3(ar), whole3(ai),
                  pl.BlockSpec((1, width), lambda b_, c: (0, 0))],
        out_specs=[rowblk, whole3(bre), whole3(bim), whole3(cre), whole3(cim), whole3(ar), whole3(ai),
                   pl.BlockSpec((1, width), lambda b_, c: (0, 0))],
        out_shape=[jax.ShapeDtypeStruct((rows, width), BF16), jax.ShapeDtypeStruct(bre.shape, F32),
                   jax.ShapeDtypeStruct(bim.shape, F32), jax.ShapeDtypeStruct(cre.shape, F32),
                   jax.ShapeDtypeStruct(cim.shape, F32), jax.ShapeDtypeStruct(ar.shape, F32),
                   jax.ShapeDtypeStruct(ai.shape, F32), jax.ShapeDtypeStruct((1, width), F32)],
        scratch_shapes=[scr, scr, scr, scr, pltpu.VMEM((2, S5_Q, ns, LANES), F32), pltpu.VMEM((2, S5_Q, ns, LANES), F32)],
        compiler_params=_params(("arbitrary", "arbitrary")))(pa, dys, states, bre, bim, cre, cim, ar, ai, dvec)


def _s5_discretize(lam_re, lam_im, log_dt, b_re, b_im):
    dt = jnp.exp(log_dt)[:, None]
    mag = jnp.exp(lam_re * dt)
    ar, ai = mag * jnp.cos(lam_im * dt), mag * jnp.sin(lam_im * dt)
    den = lam_re * lam_re + lam_im * lam_im
    qr = ((ar - 1.0) * lam_re + ai * lam_im) / den
    qi = (ai * lam_re - (ar - 1.0) * lam_im) / den
    bbr = qr[..., None] * b_re - qi[..., None] * b_im
    bbi = qr[..., None] * b_im + qi[..., None] * b_re
    return ar, ai, bbr, bbi


def _s5_expand(ar, ai, bbr, bbi, c_re, c_im):
    g, p, h = bbr.shape
    gps = LANES // h
    ns = g // gps
    eye = jnp.eye(gps, dtype=F32)

    def bexp(b):
        return jnp.einsum("sgph,gk->sghkp", b.reshape(ns, gps, p, h), eye).reshape(ns, gps * h, gps * p)

    def cexp(c):
        return jnp.einsum("sghp,gk->sgpkh", c.reshape(ns, gps, h, p), eye).reshape(ns, gps * p, gps * h)

    def aexp(a):
        return a.reshape(ns, S5_Q, LANES).transpose(1, 0, 2)

    return (bexp(bbr).astype(BF16), bexp(bbi).astype(BF16), cexp(c_re).astype(BF16), cexp(c_im).astype(BF16),
            aexp(ar), aexp(ai))


def _s5_contract(dbre, dbim, dcre, dcim, dar, dai, g, p, h):
    gps = LANES // h
    ns = g // gps
    eye = jnp.eye(gps, dtype=F32)
    bcon = lambda d: jnp.einsum("sghkp,gk->sgph", d.reshape(ns, gps, h, gps, p), eye).reshape(g, p, h)
    ccon = lambda d: jnp.einsum("sgpkh,gk->sghp", d.reshape(ns, gps, p, gps, h), eye).reshape(g, h, p)
    acon = lambda d: d.transpose(1, 0, 2).reshape(g, p)
    return bcon(dbre), bcon(dbim), ccon(dcre), ccon(dcim), acon(dar), acon(dai)


def _ml_proj_tile(cpre, xb, wq, wk, wv, gq, gk, gv):
    xc = _silu(cpre)
    q = _dot_nn(xc, wq)
    k = _dot_nn(xc, wk)
    v = _dot_nn(xb, wv)
    return q, k, v, _dot_nn(q, gq) + _dot_nn(k, gk) + _dot_nn(v, gv)


def ml_proj_fwd(cpre, xb, wq, wk, wv, gq, gk, gv, name):
    rows, width = cpre.shape
    nblk = width // LANES
    tr = _tile(rows, 1088, 16)

    def body(c_ref, x_ref, wq_ref, wk_ref, wv_ref, gq_ref, gk_ref, gv_ref, q_ref, k_ref, v_ref, g_ref):
        j = pl.program_id(1)
        q, k, v, g = _ml_proj_tile(c_ref[...], x_ref[...], wq_ref[...], wk_ref[...], wv_ref[...],
                                   gq_ref[...], gk_ref[...], gv_ref[...])
        q_ref[...] = q
        k_ref[...] = k
        v_ref[...] = v

        @pl.when(j == 0)
        def _():
            g_ref[...] = jnp.zeros_like(g_ref)

        g_ref[...] += g

    rb = pl.BlockSpec((tr, LANES), lambda i, j: (i, j))
    wb = pl.BlockSpec((None, LANES, LANES), lambda i, j: (j, 0, 0))
    return pl.pallas_call(
        body, name=name, grid=(rows // tr, nblk), in_specs=[rb, rb, wb, wb, wb, wb, wb, wb],
        out_specs=[rb, rb, rb, pl.BlockSpec((tr, LANES), lambda i, j: (i, 0))],
        out_shape=[jax.ShapeDtypeStruct((rows, width), F32)] * 3 + [jax.ShapeDtypeStruct((rows, LANES), F32)],
        compiler_params=_params(("arbitrary", "arbitrary")))(cpre, xb, wq, wk, wv, gq, gk, gv)


def ml_proj_bwd(cpre, xb, wq, wk, wv, gq, gk, gv, dq, dk, dv, dg, dcp_extra, name):
    rows, width = cpre.shape
    nblk = width // LANES
    tr = _tile(rows, 1088, 16)

    def body(c_ref, x_ref, wq_ref, wk_ref, wv_ref, gq_ref, gk_ref, gv_ref, dq_ref, dk_ref, dv_ref, dg_ref, e_ref,
             dc_ref, dx_ref, *dw_refs):
        i = pl.program_id(1)
        _, vjp = jax.vjp(_ml_proj_tile, c_ref[...], x_ref[...], wq_ref[...], wk_ref[...], wv_ref[...],
                         gq_ref[...], gk_ref[...], gv_ref[...])
        grads = vjp((dq_ref[...], dk_ref[...], dv_ref[...], dg_ref[...]))
        dc_ref[...] = grads[0] + e_ref[...]
        dx_ref[...] = grads[1]

        @pl.when(i == 0)
        def _():
            for r in dw_refs:
                r[...] = jnp.zeros_like(r)

        for r, gval in zip(dw_refs, grads[2:]):
            r[...] += gval

    rb = pl.BlockSpec((tr, LANES), lambda j, i: (i, j))
    wb = pl.BlockSpec((None, LANES, LANES), lambda j, i: (j, 0, 0))
    gb = pl.BlockSpec((tr, LANES), lambda j, i: (i, 0))
    wshape = jax.ShapeDtypeStruct((nblk, LANES, LANES), F32)
    return pl.pallas_call(
        body, name=name, grid=(nblk, rows // tr), in_specs=[rb, rb, wb, wb, wb, wb, wb, wb, rb, rb, rb, gb, rb],
        out_specs=[rb, rb] + [wb] * 6,
        out_shape=[jax.ShapeDtypeStruct((rows, width), F32)] * 2 + [wshape] * 6,
        compiler_params=_params(("arbitrary", "arbitrary")))(cpre, xb, wq, wk, wv, gq, gk, gv, dq, dk, dv, dg, dcp_extra)


def _ml_gates_tile(gl, bg, nh):
    x = gl + bg
    bcum = _dot(_tri(CHUNK), _log_sigmoid(x), ((1,), (0,)), precision=HI)
    lane = lax.broadcasted_iota(jnp.int32, x.shape, 1)
    return jnp.where(lane < nh, x, jnp.where(lane < 2 * nh, bcum, 0.0))


def _ml_core_tile(q, k, v, colg, rowg, cpre, zb, nw, sk, cst, nst, m_prev):
    c, dh = q.shape
    igc, bc = _lane_pick(colg, 0), _lane_pick(colg, 1)
    igr, br = _row_pick(rowg, 0), _row_pick(rowg, 1)
    causal = _tri(c) > 0
    dmat = jnp.where(causal, bc - br + igr, -jnp.inf)
    inter = bc + m_prev
    mt = lax.stop_gradient(jnp.maximum(inter, jnp.max(dmat, axis=1, keepdims=True)))
    wt = jnp.exp(dmat - mt)
    w_prev = jnp.exp(inter - mt)
    qs = q * (dh ** -0.5)
    s = _dot_nt(qs, k) * wt
    num = _dot_nn(s, v) + w_prev * _dot_nn(qs, cst)
    den = jnp.sum(s, axis=1, keepdims=True) + w_prev * jnp.sum(qs * nst, axis=1, keepdims=True)
    h = num * (1.0 / jnp.maximum(jnp.abs(den), jnp.exp(-mt)))
    last = (lax.broadcasted_iota(jnp.int32, (c, 1), 0) == c - 1).astype(F32)
    blast = jnp.sum(bc * last, axis=0, keepdims=True)
    g = blast - bc + igc
    m_new = lax.stop_gradient(jnp.maximum(blast + m_prev, jnp.max(g, axis=0, keepdims=True)))
    decay = jnp.exp(blast + m_prev - m_new)
    wk = jnp.exp(g - m_new) * k
    c_new = decay * cst + _dot_tn(wk, v)
    n_new = decay * nst + jnp.sum(wk, axis=0, keepdims=True)
    mu = jnp.mean(h, axis=1, keepdims=True)
    hc = h - mu
    var = jnp.mean(hc * hc, axis=1, keepdims=True)
    out = hc * lax.rsqrt(var + HEAD_NORM_EPS) * nw + sk * _silu(cpre)
    return out * _silu(zb), c_new, n_new, m_new


def _ml_core_specs(nc, dh, rev):
    ch = (lambda c: nc - 1 - c) if rev else (lambda c: c)
    rb = pl.BlockSpec((CHUNK, dh), lambda b_, c, h: (b_ * nc + ch(c), h))
    colb = pl.BlockSpec((None, CHUNK, 2), lambda b_, c, h: (h, b_ * nc + ch(c), 0))
    rowb = pl.BlockSpec((None, None, 2, CHUNK), lambda b_, c, h: (b_ * nc + ch(c), h, 0, 0))
    pb = pl.BlockSpec((1, dh), lambda b_, c, h: (0, h))
    cb = pl.BlockSpec((None, None, dh, dh), lambda b_, c, h: (b_ * nc + ch(c), h, 0, 0))
    nb_ = pl.BlockSpec((None, None, 1, dh), lambda b_, c, h: (b_ * nc + ch(c), h, 0, 0))
    mb = pl.BlockSpec((None, None, 1, 1), lambda b_, c, h: (b_ * nc + ch(c), h, 0, 0))
    return rb, colb, rowb, pb, cb, nb_, mb


def ml_core_fwd(q, k, v, colg, rowg, cpre, zb, nw, sk, nb, nh, name):
    rows, width = q.shape
    dh = width // nh
    nc = rows // nb // CHUNK
    rb, colb, rowb, pb, cb, nb_, mb = _ml_core_specs(nc, dh, False)

    def body(q_ref, k_ref, v_ref, col_ref, row_ref, c_ref, z_ref, nw_ref, sk_ref, y_ref, cs_ref, ns_ref, ms_ref,
             cst_ref, nst_ref, mst_ref):
        c = pl.program_id(1)
        h = pl.program_id(2)

        @pl.when(c == 0)
        def _():
            cst_ref[h] = jnp.zeros((dh, dh), F32)
            nst_ref[h] = jnp.zeros((1, dh), F32)
            mst_ref[h] = jnp.zeros((1, 1), F32)

        cst, nst, m_prev = cst_ref[h], nst_ref[h], mst_ref[h]
        cs_ref[...] = cst
        ns_ref[...] = nst
        ms_ref[...] = m_prev
        y, c_new, n_new, m_new = _ml_core_tile(q_ref[...], k_ref[...], v_ref[...], col_ref[...], row_ref[...],
                                               c_ref[...], z_ref[...], nw_ref[...], sk_ref[...], cst, nst, m_prev)
        y_ref[...] = y.astype(BF16)
        cst_ref[h] = c_new
        nst_ref[h] = n_new
        mst_ref[h] = m_new

    nbc = nb * nc
    return pl.pallas_call(
        body, name=name, grid=(nb, nc, nh), in_specs=[rb, rb, rb, colb, rowb, rb, rb, pb, pb],
        out_specs=[rb, cb, nb_, mb],
        out_shape=[jax.ShapeDtypeStruct((rows, width), BF16), jax.ShapeDtypeStruct((nbc, nh, dh, dh), F32),
                   jax.ShapeDtypeStruct((nbc, nh, 1, dh), F32), jax.ShapeDtypeStruct((nbc, nh, 1, 1), F32)],
        scratch_shapes=[pltpu.VMEM((nh, dh, dh), F32), pltpu.VMEM((nh, 1, dh), F32), pltpu.VMEM((nh, 1, 1), F32)],
        compiler_params=_params(("arbitrary", "arbitrary", "arbitrary")))(q, k, v, colg, rowg, cpre, zb, nw, sk)


def ml_core_bwd(q, k, v, colg, rowg, cpre, zb, nw, sk, cs, ns, ms, dy, nb, nh, name):
    rows, width = q.shape
    dh = width // nh
    nc = rows // nb // CHUNK
    rb, colb, rowb, pb, cb, nb_, mb = _ml_core_specs(nc, dh, True)

    def body(q_ref, k_ref, v_ref, col_ref, row_ref, c_ref, z_ref, nw_ref, sk_ref, cs_ref, ns_ref, ms_ref, dy_ref,
             dq_ref, dk_ref, dv_ref, dc_ref, dz_ref, dcol_ref, drow_ref, dnw_ref, dsk_ref, dcst_ref, dnst_ref):
        bb = pl.program_id(0)
        step = pl.program_id(1)
        h = pl.program_id(2)

        @pl.when(jnp.logical_and(bb == 0, jnp.logical_and(step == 0, h == 0)))
        def _():
            dnw_ref[...] = jnp.zeros_like(dnw_ref)
            dsk_ref[...] = jnp.zeros_like(dsk_ref)

        @pl.when(step == 0)
        def _():
            dcst_ref[h] = jnp.zeros((dh, dh), F32)
            dnst_ref[h] = jnp.zeros((1, dh), F32)

        m_prev = ms_ref[...]

        def f(*a):
            return _ml_core_tile(*a, m_prev)[:3]

        _, vjp = jax.vjp(f, q_ref[...], k_ref[...], v_ref[...], col_ref[...], row_ref[...], c_ref[...], z_ref[...],
                         nw_ref[...], sk_ref[...], cs_ref[...], ns_ref[...])
        g = vjp((dy_ref[...], dcst_ref[h], dnst_ref[h]))
        dq_ref[...] = g[0]
        dk_ref[...] = g[1]
        dv_ref[...] = g[2]
        dcol_ref[...] = g[3]
        drow_ref[...] = g[4]
        dc_ref[...] = g[5]
        dz_ref[...] = g[6].astype(dz_ref.dtype)
        dnw_ref[h] += g[7]
        dsk_ref[h] += g[8]
        dcst_ref[h] = g[9]
        dnst_ref[h] = g[10]

    nbc = nb * nc
    accb = pl.BlockSpec((nh, 1, dh), lambda b_, c, h: (0, 0, 0))
    return pl.pallas_call(
        body, name=name, grid=(nb, nc, nh), in_specs=[rb, rb, rb, colb, rowb, rb, rb, pb, pb, cb, nb_, mb, rb],
        out_specs=[rb, rb, rb, rb, rb, colb, rowb, accb, accb],
        out_shape=[jax.ShapeDtypeStruct((rows, width), F32)] * 4 + [jax.ShapeDtypeStruct((rows, width), BF16)]
        + [jax.ShapeDtypeStruct(colg.shape, F32), jax.ShapeDtypeStruct(rowg.shape, F32),
           jax.ShapeDtypeStruct((nh, 1, dh), F32), jax.ShapeDtypeStruct((nh, 1, dh), F32)],
        scratch_shapes=[pltpu.VMEM((nh, dh, dh), F32), pltpu.VMEM((nh, 1, dh), F32)],
        compiler_params=_params(("arbitrary", "arbitrary", "arbitrary")))(
            q, k, v, colg, rowg, cpre, zb, nw, sk, cs, ns, ms, dy)


def _ssd_dt_tile(dtr, bias, alog):
    dt = _softplus(dtr + bias)
    cum = _dot(_tri(CHUNK), dt * (-jnp.exp(alog)), ((1,), (0,)), precision=HI)
    return dt, cum


def _ssd_tile(xcs, bmc, cmc, cols, rows_, z, dvec, gn, states, hpg):
    npair = hpg // 2
    hd = SSD_HEAD_DIM
    xs = [_silu(x) for x in xcs]
    bm, cm = _silu(bmc), _silu(cmc)
    cb = _dot_nt(cm, bm)
    causal = _tri(CHUNK) > 0
    lane_lo = lax.broadcasted_iota(jnp.int32, (1, 2 * hd), 1) < hd
    lastsel = (lax.broadcasted_iota(jnp.int32, (CHUNK, 1), 0) == CHUNK - 1).astype(F32)
    heads = []
    for r in range(hpg):
        dtc, cumc = _lane_pick(cols, r), _lane_pick(cols, hpg + r)
        dtrow, cumr = _row_pick(rows_, r), _row_pick(rows_, hpg + r)
        w = cb * jnp.exp(jnp.where(causal, cumc - cumr, -jnp.inf)) * dtrow
        last = jnp.sum(cumc * lastsel, axis=0, keepdims=True)
        heads.append((w, jnp.exp(cumc), jnp.exp(last - cumc) * dtc, jnp.exp(last)))
    ys, new_states = [], []
    for j in range(npair):
        (wa, ea, da, la), (wb, eb, db, lb) = heads[2 * j], heads[2 * j + 1]
        yi = jnp.where(lane_lo, _dot_nn(wa, xs[j]), _dot_nn(wb, xs[j]))
        ys.append(yi + jnp.where(lane_lo, ea, eb) * _dot_nn(cm, states[j]))
        xd = xs[j] * jnp.where(lane_lo, da, db)
        new_states.append(jnp.where(lane_lo, la, lb) * states[j] + _dot_tn(bm, xd))
    y = jnp.concatenate(ys, axis=1) + dvec * jnp.concatenate(xs, axis=1)
    yg = y * _silu(z)
    yn = yg * lax.rsqrt(jnp.mean(yg * yg, axis=1, keepdims=True) + NORM_EPS) * gn
    return yn, new_states


def _ssd_specs(nc, hpg, ng, rev):
    npair = hpg // 2
    gw = hpg * SSD_HEAD_DIM
    xblocks = ng * npair
    ch = (lambda c: nc - 1 - c) if rev else (lambda c: c)
    xs = [pl.BlockSpec((CHUNK, LANES), functools.partial(lambda b_, c, g, jj: (b_ * nc + ch(c), g * npair + jj), jj=j))
          for j in range(npair)]
    bmb = pl.BlockSpec((CHUNK, SSD_STATE), lambda b_, c, g: (b_ * nc + ch(c), xblocks + g))
    cmb = pl.BlockSpec((CHUNK, SSD_STATE), lambda b_, c, g: (b_ * nc + ch(c), xblocks + ng + g))
    colb = pl.BlockSpec((None, CHUNK, 2 * hpg), lambda b_, c, g: (g, b_ * nc + ch(c), 0))
    rowb = pl.BlockSpec((None, None, 2 * hpg, CHUNK), lambda b_, c, g: (b_ * nc + ch(c), g, 0, 0))
    zb = pl.BlockSpec((CHUNK, gw), lambda b_, c, g: (b_ * nc + ch(c), g))
    pb = pl.BlockSpec((1, gw), lambda b_, c, g: (0, g))
    sb = pl.BlockSpec((None, None, npair, SSD_STATE, 2 * SSD_HEAD_DIM), lambda b_, c, g: (b_ * nc + ch(c), g, 0, 0, 0))
    return xs, bmb, cmb, colb, rowb, zb, pb, sb


def ssd_core_fwd(cpre, cols, rows_, z, dvec, gn, nb, hpg, name):
    rows = cpre.shape[0]
    inner = z.shape[1]
    ng = inner // (hpg * SSD_HEAD_DIM)
    npair = hpg // 2
    nc = rows // nb // CHUNK
    xs, bmb, cmb, colb, rowb, zb, pb, sb = _ssd_specs(nc, hpg, ng, False)

    def body(*refs):
        x_refs = refs[:npair]
        bm_ref, cm_ref, col_ref, row_ref, z_ref, d_ref, gn_ref, y_ref, so_ref, st_ref = refs[npair:]
        c = pl.program_id(1)
        g = pl.program_id(2)

        @pl.when(c == 0)
        def _():
            st_ref[g] = jnp.zeros((npair, SSD_STATE, 2 * SSD_HEAD_DIM), F32)

        so_ref[...] = st_ref[g]
        states = [st_ref[g, j] for j in range(npair)]
        yn, new_states = _ssd_tile([r[...] for r in x_refs], bm_ref[...], cm_ref[...], col_ref[...], row_ref[...],
                                   z_ref[...], d_ref[...], gn_ref[...], states, hpg)
        y_ref[...] = yn.astype(BF16)
        for j in range(npair):
            st_ref[g, j] = new_states[j]

    return pl.pallas_call(
        body, name=name, grid=(nb, nc, ng), in_specs=xs + [bmb, cmb, colb, rowb, zb, pb, pb],
        out_specs=[zb, sb],
        out_shape=[jax.ShapeDtypeStruct((rows, inner), BF16),
                   jax.ShapeDtypeStruct((nb * nc, ng, npair, SSD_STATE, 2 * SSD_HEAD_DIM), F32)],
        scratch_shapes=[pltpu.VMEM((ng, npair, SSD_STATE, 2 * SSD_HEAD_DIM), F32)],
        compiler_params=_params(("arbitrary", "arbitrary", "arbitrary")))(
            *([cpre] * npair), cpre, cpre, cols, rows_, z, dvec, gn)


def ssd_core_bwd(cpre, cols, rows_, z, dvec, gn, states, dyn, nb, hpg, name):
    rows = cpre.shape[0]
    inner = z.shape[1]
    gw = hpg * SSD_HEAD_DIM
    ng = inner // gw
    npair = hpg // 2
    nc = rows // nb // CHUNK
    xs, bmb, cmb, colb, rowb, zb, pb, sb = _ssd_specs(nc, hpg, ng, True)

    def body(*refs):
        x_refs = refs[:npair]
        (bm_ref, cm_ref, col_ref, row_ref, z_ref, d_ref, gn_ref, s_ref, dy_ref,
         dx_ref, dbm_ref, dcm_ref, dcol_ref, drow_ref, dz_ref, dd_ref, dgn_ref, dst_ref) = refs[npair:]
        bb = pl.program_id(0)
        step = pl.program_id(1)
        g = pl.program_id(2)

        @pl.when(jnp.logical_and(bb == 0, jnp.logical_and(step == 0, g == 0)))
        def _():
            dd_ref[...] = jnp.zeros_like(dd_ref)
            dgn_ref[...] = jnp.zeros_like(dgn_ref)

        @pl.when(step == 0)
        def _():
            dst_ref[g] = jnp.zeros((npair, SSD_STATE, 2 * SSD_HEAD_DIM), F32)

        def f(xcs, bmc, cmc, cv, rv, zv, dv_, gv, sts):
            return _ssd_tile(xcs, bmc, cmc, cv, rv, zv, dv_, gv, sts, hpg)

        _, vjp = jax.vjp(f, [r[...] for r in x_refs], bm_ref[...], cm_ref[...], col_ref[...], row_ref[...], z_ref[...],
                         d_ref[...], gn_ref[...], [s_ref[j] for j in range(npair)])
        gr = vjp((dy_ref[...], [dst_ref[g, j] for j in range(npair)]))
        dx_ref[...] = jnp.concatenate(gr[0], axis=1)
        dbm_ref[...] = gr[1]
        dcm_ref[...] = gr[2]
        dcol_ref[...] = gr[3]
        drow_ref[...] = gr[4]
        dz_ref[...] = gr[5].astype(dz_ref.dtype)
        dd_ref[g] += gr[6]
        dgn_ref[g] += gr[7]
        for j in range(npair):
            dst_ref[g, j] = gr[8][j]

    ch = lambda c: nc - 1 - c
    nblk = pl.BlockSpec((CHUNK, SSD_STATE), lambda b_, c, g: (b_ * nc + ch(c), g))
    accb = pl.BlockSpec((ng, 1, gw), lambda b_, c, g: (0, 0, 0))
    return pl.pallas_call(
        body, name=name, grid=(nb, nc, ng), in_specs=xs + [bmb, cmb, colb, rowb, zb, pb, pb, sb, zb],
        out_specs=[zb, nblk, nblk, colb, rowb, zb, accb, accb],
        out_shape=[jax.ShapeDtypeStruct((rows, inner), F32), jax.ShapeDtypeStruct((rows, ng * SSD_STATE), F32),
                   jax.ShapeDtypeStruct((rows, ng * SSD_STATE), F32), jax.ShapeDtypeStruct(cols.shape, F32),
                   jax.ShapeDtypeStruct(rows_.shape, F32), jax.ShapeDtypeStruct((rows, inner), BF16),
                   jax.ShapeDtypeStruct((ng, 1, gw), F32), jax.ShapeDtypeStruct((ng, 1, gw), F32)],
        scratch_shapes=[pltpu.VMEM((ng, npair, SSD_STATE, 2 * SSD_HEAD_DIM), F32)],
        compiler_params=_params(("arbitrary", "arbitrary", "arbitrary")))(
            *([cpre] * npair), cpre, cpre, cols, rows_, z, dvec, gn, states, dyn)


def _hw_expand(w):
    n, bi, _ = w.shape
    per = LANES // bi
    eye = jnp.eye(per, dtype=F32)
    return jnp.einsum("jbio,bc->jbico", w.reshape(n // per, per, bi, bi), eye).reshape(n // per, LANES, LANES)


def _hw_contract(d, bi=QKV_BLOCK):
    per = LANES // bi
    eye = jnp.eye(per, dtype=F32)
    return jnp.einsum("jbico,bc->jbio", d.reshape(d.shape[0], per, bi, per, bi), eye).reshape(-1, bi, bi)


def _wg_expand(wg, width):
    pad = jnp.pad(wg, ((0, 0), (0, LANES - wg.shape[1])))
    return [pad[i * width:(i + 1) * width].reshape(width // LANES, LANES, LANES) for i in range(3)]


def _wg_contract(dgs, ngate):
    return jnp.concatenate([d[:, :, :ngate].reshape(-1, ngate) for d in dgs], axis=0)


def _pad_lanes(a):
    return jnp.pad(a, ((0, 0), (0, LANES - a.shape[1])))


def _pairs_to_layouts(first, second, ngrp, per, nbc):
    rows = first.shape[0]
    both = jnp.concatenate([first.reshape(rows, ngrp, per), second.reshape(rows, ngrp, per)], axis=2)
    return both.transpose(1, 0, 2), both.reshape(nbc, CHUNK, ngrp, 2 * per).transpose(0, 2, 3, 1)


def _layouts_to_pairs(dcols, drows, ngrp, per):
    rows = dcols.shape[1]
    both = dcols.transpose(1, 0, 2) + drows.transpose(0, 3, 1, 2).reshape(rows, ngrp, 2 * per)
    return both[:, :, :per].reshape(rows, ngrp * per), both[:, :, per:].reshape(rows, ngrp * per)


_LATE = ("Wo0a", "Wo0b", "W1z", "W1x", "W1dt", "Wo1")


def _local_step(x, target, bw, sp, late_weights=None, late_grads=None, start_token=None):
    nb, seq, d = x.shape
    nh, hpg = MLSTM_HEADS, SSD_HPG
    t_len = N_META + seq
    nc = -(-t_len // CHUNK)
    tp = nc * CHUNK
    rows = nb * tp
    nbc = nb * nc
    meta = sp["meta_tokens"]
    h0 = jnp.concatenate([jnp.broadcast_to(meta[None], (nb, N_META, d)), x, jnp.zeros((nb, tp - t_len, d), F32)], axis=1)
    h0 = h0.reshape(rows, d)
    if start_token is not None:
        h0 = h0 + start_token
    tgt = jnp.pad(target, ((0, 0), (N_META, tp - t_len), (0, 0))).reshape(rows, d)

    n0 = norm_fwd(h0, sp["ab_norm"], "norm0")
    pa = mm(n0, bw["W0a"], "nn", "mm_pa")
    xb = mm(n0, bw["W0xb"], "nn", "mm_xb")
    zb = mm(n0, bw["W0zb"], "nn", "mm_zb")
    s5w = pa.shape[1] // 2
    mlw = xb.shape[1]
    s5_args = (sp["s5_lambda_re"], sp["s5_lambda_im"], sp["s5_log_dt"].reshape(-1), sp["s5_b_re"], sp["s5_b_im"])
    (ar, ai, bbr, bbi), s5_disc_vjp = jax.vjp(_s5_discretize, *s5_args)
    sg, spn, shh = bbr.shape
    bre, bim, cre, cim, are, aie = _s5_expand(ar, ai, bbr, bbi, sp["s5_c_re"], sp["s5_c_im"])
    ys5, gb, s5st = s5_fwd(pa, bre, bim, cre, cim, are, aie, sp["s5_d"], nb, "s5_fwd")
    tglu = mm(gb, bw["glu"], "nn", "mm_glu")

    def glu_tile(ys, tt, za, gbias):
        return _gelu(ys) * _sigmoid(tt + gbias) * _silu(za)

    ya = rowwise("glu_fwd", lambda i, ys, tt, pab, gbias: glu_tile(ys, tt, pab[:, s5w:], gbias),
                 [ys5, tglu, pa], [sp["s5_glu_b"]], [(s5w, BF16)], tr=_tile(rows, 256, 16))[0]

    cpre0 = conv_fwd(xb, sp["ml_conv_w"], sp["ml_conv_b"], nb, "ml_conv_fwd")
    wq_e, wk_e, wv_e = _hw_expand(sp["ml_wq"]), _hw_expand(sp["ml_wk"]), _hw_expand(sp["ml_wv"])
    gq, gk, gv = _wg_expand(sp["ml_w_gate"], mlw)
    q, k, v, gl = ml_proj_fwd(cpre0, xb, wq_e, wk_e, wv_e, gq, gk, gv, "ml_proj_fwd")
    bgate = _pad_lanes(sp["ml_b_gate"])
    gout = rowwise("ml_gates_fwd", lambda i, g_, b_: _ml_gates_tile(g_, b_, nh), [gl], [bgate], [(LANES, F32)], tr=CHUNK)[0]
    colg, rowg = _pairs_to_layouts(gout[:, :nh], gout[:, nh:2 * nh], nh, 1, nbc)
    yb, ml_cs, ml_ns, ml_ms = ml_core_fwd(q, k, v, colg, rowg, cpre0, zb, sp["ml_norm"], sp["ml_skip"], nb, nh, "ml_core_fwd")
    if late_weights is not None:
        bw = {**bw, **late_weights((ya, yb))}
    h1 = mm(ya, bw["Wo0a"], "nn", "mm_out0a", resid=h0)
    h1 = mm(yb, bw["Wo0b"], "nn", "mm_out0b", resid=h1)

    n1 = norm_fwd(h1, sp["ssd_norm"], "norm1")
    z1 = mm(n1, bw["W1z"], "nn", "mm_z1")
    xbc = mm(n1, bw["W1x"], "nn", "mm_xbc")
    dtr = mm(n1, bw["W1dt"], "nn", "mm_dt")
    inner = z1.shape[1]
    ng = inner // (hpg * SSD_HEAD_DIM)
    nhd = ng * hpg
    cpre1 = conv_fwd(xbc, sp["ssd_conv_w"], sp["ssd_conv_b"], nb, "ssd_conv_fwd")
    dt_bias, a_log = _pad_lanes(sp["ssd_dt_bias"]), _pad_lanes(sp["ssd_a_log"])
    dt, cum = rowwise("ssd_dt_fwd", lambda i, r_, b_, a_: _ssd_dt_tile(r_, b_, a_), [dtr], [dt_bias, a_log],
                      [(LANES, F32), (LANES, F32)], tr=CHUNK)
    cols, rws = _pairs_to_layouts(dt[:, :nhd], cum[:, :nhd], ng, hpg, nbc)
    dvec = jnp.repeat(sp["ssd_d"], SSD_HEAD_DIM, axis=1)
    yn, ssd_st = ssd_core_fwd(cpre1, cols, rws, z1, dvec, sp["ssd_gnorm"], nb, hpg, "ssd_core_fwd")
    h2 = mm(yn, bw["Wo1"], "nn", "mm_out1", resid=h1)

    tr_l = _tile(tp, 256, 16)
    per_ex = tp // tr_l

    def loss_tile(i, hb, tb, gfn):
        tpos = (i % per_ex) * tr_l + lax.broadcasted_iota(jnp.int32, (tr_l, 1), 0)
        mask = jnp.logical_and(tpos >= N_META, tpos < t_len).astype(F32)

        def lf(hh, gg):
            e = (_rms(hh, gg) - tb) * mask
            return 0.5 * jnp.sum(e * e) / d

        lval, (dh, dg) = jax.value_and_grad(lf, (0, 1))(hb, gfn)
        return dh, dh, jnp.full((1, LANES), lval, F32), dg

    fn = sp["final_norm"].reshape(1, d)
    dh2, dh2b, loss_acc, dfn = rowwise("loss", loss_tile, [h2, tgt], [fn], [(d, F32), (d, BF16)], [(1, LANES), (1, d)], tr=tr_l)

    gbig, gs = {}, {}
    gs["final_norm"] = dfn.reshape(sp["final_norm"].shape)
    dyn = mm(dh2b, bw["Wo1"], "nt", "mm_dyn")
    gbig["Wo1"] = mm(yn, dh2b, "tn", "mm_dWo1", out_dtype=BF16)
    dxs, dbm, dcm, dcols, drws, dz1, ddvec, dgn = ssd_core_bwd(cpre1, cols, rws, z1, dvec, sp["ssd_gnorm"], ssd_st, dyn,
                                                              nb, hpg, "ssd_core_bwd")
    gs["ssd_d"] = ddvec.reshape(1, nhd, SSD_HEAD_DIM).sum(axis=2)
    gs["ssd_gnorm"] = dgn.reshape(1, inner)
    ddt, dcum = _layouts_to_pairs(dcols, drws, ng, hpg)

    def ssd_dt_bwd_tile(i, r_, ddt_, dcum_, b_, a_):
        _, vjp = jax.vjp(_ssd_dt_tile, r_, b_, a_)
        return vjp((ddt_, dcum_))

    ddtr, dbias, dalog = rowwise("ssd_dt_bwd", ssd_dt_bwd_tile, [dtr, _pad_lanes(ddt), _pad_lanes(dcum)], [dt_bias, a_log],
                                 [(LANES, BF16)], [(1, LANES), (1, LANES)], tr=CHUNK)
    gs["ssd_dt_bias"] = dbias[:, :nhd]
    gs["ssd_a_log"] = dalog[:, :nhd]
    dcpre1 = jnp.concatenate([dxs, dbm, dcm], axis=1)
    dxbc, dcw1, dcb1 = conv_bwd(dcpre1, xbc, sp["ssd_conv_w"], nb, "ssd_conv_bwd")
    gs["ssd_conv_w"] = dcw1
    gs["ssd_conv_b"] = dcb1
    dn1 = mm(dz1, bw["W1z"], "nt", "mm_dn1z")
    dn1 = mm(dxbc, bw["W1x"], "nt", "mm_dn1x", resid=dn1)
    dn1 = mm(ddtr, bw["W1dt"], "nt", "mm_dn1dt", resid=dn1)
    gbig["W1z"] = mm(n1, dz1, "tn", "mm_dW1z", out_dtype=BF16)
    gbig["W1x"] = mm(n1, dxbc, "tn", "mm_dW1x", out_dtype=BF16)
    gbig["W1dt"] = mm(n1, ddtr, "tn", "mm_dW1dt", out_dtype=BF16)
    dh1, dh1b, dg1 = norm_bwd(h1, sp["ssd_norm"], dn1, dh2, "norm1_bwd")
    gs["ssd_norm"] = dg1

    gbig["Wo0a"] = mm(ya, dh1b, "tn", "mm_dWo0a", out_dtype=BF16)
    gbig["Wo0b"] = mm(yb, dh1b, "tn", "mm_dWo0b", out_dtype=BF16)
    wo0a, wo0b = bw["Wo0a"], bw["Wo0b"]
    if late_grads is not None:
        late_grads({n: gbig[n] for n in _LATE})
    dya = mm(dh1b, wo0a, "nt", "mm_dya")
    dyb = mm(dh1b, wo0b, "nt", "mm_dyb")
    (dq, dk, dv, dcp_skip, dzb, dcolg, drowg, dnw, dsk) = ml_core_bwd(
        q, k, v, colg, rowg, cpre0, zb, sp["ml_norm"], sp["ml_skip"], ml_cs, ml_ns, ml_ms, dyb, nb, nh, "ml_core_bwd")
    gs["ml_norm"] = dnw.reshape(1, mlw)
    gs["ml_skip"] = dsk.reshape(1, mlw)
    dig, dbcum = _layouts_to_pairs(dcolg, drowg, nh, 1)
    dgout = _pad_lanes(jnp.concatenate([dig, dbcum], axis=1))

    def ml_gates_bwd_tile(i, g_, dgo, b_):
        _, vjp = jax.vjp(lambda a, b: _ml_gates_tile(a, b, nh), g_, b_)
        return vjp(dgo)

    dgl, dbg = rowwise("ml_gates_bwd", ml_gates_bwd_tile, [gl, dgout], [bgate], [(LANES, F32)], [(1, LANES)], tr=CHUNK)
    gs["ml_b_gate"] = dbg[:, :2 * nh]
    dcpre0, dxb_v, dwq, dwk, dwv, dgq, dgk, dgv = ml_proj_bwd(cpre0, xb, wq_e, wk_e, wv_e, gq, gk, gv, dq, dk, dv, dgl,
                                                            dcp_skip, "ml_proj_bwd")
    gs["ml_wq"], gs["ml_wk"], gs["ml_wv"] = _hw_contract(dwq), _hw_contract(dwk), _hw_contract(dwv)
    gs["ml_w_gate"] = _wg_contract([dgq, dgk, dgv], 2 * nh)
    dxb, dcw0, dcb0 = conv_bwd(dcpre0, xb, sp["ml_conv_w"], nb, "ml_conv_bwd", resid=dxb_v)
    gs["ml_conv_w"] = dcw0
    gs["ml_conv_b"] = dcb0

    def glu_bwd_tile(i, ys, tt, pab, dy_, gbias):
        _, vjp = jax.vjp(glu_tile, ys, tt, pab[:, s5w:], gbias)
        return vjp(dy_)

    dys_direct, dtglu, dza, dglub = rowwise("glu_bwd", glu_bwd_tile, [ys5, tglu, pa, dya], [sp["s5_glu_b"]],
                                            [(s5w, F32), (s5w, BF16), (s5w, BF16)], [(1, s5w)], tr=_tile(rows, 256, 16))
    gs["s5_glu_b"] = dglub
    dgb = mm(dtglu, bw["glu"], "nt", "mm_dgb")
    gbig["glu"] = mm(gb, dtglu, "tn", "mm_dglu", out_dtype=BF16)

    def gelu_bwd_tile(i, ys, dg_, direct):
        _, vjp = jax.vjp(_gelu, ys)
        return vjp(dg_)[0] + direct

    dys5 = rowwise("gelu_bwd", gelu_bwd_tile, [ys5, dgb, dys_direct], [], [(s5w, F32)], tr=_tile(rows, 256, 16))[0]
    du, dbre, dbim, dcre, dcim, dare, daie, dd5 = s5_bwd(pa, dys5, s5st, bre, bim, cre, cim, are, aie, sp["s5_d"], nb, "s5_bwd")
    gs["s5_d"] = dd5
    dbbr, dbbi, dcr, dci, dar, dai = _s5_contract(dbre, dbim, dcre, dcim, dare, daie, sg, spn, shh)
    gs["s5_c_re"], gs["s5_c_im"] = dcr, dci
    (gs["s5_lambda_re"], gs["s5_lambda_im"], dlogdt, gs["s5_b_re"], gs["s5_b_im"]) = s5_disc_vjp((dar, dai, dbbr, dbbi))
    gs["s5_log_dt"] = dlogdt.reshape(1, -1)
    dpa = jnp.concatenate([du, dza], axis=1)
    dn0 = mm(dpa, bw["W0a"], "nt", "mm_dn0a")
    dn0 = mm(dxb, bw["W0xb"], "nt", "mm_dn0xb", resid=dn0)
    dn0 = mm(dzb, bw["W0zb"], "nt", "mm_dn0zb", resid=dn0)
    gbig["W0a"] = mm(n0, dpa, "tn", "mm_dW0a", out_dtype=BF16)
    gbig["W0xb"] = mm(n0, dxb, "tn", "mm_dW0xb", out_dtype=BF16)
    gbig["W0zb"] = mm(n0, dzb, "tn", "mm_dW0zb", out_dtype=BF16)
    dh0, _, dg0 = norm_bwd(h0, sp["ab_norm"], dn0, dh1, "norm0_bwd")
    gs["ab_norm"] = dg0
    dh0 = dh0.reshape(nb, tp, d)
    gs["meta_tokens"] = jnp.sum(dh0[:, :N_META], axis=0)
    return loss_acc[0, 0], dh0, gbig, gs


N_DEV = 8
N_CHIP = 4
MESH = pl.DeviceIdType.MESH
_HBM = pl.BlockSpec(memory_space=pltpu.HBM)


def _place():
    x, y, c = lax.axis_index("x"), lax.axis_index("y"), lax.axis_index("c")
    return x, y, c, [(1 - x, y), (x, 1 - y), (1 - x, 1 - y)]


def all_gather8(v, name):
    m_per, n = v.shape

    def body(x_ref, out_ref, send_sems, recv_sems, local_sem):
        x, y, c, chips = _place()
        me, sibling = (x, y, c), (x, y, 1 - c)

        def rows(px, py, pc):
            return out_ref.at[pl.ds((4 * px + 2 * py + pc) * m_per, m_per), :]

        def copy(kk, block, to, src=None):
            return pltpu.make_async_remote_copy(
                src_ref=rows(*block) if src is None else src, dst_ref=rows(*block), send_sem=send_sems.at[kk],
                recv_sem=recv_sems.at[kk], device_id=to, device_id_type=MESH)

        mine = pltpu.make_async_copy(x_ref, rows(*me), local_sem)
        mine.start()
        first = [copy(0, me, sibling, src=x_ref)]
        first += [copy(1 + j, me, (*chip, c), src=x_ref) for j, chip in enumerate(chips)]
        for cp in first:
            cp.start()
        passed = [copy(4 + j, (*chip, c), sibling) for j, chip in enumerate(chips)]
        for j, chip in enumerate(chips):
            copy(1 + j, (*chip, c), me).wait_recv()
            passed[j].start()
        copy(0, sibling, me).wait_recv()
        for j, chip in enumerate(chips):
            copy(4 + j, (*chip, 1 - c), me).wait_recv()
        for cp in first + passed:
            cp.wait_send()
        mine.wait()

    return pl.pallas_call(
        body, name=name, out_shape=jax.ShapeDtypeStruct((N_DEV * m_per, n), v.dtype),
        in_specs=[pl.BlockSpec(memory_space=pltpu.VMEM)], out_specs=pl.BlockSpec(memory_space=pltpu.VMEM),
        scratch_shapes=[pltpu.SemaphoreType.DMA((7,)), pltpu.SemaphoreType.DMA((7,)), pltpu.SemaphoreType.DMA],
        compiler_params=pltpu.CompilerParams(vmem_limit_bytes=VMEM_LIMIT))(v)


def gather_chips(vs, name):
    na = len(vs)

    def body(*refs):
        x_refs, out_refs = refs[:na], refs[na:2 * na]
        send_sems, recv_sems, local_sems = refs[2 * na:]
        x, y, c, chips = _place()
        k = 2 * x + y
        sibling = (x, y, 1 - c)

        def copy(i, kk, src, chip_k, half, to):
            return pltpu.make_async_remote_copy(
                src_ref=src, dst_ref=out_refs[i].at[chip_k, half], send_sem=send_sems.at[6 * i + kk],
                recv_sem=recv_sems.at[6 * i + kk], device_id=to, device_id_type=MESH)

        mine = [pltpu.make_async_copy(x_refs[i], out_refs[i].at[k], local_sems.at[i]) for i in range(na)]
        for cp in mine:
            cp.start()
        first = [copy(i, j, x_refs[i].at[c], k, c, (*chip, c)) for j, chip in enumerate(chips) for i in range(na)]
        for cp in first:
            cp.start()
        passed = []
        for j, (cx, cy) in enumerate(chips):
            kj = 2 * cx + cy
            for i in range(na):
                copy(i, j, out_refs[i].at[kj, c], kj, c, (cx, cy, c)).wait_recv()
                fwd = copy(i, 3 + j, out_refs[i].at[kj, c], kj, c, sibling)
                fwd.start()
                passed.append(fwd)
        for j, (cx, cy) in enumerate(chips):
            kj = 2 * cx + cy
            for i in range(na):
                copy(i, 3 + j, out_refs[i].at[kj, 1 - c], kj, 1 - c, sibling).wait_recv()
        for cp in first + passed:
            cp.wait_send()
        for cp in mine:
            cp.wait()

    return pl.pallas_call(
        body, name=name, out_shape=[jax.ShapeDtypeStruct((N_CHIP,) + v.shape, v.dtype) for v in vs],
        in_specs=[_HBM] * na, out_specs=[_HBM] * na,
        scratch_shapes=[pltpu.SemaphoreType.DMA((6 * na,)), pltpu.SemaphoreType.DMA((6 * na,)),
                        pltpu.SemaphoreType.DMA((na,))])(*vs)


def scatter_chips(ps, name):
    na = len(ps)

    def body(*refs):
        p_refs, out_refs = refs[:na], refs[na:2 * na]
        send_sems, recv_sems, local_sems = refs[2 * na:]
        x, y, c, chips = _place()
        k = 2 * x + y
        sibling = (x, y, 1 - c)

        def copy(i, kk, src, chip_k, half, to):
            return pltpu.make_async_remote_copy(
                src_ref=src, dst_ref=out_refs[i].at[chip_k, half], send_sem=send_sems.at[7 * i + kk],
                recv_sem=recv_sems.at[7 * i + kk], device_id=to, device_id_type=MESH)

        mine = [pltpu.make_async_copy(p_refs[i].at[k], out_refs[i].at[k, c], local_sems.at[i]) for i in range(na)]
        for cp in mine:
            cp.start()
        first = [copy(i, 1 + j, p_refs[i].at[2 * cx + cy], k, c, (cx, cy, c))
                 for j, (cx, cy) in enumerate(chips) for i in range(na)]
        first += [copy(i, 0, p_refs[i].at[k], k, c, sibling) for i in range(na)]
        for cp in first:
            cp.start()
        passed = []
        for j, (cx, cy) in enumerate(chips):
            kj = 2 * cx + cy
            for i in range(na):
                copy(i, 1 + j, out_refs[i].at[kj, c], kj, c, (cx, cy, c)).wait_recv()
                fwd = copy(i, 4 + j, out_refs[i].at[kj, c], kj, c, sibling)
                fwd.start()
                passed.append(fwd)
        for i in range(na):
            copy(i, 0, out_refs[i].at[k, 1 - c], k, 1 - c, sibling).wait_recv()
        for j, (cx, cy) in enumerate(chips):
            kj = 2 * cx + cy
            for i in range(na):
                copy(i, 4 + j, out_refs[i].at[kj, 1 - c], kj, 1 - c, sibling).wait_recv()
        for cp in first + passed:
            cp.wait_send()
        for cp in mine:
            cp.wait()

    return pl.pallas_call(
        body, name=name, out_shape=[jax.ShapeDtypeStruct((N_CHIP, 2) + p.shape[1:], p.dtype) for p in ps],
        in_specs=[_HBM] * na, out_specs=[_HBM] * na,
        scratch_shapes=[pltpu.SemaphoreType.DMA((7 * na,)), pltpu.SemaphoreType.DMA((7 * na,)),
                        pltpu.SemaphoreType.DMA((na,))])(*ps)


def swap_halves(gs_, name):
    na = len(gs_)

    def body(*refs):
        g_refs, out_refs = refs[:na], refs[na:2 * na]
        send_sems, recv_sems = refs[2 * na:]
        x, y, c, _ = _place()
        cps = [pltpu.make_async_remote_copy(
            src_ref=g_refs[i].at[kk, 1 - c], dst_ref=out_refs[i].at[kk], send_sem=send_sems.at[N_CHIP * i + kk],
            recv_sem=recv_sems.at[N_CHIP * i + kk], device_id=(x, y, 1 - c), device_id_type=MESH)
            for i in range(na) for kk in range(N_CHIP)]
        for cp in cps:
            cp.start()
        for cp in cps:
            cp.wait()

    return pl.pallas_call(
        body, name=name, out_shape=[jax.ShapeDtypeStruct((N_CHIP,) + g.shape[2:], g.dtype) for g in gs_],
        in_specs=[_HBM] * na, out_specs=[_HBM] * na,
        scratch_shapes=[pltpu.SemaphoreType.DMA((N_CHIP * na,)), pltpu.SemaphoreType.DMA((N_CHIP * na,))])(*gs_)


def add_halves(g, other, core, name):
    _, _, m, n = g.shape
    tr = _tile(m, 256, 16)

    def body(core_ref, g_ref, o_ref, out_ref):
        out_ref[...] = (g_ref[...].astype(F32) + o_ref[...].astype(F32)).astype(out_ref.dtype)

    grid_spec = pltpu.PrefetchScalarGridSpec(
        num_scalar_prefetch=1, grid=(N_CHIP, m // tr),
        in_specs=[pl.BlockSpec((None, None, tr, n), lambda kk, i, core_ref: (kk, core_ref[0], i, 0)),
                  pl.BlockSpec((None, tr, n), lambda kk, i, core_ref: (kk, i, 0))],
        out_specs=pl.BlockSpec((None, tr, n), lambda kk, i, core_ref: (kk, i, 0)))
    return pl.pallas_call(body, name=name, grid_spec=grid_spec, out_shape=jax.ShapeDtypeStruct((N_CHIP, m, n), g.dtype),
                          compiler_params=_params(("arbitrary", "arbitrary")))(core.reshape(1).astype(jnp.int32), g, other)


def sequencer_exchange(srcs, scatter, collective_id, name):
    from jax.experimental.pallas import tpu_sc as plsc
    na = len(srcs)
    per = 2 * N_PEER_CHIPS + (1 if scatter else 0)
    hbm = pltpu.MemorySpace.HBM
    src_refs = [jax.new_ref(a, memory_space=hbm) for a in srcs]
    out_refs = [jax.empty_ref(jax.ShapeDtypeStruct((N_CHIP, 2) + a.shape[1:], a.dtype), memory_space=hbm) for a in srcs]

    @pl.kernel(mesh=plsc.ScalarSubcoreMesh(axis_name="seq", num_cores=1), name=name,
               scratch_types=(pltpu.SemaphoreType.DMA((per * na,)), pltpu.SemaphoreType.DMA((per * na,)),
                              pltpu.SemaphoreType.DMA((na,))),
               compiler_params=pltpu.CompilerParams(collective_id=collective_id))
    def launch(send_sems, recv_sems, local_sems):
        x, y, c, chips = _place()
        k = 2 * x + y
        sibling = (x, y, 1 - c)
        barrier = pltpu.get_barrier_semaphore()
        for cx, cy in chips:
            pl.semaphore_signal(barrier, inc=1, device_id=(cx, cy, c), device_id_type=MESH)
        pl.semaphore_signal(barrier, inc=1, device_id=sibling, device_id_type=MESH)
        pl.semaphore_wait(barrier, N_CHIP)

        def copy(i, kk, src, chip_k, half, to):
            return pltpu.make_async_remote_copy(
                src_ref=src, dst_ref=out_refs[i].at[chip_k, half], send_sem=send_sems.at[per * i + kk],
                recv_sem=recv_sems.at[per * i + kk], device_id=to, device_id_type=MESH)

        if scatter:
            mine = [pltpu.make_async_copy(src_refs[i].at[k], out_refs[i].at[k, c], local_sems.at[i]) for i in range(na)]
        else:
            mine = [pltpu.make_async_copy(src_refs[i], out_refs[i].at[k], local_sems.at[i]) for i in range(na)]
        for cp in mine:
            cp.start()
        first = []
        for j, (cx, cy) in enumerate(chips):
            for i in range(na):
                src = src_refs[i].at[2 * cx + cy] if scatter else src_refs[i].at[c]
                first.append(copy(i, j, src, k, c, (cx, cy, c)))
        if scatter:
            first += [copy(i, 2 * N_PEER_CHIPS, src_refs[i].at[k], k, c, sibling) for i in range(na)]
        for cp in first:
            cp.start()
        passed = []
        for j, (cx, cy) in enumerate(chips):
            kj = 2 * cx + cy
            for i in range(na):
                copy(i, j, out_refs[i].at[kj, c], kj, c, (cx, cy, c)).wait_recv()
                fwd = copy(i, N_PEER_CHIPS + j, out_refs[i].at[kj, c], kj, c, sibling)
                fwd.start()
                passed.append(fwd)
        if scatter:
            for i in range(na):
                copy(i, 2 * N_PEER_CHIPS, out_refs[i].at[k, 1 - c], k, 1 - c, sibling).wait_recv()
        for j, (cx, cy) in enumerate(chips):
            kj = 2 * cx + cy
            for i in range(na):
                copy(i, N_PEER_CHIPS + j, out_refs[i].at[kj, 1 - c], kj, 1 - c, sibling).wait_recv()
        for cp in first + passed:
            cp.wait_send()
        for cp in mine:
            cp.wait()

    launch()
    return [r[...] for r in out_refs]


_SEM = pl.BlockSpec(memory_space=pltpu.SEMAPHORE)
_EFFECT = pltpu.SideEffectType.DATAFLOW_SIDE_EFFECTING
N_PEER_CHIPS = N_CHIP - 1


def _ici_copy(scatter, src_refs, land_refs, send_sems, recv_sems, i, j, chips, k, c, landed):
    cx, cy = chips[j]
    kj = 2 * cx + cy
    src = src_refs[i].at[kj] if scatter else src_refs[i].at[c]
    dst = land_refs[i].at[kj, c] if landed else land_refs[i].at[k, c]
    return pltpu.make_async_remote_copy(src_ref=src, dst_ref=dst, send_sem=send_sems.at[N_PEER_CHIPS * i + j],
                                        recv_sem=recv_sems.at[N_PEER_CHIPS * i + j], device_id=(cx, cy, c),
                                        device_id_type=MESH)


def ici_start(srcs, lands, scatter, collective_id, name):
    na = len(srcs)

    def body(*refs):
        src_refs, land_refs = refs[:na], refs[na:2 * na]
        send_sems, recv_sems = refs[2 * na], refs[2 * na + 1]
        token = refs[-1]
        x, y, c, chips = _place()
        barrier = pltpu.get_barrier_semaphore()
        for cx, cy in chips:
            pl.semaphore_signal(barrier, inc=1, device_id=(cx, cy, c), device_id_type=MESH)
        pl.semaphore_wait(barrier, N_PEER_CHIPS)
        for j in range(N_PEER_CHIPS):
            for i in range(na):
                _ici_copy(scatter, src_refs, land_refs, send_sems, recv_sems, i, j, chips, 2 * x + y, c, False).start()
        token[...] = jnp.zeros_like(token)

    nsem = N_PEER_CHIPS * na
    hbm = lambda a: pltpu.HBM(a.shape, a.dtype)
    res = pl.pallas_call(
        body, name=name,
        out_shape=(pltpu.SemaphoreType.DMA((nsem,)), pltpu.SemaphoreType.DMA((nsem,)), *[hbm(a) for a in srcs],
                   *[hbm(a) for a in lands], jax.ShapeDtypeStruct((SUBLANES, LANES), F32)),
        in_specs=[_HBM] * (2 * na),
        out_specs=(_SEM, _SEM, *([_HBM] * (2 * na)), pl.BlockSpec(memory_space=pltpu.VMEM)),
        input_output_aliases={i: 2 + i for i in range(2 * na)},
        compiler_params=pltpu.CompilerParams(has_side_effects=_EFFECT, collective_id=collective_id))(
            *[pltpu.with_memory_space_constraint(a, pltpu.HBM) for a in srcs],
            *[pltpu.with_memory_space_constraint(a, pltpu.HBM) for a in lands])
    return res[0], res[1], list(res[2:2 + na]), list(res[2 + na:2 + 2 * na]), res[-1]


def ici_wait(send_sems, recv_sems, srcs, lands, after, scatter, name):
    na = len(srcs)
    after = list(after)

    def body(*refs):
        src_refs, land_refs = refs[:na], refs[na:2 * na]
        s_sems, r_sems = refs[2 * na], refs[2 * na + 1]
        x, y, c, chips = _place()
        for j in range(N_PEER_CHIPS):
            for i in range(na):
                _ici_copy(scatter, src_refs, land_refs, s_sems, r_sems, i, j, chips, 2 * x + y, c, False).wait_send()
                _ici_copy(scatter, src_refs, land_refs, s_sems, r_sems, i, j, chips, 2 * x + y, c, True).wait_recv()

    hbm = lambda a: pltpu.HBM(a.shape, a.dtype)
    res = pl.pallas_call(
        body, name=name, out_shape=tuple(hbm(a) for a in list(srcs) + list(lands)),
        in_specs=[_HBM] * (2 * na) + [_SEM, _SEM] + [pl.BlockSpec(memory_space=pl.ANY)] * len(after),
        out_specs=tuple([_HBM] * (2 * na)), input_output_aliases={i: i for i in range(2 * na)},
        compiler_params=pltpu.CompilerParams(has_side_effects=_EFFECT))(*srcs, *lands, send_sems, recv_sems, *after)
    return list(res[:na]), list(res[na:])


def sibling_finish(srcs, lands, scatter, name):
    na = len(srcs)
    per = N_CHIP if scatter else N_PEER_CHIPS

    def body(*refs):
        src_refs, land_in = refs[:na], refs[na:2 * na]
        land_refs = refs[2 * na:3 * na]
        send_sems, recv_sems, local_sems = refs[3 * na:]
        x, y, c, chips = _place()
        k = 2 * x + y
        sibling = (x, y, 1 - c)

        def copy(i, kk, src, chip_k, half):
            return pltpu.make_async_remote_copy(
                src_ref=src, dst_ref=land_refs[i].at[chip_k, half], send_sem=send_sems.at[per * i + kk],
                recv_sem=recv_sems.at[per * i + kk], device_id=sibling, device_id_type=MESH)

        if scatter:
            mine = [pltpu.make_async_copy(src_refs[i].at[k], land_refs[i].at[k, c], local_sems.at[i]) for i in range(na)]
        else:
            mine = [pltpu.make_async_copy(src_refs[i], land_refs[i].at[k], local_sems.at[i]) for i in range(na)]
        for cp in mine:
            cp.start()
        sent = []
        for j, (cx, cy) in enumerate(chips):
            kj = 2 * cx + cy
            sent += [copy(i, j, land_in[i].at[kj, c], kj, c) for i in range(na)]
        if scatter:
            sent += [copy(i, N_PEER_CHIPS, src_refs[i].at[k], k, c) for i in range(na)]
        for cp in sent:
            cp.start()
        for j, (cx, cy) in enumerate(chips):
            kj = 2 * cx + cy
            for i in range(na):
                copy(i, j, land_refs[i].at[kj, 1 - c], kj, 1 - c).wait_recv()
        if scatter:
            for i in range(na):
                copy(i, N_PEER_CHIPS, land_refs[i].at[k, 1 - c], k, 1 - c).wait_recv()
        for cp in sent:
            cp.wait_send()
        for cp in mine:
            cp.wait()

    return pl.pallas_call(
        body, name=name, out_shape=[jax.ShapeDtypeStruct(a.shape, a.dtype) for a in lands],
        in_specs=[_HBM] * (2 * na), out_specs=[_HBM] * na, input_output_aliases={na + i: i for i in range(na)},
        scratch_shapes=[pltpu.SemaphoreType.DMA((per * na,)), pltpu.SemaphoreType.DMA((per * na,)),
                        pltpu.SemaphoreType.DMA((na,))])(*srcs, *lands)


PACK_LANES = 512


def _pack(arrs, dtype, lanes, row_align):
    flat = jnp.concatenate([a.reshape(-1).astype(dtype) for a in arrs])
    unit = lanes * row_align
    total = -(-flat.shape[0] // unit) * unit
    return jnp.pad(flat, (0, total - flat.shape[0])).reshape(total // lanes, lanes)


def _unpack(flat, shapes):
    flat = flat.reshape(-1)
    out, off = [], 0
    for s in shapes:
        n = math.prod(s)
        out.append(flat[off:off + n].reshape(s))
        off += n
    return out


def _adam_tile(w, m, v, g):
    m2 = ADAM_B1 * m + (1.0 - ADAM_B1) * g
    v2 = ADAM_B2 * v + (1.0 - ADAM_B2) * (g * g)
    m_hat = m2 / (1.0 - ADAM_B1 ** ADAM_STEP)
    v_hat = v2 / (1.0 - ADAM_B2 ** ADAM_STEP)
    delta = -ADAM_LR * (m_hat / (jnp.sqrt(v_hat) + ADAM_EPS) + ADAM_WD * w)
    return delta, m2, v2


def adam_big(w, m, v, pieces, name):
    _, r, c = w.shape
    tr = _tile(r, 128, 16)

    def body(w_ref, m_ref, v_ref, p0, p1, p2, p3, g_ref, d_ref, mo_ref, vo_ref):
        g = ((p0[...].astype(F32) + p1[...].astype(F32)) + p2[...].astype(F32)) + p3[...].astype(F32)
        delta, m2, v2 = _adam_tile(w_ref[...], m_ref[...], v_ref[...], g)
        g_ref[...] = g
        d_ref[...] = delta
        mo_ref[...] = m2
        vo_ref[...] = v2

    wspec = pl.BlockSpec((None, tr, c), lambda i: (0, i, 0))
    pspecs = [pl.BlockSpec((None, tr, c), functools.partial(lambda i, kk: (kk, i, 0), kk=kk)) for kk in range(N_CHIP)]
    return pl.pallas_call(
        body, name=name, grid=(r // tr,), in_specs=[wspec] * 3 + pspecs, out_specs=[wspec] * 4,
        out_shape=[jax.ShapeDtypeStruct(w.shape, F32)] * 4, compiler_params=_params(("parallel",)))(
            w, m, v, pieces, pieces, pieces, pieces)


_WEIGHTS = (
    ("meta_tokens", "small", 1), ("ab_norm", "small", None), ("ab_w_in", "big", 2), ("s5_lambda_re", "small", None),
    ("s5_lambda_im", "small", None), ("s5_log_dt", "small", None), ("s5_b_re", "small", None), ("s5_b_im", "small", None),
    ("s5_c_re", "small", None), ("s5_c_im", "small", None), ("s5_d", "small", None), ("s5_glu_w", "big", 1),
    ("s5_glu_b", "small", None), ("ml_conv_w", "small", 2), ("ml_conv_b", "small", None), ("ml_wq", "small", 1),
    ("ml_wk", "small", 1), ("ml_wv", "small", 1), ("ml_w_gate", "small", 1), ("ml_b_gate", "small", None),
    ("ml_norm", "small", None), ("ml_skip", "small", None), ("ab_w_out", "big", 1), ("ssd_norm", "small", 1),
    ("ssd_w_in", "big", 2), ("ssd_conv_w", "small", 2), ("ssd_conv_b", "small", 1), ("ssd_dt_bias", "small", None),
    ("ssd_a_log", "small", None), ("ssd_d", "small", None), ("ssd_gnorm", "small", 1), ("ssd_w_out", "big", 1),
    ("final_norm", "small", None),
)


def _squeeze(a):
    return a[0] if a.ndim >= 3 else a


def kernel(x, meta_tokens, ab_norm, ab_w_in, s5_lambda_re, s5_lambda_im, s5_log_dt, s5_b_re, s5_b_im, s5_c_re, s5_c_im, s5_d, s5_glu_w, s5_glu_b, ml_conv_w, ml_conv_b, ml_wq, ml_wk, ml_wv, ml_w_gate, ml_b_gate, ml_norm, ml_skip, ab_w_out, ssd_norm, ssd_w_in, ssd_conv_w, ssd_conv_b, ssd_dt_bias, ssd_a_log, ssd_d, ssd_gnorm, ssd_w_out, final_norm, loss_target, m_meta_tokens, m_ab_norm, m_ab_w_in, m_s5_lambda_re, m_s5_lambda_im, m_s5_log_dt, m_s5_b_re, m_s5_b_im, m_s5_c_re, m_s5_c_im, m_s5_d, m_s5_glu_w, m_s5_glu_b, m_ml_conv_w, m_ml_conv_b, m_ml_wq, m_ml_wk, m_ml_wv, m_ml_w_gate, m_ml_b_gate, m_ml_norm, m_ml_skip, m_ab_w_out, m_ssd_norm, m_ssd_w_in, m_ssd_conv_w, m_ssd_conv_b, m_ssd_dt_bias, m_ssd_a_log, m_ssd_d, m_ssd_gnorm, m_ssd_w_out, m_final_norm, v_meta_tokens, v_ab_norm, v_ab_w_in, v_s5_lambda_re, v_s5_lambda_im, v_s5_log_dt, v_s5_b_re, v_s5_b_im, v_s5_c_re, v_s5_c_im, v_s5_d, v_s5_glu_w, v_s5_glu_b, v_ml_conv_w, v_ml_conv_b, v_ml_wq, v_ml_wk, v_ml_wv, v_ml_w_gate, v_ml_b_gate, v_ml_norm, v_ml_skip, v_ab_w_out, v_ssd_norm, v_ssd_w_in, v_ssd_conv_w, v_ssd_conv_b, v_ssd_dt_bias, v_ssd_a_log, v_ssd_d, v_ssd_gnorm, v_ssd_w_out, v_final_norm):
    args = (meta_tokens, ab_norm, ab_w_in, s5_lambda_re, s5_lambda_im, s5_log_dt, s5_b_re, s5_b_im, s5_c_re, s5_c_im, s5_d, s5_glu_w, s5_glu_b, ml_conv_w, ml_conv_b, ml_wq, ml_wk, ml_wv, ml_w_gate, ml_b_gate, ml_norm, ml_skip, ab_w_out, ssd_norm, ssd_w_in, ssd_conv_w, ssd_conv_b, ssd_dt_bias, ssd_a_log, ssd_d, ssd_gnorm, ssd_w_out, final_norm)
    m_args = (m_meta_tokens, m_ab_norm, m_ab_w_in, m_s5_lambda_re, m_s5_lambda_im, m_s5_log_dt, m_s5_b_re, m_s5_b_im, m_s5_c_re, m_s5_c_im, m_s5_d, m_s5_glu_w, m_s5_glu_b, m_ml_conv_w, m_ml_conv_b, m_ml_wq, m_ml_wk, m_ml_wv, m_ml_w_gate, m_ml_b_gate, m_ml_norm, m_ml_skip, m_ab_w_out, m_ssd_norm, m_ssd_w_in, m_ssd_conv_w, m_ssd_conv_b, m_ssd_dt_bias, m_ssd_a_log, m_ssd_d, m_ssd_gnorm, m_ssd_w_out, m_final_norm)
    v_args = (v_meta_tokens, v_ab_norm, v_ab_w_in, v_s5_lambda_re, v_s5_lambda_im, v_s5_log_dt, v_s5_b_re, v_s5_b_im, v_s5_c_re, v_s5_c_im, v_s5_d, v_s5_glu_w, v_s5_glu_b, v_ml_conv_w, v_ml_conv_b, v_ml_wq, v_ml_wk, v_ml_wv, v_ml_w_gate, v_ml_b_gate, v_ml_norm, v_ml_skip, v_ab_w_out, v_ssd_norm, v_ssd_w_in, v_ssd_conv_w, v_ssd_conv_b, v_ssd_dt_bias, v_ssd_a_log, v_ssd_d, v_ssd_gnorm, v_ssd_w_out, v_final_norm)
    names = [w[0] for w in _WEIGHTS]
    kind = {w[0]: w[1] for w in _WEIGHTS}
    axis = {w[0]: w[2] for w in _WEIGHTS}
    w_loc = dict(zip(names, args))
    m_loc = dict(zip(names, m_args))
    v_loc = dict(zip(names, v_args))
    chip = 2 * lax.axis_index("x") + lax.axis_index("y")
    core = lax.axis_index("c")
    big = [n for n in names if kind[n] == "big"]
    small = [n for n in names if kind[n] == "small"]
    small_sh = [n for n in small if axis[n] is not None]

    def halves(a):
        return a.astype(BF16).reshape(2, a.shape[1] // 2, a.shape[2])

    def assemble(n, gth):
        shard = gth.reshape((N_CHIP,) + w_loc[n].shape[1:])
        if axis[n] == 1:
            return shard.reshape(-1, shard.shape[2])
        return jnp.concatenate([shard[kk] for kk in range(N_CHIP)], axis=1)

    early = ["ab_w_in", "s5_glu_w"]
    late = ["ab_w_out", "ssd_w_in", "ssd_w_out"]
    gathered = gather_chips([halves(w_loc[n]) for n in early], "gather_early_w")
    after_early = (gathered[0][0, 0, 0, 0] * 0).astype(BF16)
    late_gathered = sequencer_exchange([halves(w_loc[n]) + after_early for n in late], False, 1, "gather_late_w")
    w_in0_shards = gathered[0].reshape((N_CHIP,) + w_loc["ab_w_in"].shape[1:])
    glu_full = assemble("s5_glu_w", gathered[1])

    def columns(lo, hi):
        cw = w_in0_shards.shape[2]
        parts = [w_in0_shards[kk][:, max(lo - kk * cw, 0):min(hi - kk * cw, cw)]
                 for kk in range(N_CHIP) if lo < (kk + 1) * cw and hi > kk * cw]
        return parts[0] if len(parts) == 1 else jnp.concatenate(parts, axis=1)

    small_sh_shapes = [w_loc[n].shape for n in small_sh]
    packed_s = _pack([w_loc[n] for n in small_sh], F32, LANES, SUBLANES)
    g8 = all_gather8(packed_s, "gather_small_w").reshape(N_CHIP, 2, -1)
    sp = {}
    for n in small:
        if axis[n] is None:
            sp[n] = _squeeze(w_loc[n])
    per_chip = [_unpack(g8[kk, 0], small_sh_shapes) for kk in range(N_CHIP)]
    for i, n in enumerate(small_sh):
        sp[n] = _squeeze(jnp.concatenate([per_chip[kk][i] for kk in range(N_CHIP)], axis=axis[n]))

    s5w = glu_full.shape[0]
    mlw = w_loc["ab_w_out"].shape[1] * N_CHIP - s5w
    inner = w_loc["ssd_w_out"].shape[1] * N_CHIP
    n_heads1 = sp["ssd_d"].shape[1]
    cdim = w_loc["ssd_w_in"].shape[2] * N_CHIP - inner - n_heads1
    bw = dict(W0a=columns(0, 2 * s5w), W0xb=columns(2 * s5w, 2 * s5w + mlw),
              W0zb=columns(2 * s5w + mlw, 2 * (s5w + mlw)), glu=glu_full)

    def late_weights(after):
        del after
        fb = {n: assemble(n, gth) for n, gth in zip(late, late_gathered)}
        w_in1 = fb["ssd_w_in"]
        return dict(Wo0a=fb["ab_w_out"][:s5w], Wo0b=fb["ab_w_out"][s5w:], W1z=w_in1[:, :inner],
                    W1x=w_in1[:, inner:inner + cdim], W1dt=_pad_lanes(w_in1[:, inner + cdim:]), Wo1=fb["ssd_w_out"])

    def chip_halves(n, gf):
        _, r, c_ = w_loc[n].shape
        if axis[n] == 1:
            return gf.reshape(N_CHIP, 2, r // 2, c_)
        return jnp.stack([gf[:, kk * c_:(kk + 1) * c_] for kk in range(N_CHIP)]).reshape(N_CHIP, 2, r // 2, c_)

    def chip_partials(ns, gfull, tag):
        gps = [chip_halves(n, gfull[n]) for n in ns]
        from_sibling = swap_halves(gps, "swap_" + tag)
        return [add_halves(gp, oth, core, "add_" + n) for n, gp, oth in zip(ns, gps, from_sibling)]

    late_state = {}

    def late_grads(g):
        gfull = {"ab_w_out": jnp.concatenate([g["Wo0a"], g["Wo0b"]], axis=0),
                 "ssd_w_in": jnp.concatenate([g["W1z"], g["W1x"], g["W1dt"][:, :n_heads1]], axis=1),
                 "ssd_w_out": g["Wo1"]}
        late_state["pieces"] = sequencer_exchange(chip_partials(late, gfull, "late_g"), True, 2, "scatter_late_g")

    loss_local, dh0, gbig, gs = _local_step(x, loss_target, bw, sp, late_weights, late_grads)
    loss = lax.psum(loss_local, ("x", "y", "c"))
    grad_x = dh0[:, N_META:N_META + x.shape[1]]

    gfull = {"ab_w_in": jnp.concatenate([gbig["W0a"], gbig["W0xb"], gbig["W0zb"]], axis=1), "s5_glu_w": gbig["glu"]}
    pieces = dict(zip(early, scatter_chips(chip_partials(early, gfull, "early_g"), "scatter_early_g")))
    pieces.update(zip(late, late_state["pieces"]))

    out_g, out_d, out_m, out_v = {}, {}, {}, {}
    for n in big:
        pcs = pieces[n].reshape((N_CHIP,) + w_loc[n].shape[1:])
        out_g[n], out_d[n], out_m[n], out_v[n] = adam_big(w_loc[n], m_loc[n], v_loc[n], pcs, "adam_" + n)

    small_full_shapes = [sp[n].shape for n in small]
    packed_gs = _pack([gs[n] for n in small], F32, LANES, SUBLANES)
    rows_s = packed_gs.shape[0]
    all_gs = all_gather8(packed_gs, "gather_small_g")
    blocks = [all_gs[i * rows_s:(i + 1) * rows_s] for i in range(N_DEV)]

    def sum8(i, *b):
        acc = b[0]
        for t in b[1:]:
            acc = acc + t
        return acc

    gsum = rowwise("sum_small_g", sum8, blocks, [], [(LANES, F32)], tr=_tile(rows_s, 512, 8))[0]
    g_small = dict(zip(small, _unpack(gsum, small_full_shapes)))
    g_loc = {}
    for n in small:
        g = g_small[n].reshape((1,) + g_small[n].shape) if w_loc[n].ndim >= 3 else g_small[n]
        if axis[n] is not None:
            size = w_loc[n].shape[axis[n]]
            g = lax.dynamic_slice_in_dim(g, chip * size, size, axis=axis[n])
        g_loc[n] = g.reshape(w_loc[n].shape)
    loc_shapes = [w_loc[n].shape for n in small]
    pw, pm, pv, pg = (_pack([d[n] for n in small], F32, LANES, SUBLANES) for d in (w_loc, m_loc, v_loc, g_loc))
    dl, mn, vn = rowwise("adam_small", lambda i, a, b, c_, d_: _adam_tile(a, b, c_, d_), [pw, pm, pv, pg], [],
                         [(LANES, F32)] * 3, tr=_tile(pw.shape[0], 512, 8))
    for d_out, flat in ((out_d, dl), (out_m, mn), (out_v, vn)):
        for n, a in zip(small, _unpack(flat, loc_shapes)):
            d_out[n] = a
    for n in small:
        out_g[n] = g_loc[n]

    return (loss, grad_x, *[out_g[n] for n in names], *[out_d[n] for n in names], *[out_m[n] for n in names],
            *[out_v[n] for n in names])
```

```python
import functools
import math

import jax
import jax.numpy as jnp
from jax import lax
from jax.experimental import pallas as pl
from jax.experimental.pallas import tpu as pltpu
from jax.experimental.pallas import tpu_sc as plsc

F32 = jnp.float32
BF16 = jnp.bfloat16
HI = lax.Precision.HIGHEST

D_MODEL = 2048
SEQ = 2048
N_META = 16
CHUNK = 128
NORM_EPS = 1e-6
HEAD_NORM_EPS = 1e-5
S5_GROUP_SIZE = 16
S5_STATE = 64
MLSTM_HEADS = 8
QKV_BLOCK = 4
SSD_HEAD_DIM = 64
SSD_STATE = 128
SSD_HPG = 8
ADAM_LR = 0.001
ADAM_B1 = 0.9
ADAM_B2 = 0.999
ADAM_EPS = 1e-08
ADAM_WD = 0.01
ADAM_STEP = 10

LANES = 128
SUBLANES = 8
VMEM_LIMIT = 56 * 1024 * 1024
MM_OPERAND_VMEM = 34 * 1024 * 1024


def _sigmoid(x):
    return 0.5 * jnp.tanh(0.5 * x) + 0.5


@jax.custom_vjp
def _silu(x):
    return x * _sigmoid(x)


def _silu_fwd(x):
    return x * _sigmoid(x), x


def _silu_bwd(x, ct):
    s = _sigmoid(x)
    return (ct * (s * (1.0 + x * (1.0 - s))),)


_silu.defvjp(_silu_fwd, _silu_bwd)


def _softplus(x):
    return jnp.maximum(x, 0.0) + jnp.log(1.0 + jnp.exp(-jnp.abs(x)))


def _log_sigmoid(x):
    return jnp.minimum(x, 0.0) - jnp.log(1.0 + jnp.exp(-jnp.abs(x)))


def _gelu(x):
    return 0.5 * x * (1.0 + jnp.tanh(math.sqrt(2.0 / math.pi) * (x + 0.044715 * (x * x * x))))


def _dot(a, b, dims, precision=None):
    return lax.dot_general(a, b, (dims, ((), ())), preferred_element_type=F32, precision=precision)


_NN, _NT, _TN = ((1,), (0,)), ((1,), (1,)), ((0,), (0,))


def _bf16_dot(dims, da_rule, db_rule):
    @jax.custom_vjp
    def f(a, b):
        return _dot(a.astype(BF16), b.astype(BF16), dims)

    def fwd(a, b):
        ab, bb = a.astype(BF16), b.astype(BF16)
        return _dot(ab, bb, dims), (ab, bb, jnp.zeros((), a.dtype), jnp.zeros((), b.dtype))

    def bwd(res, ct):
        ab, bb, a_like, b_like = res
        cb = ct.astype(BF16)
        return da_rule(ab, bb, cb).astype(a_like.dtype), db_rule(ab, bb, cb).astype(b_like.dtype)

    f.defvjp(fwd, bwd)
    return f


_dot_nn = _bf16_dot(_NN, lambda a, b, c: _dot(c, b, _NT), lambda a, b, c: _dot(a, c, _TN))
_dot_nt = _bf16_dot(_NT, lambda a, b, c: _dot(c, b, _NN), lambda a, b, c: _dot(c, a, _TN))
_dot_tn = _bf16_dot(_TN, lambda a, b, c: _dot(b, c, _NT), lambda a, b, c: _dot(a, c, _NN))


def _lane_pick(a, idx):
    sel = (lax.broadcasted_iota(jnp.int32, (1, a.shape[1]), 1) == idx).astype(a.dtype)
    return jnp.sum(a * sel, axis=1, keepdims=True)


def _row_pick(a, idx):
    sel = (lax.broadcasted_iota(jnp.int32, (a.shape[0], 1), 0) == idx).astype(a.dtype)
    return jnp.sum(a * sel, axis=0, keepdims=True)


def _tri(n, upper=False):
    r = lax.broadcasted_iota(jnp.int32, (n, n), 0)
    c = lax.broadcasted_iota(jnp.int32, (n, n), 1)
    return ((r <= c) if upper else (r >= c)).astype(F32)


def _tile(n, target, align):
    if n <= target:
        return n
    t = (target // align) * align
    while t >= align:
        if n % t == 0:
            return t
        t -= align
    return n


def _params(sem=None):
    return pltpu.CompilerParams(dimension_semantics=sem, vmem_limit_bytes=VMEM_LIMIT)


def mm(a, b, mode, name, resid=None, out_dtype=F32):
    if mode == "nn":
        (m, k), (k2, n) = a.shape, b.shape
    elif mode == "nt":
        (m, k), (n, k2) = a.shape, b.shape
    else:
        (k, m), (k2, n) = a.shape, b.shape
    assert k == k2, (a.shape, b.shape, mode)
    a_sz, b_sz = a.dtype.itemsize, b.dtype.itemsize
    if mode == "tn":
        tm, tn = _tile(m, 1024, LANES), _tile(n, 1024, LANES)
        tk = _tile(k, MM_OPERAND_VMEM // (2 * (tm * a_sz + tn * b_sz)), 16)
    else:
        tm, tn = _tile(m, 1088, 16), _tile(n, 512, LANES)
        tk = _tile(k, MM_OPERAND_VMEM // (2 * (tm * a_sz + tn * b_sz)), LANES)
    nk = k // tk
    dims = {"nn": ((1,), (0,)), "nt": ((1,), (1,)), "tn": ((0,), (0,))}[mode]
    has_resid = resid is not None

    def body(*refs):
        if has_resid:
            a_ref, b_ref, r_ref, o_ref = refs[:4]
        else:
            a_ref, b_ref, o_ref = refs[:3]
        part = _dot(a_ref[...].astype(BF16), b_ref[...].astype(BF16), dims)

        def finish(res):
            if has_resid:
                res = res + r_ref[...].astype(F32)
            o_ref[...] = res.astype(o_ref.dtype)

        if nk == 1:
            finish(part)
            return
        acc_ref = refs[-1]
        kk = pl.program_id(2)

        @pl.when(kk == 0)
        def _():
            acc_ref[...] = part

        @pl.when(jnp.logical_and(kk > 0, kk < nk - 1))
        def _():
            acc_ref[...] += part

        @pl.when(kk == nk - 1)
        def _():
            finish(acc_ref[...] + part)

    if mode == "tn":
        a_spec = pl.BlockSpec((tk, tm), lambda i, j, kk: (kk, i))
    else:
        a_spec = pl.BlockSpec((tm, tk), lambda i, j, kk: (i, kk))
    if mode == "nt":
        b_spec = pl.BlockSpec((tn, tk), lambda i, j, kk: (j, kk))
    else:
        b_spec = pl.BlockSpec((tk, tn), lambda i, j, kk: (kk, j))
    o_spec = pl.BlockSpec((tm, tn), lambda i, j, kk: (i, j))
    in_specs = [a_spec, b_spec] + ([o_spec] if has_resid else [])
    args = (a, b) + ((resid,) if has_resid else ())
    return pl.pallas_call(
        body, name=name, grid=(m // tm, n // tn, nk), in_specs=in_specs, out_specs=o_spec,
        out_shape=jax.ShapeDtypeStruct((m, n), out_dtype), scratch_shapes=[pltpu.VMEM((tm, tn), F32)] if nk > 1 else [],
        compiler_params=_params(("parallel", "parallel", "arbitrary")))(*args)


def rowwise(name, f, rows, params, outs, accs=(), tr=128):
    n_rows = rows[0].shape[0]
    assert n_rows % tr == 0
    n_r, n_p, n_o, n_a = len(rows), len(params), len(outs), len(accs)

    def body(*refs):
        i = pl.program_id(0)
        r_vals = [r[...] for r in refs[:n_r]]
        p_vals = [r[...] for r in refs[n_r:n_r + n_p]]
        o_refs = refs[n_r + n_p:n_r + n_p + n_o]
        a_refs = refs[n_r + n_p + n_o:]
        res = f(i, *r_vals, *p_vals)
        if not isinstance(res, (tuple, list)):
            res = (res,)
        assert len(res) == n_o + n_a, (name, len(res))
        for o_ref, val in zip(o_refs, res[:n_o]):
            o_ref[...] = val.astype(o_ref.dtype)
        if n_a:
            @pl.when(i == 0)
            def _():
                for a_ref in a_refs:
                    a_ref[...] = jnp.zeros_like(a_ref)

            for a_ref, val in zip(a_refs, res[n_o:]):
                a_ref[...] += val.astype(F32)

    in_specs = [pl.BlockSpec((tr, r.shape[1]), lambda i: (i, 0)) for r in rows]
    in_specs += [pl.BlockSpec(p.shape, lambda i: (0, 0)) for p in params]
    out_specs = [pl.BlockSpec((tr, w), lambda i: (i, 0)) for w, _ in outs]
    out_specs += [pl.BlockSpec(s, lambda i: (0, 0)) for s in accs]
    out_shape = [jax.ShapeDtypeStruct((n_rows, w), dt) for w, dt in outs]
    out_shape += [jax.ShapeDtypeStruct(s, F32) for s in accs]
    res = pl.pallas_call(
        body, name=name, grid=(n_rows // tr,), in_specs=in_specs, out_specs=out_specs, out_shape=out_shape,
        compiler_params=_params(("arbitrary",)))(*rows, *params)
    return res


def _rms(x, g, eps=NORM_EPS):
    return x * lax.rsqrt(jnp.mean(x * x, axis=-1, keepdims=True) + eps) * g


def norm_fwd(x, g, name):
    return rowwise(name, lambda i, xb, gb: _rms(xb, gb), [x], [g], [(x.shape[1], BF16)], tr=_tile(x.shape[0], 256, 16))[0]


def norm_bwd(x, g, dn, resid, name):
    def f(i, xb, dnb, rb, gb):
        _, vjp = jax.vjp(_rms, xb, gb)
        dx, dg = vjp(dnb)
        return dx + rb, dx + rb, dg

    return rowwise(name, f, [x, dn, resid], [g], [(x.shape[1], F32), (x.shape[1], BF16)], [g.shape],
                   tr=_tile(x.shape[0], 256, 16))


def conv_fwd(x, w, b, nb, name):
    rows, width = x.shape
    nc = rows // nb // CHUNK
    tw = _tile(width, 1024, LANES)
    ksz = w.shape[0]

    def body(x_ref, w_ref, b_ref, o_ref, ext_ref):
        c = pl.program_id(2)

        @pl.when(c == 0)
        def _():
            ext_ref[0:SUBLANES, :] = jnp.zeros((SUBLANES, tw), F32)

        taps = [w_ref[j:j + 1, :] for j in range(ksz)]
        bias = b_ref[...]
        row = lax.broadcasted_iota(jnp.int32, (SUBLANES, tw), 0)
        prev_rot = [pltpu.roll(ext_ref[0:SUBLANES, :], k, 0) for k in range(1, ksz)]
        for s in range(CHUNK // SUBLANES):
            r0 = s * SUBLANES
            cur = x_ref[r0:r0 + SUBLANES, :]
            cur_rot = [pltpu.roll(cur, k, 0) for k in range(1, ksz)]
            acc = bias + taps[ksz - 1] * cur
            for k in range(1, ksz):
                acc = acc + taps[ksz - 1 - k] * jnp.where(row >= k, cur_rot[k - 1], prev_rot[k - 1])
            o_ref[r0:r0 + SUBLANES, :] = acc
            prev_rot = cur_rot
        ext_ref[0:SUBLANES, :] = x_ref[CHUNK - SUBLANES:CHUNK, :]

    return pl.pallas_call(
        body, name=name, grid=(width // tw, nb, nc),
        in_specs=[pl.BlockSpec((CHUNK, tw), lambda j, bb, c: (bb * nc + c, j)),
                  pl.BlockSpec((ksz, tw), lambda j, bb, c: (0, j)),
                  pl.BlockSpec((1, tw), lambda j, bb, c: (0, j))],
        out_specs=pl.BlockSpec((CHUNK, tw), lambda j, bb, c: (bb * nc + c, j)),
        out_shape=jax.ShapeDtypeStruct((rows, width), F32),
        scratch_shapes=[pltpu.VMEM((2 * SUBLANES, tw), F32)],
        compiler_params=_params(("arbitrary", "arbitrary", "arbitrary")))(x, w, b)


def conv_bwd(dc, x, w, nb, name, resid=None, dx_dtype=BF16):
    rows, width = x.shape
    nc = rows // nb // CHUNK
    tw = _tile(width, 1024, LANES)
    ksz = w.shape[0]
    per = CHUNK // SUBLANES
    has_resid = resid is not None

    def body(*refs):
        if has_resid:
            dc_ref, x_ref, halo_ref, w_ref, r_ref, dx_ref, dw_ref, db_ref, extd_ref, extx_ref = refs
        else:
            dc_ref, x_ref, halo_ref, w_ref, dx_ref, dw_ref, db_ref, extd_ref, extx_ref = refs
        bb = pl.program_id(1)
        step = pl.program_id(2)
        c = nc - 1 - step

        @pl.when(jnp.logical_and(bb == 0, step == 0))
        def _():
            dw_ref[...] = jnp.zeros_like(dw_ref)
            db_ref[...] = jnp.zeros_like(db_ref)

        @pl.when(step == 0)
        def _():
            extd_ref[SUBLANES:2 * SUBLANES, :] = jnp.zeros((SUBLANES, tw), F32)

        nstrip = CHUNK // SUBLANES
        taps = [w_ref[j:j + 1, :] for j in range(ksz)]
        row = lax.broadcasted_iota(jnp.int32, (SUBLANES, tw), 0)
        x_prev_rot = [pltpu.roll(jnp.where(c == 0, 0.0, halo_ref[...]), k, 0) for k in range(1, ksz)]
        dcs = dc_ref[0:SUBLANES, :]
        dc_rot = [pltpu.roll(dcs, SUBLANES - k, 0) for k in range(1, ksz)]
        for s in range(nstrip):
            r0 = s * SUBLANES
            nxt = extd_ref[SUBLANES:2 * SUBLANES, :] if s == nstrip - 1 else dc_ref[r0 + SUBLANES:r0 + 2 * SUBLANES, :]
            nxt_rot = [pltpu.roll(nxt, SUBLANES - k, 0) for k in range(1, ksz)]
            xc = x_ref[r0:r0 + SUBLANES, :]
            x_rot = [pltpu.roll(xc, k, 0) for k in range(1, ksz)]
            dx = r_ref[r0:r0 + SUBLANES, :].astype(F32) if has_resid else jnp.zeros((SUBLANES, tw), F32)
            dx = dx + taps[ksz - 1] * dcs
            dw_ref[(ksz - 1) * SUBLANES:ksz * SUBLANES, :] += dcs * xc
            for k in range(1, ksz):
                j = ksz - 1 - k
                dx = dx + taps[j] * jnp.where(row < SUBLANES - k, dc_rot[k - 1], nxt_rot[k - 1])
                dw_ref[j * SUBLANES:(j + 1) * SUBLANES, :] += dcs * jnp.where(row >= k, x_rot[k - 1], x_prev_rot[k - 1])
            if s % 2 == 0:
                held = dx
            else:
                dx_ref[r0 - SUBLANES:r0 + SUBLANES, :] = jnp.concatenate([held, dx], axis=0).astype(dx_ref.dtype)
            db_ref[...] += dcs
            dcs, dc_rot, x_prev_rot = nxt, nxt_rot, x_rot
        extd_ref[SUBLANES:2 * SUBLANES, :] = dc_ref[0:SUBLANES, :]

    def blk(j, bb, step):
        return (bb * nc + nc - 1 - step, j)

    def halo(j, bb, step):
        return (jnp.maximum((bb * nc + nc - 1 - step) * per - 1, 0), j)

    in_specs = [pl.BlockSpec((CHUNK, tw), blk), pl.BlockSpec((CHUNK, tw), blk), pl.BlockSpec((SUBLANES, tw), halo),
                pl.BlockSpec((ksz, tw), lambda j, bb, step: (0, j))]
    args = [dc, x, x, w]
    if has_resid:
        in_specs.append(pl.BlockSpec((CHUNK, tw), blk))
        args.append(resid)
    dx, dw_raw, db_raw = pl.pallas_call(
        body, name=name, grid=(width // tw, nb, nc), in_specs=in_specs,
        out_specs=[pl.BlockSpec((CHUNK, tw), blk), pl.BlockSpec((ksz * SUBLANES, tw), lambda j, bb, step: (0, j)),
                   pl.BlockSpec((SUBLANES, tw), lambda j, bb, step: (0, j))],
        out_shape=[jax.ShapeDtypeStruct((rows, width), dx_dtype), jax.ShapeDtypeStruct((ksz * SUBLANES, width), F32),
                   jax.ShapeDtypeStruct((SUBLANES, width), F32)],
        scratch_shapes=[pltpu.VMEM((2 * SUBLANES, tw), F32), pltpu.VMEM((2 * SUBLANES, tw), F32)],
        compiler_params=_params(("arbitrary", "arbitrary", "arbitrary")))(*args)
    return dx, dw_raw.reshape(ksz, SUBLANES, width).sum(axis=1), db_raw.sum(axis=0, keepdims=True)


S5_Q = 4


def _s5_fill_bu(u, bre_ref, bim_ref, xr_ref, xi_ref, ns):
    for s in range(ns):
        ub = u[:, s * LANES:(s + 1) * LANES].astype(BF16)
        bur = _dot(ub, bre_ref[s], ((1,), (0,)))
        bui = _dot(ub, bim_ref[s], ((1,), (0,)))
        for q in range(S5_Q):
            xr_ref[q, pl.ds(s, CHUNK, stride=ns), :] = bur[:, q * LANES:(q + 1) * LANES]
            xi_ref[q, pl.ds(s, CHUNK, stride=ns), :] = bui[:, q * LANES:(q + 1) * LANES]


def _s5_scan(xr_ref, xi_ref, ar_ref, ai_ref, st_ref, ns):
    ar = [ar_ref[q] for q in range(S5_Q)]
    ai = [ai_ref[q] for q in range(S5_Q)]

    def step(t, carry):
        rows = pl.ds(pl.multiple_of(t * ns, ns), ns)
        out = []
        for q in range(S5_Q):
            pr, pi_ = carry[2 * q], carry[2 * q + 1]
            nr = ar[q] * pr - ai[q] * pi_ + xr_ref[q, rows, :]
            ni = ar[q] * pi_ + ai[q] * pr + xi_ref[q, rows, :]
            xr_ref[q, rows, :] = nr
            xi_ref[q, rows, :] = ni
            out += [nr, ni]
        return tuple(out)

    init = []
    for q in range(S5_Q):
        init += [st_ref[0, q], st_ref[1, q]]
    fin = lax.fori_loop(0, CHUNK, step, tuple(init), unroll=2)
    for q in range(S5_Q):
        st_ref[0, q] = fin[2 * q]
        st_ref[1, q] = fin[2 * q + 1]


def s5_fwd(pa, bre, bim, cre, cim, ar, ai, dvec, nb, name):
    rows = pa.shape[0]
    width = pa.shape[1] // 2
    ns = width // LANES
    nc = rows // nb // CHUNK

    def body(u_ref, bre_ref, bim_ref, cre_ref, cim_ref, ar_ref, ai_ref, d_ref, y_ref, g_ref, so_ref, xr_ref, xi_ref, st_ref):
        c = pl.program_id(1)

        @pl.when(c == 0)
        def _():
            st_ref[...] = jnp.zeros_like(st_ref)

        so_ref[...] = st_ref[...]
        u = u_ref[...]
        _s5_fill_bu(u, bre_ref, bim_ref, xr_ref, xi_ref, ns)
        _s5_scan(xr_ref, xi_ref, ar_ref, ai_ref, st_ref, ns)
        for s in range(ns):
            acc = jnp.zeros((CHUNK, LANES), F32)
            for q in range(S5_Q):
                xr = xr_ref[q, pl.ds(s, CHUNK, stride=ns), :].astype(BF16)
                xi = xi_ref[q, pl.ds(s, CHUNK, stride=ns), :].astype(BF16)
                acc = acc + _dot(xr, cre_ref[s, q * LANES:(q + 1) * LANES, :], ((1,), (0,)))
                acc = acc - _dot(xi, cim_ref[s, q * LANES:(q + 1) * LANES, :], ((1,), (0,)))
            cols = slice(s * LANES, (s + 1) * LANES)
            y = acc + d_ref[:, cols] * u[:, cols]
            y_ref[:, cols] = y
            g_ref[:, cols] = _gelu(y).astype(BF16)

    whole3 = lambda a: pl.BlockSpec(a.shape, lambda b_, c: (0, 0, 0))
    return pl.pallas_call(
        body, name=name, grid=(nb, nc),
        in_specs=[pl.BlockSpec((CHUNK, width), lambda b_, c: (b_ * nc + c, 0)), whole3(bre), whole3(bim), whole3(cre),
                  whole3(cim), whole3(ar), whole3(ai), pl.BlockSpec((1, width), lambda b_, c: (0, 0))],
        out_specs=[pl.BlockSpec((CHUNK, width), lambda b_, c: (b_ * nc + c, 0)),
                   pl.BlockSpec((CHUNK, width), lambda b_, c: (b_ * nc + c, 0)),
                   pl.BlockSpec((None, 2, S5_Q, ns, LANES), lambda b_, c: (b_ * nc + c, 0, 0, 0, 0))],
        out_shape=[jax.ShapeDtypeStruct((rows, width), F32), jax.ShapeDtypeStruct((rows, width), BF16),
                   jax.ShapeDtypeStruct((nb * nc, 2, S5_Q, ns, LANES), F32)],
        scratch_shapes=[pltpu.VMEM((S5_Q, CHUNK * ns, LANES), F32), pltpu.VMEM((S5_Q, CHUNK * ns, LANES), F32),
                        pltpu.VMEM((2, S5_Q, ns, LANES), F32)],
        compiler_params=_params(("arbitrary", "arbitrary")))(pa, bre, bim, cre, cim, ar, ai, dvec)


def s5_bwd(pa, dys, states, bre, bim, cre, cim, ar, ai, dvec, nb, name):
    rows = pa.shape[0]
    width = pa.shape[1] // 2
    ns = width // LANES
    nc = rows // nb // CHUNK

    def body(u_ref, dy_ref, sin_ref, bre_ref, bim_ref, cre_ref, cim_ref, ar_ref, ai_ref, d_ref,
             du_ref, dbre_ref, dbim_ref, dcre_ref, dcim_ref, dar_ref, dai_ref, dd_ref,
             xr_ref, xi_ref, lr_ref, li_ref, st_ref, lam_ref):
        bb = pl.program_id(0)
        step_i = pl.program_id(1)

        @pl.when(jnp.logical_and(bb == 0, step_i == 0))
        def _():
            for r in (dbre_ref, dbim_ref, dcre_ref, dcim_ref, dar_ref, dai_ref, dd_ref):
                r[...] = jnp.zeros_like(r)

        @pl.when(step_i == 0)
        def _():
            lam_ref[...] = jnp.zeros_like(lam_ref)

        u = u_ref[...]
        dy = dy_ref[...]
        st_ref[...] = sin_ref[...]
        _s5_fill_bu(u, bre_ref, bim_ref, xr_ref, xi_ref, ns)
        _s5_scan(xr_ref, xi_ref, ar_ref, ai_ref, st_ref, ns)
        dd_ref[...] += jnp.sum(dy * u, axis=0, keepdims=True)
        for s in range(ns):
            dyb = dy[:, s * LANES:(s + 1) * LANES].astype(BF16)
            gr = _dot(dyb, cre_ref[s], ((1,), (1,)))
            gi = -_dot(dyb, cim_ref[s], ((1,), (1,)))
            for q in range(S5_Q):
                lr_ref[q, pl.ds(s, CHUNK, stride=ns), :] = gr[:, q * LANES:(q + 1) * LANES]
                li_ref[q, pl.ds(s, CHUNK, stride=ns), :] = gi[:, q * LANES:(q + 1) * LANES]
                xr = xr_ref[q, pl.ds(s, CHUNK, stride=ns), :].astype(BF16)
                xi = xi_ref[q, pl.ds(s, CHUNK, stride=ns), :].astype(BF16)
                dcre_ref[s, q * LANES:(q + 1) * LANES, :] += _dot(xr, dyb, ((0,), (0,)))
                dcim_ref[s, q * LANES:(q + 1) * LANES, :] -= _dot(xi, dyb, ((0,), (0,)))
        ar = [ar_ref[q] for q in range(S5_Q)]
        ai = [ai_ref[q] for q in range(S5_Q)]

        def one(t_rows, p_r, p_i, carry):
            out = []
            for q in range(S5_Q):
                l_r, l_i, da_r, da_i = carry[4 * q:4 * q + 4]
                n_r = lr_ref[q, t_rows, :] + ar[q] * l_r + ai[q] * l_i
                n_i = li_ref[q, t_rows, :] + ar[q] * l_i - ai[q] * l_r
                lr_ref[q, t_rows, :] = n_r
                li_ref[q, t_rows, :] = n_i
                xpr, xpi = p_r(q), p_i(q)
                out += [n_r, n_i, da_r + n_r * xpr + n_i * xpi, da_i + n_i * xpr - n_r * xpi]
            return tuple(out)

        def step(k, carry):
            t = CHUNK - 1 - k
            t_rows = pl.ds(pl.multiple_of(t * ns, ns), ns)
            p_rows = pl.ds(pl.multiple_of((t - 1) * ns, ns), ns)
            return one(t_rows, lambda q: xr_ref[q, p_rows, :], lambda q: xi_ref[q, p_rows, :], carry)

        init = []
        zero = jnp.zeros((ns, LANES), F32)
        for q in range(S5_Q):
            init += [lam_ref[0, q], lam_ref[1, q], zero, zero]
        carry = lax.fori_loop(0, CHUNK - 1, step, tuple(init), unroll=2)
        carry = one(pl.ds(0, ns), lambda q: sin_ref[0, q], lambda q: sin_ref[1, q], carry)
        for q in range(S5_Q):
            lam_ref[0, q] = carry[4 * q]
            lam_ref[1, q] = carry[4 * q + 1]
            dar_ref[q] += carry[4 * q + 2]
            dai_ref[q] += carry[4 * q + 3]
        for s in range(ns):
            cols = slice(s * LANES, (s + 1) * LANES)
            ub = u[:, cols].astype(BF16)
            acc = d_ref[:, cols] * dy[:, cols]
            for q in range(S5_Q):
                qs = slice(q * LANES, (q + 1) * LANES)
                lr = lr_ref[q, pl.ds(s, CHUNK, stride=ns), :].astype(BF16)
                li = li_ref[q, pl.ds(s, CHUNK, stride=ns), :].astype(BF16)
                dbre_ref[s, :, qs] += _dot(ub, lr, ((0,), (0,)))
                dbim_ref[s, :, qs] += _dot(ub, li, ((0,), (0,)))
                acc = acc + _dot(lr, bre_ref[s, :, qs], ((1,), (1,))) + _dot(li, bim_ref[s, :, qs], ((1,), (1,)))
            du_ref[:, cols] = acc.astype(du_ref.dtype)

    whole3 = lambda a: pl.BlockSpec(a.shape, lambda b_, c: (0, 0, 0))
    rowblk = pl.BlockSpec((CHUNK, width), lambda b_, c: (b_ * nc + nc - 1 - c, 0))
    scr = pltpu.VMEM((S5_Q, CHUNK * ns, LANES), F32)
    return pl.pallas_call(
        body, name=name, grid=(nb, nc),
        in_specs=[rowblk, rowblk,
                  pl.BlockSpec((None, 2, S5_Q, ns, LANES), lambda b_, c: (b_ * nc + nc - 1 - c, 0, 0, 0, 0)),
                  whole3(bre), whole3(bim), whole3(cre), whole3(cim), whole3(ar), whole3(ai),
                  pl.BlockSpec((1, width), lambda b_, c: (0, 0))],
        out_specs=[rowblk, whole3(bre), whole3(bim), whole3(cre), whole3(cim), whole3(ar), whole3(ai),
                   pl.BlockSpec((1, width), lambda b_, c: (0, 0))],
        out_shape=[jax.ShapeDtypeStruct((rows, width), BF16), jax.ShapeDtypeStruct(bre.shape, F32),
                   jax.ShapeDtypeStruct(bim.shape, F32), jax.ShapeDtypeStruct(cre.shape, F32),
                   jax.ShapeDtypeStruct(cim.shape, F32), jax.ShapeDtypeStruct(ar.shape, F32),
                   jax.ShapeDtypeStruct(ai.shape, F32), jax.ShapeDtypeStruct((1, width), F32)],
        scratch_shapes=[scr, scr, scr, scr, pltpu.VMEM((2, S5_Q, ns, LANES), F32), pltpu.VMEM((2, S5_Q, ns, LANES), F32)],
        compiler_params=_params(("arbitrary", "arbitrary")))(pa, dys, states, bre, bim, cre, cim, ar, ai, dvec)


def _s5_discretize(lam_re, lam_im, log_dt, b_re, b_im):
    dt = jnp.exp(log_dt)[:, None]
    mag = jnp.exp(lam_re * dt)
    ar, ai = mag * jnp.cos(lam_im * dt), mag * jnp.sin(lam_im * dt)
    den = lam_re * lam_re + lam_im * lam_im
    qr = ((ar - 1.0) * lam_re + ai * lam_im) / den
    qi = (ai * lam_re - (ar - 1.0) * lam_im) / den
    bbr = qr[..., None] * b_re - qi[..., None] * b_im
    bbi = qr[..., None] * b_im + qi[..., None] * b_re
    return ar, ai, bbr, bbi


def _s5_expand(ar, ai, bbr, bbi, c_re, c_im):
    g, p, h = bbr.shape
    gps = LANES // h
    ns = g // gps
    eye = jnp.eye(gps, dtype=F32)

    def bexp(b):
        return jnp.einsum("sgph,gk->sghkp", b.reshape(ns, gps, p, h), eye).reshape(ns, gps * h, gps * p)

    def cexp(c):
        return jnp.einsum("sghp,gk->sgpkh", c.reshape(ns, gps, h, p), eye).reshape(ns, gps * p, gps * h)

    def aexp(a):
        return a.reshape(ns, S5_Q, LANES).transpose(1, 0, 2)

    return (bexp(bbr).astype(BF16), bexp(bbi).astype(BF16), cexp(c_re).astype(BF16), cexp(c_im).astype(BF16),
            aexp(ar), aexp(ai))


def _s5_contract(dbre, dbim, dcre, dcim, dar, dai, g, p, h):
    gps = LANES // h
    ns = g // gps
    eye = jnp.eye(gps, dtype=F32)
    bcon = lambda d: jnp.einsum("sghkp,gk->sgph", d.reshape(ns, gps, h, gps, p), eye).reshape(g, p, h)
    ccon = lambda d: jnp.einsum("sgpkh,gk->sghp", d.reshape(ns, gps, p, gps, h), eye).reshape(g, h, p)
    acon = lambda d: d.transpose(1, 0, 2).reshape(g, p)
    return bcon(dbre), bcon(dbim), ccon(dcre), ccon(dcim), acon(dar), acon(dai)


def _ml_proj_tile(cpre, xb, wq, wk, wv, gq, gk, gv):
    xc = _silu(cpre)
    q = _dot_nn(xc, wq)
    k = _dot_nn(xc, wk)
    v = _dot_nn(xb, wv)
    return q, k, v, _dot_nn(q, gq) + _dot_nn(k, gk) + _dot_nn(v, gv)


def ml_proj_fwd(cpre, xb, wq, wk, wv, gq, gk, gv, name):
    rows, width = cpre.shape
    nblk = width // LANES
    tr = _tile(rows, 1088, 16)

    def body(c_ref, x_ref, wq_ref, wk_ref, wv_ref, gq_ref, gk_ref, gv_ref, q_ref, k_ref, v_ref, g_ref):
        j = pl.program_id(1)
        q, k, v, g = _ml_proj_tile(c_ref[...], x_ref[...], wq_ref[...], wk_ref[...], wv_ref[...],
                                   gq_ref[...], gk_ref[...], gv_ref[...])
        q_ref[...] = q
        k_ref[...] = k
        v_ref[...] = v

        @pl.when(j == 0)
        def _():
            g_ref[...] = jnp.zeros_like(g_ref)

        g_ref[...] += g

    rb = pl.BlockSpec((tr, LANES), lambda i, j: (i, j))
    wb = pl.BlockSpec((None, LANES, LANES), lambda i, j: (j, 0, 0))
    return pl.pallas_call(
        body, name=name, grid=(rows // tr, nblk), in_specs=[rb, rb, wb, wb, wb, wb, wb, wb],
        out_specs=[rb, rb, rb, pl.BlockSpec((tr, LANES), lambda i, j: (i, 0))],
        out_shape=[jax.ShapeDtypeStruct((rows, width), F32)] * 3 + [jax.ShapeDtypeStruct((rows, LANES), F32)],
        compiler_params=_params(("arbitrary", "arbitrary")))(cpre, xb, wq, wk, wv, gq, gk, gv)


def ml_proj_bwd(cpre, xb, wq, wk, wv, gq, gk, gv, dq, dk, dv, dg, dcp_extra, name):
    rows, width = cpre.shape
    nblk = width // LANES
    tr = _tile(rows, 1088, 16)

    def body(c_ref, x_ref, wq_ref, wk_ref, wv_ref, gq_ref, gk_ref, gv_ref, dq_ref, dk_ref, dv_ref, dg_ref, e_ref,
             dc_ref, dx_ref, *dw_refs):
        i = pl.program_id(1)
        _, vjp = jax.vjp(_ml_proj_tile, c_ref[...], x_ref[...], wq_ref[...], wk_ref[...], wv_ref[...],
                         gq_ref[...], gk_ref[...], gv_ref[...])
        grads = vjp((dq_ref[...], dk_ref[...], dv_ref[...], dg_ref[...]))
        dc_ref[...] = grads[0] + e_ref[...]
        dx_ref[...] = grads[1]

        @pl.when(i == 0)
        def _():
            for r in dw_refs:
                r[...] = jnp.zeros_like(r)

        for r, gval in zip(dw_refs, grads[2:]):
            r[...] += gval

    rb = pl.BlockSpec((tr, LANES), lambda j, i: (i, j))
    wb = pl.BlockSpec((None, LANES, LANES), lambda j, i: (j, 0, 0))
    gb = pl.BlockSpec((tr, LANES), lambda j, i: (i, 0))
    wshape = jax.ShapeDtypeStruct((nblk, LANES, LANES), F32)
    return pl.pallas_call(
        body, name=name, grid=(nblk, rows // tr), in_specs=[rb, rb, wb, wb, wb, wb, wb, wb, rb, rb, rb, gb, rb],
        out_specs=[rb, rb] + [wb] * 6,
        out_shape=[jax.ShapeDtypeStruct((rows, width), F32)] * 2 + [wshape] * 6,
        compiler_params=_params(("arbitrary", "arbitrary")))(cpre, xb, wq, wk, wv, gq, gk, gv, dq, dk, dv, dg, dcp_extra)


def _ml_gates_tile(gl, bg, nh):
    x = gl + bg
    bcum = _dot(_tri(CHUNK), _log_sigmoid(x), ((1,), (0,)), precision=HI)
    lane = lax.broadcasted_iota(jnp.int32, x.shape, 1)
    return jnp.where(lane < nh, x, jnp.where(lane < 2 * nh, bcum, 0.0))


def _ml_core_tile(q, k, v, colg, rowg, cpre, zb, nw, sk, cst, nst, m_prev):
    c, dh = q.shape
    igc, bc = _lane_pick(colg, 0), _lane_pick(colg, 1)
    igr, br = _row_pick(rowg, 0), _row_pick(rowg, 1)
    causal = _tri(c) > 0
    dmat = jnp.where(causal, bc - br + igr, -jnp.inf)
    inter = bc + m_prev
    mt = lax.stop_gradient(jnp.maximum(inter, jnp.max(dmat, axis=1, keepdims=True)))
    wt = jnp.exp(dmat - mt)
    w_prev = jnp.exp(inter - mt)
    qs = q * (dh ** -0.5)
    s = _dot_nt(qs, k) * wt
    num = _dot_nn(s, v) + w_prev * _dot_nn(qs, cst)
    den = jnp.sum(s, axis=1, keepdims=True) + w_prev * jnp.sum(qs * nst, axis=1, keepdims=True)
    h = num * (1.0 / jnp.maximum(jnp.abs(den), jnp.exp(-mt)))
    last = (lax.broadcasted_iota(jnp.int32, (c, 1), 0) == c - 1).astype(F32)
    blast = jnp.sum(bc * last, axis=0, keepdims=True)
    g = blast - bc + igc
    m_new = lax.stop_gradient(jnp.maximum(blast + m_prev, jnp.max(g, axis=0, keepdims=True)))
    decay = jnp.exp(blast + m_prev - m_new)
    wk = jnp.exp(g - m_new) * k
    c_new = decay * cst + _dot_tn(wk, v)
    n_new = decay * nst + jnp.sum(wk, axis=0, keepdims=True)
    mu = jnp.mean(h, axis=1, keepdims=True)
    hc = h - mu
    var = jnp.mean(hc * hc, axis=1, keepdims=True)
    out = hc * lax.rsqrt(var + HEAD_NORM_EPS) * nw + sk * _silu(cpre)
    return out * _silu(zb), c_new, n_new, m_new


def _ml_core_specs(nc, dh, rev):
    ch = (lambda c: nc - 1 - c) if rev else (lambda c: c)
    rb = pl.BlockSpec((CHUNK, dh), lambda b_, c, h: (b_ * nc + ch(c), h))
    colb = pl.BlockSpec((None, CHUNK, 2), lambda b_, c, h: (h, b_ * nc + ch(c), 0))
    rowb = pl.BlockSpec((None, None, 2, CHUNK), lambda b_, c, h: (b_ * nc + ch(c), h, 0, 0))
    pb = pl.BlockSpec((1, dh), lambda b_, c, h: (0, h))
    cb = pl.BlockSpec((None, None, dh, dh), lambda b_, c, h: (b_ * nc + ch(c), h, 0, 0))
    nb_ = pl.BlockSpec((None, None, 1, dh), lambda b_, c, h: (b_ * nc + ch(c), h, 0, 0))
    mb = pl.BlockSpec((None, None, 1, 1), lambda b_, c, h: (b_ * nc + ch(c), h, 0, 0))
    return rb, colb, rowb, pb, cb, nb_, mb


def ml_core_fwd(q, k, v, colg, rowg, cpre, zb, nw, sk, nb, nh, name):
    rows, width = q.shape
    dh = width // nh
    nc = rows // nb // CHUNK
    rb, colb, rowb, pb, cb, nb_, mb = _ml_core_specs(nc, dh, False)

    def body(q_ref, k_ref, v_ref, col_ref, row_ref, c_ref, z_ref, nw_ref, sk_ref, y_ref, cs_ref, ns_ref, ms_ref,
             cst_ref, nst_ref, mst_ref):
        c = pl.program_id(1)
        h = pl.program_id(2)

        @pl.when(c == 0)
        def _():
            cst_ref[h] = jnp.zeros((dh, dh), F32)
            nst_ref[h] = jnp.zeros((1, dh), F32)
            mst_ref[h] = jnp.zeros((1, 1), F32)

        cst, nst, m_prev = cst_ref[h], nst_ref[h], mst_ref[h]
        cs_ref[...] = cst
        ns_ref[...] = nst
        ms_ref[...] = m_prev
        y, c_new, n_new, m_new = _ml_core_tile(q_ref[...], k_ref[...], v_ref[...], col_ref[...], row_ref[...],
                                               c_ref[...], z_ref[...], nw_ref[...], sk_ref[...], cst, nst, m_prev)
        y_ref[...] = y.astype(BF16)
        cst_ref[h] = c_new
        nst_ref[h] = n_new
        mst_ref[h] = m_new

    nbc = nb * nc
    return pl.pallas_call(
        body, name=name, grid=(nb, nc, nh), in_specs=[rb, rb, rb, colb, rowb, rb, rb, pb, pb],
        out_specs=[rb, cb, nb_, mb],
        out_shape=[jax.ShapeDtypeStruct((rows, width), BF16), jax.ShapeDtypeStruct((nbc, nh, dh, dh), F32),
                   jax.ShapeDtypeStruct((nbc, nh, 1, dh), F32), jax.ShapeDtypeStruct((nbc, nh, 1, 1), F32)],
        scratch_shapes=[pltpu.VMEM((nh, dh, dh), F32), pltpu.VMEM((nh, 1, dh), F32), pltpu.VMEM((nh, 1, 1), F32)],
        compiler_params=_params(("arbitrary", "arbitrary", "arbitrary")))(q, k, v, colg, rowg, cpre, zb, nw, sk)


def ml_core_bwd(q, k, v, colg, rowg, cpre, zb, nw, sk, cs, ns, ms, dy, nb, nh, name):
    rows, width = q.shape
    dh = width // nh
    nc = rows // nb // CHUNK
    rb, colb, rowb, pb, cb, nb_, mb = _ml_core_specs(nc, dh, True)

    def body(q_ref, k_ref, v_ref, col_ref, row_ref, c_ref, z_ref, nw_ref, sk_ref, cs_ref, ns_ref, ms_ref, dy_ref,
             dq_ref, dk_ref, dv_ref, dc_ref, dz_ref, dcol_ref, drow_ref, dnw_ref, dsk_ref, dcst_ref, dnst_ref):
        bb = pl.program_id(0)
        step = pl.program_id(1)
        h = pl.program_id(2)

        @pl.when(jnp.logical_and(bb == 0, jnp.logical_and(step == 0, h == 0)))
        def _():
            dnw_ref[...] = jnp.zeros_like(dnw_ref)
            dsk_ref[...] = jnp.zeros_like(dsk_ref)

        @pl.when(step == 0)
        def _():
            dcst_ref[h] = jnp.zeros((dh, dh), F32)
            dnst_ref[h] = jnp.zeros((1, dh), F32)

        m_prev = ms_ref[...]

        def f(*a):
            return _ml_core_tile(*a, m_prev)[:3]

        _, vjp = jax.vjp(f, q_ref[...], k_ref[...], v_ref[...], col_ref[...], row_ref[...], c_ref[...], z_ref[...],
                         nw_ref[...], sk_ref[...], cs_ref[...], ns_ref[...])
        g = vjp((dy_ref[...], dcst_ref[h], dnst_ref[h]))
        dq_ref[...] = g[0]
        dk_ref[...] = g[1]
        dv_ref[...] = g[2]
        dcol_ref[...] = g[3]
        drow_ref[...] = g[4]
        dc_ref[...] = g[5]
        dz_ref[...] = g[6].astype(dz_ref.dtype)
        dnw_ref[h] += g[7]
        dsk_ref[h] += g[8]
        dcst_ref[h] = g[9]
        dnst_ref[h] = g[10]

    nbc = nb * nc
    accb = pl.BlockSpec((nh, 1, dh), lambda b_, c, h: (0, 0, 0))
    return pl.pallas_call(
        body, name=name, grid=(nb, nc, nh), in_specs=[rb, rb, rb, colb, rowb, rb, rb, pb, pb, cb, nb_, mb, rb],
        out_specs=[rb, rb, rb, rb, rb, colb, rowb, accb, accb],
        out_shape=[jax.ShapeDtypeStruct((rows, width), F32)] * 4 + [jax.ShapeDtypeStruct((rows, width), BF16)]
        + [jax.ShapeDtypeStruct(colg.shape, F32), jax.ShapeDtypeStruct(rowg.shape, F32),
           jax.ShapeDtypeStruct((nh, 1, dh), F32), jax.ShapeDtypeStruct((nh, 1, dh), F32)],
        scratch_shapes=[pltpu.VMEM((nh, dh, dh), F32), pltpu.VMEM((nh, 1, dh), F32)],
        compiler_params=_params(("arbitrary", "arbitrary", "arbitrary")))(
            q, k, v, colg, rowg, cpre, zb, nw, sk, cs, ns, ms, dy)


def _ssd_dt_tile(dtr, bias, alog):
    dt = _softplus(dtr + bias)
    cum = _dot(_tri(CHUNK), dt * (-jnp.exp(alog)), ((1,), (0,)), precision=HI)
    return dt, cum


def _ssd_tile(xcs, bmc, cmc, cols, rows_, z, dvec, gn, states, hpg):
    npair = hpg // 2
    hd = SSD_HEAD_DIM
    xs = [_silu(x) for x in xcs]
    bm, cm = _silu(bmc), _silu(cmc)
    cb = _dot_nt(cm, bm)
    causal = _tri(CHUNK) > 0
    lane_lo = lax.broadcasted_iota(jnp.int32, (1, 2 * hd), 1) < hd
    lastsel = (lax.broadcasted_iota(jnp.int32, (CHUNK, 1), 0) == CHUNK - 1).astype(F32)
    heads = []
    for r in range(hpg):
        dtc, cumc = _lane_pick(cols, r), _lane_pick(cols, hpg + r)
        dtrow, cumr = _row_pick(rows_, r), _row_pick(rows_, hpg + r)
        w = cb * jnp.exp(jnp.where(causal, cumc - cumr, -jnp.inf)) * dtrow
        last = jnp.sum(cumc * lastsel, axis=0, keepdims=True)
        heads.append((w, jnp.exp(cumc), jnp.exp(last - cumc) * dtc, jnp.exp(last)))
    ys, new_states = [], []
    for j in range(npair):
        (wa, ea, da, la), (wb, eb, db, lb) = heads[2 * j], heads[2 * j + 1]
        yi = jnp.where(lane_lo, _dot_nn(wa, xs[j]), _dot_nn(wb, xs[j]))
        ys.append(yi + jnp.where(lane_lo, ea, eb) * _dot_nn(cm, states[j]))
        xd = xs[j] * jnp.where(lane_lo, da, db)
        new_states.append(jnp.where(lane_lo, la, lb) * states[j] + _dot_tn(bm, xd))
    y = jnp.concatenate(ys, axis=1) + dvec * jnp.concatenate(xs, axis=1)
    yg = y * _silu(z)
    yn = yg * lax.rsqrt(jnp.mean(yg * yg, axis=1, keepdims=True) + NORM_EPS) * gn
    return yn, new_states


def _ssd_specs(nc, hpg, ng, rev):
    npair = hpg // 2
    gw = hpg * SSD_HEAD_DIM
    xblocks = ng * npair
    ch = (lambda c: nc - 1 - c) if rev else (lambda c: c)
    xs = [pl.BlockSpec((CHUNK, LANES), functools.partial(lambda b_, c, g, jj: (b_ * nc + ch(c), g * npair + jj), jj=j))
          for j in range(npair)]
    bmb = pl.BlockSpec((CHUNK, SSD_STATE), lambda b_, c, g: (b_ * nc + ch(c), xblocks + g))
    cmb = pl.BlockSpec((CHUNK, SSD_STATE), lambda b_, c, g: (b_ * nc + ch(c), xblocks + ng + g))
    colb = pl.BlockSpec((None, CHUNK, 2 * hpg), lambda b_, c, g: (g, b_ * nc + ch(c), 0))
    rowb = pl.BlockSpec((None, None, 2 * hpg, CHUNK), lambda b_, c, g: (b_ * nc + ch(c), g, 0, 0))
    zb = pl.BlockSpec((CHUNK, gw), lambda b_, c, g: (b_ * nc + ch(c), g))
    pb = pl.BlockSpec((1, gw), lambda b_, c, g: (0, g))
    sb = pl.BlockSpec((None, None, npair, SSD_STATE, 2 * SSD_HEAD_DIM), lambda b_, c, g: (b_ * nc + ch(c), g, 0, 0, 0))
    return xs, bmb, cmb, colb, rowb, zb, pb, sb


def ssd_core_fwd(cpre, cols, rows_, z, dvec, gn, nb, hpg, name):
    rows = cpre.shape[0]
    inner = z.shape[1]
    ng = inner // (hpg * SSD_HEAD_DIM)
    npair = hpg // 2
    nc = rows // nb // CHUNK
    xs, bmb, cmb, colb, rowb, zb, pb, sb = _ssd_specs(nc, hpg, ng, False)

    def body(*refs):
        x_refs = refs[:npair]
        bm_ref, cm_ref, col_ref, row_ref, z_ref, d_ref, gn_ref, y_ref, so_ref, st_ref = refs[npair:]
        c = pl.program_id(1)
        g = pl.program_id(2)

        @pl.when(c == 0)
        def _():
            st_ref[g] = jnp.zeros((npair, SSD_STATE, 2 * SSD_HEAD_DIM), F32)

        so_ref[...] = st_ref[g]
        states = [st_ref[g, j] for j in range(npair)]
        yn, new_states = _ssd_tile([r[...] for r in x_refs], bm_ref[...], cm_ref[...], col_ref[...], row_ref[...],
                                   z_ref[...], d_ref[...], gn_ref[...], states, hpg)
        y_ref[...] = yn.astype(BF16)
        for j in range(npair):
            st_ref[g, j] = new_states[j]

    return pl.pallas_call(
        body, name=name, grid=(nb, nc, ng), in_specs=xs + [bmb, cmb, colb, rowb, zb, pb, pb],
        out_specs=[zb, sb],
        out_shape=[jax.ShapeDtypeStruct((rows, inner), BF16),
                   jax.ShapeDtypeStruct((nb * nc, ng, npair, SSD_STATE, 2 * SSD_HEAD_DIM), F32)],
        scratch_shapes=[pltpu.VMEM((ng, npair, SSD_STATE, 2 * SSD_HEAD_DIM), F32)],
        compiler_params=_params(("arbitrary", "arbitrary", "arbitrary")))(
            *([cpre] * npair), cpre, cpre, cols, rows_, z, dvec, gn)


def ssd_core_bwd(cpre, cols, rows_, z, dvec, gn, states, dyn, nb, hpg, name):
    rows = cpre.shape[0]
    inner = z.shape[1]
    gw = hpg * SSD_HEAD_DIM
    ng = inner // gw
    npair = hpg // 2
    nc = rows // nb // CHUNK
    xs, bmb, cmb, colb, rowb, zb, pb, sb = _ssd_specs(nc, hpg, ng, True)

    def body(*refs):
        x_refs = refs[:npair]
        (bm_ref, cm_ref, col_ref, row_ref, z_ref, d_ref, gn_ref, s_ref, dy_ref,
         dx_ref, dbm_ref, dcm_ref, dcol_ref, drow_ref, dz_ref, dd_ref, dgn_ref, dst_ref) = refs[npair:]
        bb = pl.program_id(0)
        step = pl.program_id(1)
        g = pl.program_id(2)

        @pl.when(jnp.logical_and(bb == 0, jnp.logical_and(step == 0, g == 0)))
        def _():
            dd_ref[...] = jnp.zeros_like(dd_ref)
            dgn_ref[...] = jnp.zeros_like(dgn_ref)

        @pl.when(step == 0)
        def _():
            dst_ref[g] = jnp.zeros((npair, SSD_STATE, 2 * SSD_HEAD_DIM), F32)

        def f(xcs, bmc, cmc, cv, rv, zv, dv_, gv, sts):
            return _ssd_tile(xcs, bmc, cmc, cv, rv, zv, dv_, gv, sts, hpg)

        _, vjp = jax.vjp(f, [r[...] for r in x_refs], bm_ref[...], cm_ref[...], col_ref[...], row_ref[...], z_ref[...],
                         d_ref[...], gn_ref[...], [s_ref[j] for j in range(npair)])
        gr = vjp((dy_ref[...], [dst_ref[g, j] for j in range(npair)]))
        dx_ref[...] = jnp.concatenate(gr[0], axis=1)
        dbm_ref[...] = gr[1]
        dcm_ref[...] = gr[2]
        dcol_ref[...] = gr[3]
        drow_ref[...] = gr[4]
        dz_ref[...] = gr[5].astype(dz_ref.dtype)
        dd_ref[g] += gr[6]
        dgn_ref[g] += gr[7]
        for j in range(npair):
            dst_ref[g, j] = gr[8][j]

    ch = lambda c: nc - 1 - c
    nblk = pl.BlockSpec((CHUNK, SSD_STATE), lambda b_, c, g: (b_ * nc + ch(c), g))
    accb = pl.BlockSpec((ng, 1, gw), lambda b_, c, g: (0, 0, 0))
    return pl.pallas_call(
        body, name=name, grid=(nb, nc, ng), in_specs=xs + [bmb, cmb, colb, rowb, zb, pb, pb, sb, zb],
        out_specs=[zb, nblk, nblk, colb, rowb, zb, accb, accb],
        out_shape=[jax.ShapeDtypeStruct((rows, inner), F32), jax.ShapeDtypeStruct((rows, ng * SSD_STATE), F32),
                   jax.ShapeDtypeStruct((rows, ng * SSD_STATE), F32), jax.ShapeDtypeStruct(cols.shape, F32),
                   jax.ShapeDtypeStruct(rows_.shape, F32), jax.ShapeDtypeStruct((rows, inner), BF16),
                   jax.ShapeDtypeStruct((ng, 1, gw), F32), jax.ShapeDtypeStruct((ng, 1, gw), F32)],
        scratch_shapes=[pltpu.VMEM((ng, npair, SSD_STATE, 2 * SSD_HEAD_DIM), F32)],
        compiler_params=_params(("arbitrary", "arbitrary", "arbitrary")))(
            *([cpre] * npair), cpre, cpre, cols, rows_, z, dvec, gn, states, dyn)


def _hw_expand(w):
    n, bi, _ = w.shape
    per = LANES // bi
    eye = jnp.eye(per, dtype=F32)
    return jnp.einsum("jbio,bc->jbico", w.reshape(n // per, per, bi, bi), eye).reshape(n // per, LANES, LANES)


def _hw_contract(d, bi=QKV_BLOCK):
    per = LANES // bi
    eye = jnp.eye(per, dtype=F32)
    return jnp.einsum("jbico,bc->jbio", d.reshape(d.shape[0], per, bi, per, bi), eye).reshape(-1, bi, bi)


def _wg_expand(wg, width):
    pad = jnp.pad(wg, ((0, 0), (0, LANES - wg.shape[1])))
    return [pad[i * width:(i + 1) * width].reshape(width // LANES, LANES, LANES) for i in range(3)]


def _wg_contract(dgs, ngate):
    return jnp.concatenate([d[:, :, :ngate].reshape(-1, ngate) for d in dgs], axis=0)


def _pad_lanes(a):
    return jnp.pad(a, ((0, 0), (0, LANES - a.shape[1])))


def _pairs_to_layouts(first, second, ngrp, per, nbc):
    rows = first.shape[0]
    both = jnp.concatenate([first.reshape(rows, ngrp, per), second.reshape(rows, ngrp, per)], axis=2)
    return both.transpose(1, 0, 2), both.reshape(nbc, CHUNK, ngrp, 2 * per).transpose(0, 2, 3, 1)


def _layouts_to_pairs(dcols, drows, ngrp, per):
    rows = dcols.shape[1]
    both = dcols.transpose(1, 0, 2) + drows.transpose(0, 3, 1, 2).reshape(rows, ngrp, 2 * per)
    return both[:, :, :per].reshape(rows, ngrp * per), both[:, :, per:].reshape(rows, ngrp * per)


_EARLY = ("W0a", "W0xb", "W0zb", "glu")
_LATE = ("Wo0a", "Wo0b", "W1z", "W1x", "W1dt", "Wo1")


def _local_step(x, target, bw, sp, late_weights=None, late_grads=None, early_grads=None):
    nb, seq, d = x.shape
    nh, hpg = MLSTM_HEADS, SSD_HPG
    t_len = N_META + seq
    nc = -(-t_len // CHUNK)
    tp = nc * CHUNK
    rows = nb * tp
    nbc = nb * nc
    meta = sp["meta_tokens"]
    h0 = jnp.concatenate([jnp.broadcast_to(meta[None], (nb, N_META, d)), x, jnp.zeros((nb, tp - t_len, d), F32)], axis=1)
    h0 = h0.reshape(rows, d)
    tgt = jnp.pad(target, ((0, 0), (N_META, tp - t_len), (0, 0))).reshape(rows, d)

    n0 = norm_fwd(h0, sp["ab_norm"], "norm0")
    pa = mm(n0, bw["W0a"], "nn", "mm_pa")
    xb = mm(n0, bw["W0xb"], "nn", "mm_xb")
    zb = mm(n0, bw["W0zb"], "nn", "mm_zb")
    s5w = pa.shape[1] // 2
    mlw = xb.shape[1]
    s5_args = (sp["s5_lambda_re"], sp["s5_lambda_im"], sp["s5_log_dt"].reshape(-1), sp["s5_b_re"], sp["s5_b_im"])
    (ar, ai, bbr, bbi), s5_disc_vjp = jax.vjp(_s5_discretize, *s5_args)
    sg, spn, shh = bbr.shape
    bre, bim, cre, cim, are, aie = _s5_expand(ar, ai, bbr, bbi, sp["s5_c_re"], sp["s5_c_im"])
    ys5, gb, s5st = s5_fwd(pa, bre, bim, cre, cim, are, aie, sp["s5_d"], nb, "s5_fwd")
    tglu = mm(gb, bw["glu"], "nn", "mm_glu")

    def glu_tile(ys, tt, za, gbias):
        return _gelu(ys) * _sigmoid(tt + gbias) * _silu(za)

    ya = rowwise("glu_fwd", lambda i, ys, tt, pab, gbias: glu_tile(ys, tt, pab[:, s5w:], gbias),
                 [ys5, tglu, pa], [sp["s5_glu_b"]], [(s5w, BF16)], tr=_tile(rows, 256, 16))[0]

    cpre0 = conv_fwd(xb, sp["ml_conv_w"], sp["ml_conv_b"], nb, "ml_conv_fwd")
    wq_e, wk_e, wv_e = _hw_expand(sp["ml_wq"]), _hw_expand(sp["ml_wk"]), _hw_expand(sp["ml_wv"])
    gq, gk, gv = _wg_expand(sp["ml_w_gate"], mlw)
    q, k, v, gl = ml_proj_fwd(cpre0, xb, wq_e, wk_e, wv_e, gq, gk, gv, "ml_proj_fwd")
    bgate = _pad_lanes(sp["ml_b_gate"])
    gout = rowwise("ml_gates_fwd", lambda i, g_, b_: _ml_gates_tile(g_, b_, nh), [gl], [bgate], [(LANES, F32)], tr=CHUNK)[0]
    colg, rowg = _pairs_to_layouts(gout[:, :nh], gout[:, nh:2 * nh], nh, 1, nbc)
    yb, ml_cs, ml_ns, ml_ms = ml_core_fwd(q, k, v, colg, rowg, cpre0, zb, sp["ml_norm"], sp["ml_skip"], nb, nh, "ml_core_fwd")
    if late_weights is not None:
        bw = {**bw, **late_weights()}
    h1 = mm(ya, bw["Wo0a"], "nn", "mm_out0a", resid=h0)
    h1 = mm(yb, bw["Wo0b"], "nn", "mm_out0b", resid=h1)

    n1 = norm_fwd(h1, sp["ssd_norm"], "norm1")
    z1 = mm(n1, bw["W1z"], "nn", "mm_z1")
    xbc = mm(n1, bw["W1x"], "nn", "mm_xbc")
    dtr = mm(n1, bw["W1dt"], "nn", "mm_dt")
    inner = z1.shape[1]
    ng = inner // (hpg * SSD_HEAD_DIM)
    nhd = ng * hpg
    cpre1 = conv_fwd(xbc, sp["ssd_conv_w"], sp["ssd_conv_b"], nb, "ssd_conv_fwd")
    dt_bias, a_log = _pad_lanes(sp["ssd_dt_bias"]), _pad_lanes(sp["ssd_a_log"])
    dt, cum = rowwise("ssd_dt_fwd", lambda i, r_, b_, a_: _ssd_dt_tile(r_, b_, a_), [dtr], [dt_bias, a_log],
                      [(LANES, F32), (LANES, F32)], tr=CHUNK)
    cols, rws = _pairs_to_layouts(dt[:, :nhd], cum[:, :nhd], ng, hpg, nbc)
    dvec = jnp.repeat(sp["ssd_d"], SSD_HEAD_DIM, axis=1)
    yn, ssd_st = ssd_core_fwd(cpre1, cols, rws, z1, dvec, sp["ssd_gnorm"], nb, hpg, "ssd_core_fwd")
    h2 = mm(yn, bw["Wo1"], "nn", "mm_out1", resid=h1)

    tr_l = _tile(tp, 256, 16)
    per_ex = tp // tr_l

    def loss_tile(i, hb, tb, gfn):
        tpos = (i % per_ex) * tr_l + lax.broadcasted_iota(jnp.int32, (tr_l, 1), 0)
        mask = jnp.logical_and(tpos >= N_META, tpos < t_len).astype(F32)

        def lf(hh, gg):
            e = (_rms(hh, gg) - tb) * mask
            return 0.5 * jnp.sum(e * e) / d

        lval, (dh, dg) = jax.value_and_grad(lf, (0, 1))(hb, gfn)
        return dh, dh, jnp.full((1, LANES), lval, F32), dg

    fn = sp["final_norm"].reshape(1, d)
    dh2, dh2b, loss_acc, dfn = rowwise("loss", loss_tile, [h2, tgt], [fn], [(d, F32), (d, BF16)], [(1, LANES), (1, d)], tr=tr_l)

    gbig, gs = {}, {}
    gs["final_norm"] = dfn.reshape(sp["final_norm"].shape)
    dyn = mm(dh2b, bw["Wo1"], "nt", "mm_dyn")
    gbig["Wo1"] = mm(yn, dh2b, "tn", "mm_dWo1", out_dtype=BF16)
    dxs, dbm, dcm, dcols, drws, dz1, ddvec, dgn = ssd_core_bwd(cpre1, cols, rws, z1, dvec, sp["ssd_gnorm"], ssd_st, dyn,
                                                              nb, hpg, "ssd_core_bwd")
    gs["ssd_d"] = ddvec.reshape(1, nhd, SSD_HEAD_DIM).sum(axis=2)
    gs["ssd_gnorm"] = dgn.reshape(1, inner)
    ddt, dcum = _layouts_to_pairs(dcols, drws, ng, hpg)

    def ssd_dt_bwd_tile(i, r_, ddt_, dcum_, b_, a_):
        _, vjp = jax.vjp(_ssd_dt_tile, r_, b_, a_)
        return vjp((ddt_, dcum_))

    ddtr, dbias, dalog = rowwise("ssd_dt_bwd", ssd_dt_bwd_tile, [dtr, _pad_lanes(ddt), _pad_lanes(dcum)], [dt_bias, a_log],
                                 [(LANES, BF16)], [(1, LANES), (1, LANES)], tr=CHUNK)
    gs["ssd_dt_bias"] = dbias[:, :nhd]
    gs["ssd_a_log"] = dalog[:, :nhd]
    dcpre1 = jnp.concatenate([dxs, dbm, dcm], axis=1)
    dxbc, dcw1, dcb1 = conv_bwd(dcpre1, xbc, sp["ssd_conv_w"], nb, "ssd_conv_bwd")
    gs["ssd_conv_w"] = dcw1
    gs["ssd_conv_b"] = dcb1
    dn1 = mm(dz1, bw["W1z"], "nt", "mm_dn1z")
    dn1 = mm(dxbc, bw["W1x"], "nt", "mm_dn1x", resid=dn1)
    dn1 = mm(ddtr, bw["W1dt"], "nt", "mm_dn1dt", resid=dn1)
    gbig["W1z"] = mm(n1, dz1, "tn", "mm_dW1z", out_dtype=BF16)
    gbig["W1x"] = mm(n1, dxbc, "tn", "mm_dW1x", out_dtype=BF16)
    gbig["W1dt"] = mm(n1, ddtr, "tn", "mm_dW1dt", out_dtype=BF16)
    dh1, dh1b, dg1 = norm_bwd(h1, sp["ssd_norm"], dn1, dh2, "norm1_bwd")
    gs["ssd_norm"] = dg1

    gbig["Wo0a"] = mm(ya, dh1b, "tn", "mm_dWo0a", out_dtype=BF16)
    gbig["Wo0b"] = mm(yb, dh1b, "tn", "mm_dWo0b", out_dtype=BF16)
    if late_grads is not None:
        late_grads({n: gbig[n] for n in _LATE})
    dya = mm(dh1b, bw["Wo0a"], "nt", "mm_dya")
    dyb = mm(dh1b, bw["Wo0b"], "nt", "mm_dyb")
    (dq, dk, dv, dcp_skip, dzb, dcolg, drowg, dnw, dsk) = ml_core_bwd(
        q, k, v, colg, rowg, cpre0, zb, sp["ml_norm"], sp["ml_skip"], ml_cs, ml_ns, ml_ms, dyb, nb, nh, "ml_core_bwd")
    gs["ml_norm"] = dnw.reshape(1, mlw)
    gs["ml_skip"] = dsk.reshape(1, mlw)
    dig, dbcum = _layouts_to_pairs(dcolg, drowg, nh, 1)
    dgout = _pad_lanes(jnp.concatenate([dig, dbcum], axis=1))

    def ml_gates_bwd_tile(i, g_, dgo, b_):
        _, vjp = jax.vjp(lambda a, b: _ml_gates_tile(a, b, nh), g_, b_)
        return vjp(dgo)

    dgl, dbg = rowwise("ml_gates_bwd", ml_gates_bwd_tile, [gl, dgout], [bgate], [(LANES, F32)], [(1, LANES)], tr=CHUNK)
    gs["ml_b_gate"] = dbg[:, :2 * nh]
    dcpre0, dxb_v, dwq, dwk, dwv, dgq, dgk, dgv = ml_proj_bwd(cpre0, xb, wq_e, wk_e, wv_e, gq, gk, gv, dq, dk, dv, dgl,
                                                            dcp_skip, "ml_proj_bwd")
    gs["ml_wq"], gs["ml_wk"], gs["ml_wv"] = _hw_contract(dwq), _hw_contract(dwk), _hw_contract(dwv)
    gs["ml_w_gate"] = _wg_contract([dgq, dgk, dgv], 2 * nh)
    dxb, dcw0, dcb0 = conv_bwd(dcpre0, xb, sp["ml_conv_w"], nb, "ml_conv_bwd", resid=dxb_v)
    gs["ml_conv_w"] = dcw0
    gs["ml_conv_b"] = dcb0

    def glu_bwd_tile(i, ys, tt, pab, dy_, gbias):
        _, vjp = jax.vjp(glu_tile, ys, tt, pab[:, s5w:], gbias)
        return vjp(dy_)

    dys_direct, dtglu, dza, dglub = rowwise("glu_bwd", glu_bwd_tile, [ys5, tglu, pa, dya], [sp["s5_glu_b"]],
                                            [(s5w, F32), (s5w, BF16), (s5w, BF16)], [(1, s5w)], tr=_tile(rows, 256, 16))
    gs["s5_glu_b"] = dglub
    dgb = mm(dtglu, bw["glu"], "nt", "mm_dgb")
    gbig["glu"] = mm(gb, dtglu, "tn", "mm_dglu", out_dtype=BF16)

    def gelu_bwd_tile(i, ys, dg_, direct):
        _, vjp = jax.vjp(_gelu, ys)
        return vjp(dg_)[0] + direct

    dys5 = rowwise("gelu_bwd", gelu_bwd_tile, [ys5, dgb, dys_direct], [], [(s5w, F32)], tr=_tile(rows, 256, 16))[0]
    du, dbre, dbim, dcre, dcim, dare, daie, dd5 = s5_bwd(pa, dys5, s5st, bre, bim, cre, cim, are, aie, sp["s5_d"], nb, "s5_bwd")
    gs["s5_d"] = dd5
    dbbr, dbbi, dcr, dci, dar, dai = _s5_contract(dbre, dbim, dcre, dcim, dare, daie, sg, spn, shh)
    gs["s5_c_re"], gs["s5_c_im"] = dcr, dci
    (gs["s5_lambda_re"], gs["s5_lambda_im"], dlogdt, gs["s5_b_re"], gs["s5_b_im"]) = s5_disc_vjp((dar, dai, dbbr, dbbi))
    gs["s5_log_dt"] = dlogdt.reshape(1, -1)
    dpa = jnp.concatenate([du, dza], axis=1)
    gbig["W0a"] = mm(n0, dpa, "tn", "mm_dW0a", out_dtype=BF16)
    gbig["W0xb"] = mm(n0, dxb, "tn", "mm_dW0xb", out_dtype=BF16)
    gbig["W0zb"] = mm(n0, dzb, "tn", "mm_dW0zb", out_dtype=BF16)
    if early_grads is not None:
        early_grads({n: gbig[n] for n in _EARLY})
    dn0 = mm(dpa, bw["W0a"], "nt", "mm_dn0a")
    dn0 = mm(dxb, bw["W0xb"], "nt", "mm_dn0xb", resid=dn0)
    dn0 = mm(dzb, bw["W0zb"], "nt", "mm_dn0zb", resid=dn0)
    dh0, _, dg0 = norm_bwd(h0, sp["ab_norm"], dn0, dh1, "norm0_bwd")
    gs["ab_norm"] = dg0
    dh0 = dh0.reshape(nb, tp, d)
    gs["meta_tokens"] = jnp.sum(dh0[:, :N_META], axis=0)
    return loss_acc[0, 0], dh0, gbig, gs


N_DEV = 8
N_CHIP = 4
N_PEER_CHIPS = N_CHIP - 1
MESH = pl.DeviceIdType.MESH
_HBM = pl.BlockSpec(memory_space=pltpu.HBM)


def _place():
    x, y, c = lax.axis_index("x"), lax.axis_index("y"), lax.axis_index("c")
    return x, y, c, [(1 - x, y), (x, 1 - y), (1 - x, 1 - y)]


def all_gather8(v, name):
    m_per, n = v.shape

    def body(x_ref, out_ref, send_sems, recv_sems, local_sem):
        x, y, c, chips = _place()
        me, sibling = (x, y, c), (x, y, 1 - c)

        def rows(px, py, pc):
            return out_ref.at[pl.ds((4 * px + 2 * py + pc) * m_per, m_per), :]

        def copy(kk, block, to, src=None):
            return pltpu.make_async_remote_copy(
                src_ref=rows(*block) if src is None else src, dst_ref=rows(*block), send_sem=send_sems.at[kk],
                recv_sem=recv_sems.at[kk], device_id=to, device_id_type=MESH)

        mine = pltpu.make_async_copy(x_ref, rows(*me), local_sem)
        mine.start()
        first = [copy(0, me, sibling, src=x_ref)]
        first += [copy(1 + j, me, (*chip, c), src=x_ref) for j, chip in enumerate(chips)]
        for cp in first:
            cp.start()
        passed = [copy(4 + j, (*chip, c), sibling) for j, chip in enumerate(chips)]
        for j, chip in enumerate(chips):
            copy(1 + j, (*chip, c), me).wait_recv()
            passed[j].start()
        copy(0, sibling, me).wait_recv()
        for j, chip in enumerate(chips):
            copy(4 + j, (*chip, 1 - c), me).wait_recv()
        for cp in first + passed:
            cp.wait_send()
        mine.wait()

    return pl.pallas_call(
        body, name=name, out_shape=jax.ShapeDtypeStruct((N_DEV * m_per, n), v.dtype),
        in_specs=[pl.BlockSpec(memory_space=pltpu.VMEM)], out_specs=pl.BlockSpec(memory_space=pltpu.VMEM),
        scratch_shapes=[pltpu.SemaphoreType.DMA((7,)), pltpu.SemaphoreType.DMA((7,)), pltpu.SemaphoreType.DMA],
        compiler_params=pltpu.CompilerParams(vmem_limit_bytes=VMEM_LIMIT))(v)


def gather_chips(vs, name):
    na = len(vs)

    def body(*refs):
        x_refs, out_refs = refs[:na], refs[na:2 * na]
        send_sems, recv_sems, local_sems = refs[2 * na:]
        x, y, c, chips = _place()
        k = 2 * x + y
        sibling = (x, y, 1 - c)

        def copy(i, kk, src, chip_k, half, to):
            return pltpu.make_async_remote_copy(
                src_ref=src, dst_ref=out_refs[i].at[chip_k, half], send_sem=send_sems.at[6 * i + kk],
                recv_sem=recv_sems.at[6 * i + kk], device_id=to, device_id_type=MESH)

        mine = [pltpu.make_async_copy(x_refs[i], out_refs[i].at[k], local_sems.at[i]) for i in range(na)]
        for cp in mine:
            cp.start()
        first = [copy(i, j, x_refs[i].at[c], k, c, (*chip, c)) for j, chip in enumerate(chips) for i in range(na)]
        for cp in first:
            cp.start()
        passed = []
        for j, (cx, cy) in enumerate(chips):
            kj = 2 * cx + cy
            for i in range(na):
                copy(i, j, out_refs[i].at[kj, c], kj, c, (cx, cy, c)).wait_recv()
                fwd = copy(i, 3 + j, out_refs[i].at[kj, c], kj, c, sibling)
                fwd.start()
                passed.append(fwd)
        for j, (cx, cy) in enumerate(chips):
            kj = 2 * cx + cy
            for i in range(na):
                copy(i, 3 + j, out_refs[i].at[kj, 1 - c], kj, 1 - c, sibling).wait_recv()
        for cp in first + passed:
            cp.wait_send()
        for cp in mine:
            cp.wait()

    return pl.pallas_call(
        body, name=name, out_shape=[jax.ShapeDtypeStruct((N_CHIP,) + v.shape, v.dtype) for v in vs],
        in_specs=[_HBM] * na, out_specs=[_HBM] * na,
        scratch_shapes=[pltpu.SemaphoreType.DMA((6 * na,)), pltpu.SemaphoreType.DMA((6 * na,)),
                        pltpu.SemaphoreType.DMA((na,))])(*vs)


def scatter_chips(ps, name):
    na = len(ps)

    def body(*refs):
        p_refs, out_refs = refs[:na], refs[na:2 * na]
        send_sems, recv_sems, local_sems = refs[2 * na:]
        x, y, c, chips = _place()
        k = 2 * x + y
        sibling = (x, y, 1 - c)

        def copy(i, kk, src, chip_k, half, to):
            return pltpu.make_async_remote_copy(
                src_ref=src, dst_ref=out_refs[i].at[chip_k, half], send_sem=send_sems.at[7 * i + kk],
                recv_sem=recv_sems.at[7 * i + kk], device_id=to, device_id_type=MESH)

        mine = [pltpu.make_async_copy(p_refs[i].at[k], out_refs[i].at[k, c], local_sems.at[i]) for i in range(na)]
        for cp in mine:
            cp.start()
        first = [copy(i, 1 + j, p_refs[i].at[2 * cx + cy], k, c, (cx, cy, c))
                 for j, (cx, cy) in enumerate(chips) for i in range(na)]
        first += [copy(i, 0, p_refs[i].at[k], k, c, sibling) for i in range(na)]
        for cp in first:
            cp.start()
        passed = []
        for j, (cx, cy) in enumerate(chips):
            kj = 2 * cx + cy
            for i in range(na):
                copy(i, 1 + j, out_refs[i].at[kj, c], kj, c, (cx, cy, c)).wait_recv()
                fwd = copy(i, 4 + j, out_refs[i].at[kj, c], kj, c, sibling)
                fwd.start()
                passed.append(fwd)
        for i in range(na):
            copy(i, 0, out_refs[i].at[k, 1 - c], k, 1 - c, sibling).wait_recv()
        for j, (cx, cy) in enumerate(chips):
            kj = 2 * cx + cy
            for i in range(na):
                copy(i, 4 + j, out_refs[i].at[kj, 1 - c], kj, 1 - c, sibling).wait_recv()
        for cp in first + passed:
            cp.wait_send()
        for cp in mine:
            cp.wait()

    return pl.pallas_call(
        body, name=name, out_shape=[jax.ShapeDtypeStruct((N_CHIP, 2) + p.shape[1:], p.dtype) for p in ps],
        in_specs=[_HBM] * na, out_specs=[_HBM] * na,
        scratch_shapes=[pltpu.SemaphoreType.DMA((7 * na,)), pltpu.SemaphoreType.DMA((7 * na,)),
                        pltpu.SemaphoreType.DMA((na,))])(*ps)


def swap_halves(gs_, name):
    na = len(gs_)

    def body(*refs):
        g_refs, out_refs = refs[:na], refs[na:2 * na]
        send_sems, recv_sems = refs[2 * na:]
        x, y, c, _ = _place()
        cps = [pltpu.make_async_remote_copy(
            src_ref=g_refs[i].at[kk, 1 - c], dst_ref=out_refs[i].at[kk], send_sem=send_sems.at[N_CHIP * i + kk],
            recv_sem=recv_sems.at[N_CHIP * i + kk], device_id=(x, y, 1 - c), device_id_type=MESH)
            for i in range(na) for kk in range(N_CHIP)]
        for cp in cps:
            cp.start()
        for cp in cps:
            cp.wait()

    return pl.pallas_call(
        body, name=name, out_shape=[jax.ShapeDtypeStruct((N_CHIP,) + g.shape[2:], g.dtype) for g in gs_],
        in_specs=[_HBM] * na, out_specs=[_HBM] * na,
        scratch_shapes=[pltpu.SemaphoreType.DMA((N_CHIP * na,)), pltpu.SemaphoreType.DMA((N_CHIP * na,))])(*gs_)


def add_halves(g, other, core, name):
    _, _, m, n = g.shape
    tr = _tile(m, 256, 16)

    def body(core_ref, g_ref, o_ref, out_ref):
        out_ref[...] = (g_ref[...].astype(F32) + o_ref[...].astype(F32)).astype(out_ref.dtype)

    grid_spec = pltpu.PrefetchScalarGridSpec(
        num_scalar_prefetch=1, grid=(N_CHIP, m // tr),
        in_specs=[pl.BlockSpec((None, None, tr, n), lambda kk, i, core_ref: (kk, core_ref[0], i, 0)),
                  pl.BlockSpec((None, tr, n), lambda kk, i, core_ref: (kk, i, 0))],
        out_specs=pl.BlockSpec((None, tr, n), lambda kk, i, core_ref: (kk, i, 0)))
    return pl.pallas_call(body, name=name, grid_spec=grid_spec, out_shape=jax.ShapeDtypeStruct((N_CHIP, m, n), g.dtype),
                          compiler_params=_params(("arbitrary", "arbitrary")))(core.reshape(1).astype(jnp.int32), g, other)


def sequencer_exchange(srcs, scatter, collective_id, name):
    na = len(srcs)
    per = 2 * N_PEER_CHIPS + (1 if scatter else 0)
    hbm = pltpu.MemorySpace.HBM
    src_refs = [jax.new_ref(a, memory_space=hbm) for a in srcs]
    out_refs = [jax.empty_ref(jax.ShapeDtypeStruct((N_CHIP, 2) + a.shape[1:], a.dtype), memory_space=hbm) for a in srcs]

    @pl.kernel(mesh=plsc.ScalarSubcoreMesh(axis_name="seq", num_cores=1), name=name,
               scratch_types=(pltpu.SemaphoreType.DMA((per * na,)), pltpu.SemaphoreType.DMA((per * na,)),
                              pltpu.SemaphoreType.DMA((na,))),
               compiler_params=pltpu.CompilerParams(collective_id=collective_id))
    def launch(send_sems, recv_sems, local_sems):
        x, y, c, chips = _place()
        k = 2 * x + y
        sibling = (x, y, 1 - c)
        barrier = pltpu.get_barrier_semaphore()
        for cx, cy in chips:
            pl.semaphore_signal(barrier, inc=1, device_id=(cx, cy, c), device_id_type=MESH)
        pl.semaphore_signal(barrier, inc=1, device_id=sibling, device_id_type=MESH)
        pl.semaphore_wait(barrier, N_CHIP)

        def copy(i, kk, src, chip_k, half, to):
            return pltpu.make_async_remote_copy(
                src_ref=src, dst_ref=out_refs[i].at[chip_k, half], send_sem=send_sems.at[per * i + kk],
                recv_sem=recv_sems.at[per * i + kk], device_id=to, device_id_type=MESH)

        if scatter:
            mine = [pltpu.make_async_copy(src_refs[i].at[k], out_refs[i].at[k, c], local_sems.at[i]) for i in range(na)]
        else:
            mine = [pltpu.make_async_copy(src_refs[i], out_refs[i].at[k], local_sems.at[i]) for i in range(na)]
        for cp in mine:
            cp.start()
        first = []
        for j, (cx, cy) in enumerate(chips):
            for i in range(na):
                src = src_refs[i].at[2 * cx + cy] if scatter else src_refs[i].at[c]
                first.append(copy(i, j, src, k, c, (cx, cy, c)))
        if scatter:
            first += [copy(i, 2 * N_PEER_CHIPS, src_refs[i].at[k], k, c, sibling) for i in range(na)]
        for cp in first:
            cp.start()
        passed = []
        for j, (cx, cy) in enumerate(chips):
            kj = 2 * cx + cy
            for i in range(na):
                copy(i, j, out_refs[i].at[kj, c], kj, c, (cx, cy, c)).wait_recv()
                fwd = copy(i, N_PEER_CHIPS + j, out_refs[i].at[kj, c], kj, c, sibling)
                fwd.start()
                passed.append(fwd)
        if scatter:
            for i in range(na):
                copy(i, 2 * N_PEER_CHIPS, out_refs[i].at[k, 1 - c], k, 1 - c, sibling).wait_recv()
        for j, (cx, cy) in enumerate(chips):
            kj = 2 * cx + cy
            for i in range(na):
                copy(i, N_PEER_CHIPS + j, out_refs[i].at[kj, 1 - c], kj, 1 - c, sibling).wait_recv()
        for cp in first + passed:
            cp.wait_send()
        for cp in mine:
            cp.wait()

    launch()
    return [r[...] for r in out_refs]


PACK_LANES = 512


def _pack(arrs, dtype, lanes, row_align):
    flat = jnp.concatenate([a.reshape(-1).astype(dtype) for a in arrs])
    unit = lanes * row_align
    total = -(-flat.shape[0] // unit) * unit
    return jnp.pad(flat, (0, total - flat.shape[0])).reshape(total // lanes, lanes)


def _unpack(flat, shapes):
    flat = flat.reshape(-1)
    out, off = [], 0
    for s in shapes:
        n = math.prod(s)
        out.append(flat[off:off + n].reshape(s))
        off += n
    return out


def _adam_tile(w, m, v, g):
    m2 = ADAM_B1 * m + (1.0 - ADAM_B1) * g
    v2 = ADAM_B2 * v + (1.0 - ADAM_B2) * (g * g)
    m_hat = m2 / (1.0 - ADAM_B1 ** ADAM_STEP)
    v_hat = v2 / (1.0 - ADAM_B2 ** ADAM_STEP)
    delta = -ADAM_LR * (m_hat / (jnp.sqrt(v_hat) + ADAM_EPS) + ADAM_WD * w)
    return delta, m2, v2


def adam_big(w, m, v, pieces, name):
    _, r, c = w.shape
    tr = _tile(r, 128, 16)

    def body(w_ref, m_ref, v_ref, p0, p1, p2, p3, g_ref, d_ref, mo_ref, vo_ref):
        g = ((p0[...].astype(F32) + p1[...].astype(F32)) + p2[...].astype(F32)) + p3[...].astype(F32)
        delta, m2, v2 = _adam_tile(w_ref[...], m_ref[...], v_ref[...], g)
        g_ref[...] = g
        d_ref[...] = delta
        mo_ref[...] = m2
        vo_ref[...] = v2

    wspec = pl.BlockSpec((None, tr, c), lambda i: (0, i, 0))
    pspecs = [pl.BlockSpec((None, tr, c), functools.partial(lambda i, kk: (kk, i, 0), kk=kk)) for kk in range(N_CHIP)]
    return pl.pallas_call(
        body, name=name, grid=(r // tr,), in_specs=[wspec] * 3 + pspecs, out_specs=[wspec] * 4,
        out_shape=[jax.ShapeDtypeStruct(w.shape, F32)] * 4, compiler_params=_params(("parallel",)))(
            w, m, v, pieces, pieces, pieces, pieces)


_WEIGHTS = (
    ("meta_tokens", "small", 1), ("ab_norm", "small", None), ("ab_w_in", "big", 2), ("s5_lambda_re", "small", None),
    ("s5_lambda_im", "small", None), ("s5_log_dt", "small", None), ("s5_b_re", "small", None), ("s5_b_im", "small", None),
    ("s5_c_re", "small", None), ("s5_c_im", "small", None), ("s5_d", "small", None), ("s5_glu_w", "big", 1),
    ("s5_glu_b", "small", None), ("ml_conv_w", "small", 2), ("ml_conv_b", "small", None), ("ml_wq", "small", 1),
    ("ml_wk", "small", 1), ("ml_wv", "small", 1), ("ml_w_gate", "small", 1), ("ml_b_gate", "small", None),
    ("ml_norm", "small", None), ("ml_skip", "small", None), ("ab_w_out", "big", 1), ("ssd_norm", "small", 1),
    ("ssd_w_in", "big", 2), ("ssd_conv_w", "small", 2), ("ssd_conv_b", "small", 1), ("ssd_dt_bias", "small", None),
    ("ssd_a_log", "small", None), ("ssd_d", "small", None), ("ssd_gnorm", "small", 1), ("ssd_w_out", "big", 1),
    ("final_norm", "small", None),
)


def _squeeze(a):
    return a[0] if a.ndim >= 3 else a


def kernel(x, meta_tokens, ab_norm, ab_w_in, s5_lambda_re, s5_lambda_im, s5_log_dt, s5_b_re, s5_b_im, s5_c_re, s5_c_im, s5_d, s5_glu_w, s5_glu_b, ml_conv_w, ml_conv_b, ml_wq, ml_wk, ml_wv, ml_w_gate, ml_b_gate, ml_norm, ml_skip, ab_w_out, ssd_norm, ssd_w_in, ssd_conv_w, ssd_conv_b, ssd_dt_bias, ssd_a_log, ssd_d, ssd_gnorm, ssd_w_out, final_norm, loss_target, m_meta_tokens, m_ab_norm, m_ab_w_in, m_s5_lambda_re, m_s5_lambda_im, m_s5_log_dt, m_s5_b_re, m_s5_b_im, m_s5_c_re, m_s5_c_im, m_s5_d, m_s5_glu_w, m_s5_glu_b, m_ml_conv_w, m_ml_conv_b, m_ml_wq, m_ml_wk, m_ml_wv, m_ml_w_gate, m_ml_b_gate, m_ml_norm, m_ml_skip, m_ab_w_out, m_ssd_norm, m_ssd_w_in, m_ssd_conv_w, m_ssd_conv_b, m_ssd_dt_bias, m_ssd_a_log, m_ssd_d, m_ssd_gnorm, m_ssd_w_out, m_final_norm, v_meta_tokens, v_ab_norm, v_ab_w_in, v_s5_lambda_re, v_s5_lambda_im, v_s5_log_dt, v_s5_b_re, v_s5_b_im, v_s5_c_re, v_s5_c_im, v_s5_d, v_s5_glu_w, v_s5_glu_b, v_ml_conv_w, v_ml_conv_b, v_ml_wq, v_ml_wk, v_ml_wv, v_ml_w_gate, v_ml_b_gate, v_ml_norm, v_ml_skip, v_ab_w_out, v_ssd_norm, v_ssd_w_in, v_ssd_conv_w, v_ssd_conv_b, v_ssd_dt_bias, v_ssd_a_log, v_ssd_d, v_ssd_gnorm, v_ssd_w_out, v_final_norm):
    args = (meta_tokens, ab_norm, ab_w_in, s5_lambda_re, s5_lambda_im, s5_log_dt, s5_b_re, s5_b_im, s5_c_re, s5_c_im, s5_d, s5_glu_w, s5_glu_b, ml_conv_w, ml_conv_b, ml_wq, ml_wk, ml_wv, ml_w_gate, ml_b_gate, ml_norm, ml_skip, ab_w_out, ssd_norm, ssd_w_in, ssd_conv_w, ssd_conv_b, ssd_dt_bias, ssd_a_log, ssd_d, ssd_gnorm, ssd_w_out, final_norm)
    m_args = (m_meta_tokens, m_ab_norm, m_ab_w_in, m_s5_lambda_re, m_s5_lambda_im, m_s5_log_dt, m_s5_b_re, m_s5_b_im, m_s5_c_re, m_s5_c_im, m_s5_d, m_s5_glu_w, m_s5_glu_b, m_ml_conv_w, m_ml_conv_b, m_ml_wq, m_ml_wk, m_ml_wv, m_ml_w_gate, m_ml_b_gate, m_ml_norm, m_ml_skip, m_ab_w_out, m_ssd_norm, m_ssd_w_in, m_ssd_conv_w, m_ssd_conv_b, m_ssd_dt_bias, m_ssd_a_log, m_ssd_d, m_ssd_gnorm, m_ssd_w_out, m_final_norm)
    v_args = (v_meta_tokens, v_ab_norm, v_ab_w_in, v_s5_lambda_re, v_s5_lambda_im, v_s5_log_dt, v_s5_b_re, v_s5_b_im, v_s5_c_re, v_s5_c_im, v_s5_d, v_s5_glu_w, v_s5_glu_b, v_ml_conv_w, v_ml_conv_b, v_ml_wq, v_ml_wk, v_ml_wv, v_ml_w_gate, v_ml_b_gate, v_ml_norm, v_ml_skip, v_ab_w_out, v_ssd_norm, v_ssd_w_in, v_ssd_conv_w, v_ssd_conv_b, v_ssd_dt_bias, v_ssd_a_log, v_ssd_d, v_ssd_gnorm, v_ssd_w_out, v_final_norm)
    names = [w[0] for w in _WEIGHTS]
    kind = {w[0]: w[1] for w in _WEIGHTS}
    axis = {w[0]: w[2] for w in _WEIGHTS}
    w_loc = dict(zip(names, args))
    m_loc = dict(zip(names, m_args))
    v_loc = dict(zip(names, v_args))
    chip = 2 * lax.axis_index("x") + lax.axis_index("y")
    core = lax.axis_index("c")
    big = [n for n in names if kind[n] == "big"]
    small = [n for n in names if kind[n] == "small"]
    small_sh = [n for n in small if axis[n] is not None]

    def halves(a):
        return a.astype(BF16).reshape(2, a.shape[1] // 2, a.shape[2])

    def assemble(n, gth):
        shard = gth.reshape((N_CHIP,) + w_loc[n].shape[1:])
        if axis[n] == 1:
            return shard.reshape(-1, shard.shape[2])
        return jnp.concatenate([shard[kk] for kk in range(N_CHIP)], axis=1)

    early = ["ab_w_in", "s5_glu_w"]
    late = ["ab_w_out", "ssd_w_in", "ssd_w_out"]
    gathered = gather_chips([halves(w_loc[n]) for n in early], "gather_early_w")
    after_early = (gathered[0][0, 0, 0, 0] * 0).astype(BF16)
    late_gathered = sequencer_exchange([halves(w_loc[n]) + after_early for n in late], False, 1, "gather_late_w")
    w_in0_shards = gathered[0].reshape((N_CHIP,) + w_loc["ab_w_in"].shape[1:])
    glu_full = assemble("s5_glu_w", gathered[1])

    def columns(lo, hi):
        cw = w_in0_shards.shape[2]
        parts = [w_in0_shards[kk][:, max(lo - kk * cw, 0):min(hi - kk * cw, cw)]
                 for kk in range(N_CHIP) if lo < (kk + 1) * cw and hi > kk * cw]
        return parts[0] if len(parts) == 1 else jnp.concatenate(parts, axis=1)

    small_sh_shapes = [w_loc[n].shape for n in small_sh]
    packed_s = _pack([w_loc[n] for n in small_sh], F32, LANES, SUBLANES)
    g8 = all_gather8(packed_s, "gather_small_w").reshape(N_CHIP, 2, -1)
    sp = {}
    for n in small:
        if axis[n] is None:
            sp[n] = _squeeze(w_loc[n])
    per_chip = [_unpack(g8[kk, 0], small_sh_shapes) for kk in range(N_CHIP)]
    for i, n in enumerate(small_sh):
        sp[n] = _squeeze(jnp.concatenate([per_chip[kk][i] for kk in range(N_CHIP)], axis=axis[n]))

    s5w = glu_full.shape[0]
    mlw = w_loc["ab_w_out"].shape[1] * N_CHIP - s5w
    inner = w_loc["ssd_w_out"].shape[1] * N_CHIP
    n_heads1 = sp["ssd_d"].shape[1]
    cdim = w_loc["ssd_w_in"].shape[2] * N_CHIP - inner - n_heads1
    bw = dict(W0a=columns(0, 2 * s5w), W0xb=columns(2 * s5w, 2 * s5w + mlw),
              W0zb=columns(2 * s5w + mlw, 2 * (s5w + mlw)), glu=glu_full)

    def late_weights():
        fb = {n: assemble(n, gth) for n, gth in zip(late, late_gathered)}
        w_in1 = fb["ssd_w_in"]
        return dict(Wo0a=fb["ab_w_out"][:s5w], Wo0b=fb["ab_w_out"][s5w:], W1z=w_in1[:, :inner],
                    W1x=w_in1[:, inner:inner + cdim], W1dt=_pad_lanes(w_in1[:, inner + cdim:]), Wo1=fb["ssd_w_out"])

    def chip_halves(n, gf):
        _, r, c_ = w_loc[n].shape
        if axis[n] == 1:
            return gf.reshape(N_CHIP, 2, r // 2, c_)
        return jnp.stack([gf[:, kk * c_:(kk + 1) * c_] for kk in range(N_CHIP)]).reshape(N_CHIP, 2, r // 2, c_)

    def chip_partials(ns, gfull, tag):
        gps = [chip_halves(n, gfull[n]) for n in ns]
        from_sibling = swap_halves(gps, "swap_" + tag)
        return [add_halves(gp, oth, core, "add_" + n) for n, gp, oth in zip(ns, gps, from_sibling)]

    pieces = {}

    def late_grads(g):
        gfull = {"ab_w_out": jnp.concatenate([g["Wo0a"], g["Wo0b"]], axis=0),
                 "ssd_w_in": jnp.concatenate([g["W1z"], g["W1x"], g["W1dt"][:, :n_heads1]], axis=1),
                 "ssd_w_out": g["Wo1"]}
        pieces.update(zip(late, sequencer_exchange(chip_partials(late, gfull, "late_g"), True, 2, "scatter_late_g")))

    def early_grads(g):
        gfull = {"ab_w_in": jnp.concatenate([g["W0a"], g["W0xb"], g["W0zb"]], axis=1), "s5_glu_w": g["glu"]}
        pieces.update(zip(early, sequencer_exchange(chip_partials(early, gfull, "early_g"), True, 3, "scatter_early_g")))

    loss_local, dh0, gbig, gs = _local_step(x, loss_target, bw, sp, late_weights, late_grads, early_grads)
    loss = lax.psum(loss_local, ("x", "y", "c"))
    grad_x = dh0[:, N_META:N_META + x.shape[1]]

    out_g, out_d, out_m, out_v = {}, {}, {}, {}
    for n in big:
        pcs = pieces[n].reshape((N_CHIP,) + w_loc[n].shape[1:])
        out_g[n], out_d[n], out_m[n], out_v[n] = adam_big(w_loc[n], m_loc[n], v_loc[n], pcs, "adam_" + n)

    small_full_shapes = [sp[n].shape for n in small]
    packed_gs = _pack([gs[n] for n in small], F32, LANES, SUBLANES)
    rows_s = packed_gs.shape[0]
    all_gs = all_gather8(packed_gs, "gather_small_g")
    blocks = [all_gs[i * rows_s:(i + 1) * rows_s] for i in range(N_DEV)]

    def sum8(i, *b):
        acc = b[0]
        for t in b[1:]:
            acc = acc + t
        return acc

    gsum = rowwise("sum_small_g", sum8, blocks, [], [(LANES, F32)], tr=_tile(rows_s, 512, 8))[0]
    g_small = dict(zip(small, _unpack(gsum, small_full_shapes)))
    g_loc = {}
    for n in small:
        g = g_small[n].reshape((1,) + g_small[n].shape) if w_loc[n].ndim >= 3 else g_small[n]
        if axis[n] is not None:
            size = w_loc[n].shape[axis[n]]
            g = lax.dynamic_slice_in_dim(g, chip * size, size, axis=axis[n])
        g_loc[n] = g.reshape(w_loc[n].shape)
    loc_shapes = [w_loc[n].shape for n in small]
    pw, pm, pv, pg = (_pack([d[n] for n in small], F32, LANES, SUBLANES) for d in (w_loc, m_loc, v_loc, g_loc))
    dl, mn, vn = rowwise("adam_small", lambda i, a, b, c_, d_: _adam_tile(a, b, c_, d_), [pw, pm, pv, pg], [],
                         [(LANES, F32)] * 3, tr=_tile(pw.shape[0], 512, 8))
    for d_out, flat in ((out_d, dl), (out_m, mn), (out_v, vn)):
        for n, a in zip(small, _unpack(flat, loc_shapes)):
            d_out[n] = a
    for n in small:
        out_g[n] = g_loc[n]

    return (loss, grad_x, *[out_g[n] for n in names], *[out_d[n] for n in names], *[out_m[n] for n in names],
            *[out_v[n] for n in names])
```

```python
import functools
import math

import jax
import jax.numpy as jnp
from jax import lax
from jax.experimental import pallas as pl
from jax.experimental.pallas import tpu as pltpu
from jax.experimental.pallas import tpu_sc as plsc

F32 = jnp.float32
BF16 = jnp.bfloat16
HI = lax.Precision.HIGHEST

D_MODEL = 2048
SEQ = 2048
N_META = 16
CHUNK = 128
NORM_EPS = 1e-6
HEAD_NORM_EPS = 1e-5
S5_GROUP_SIZE = 16
S5_STATE = 64
MLSTM_HEADS = 8
QKV_BLOCK = 4
SSD_HEAD_DIM = 64
SSD_STATE = 128
SSD_HPG = 8
ADAM_LR = 0.001
ADAM_B1 = 0.9
ADAM_B2 = 0.999
ADAM_EPS = 1e-08
ADAM_WD = 0.01
ADAM_STEP = 10

LANES = 128
SUBLANES = 8
VMEM_LIMIT = 56 * 1024 * 1024
MM_OPERAND_VMEM = 34 * 1024 * 1024


def _sigmoid(x):
    return 0.5 * jnp.tanh(0.5 * x) + 0.5


@jax.custom_vjp
def _silu(x):
    return x * _sigmoid(x)


def _silu_fwd(x):
    return x * _sigmoid(x), x


def _silu_bwd(x, ct):
    s = _sigmoid(x)
    return (ct * (s * (1.0 + x * (1.0 - s))),)


_silu.defvjp(_silu_fwd, _silu_bwd)


def _softplus(x):
    return jnp.maximum(x, 0.0) + jnp.log(1.0 + jnp.exp(-jnp.abs(x)))


def _log_sigmoid(x):
    return jnp.minimum(x, 0.0) - jnp.log(1.0 + jnp.exp(-jnp.abs(x)))


def _gelu(x):
    return 0.5 * x * (1.0 + jnp.tanh(math.sqrt(2.0 / math.pi) * (x + 0.044715 * (x * x * x))))


def _dot(a, b, dims, precision=None):
    return lax.dot_general(a, b, (dims, ((), ())), preferred_element_type=F32, precision=precision)


_NN, _NT, _TN = ((1,), (0,)), ((1,), (1,)), ((0,), (0,))


def _bf16_dot(dims, da_rule, db_rule):
    @jax.custom_vjp
    def f(a, b):
        return _dot(a.astype(BF16), b.astype(BF16), dims)

    def fwd(a, b):
        ab, bb = a.astype(BF16), b.astype(BF16)
        return _dot(ab, bb, dims), (ab, bb, jnp.zeros((), a.dtype), jnp.zeros((), b.dtype))

    def bwd(res, ct):
        ab, bb, a_like, b_like = res
        cb = ct.astype(BF16)
        return da_rule(ab, bb, cb).astype(a_like.dtype), db_rule(ab, bb, cb).astype(b_like.dtype)

    f.defvjp(fwd, bwd)
    return f


_dot_nn = _bf16_dot(_NN, lambda a, b, c: _dot(c, b, _NT), lambda a, b, c: _dot(a, c, _TN))
_dot_nt = _bf16_dot(_NT, lambda a, b, c: _dot(c, b, _NN), lambda a, b, c: _dot(c, a, _TN))
_dot_tn = _bf16_dot(_TN, lambda a, b, c: _dot(b, c, _NT), lambda a, b, c: _dot(a, c, _NN))


def _lane_pick(a, idx):
    sel = (lax.broadcasted_iota(jnp.int32, (1, a.shape[1]), 1) == idx).astype(a.dtype)
    return jnp.sum(a * sel, axis=1, keepdims=True)


def _row_pick(a, idx):
    sel = (lax.broadcasted_iota(jnp.int32, (a.shape[0], 1), 0) == idx).astype(a.dtype)
    return jnp.sum(a * sel, axis=0, keepdims=True)


def _tri(n, upper=False):
    r = lax.broadcasted_iota(jnp.int32, (n, n), 0)
    c = lax.broadcasted_iota(jnp.int32, (n, n), 1)
    return ((r <= c) if upper else (r >= c)).astype(F32)


def _tile(n, target, align):
    if n <= target:
        return n
    t = (target // align) * align
    while t >= align:
        if n % t == 0:
            return t
        t -= align
    return n


def _params(sem=None):
    return pltpu.CompilerParams(dimension_semantics=sem, vmem_limit_bytes=VMEM_LIMIT)


def mm(a, b, mode, name, resid=None, out_dtype=F32):
    if mode == "nn":
        (m, k), (k2, n) = a.shape, b.shape
    elif mode == "nt":
        (m, k), (n, k2) = a.shape, b.shape
    else:
        (k, m), (k2, n) = a.shape, b.shape
    assert k == k2, (a.shape, b.shape, mode)
    a_sz, b_sz = a.dtype.itemsize, b.dtype.itemsize
    if mode == "tn":
        tm, tn = _tile(m, 1024, LANES), _tile(n, 1024, LANES)
        tk = _tile(k, MM_OPERAND_VMEM // (2 * (tm * a_sz + tn * b_sz)), 16)
    else:
        tm, tn = _tile(m, 1088, 16), _tile(n, 512, LANES)
        tk = _tile(k, MM_OPERAND_VMEM // (2 * (tm * a_sz + tn * b_sz)), LANES)
    nk = k // tk
    dims = {"nn": ((1,), (0,)), "nt": ((1,), (1,)), "tn": ((0,), (0,))}[mode]
    has_resid = resid is not None

    def body(*refs):
        if has_resid:
            a_ref, b_ref, r_ref, o_ref = refs[:4]
        else:
            a_ref, b_ref, o_ref = refs[:3]
        part = _dot(a_ref[...].astype(BF16), b_ref[...].astype(BF16), dims)

        def finish(res):
            if has_resid:
                res = res + r_ref[...].astype(F32)
            o_ref[...] = res.astype(o_ref.dtype)

        if nk == 1:
            finish(part)
            return
        acc_ref = refs[-1]
        kk = pl.program_id(2)

        @pl.when(kk == 0)
        def _():
            acc_ref[...] = part

        @pl.when(jnp.logical_and(kk > 0, kk < nk - 1))
        def _():
            acc_ref[...] += part

        @pl.when(kk == nk - 1)
        def _():
            finish(acc_ref[...] + part)

    if mode == "tn":
        a_spec = pl.BlockSpec((tk, tm), lambda i, j, kk: (kk, i))
    else:
        a_spec = pl.BlockSpec((tm, tk), lambda i, j, kk: (i, kk))
    if mode == "nt":
        b_spec = pl.BlockSpec((tn, tk), lambda i, j, kk: (j, kk))
    else:
        b_spec = pl.BlockSpec((tk, tn), lambda i, j, kk: (kk, j))
    o_spec = pl.BlockSpec((tm, tn), lambda i, j, kk: (i, j))
    in_specs = [a_spec, b_spec] + ([o_spec] if has_resid else [])
    args = (a, b) + ((resid,) if has_resid else ())
    return pl.pallas_call(
        body, name=name, grid=(m // tm, n // tn, nk), in_specs=in_specs, out_specs=o_spec,
        out_shape=jax.ShapeDtypeStruct((m, n), out_dtype), scratch_shapes=[pltpu.VMEM((tm, tn), F32)] if nk > 1 else [],
        compiler_params=_params(("parallel", "parallel", "arbitrary")))(*args)


def rowwise(name, f, rows, params, outs, accs=(), tr=128):
    n_rows = rows[0].shape[0]
    assert n_rows % tr == 0
    n_r, n_p, n_o, n_a = len(rows), len(params), len(outs), len(accs)

    def body(*refs):
        i = pl.program_id(0)
        r_vals = [r[...] for r in refs[:n_r]]
        p_vals = [r[...] for r in refs[n_r:n_r + n_p]]
        o_refs = refs[n_r + n_p:n_r + n_p + n_o]
        a_refs = refs[n_r + n_p + n_o:]
        res = f(i, *r_vals, *p_vals)
        if not isinstance(res, (tuple, list)):
            res = (res,)
        assert len(res) == n_o + n_a, (name, len(res))
        for o_ref, val in zip(o_refs, res[:n_o]):
            o_ref[...] = val.astype(o_ref.dtype)
        if n_a:
            @pl.when(i == 0)
            def _():
                for a_ref in a_refs:
                    a_ref[...] = jnp.zeros_like(a_ref)

            for a_ref, val in zip(a_refs, res[n_o:]):
                a_ref[...] += val.astype(F32)

    in_specs = [pl.BlockSpec((tr, r.shape[1]), lambda i: (i, 0)) for r in rows]
    in_specs += [pl.BlockSpec(p.shape, lambda i: (0, 0)) for p in params]
    out_specs = [pl.BlockSpec((tr, w), lambda i: (i, 0)) for w, _ in outs]
    out_specs += [pl.BlockSpec(s, lambda i: (0, 0)) for s in accs]
    out_shape = [jax.ShapeDtypeStruct((n_rows, w), dt) for w, dt in outs]
    out_shape += [jax.ShapeDtypeStruct(s, F32) for s in accs]
    res = pl.pallas_call(
        body, name=name, grid=(n_rows // tr,), in_specs=in_specs, out_specs=out_specs, out_shape=out_shape,
        compiler_params=_params(("arbitrary",)))(*rows, *params)
    return res


def _rms(x, g, eps=NORM_EPS):
    return x * lax.rsqrt(jnp.mean(x * x, axis=-1, keepdims=True) + eps) * g


def norm_fwd(x, g, name):
    return rowwise(name, lambda i, xb, gb: _rms(xb, gb), [x], [g], [(x.shape[1], BF16)], tr=_tile(x.shape[0], 256, 16))[0]


def norm_bwd(x, g, dn, resid, name):
    def f(i, xb, dnb, rb, gb):
        _, vjp = jax.vjp(_rms, xb, gb)
        dx, dg = vjp(dnb)
        return dx + rb, dx + rb, dg

    return rowwise(name, f, [x, dn, resid], [g], [(x.shape[1], F32), (x.shape[1], BF16)], [g.shape],
                   tr=_tile(x.shape[0], 256, 16))


def conv_fwd(x, w, b, nb, name):
    rows, width = x.shape
    nc = rows // nb // CHUNK
    tw = _tile(width, 1024, LANES)
    ksz = w.shape[0]

    def body(x_ref, w_ref, b_ref, o_ref, ext_ref):
        c = pl.program_id(2)

        @pl.when(c == 0)
        def _():
            ext_ref[0:SUBLANES, :] = jnp.zeros((SUBLANES, tw), F32)

        taps = [w_ref[j:j + 1, :] for j in range(ksz)]
        bias = b_ref[...]
        row = lax.broadcasted_iota(jnp.int32, (SUBLANES, tw), 0)
        prev_rot = [pltpu.roll(ext_ref[0:SUBLANES, :], k, 0) for k in range(1, ksz)]
        for s in range(CHUNK // SUBLANES):
            r0 = s * SUBLANES
            cur = x_ref[r0:r0 + SUBLANES, :]
            cur_rot = [pltpu.roll(cur, k, 0) for k in range(1, ksz)]
            acc = bias + taps[ksz - 1] * cur
            for k in range(1, ksz):
                acc = acc + taps[ksz - 1 - k] * jnp.where(row >= k, cur_rot[k - 1], prev_rot[k - 1])
            o_ref[r0:r0 + SUBLANES, :] = acc
            prev_rot = cur_rot
        ext_ref[0:SUBLANES, :] = x_ref[CHUNK - SUBLANES:CHUNK, :]

    return pl.pallas_call(
        body, name=name, grid=(width // tw, nb, nc),
        in_specs=[pl.BlockSpec((CHUNK, tw), lambda j, bb, c: (bb * nc + c, j)),
                  pl.BlockSpec((ksz, tw), lambda j, bb, c: (0, j)),
                  pl.BlockSpec((1, tw), lambda j, bb, c: (0, j))],
        out_specs=pl.BlockSpec((CHUNK, tw), lambda j, bb, c: (bb * nc + c, j)),
        out_shape=jax.ShapeDtypeStruct((rows, width), F32),
        scratch_shapes=[pltpu.VMEM((2 * SUBLANES, tw), F32)],
        compiler_params=_params(("arbitrary", "arbitrary", "arbitrary")))(x, w, b)


def conv_bwd(dc, x, w, nb, name, resid=None, dx_dtype=BF16):
    rows, width = x.shape
    nc = rows // nb // CHUNK
    tw = _tile(width, 1024, LANES)
    ksz = w.shape[0]
    per = CHUNK // SUBLANES
    has_resid = resid is not None

    def body(*refs):
        if has_resid:
            dc_ref, x_ref, halo_ref, w_ref, r_ref, dx_ref, dw_ref, db_ref, extd_ref, extx_ref = refs
        else:
            dc_ref, x_ref, halo_ref, w_ref, dx_ref, dw_ref, db_ref, extd_ref, extx_ref = refs
        bb = pl.program_id(1)
        step = pl.program_id(2)
        c = nc - 1 - step

        @pl.when(jnp.logical_and(bb == 0, step == 0))
        def _():
            dw_ref[...] = jnp.zeros_like(dw_ref)
            db_ref[...] = jnp.zeros_like(db_ref)

        @pl.when(step == 0)
        def _():
            extd_ref[SUBLANES:2 * SUBLANES, :] = jnp.zeros((SUBLANES, tw), F32)

        nstrip = CHUNK // SUBLANES
        taps = [w_ref[j:j + 1, :] for j in range(ksz)]
        row = lax.broadcasted_iota(jnp.int32, (SUBLANES, tw), 0)
        x_prev_rot = [pltpu.roll(jnp.where(c == 0, 0.0, halo_ref[...]), k, 0) for k in range(1, ksz)]
        dcs = dc_ref[0:SUBLANES, :]
        dc_rot = [pltpu.roll(dcs, SUBLANES - k, 0) for k in range(1, ksz)]
        for s in range(nstrip):
            r0 = s * SUBLANES
            nxt = extd_ref[SUBLANES:2 * SUBLANES, :] if s == nstrip - 1 else dc_ref[r0 + SUBLANES:r0 + 2 * SUBLANES, :]
            nxt_rot = [pltpu.roll(nxt, SUBLANES - k, 0) for k in range(1, ksz)]
            xc = x_ref[r0:r0 + SUBLANES, :]
            x_rot = [pltpu.roll(xc, k, 0) for k in range(1, ksz)]
            dx = r_ref[r0:r0 + SUBLANES, :].astype(F32) if has_resid else jnp.zeros((SUBLANES, tw), F32)
            dx = dx + taps[ksz - 1] * dcs
            dw_ref[(ksz - 1) * SUBLANES:ksz * SUBLANES, :] += dcs * xc
            for k in range(1, ksz):
                j = ksz - 1 - k
                dx = dx + taps[j] * jnp.where(row < SUBLANES - k, dc_rot[k - 1], nxt_rot[k - 1])
                dw_ref[j * SUBLANES:(j + 1) * SUBLANES, :] += dcs * jnp.where(row >= k, x_rot[k - 1], x_prev_rot[k - 1])
            if s % 2 == 0:
                held = dx
            else:
                dx_ref[r0 - SUBLANES:r0 + SUBLANES, :] = jnp.concatenate([held, dx], axis=0).astype(dx_ref.dtype)
            db_ref[...] += dcs
            dcs, dc_rot, x_prev_rot = nxt, nxt_rot, x_rot
        extd_ref[SUBLANES:2 * SUBLANES, :] = dc_ref[0:SUBLANES, :]

    def blk(j, bb, step):
        return (bb * nc + nc - 1 - step, j)

    def halo(j, bb, step):
        return (jnp.maximum((bb * nc + nc - 1 - step) * per - 1, 0), j)

    in_specs = [pl.BlockSpec((CHUNK, tw), blk), pl.BlockSpec((CHUNK, tw), blk), pl.BlockSpec((SUBLANES, tw), halo),
                pl.BlockSpec((ksz, tw), lambda j, bb, step: (0, j))]
    args = [dc, x, x, w]
    if has_resid:
        in_specs.append(pl.BlockSpec((CHUNK, tw), blk))
        args.append(resid)
    dx, dw_raw, db_raw = pl.pallas_call(
        body, name=name, grid=(width // tw, nb, nc), in_specs=in_specs,
        out_specs=[pl.BlockSpec((CHUNK, tw), blk), pl.BlockSpec((ksz * SUBLANES, tw), lambda j, bb, step: (0, j)),
                   pl.BlockSpec((SUBLANES, tw), lambda j, bb, step: (0, j))],
        out_shape=[jax.ShapeDtypeStruct((rows, width), dx_dtype), jax.ShapeDtypeStruct((ksz * SUBLANES, width), F32),
                   jax.ShapeDtypeStruct((SUBLANES, width), F32)],
        scratch_shapes=[pltpu.VMEM((2 * SUBLANES, tw), F32), pltpu.VMEM((2 * SUBLANES, tw), F32)],
        compiler_params=_params(("arbitrary", "arbitrary", "arbitrary")))(*args)
    return dx, dw_raw.reshape(ksz, SUBLANES, width).sum(axis=1), db_raw.sum(axis=0, keepdims=True)


S5_Q = 4


def _s5_fill_bu(u, bre_ref, bim_ref, xr_ref, xi_ref, ns):
    for s in range(ns):
        ub = u[:, s * LANES:(s + 1) * LANES].astype(BF16)
        bur = _dot(ub, bre_ref[s], ((1,), (0,)))
        bui = _dot(ub, bim_ref[s], ((1,), (0,)))
        for q in range(S5_Q):
            xr_ref[q, pl.ds(s, CHUNK, stride=ns), :] = bur[:, q * LANES:(q + 1) * LANES]
            xi_ref[q, pl.ds(s, CHUNK, stride=ns), :] = bui[:, q * LANES:(q + 1) * LANES]


def _s5_scan(xr_ref, xi_ref, ar_ref, ai_ref, st_ref, ns):
    ar = [ar_ref[q] for q in range(S5_Q)]
    ai = [ai_ref[q] for q in range(S5_Q)]

    def step(t, carry):
        rows = pl.ds(pl.multiple_of(t * ns, ns), ns)
        out = []
        for q in range(S5_Q):
            pr, pi_ = carry[2 * q], carry[2 * q + 1]
            nr = ar[q] * pr - ai[q] * pi_ + xr_ref[q, rows, :]
            ni = ar[q] * pi_ + ai[q] * pr + xi_ref[q, rows, :]
            xr_ref[q, rows, :] = nr
            xi_ref[q, rows, :] = ni
            out += [nr, ni]
        return tuple(out)

    init = []
    for q in range(S5_Q):
        init += [st_ref[0, q], st_ref[1, q]]
    fin = lax.fori_loop(0, CHUNK, step, tuple(init), unroll=2)
    for q in range(S5_Q):
        st_ref[0, q] = fin[2 * q]
        st_ref[1, q] = fin[2 * q + 1]


def s5_fwd(pa, bre, bim, cre, cim, ar, ai, dvec, nb, name):
    rows = pa.shape[0]
    width = pa.shape[1] // 2
    ns = width // LANES
    nc = rows // nb // CHUNK

    def body(u_ref, bre_ref, bim_ref, cre_ref, cim_ref, ar_ref, ai_ref, d_ref, y_ref, g_ref, so_ref, xr_ref, xi_ref, st_ref):
        c = pl.program_id(1)

        @pl.when(c == 0)
        def _():
            st_ref[...] = jnp.zeros_like(st_ref)

        so_ref[...] = st_ref[...]
        u = u_ref[...]
        _s5_fill_bu(u, bre_ref, bim_ref, xr_ref, xi_ref, ns)
        _s5_scan(xr_ref, xi_ref, ar_ref, ai_ref, st_ref, ns)
        for s in range(ns):
            acc = jnp.zeros((CHUNK, LANES), F32)
            for q in range(S5_Q):
                xr = xr_ref[q, pl.ds(s, CHUNK, stride=ns), :].astype(BF16)
                xi = xi_ref[q, pl.ds(s, CHUNK, stride=ns), :].astype(BF16)
                acc = acc + _dot(xr, cre_ref[s, q * LANES:(q + 1) * LANES, :], ((1,), (0,)))
                acc = acc - _dot(xi, cim_ref[s, q * LANES:(q + 1) * LANES, :], ((1,), (0,)))
            cols = slice(s * LANES, (s + 1) * LANES)
            y = acc + d_ref[:, cols] * u[:, cols]
            y_ref[:, cols] = y
            g_ref[:, cols] = _gelu(y).astype(BF16)

    whole3 = lambda a: pl.BlockSpec(a.shape, lambda b_, c: (0, 0, 0))
    return pl.pallas_call(
        body, name=name, grid=(nb, nc),
        in_specs=[pl.BlockSpec((CHUNK, width), lambda b_, c: (b_ * nc + c, 0)), whole3(bre), whole3(bim), whole3(cre),
                  whole3(cim), whole3(ar), whole3(ai), pl.BlockSpec((1, width), lambda b_, c: (0, 0))],
        out_specs=[pl.BlockSpec((CHUNK, width), lambda b_, c: (b_ * nc + c, 0)),
                   pl.BlockSpec((CHUNK, width), lambda b_, c: (b_ * nc + c, 0)),
                   pl.BlockSpec((None, 2, S5_Q, ns, LANES), lambda b_, c: (b_ * nc + c, 0, 0, 0, 0))],
        out_shape=[jax.ShapeDtypeStruct((rows, width), F32), jax.ShapeDtypeStruct((rows, width), BF16),
                   jax.ShapeDtypeStruct((nb * nc, 2, S5_Q, ns, LANES), F32)],
        scratch_shapes=[pltpu.VMEM((S5_Q, CHUNK * ns, LANES), F32), pltpu.VMEM((S5_Q, CHUNK * ns, LANES), F32),
                        pltpu.VMEM((2, S5_Q, ns, LANES), F32)],
        compiler_params=_params(("arbitrary", "arbitrary")))(pa, bre, bim, cre, cim, ar, ai, dvec)


def s5_bwd(pa, dys, states, bre, bim, cre, cim, ar, ai, dvec, nb, name):
    rows = pa.shape[0]
    width = pa.shape[1] // 2
    ns = width // LANES
    nc = rows // nb // CHUNK

    def body(u_ref, dy_ref, sin_ref, bre_ref, bim_ref, cre_ref, cim_ref, ar_ref, ai_ref, d_ref,
             du_ref, dbre_ref, dbim_ref, dcre_ref, dcim_ref, dar_ref, dai_ref, dd_ref,
             xr_ref, xi_ref, lr_ref, li_ref, st_ref, lam_ref):
        bb = pl.program_id(0)
        step_i = pl.program_id(1)

        @pl.when(jnp.logical_and(bb == 0, step_i == 0))
        def _():
            for r in (dbre_ref, dbim_ref, dcre_ref, dcim_ref, dar_ref, dai_ref, dd_ref):
                r[...] = jnp.zeros_like(r)

        @pl.when(step_i == 0)
        def _():
            lam_ref[...] = jnp.zeros_like(lam_ref)

        u = u_ref[...]
        dy = dy_ref[...]
        st_ref[...] = sin_ref[...]
        _s5_fill_bu(u, bre_ref, bim_ref, xr_ref, xi_ref, ns)
        _s5_scan(xr_ref, xi_ref, ar_ref, ai_ref, st_ref, ns)
        dd_ref[...] += jnp.sum(dy * u, axis=0, keepdims=True)
        for s in range(ns):
            dyb = dy[:, s * LANES:(s + 1) * LANES].astype(BF16)
            gr = _dot(dyb, cre_ref[s], ((1,), (1,)))
            gi = -_dot(dyb, cim_ref[s], ((1,), (1,)))
            for q in range(S5_Q):
                lr_ref[q, pl.ds(s, CHUNK, stride=ns), :] = gr[:, q * LANES:(q + 1) * LANES]
                li_ref[q, pl.ds(s, CHUNK, stride=ns), :] = gi[:, q * LANES:(q + 1) * LANES]
                xr = xr_ref[q, pl.ds(s, CHUNK, stride=ns), :].astype(BF16)
                xi = xi_ref[q, pl.ds(s, CHUNK, stride=ns), :].astype(BF16)
                dcre_ref[s, q * LANES:(q + 1) * LANES, :] += _dot(xr, dyb, ((0,), (0,)))
                dcim_ref[s, q * LANES:(q + 1) * LANES, :] -= _dot(xi, dyb, ((0,), (0,)))
        ar = [ar_ref[q] for q in range(S5_Q)]
        ai = [ai_ref[q] for q in range(S5_Q)]

        def one(t_rows, p_r, p_i, carry):
            out = []
            for q in range(S5_Q):
                l_r, l_i, da_r, da_i = carry[4 * q:4 * q + 4]
                n_r = lr_ref[q, t_rows, :] + ar[q] * l_r + ai[q] * l_i
                n_i = li_ref[q, t_rows, :] + ar[q] * l_i - ai[q] * l_r
                lr_ref[q, t_rows, :] = n_r
                li_ref[q, t_rows, :] = n_i
                xpr, xpi = p_r(q), p_i(q)
                out += [n_r, n_i, da_r + n_r * xpr + n_i * xpi, da_i + n_i * xpr - n_r * xpi]
            return tuple(out)

        def step(k, carry):
            t = CHUNK - 1 - k
            t_rows = pl.ds(pl.multiple_of(t * ns, ns), ns)
            p_rows = pl.ds(pl.multiple_of((t - 1) * ns, ns), ns)
            return one(t_rows, lambda q: xr_ref[q, p_rows, :], lambda q: xi_ref[q, p_rows, :], carry)

        init = []
        zero = jnp.zeros((ns, LANES), F32)
        for q in range(S5_Q):
            init += [lam_ref[0, q], lam_ref[1, q], zero, zero]
        carry = lax.fori_loop(0, CHUNK - 1, step, tuple(init), unroll=2)
        carry = one(pl.ds(0, ns), lambda q: sin_ref[0, q], lambda q: sin_ref[1, q], carry)
        for q in range(S5_Q):
            lam_ref[0, q] = carry[4 * q]
            lam_ref[1, q] = carry[4 * q + 1]
            dar_ref[q] += carry[4 * q + 2]
            dai_ref[q] += carry[4 * q + 3]
        for s in range(ns):
            cols = slice(s * LANES, (s + 1) * LANES)
            ub = u[:, cols].astype(BF16)
            acc = d_ref[:, cols] * dy[:, cols]
            for q in range(S5_Q):
                qs = slice(q * LANES, (q + 1) * LANES)
                lr = lr_ref[q, pl.ds(s, CHUNK, stride=ns), :].astype(BF16)
                li = li_ref[q, pl.ds(s, CHUNK, stride=ns), :].astype(BF16)
                dbre_ref[s, :, qs] += _dot(ub, lr, ((0,), (0,)))
                dbim_ref[s, :, qs] += _dot(ub, li, ((0,), (0,)))
                acc = acc + _dot(lr, bre_ref[s, :, qs], ((1,), (1,))) + _dot(li, bim_ref[s, :, qs], ((1,), (1,)))
            du_ref[:, cols] = acc.astype(du_ref.dtype)

    whole3 = lambda a: pl.BlockSpec(a.shape, lambda b_, c: (0, 0, 0))
    rowblk = pl.BlockSpec((CHUNK, width), lambda b_, c: (b_ * nc + nc - 1 - c, 0))
    scr = pltpu.VMEM((S5_Q, CHUNK * ns, LANES), F32)
    return pl.pallas_call(
        body, name=name, grid=(nb, nc),
        in_specs=[rowblk, rowblk,
                  pl.BlockSpec((None, 2, S5_Q, ns, LANES), lambda b_, c: (b_ * nc + nc - 1 - c, 0, 0, 0, 0)),
                  whole3(bre), whole3(bim), whole3(cre), whole3(cim), whole3(ar), whole3(ai),
                  pl.BlockSpec((1, width), lambda b_, c: (0, 0))],
        out_specs=[rowblk, whole3(bre), whole3(bim), whole3(cre), whole3(cim), whole3(ar), whole3(ai),
                   pl.BlockSpec((1, width), lambda b_, c: (0, 0))],
        out_shape=[jax.ShapeDtypeStruct((rows, width), BF16), jax.ShapeDtypeStruct(bre.shape, F32),
                   jax.ShapeDtypeStruct(bim.shape, F32), jax.ShapeDtypeStruct(cre.shape, F32),
                   jax.ShapeDtypeStruct(cim.shape, F32), jax.ShapeDtypeStruct(ar.shape, F32),
                   jax.ShapeDtypeStruct(ai.shape, F32), jax.ShapeDtypeStruct((1, width), F32)],
        scratch_shapes=[scr, scr, scr, scr, pltpu.VMEM((2, S5_Q, ns, LANES), F32), pltpu.VMEM((2, S5_Q, ns, LANES), F32)],
        compiler_params=_params(("arbitrary", "arbitrary")))(pa, dys, states, bre, bim, cre, cim, ar, ai, dvec)


def _s5_discretize(lam_re, lam_im, log_dt, b_re, b_im):
    dt = jnp.exp(log_dt)[:, None]
    mag = jnp.exp(lam_re * dt)
    ar, ai = mag * jnp.cos(lam_im * dt), mag * jnp.sin(lam_im * dt)
    den = lam_re * lam_re + lam_im * lam_im
    qr = ((ar - 1.0) * lam_re + ai * lam_im) / den
    qi = (ai * lam_re - (ar - 1.0) * lam_im) / den
    bbr = qr[..., None] * b_re - qi[..., None] * b_im
    bbi = qr[..., None] * b_im + qi[..., None] * b_re
    return ar, ai, bbr, bbi


def _s5_expand(ar, ai, bbr, bbi, c_re, c_im):
    g, p, h = bbr.shape
    gps = LANES // h
    ns = g // gps
    eye = jnp.eye(gps, dtype=F32)

    def bexp(b):
        return jnp.einsum("sgph,gk->sghkp", b.reshape(ns, gps, p, h), eye).reshape(ns, gps * h, gps * p)

    def cexp(c):
        return jnp.einsum("sghp,gk->sgpkh", c.reshape(ns, gps, h, p), eye).reshape(ns, gps * p, gps * h)

    def aexp(a):
        return a.reshape(ns, S5_Q, LANES).transpose(1, 0, 2)

    return (bexp(bbr).astype(BF16), bexp(bbi).astype(BF16), cexp(c_re).astype(BF16), cexp(c_im).astype(BF16),
            aexp(ar), aexp(ai))


def _s5_contract(dbre, dbim, dcre, dcim, dar, dai, g, p, h):
    gps = LANES // h
    ns = g // gps
    eye = jnp.eye(gps, dtype=F32)
    bcon = lambda d: jnp.einsum("sghkp,gk->sgph", d.reshape(ns, gps, h, gps, p), eye).reshape(g, p, h)
    ccon = lambda d: jnp.einsum("sgpkh,gk->sghp", d.reshape(ns, gps, p, gps, h), eye).reshape(g, h, p)
    acon = lambda d: d.transpose(1, 0, 2).reshape(g, p)
    return bcon(dbre), bcon(dbim), ccon(dcre), ccon(dcim), acon(dar), acon(dai)


def _ml_proj_tile(cpre, xb, wq, wk, wv, gq, gk, gv):
    xc = _silu(cpre)
    q = _dot_nn(xc, wq)
    k = _dot_nn(xc, wk)
    v = _dot_nn(xb, wv)
    return q, k, v, _dot_nn(q, gq) + _dot_nn(k, gk) + _dot_nn(v, gv)


def ml_proj_fwd(cpre, xb, wq, wk, wv, gq, gk, gv, name):
    rows, width = cpre.shape
    nblk = width // LANES
    tr = _tile(rows, 1088, 16)

    def body(c_ref, x_ref, wq_ref, wk_ref, wv_ref, gq_ref, gk_ref, gv_ref, q_ref, k_ref, v_ref, g_ref):
        j = pl.program_id(1)
        q, k, v, g = _ml_proj_tile(c_ref[...], x_ref[...], wq_ref[...], wk_ref[...], wv_ref[...],
                                   gq_ref[...], gk_ref[...], gv_ref[...])
        q_ref[...] = q
        k_ref[...] = k
        v_ref[...] = v

        @pl.when(j == 0)
        def _():
            g_ref[...] = jnp.zeros_like(g_ref)

        g_ref[...] += g

    rb = pl.BlockSpec((tr, LANES), lambda i, j: (i, j))
    wb = pl.BlockSpec((None, LANES, LANES), lambda i, j: (j, 0, 0))
    return pl.pallas_call(
        body, name=name, grid=(rows // tr, nblk), in_specs=[rb, rb, wb, wb, wb, wb, wb, wb],
        out_specs=[rb, rb, rb, pl.BlockSpec((tr, LANES), lambda i, j: (i, 0))],
        out_shape=[jax.ShapeDtypeStruct((rows, width), F32)] * 3 + [jax.ShapeDtypeStruct((rows, LANES), F32)],
        compiler_params=_params(("arbitrary", "arbitrary")))(cpre, xb, wq, wk, wv, gq, gk, gv)


def ml_proj_bwd(cpre, xb, wq, wk, wv, gq, gk, gv, dq, dk, dv, dg, dcp_extra, name):
    rows, width = cpre.shape
    nblk = width // LANES
    tr = _tile(rows, 1088, 16)

    def body(c_ref, x_ref, wq_ref, wk_ref, wv_ref, gq_ref, gk_ref, gv_ref, dq_ref, dk_ref, dv_ref, dg_ref, e_ref,
             dc_ref, dx_ref, *dw_refs):
        i = pl.program_id(1)
        _, vjp = jax.vjp(_ml_proj_tile, c_ref[...], x_ref[...], wq_ref[...], wk_ref[...], wv_ref[...],
                         gq_ref[...], gk_ref[...], gv_ref[...])
        grads = vjp((dq_ref[...], dk_ref[...], dv_ref[...], dg_ref[...]))
        dc_ref[...] = grads[0] + e_ref[...]
        dx_ref[...] = grads[1]

        @pl.when(i == 0)
        def _():
            for r in dw_refs:
                r[...] = jnp.zeros_like(r)

        for r, gval in zip(dw_refs, grads[2:]):
            r[...] += gval

    rb = pl.BlockSpec((tr, LANES), lambda j, i: (i, j))
    wb = pl.BlockSpec((None, LANES, LANES), lambda j, i: (j, 0, 0))
    gb = pl.BlockSpec((tr, LANES), lambda j, i: (i, 0))
    wshape = jax.ShapeDtypeStruct((nblk, LANES, LANES), F32)
    return pl.pallas_call(
        body, name=name, grid=(nblk, rows // tr), in_specs=[rb, rb, wb, wb, wb, wb, wb, wb, rb, rb, rb, gb, rb],
        out_specs=[rb, rb] + [wb] * 6,
        out_shape=[jax.ShapeDtypeStruct((rows, width), F32)] * 2 + [wshape] * 6,
        compiler_params=_params(("arbitrary", "arbitrary")))(cpre, xb, wq, wk, wv, gq, gk, gv, dq, dk, dv, dg, dcp_extra)


def _ml_gates_tile(gl, bg, nh):
    x = gl + bg
    bcum = _dot(_tri(CHUNK), _log_sigmoid(x), ((1,), (0,)), precision=HI)
    lane = lax.broadcasted_iota(jnp.int32, x.shape, 1)
    return jnp.where(lane < nh, x, jnp.where(lane < 2 * nh, bcum, 0.0))


def _ml_core_tile(q, k, v, colg, rowg, cpre, zb, nw, sk, cst, nst, m_prev):
    c, dh = q.shape
    igc, bc = _lane_pick(colg, 0), _lane_pick(colg, 1)
    igr, br = _row_pick(rowg, 0), _row_pick(rowg, 1)
    causal = _tri(c) > 0
    dmat = jnp.where(causal, bc - br + igr, -jnp.inf)
    inter = bc + m_prev
    mt = lax.stop_gradient(jnp.maximum(inter, jnp.max(dmat, axis=1, keepdims=True)))
    wt = jnp.exp(dmat - mt)
    w_prev = jnp.exp(inter - mt)
    qs = q * (dh ** -0.5)
    s = _dot_nt(qs, k) * wt
    num = _dot_nn(s, v) + w_prev * _dot_nn(qs, cst)
    den = jnp.sum(s, axis=1, keepdims=True) + w_prev * jnp.sum(qs * nst, axis=1, keepdims=True)
    h = num * (1.0 / jnp.maximum(jnp.abs(den), jnp.exp(-mt)))
    last = (lax.broadcasted_iota(jnp.int32, (c, 1), 0) == c - 1).astype(F32)
    blast = jnp.sum(bc * last, axis=0, keepdims=True)
    g = blast - bc + igc
    m_new = lax.stop_gradient(jnp.maximum(blast + m_prev, jnp.max(g, axis=0, keepdims=True)))
    decay = jnp.exp(blast + m_prev - m_new)
    wk = jnp.exp(g - m_new) * k
    c_new = decay * cst + _dot_tn(wk, v)
    n_new = decay * nst + jnp.sum(wk, axis=0, keepdims=True)
    mu = jnp.mean(h, axis=1, keepdims=True)
    hc = h - mu
    var = jnp.mean(hc * hc, axis=1, keepdims=True)
    out = hc * lax.rsqrt(var + HEAD_NORM_EPS) * nw + sk * _silu(cpre)
    return out * _silu(zb), c_new, n_new, m_new


def _ml_core_specs(nc, dh, rev):
    ch = (lambda c: nc - 1 - c) if rev else (lambda c: c)
    rb = pl.BlockSpec((CHUNK, dh), lambda b_, c, h: (b_ * nc + ch(c), h))
    colb = pl.BlockSpec((None, CHUNK, 2), lambda b_, c, h: (h, b_ * nc + ch(c), 0))
    rowb = pl.BlockSpec((None, None, 2, CHUNK), lambda b_, c, h: (b_ * nc + ch(c), h, 0, 0))
    pb = pl.BlockSpec((1, dh), lambda b_, c, h: (0, h))
    cb = pl.BlockSpec((None, None, dh, dh), lambda b_, c, h: (b_ * nc + ch(c), h, 0, 0))
    nb_ = pl.BlockSpec((None, None, 1, dh), lambda b_, c, h: (b_ * nc + ch(c), h, 0, 0))
    mb = pl.BlockSpec((None, None, 1, 1), lambda b_, c, h: (b_ * nc + ch(c), h, 0, 0))
    return rb, colb, rowb, pb, cb, nb_, mb


def ml_core_fwd(q, k, v, colg, rowg, cpre, zb, nw, sk, nb, nh, name):
    rows, width = q.shape
    dh = width // nh
    nc = rows // nb // CHUNK
    rb, colb, rowb, pb, cb, nb_, mb = _ml_core_specs(nc, dh, False)

    def body(q_ref, k_ref, v_ref, col_ref, row_ref, c_ref, z_ref, nw_ref, sk_ref, y_ref, cs_ref, ns_ref, ms_ref,
             cst_ref, nst_ref, mst_ref):
        c = pl.program_id(1)
        h = pl.program_id(2)

        @pl.when(c == 0)
        def _():
            cst_ref[h] = jnp.zeros((dh, dh), F32)
            nst_ref[h] = jnp.zeros((1, dh), F32)
            mst_ref[h] = jnp.zeros((1, 1), F32)

        cst, nst, m_prev = cst_ref[h], nst_ref[h], mst_ref[h]
        cs_ref[...] = cst
        ns_ref[...] = nst
        ms_ref[...] = m_prev
        y, c_new, n_new, m_new = _ml_core_tile(q_ref[...], k_ref[...], v_ref[...], col_ref[...], row_ref[...],
                                               c_ref[...], z_ref[...], nw_ref[...], sk_ref[...], cst, nst, m_prev)
        y_ref[...] = y.astype(BF16)
        cst_ref[h] = c_new
        nst_ref[h] = n_new
        mst_ref[h] = m_new

    nbc = nb * nc
    return pl.pallas_call(
        body, name=name, grid=(nb, nc, nh), in_specs=[rb, rb, rb, colb, rowb, rb, rb, pb, pb],
        out_specs=[rb, cb, nb_, mb],
        out_shape=[jax.ShapeDtypeStruct((rows, width), BF16), jax.ShapeDtypeStruct((nbc, nh, dh, dh), F32),
                   jax.ShapeDtypeStruct((nbc, nh, 1, dh), F32), jax.ShapeDtypeStruct((nbc, nh, 1, 1), F32)],
        scratch_shapes=[pltpu.VMEM((nh, dh, dh), F32), pltpu.VMEM((nh, 1, dh), F32), pltpu.VMEM((nh, 1, 1), F32)],
        compiler_params=_params(("arbitrary", "arbitrary", "arbitrary")))(q, k, v, colg, rowg, cpre, zb, nw, sk)


def ml_core_bwd(q, k, v, colg, rowg, cpre, zb, nw, sk, cs, ns, ms, dy, nb, nh, name):
    rows, width = q.shape
    dh = width // nh
    nc = rows // nb // CHUNK
    rb, colb, rowb, pb, cb, nb_, mb = _ml_core_specs(nc, dh, True)

    def body(q_ref, k_ref, v_ref, col_ref, row_ref, c_ref, z_ref, nw_ref, sk_ref, cs_ref, ns_ref, ms_ref, dy_ref,
             dq_ref, dk_ref, dv_ref, dc_ref, dz_ref, dcol_ref, drow_ref, dnw_ref, dsk_ref, dcst_ref, dnst_ref):
        bb = pl.program_id(0)
        step = pl.program_id(1)
        h = pl.program_id(2)

        @pl.when(jnp.logical_and(bb == 0, jnp.logical_and(step == 0, h == 0)))
        def _():
            dnw_ref[...] = jnp.zeros_like(dnw_ref)
            dsk_ref[...] = jnp.zeros_like(dsk_ref)

        @pl.when(step == 0)
        def _():
            dcst_ref[h] = jnp.zeros((dh, dh), F32)
            dnst_ref[h] = jnp.zeros((1, dh), F32)

        m_prev = ms_ref[...]

        def f(*a):
            return _ml_core_tile(*a, m_prev)[:3]

        _, vjp = jax.vjp(f, q_ref[...], k_ref[...], v_ref[...], col_ref[...], row_ref[...], c_ref[...], z_ref[...],
                         nw_ref[...], sk_ref[...], cs_ref[...], ns_ref[...])
        g = vjp((dy_ref[...], dcst_ref[h], dnst_ref[h]))
        dq_ref[...] = g[0]
        dk_ref[...] = g[1]
        dv_ref[...] = g[2]
        dcol_ref[...] = g[3]
        drow_ref[...] = g[4]
        dc_ref[...] = g[5]
        dz_ref[...] = g[6].astype(dz_ref.dtype)
        dnw_ref[h] += g[7]
        dsk_ref[h] += g[8]
        dcst_ref[h] = g[9]
        dnst_ref[h] = g[10]

    nbc = nb * nc
    accb = pl.BlockSpec((nh, 1, dh), lambda b_, c, h: (0, 0, 0))
    return pl.pallas_call(
        body, name=name, grid=(nb, nc, nh), in_specs=[rb, rb, rb, colb, rowb, rb, rb, pb, pb, cb, nb_, mb, rb],
        out_specs=[rb, rb, rb, rb, rb, colb, rowb, accb, accb],
        out_shape=[jax.ShapeDtypeStruct((rows, width), F32)] * 4 + [jax.ShapeDtypeStruct((rows, width), BF16)]
        + [jax.ShapeDtypeStruct(colg.shape, F32), jax.ShapeDtypeStruct(rowg.shape, F32),
           jax.ShapeDtypeStruct((nh, 1, dh), F32), jax.ShapeDtypeStruct((nh, 1, dh), F32)],
        scratch_shapes=[pltpu.VMEM((nh, dh, dh), F32), pltpu.VMEM((nh, 1, dh), F32)],
        compiler_params=_params(("arbitrary", "arbitrary", "arbitrary")))(
            q, k, v, colg, rowg, cpre, zb, nw, sk, cs, ns, ms, dy)


def _ssd_dt_tile(dtr, bias, alog):
    dt = _softplus(dtr + bias)
    cum = _dot(_tri(CHUNK), dt * (-jnp.exp(alog)), ((1,), (0,)), precision=HI)
    return dt, cum


def _ssd_tile(xcs, bmc, cmc, cols, rows_, z, dvec, gn, states, hpg):
    npair = hpg // 2
    hd = SSD_HEAD_DIM
    xs = [_silu(x) for x in xcs]
    bm, cm = _silu(bmc), _silu(cmc)
    cb = _dot_nt(cm, bm)
    causal = _tri(CHUNK) > 0
    lane_lo = lax.broadcasted_iota(jnp.int32, (1, 2 * hd), 1) < hd
    lastsel = (lax.broadcasted_iota(jnp.int32, (CHUNK, 1), 0) == CHUNK - 1).astype(F32)
    heads = []
    for r in range(hpg):
        dtc, cumc = _lane_pick(cols, r), _lane_pick(cols, hpg + r)
        dtrow, cumr = _row_pick(rows_, r), _row_pick(rows_, hpg + r)
        w = cb * jnp.exp(jnp.where(causal, cumc - cumr, -jnp.inf)) * dtrow
        last = jnp.sum(cumc * lastsel, axis=0, keepdims=True)
        heads.append((w, jnp.exp(cumc), jnp.exp(last - cumc) * dtc, jnp.exp(last)))
    ys, new_states = [], []
    for j in range(npair):
        (wa, ea, da, la), (wb, eb, db, lb) = heads[2 * j], heads[2 * j + 1]
        yi = jnp.where(lane_lo, _dot_nn(wa, xs[j]), _dot_nn(wb, xs[j]))
        ys.append(yi + jnp.where(lane_lo, ea, eb) * _dot_nn(cm, states[j]))
        xd = xs[j] * jnp.where(lane_lo, da, db)
        new_states.append(jnp.where(lane_lo, la, lb) * states[j] + _dot_tn(bm, xd))
    y = jnp.concatenate(ys, axis=1) + dvec * jnp.concatenate(xs, axis=1)
    yg = y * _silu(z)
    yn = yg * lax.rsqrt(jnp.mean(yg * yg, axis=1, keepdims=True) + NORM_EPS) * gn
    return yn, new_states


def _ssd_specs(nc, hpg, ng, rev):
    npair = hpg // 2
    gw = hpg * SSD_HEAD_DIM
    xblocks = ng * npair
    ch = (lambda c: nc - 1 - c) if rev else (lambda c: c)
    xs = [pl.BlockSpec((CHUNK, LANES), functools.partial(lambda b_, c, g, jj: (b_ * nc + ch(c), g * npair + jj), jj=j))
          for j in range(npair)]
    bmb = pl.BlockSpec((CHUNK, SSD_STATE), lambda b_, c, g: (b_ * nc + ch(c), xblocks + g))
    cmb = pl.BlockSpec((CHUNK, SSD_STATE), lambda b_, c, g: (b_ * nc + ch(c), xblocks + ng + g))
    colb = pl.BlockSpec((None, CHUNK, 2 * hpg), lambda b_, c, g: (g, b_ * nc + ch(c), 0))
    rowb = pl.BlockSpec((None, None, 2 * hpg, CHUNK), lambda b_, c, g: (b_ * nc + ch(c), g, 0, 0))
    zb = pl.BlockSpec((CHUNK, gw), lambda b_, c, g: (b_ * nc + ch(c), g))
    pb = pl.BlockSpec((1, gw), lambda b_, c, g: (0, g))
    sb = pl.BlockSpec((None, None, npair, SSD_STATE, 2 * SSD_HEAD_DIM), lambda b_, c, g: (b_ * nc + ch(c), g, 0, 0, 0))
    return xs, bmb, cmb, colb, rowb, zb, pb, sb


def ssd_core_fwd(cpre, cols, rows_, z, dvec, gn, nb, hpg, name):
    rows = cpre.shape[0]
    inner = z.shape[1]
    ng = inner // (hpg * SSD_HEAD_DIM)
    npair = hpg // 2
    nc = rows // nb // CHUNK
    xs, bmb, cmb, colb, rowb, zb, pb, sb = _ssd_specs(nc, hpg, ng, False)

    def body(*refs):
        x_refs = refs[:npair]
        bm_ref, cm_ref, col_ref, row_ref, z_ref, d_ref, gn_ref, y_ref, so_ref, st_ref = refs[npair:]
        c = pl.program_id(1)
        g = pl.program_id(2)

        @pl.when(c == 0)
        def _():
            st_ref[g] = jnp.zeros((npair, SSD_STATE, 2 * SSD_HEAD_DIM), F32)

        so_ref[...] = st_ref[g]
        states = [st_ref[g, j] for j in range(npair)]
        yn, new_states = _ssd_tile([r[...] for r in x_refs], bm_ref[...], cm_ref[...], col_ref[...], row_ref[...],
                                   z_ref[...], d_ref[...], gn_ref[...], states, hpg)
        y_ref[...] = yn.astype(BF16)
        for j in range(npair):
            st_ref[g, j] = new_states[j]

    return pl.pallas_call(
        body, name=name, grid=(nb, nc, ng), in_specs=xs + [bmb, cmb, colb, rowb, zb, pb, pb],
        out_specs=[zb, sb],
        out_shape=[jax.ShapeDtypeStruct((rows, inner), BF16),
                   jax.ShapeDtypeStruct((nb * nc, ng, npair, SSD_STATE, 2 * SSD_HEAD_DIM), F32)],
        scratch_shapes=[pltpu.VMEM((ng, npair, SSD_STATE, 2 * SSD_HEAD_DIM), F32)],
        compiler_params=_params(("arbitrary", "arbitrary", "arbitrary")))(
            *([cpre] * npair), cpre, cpre, cols, rows_, z, dvec, gn)


def ssd_core_bwd(cpre, cols, rows_, z, dvec, gn, states, dyn, nb, hpg, name):
    rows = cpre.shape[0]
    inner = z.shape[1]
    gw = hpg * SSD_HEAD_DIM
    ng = inner // gw
    npair = hpg // 2
    nc = rows // nb // CHUNK
    xs, bmb, cmb, colb, rowb, zb, pb, sb = _ssd_specs(nc, hpg, ng, True)

    def body(*refs):
        x_refs = refs[:npair]
        (bm_ref, cm_ref, col_ref, row_ref, z_ref, d_ref, gn_ref, s_ref, dy_ref,
         dx_ref, dbm_ref, dcm_ref, dcol_ref, drow_ref, dz_ref, dd_ref, dgn_ref, dst_ref) = refs[npair:]
        bb = pl.program_id(0)
        step = pl.program_id(1)
        g = pl.program_id(2)

        @pl.when(jnp.logical_and(bb == 0, jnp.logical_and(step == 0, g == 0)))
        def _():
            dd_ref[...] = jnp.zeros_like(dd_ref)
            dgn_ref[...] = jnp.zeros_like(dgn_ref)

        @pl.when(step == 0)
        def _():
            dst_ref[g] = jnp.zeros((npair, SSD_STATE, 2 * SSD_HEAD_DIM), F32)

        def f(xcs, bmc, cmc, cv, rv, zv, dv_, gv, sts):
            return _ssd_tile(xcs, bmc, cmc, cv, rv, zv, dv_, gv, sts, hpg)

        _, vjp = jax.vjp(f, [r[...] for r in x_refs], bm_ref[...], cm_ref[...], col_ref[...], row_ref[...], z_ref[...],
                         d_ref[...], gn_ref[...], [s_ref[j] for j in range(npair)])
        gr = vjp((dy_ref[...], [dst_ref[g, j] for j in range(npair)]))
        dx_ref[...] = jnp.concatenate(gr[0], axis=1)
        dbm_ref[...] = gr[1]
        dcm_ref[...] = gr[2]
        dcol_ref[...] = gr[3]
        drow_ref[...] = gr[4]
        dz_ref[...] = gr[5].astype(dz_ref.dtype)
        dd_ref[g] += gr[6]
        dgn_ref[g] += gr[7]
        for j in range(npair):
            dst_ref[g, j] = gr[8][j]

    ch = lambda c: nc - 1 - c
    nblk = pl.BlockSpec((CHUNK, SSD_STATE), lambda b_, c, g: (b_ * nc + ch(c), g))
    accb = pl.BlockSpec((ng, 1, gw), lambda b_, c, g: (0, 0, 0))
    return pl.pallas_call(
        body, name=name, grid=(nb, nc, ng), in_specs=xs + [bmb, cmb, colb, rowb, zb, pb, pb, sb, zb],
        out_specs=[zb, nblk, nblk, colb, rowb, zb, accb, accb],
        out_shape=[jax.ShapeDtypeStruct((rows, inner), F32), jax.ShapeDtypeStruct((rows, ng * SSD_STATE), F32),
                   jax.ShapeDtypeStruct((rows, ng * SSD_STATE), F32), jax.ShapeDtypeStruct(cols.shape, F32),
                   jax.ShapeDtypeStruct(rows_.shape, F32), jax.ShapeDtypeStruct((rows, inner), BF16),
                   jax.ShapeDtypeStruct((ng, 1, gw), F32), jax.ShapeDtypeStruct((ng, 1, gw), F32)],
        scratch_shapes=[pltpu.VMEM((ng, npair, SSD_STATE, 2 * SSD_HEAD_DIM), F32)],
        compiler_params=_params(("arbitrary", "arbitrary", "arbitrary")))(
            *([cpre] * npair), cpre, cpre, cols, rows_, z, dvec, gn, states, dyn)


def _hw_expand(w):
    n, bi, _ = w.shape
    per = LANES // bi
    eye = jnp.eye(per, dtype=F32)
    return jnp.einsum("jbio,bc->jbico", w.reshape(n // per, per, bi, bi), eye).reshape(n // per, LANES, LANES)


def _hw_contract(d, bi=QKV_BLOCK):
    per = LANES // bi
    eye = jnp.eye(per, dtype=F32)
    return jnp.einsum("jbico,bc->jbio", d.reshape(d.shape[0], per, bi, per, bi), eye).reshape(-1, bi, bi)


def _wg_expand(wg, width):
    pad = jnp.pad(wg, ((0, 0), (0, LANES - wg.shape[1])))
    return [pad[i * width:(i + 1) * width].reshape(width // LANES, LANES, LANES) for i in range(3)]


def _wg_contract(dgs, ngate):
    return jnp.concatenate([d[:, :, :ngate].reshape(-1, ngate) for d in dgs], axis=0)


def _pad_lanes(a):
    return jnp.pad(a, ((0, 0), (0, LANES - a.shape[1])))


def _pairs_to_layouts(first, second, ngrp, per, nbc):
    rows = first.shape[0]
    both = jnp.concatenate([first.reshape(rows, ngrp, per), second.reshape(rows, ngrp, per)], axis=2)
    return both.transpose(1, 0, 2), both.reshape(nbc, CHUNK, ngrp, 2 * per).transpose(0, 2, 3, 1)


def _layouts_to_pairs(dcols, drows, ngrp, per):
    rows = dcols.shape[1]
    both = dcols.transpose(1, 0, 2) + drows.transpose(0, 3, 1, 2).reshape(rows, ngrp, 2 * per)
    return both[:, :, :per].reshape(rows, ngrp * per), both[:, :, per:].reshape(rows, ngrp * per)


_EARLY = ("W0a", "W0xb", "W0zb", "glu")
_LATE = ("Wo0a", "Wo0b", "W1z", "W1x", "W1dt", "Wo1")


def _local_step(x, target, bw, sp, late_weights=None, late_grads=None, early_grads=None):
    nb, seq, d = x.shape
    nh, hpg = MLSTM_HEADS, SSD_HPG
    t_len = N_META + seq
    nc = -(-t_len // CHUNK)
    tp = nc * CHUNK
    rows = nb * tp
    nbc = nb * nc
    meta = sp["meta_tokens"]
    h0 = jnp.concatenate([jnp.broadcast_to(meta[None], (nb, N_META, d)), x, jnp.zeros((nb, tp - t_len, d), F32)], axis=1)
    h0 = h0.reshape(rows, d)
    tgt = jnp.pad(target, ((0, 0), (N_META, tp - t_len), (0, 0))).reshape(rows, d)

    n0 = norm_fwd(h0, sp["ab_norm"], "norm0")
    pa = mm(n0, bw["W0a"], "nn", "mm_pa")
    xb = mm(n0, bw["W0xb"], "nn", "mm_xb")
    zb = mm(n0, bw["W0zb"], "nn", "mm_zb")
    s5w = pa.shape[1] // 2
    mlw = xb.shape[1]
    s5_args = (sp["s5_lambda_re"], sp["s5_lambda_im"], sp["s5_log_dt"].reshape(-1), sp["s5_b_re"], sp["s5_b_im"])
    (ar, ai, bbr, bbi), s5_disc_vjp = jax.vjp(_s5_discretize, *s5_args)
    sg, spn, shh = bbr.shape
    bre, bim, cre, cim, are, aie = _s5_expand(ar, ai, bbr, bbi, sp["s5_c_re"], sp["s5_c_im"])
    ys5, gb, s5st = s5_fwd(pa, bre, bim, cre, cim, are, aie, sp["s5_d"], nb, "s5_fwd")
    tglu = mm(gb, bw["glu"], "nn", "mm_glu")

    def glu_tile(ys, tt, za, gbias):
        return _gelu(ys) * _sigmoid(tt + gbias) * _silu(za)

    ya = rowwise("glu_fwd", lambda i, ys, tt, pab, gbias: glu_tile(ys, tt, pab[:, s5w:], gbias),
                 [ys5, tglu, pa], [sp["s5_glu_b"]], [(s5w, BF16)], tr=_tile(rows, 256, 16))[0]

    cpre0 = conv_fwd(xb, sp["ml_conv_w"], sp["ml_conv_b"], nb, "ml_conv_fwd")
    wq_e, wk_e, wv_e = _hw_expand(sp["ml_wq"]), _hw_expand(sp["ml_wk"]), _hw_expand(sp["ml_wv"])
    gq, gk, gv = _wg_expand(sp["ml_w_gate"], mlw)
    q, k, v, gl = ml_proj_fwd(cpre0, xb, wq_e, wk_e, wv_e, gq, gk, gv, "ml_proj_fwd")
    bgate = _pad_lanes(sp["ml_b_gate"])
    gout = rowwise("ml_gates_fwd", lambda i, g_, b_: _ml_gates_tile(g_, b_, nh), [gl], [bgate], [(LANES, F32)], tr=CHUNK)[0]
    colg, rowg = _pairs_to_layouts(gout[:, :nh], gout[:, nh:2 * nh], nh, 1, nbc)
    yb, ml_cs, ml_ns, ml_ms = ml_core_fwd(q, k, v, colg, rowg, cpre0, zb, sp["ml_norm"], sp["ml_skip"], nb, nh, "ml_core_fwd")
    if late_weights is not None:
        bw = {**bw, **late_weights()}
    h1 = mm(ya, bw["Wo0a"], "nn", "mm_out0a", resid=h0)
    h1 = mm(yb, bw["Wo0b"], "nn", "mm_out0b", resid=h1)

    n1 = norm_fwd(h1, sp["ssd_norm"], "norm1")
    z1 = mm(n1, bw["W1z"], "nn", "mm_z1")
    xbc = mm(n1, bw["W1x"], "nn", "mm_xbc")
    dtr = mm(n1, bw["W1dt"], "nn", "mm_dt")
    inner = z1.shape[1]
    ng = inner // (hpg * SSD_HEAD_DIM)
    nhd = ng * hpg
    cpre1 = conv_fwd(xbc, sp["ssd_conv_w"], sp["ssd_conv_b"], nb, "ssd_conv_fwd")
    dt_bias, a_log = _pad_lanes(sp["ssd_dt_bias"]), _pad_lanes(sp["ssd_a_log"])
    dt, cum = rowwise("ssd_dt_fwd", lambda i, r_, b_, a_: _ssd_dt_tile(r_, b_, a_), [dtr], [dt_bias, a_log],
                      [(LANES, F32), (LANES, F32)], tr=CHUNK)
    cols, rws = _pairs_to_layouts(dt[:, :nhd], cum[:, :nhd], ng, hpg, nbc)
    dvec = jnp.repeat(sp["ssd_d"], SSD_HEAD_DIM, axis=1)
    yn, ssd_st = ssd_core_fwd(cpre1, cols, rws, z1, dvec, sp["ssd_gnorm"], nb, hpg, "ssd_core_fwd")
    h2 = mm(yn, bw["Wo1"], "nn", "mm_out1", resid=h1)

    tr_l = _tile(tp, 256, 16)
    per_ex = tp // tr_l

    def loss_tile(i, hb, tb, gfn):
        tpos = (i % per_ex) * tr_l + lax.broadcasted_iota(jnp.int32, (tr_l, 1), 0)
        mask = jnp.logical_and(tpos >= N_META, tpos < t_len).astype(F32)

        def lf(hh, gg):
            e = (_rms(hh, gg) - tb) * mask
            return 0.5 * jnp.sum(e * e) / d

        lval, (dh, dg) = jax.value_and_grad(lf, (0, 1))(hb, gfn)
        return dh, dh, jnp.full((1, LANES), lval, F32), dg

    fn = sp["final_norm"].reshape(1, d)
    dh2, dh2b, loss_acc, dfn = rowwise("loss", loss_tile, [h2, tgt], [fn], [(d, F32), (d, BF16)], [(1, LANES), (1, d)], tr=tr_l)

    gbig, gs = {}, {}
    gs["final_norm"] = dfn.reshape(sp["final_norm"].shape)
    dyn = mm(dh2b, bw["Wo1"], "nt", "mm_dyn")
    gbig["Wo1"] = mm(yn, dh2b, "tn", "mm_dWo1", out_dtype=BF16)
    dxs, dbm, dcm, dcols, drws, dz1, ddvec, dgn = ssd_core_bwd(cpre1, cols, rws, z1, dvec, sp["ssd_gnorm"], ssd_st, dyn,
                                                              nb, hpg, "ssd_core_bwd")
    gs["ssd_d"] = ddvec.reshape(1, nhd, SSD_HEAD_DIM).sum(axis=2)
    gs["ssd_gnorm"] = dgn.reshape(1, inner)
    ddt, dcum = _layouts_to_pairs(dcols, drws, ng, hpg)

    def ssd_dt_bwd_tile(i, r_, ddt_, dcum_, b_, a_):
        _, vjp = jax.vjp(_ssd_dt_tile, r_, b_, a_)
        return vjp((ddt_, dcum_))

    ddtr, dbias, dalog = rowwise("ssd_dt_bwd", ssd_dt_bwd_tile, [dtr, _pad_lanes(ddt), _pad_lanes(dcum)], [dt_bias, a_log],
                                 [(LANES, BF16)], [(1, LANES), (1, LANES)], tr=CHUNK)
    gs["ssd_dt_bias"] = dbias[:, :nhd]
    gs["ssd_a_log"] = dalog[:, :nhd]
    dcpre1 = jnp.concatenate([dxs, dbm, dcm], axis=1)
    dxbc, dcw1, dcb1 = conv_bwd(dcpre1, xbc, sp["ssd_conv_w"], nb, "ssd_conv_bwd")
    gs["ssd_conv_w"] = dcw1
    gs["ssd_conv_b"] = dcb1
    dn1 = mm(dz1, bw["W1z"], "nt", "mm_dn1z")
    dn1 = mm(dxbc, bw["W1x"], "nt", "mm_dn1x", resid=dn1)
    dn1 = mm(ddtr, bw["W1dt"], "nt", "mm_dn1dt", resid=dn1)
    gbig["W1z"] = mm(n1, dz1, "tn", "mm_dW1z", out_dtype=BF16)
    gbig["W1x"] = mm(n1, dxbc, "tn", "mm_dW1x", out_dtype=BF16)
    gbig["W1dt"] = mm(n1, ddtr, "tn", "mm_dW1dt", out_dtype=BF16)
    dh1, dh1b, dg1 = norm_bwd(h1, sp["ssd_norm"], dn1, dh2, "norm1_bwd")
    gs["ssd_norm"] = dg1

    gbig["Wo0a"] = mm(ya, dh1b, "tn", "mm_dWo0a", out_dtype=BF16)
    gbig["Wo0b"] = mm(yb, dh1b, "tn", "mm_dWo0b", out_dtype=BF16)
    if late_grads is not None:
        late_grads({n: gbig[n] for n in _LATE})
    dya = mm(dh1b, bw["Wo0a"], "nt", "mm_dya")
    dyb = mm(dh1b, bw["Wo0b"], "nt", "mm_dyb")
    (dq, dk, dv, dcp_skip, dzb, dcolg, drowg, dnw, dsk) = ml_core_bwd(
        q, k, v, colg, rowg, cpre0, zb, sp["ml_norm"], sp["ml_skip"], ml_cs, ml_ns, ml_ms, dyb, nb, nh, "ml_core_bwd")
    gs["ml_norm"] = dnw.reshape(1, mlw)
    gs["ml_skip"] = dsk.reshape(1, mlw)
    dig, dbcum = _layouts_to_pairs(dcolg, drowg, nh, 1)
    dgout = _pad_lanes(jnp.concatenate([dig, dbcum], axis=1))

    def ml_gates_bwd_tile(i, g_, dgo, b_):
        _, vjp = jax.vjp(lambda a, b: _ml_gates_tile(a, b, nh), g_, b_)
        return vjp(dgo)

    dgl, dbg = rowwise("ml_gates_bwd", ml_gates_bwd_tile, [gl, dgout], [bgate], [(LANES, F32)], [(1, LANES)], tr=CHUNK)
    gs["ml_b_gate"] = dbg[:, :2 * nh]
    dcpre0, dxb_v, dwq, dwk, dwv, dgq, dgk, dgv = ml_proj_bwd(cpre0, xb, wq_e, wk_e, wv_e, gq, gk, gv, dq, dk, dv, dgl,
                                                            dcp_skip, "ml_proj_bwd")
    gs["ml_wq"], gs["ml_wk"], gs["ml_wv"] = _hw_contract(dwq), _hw_contract(dwk), _hw_contract(dwv)
    gs["ml_w_gate"] = _wg_contract([dgq, dgk, dgv], 2 * nh)
    dxb, dcw0, dcb0 = conv_bwd(dcpre0, xb, sp["ml_conv_w"], nb, "ml_conv_bwd", resid=dxb_v)
    gs["ml_conv_w"] = dcw0
    gs["ml_conv_b"] = dcb0

    def glu_bwd_tile(i, ys, tt, pab, dy_, gbias):
        _, vjp = jax.vjp(glu_tile, ys, tt, pab[:, s5w:], gbias)
        return vjp(dy_)

    dys_direct, dtglu, dza, dglub = rowwise("glu_bwd", glu_bwd_tile, [ys5, tglu, pa, dya], [sp["s5_glu_b"]],
                                            [(s5w, F32), (s5w, BF16), (s5w, BF16)], [(1, s5w)], tr=_tile(rows, 256, 16))
    gs["s5_glu_b"] = dglub
    dgb = mm(dtglu, bw["glu"], "nt", "mm_dgb")
    gbig["glu"] = mm(gb, dtglu, "tn", "mm_dglu", out_dtype=BF16)

    def gelu_bwd_tile(i, ys, dg_, direct):
        _, vjp = jax.vjp(_gelu, ys)
        return vjp(dg_)[0] + direct

    dys5 = rowwise("gelu_bwd", gelu_bwd_tile, [ys5, dgb, dys_direct], [], [(s5w, F32)], tr=_tile(rows, 256, 16))[0]
    du, dbre, dbim, dcre, dcim, dare, daie, dd5 = s5_bwd(pa, dys5, s5st, bre, bim, cre, cim, are, aie, sp["s5_d"], nb, "s5_bwd")
    gs["s5_d"] = dd5
    dbbr, dbbi, dcr, dci, dar, dai = _s5_contract(dbre, dbim, dcre, dcim, dare, daie, sg, spn, shh)
    gs["s5_c_re"], gs["s5_c_im"] = dcr, dci
    (gs["s5_lambda_re"], gs["s5_lambda_im"], dlogdt, gs["s5_b_re"], gs["s5_b_im"]) = s5_disc_vjp((dar, dai, dbbr, dbbi))
    gs["s5_log_dt"] = dlogdt.reshape(1, -1)
    dpa = jnp.concatenate([du, dza], axis=1)
    gbig["W0a"] = mm(n0, dpa, "tn", "mm_dW0a", out_dtype=BF16)
    gbig["W0xb"] = mm(n0, dxb, "tn", "mm_dW0xb", out_dtype=BF16)
    gbig["W0zb"] = mm(n0, dzb, "tn", "mm_dW0zb", out_dtype=BF16)
    if early_grads is not None:
        early_grads({n: gbig[n] for n in _EARLY})
    dn0 = mm(dpa, bw["W0a"], "nt", "mm_dn0a")
    dn0 = mm(dxb, bw["W0xb"], "nt", "mm_dn0xb", resid=dn0)
    dn0 = mm(dzb, bw["W0zb"], "nt", "mm_dn0zb", resid=dn0)
    dh0, _, dg0 = norm_bwd(h0, sp["ab_norm"], dn0, dh1, "norm0_bwd")
    gs["ab_norm"] = dg0
    dh0 = dh0.reshape(nb, tp, d)
    gs["meta_tokens"] = jnp.sum(dh0[:, :N_META], axis=0)
    return loss_acc[0, 0], dh0, gbig, gs


N_DEV = 8
N_CHIP = 4
N_PEER_CHIPS = N_CHIP - 1
MESH = pl.DeviceIdType.MESH
_HBM = pl.BlockSpec(memory_space=pltpu.HBM)


def _place():
    x, y, c = lax.axis_index("x"), lax.axis_index("y"), lax.axis_index("c")
    return x, y, c, [(1 - x, y), (x, 1 - y), (1 - x, 1 - y)]


def all_gather8(v, name):
    m_per, n = v.shape

    def body(x_ref, out_ref, send_sems, recv_sems, local_sem):
        x, y, c, chips = _place()
        me, sibling = (x, y, c), (x, y, 1 - c)

        def rows(px, py, pc):
            return out_ref.at[pl.ds((4 * px + 2 * py + pc) * m_per, m_per), :]

        def copy(kk, block, to, src=None):
            return pltpu.make_async_remote_copy(
                src_ref=rows(*block) if src is None else src, dst_ref=rows(*block), send_sem=send_sems.at[kk],
                recv_sem=recv_sems.at[kk], device_id=to, device_id_type=MESH)

        mine = pltpu.make_async_copy(x_ref, rows(*me), local_sem)
        mine.start()
        first = [copy(0, me, sibling, src=x_ref)]
        first += [copy(1 + j, me, (*chip, c), src=x_ref) for j, chip in enumerate(chips)]
        for cp in first:
            cp.start()
        passed = [copy(4 + j, (*chip, c), sibling) for j, chip in enumerate(chips)]
        for j, chip in enumerate(chips):
            copy(1 + j, (*chip, c), me).wait_recv()
            passed[j].start()
        copy(0, sibling, me).wait_recv()
        for j, chip in enumerate(chips):
            copy(4 + j, (*chip, 1 - c), me).wait_recv()
        for cp in first + passed:
            cp.wait_send()
        mine.wait()

    return pl.pallas_call(
        body, name=name, out_shape=jax.ShapeDtypeStruct((N_DEV * m_per, n), v.dtype),
        in_specs=[pl.BlockSpec(memory_space=pltpu.VMEM)], out_specs=pl.BlockSpec(memory_space=pltpu.VMEM),
        scratch_shapes=[pltpu.SemaphoreType.DMA((7,)), pltpu.SemaphoreType.DMA((7,)), pltpu.SemaphoreType.DMA],
        compiler_params=pltpu.CompilerParams(vmem_limit_bytes=VMEM_LIMIT))(v)


def gather_chips(vs, name):
    na = len(vs)

    def body(*refs):
        x_refs, out_refs = refs[:na], refs[na:2 * na]
        send_sems, recv_sems, local_sems = refs[2 * na:]
        x, y, c, chips = _place()
        k = 2 * x + y
        sibling = (x, y, 1 - c)

        def copy(i, kk, src, chip_k, half, to):
            return pltpu.make_async_remote_copy(
                src_ref=src, dst_ref=out_refs[i].at[chip_k, half], send_sem=send_sems.at[6 * i + kk],
                recv_sem=recv_sems.at[6 * i + kk], device_id=to, device_id_type=MESH)

        mine = [pltpu.make_async_copy(x_refs[i], out_refs[i].at[k], local_sems.at[i]) for i in range(na)]
        for cp in mine:
            cp.start()
        first = [copy(i, j, x_refs[i].at[c], k, c, (*chip, c)) for j, chip in enumerate(chips) for i in range(na)]
        for cp in first:
            cp.start()
        passed = []
        for j, (cx, cy) in enumerate(chips):
            kj = 2 * cx + cy
            for i in range(na):
                copy(i, j, out_refs[i].at[kj, c], kj, c, (cx, cy, c)).wait_recv()
                fwd = copy(i, 3 + j, out_refs[i].at[kj, c], kj, c, sibling)
                fwd.start()
                passed.append(fwd)
        for j, (cx, cy) in enumerate(chips):
            kj = 2 * cx + cy
            for i in range(na):
                copy(i, 3 + j, out_refs[i].at[kj, 1 - c], kj, 1 - c, sibling).wait_recv()
        for cp in first + passed:
            cp.wait_send()
        for cp in mine:
            cp.wait()

    return pl.pallas_call(
        body, name=name, out_shape=[jax.ShapeDtypeStruct((N_CHIP,) + v.shape, v.dtype) for v in vs],
        in_specs=[_HBM] * na, out_specs=[_HBM] * na,
        scratch_shapes=[pltpu.SemaphoreType.DMA((6 * na,)), pltpu.SemaphoreType.DMA((6 * na,)),
                        pltpu.SemaphoreType.DMA((na,))])(*vs)


def scatter_chips(ps, name):
    na = len(ps)

    def body(*refs):
        p_refs, out_refs = refs[:na], refs[na:2 * na]
        send_sems, recv_sems, local_sems = refs[2 * na:]
        x, y, c, chips = _place()
        k = 2 * x + y
        sibling = (x, y, 1 - c)

        def copy(i, kk, src, chip_k, half, to):
            return pltpu.make_async_remote_copy(
                src_ref=src, dst_ref=out_refs[i].at[chip_k, half], send_sem=send_sems.at[7 * i + kk],
                recv_sem=recv_sems.at[7 * i + kk], device_id=to, device_id_type=MESH)

        mine = [pltpu.make_async_copy(p_refs[i].at[k], out_refs[i].at[k, c], local_sems.at[i]) for i in range(na)]
        for cp in mine:
            cp.start()
        first = [copy(i, 1 + j, p_refs[i].at[2 * cx + cy], k, c, (cx, cy, c))
                 for j, (cx, cy) in enumerate(chips) for i in range(na)]
        first += [copy(i, 0, p_refs[i].at[k], k, c, sibling) for i in range(na)]
        for cp in first:
            cp.start()
        passed = []
        for j, (cx, cy) in enumerate(chips):
            kj = 2 * cx + cy
            for i in range(na):
                copy(i, 1 + j, out_refs[i].at[kj, c], kj, c, (cx, cy, c)).wait_recv()
                fwd = copy(i, 4 + j, out_refs[i].at[kj, c], kj, c, sibling)
                fwd.start()
                passed.append(fwd)
        for i in range(na):
            copy(i, 0, out_refs[i].at[k, 1 - c], k, 1 - c, sibling).wait_recv()
        for j, (cx, cy) in enumerate(chips):
            kj = 2 * cx + cy
            for i in range(na):
                copy(i, 4 + j, out_refs[i].at[kj, 1 - c], kj, 1 - c, sibling).wait_recv()
        for cp in first + passed:
            cp.wait_send()
        for cp in mine:
            cp.wait()

    return pl.pallas_call(
        body, name=name, out_shape=[jax.ShapeDtypeStruct((N_CHIP, 2) + p.shape[1:], p.dtype) for p in ps],
        in_specs=[_HBM] * na, out_specs=[_HBM] * na,
        scratch_shapes=[pltpu.SemaphoreType.DMA((7 * na,)), pltpu.SemaphoreType.DMA((7 * na,)),
                        pltpu.SemaphoreType.DMA((na,))])(*ps)


def swap_halves(gs_, name):
    na = len(gs_)

    def body(*refs):
        g_refs, out_refs = refs[:na], refs[na:2 * na]
        send_sems, recv_sems = refs[2 * na:]
        x, y, c, _ = _place()
        cps = [pltpu.make_async_remote_copy(
            src_ref=g_refs[i].at[kk, 1 - c], dst_ref=out_refs[i].at[kk], send_sem=send_sems.at[N_CHIP * i + kk],
            recv_sem=recv_sems.at[N_CHIP * i + kk], device_id=(x, y, 1 - c), device_id_type=MESH)
            for i in range(na) for kk in range(N_CHIP)]
        for cp in cps:
            cp.start()
        for cp in cps:
            cp.wait()

    return pl.pallas_call(
        body, name=name, out_shape=[jax.ShapeDtypeStruct((N_CHIP,) + g.shape[2:], g.dtype) for g in gs_],
        in_specs=[_HBM] * na, out_specs=[_HBM] * na,
        scratch_shapes=[pltpu.SemaphoreType.DMA((N_CHIP * na,)), pltpu.SemaphoreType.DMA((N_CHIP * na,))])(*gs_)


def add_halves(g, other, core, name):
    _, _, m, n = g.shape
    tr = _tile(m, 256, 16)

    def body(core_ref, g_ref, o_ref, out_ref):
        out_ref[...] = (g_ref[...].astype(F32) + o_ref[...].astype(F32)).astype(out_ref.dtype)

    grid_spec = pltpu.PrefetchScalarGridSpec(
        num_scalar_prefetch=1, grid=(N_CHIP, m // tr),
        in_specs=[pl.BlockSpec((None, None, tr, n), lambda kk, i, core_ref: (kk, core_ref[0], i, 0)),
                  pl.BlockSpec((None, tr, n), lambda kk, i, core_ref: (kk, i, 0))],
        out_specs=pl.BlockSpec((None, tr, n), lambda kk, i, core_ref: (kk, i, 0)))
    return pl.pallas_call(body, name=name, grid_spec=grid_spec, out_shape=jax.ShapeDtypeStruct((N_CHIP, m, n), g.dtype),
                          compiler_params=_params(("arbitrary", "arbitrary")))(core.reshape(1).astype(jnp.int32), g, other)


def sequencer_exchange(srcs, scatter, collective_id, name):
    na = len(srcs)
    per = 2 * N_PEER_CHIPS + (1 if scatter else 0)
    hbm = pltpu.MemorySpace.HBM
    src_refs = [jax.new_ref(a, memory_space=hbm) for a in srcs]
    out_refs = [jax.empty_ref(jax.ShapeDtypeStruct((N_CHIP, 2) + a.shape[1:], a.dtype), memory_space=hbm) for a in srcs]

    @pl.kernel(mesh=plsc.ScalarSubcoreMesh(axis_name="seq", num_cores=1), name=name,
               scratch_types=(pltpu.SemaphoreType.DMA((per * na,)), pltpu.SemaphoreType.DMA((per * na,)),
                              pltpu.SemaphoreType.DMA((na,))),
               compiler_params=pltpu.CompilerParams(collective_id=collective_id))
    def launch(send_sems, recv_sems, local_sems):
        x, y, c, chips = _place()
        k = 2 * x + y
        sibling = (x, y, 1 - c)
        barrier = pltpu.get_barrier_semaphore()
        for cx, cy in chips:
            pl.semaphore_signal(barrier, inc=1, device_id=(cx, cy, c), device_id_type=MESH)
        pl.semaphore_signal(barrier, inc=1, device_id=sibling, device_id_type=MESH)
        pl.semaphore_wait(barrier, N_CHIP)

        def copy(i, kk, src, chip_k, half, to):
            return pltpu.make_async_remote_copy(
                src_ref=src, dst_ref=out_refs[i].at[chip_k, half], send_sem=send_sems.at[per * i + kk],
                recv_sem=recv_sems.at[per * i + kk], device_id=to, device_id_type=MESH)

        if scatter:
            mine = [pltpu.make_async_copy(src_refs[i].at[k], out_refs[i].at[k, c], local_sems.at[i]) for i in range(na)]
        else:
            mine = [pltpu.make_async_copy(src_refs[i], out_refs[i].at[k], local_sems.at[i]) for i in range(na)]
        for cp in mine:
            cp.start()
        first = []
        for j, (cx, cy) in enumerate(chips):
            for i in range(na):
                src = src_refs[i].at[2 * cx + cy] if scatter else src_refs[i].at[c]
                first.append(copy(i, j, src, k, c, (cx, cy, c)))
        if scatter:
            first += [copy(i, 2 * N_PEER_CHIPS, src_refs[i].at[k], k, c, sibling) for i in range(na)]
        for cp in first:
            cp.start()
        passed = []
        for j, (cx, cy) in enumerate(chips):
            kj = 2 * cx + cy
            for i in range(na):
                copy(i, j, out_refs[i].at[kj, c], kj, c, (cx, cy, c)).wait_recv()
                fwd = copy(i, N_PEER_CHIPS + j, out_refs[i].at[kj, c], kj, c, sibling)
                fwd.start()
                passed.append(fwd)
        if scatter:
            for i in range(na):
                copy(i, 2 * N_PEER_CHIPS, out_refs[i].at[k, 1 - c], k, 1 - c, sibling).wait_recv()
        for j, (cx, cy) in enumerate(chips):
            kj = 2 * cx + cy
            for i in range(na):
                copy(i, N_PEER_CHIPS + j, out_refs[i].at[kj, 1 - c], kj, 1 - c, sibling).wait_recv()
        for cp in first + passed:
            cp.wait_send()
        for cp in mine:
            cp.wait()

    launch()
    return [r[...] for r in out_refs]


PACK_LANES = 512


def _pack(arrs, dtype, lanes, row_align):
    flat = jnp.concatenate([a.reshape(-1).astype(dtype) for a in arrs])
    unit = lanes * row_align
    total = -(-flat.shape[0] // unit) * unit
    return jnp.pad(flat, (0, total - flat.shape[0])).reshape(total // lanes, lanes)


def _unpack(flat, shapes):
    flat = flat.reshape(-1)
    out, off = [], 0
    for s in shapes:
        n = math.prod(s)
        out.append(flat[off:off + n].reshape(s))
        off += n
    return out


def _adam_tile(w, m, v, g):
    m2 = ADAM_B1 * m + (1.0 - ADAM_B1) * g
    v2 = ADAM_B2 * v + (1.0 - ADAM_B2) * (g * g)
    m_hat = m2 / (1.0 - ADAM_B1 ** ADAM_STEP)
    v_hat = v2 / (1.0 - ADAM_B2 ** ADAM_STEP)
    delta = -ADAM_LR * (m_hat / (jnp.sqrt(v_hat) + ADAM_EPS) + ADAM_WD * w)
    return delta, m2, v2


def adam_big(w, m, v, pieces, name):
    _, r, c = w.shape
    tr = _tile(r, 128, 16)

    def body(w_ref, m_ref, v_ref, p0, p1, p2, p3, g_ref, d_ref, mo_ref, vo_ref):
        g = ((p0[...].astype(F32) + p1[...].astype(F32)) + p2[...].astype(F32)) + p3[...].astype(F32)
        delta, m2, v2 = _adam_tile(w_ref[...], m_ref[...], v_ref[...], g)
        g_ref[...] = g
        d_ref[...] = delta
        mo_ref[...] = m2
        vo_ref[...] = v2

    wspec = pl.BlockSpec((None, tr, c), lambda i: (0, i, 0))
    pspecs = [pl.BlockSpec((None, tr, c), functools.partial(lambda i, kk: (kk, i, 0), kk=kk)) for kk in range(N_CHIP)]
    return pl.pallas_call(
        body, name=name, grid=(r // tr,), in_specs=[wspec] * 3 + pspecs, out_specs=[wspec] * 4,
        out_shape=[jax.ShapeDtypeStruct(w.shape, F32)] * 4, compiler_params=_params(("parallel",)))(
            w, m, v, pieces, pieces, pieces, pieces)


_WEIGHTS = (
    ("meta_tokens", "small", 1), ("ab_norm", "small", None), ("ab_w_in", "big", 2), ("s5_lambda_re", "small", None),
    ("s5_lambda_im", "small", None), ("s5_log_dt", "small", None), ("s5_b_re", "small", None), ("s5_b_im", "small", None),
    ("s5_c_re", "small", None), ("s5_c_im", "small", None), ("s5_d", "small", None), ("s5_glu_w", "big", 1),
    ("s5_glu_b", "small", None), ("ml_conv_w", "small", 2), ("ml_conv_b", "small", None), ("ml_wq", "small", 1),
    ("ml_wk", "small", 1), ("ml_wv", "small", 1), ("ml_w_gate", "small", 1), ("ml_b_gate", "small", None),
    ("ml_norm", "small", None), ("ml_skip", "small", None), ("ab_w_out", "big", 1), ("ssd_norm", "small", 1),
    ("ssd_w_in", "big", 2), ("ssd_conv_w", "small", 2), ("ssd_conv_b", "small", 1), ("ssd_dt_bias", "small", None),
    ("ssd_a_log", "small", None), ("ssd_d", "small", None), ("ssd_gnorm", "small", 1), ("ssd_w_out", "big", 1),
    ("final_norm", "small", None),
)


def _squeeze(a):
    return a[0] if a.ndim >= 3 else a


def kernel(x, meta_tokens, ab_norm, ab_w_in, s5_lambda_re, s5_lambda_im, s5_log_dt, s5_b_re, s5_b_im, s5_c_re, s5_c_im, s5_d, s5_glu_w, s5_glu_b, ml_conv_w, ml_conv_b, ml_wq, ml_wk, ml_wv, ml_w_gate, ml_b_gate, ml_norm, ml_skip, ab_w_out, ssd_norm, ssd_w_in, ssd_conv_w, ssd_conv_b, ssd_dt_bias, ssd_a_log, ssd_d, ssd_gnorm, ssd_w_out, final_norm, loss_target, m_meta_tokens, m_ab_norm, m_ab_w_in, m_s5_lambda_re, m_s5_lambda_im, m_s5_log_dt, m_s5_b_re, m_s5_b_im, m_s5_c_re, m_s5_c_im, m_s5_d, m_s5_glu_w, m_s5_glu_b, m_ml_conv_w, m_ml_conv_b, m_ml_wq, m_ml_wk, m_ml_wv, m_ml_w_gate, m_ml_b_gate, m_ml_norm, m_ml_skip, m_ab_w_out, m_ssd_norm, m_ssd_w_in, m_ssd_conv_w, m_ssd_conv_b, m_ssd_dt_bias, m_ssd_a_log, m_ssd_d, m_ssd_gnorm, m_ssd_w_out, m_final_norm, v_meta_tokens, v_ab_norm, v_ab_w_in, v_s5_lambda_re, v_s5_lambda_im, v_s5_log_dt, v_s5_b_re, v_s5_b_im, v_s5_c_re, v_s5_c_im, v_s5_d, v_s5_glu_w, v_s5_glu_b, v_ml_conv_w, v_ml_conv_b, v_ml_wq, v_ml_wk, v_ml_wv, v_ml_w_gate, v_ml_b_gate, v_ml_norm, v_ml_skip, v_ab_w_out, v_ssd_norm, v_ssd_w_in, v_ssd_conv_w, v_ssd_conv_b, v_ssd_dt_bias, v_ssd_a_log, v_ssd_d, v_ssd_gnorm, v_ssd_w_out, v_final_norm):
    args = (meta_tokens, ab_norm, ab_w_in, s5_lambda_re, s5_lambda_im, s5_log_dt, s5_b_re, s5_b_im, s5_c_re, s5_c_im, s5_d, s5_glu_w, s5_glu_b, ml_conv_w, ml_conv_b, ml_wq, ml_wk, ml_wv, ml_w_gate, ml_b_gate, ml_norm, ml_skip, ab_w_out, ssd_norm, ssd_w_in, ssd_conv_w, ssd_conv_b, ssd_dt_bias, ssd_a_log, ssd_d, ssd_gnorm, ssd_w_out, final_norm)
    m_args = (m_meta_tokens, m_ab_norm, m_ab_w_in, m_s5_lambda_re, m_s5_lambda_im, m_s5_log_dt, m_s5_b_re, m_s5_b_im, m_s5_c_re, m_s5_c_im, m_s5_d, m_s5_glu_w, m_s5_glu_b, m_ml_conv_w, m_ml_conv_b, m_ml_wq, m_ml_wk, m_ml_wv, m_ml_w_gate, m_ml_b_gate, m_ml_norm, m_ml_skip, m_ab_w_out, m_ssd_norm, m_ssd_w_in, m_ssd_conv_w, m_ssd_conv_b, m_ssd_dt_bias, m_ssd_a_log, m_ssd_d, m_ssd_gnorm, m_ssd_w_out, m_final_norm)
    v_args = (v_meta_tokens, v_ab_norm, v_ab_w_in, v_s5_lambda_re, v_s5_lambda_im, v_s5_log_dt, v_s5_b_re, v_s5_b_im, v_s5_c_re, v_s5_c_im, v_s5_d, v_s5_glu_w, v_s5_glu_b, v_ml_conv_w, v_ml_conv_b, v_ml_wq, v_ml_wk, v_ml_wv, v_ml_w_gate, v_ml_b_gate, v_ml_norm, v_ml_skip, v_ab_w_out, v_ssd_norm, v_ssd_w_in, v_ssd_conv_w, v_ssd_conv_b, v_ssd_dt_bias, v_ssd_a_log, v_ssd_d, v_ssd_gnorm, v_ssd_w_out, v_final_norm)
    names = [w[0] for w in _WEIGHTS]
    kind = {w[0]: w[1] for w in _WEIGHTS}
    axis = {w[0]: w[2] for w in _WEIGHTS}
    w_loc = dict(zip(names, args))
    m_loc = dict(zip(names, m_args))
    v_loc = dict(zip(names, v_args))
    chip = 2 * lax.axis_index("x") + lax.axis_index("y")
    core = lax.axis_index("c")
    big = [n for n in names if kind[n] == "big"]
    small = [n for n in names if kind[n] == "small"]
    small_sh = [n for n in small if axis[n] is not None]

    def halves(a):
        return a.astype(BF16).reshape(2, a.shape[1] // 2, a.shape[2])

    def assemble(n, gth):
        shard = gth.reshape((N_CHIP,) + w_loc[n].shape[1:])
        if axis[n] == 1:
            return shard.reshape(-1, shard.shape[2])
        return jnp.concatenate([shard[kk] for kk in range(N_CHIP)], axis=1)

    early = ["ab_w_in", "s5_glu_w"]
    late = ["ab_w_out", "ssd_w_in", "ssd_w_out"]
    gathered = gather_chips([halves(w_loc[n]) for n in early], "gather_early_w")
    after_early = (gathered[0][0, 0, 0, 0] * 0).astype(BF16)
    late_gathered = sequencer_exchange([halves(w_loc[n]) + after_early for n in late], False, 1, "gather_late_w")
    w_in0_shards = gathered[0].reshape((N_CHIP,) + w_loc["ab_w_in"].shape[1:])
    glu_full = assemble("s5_glu_w", gathered[1])

    def columns(shards, lo, hi):
        cw = shards.shape[2]
        parts = [shards[kk][:, max(lo - kk * cw, 0):min(hi - kk * cw, cw)]
                 for kk in range(N_CHIP) if lo < (kk + 1) * cw and hi > kk * cw]
        return parts[0] if len(parts) == 1 else jnp.concatenate(parts, axis=1)

    small_sh_shapes = [w_loc[n].shape for n in small_sh]
    packed_s = _pack([w_loc[n] for n in small_sh], F32, LANES, SUBLANES)
    g8 = all_gather8(packed_s, "gather_small_w").reshape(N_CHIP, 2, -1)
    sp = {}
    for n in small:
        if axis[n] is None:
            sp[n] = _squeeze(w_loc[n])
    per_chip = [_unpack(g8[kk, 0], small_sh_shapes) for kk in range(N_CHIP)]
    for i, n in enumerate(small_sh):
        sp[n] = _squeeze(jnp.concatenate([per_chip[kk][i] for kk in range(N_CHIP)], axis=axis[n]))

    s5w = glu_full.shape[0]
    mlw = w_loc["ab_w_out"].shape[1] * N_CHIP - s5w
    inner = w_loc["ssd_w_out"].shape[1] * N_CHIP
    n_heads1 = sp["ssd_d"].shape[1]
    cdim = w_loc["ssd_w_in"].shape[2] * N_CHIP - inner - n_heads1
    bw = dict(W0a=columns(w_in0_shards, 0, 2 * s5w), W0xb=columns(w_in0_shards, 2 * s5w, 2 * s5w + mlw),
              W0zb=columns(w_in0_shards, 2 * s5w + mlw, 2 * (s5w + mlw)), glu=glu_full)

    def late_weights():
        fb = dict(zip(late, late_gathered))
        w_out0 = assemble("ab_w_out", fb["ab_w_out"])
        w1 = fb["ssd_w_in"].reshape((N_CHIP,) + w_loc["ssd_w_in"].shape[1:])
        return dict(Wo0a=w_out0[:s5w], Wo0b=w_out0[s5w:], W1z=columns(w1, 0, inner),
                    W1x=columns(w1, inner, inner + cdim), W1dt=_pad_lanes(columns(w1, inner + cdim, inner + cdim + n_heads1)),
                    Wo1=assemble("ssd_w_out", fb["ssd_w_out"]))

    def piece_columns(parts, lo, hi):
        out, off = [], 0
        for p in parts:
            a, b = max(lo - off, 0), min(hi - off, p.shape[1])
            if a < b:
                out.append(p[:, a:b])
            off += p.shape[1]
        return out[0] if len(out) == 1 else jnp.concatenate(out, axis=1)

    def chip_halves(n, parts):
        _, r, c_ = w_loc[n].shape
        if axis[n] == 1:
            whole = parts[0] if len(parts) == 1 else jnp.concatenate(parts, axis=0)
            return whole.reshape(N_CHIP, 2, r // 2, c_)
        shards = [piece_columns(parts, kk * c_, (kk + 1) * c_) for kk in range(N_CHIP)]
        return jnp.stack(shards).reshape(N_CHIP, 2, r // 2, c_)

    def chip_partials(ns, gfull, tag):
        gps = [chip_halves(n, gfull[n]) for n in ns]
        from_sibling = swap_halves(gps, "swap_" + tag)
        return [add_halves(gp, oth, core, "add_" + n) for n, gp, oth in zip(ns, gps, from_sibling)]

    pieces = {}

    def late_grads(g):
        gfull = {"ab_w_out": [g["Wo0a"], g["Wo0b"]], "ssd_w_in": [g["W1z"], g["W1x"], g["W1dt"][:, :n_heads1]],
                 "ssd_w_out": [g["Wo1"]]}
        pieces.update(zip(late, sequencer_exchange(chip_partials(late, gfull, "late_g"), True, 2, "scatter_late_g")))

    def early_grads(g):
        gfull = {"ab_w_in": [g["W0a"], g["W0xb"], g["W0zb"]], "s5_glu_w": [g["glu"]]}
        pieces.update(zip(early, sequencer_exchange(chip_partials(early, gfull, "early_g"), True, 3, "scatter_early_g")))

    loss_local, dh0, gbig, gs = _local_step(x, loss_target, bw, sp, late_weights, late_grads, early_grads)
    grad_x = dh0[:, N_META:N_META + x.shape[1]]

    out_g, out_d, out_m, out_v = {}, {}, {}, {}
    small_full_shapes = [sp[n].shape for n in small] + [(1, 1)]
    packed_gs = _pack([gs[n] for n in small] + [loss_local.reshape(1, 1)], F32, LANES, SUBLANES)
    rows_s = packed_gs.shape[0]
    all_gs = sequencer_exchange([jnp.broadcast_to(packed_gs[None], (N_CHIP,) + packed_gs.shape)], True, 4,
                                "gather_small_g")[0].reshape(N_DEV, rows_s, LANES)
    blocks = [all_gs[i] for i in range(N_DEV)]

    for n in late + early:
        pcs = pieces[n].reshape((N_CHIP,) + w_loc[n].shape[1:])
        out_g[n], out_d[n], out_m[n], out_v[n] = adam_big(w_loc[n], m_loc[n], v_loc[n], pcs, "adam_" + n)

    def sum8(i, *b):
        acc = b[0]
        for t in b[1:]:
            acc = acc + t
        return acc

    gsum = rowwise("sum_small_g", sum8, blocks, [], [(LANES, F32)], tr=_tile(rows_s, 512, 8))[0]
    summed = _unpack(gsum, small_full_shapes)
    loss = summed[-1].reshape(())
    g_small = dict(zip(small, summed[:-1]))
    g_loc = {}
    for n in small:
        g = g_small[n].reshape((1,) + g_small[n].shape) if w_loc[n].ndim >= 3 else g_small[n]
        if axis[n] is not None:
            size = w_loc[n].shape[axis[n]]
            g = lax.dynamic_slice_in_dim(g, chip * size, size, axis=axis[n])
        g_loc[n] = g.reshape(w_loc[n].shape)
    loc_shapes = [w_loc[n].shape for n in small]
    pw, pm, pv, pg = (_pack([d[n] for n in small], F32, LANES, SUBLANES) for d in (w_loc, m_loc, v_loc, g_loc))
    dl, mn, vn = rowwise("adam_small", lambda i, a, b, c_, d_: _adam_tile(a, b, c_, d_), [pw, pm, pv, pg], [],
                         [(LANES, F32)] * 3, tr=_tile(pw.shape[0], 512, 8))
    for d_out, flat in ((out_d, dl), (out_m, mn), (out_v, vn)):
        for n, a in zip(small, _unpack(flat, loc_shapes)):
            d_out[n] = a
    for n in small:
        out_g[n] = g_loc[n]

    return (loss, grad_x, *[out_g[n] for n in names], *[out_d[n] for n in names], *[out_m[n] for n in names],
            *[out_v[n] for n in names])
```

```python
import functools
import math

import jax
import jax.numpy as jnp
from jax import lax
from jax.experimental import pallas as pl
from jax.experimental.pallas import tpu as pltpu
from jax.experimental.pallas import tpu_sc as plsc

F32 = jnp.float32
BF16 = jnp.bfloat16
HI = lax.Precision.HIGHEST

D_MODEL = 2048
SEQ = 2048
N_META = 16
CHUNK = 128
NORM_EPS = 1e-6
HEAD_NORM_EPS = 1e-5
S5_GROUP_SIZE = 16
S5_STATE = 64
MLSTM_HEADS = 8
QKV_BLOCK = 4
SSD_HEAD_DIM = 64
SSD_STATE = 128
SSD_HPG = 8
ADAM_LR = 0.001
ADAM_B1 = 0.9
ADAM_B2 = 0.999
ADAM_EPS = 1e-08
ADAM_WD = 0.01
ADAM_STEP = 10

LANES = 128
SUBLANES = 8
VMEM_LIMIT = 56 * 1024 * 1024
MM_OPERAND_VMEM = 34 * 1024 * 1024


def _sigmoid(x):
    return 0.5 * jnp.tanh(0.5 * x) + 0.5


@jax.custom_vjp
def _silu(x):
    return x * _sigmoid(x)


def _silu_fwd(x):
    return x * _sigmoid(x), x


def _silu_bwd(x, ct):
    s = _sigmoid(x)
    return (ct * (s * (1.0 + x * (1.0 - s))),)


_silu.defvjp(_silu_fwd, _silu_bwd)


def _softplus(x):
    return jnp.maximum(x, 0.0) + jnp.log(1.0 + jnp.exp(-jnp.abs(x)))


def _log_sigmoid(x):
    return jnp.minimum(x, 0.0) - jnp.log(1.0 + jnp.exp(-jnp.abs(x)))


def _gelu(x):
    return 0.5 * x * (1.0 + jnp.tanh(math.sqrt(2.0 / math.pi) * (x + 0.044715 * (x * x * x))))


def _dot(a, b, dims, precision=None):
    return lax.dot_general(a, b, (dims, ((), ())), preferred_element_type=F32, precision=precision)


_NN, _NT, _TN = ((1,), (0,)), ((1,), (1,)), ((0,), (0,))


def _bf16_dot(dims, da_rule, db_rule):
    @jax.custom_vjp
    def f(a, b):
        return _dot(a.astype(BF16), b.astype(BF16), dims)

    def fwd(a, b):
        ab, bb = a.astype(BF16), b.astype(BF16)
        return _dot(ab, bb, dims), (ab, bb, jnp.zeros((), a.dtype), jnp.zeros((), b.dtype))

    def bwd(res, ct):
        ab, bb, a_like, b_like = res
        cb = ct.astype(BF16)
        return da_rule(ab, bb, cb).astype(a_like.dtype), db_rule(ab, bb, cb).astype(b_like.dtype)

    f.defvjp(fwd, bwd)
    return f


_dot_nn = _bf16_dot(_NN, lambda a, b, c: _dot(c, b, _NT), lambda a, b, c: _dot(a, c, _TN))
_dot_nt = _bf16_dot(_NT, lambda a, b, c: _dot(c, b, _NN), lambda a, b, c: _dot(c, a, _TN))
_dot_tn = _bf16_dot(_TN, lambda a, b, c: _dot(b, c, _NT), lambda a, b, c: _dot(a, c, _NN))


def _lane_pick(a, idx):
    sel = (lax.broadcasted_iota(jnp.int32, (1, a.shape[1]), 1) == idx).astype(a.dtype)
    return jnp.sum(a * sel, axis=1, keepdims=True)


def _row_pick(a, idx):
    sel = (lax.broadcasted_iota(jnp.int32, (a.shape[0], 1), 0) == idx).astype(a.dtype)
    return jnp.sum(a * sel, axis=0, keepdims=True)


def _tri(n, upper=False):
    r = lax.broadcasted_iota(jnp.int32, (n, n), 0)
    c = lax.broadcasted_iota(jnp.int32, (n, n), 1)
    return ((r <= c) if upper else (r >= c)).astype(F32)


def _tile(n, target, align):
    if n <= target:
        return n
    t = (target // align) * align
    while t >= align:
        if n % t == 0:
            return t
        t -= align
    return n


def _params(sem=None):
    return pltpu.CompilerParams(dimension_semantics=sem, vmem_limit_bytes=VMEM_LIMIT)


def mm(a, b, mode, name, resid=None, out_dtype=F32):
    if mode == "nn":
        (m, k), (k2, n) = a.shape, b.shape
    elif mode == "nt":
        (m, k), (n, k2) = a.shape, b.shape
    else:
        (k, m), (k2, n) = a.shape, b.shape
    assert k == k2, (a.shape, b.shape, mode)
    a_sz, b_sz = a.dtype.itemsize, b.dtype.itemsize
    if mode == "tn":
        tm, tn = _tile(m, 1024, LANES), _tile(n, 1024, LANES)
        tk = _tile(k, MM_OPERAND_VMEM // (2 * (tm * a_sz + tn * b_sz)), 16)
    else:
        tm, tn = _tile(m, 1088, 16), _tile(n, 512, LANES)
        tk = _tile(k, MM_OPERAND_VMEM // (2 * (tm * a_sz + tn * b_sz)), LANES)
    nk = k // tk
    dims = {"nn": ((1,), (0,)), "nt": ((1,), (1,)), "tn": ((0,), (0,))}[mode]
    has_resid = resid is not None

    def body(*refs):
        if has_resid:
            a_ref, b_ref, r_ref, o_ref = refs[:4]
        else:
            a_ref, b_ref, o_ref = refs[:3]
        part = _dot(a_ref[...].astype(BF16), b_ref[...].astype(BF16), dims)

        def finish(res):
            if has_resid:
                res = res + r_ref[...].astype(F32)
            o_ref[...] = res.astype(o_ref.dtype)

        if nk == 1:
            finish(part)
            return
        acc_ref = refs[-1]
        kk = pl.program_id(2)

        @pl.when(kk == 0)
        def _():
            acc_ref[...] = part

        @pl.when(jnp.logical_and(kk > 0, kk < nk - 1))
        def _():
            acc_ref[...] += part

        @pl.when(kk == nk - 1)
        def _():
            finish(acc_ref[...] + part)

    if mode == "tn":
        a_spec = pl.BlockSpec((tk, tm), lambda i, j, kk: (kk, i))
    else:
        a_spec = pl.BlockSpec((tm, tk), lambda i, j, kk: (i, kk))
    if mode == "nt":
        b_spec = pl.BlockSpec((tn, tk), lambda i, j, kk: (j, kk))
    else:
        b_spec = pl.BlockSpec((tk, tn), lambda i, j, kk: (kk, j))
    o_spec = pl.BlockSpec((tm, tn), lambda i, j, kk: (i, j))
    in_specs = [a_spec, b_spec] + ([o_spec] if has_resid else [])
    args = (a, b) + ((resid,) if has_resid else ())
    return pl.pallas_call(
        body, name=name, grid=(m // tm, n // tn, nk), in_specs=in_specs, out_specs=o_spec,
        out_shape=jax.ShapeDtypeStruct((m, n), out_dtype), scratch_shapes=[pltpu.VMEM((tm, tn), F32)] if nk > 1 else [],
        compiler_params=_params(("parallel", "parallel", "arbitrary")))(*args)


def rowwise(name, f, rows, params, outs, accs=(), tr=128):
    n_rows = rows[0].shape[0]
    assert n_rows % tr == 0
    n_r, n_p, n_o, n_a = len(rows), len(params), len(outs), len(accs)

    def body(*refs):
        i = pl.program_id(0)
        r_vals = [r[...] for r in refs[:n_r]]
        p_vals = [r[...] for r in refs[n_r:n_r + n_p]]
        o_refs = refs[n_r + n_p:n_r + n_p + n_o]
        a_refs = refs[n_r + n_p + n_o:]
        res = f(i, *r_vals, *p_vals)
        if not isinstance(res, (tuple, list)):
            res = (res,)
        assert len(res) == n_o + n_a, (name, len(res))
        for o_ref, val in zip(o_refs, res[:n_o]):
            o_ref[...] = val.astype(o_ref.dtype)
        if n_a:
            @pl.when(i == 0)
            def _():
                for a_ref in a_refs:
                    a_ref[...] = jnp.zeros_like(a_ref)

            for a_ref, val in zip(a_refs, res[n_o:]):
                a_ref[...] += val.astype(F32)

    in_specs = [pl.BlockSpec((tr, r.shape[1]), lambda i: (i, 0)) for r in rows]
    in_specs += [pl.BlockSpec(p.shape, lambda i: (0, 0)) for p in params]
    out_specs = [pl.BlockSpec((tr, w), lambda i: (i, 0)) for w, _ in outs]
    out_specs += [pl.BlockSpec(s, lambda i: (0, 0)) for s in accs]
    out_shape = [jax.ShapeDtypeStruct((n_rows, w), dt) for w, dt in outs]
    out_shape += [jax.ShapeDtypeStruct(s, F32) for s in accs]
    res = pl.pallas_call(
        body, name=name, grid=(n_rows // tr,), in_specs=in_specs, out_specs=out_specs, out_shape=out_shape,
        compiler_params=_params(("arbitrary",)))(*rows, *params)
    return res


def _rms(x, g, eps=NORM_EPS):
    return x * lax.rsqrt(jnp.mean(x * x, axis=-1, keepdims=True) + eps) * g


def norm_fwd(x, g, name):
    return rowwise(name, lambda i, xb, gb: _rms(xb, gb), [x], [g], [(x.shape[1], BF16)], tr=_tile(x.shape[0], 256, 16))[0]


def norm_bwd(x, g, dn, resid, name):
    def f(i, xb, dnb, rb, gb):
        _, vjp = jax.vjp(_rms, xb, gb)
        dx, dg = vjp(dnb)
        return dx + rb, dx + rb, dg

    return rowwise(name, f, [x, dn, resid], [g], [(x.shape[1], F32), (x.shape[1], BF16)], [g.shape],
                   tr=_tile(x.shape[0], 256, 16))


def conv_fwd(x, w, b, nb, name):
    rows, width = x.shape
    nc = rows // nb // CHUNK
    tw = _tile(width, 1024, LANES)
    ksz = w.shape[0]

    def body(x_ref, w_ref, b_ref, o_ref, ext_ref):
        c = pl.program_id(2)

        @pl.when(c == 0)
        def _():
            ext_ref[0:SUBLANES, :] = jnp.zeros((SUBLANES, tw), F32)

        taps = [w_ref[j:j + 1, :] for j in range(ksz)]
        bias = b_ref[...]
        row = lax.broadcasted_iota(jnp.int32, (SUBLANES, tw), 0)
        prev_rot = [pltpu.roll(ext_ref[0:SUBLANES, :], k, 0) for k in range(1, ksz)]
        for s in range(CHUNK // SUBLANES):
            r0 = s * SUBLANES
            cur = x_ref[r0:r0 + SUBLANES, :]
            cur_rot = [pltpu.roll(cur, k, 0) for k in range(1, ksz)]
            acc = bias + taps[ksz - 1] * cur
            for k in range(1, ksz):
                acc = acc + taps[ksz - 1 - k] * jnp.where(row >= k, cur_rot[k - 1], prev_rot[k - 1])
            o_ref[r0:r0 + SUBLANES, :] = acc
            prev_rot = cur_rot
        ext_ref[0:SUBLANES, :] = x_ref[CHUNK - SUBLANES:CHUNK, :]

    return pl.pallas_call(
        body, name=name, grid=(width // tw, nb, nc),
        in_specs=[pl.BlockSpec((CHUNK, tw), lambda j, bb, c: (bb * nc + c, j)),
                  pl.BlockSpec((ksz, tw), lambda j, bb, c: (0, j)),
                  pl.BlockSpec((1, tw), lambda j, bb, c: (0, j))],
        out_specs=pl.BlockSpec((CHUNK, tw), lambda j, bb, c: (bb * nc + c, j)),
        out_shape=jax.ShapeDtypeStruct((rows, width), F32),
        scratch_shapes=[pltpu.VMEM((2 * SUBLANES, tw), F32)],
        compiler_params=_params(("arbitrary", "arbitrary", "arbitrary")))(x, w, b)


def conv_bwd(dc, x, w, nb, name, resid=None, dx_dtype=BF16):
    rows, width = x.shape
    nc = rows // nb // CHUNK
    tw = _tile(width, 1024, LANES)
    ksz = w.shape[0]
    per = CHUNK // SUBLANES
    has_resid = resid is not None

    def body(*refs):
        if has_resid:
            dc_ref, x_ref, halo_ref, w_ref, r_ref, dx_ref, dw_ref, db_ref, extd_ref, extx_ref = refs
        else:
            dc_ref, x_ref, halo_ref, w_ref, dx_ref, dw_ref, db_ref, extd_ref, extx_ref = refs
        bb = pl.program_id(1)
        step = pl.program_id(2)
        c = nc - 1 - step

        @pl.when(jnp.logical_and(bb == 0, step == 0))
        def _():
            dw_ref[...] = jnp.zeros_like(dw_ref)
            db_ref[...] = jnp.zeros_like(db_ref)

        @pl.when(step == 0)
        def _():
            extd_ref[SUBLANES:2 * SUBLANES, :] = jnp.zeros((SUBLANES, tw), F32)

        nstrip = CHUNK // SUBLANES
        taps = [w_ref[j:j + 1, :] for j in range(ksz)]
        row = lax.broadcasted_iota(jnp.int32, (SUBLANES, tw), 0)
        x_prev_rot = [pltpu.roll(jnp.where(c == 0, 0.0, halo_ref[...]), k, 0) for k in range(1, ksz)]
        dcs = dc_ref[0:SUBLANES, :]
        dc_rot = [pltpu.roll(dcs, SUBLANES - k, 0) for k in range(1, ksz)]
        for s in range(nstrip):
            r0 = s * SUBLANES
            nxt = extd_ref[SUBLANES:2 * SUBLANES, :] if s == nstrip - 1 else dc_ref[r0 + SUBLANES:r0 + 2 * SUBLANES, :]
            nxt_rot = [pltpu.roll(nxt, SUBLANES - k, 0) for k in range(1, ksz)]
            xc = x_ref[r0:r0 + SUBLANES, :]
            x_rot = [pltpu.roll(xc, k, 0) for k in range(1, ksz)]
            dx = r_ref[r0:r0 + SUBLANES, :].astype(F32) if has_resid else jnp.zeros((SUBLANES, tw), F32)
            dx = dx + taps[ksz - 1] * dcs
            dw_ref[(ksz - 1) * SUBLANES:ksz * SUBLANES, :] += dcs * xc
            for k in range(1, ksz):
                j = ksz - 1 - k
                dx = dx + taps[j] * jnp.where(row < SUBLANES - k, dc_rot[k - 1], nxt_rot[k - 1])
                dw_ref[j * SUBLANES:(j + 1) * SUBLANES, :] += dcs * jnp.where(row >= k, x_rot[k - 1], x_prev_rot[k - 1])
            if s % 2 == 0:
                held = dx
            else:
                dx_ref[r0 - SUBLANES:r0 + SUBLANES, :] = jnp.concatenate([held, dx], axis=0).astype(dx_ref.dtype)
            db_ref[...] += dcs
            dcs, dc_rot, x_prev_rot = nxt, nxt_rot, x_rot
        extd_ref[SUBLANES:2 * SUBLANES, :] = dc_ref[0:SUBLANES, :]

    def blk(j, bb, step):
        return (bb * nc + nc - 1 - step, j)

    def halo(j, bb, step):
        return (jnp.maximum((bb * nc + nc - 1 - step) * per - 1, 0), j)

    in_specs = [pl.BlockSpec((CHUNK, tw), blk), pl.BlockSpec((CHUNK, tw), blk), pl.BlockSpec((SUBLANES, tw), halo),
                pl.BlockSpec((ksz, tw), lambda j, bb, step: (0, j))]
    args = [dc, x, x, w]
    if has_resid:
        in_specs.append(pl.BlockSpec((CHUNK, tw), blk))
        args.append(resid)
    dx, dw_raw, db_raw = pl.pallas_call(
        body, name=name, grid=(width // tw, nb, nc), in_specs=in_specs,
        out_specs=[pl.BlockSpec((CHUNK, tw), blk), pl.BlockSpec((ksz * SUBLANES, tw), lambda j, bb, step: (0, j)),
                   pl.BlockSpec((SUBLANES, tw), lambda j, bb, step: (0, j))],
        out_shape=[jax.ShapeDtypeStruct((rows, width), dx_dtype), jax.ShapeDtypeStruct((ksz * SUBLANES, width), F32),
                   jax.ShapeDtypeStruct((SUBLANES, width), F32)],
        scratch_shapes=[pltpu.VMEM((2 * SUBLANES, tw), F32), pltpu.VMEM((2 * SUBLANES, tw), F32)],
        compiler_params=_params(("arbitrary", "arbitrary", "arbitrary")))(*args)
    return dx, dw_raw.reshape(ksz, SUBLANES, width).sum(axis=1), db_raw.sum(axis=0, keepdims=True)


S5_Q = 4


def _s5_fill_bu(u, bre_ref, bim_ref, xr_ref, xi_ref, ns):
    for s in range(ns):
        ub = u[:, s * LANES:(s + 1) * LANES].astype(BF16)
        bur = _dot(ub, bre_ref[s], ((1,), (0,)))
        bui = _dot(ub, bim_ref[s], ((1,), (0,)))
        for q in range(S5_Q):
            xr_ref[q, pl.ds(s, CHUNK, stride=ns), :] = bur[:, q * LANES:(q + 1) * LANES]
            xi_ref[q, pl.ds(s, CHUNK, stride=ns), :] = bui[:, q * LANES:(q + 1) * LANES]


def _s5_scan(xr_ref, xi_ref, ar_ref, ai_ref, st_ref, ns):
    ar = [ar_ref[q] for q in range(S5_Q)]
    ai = [ai_ref[q] for q in range(S5_Q)]

    def step(t, carry):
        rows = pl.ds(pl.multiple_of(t * ns, ns), ns)
        out = []
        for q in range(S5_Q):
            pr, pi_ = carry[2 * q], carry[2 * q + 1]
            nr = ar[q] * pr - ai[q] * pi_ + xr_ref[q, rows, :]
            ni = ar[q] * pi_ + ai[q] * pr + xi_ref[q, rows, :]
            xr_ref[q, rows, :] = nr
            xi_ref[q, rows, :] = ni
            out += [nr, ni]
        return tuple(out)

    init = []
    for q in range(S5_Q):
        init += [st_ref[0, q], st_ref[1, q]]
    fin = lax.fori_loop(0, CHUNK, step, tuple(init), unroll=2)
    for q in range(S5_Q):
        st_ref[0, q] = fin[2 * q]
        st_ref[1, q] = fin[2 * q + 1]


def s5_fwd(pa, bre, bim, cre, cim, ar, ai, dvec, nb, name):
    rows = pa.shape[0]
    width = pa.shape[1] // 2
    ns = width // LANES
    nc = rows // nb // CHUNK

    def body(u_ref, bre_ref, bim_ref, cre_ref, cim_ref, ar_ref, ai_ref, d_ref, y_ref, g_ref, so_ref, xr_ref, xi_ref, st_ref):
        c = pl.program_id(1)

        @pl.when(c == 0)
        def _():
            st_ref[...] = jnp.zeros_like(st_ref)

        so_ref[...] = st_ref[...]
        u = u_ref[...]
        _s5_fill_bu(u, bre_ref, bim_ref, xr_ref, xi_ref, ns)
        _s5_scan(xr_ref, xi_ref, ar_ref, ai_ref, st_ref, ns)
        for s in range(ns):
            acc = jnp.zeros((CHUNK, LANES), F32)
            for q in range(S5_Q):
                xr = xr_ref[q, pl.ds(s, CHUNK, stride=ns), :].astype(BF16)
                xi = xi_ref[q, pl.ds(s, CHUNK, stride=ns), :].astype(BF16)
                acc = acc + _dot(xr, cre_ref[s, q * LANES:(q + 1) * LANES, :], ((1,), (0,)))
                acc = acc - _dot(xi, cim_ref[s, q * LANES:(q + 1) * LANES, :], ((1,), (0,)))
            cols = slice(s * LANES, (s + 1) * LANES)
            y = acc + d_ref[:, cols] * u[:, cols]
            y_ref[:, cols] = y
            g_ref[:, cols] = _gelu(y).astype(BF16)

    whole3 = lambda a: pl.BlockSpec(a.shape, lambda b_, c: (0, 0, 0))
    return pl.pallas_call(
        body, name=name, grid=(nb, nc),
        in_specs=[pl.BlockSpec((CHUNK, width), lambda b_, c: (b_ * nc + c, 0)), whole3(bre), whole3(bim), whole3(cre),
                  whole3(cim), whole3(ar), whole3(ai), pl.BlockSpec((1, width), lambda b_, c: (0, 0))],
        out_specs=[pl.BlockSpec((CHUNK, width), lambda b_, c: (b_ * nc + c, 0)),
                   pl.BlockSpec((CHUNK, width), lambda b_, c: (b_ * nc + c, 0)),
                   pl.BlockSpec((None, 2, S5_Q, ns, LANES), lambda b_, c: (b_ * nc + c, 0, 0, 0, 0))],
        out_shape=[jax.ShapeDtypeStruct((rows, width), F32), jax.ShapeDtypeStruct((rows, width), BF16),
                   jax.ShapeDtypeStruct((nb * nc, 2, S5_Q, ns, LANES), F32)],
        scratch_shapes=[pltpu.VMEM((S5_Q, CHUNK * ns, LANES), F32), pltpu.VMEM((S5_Q, CHUNK * ns, LANES), F32),
                        pltpu.VMEM((2, S5_Q, ns, LANES), F32)],
        compiler_params=_params(("arbitrary", "arbitrary")))(pa, bre, bim, cre, cim, ar, ai, dvec)


def s5_bwd(pa, dys, states, bre, bim, cre, cim, ar, ai, dvec, nb, name):
    rows = pa.shape[0]
    width = pa.shape[1] // 2
    ns = width // LANES
    nc = rows // nb // CHUNK

    def body(u_ref, dy_ref, sin_ref, bre_ref, bim_ref, cre_ref, cim_ref, ar_ref, ai_ref, d_ref,
             du_ref, dbre_ref, dbim_ref, dcre_ref, dcim_ref, dar_ref, dai_ref, dd_ref,
             xr_ref, xi_ref, lr_ref, li_ref, st_ref, lam_ref):
        bb = pl.program_id(0)
        step_i = pl.program_id(1)

        @pl.when(jnp.logical_and(bb == 0, step_i == 0))
        def _():
            for r in (dbre_ref, dbim_ref, dcre_ref, dcim_ref, dar_ref, dai_ref, dd_ref):
                r[...] = jnp.zeros_like(r)

        @pl.when(step_i == 0)
        def _():
            lam_ref[...] = jnp.zeros_like(lam_ref)

        u = u_ref[...]
        dy = dy_ref[...]
        st_ref[...] = sin_ref[...]
        _s5_fill_bu(u, bre_ref, bim_ref, xr_ref, xi_ref, ns)
        _s5_scan(xr_ref, xi_ref, ar_ref, ai_ref, st_ref, ns)
        dd_ref[...] += jnp.sum(dy * u, axis=0, keepdims=True)
        for s in range(ns):
            dyb = dy[:, s * LANES:(s + 1) * LANES].astype(BF16)
            gr = _dot(dyb, cre_ref[s], ((1,), (1,)))
            gi = -_dot(dyb, cim_ref[s], ((1,), (1,)))
            for q in range(S5_Q):
                lr_ref[q, pl.ds(s, CHUNK, stride=ns), :] = gr[:, q * LANES:(q + 1) * LANES]
                li_ref[q, pl.ds(s, CHUNK, stride=ns), :] = gi[:, q * LANES:(q + 1) * LANES]
                xr = xr_ref[q, pl.ds(s, CHUNK, stride=ns), :].astype(BF16)
                xi = xi_ref[q, pl.ds(s, CHUNK, stride=ns), :].astype(BF16)
                dcre_ref[s, q * LANES:(q + 1) * LANES, :] += _dot(xr, dyb, ((0,), (0,)))
                dcim_ref[s, q * LANES:(q + 1) * LANES, :] -= _dot(xi, dyb, ((0,), (0,)))
        ar = [ar_ref[q] for q in range(S5_Q)]
        ai = [ai_ref[q] for q in range(S5_Q)]

        def one(t_rows, p_r, p_i, carry):
            out = []
            for q in range(S5_Q):
                l_r, l_i, da_r, da_i = carry[4 * q:4 * q + 4]
                n_r = lr_ref[q, t_rows, :] + ar[q] * l_r + ai[q] * l_i
                n_i = li_ref[q, t_rows, :] + ar[q] * l_i - ai[q] * l_r
                lr_ref[q, t_rows, :] = n_r
                li_ref[q, t_rows, :] = n_i
                xpr, xpi = p_r(q), p_i(q)
                out += [n_r, n_i, da_r + n_r * xpr + n_i * xpi, da_i + n_i * xpr - n_r * xpi]
            return tuple(out)

        def step(k, carry):
            t = CHUNK - 1 - k
            t_rows = pl.ds(pl.multiple_of(t * ns, ns), ns)
            p_rows = pl.ds(pl.multiple_of((t - 1) * ns, ns), ns)
            return one(t_rows, lambda q: xr_ref[q, p_rows, :], lambda q: xi_ref[q, p_rows, :], carry)

        init = []
        zero = jnp.zeros((ns, LANES), F32)
        for q in range(S5_Q):
            init += [lam_ref[0, q], lam_ref[1, q], zero, zero]
        carry = lax.fori_loop(0, CHUNK - 1, step, tuple(init), unroll=2)
        carry = one(pl.ds(0, ns), lambda q: sin_ref[0, q], lambda q: sin_ref[1, q], carry)
        for q in range(S5_Q):
            lam_ref[0, q] = carry[4 * q]
            lam_ref[1, q] = carry[4 * q + 1]
            dar_ref[q] += carry[4 * q + 2]
            dai_ref[q] += carry[4 * q + 3]
        for s in range(ns):
            cols = slice(s * LANES, (s + 1) * LANES)
            ub = u[:, cols].astype(BF16)
            acc = d_ref[:, cols] * dy[:, cols]
            for q in range(S5_Q):
                qs = slice(q * LANES, (q + 1) * LANES)
                lr = lr_ref[q, pl.ds(s, CHUNK, stride=ns), :].astype(BF16)
                li = li_ref[q, pl.ds(s, CHUNK, stride=ns), :].astype(BF16)
                dbre_ref[s, :, qs] += _dot(ub, lr, ((0,), (0,)))
                dbim_ref[s, :, qs] += _dot(ub, li, ((0,), (0,)))
                acc = acc + _dot(lr, bre_ref[s, :, qs], ((1,), (1,))) + _dot(li, bim_ref[s, :, qs], ((1,), (1,)))
            du_ref[:, cols] = acc.astype(du_ref.dtype)

    whole3 = lambda a: pl.BlockSpec(a.shape, lambda b_, c: (0, 0, 0))
    rowblk = pl.BlockSpec((CHUNK, width), lambda b_, c: (b_ * nc + nc - 1 - c, 0))
    scr = pltpu.VMEM((S5_Q, CHUNK * ns, LANES), F32)
    return pl.pallas_call(
        body, name=name, grid=(nb, nc),
        in_specs=[rowblk, rowblk,
                  pl.BlockSpec((None, 2, S5_Q, ns, LANES), lambda b_, c: (b_ * nc + nc - 1 - c, 0, 0, 0, 0)),
                  whole3(bre), whole3(bim), whole3(cre), whole3(cim), whole3(ar), whole3(ai),
                  pl.BlockSpec((1, width), lambda b_, c: (0, 0))],
        out_specs=[rowblk, whole3(bre), whole3(bim), whole3(cre), whole3(cim), whole3(ar), whole3(ai),
                   pl.BlockSpec((1, width), lambda b_, c: (0, 0))],
        out_shape=[jax.ShapeDtypeStruct((rows, width), BF16), jax.ShapeDtypeStruct(bre.shape, F32),
                   jax.ShapeDtypeStruct(bim.shape, F32), jax.ShapeDtypeStruct(cre.shape, F32),
                   jax.ShapeDtypeStruct(cim.shape, F32), jax.ShapeDtypeStruct(ar.shape, F32),
                   jax.ShapeDtypeStruct(ai.shape, F32), jax.ShapeDtypeStruct((1, width), F32)],
        scratch_shapes=[scr, scr, scr, scr, pltpu.VMEM((2, S5_Q, ns, LANES), F32), pltpu.VMEM((2, S5_Q, ns, LANES), F32)],
        compiler_params=_params(("arbitrary", "arbitrary")))(pa, dys, states, bre, bim, cre, cim, ar, ai, dvec)


def _s5_discretize(lam_re, lam_im, log_dt, b_re, b_im):
    dt = jnp.exp(log_dt)[:, None]
    mag = jnp.exp(lam_re * dt)
    ar, ai = mag * jnp.cos(lam_im * dt), mag * jnp.sin(lam_im * dt)
    den = lam_re * lam_re + lam_im * lam_im
    qr = ((ar - 1.0) * lam_re + ai * lam_im) / den
    qi = (ai * lam_re - (ar - 1.0) * lam_im) / den
    bbr = qr[..., None] * b_re - qi[..., None] * b_im
    bbi = qr[..., None] * b_im + qi[..., None] * b_re
    return ar, ai, bbr, bbi


def _s5_expand(ar, ai, bbr, bbi, c_re, c_im):
    g, p, h = bbr.shape
    gps = LANES // h
    ns = g // gps
    eye = jnp.eye(gps, dtype=F32)

    def bexp(b):
        return jnp.einsum("sgph,gk->sghkp", b.reshape(ns, gps, p, h), eye).reshape(ns, gps * h, gps * p)

    def cexp(c):
        return jnp.einsum("sghp,gk->sgpkh", c.reshape(ns, gps, h, p), eye).reshape(ns, gps * p, gps * h)

    def aexp(a):
        return a.reshape(ns, S5_Q, LANES).transpose(1, 0, 2)

    return (bexp(bbr).astype(BF16), bexp(bbi).astype(BF16), cexp(c_re).astype(BF16), cexp(c_im).astype(BF16),
            aexp(ar), aexp(ai))


def _s5_contract(dbre, dbim, dcre, dcim, dar, dai, g, p, h):
    gps = LANES // h
    ns = g // gps
    eye = jnp.eye(gps, dtype=F32)
    bcon = lambda d: jnp.einsum("sghkp,gk->sgph", d.reshape(ns, gps, h, gps, p), eye).reshape(g, p, h)
    ccon = lambda d: jnp.einsum("sgpkh,gk->sghp", d.reshape(ns, gps, p, gps, h), eye).reshape(g, h, p)
    acon = lambda d: d.transpose(1, 0, 2).reshape(g, p)
    return bcon(dbre), bcon(dbim), ccon(dcre), ccon(dcim), acon(dar), acon(dai)


def _ml_proj_tile(cpre, xb, wq, wk, wv, gq, gk, gv):
    xc = _silu(cpre)
    q = _dot_nn(xc, wq)
    k = _dot_nn(xc, wk)
    v = _dot_nn(xb, wv)
    return q, k, v, _dot_nn(q, gq) + _dot_nn(k, gk) + _dot_nn(v, gv)


def ml_proj_fwd(cpre, xb, wq, wk, wv, gq, gk, gv, name):
    rows, width = cpre.shape
    nblk = width // LANES
    tr = _tile(rows, 1088, 16)

    def body(c_ref, x_ref, wq_ref, wk_ref, wv_ref, gq_ref, gk_ref, gv_ref, q_ref, k_ref, v_ref, g_ref):
        j = pl.program_id(1)
        q, k, v, g = _ml_proj_tile(c_ref[...], x_ref[...], wq_ref[...], wk_ref[...], wv_ref[...],
                                   gq_ref[...], gk_ref[...], gv_ref[...])
        q_ref[...] = q
        k_ref[...] = k
        v_ref[...] = v

        @pl.when(j == 0)
        def _():
            g_ref[...] = jnp.zeros_like(g_ref)

        g_ref[...] += g

    rb = pl.BlockSpec((tr, LANES), lambda i, j: (i, j))
    wb = pl.BlockSpec((None, LANES, LANES), lambda i, j: (j, 0, 0))
    return pl.pallas_call(
        body, name=name, grid=(rows // tr, nblk), in_specs=[rb, rb, wb, wb, wb, wb, wb, wb],
        out_specs=[rb, rb, rb, pl.BlockSpec((tr, LANES), lambda i, j: (i, 0))],
        out_shape=[jax.ShapeDtypeStruct((rows, width), F32)] * 3 + [jax.ShapeDtypeStruct((rows, LANES), F32)],
        compiler_params=_params(("arbitrary", "arbitrary")))(cpre, xb, wq, wk, wv, gq, gk, gv)


def ml_proj_bwd(cpre, xb, wq, wk, wv, gq, gk, gv, dq, dk, dv, dg, dcp_extra, name):
    rows, width = cpre.shape
    nblk = width // LANES
    tr = _tile(rows, 1088, 16)

    def body(c_ref, x_ref, wq_ref, wk_ref, wv_ref, gq_ref, gk_ref, gv_ref, dq_ref, dk_ref, dv_ref, dg_ref, e_ref,
             dc_ref, dx_ref, *dw_refs):
        i = pl.program_id(1)
        _, vjp = jax.vjp(_ml_proj_tile, c_ref[...], x_ref[...], wq_ref[...], wk_ref[...], wv_ref[...],
                         gq_ref[...], gk_ref[...], gv_ref[...])
        grads = vjp((dq_ref[...], dk_ref[...], dv_ref[...], dg_ref[...]))
        dc_ref[...] = grads[0] + e_ref[...]
        dx_ref[...] = grads[1]

        @pl.when(i == 0)
        def _():
            for r in dw_refs:
                r[...] = jnp.zeros_like(r)

        for r, gval in zip(dw_refs, grads[2:]):
            r[...] += gval

    rb = pl.BlockSpec((tr, LANES), lambda j, i: (i, j))
    wb = pl.BlockSpec((None, LANES, LANES), lambda j, i: (j, 0, 0))
    gb = pl.BlockSpec((tr, LANES), lambda j, i: (i, 0))
    wshape = jax.ShapeDtypeStruct((nblk, LANES, LANES), F32)
    return pl.pallas_call(
        body, name=name, grid=(nblk, rows // tr), in_specs=[rb, rb, wb, wb, wb, wb, wb, wb, rb, rb, rb, gb, rb],
        out_specs=[rb, rb] + [wb] * 6,
        out_shape=[jax.ShapeDtypeStruct((rows, width), F32)] * 2 + [wshape] * 6,
        compiler_params=_params(("arbitrary", "arbitrary")))(cpre, xb, wq, wk, wv, gq, gk, gv, dq, dk, dv, dg, dcp_extra)


def _ml_gates_tile(gl, bg, nh):
    x = gl + bg
    bcum = _dot(_tri(CHUNK), _log_sigmoid(x), ((1,), (0,)), precision=HI)
    lane = lax.broadcasted_iota(jnp.int32, x.shape, 1)
    return jnp.where(lane < nh, x, jnp.where(lane < 2 * nh, bcum, 0.0))


def _ml_core_tile(q, k, v, colg, rowg, cpre, zb, nw, sk, cst, nst, m_prev):
    c, dh = q.shape
    igc, bc = _lane_pick(colg, 0), _lane_pick(colg, 1)
    igr, br = _row_pick(rowg, 0), _row_pick(rowg, 1)
    causal = _tri(c) > 0
    dmat = jnp.where(causal, bc - br + igr, -jnp.inf)
    inter = bc + m_prev
    mt = lax.stop_gradient(jnp.maximum(inter, jnp.max(dmat, axis=1, keepdims=True)))
    wt = jnp.exp(dmat - mt)
    w_prev = jnp.exp(inter - mt)
    qs = q * (dh ** -0.5)
    s = _dot_nt(qs, k) * wt
    num = _dot_nn(s, v) + w_prev * _dot_nn(qs, cst)
    den = jnp.sum(s, axis=1, keepdims=True) + w_prev * jnp.sum(qs * nst, axis=1, keepdims=True)
    h = num * (1.0 / jnp.maximum(jnp.abs(den), jnp.exp(-mt)))
    last = (lax.broadcasted_iota(jnp.int32, (c, 1), 0) == c - 1).astype(F32)
    blast = jnp.sum(bc * last, axis=0, keepdims=True)
    g = blast - bc + igc
    m_new = lax.stop_gradient(jnp.maximum(blast + m_prev, jnp.max(g, axis=0, keepdims=True)))
    decay = jnp.exp(blast + m_prev - m_new)
    wk = jnp.exp(g - m_new) * k
    c_new = decay * cst + _dot_tn(wk, v)
    n_new = decay * nst + jnp.sum(wk, axis=0, keepdims=True)
    mu = jnp.mean(h, axis=1, keepdims=True)
    hc = h - mu
    var = jnp.mean(hc * hc, axis=1, keepdims=True)
    out = hc * lax.rsqrt(var + HEAD_NORM_EPS) * nw + sk * _silu(cpre)
    return out * _silu(zb), c_new, n_new, m_new


def _ml_core_specs(nc, dh, rev):
    ch = (lambda c: nc - 1 - c) if rev else (lambda c: c)
    rb = pl.BlockSpec((CHUNK, dh), lambda b_, c, h: (b_ * nc + ch(c), h))
    colb = pl.BlockSpec((None, CHUNK, 2), lambda b_, c, h: (h, b_ * nc + ch(c), 0))
    rowb = pl.BlockSpec((None, None, 2, CHUNK), lambda b_, c, h: (b_ * nc + ch(c), h, 0, 0))
    pb = pl.BlockSpec((1, dh), lambda b_, c, h: (0, h))
    cb = pl.BlockSpec((None, None, dh, dh), lambda b_, c, h: (b_ * nc + ch(c), h, 0, 0))
    nb_ = pl.BlockSpec((None, None, 1, dh), lambda b_, c, h: (b_ * nc + ch(c), h, 0, 0))
    mb = pl.BlockSpec((None, None, 1, 1), lambda b_, c, h: (b_ * nc + ch(c), h, 0, 0))
    return rb, colb, rowb, pb, cb, nb_, mb


def ml_core_fwd(q, k, v, colg, rowg, cpre, zb, nw, sk, nb, nh, name):
    rows, width = q.shape
    dh = width // nh
    nc = rows // nb // CHUNK
    rb, colb, rowb, pb, cb, nb_, mb = _ml_core_specs(nc, dh, False)

    def body(q_ref, k_ref, v_ref, col_ref, row_ref, c_ref, z_ref, nw_ref, sk_ref, y_ref, cs_ref, ns_ref, ms_ref,
             cst_ref, nst_ref, mst_ref):
        c = pl.program_id(1)
        h = pl.program_id(2)

        @pl.when(c == 0)
        def _():
            cst_ref[h] = jnp.zeros((dh, dh), F32)
            nst_ref[h] = jnp.zeros((1, dh), F32)
            mst_ref[h] = jnp.zeros((1, 1), F32)

        cst, nst, m_prev = cst_ref[h], nst_ref[h], mst_ref[h]
        cs_ref[...] = cst
        ns_ref[...] = nst
        ms_ref[...] = m_prev
        y, c_new, n_new, m_new = _ml_core_tile(q_ref[...], k_ref[...], v_ref[...], col_ref[...], row_ref[...],
                                               c_ref[...], z_ref[...], nw_ref[...], sk_ref[...], cst, nst, m_prev)
        y_ref[...] = y.astype(BF16)
        cst_ref[h] = c_new
        nst_ref[h] = n_new
        mst_ref[h] = m_new

    nbc = nb * nc
    return pl.pallas_call(
        body, name=name, grid=(nb, nc, nh), in_specs=[rb, rb, rb, colb, rowb, rb, rb, pb, pb],
        out_specs=[rb, cb, nb_, mb],
        out_shape=[jax.ShapeDtypeStruct((rows, width), BF16), jax.ShapeDtypeStruct((nbc, nh, dh, dh), F32),
                   jax.ShapeDtypeStruct((nbc, nh, 1, dh), F32), jax.ShapeDtypeStruct((nbc, nh, 1, 1), F32)],
        scratch_shapes=[pltpu.VMEM((nh, dh, dh), F32), pltpu.VMEM((nh, 1, dh), F32), pltpu.VMEM((nh, 1, 1), F32)],
        compiler_params=_params(("arbitrary", "arbitrary", "arbitrary")))(q, k, v, colg, rowg, cpre, zb, nw, sk)


def ml_core_bwd(q, k, v, colg, rowg, cpre, zb, nw, sk, cs, ns, ms, dy, nb, nh, name):
    rows, width = q.shape
    dh = width // nh
    nc = rows // nb // CHUNK
    rb, colb, rowb, pb, cb, nb_, mb = _ml_core_specs(nc, dh, True)

    def body(q_ref, k_ref, v_ref, col_ref, row_ref, c_ref, z_ref, nw_ref, sk_ref, cs_ref, ns_ref, ms_ref, dy_ref,
             dq_ref, dk_ref, dv_ref, dc_ref, dz_ref, dcol_ref, drow_ref, dnw_ref, dsk_ref, dcst_ref, dnst_ref):
        bb = pl.program_id(0)
        step = pl.program_id(1)
        h = pl.program_id(2)

        @pl.when(jnp.logical_and(bb == 0, jnp.logical_and(step == 0, h == 0)))
        def _():
            dnw_ref[...] = jnp.zeros_like(dnw_ref)
            dsk_ref[...] = jnp.zeros_like(dsk_ref)

        @pl.when(step == 0)
        def _():
            dcst_ref[h] = jnp.zeros((dh, dh), F32)
            dnst_ref[h] = jnp.zeros((1, dh), F32)

        m_prev = ms_ref[...]

        def f(*a):
            return _ml_core_tile(*a, m_prev)[:3]

        _, vjp = jax.vjp(f, q_ref[...], k_ref[...], v_ref[...], col_ref[...], row_ref[...], c_ref[...], z_ref[...],
                         nw_ref[...], sk_ref[...], cs_ref[...], ns_ref[...])
        g = vjp((dy_ref[...], dcst_ref[h], dnst_ref[h]))
        dq_ref[...] = g[0]
        dk_ref[...] = g[1]
        dv_ref[...] = g[2]
        dcol_ref[...] = g[3]
        drow_ref[...] = g[4]
        dc_ref[...] = g[5]
        dz_ref[...] = g[6].astype(dz_ref.dtype)
        dnw_ref[h] += g[7]
        dsk_ref[h] += g[8]
        dcst_ref[h] = g[9]
        dnst_ref[h] = g[10]

    nbc = nb * nc
    accb = pl.BlockSpec((nh, 1, dh), lambda b_, c, h: (0, 0, 0))
    return pl.pallas_call(
        body, name=name, grid=(nb, nc, nh), in_specs=[rb, rb, rb, colb, rowb, rb, rb, pb, pb, cb, nb_, mb, rb],
        out_specs=[rb, rb, rb, rb, rb, colb, rowb, accb, accb],
        out_shape=[jax.ShapeDtypeStruct((rows, width), F32)] * 4 + [jax.ShapeDtypeStruct((rows, width), BF16)]
        + [jax.ShapeDtypeStruct(colg.shape, F32), jax.ShapeDtypeStruct(rowg.shape, F32),
           jax.ShapeDtypeStruct((nh, 1, dh), F32), jax.ShapeDtypeStruct((nh, 1, dh), F32)],
        scratch_shapes=[pltpu.VMEM((nh, dh, dh), F32), pltpu.VMEM((nh, 1, dh), F32)],
        compiler_params=_params(("arbitrary", "arbitrary", "arbitrary")))(
            q, k, v, colg, rowg, cpre, zb, nw, sk, cs, ns, ms, dy)


def _ssd_dt_tile(dtr, bias, alog):
    dt = _softplus(dtr + bias)
    cum = _dot(_tri(CHUNK), dt * (-jnp.exp(alog)), ((1,), (0,)), precision=HI)
    return dt, cum


def _ssd_tile(xcs, bmc, cmc, cols, rows_, z, dvec, gn, states, hpg):
    npair = hpg // 2
    hd = SSD_HEAD_DIM
    xs = [_silu(x) for x in xcs]
    bm, cm = _silu(bmc), _silu(cmc)
    cb = _dot_nt(cm, bm)
    causal = _tri(CHUNK) > 0
    lane_lo = lax.broadcasted_iota(jnp.int32, (1, 2 * hd), 1) < hd
    lastsel = (lax.broadcasted_iota(jnp.int32, (CHUNK, 1), 0) == CHUNK - 1).astype(F32)
    heads = []
    for r in range(hpg):
        dtc, cumc = _lane_pick(cols, r), _lane_pick(cols, hpg + r)
        dtrow, cumr = _row_pick(rows_, r), _row_pick(rows_, hpg + r)
        w = cb * jnp.exp(jnp.where(causal, cumc - cumr, -jnp.inf)) * dtrow
        last = jnp.sum(cumc * lastsel, axis=0, keepdims=True)
        heads.append((w, jnp.exp(cumc), jnp.exp(last - cumc) * dtc, jnp.exp(last)))
    ys, new_states = [], []
    for j in range(npair):
        (wa, ea, da, la), (wb, eb, db, lb) = heads[2 * j], heads[2 * j + 1]
        yi = jnp.where(lane_lo, _dot_nn(wa, xs[j]), _dot_nn(wb, xs[j]))
        ys.append(yi + jnp.where(lane_lo, ea, eb) * _dot_nn(cm, states[j]))
        xd = xs[j] * jnp.where(lane_lo, da, db)
        new_states.append(jnp.where(lane_lo, la, lb) * states[j] + _dot_tn(bm, xd))
    y = jnp.concatenate(ys, axis=1) + dvec * jnp.concatenate(xs, axis=1)
    yg = y * _silu(z)
    yn = yg * lax.rsqrt(jnp.mean(yg * yg, axis=1, keepdims=True) + NORM_EPS) * gn
    return yn, new_states


def _ssd_specs(nc, hpg, ng, rev):
    npair = hpg // 2
    gw = hpg * SSD_HEAD_DIM
    xblocks = ng * npair
    ch = (lambda c: nc - 1 - c) if rev else (lambda c: c)
    xs = [pl.BlockSpec((CHUNK, LANES), functools.partial(lambda b_, c, g, jj: (b_ * nc + ch(c), g * npair + jj), jj=j))
          for j in range(npair)]
    bmb = pl.BlockSpec((CHUNK, SSD_STATE), lambda b_, c, g: (b_ * nc + ch(c), xblocks + g))
    cmb = pl.BlockSpec((CHUNK, SSD_STATE), lambda b_, c, g: (b_ * nc + ch(c), xblocks + ng + g))
    colb = pl.BlockSpec((None, CHUNK, 2 * hpg), lambda b_, c, g: (g, b_ * nc + ch(c), 0))
    rowb = pl.BlockSpec((None, None, 2 * hpg, CHUNK), lambda b_, c, g: (b_ * nc + ch(c), g, 0, 0))
    zb = pl.BlockSpec((CHUNK, gw), lambda b_, c, g: (b_ * nc + ch(c), g))
    pb = pl.BlockSpec((1, gw), lambda b_, c, g: (0, g))
    sb = pl.BlockSpec((None, None, npair, SSD_STATE, 2 * SSD_HEAD_DIM), lambda b_, c, g: (b_ * nc + ch(c), g, 0, 0, 0))
    return xs, bmb, cmb, colb, rowb, zb, pb, sb


def ssd_core_fwd(cpre, cols, rows_, z, dvec, gn, nb, hpg, name):
    rows = cpre.shape[0]
    inner = z.shape[1]
    ng = inner // (hpg * SSD_HEAD_DIM)
    npair = hpg // 2
    nc = rows // nb // CHUNK
    xs, bmb, cmb, colb, rowb, zb, pb, sb = _ssd_specs(nc, hpg, ng, False)

    def body(*refs):
        x_refs = refs[:npair]
        bm_ref, cm_ref, col_ref, row_ref, z_ref, d_ref, gn_ref, y_ref, so_ref, st_ref = refs[npair:]
        c = pl.program_id(1)
        g = pl.program_id(2)

        @pl.when(c == 0)
        def _():
            st_ref[g] = jnp.zeros((npair, SSD_STATE, 2 * SSD_HEAD_DIM), F32)

        so_ref[...] = st_ref[g]
        states = [st_ref[g, j] for j in range(npair)]
        yn, new_states = _ssd_tile([r[...] for r in x_refs], bm_ref[...], cm_ref[...], col_ref[...], row_ref[...],
                                   z_ref[...], d_ref[...], gn_ref[...], states, hpg)
        y_ref[...] = yn.astype(BF16)
        for j in range(npair):
            st_ref[g, j] = new_states[j]

    return pl.pallas_call(
        body, name=name, grid=(nb, nc, ng), in_specs=xs + [bmb, cmb, colb, rowb, zb, pb, pb],
        out_specs=[zb, sb],
        out_shape=[jax.ShapeDtypeStruct((rows, inner), BF16),
                   jax.ShapeDtypeStruct((nb * nc, ng, npair, SSD_STATE, 2 * SSD_HEAD_DIM), F32)],
        scratch_shapes=[pltpu.VMEM((ng, npair, SSD_STATE, 2 * SSD_HEAD_DIM), F32)],
        compiler_params=_params(("arbitrary", "arbitrary", "arbitrary")))(
            *([cpre] * npair), cpre, cpre, cols, rows_, z, dvec, gn)


def ssd_core_bwd(cpre, cols, rows_, z, dvec, gn, states, dyn, nb, hpg, name):
    rows = cpre.shape[0]
    inner = z.shape[1]
    gw = hpg * SSD_HEAD_DIM
    ng = inner // gw
    npair = hpg // 2
    nc = rows // nb // CHUNK
    xs, bmb, cmb, colb, rowb, zb, pb, sb = _ssd_specs(nc, hpg, ng, True)

    def body(*refs):
        x_refs = refs[:npair]
        (bm_ref, cm_ref, col_ref, row_ref, z_ref, d_ref, gn_ref, s_ref, dy_ref,
         dx_ref, dbm_ref, dcm_ref, dcol_ref, drow_ref, dz_ref, dd_ref, dgn_ref, dst_ref) = refs[npair:]
        bb = pl.program_id(0)
        step = pl.program_id(1)
        g = pl.program_id(2)

        @pl.when(jnp.logical_and(bb == 0, jnp.logical_and(step == 0, g == 0)))
        def _():
            dd_ref[...] = jnp.zeros_like(dd_ref)
            dgn_ref[...] = jnp.zeros_like(dgn_ref)

        @pl.when(step == 0)
        def _():
            dst_ref[g] = jnp.zeros((npair, SSD_STATE, 2 * SSD_HEAD_DIM), F32)

        def f(xcs, bmc, cmc, cv, rv, zv, dv_, gv, sts):
            return _ssd_tile(xcs, bmc, cmc, cv, rv, zv, dv_, gv, sts, hpg)

        _, vjp = jax.vjp(f, [r[...] for r in x_refs], bm_ref[...], cm_ref[...], col_ref[...], row_ref[...], z_ref[...],
                         d_ref[...], gn_ref[...], [s_ref[j] for j in range(npair)])
        gr = vjp((dy_ref[...], [dst_ref[g, j] for j in range(npair)]))
        dx_ref[...] = jnp.concatenate(gr[0], axis=1)
        dbm_ref[...] = gr[1]
        dcm_ref[...] = gr[2]
        dcol_ref[...] = gr[3]
        drow_ref[...] = gr[4]
        dz_ref[...] = gr[5].astype(dz_ref.dtype)
        dd_ref[g] += gr[6]
        dgn_ref[g] += gr[7]
        for j in range(npair):
            dst_ref[g, j] = gr[8][j]

    ch = lambda c: nc - 1 - c
    nblk = pl.BlockSpec((CHUNK, SSD_STATE), lambda b_, c, g: (b_ * nc + ch(c), g))
    accb = pl.BlockSpec((ng, 1, gw), lambda b_, c, g: (0, 0, 0))
    return pl.pallas_call(
        body, name=name, grid=(nb, nc, ng), in_specs=xs + [bmb, cmb, colb, rowb, zb, pb, pb, sb, zb],
        out_specs=[zb, nblk, nblk, colb, rowb, zb, accb, accb],
        out_shape=[jax.ShapeDtypeStruct((rows, inner), F32), jax.ShapeDtypeStruct((rows, ng * SSD_STATE), F32),
                   jax.ShapeDtypeStruct((rows, ng * SSD_STATE), F32), jax.ShapeDtypeStruct(cols.shape, F32),
                   jax.ShapeDtypeStruct(rows_.shape, F32), jax.ShapeDtypeStruct((rows, inner), BF16),
                   jax.ShapeDtypeStruct((ng, 1, gw), F32), jax.ShapeDtypeStruct((ng, 1, gw), F32)],
        scratch_shapes=[pltpu.VMEM((ng, npair, SSD_STATE, 2 * SSD_HEAD_DIM), F32)],
        compiler_params=_params(("arbitrary", "arbitrary", "arbitrary")))(
            *([cpre] * npair), cpre, cpre, cols, rows_, z, dvec, gn, states, dyn)


def _hw_expand(w):
    n, bi, _ = w.shape
    per = LANES // bi
    eye = jnp.eye(per, dtype=F32)
    return jnp.einsum("jbio,bc->jbico", w.reshape(n // per, per, bi, bi), eye).reshape(n // per, LANES, LANES)


def _hw_contract(d, bi=QKV_BLOCK):
    per = LANES // bi
    eye = jnp.eye(per, dtype=F32)
    return jnp.einsum("jbico,bc->jbio", d.reshape(d.shape[0], per, bi, per, bi), eye).reshape(-1, bi, bi)


def _wg_expand(wg, width):
    pad = jnp.pad(wg, ((0, 0), (0, LANES - wg.shape[1])))
    return [pad[i * width:(i + 1) * width].reshape(width // LANES, LANES, LANES) for i in range(3)]


def _wg_contract(dgs, ngate):
    return jnp.concatenate([d[:, :, :ngate].reshape(-1, ngate) for d in dgs], axis=0)


def _pad_lanes(a):
    return jnp.pad(a, ((0, 0), (0, LANES - a.shape[1])))


def _pairs_to_layouts(first, second, ngrp, per, nbc):
    rows = first.shape[0]
    both = jnp.concatenate([first.reshape(rows, ngrp, per), second.reshape(rows, ngrp, per)], axis=2)
    return both.transpose(1, 0, 2), both.reshape(nbc, CHUNK, ngrp, 2 * per).transpose(0, 2, 3, 1)


def _layouts_to_pairs(dcols, drows, ngrp, per):
    rows = dcols.shape[1]
    both = dcols.transpose(1, 0, 2) + drows.transpose(0, 3, 1, 2).reshape(rows, ngrp, 2 * per)
    return both[:, :, :per].reshape(rows, ngrp * per), both[:, :, per:].reshape(rows, ngrp * per)


_EARLY = ("W0a", "W0xb", "W0zb", "glu")
_LATE = ("Wo0a", "Wo0b", "W1z", "W1x", "W1dt", "Wo1")


def _local_step(x, target, bw, sp, late_weights=None, late_grads=None, early_grads=None):
    nb, seq, d = x.shape
    nh, hpg = MLSTM_HEADS, SSD_HPG
    t_len = N_META + seq
    nc = -(-t_len // CHUNK)
    tp = nc * CHUNK
    rows = nb * tp
    nbc = nb * nc
    meta = sp["meta_tokens"]
    h0 = jnp.concatenate([jnp.broadcast_to(meta[None], (nb, N_META, d)), x, jnp.zeros((nb, tp - t_len, d), F32)], axis=1)
    h0 = h0.reshape(rows, d)
    tgt = jnp.pad(target, ((0, 0), (N_META, tp - t_len), (0, 0))).reshape(rows, d)

    n0 = norm_fwd(h0, sp["ab_norm"], "norm0")
    pa = mm(n0, bw["W0a"], "nn", "mm_pa")
    xb = mm(n0, bw["W0xb"], "nn", "mm_xb")
    zb = mm(n0, bw["W0zb"], "nn", "mm_zb")
    s5w = pa.shape[1] // 2
    mlw = xb.shape[1]
    s5_args = (sp["s5_lambda_re"], sp["s5_lambda_im"], sp["s5_log_dt"].reshape(-1), sp["s5_b_re"], sp["s5_b_im"])
    (ar, ai, bbr, bbi), s5_disc_vjp = jax.vjp(_s5_discretize, *s5_args)
    sg, spn, shh = bbr.shape
    bre, bim, cre, cim, are, aie = _s5_expand(ar, ai, bbr, bbi, sp["s5_c_re"], sp["s5_c_im"])
    ys5, gb, s5st = s5_fwd(pa, bre, bim, cre, cim, are, aie, sp["s5_d"], nb, "s5_fwd")
    tglu = mm(gb, bw["glu"], "nn", "mm_glu")

    def glu_tile(ys, tt, za, gbias):
        return _gelu(ys) * _sigmoid(tt + gbias) * _silu(za)

    ya = rowwise("glu_fwd", lambda i, ys, tt, pab, gbias: glu_tile(ys, tt, pab[:, s5w:], gbias),
                 [ys5, tglu, pa], [sp["s5_glu_b"]], [(s5w, BF16)], tr=_tile(rows, 256, 16))[0]

    cpre0 = conv_fwd(xb, sp["ml_conv_w"], sp["ml_conv_b"], nb, "ml_conv_fwd")
    wq_e, wk_e, wv_e = _hw_expand(sp["ml_wq"]), _hw_expand(sp["ml_wk"]), _hw_expand(sp["ml_wv"])
    gq, gk, gv = _wg_expand(sp["ml_w_gate"], mlw)
    q, k, v, gl = ml_proj_fwd(cpre0, xb, wq_e, wk_e, wv_e, gq, gk, gv, "ml_proj_fwd")
    bgate = _pad_lanes(sp["ml_b_gate"])
    gout = rowwise("ml_gates_fwd", lambda i, g_, b_: _ml_gates_tile(g_, b_, nh), [gl], [bgate], [(LANES, F32)], tr=CHUNK)[0]
    colg, rowg = _pairs_to_layouts(gout[:, :nh], gout[:, nh:2 * nh], nh, 1, nbc)
    yb, ml_cs, ml_ns, ml_ms = ml_core_fwd(q, k, v, colg, rowg, cpre0, zb, sp["ml_norm"], sp["ml_skip"], nb, nh, "ml_core_fwd")
    if late_weights is not None:
        bw = {**bw, **late_weights()}
    h1 = mm(ya, bw["Wo0a"], "nn", "mm_out0a", resid=h0)
    h1 = mm(yb, bw["Wo0b"], "nn", "mm_out0b", resid=h1)

    n1 = norm_fwd(h1, sp["ssd_norm"], "norm1")
    z1 = mm(n1, bw["W1z"], "nn", "mm_z1")
    xbc = mm(n1, bw["W1x"], "nn", "mm_xbc")
    dtr = mm(n1, bw["W1dt"], "nn", "mm_dt")
    inner = z1.shape[1]
    ng = inner // (hpg * SSD_HEAD_DIM)
    nhd = ng * hpg
    cpre1 = conv_fwd(xbc, sp["ssd_conv_w"], sp["ssd_conv_b"], nb, "ssd_conv_fwd")
    dt_bias, a_log = _pad_lanes(sp["ssd_dt_bias"]), _pad_lanes(sp["ssd_a_log"])
    dt, cum = rowwise("ssd_dt_fwd", lambda i, r_, b_, a_: _ssd_dt_tile(r_, b_, a_), [dtr], [dt_bias, a_log],
                      [(LANES, F32), (LANES, F32)], tr=CHUNK)
    cols, rws = _pairs_to_layouts(dt[:, :nhd], cum[:, :nhd], ng, hpg, nbc)
    dvec = jnp.repeat(sp["ssd_d"], SSD_HEAD_DIM, axis=1)
    yn, ssd_st = ssd_core_fwd(cpre1, cols, rws, z1, dvec, sp["ssd_gnorm"], nb, hpg, "ssd_core_fwd")
    h2 = mm(yn, bw["Wo1"], "nn", "mm_out1", resid=h1)

    tr_l = _tile(tp, 256, 16)
    per_ex = tp // tr_l

    def loss_tile(i, hb, tb, gfn):
        tpos = (i % per_ex) * tr_l + lax.broadcasted_iota(jnp.int32, (tr_l, 1), 0)
        mask = jnp.logical_and(tpos >= N_META, tpos < t_len).astype(F32)

        def lf(hh, gg):
            e = (_rms(hh, gg) - tb) * mask
            return 0.5 * jnp.sum(e * e) / d

        lval, (dh, dg) = jax.value_and_grad(lf, (0, 1))(hb, gfn)
        return dh, dh, jnp.full((1, LANES), lval, F32), dg

    fn = sp["final_norm"].reshape(1, d)
    dh2, dh2b, loss_acc, dfn = rowwise("loss", loss_tile, [h2, tgt], [fn], [(d, F32), (d, BF16)], [(1, LANES), (1, d)], tr=tr_l)

    gbig, gs = {}, {}
    gs["final_norm"] = dfn.reshape(sp["final_norm"].shape)
    dyn = mm(dh2b, bw["Wo1"], "nt", "mm_dyn")
    gbig["Wo1"] = mm(yn, dh2b, "tn", "mm_dWo1", out_dtype=BF16)
    dxs, dbm, dcm, dcols, drws, dz1, ddvec, dgn = ssd_core_bwd(cpre1, cols, rws, z1, dvec, sp["ssd_gnorm"], ssd_st, dyn,
                                                              nb, hpg, "ssd_core_bwd")
    gs["ssd_d"] = ddvec.reshape(1, nhd, SSD_HEAD_DIM).sum(axis=2)
    gs["ssd_gnorm"] = dgn.reshape(1, inner)
    ddt, dcum = _layouts_to_pairs(dcols, drws, ng, hpg)

    def ssd_dt_bwd_tile(i, r_, ddt_, dcum_, b_, a_):
        _, vjp = jax.vjp(_ssd_dt_tile, r_, b_, a_)
        return vjp((ddt_, dcum_))

    ddtr, dbias, dalog = rowwise("ssd_dt_bwd", ssd_dt_bwd_tile, [dtr, _pad_lanes(ddt), _pad_lanes(dcum)], [dt_bias, a_log],
                                 [(LANES, BF16)], [(1, LANES), (1, LANES)], tr=CHUNK)
    gs["ssd_dt_bias"] = dbias[:, :nhd]
    gs["ssd_a_log"] = dalog[:, :nhd]
    dcpre1 = jnp.concatenate([dxs, dbm, dcm], axis=1)
    dxbc, dcw1, dcb1 = conv_bwd(dcpre1, xbc, sp["ssd_conv_w"], nb, "ssd_conv_bwd")
    gs["ssd_conv_w"] = dcw1
    gs["ssd_conv_b"] = dcb1
    dn1 = mm(dz1, bw["W1z"], "nt", "mm_dn1z")
    dn1 = mm(dxbc, bw["W1x"], "nt", "mm_dn1x", resid=dn1)
    dn1 = mm(ddtr, bw["W1dt"], "nt", "mm_dn1dt", resid=dn1)
    gbig["W1z"] = mm(n1, dz1, "tn", "mm_dW1z", out_dtype=BF16)
    gbig["W1x"] = mm(n1, dxbc, "tn", "mm_dW1x", out_dtype=BF16)
    gbig["W1dt"] = mm(n1, ddtr, "tn", "mm_dW1dt", out_dtype=BF16)
    dh1, dh1b, dg1 = norm_bwd(h1, sp["ssd_norm"], dn1, dh2, "norm1_bwd")
    gs["ssd_norm"] = dg1

    gbig["Wo0a"] = mm(ya, dh1b, "tn", "mm_dWo0a", out_dtype=BF16)
    gbig["Wo0b"] = mm(yb, dh1b, "tn", "mm_dWo0b", out_dtype=BF16)
    late_reduce = late_grads({n: gbig[n] for n in _LATE}) if late_grads is not None else None
    dya = mm(dh1b, bw["Wo0a"], "nt", "mm_dya")
    dyb = mm(dh1b, bw["Wo0b"], "nt", "mm_dyb")
    (dq, dk, dv, dcp_skip, dzb, dcolg, drowg, dnw, dsk) = ml_core_bwd(
        q, k, v, colg, rowg, cpre0, zb, sp["ml_norm"], sp["ml_skip"], ml_cs, ml_ns, ml_ms, dyb, nb, nh, "ml_core_bwd")
    gs["ml_norm"] = dnw.reshape(1, mlw)
    gs["ml_skip"] = dsk.reshape(1, mlw)
    dig, dbcum = _layouts_to_pairs(dcolg, drowg, nh, 1)
    dgout = _pad_lanes(jnp.concatenate([dig, dbcum], axis=1))

    def ml_gates_bwd_tile(i, g_, dgo, b_):
        _, vjp = jax.vjp(lambda a, b: _ml_gates_tile(a, b, nh), g_, b_)
        return vjp(dgo)

    dgl, dbg = rowwise("ml_gates_bwd", ml_gates_bwd_tile, [gl, dgout], [bgate], [(LANES, F32)], [(1, LANES)], tr=CHUNK)
    gs["ml_b_gate"] = dbg[:, :2 * nh]
    dcpre0, dxb_v, dwq, dwk, dwv, dgq, dgk, dgv = ml_proj_bwd(cpre0, xb, wq_e, wk_e, wv_e, gq, gk, gv, dq, dk, dv, dgl,
                                                            dcp_skip, "ml_proj_bwd")
    gs["ml_wq"], gs["ml_wk"], gs["ml_wv"] = _hw_contract(dwq), _hw_contract(dwk), _hw_contract(dwv)
    gs["ml_w_gate"] = _wg_contract([dgq, dgk, dgv], 2 * nh)
    dxb, dcw0, dcb0 = conv_bwd(dcpre0, xb, sp["ml_conv_w"], nb, "ml_conv_bwd", resid=dxb_v)
    gs["ml_conv_w"] = dcw0
    gs["ml_conv_b"] = dcb0

    def glu_bwd_tile(i, ys, tt, pab, dy_, gbias):
        _, vjp = jax.vjp(glu_tile, ys, tt, pab[:, s5w:], gbias)
        return vjp(dy_)

    dys_direct, dtglu, dza, dglub = rowwise("glu_bwd", glu_bwd_tile, [ys5, tglu, pa, dya], [sp["s5_glu_b"]],
                                            [(s5w, F32), (s5w, BF16), (s5w, BF16)], [(1, s5w)], tr=_tile(rows, 256, 16))
    gs["s5_glu_b"] = dglub
    dgb = mm(dtglu, bw["glu"], "nt", "mm_dgb")
    gbig["glu"] = mm(gb, dtglu, "tn", "mm_dglu", out_dtype=BF16)

    def gelu_bwd_tile(i, ys, dg_, direct):
        _, vjp = jax.vjp(_gelu, ys)
        return vjp(dg_)[0] + direct

    dys5 = rowwise("gelu_bwd", gelu_bwd_tile, [ys5, dgb, dys_direct], [], [(s5w, F32)], tr=_tile(rows, 256, 16))[0]
    if late_reduce is not None:
        late_reduce()
    du, dbre, dbim, dcre, dcim, dare, daie, dd5 = s5_bwd(pa, dys5, s5st, bre, bim, cre, cim, are, aie, sp["s5_d"], nb, "s5_bwd")
    gs["s5_d"] = dd5
    dbbr, dbbi, dcr, dci, dar, dai = _s5_contract(dbre, dbim, dcre, dcim, dare, daie, sg, spn, shh)
    gs["s5_c_re"], gs["s5_c_im"] = dcr, dci
    (gs["s5_lambda_re"], gs["s5_lambda_im"], dlogdt, gs["s5_b_re"], gs["s5_b_im"]) = s5_disc_vjp((dar, dai, dbbr, dbbi))
    gs["s5_log_dt"] = dlogdt.reshape(1, -1)
    dpa = jnp.concatenate([du, dza], axis=1)
    gbig["W0a"] = mm(n0, dpa, "tn", "mm_dW0a", out_dtype=BF16)
    gbig["W0xb"] = mm(n0, dxb, "tn", "mm_dW0xb", out_dtype=BF16)
    gbig["W0zb"] = mm(n0, dzb, "tn", "mm_dW0zb", out_dtype=BF16)
    early_reduce = early_grads({n: gbig[n] for n in _EARLY}) if early_grads is not None else None
    dn0 = mm(dpa, bw["W0a"], "nt", "mm_dn0a")
    dn0 = mm(dxb, bw["W0xb"], "nt", "mm_dn0xb", resid=dn0)
    dn0 = mm(dzb, bw["W0zb"], "nt", "mm_dn0zb", resid=dn0)
    if early_reduce is not None:
        early_reduce()
    dh0, _, dg0 = norm_bwd(h0, sp["ab_norm"], dn0, dh1, "norm0_bwd")
    gs["ab_norm"] = dg0
    dh0 = dh0.reshape(nb, tp, d)
    gs["meta_tokens"] = jnp.sum(dh0[:, :N_META], axis=0)
    return loss_acc[0, 0], dh0, gbig, gs


N_DEV = 8
N_CHIP = 4
N_PEER_CHIPS = N_CHIP - 1
MESH = pl.DeviceIdType.MESH
_HBM = pl.BlockSpec(memory_space=pltpu.HBM)


def _place():
    x, y, c = lax.axis_index("x"), lax.axis_index("y"), lax.axis_index("c")
    return x, y, c, [(1 - x, y), (x, 1 - y), (1 - x, 1 - y)]


def all_gather8(v, name):
    m_per, n = v.shape

    def body(x_ref, out_ref, send_sems, recv_sems, local_sem):
        x, y, c, chips = _place()
        me, sibling = (x, y, c), (x, y, 1 - c)

        def rows(px, py, pc):
            return out_ref.at[pl.ds((4 * px + 2 * py + pc) * m_per, m_per), :]

        def copy(kk, block, to, src=None):
            return pltpu.make_async_remote_copy(
                src_ref=rows(*block) if src is None else src, dst_ref=rows(*block), send_sem=send_sems.at[kk],
                recv_sem=recv_sems.at[kk], device_id=to, device_id_type=MESH)

        mine = pltpu.make_async_copy(x_ref, rows(*me), local_sem)
        mine.start()
        first = [copy(0, me, sibling, src=x_ref)]
        first += [copy(1 + j, me, (*chip, c), src=x_ref) for j, chip in enumerate(chips)]
        for cp in first:
            cp.start()
        passed = [copy(4 + j, (*chip, c), sibling) for j, chip in enumerate(chips)]
        for j, chip in enumerate(chips):
            copy(1 + j, (*chip, c), me).wait_recv()
            passed[j].start()
        copy(0, sibling, me).wait_recv()
        for j, chip in enumerate(chips):
            copy(4 + j, (*chip, 1 - c), me).wait_recv()
        for cp in first + passed:
            cp.wait_send()
        mine.wait()

    return pl.pallas_call(
        body, name=name, out_shape=jax.ShapeDtypeStruct((N_DEV * m_per, n), v.dtype),
        in_specs=[pl.BlockSpec(memory_space=pltpu.VMEM)], out_specs=pl.BlockSpec(memory_space=pltpu.VMEM),
        scratch_shapes=[pltpu.SemaphoreType.DMA((7,)), pltpu.SemaphoreType.DMA((7,)), pltpu.SemaphoreType.DMA],
        compiler_params=pltpu.CompilerParams(vmem_limit_bytes=VMEM_LIMIT))(v)


def gather_chips(vs, name):
    na = len(vs)

    def body(*refs):
        x_refs, out_refs = refs[:na], refs[na:2 * na]
        send_sems, recv_sems, local_sems = refs[2 * na:]
        x, y, c, chips = _place()
        k = 2 * x + y
        sibling = (x, y, 1 - c)

        def copy(i, kk, src, chip_k, half, to):
            return pltpu.make_async_remote_copy(
                src_ref=src, dst_ref=out_refs[i].at[chip_k, half], send_sem=send_sems.at[6 * i + kk],
                recv_sem=recv_sems.at[6 * i + kk], device_id=to, device_id_type=MESH)

        mine = [pltpu.make_async_copy(x_refs[i], out_refs[i].at[k], local_sems.at[i]) for i in range(na)]
        for cp in mine:
            cp.start()
        first = [copy(i, j, x_refs[i].at[c], k, c, (*chip, c)) for j, chip in enumerate(chips) for i in range(na)]
        for cp in first:
            cp.start()
        passed = []
        for j, (cx, cy) in enumerate(chips):
            kj = 2 * cx + cy
            for i in range(na):
                copy(i, j, out_refs[i].at[kj, c], kj, c, (cx, cy, c)).wait_recv()
                fwd = copy(i, 3 + j, out_refs[i].at[kj, c], kj, c, sibling)
                fwd.start()
                passed.append(fwd)
        for j, (cx, cy) in enumerate(chips):
            kj = 2 * cx + cy
            for i in range(na):
                copy(i, 3 + j, out_refs[i].at[kj, 1 - c], kj, 1 - c, sibling).wait_recv()
        for cp in first + passed:
            cp.wait_send()
        for cp in mine:
            cp.wait()

    return pl.pallas_call(
        body, name=name, out_shape=[jax.ShapeDtypeStruct((N_CHIP,) + v.shape, v.dtype) for v in vs],
        in_specs=[_HBM] * na, out_specs=[_HBM] * na,
        scratch_shapes=[pltpu.SemaphoreType.DMA((6 * na,)), pltpu.SemaphoreType.DMA((6 * na,)),
                        pltpu.SemaphoreType.DMA((na,))])(*vs)


def scatter_chips(ps, name):
    na = len(ps)

    def body(*refs):
        p_refs, out_refs = refs[:na], refs[na:2 * na]
        send_sems, recv_sems, local_sems = refs[2 * na:]
        x, y, c, chips = _place()
        k = 2 * x + y
        sibling = (x, y, 1 - c)

        def copy(i, kk, src, chip_k, half, to):
            return pltpu.make_async_remote_copy(
                src_ref=src, dst_ref=out_refs[i].at[chip_k, half], send_sem=send_sems.at[7 * i + kk],
                recv_sem=recv_sems.at[7 * i + kk], device_id=to, device_id_type=MESH)

        mine = [pltpu.make_async_copy(p_refs[i].at[k], out_refs[i].at[k, c], local_sems.at[i]) for i in range(na)]
        for cp in mine:
            cp.start()
        first = [copy(i, 1 + j, p_refs[i].at[2 * cx + cy], k, c, (cx, cy, c))
                 for j, (cx, cy) in enumerate(chips) for i in range(na)]
        first += [copy(i, 0, p_refs[i].at[k], k, c, sibling) for i in range(na)]
        for cp in first:
            cp.start()
        passed = []
        for j, (cx, cy) in enumerate(chips):
            kj = 2 * cx + cy
            for i in range(na):
                copy(i, 1 + j, out_refs[i].at[kj, c], kj, c, (cx, cy, c)).wait_recv()
                fwd = copy(i, 4 + j, out_refs[i].at[kj, c], kj, c, sibling)
                fwd.start()
                passed.append(fwd)
        for i in range(na):
            copy(i, 0, out_refs[i].at[k, 1 - c], k, 1 - c, sibling).wait_recv()
        for j, (cx, cy) in enumerate(chips):
            kj = 2 * cx + cy
            for i in range(na):
                copy(i, 4 + j, out_refs[i].at[kj, 1 - c], kj, 1 - c, sibling).wait_recv()
        for cp in first + passed:
            cp.wait_send()
        for cp in mine:
            cp.wait()

    return pl.pallas_call(
        body, name=name, out_shape=[jax.ShapeDtypeStruct((N_CHIP, 2) + p.shape[1:], p.dtype) for p in ps],
        in_specs=[_HBM] * na, out_specs=[_HBM] * na,
        scratch_shapes=[pltpu.SemaphoreType.DMA((7 * na,)), pltpu.SemaphoreType.DMA((7 * na,)),
                        pltpu.SemaphoreType.DMA((na,))])(*ps)


def swap_halves(gs_, name):
    na = len(gs_)

    def body(*refs):
        g_refs, out_refs = refs[:na], refs[na:2 * na]
        send_sems, recv_sems = refs[2 * na:]
        x, y, c, _ = _place()
        cps = [pltpu.make_async_remote_copy(
            src_ref=g_refs[i].at[kk, 1 - c], dst_ref=out_refs[i].at[kk], send_sem=send_sems.at[N_CHIP * i + kk],
            recv_sem=recv_sems.at[N_CHIP * i + kk], device_id=(x, y, 1 - c), device_id_type=MESH)
            for i in range(na) for kk in range(N_CHIP)]
        for cp in cps:
            cp.start()
        for cp in cps:
            cp.wait()

    return pl.pallas_call(
        body, name=name, out_shape=[jax.ShapeDtypeStruct((N_CHIP,) + g.shape[2:], g.dtype) for g in gs_],
        in_specs=[_HBM] * na, out_specs=[_HBM] * na,
        scratch_shapes=[pltpu.SemaphoreType.DMA((N_CHIP * na,)), pltpu.SemaphoreType.DMA((N_CHIP * na,))])(*gs_)


def sequencer_swap(gs_, collective_id, name):
    na = len(gs_)
    hbm = pltpu.MemorySpace.HBM
    g_refs = [jax.new_ref(g, memory_space=hbm) for g in gs_]
    out_refs = [jax.empty_ref(jax.ShapeDtypeStruct((N_CHIP,) + g.shape[2:], g.dtype), memory_space=hbm) for g in gs_]

    @pl.kernel(mesh=plsc.ScalarSubcoreMesh(axis_name="seq", num_cores=1), name=name,
               scratch_types=(pltpu.SemaphoreType.DMA((N_CHIP * na,)), pltpu.SemaphoreType.DMA((N_CHIP * na,))),
               compiler_params=pltpu.CompilerParams(collective_id=collective_id))
    def launch(send_sems, recv_sems):
        x, y, c, _ = _place()
        sibling = (x, y, 1 - c)
        barrier = pltpu.get_barrier_semaphore()
        pl.semaphore_signal(barrier, inc=1, device_id=sibling, device_id_type=MESH)
        pl.semaphore_wait(barrier, 1)
        cps = [pltpu.make_async_remote_copy(
            src_ref=g_refs[i].at[kk, 1 - c], dst_ref=out_refs[i].at[kk], send_sem=send_sems.at[N_CHIP * i + kk],
            recv_sem=recv_sems.at[N_CHIP * i + kk], device_id=sibling, device_id_type=MESH)
            for i in range(na) for kk in range(N_CHIP)]
        for cp in cps:
            cp.start()
        for cp in cps:
            cp.wait()

    launch()
    return [r[...] for r in out_refs]


def add_halves(g, other, core, name):
    _, _, m, n = g.shape
    tr = _tile(m, 256, 16)

    def body(core_ref, g_ref, o_ref, out_ref):
        out_ref[...] = (g_ref[...].astype(F32) + o_ref[...].astype(F32)).astype(out_ref.dtype)

    grid_spec = pltpu.PrefetchScalarGridSpec(
        num_scalar_prefetch=1, grid=(N_CHIP, m // tr),
        in_specs=[pl.BlockSpec((None, None, tr, n), lambda kk, i, core_ref: (kk, core_ref[0], i, 0)),
                  pl.BlockSpec((None, tr, n), lambda kk, i, core_ref: (kk, i, 0))],
        out_specs=pl.BlockSpec((None, tr, n), lambda kk, i, core_ref: (kk, i, 0)))
    return pl.pallas_call(body, name=name, grid_spec=grid_spec, out_shape=jax.ShapeDtypeStruct((N_CHIP, m, n), g.dtype),
                          compiler_params=_params(("arbitrary", "arbitrary")))(core.reshape(1).astype(jnp.int32), g, other)


def sequencer_exchange(srcs, scatter, collective_id, name):
    na = len(srcs)
    per = 2 * N_PEER_CHIPS + (1 if scatter else 0)
    hbm = pltpu.MemorySpace.HBM
    src_refs = [jax.new_ref(a, memory_space=hbm) for a in srcs]
    out_refs = [jax.empty_ref(jax.ShapeDtypeStruct((N_CHIP, 2) + a.shape[1:], a.dtype), memory_space=hbm) for a in srcs]

    @pl.kernel(mesh=plsc.ScalarSubcoreMesh(axis_name="seq", num_cores=1), name=name,
               scratch_types=(pltpu.SemaphoreType.DMA((per * na,)), pltpu.SemaphoreType.DMA((per * na,)),
                              pltpu.SemaphoreType.DMA((na,))),
               compiler_params=pltpu.CompilerParams(collective_id=collective_id))
    def launch(send_sems, recv_sems, local_sems):
        x, y, c, chips = _place()
        k = 2 * x + y
        sibling = (x, y, 1 - c)
        barrier = pltpu.get_barrier_semaphore()
        for cx, cy in chips:
            pl.semaphore_signal(barrier, inc=1, device_id=(cx, cy, c), device_id_type=MESH)
        pl.semaphore_signal(barrier, inc=1, device_id=sibling, device_id_type=MESH)
        pl.semaphore_wait(barrier, N_CHIP)

        def copy(i, kk, src, chip_k, half, to):
            return pltpu.make_async_remote_copy(
                src_ref=src, dst_ref=out_refs[i].at[chip_k, half], send_sem=send_sems.at[per * i + kk],
                recv_sem=recv_sems.at[per * i + kk], device_id=to, device_id_type=MESH)

        if scatter:
            mine = [pltpu.make_async_copy(src_refs[i].at[k], out_refs[i].at[k, c], local_sems.at[i]) for i in range(na)]
        else:
            mine = [pltpu.make_async_copy(src_refs[i], out_refs[i].at[k], local_sems.at[i]) for i in range(na)]
        for cp in mine:
            cp.start()
        first = []
        for j, (cx, cy) in enumerate(chips):
            for i in range(na):
                src = src_refs[i].at[2 * cx + cy] if scatter else src_refs[i].at[c]
                first.append(copy(i, j, src, k, c, (cx, cy, c)))
        if scatter:
            first += [copy(i, 2 * N_PEER_CHIPS, src_refs[i].at[k], k, c, sibling) for i in range(na)]
        for cp in first:
            cp.start()
        passed = []
        for j, (cx, cy) in enumerate(chips):
            kj = 2 * cx + cy
            for i in range(na):
                copy(i, j, out_refs[i].at[kj, c], kj, c, (cx, cy, c)).wait_recv()
                fwd = copy(i, N_PEER_CHIPS + j, out_refs[i].at[kj, c], kj, c, sibling)
                fwd.start()
                passed.append(fwd)
        if scatter:
            for i in range(na):
                copy(i, 2 * N_PEER_CHIPS, out_refs[i].at[k, 1 - c], k, 1 - c, sibling).wait_recv()
        for j, (cx, cy) in enumerate(chips):
            kj = 2 * cx + cy
            for i in range(na):
                copy(i, N_PEER_CHIPS + j, out_refs[i].at[kj, 1 - c], kj, 1 - c, sibling).wait_recv()
        for cp in first + passed:
            cp.wait_send()
        for cp in mine:
            cp.wait()

    launch()
    return [r[...] for r in out_refs]


PACK_LANES = 512


def _pack(arrs, dtype, lanes, row_align):
    flat = jnp.concatenate([a.reshape(-1).astype(dtype) for a in arrs])
    unit = lanes * row_align
    total = -(-flat.shape[0] // unit) * unit
    return jnp.pad(flat, (0, total - flat.shape[0])).reshape(total // lanes, lanes)


def _unpack(flat, shapes):
    flat = flat.reshape(-1)
    out, off = [], 0
    for s in shapes:
        n = math.prod(s)
        out.append(flat[off:off + n].reshape(s))
        off += n
    return out


def _adam_tile(w, m, v, g):
    m2 = ADAM_B1 * m + (1.0 - ADAM_B1) * g
    v2 = ADAM_B2 * v + (1.0 - ADAM_B2) * (g * g)
    m_hat = m2 / (1.0 - ADAM_B1 ** ADAM_STEP)
    v_hat = v2 / (1.0 - ADAM_B2 ** ADAM_STEP)
    delta = -ADAM_LR * (m_hat / (jnp.sqrt(v_hat) + ADAM_EPS) + ADAM_WD * w)
    return delta, m2, v2


def adam_big(w, m, v, pieces, name):
    _, r, c = w.shape
    tr = _tile(r, 128, 16)

    def body(w_ref, m_ref, v_ref, p0, p1, p2, p3, g_ref, d_ref, mo_ref, vo_ref):
        g = ((p0[...].astype(F32) + p1[...].astype(F32)) + p2[...].astype(F32)) + p3[...].astype(F32)
        delta, m2, v2 = _adam_tile(w_ref[...], m_ref[...], v_ref[...], g)
        g_ref[...] = g
        d_ref[...] = delta
        mo_ref[...] = m2
        vo_ref[...] = v2

    wspec = pl.BlockSpec((None, tr, c), lambda i: (0, i, 0))
    pspecs = [pl.BlockSpec((None, tr, c), functools.partial(lambda i, kk: (kk, i, 0), kk=kk)) for kk in range(N_CHIP)]
    return pl.pallas_call(
        body, name=name, grid=(r // tr,), in_specs=[wspec] * 3 + pspecs, out_specs=[wspec] * 4,
        out_shape=[jax.ShapeDtypeStruct(w.shape, F32)] * 4, compiler_params=_params(("parallel",)))(
            w, m, v, pieces, pieces, pieces, pieces)


_WEIGHTS = (
    ("meta_tokens", "small", 1), ("ab_norm", "small", None), ("ab_w_in", "big", 2), ("s5_lambda_re", "small", None),
    ("s5_lambda_im", "small", None), ("s5_log_dt", "small", None), ("s5_b_re", "small", None), ("s5_b_im", "small", None),
    ("s5_c_re", "small", None), ("s5_c_im", "small", None), ("s5_d", "small", None), ("s5_glu_w", "big", 1),
    ("s5_glu_b", "small", None), ("ml_conv_w", "small", 2), ("ml_conv_b", "small", None), ("ml_wq", "small", 1),
    ("ml_wk", "small", 1), ("ml_wv", "small", 1), ("ml_w_gate", "small", 1), ("ml_b_gate", "small", None),
    ("ml_norm", "small", None), ("ml_skip", "small", None), ("ab_w_out", "big", 1), ("ssd_norm", "small", 1),
    ("ssd_w_in", "big", 2), ("ssd_conv_w", "small", 2), ("ssd_conv_b", "small", 1), ("ssd_dt_bias", "small", None),
    ("ssd_a_log", "small", None), ("ssd_d", "small", None), ("ssd_gnorm", "small", 1), ("ssd_w_out", "big", 1),
    ("final_norm", "small", None),
)


def _squeeze(a):
    return a[0] if a.ndim >= 3 else a


def kernel(x, meta_tokens, ab_norm, ab_w_in, s5_lambda_re, s5_lambda_im, s5_log_dt, s5_b_re, s5_b_im, s5_c_re, s5_c_im, s5_d, s5_glu_w, s5_glu_b, ml_conv_w, ml_conv_b, ml_wq, ml_wk, ml_wv, ml_w_gate, ml_b_gate, ml_norm, ml_skip, ab_w_out, ssd_norm, ssd_w_in, ssd_conv_w, ssd_conv_b, ssd_dt_bias, ssd_a_log, ssd_d, ssd_gnorm, ssd_w_out, final_norm, loss_target, m_meta_tokens, m_ab_norm, m_ab_w_in, m_s5_lambda_re, m_s5_lambda_im, m_s5_log_dt, m_s5_b_re, m_s5_b_im, m_s5_c_re, m_s5_c_im, m_s5_d, m_s5_glu_w, m_s5_glu_b, m_ml_conv_w, m_ml_conv_b, m_ml_wq, m_ml_wk, m_ml_wv, m_ml_w_gate, m_ml_b_gate, m_ml_norm, m_ml_skip, m_ab_w_out, m_ssd_norm, m_ssd_w_in, m_ssd_conv_w, m_ssd_conv_b, m_ssd_dt_bias, m_ssd_a_log, m_ssd_d, m_ssd_gnorm, m_ssd_w_out, m_final_norm, v_meta_tokens, v_ab_norm, v_ab_w_in, v_s5_lambda_re, v_s5_lambda_im, v_s5_log_dt, v_s5_b_re, v_s5_b_im, v_s5_c_re, v_s5_c_im, v_s5_d, v_s5_glu_w, v_s5_glu_b, v_ml_conv_w, v_ml_conv_b, v_ml_wq, v_ml_wk, v_ml_wv, v_ml_w_gate, v_ml_b_gate, v_ml_norm, v_ml_skip, v_ab_w_out, v_ssd_norm, v_ssd_w_in, v_ssd_conv_w, v_ssd_conv_b, v_ssd_dt_bias, v_ssd_a_log, v_ssd_d, v_ssd_gnorm, v_ssd_w_out, v_final_norm):
    args = (meta_tokens, ab_norm, ab_w_in, s5_lambda_re, s5_lambda_im, s5_log_dt, s5_b_re, s5_b_im, s5_c_re, s5_c_im, s5_d, s5_glu_w, s5_glu_b, ml_conv_w, ml_conv_b, ml_wq, ml_wk, ml_wv, ml_w_gate, ml_b_gate, ml_norm, ml_skip, ab_w_out, ssd_norm, ssd_w_in, ssd_conv_w, ssd_conv_b, ssd_dt_bias, ssd_a_log, ssd_d, ssd_gnorm, ssd_w_out, final_norm)
    m_args = (m_meta_tokens, m_ab_norm, m_ab_w_in, m_s5_lambda_re, m_s5_lambda_im, m_s5_log_dt, m_s5_b_re, m_s5_b_im, m_s5_c_re, m_s5_c_im, m_s5_d, m_s5_glu_w, m_s5_glu_b, m_ml_conv_w, m_ml_conv_b, m_ml_wq, m_ml_wk, m_ml_wv, m_ml_w_gate, m_ml_b_gate, m_ml_norm, m_ml_skip, m_ab_w_out, m_ssd_norm, m_ssd_w_in, m_ssd_conv_w, m_ssd_conv_b, m_ssd_dt_bias, m_ssd_a_log, m_ssd_d, m_ssd_gnorm, m_ssd_w_out, m_final_norm)
    v_args = (v_meta_tokens, v_ab_norm, v_ab_w_in, v_s5_lambda_re, v_s5_lambda_im, v_s5_log_dt, v_s5_b_re, v_s5_b_im, v_s5_c_re, v_s5_c_im, v_s5_d, v_s5_glu_w, v_s5_glu_b, v_ml_conv_w, v_ml_conv_b, v_ml_wq, v_ml_wk, v_ml_wv, v_ml_w_gate, v_ml_b_gate, v_ml_norm, v_ml_skip, v_ab_w_out, v_ssd_norm, v_ssd_w_in, v_ssd_conv_w, v_ssd_conv_b, v_ssd_dt_bias, v_ssd_a_log, v_ssd_d, v_ssd_gnorm, v_ssd_w_out, v_final_norm)
    names = [w[0] for w in _WEIGHTS]
    kind = {w[0]: w[1] for w in _WEIGHTS}
    axis = {w[0]: w[2] for w in _WEIGHTS}
    w_loc = dict(zip(names, args))
    m_loc = dict(zip(names, m_args))
    v_loc = dict(zip(names, v_args))
    chip = 2 * lax.axis_index("x") + lax.axis_index("y")
    core = lax.axis_index("c")
    big = [n for n in names if kind[n] == "big"]
    small = [n for n in names if kind[n] == "small"]
    small_sh = [n for n in small if axis[n] is not None]

    def halves(a):
        return a.astype(BF16).reshape(2, a.shape[1] // 2, a.shape[2])

    def assemble(n, gth):
        shard = gth.reshape((N_CHIP,) + w_loc[n].shape[1:])
        if axis[n] == 1:
            return shard.reshape(-1, shard.shape[2])
        return jnp.concatenate([shard[kk] for kk in range(N_CHIP)], axis=1)

    early = ["ab_w_in", "s5_glu_w"]
    late = ["ab_w_out", "ssd_w_in", "ssd_w_out"]
    gathered = gather_chips([halves(w_loc[n]) for n in early], "gather_early_w")
    after_early = (gathered[0][0, 0, 0, 0] * 0).astype(BF16)
    late_gathered = sequencer_exchange([halves(w_loc[n]) + after_early for n in late], False, 1, "gather_late_w")
    w_in0_shards = gathered[0].reshape((N_CHIP,) + w_loc["ab_w_in"].shape[1:])
    glu_full = assemble("s5_glu_w", gathered[1])

    def columns(shards, lo, hi):
        cw = shards.shape[2]
        parts = [shards[kk][:, max(lo - kk * cw, 0):min(hi - kk * cw, cw)]
                 for kk in range(N_CHIP) if lo < (kk + 1) * cw and hi > kk * cw]
        return parts[0] if len(parts) == 1 else jnp.concatenate(parts, axis=1)

    small_sh_shapes = [w_loc[n].shape for n in small_sh]
    packed_s = _pack([w_loc[n] for n in small_sh], F32, LANES, SUBLANES)
    g8 = all_gather8(packed_s, "gather_small_w").reshape(N_CHIP, 2, -1)
    sp = {}
    for n in small:
        if axis[n] is None:
            sp[n] = _squeeze(w_loc[n])
    per_chip = [_unpack(g8[kk, 0], small_sh_shapes) for kk in range(N_CHIP)]
    for i, n in enumerate(small_sh):
        sp[n] = _squeeze(jnp.concatenate([per_chip[kk][i] for kk in range(N_CHIP)], axis=axis[n]))

    s5w = glu_full.shape[0]
    mlw = w_loc["ab_w_out"].shape[1] * N_CHIP - s5w
    inner = w_loc["ssd_w_out"].shape[1] * N_CHIP
    n_heads1 = sp["ssd_d"].shape[1]
    cdim = w_loc["ssd_w_in"].shape[2] * N_CHIP - inner - n_heads1
    bw = dict(W0a=columns(w_in0_shards, 0, 2 * s5w), W0xb=columns(w_in0_shards, 2 * s5w, 2 * s5w + mlw),
              W0zb=columns(w_in0_shards, 2 * s5w + mlw, 2 * (s5w + mlw)), glu=glu_full)

    def late_weights():
        fb = dict(zip(late, late_gathered))
        w_out0 = assemble("ab_w_out", fb["ab_w_out"])
        w1 = fb["ssd_w_in"].reshape((N_CHIP,) + w_loc["ssd_w_in"].shape[1:])
        return dict(Wo0a=w_out0[:s5w], Wo0b=w_out0[s5w:], W1z=columns(w1, 0, inner),
                    W1x=columns(w1, inner, inner + cdim), W1dt=_pad_lanes(columns(w1, inner + cdim, inner + cdim + n_heads1)),
                    Wo1=assemble("ssd_w_out", fb["ssd_w_out"]))

    def piece_columns(parts, lo, hi):
        out, off = [], 0
        for p in parts:
            a, b = max(lo - off, 0), min(hi - off, p.shape[1])
            if a < b:
                out.append(p[:, a:b])
            off += p.shape[1]
        return out[0] if len(out) == 1 else jnp.concatenate(out, axis=1)

    def chip_halves(n, parts):
        _, r, c_ = w_loc[n].shape
        if axis[n] == 1:
            whole = parts[0] if len(parts) == 1 else jnp.concatenate(parts, axis=0)
            return whole.reshape(N_CHIP, 2, r // 2, c_)
        shards = [piece_columns(parts, kk * c_, (kk + 1) * c_) for kk in range(N_CHIP)]
        return jnp.stack(shards).reshape(N_CHIP, 2, r // 2, c_)

    pieces = {}

    def reduce_group(ns, gfull, tag, ids):
        gps = [chip_halves(n, gfull[n]) for n in ns]
        from_sibling = sequencer_swap(gps, ids[0], "swap_" + tag)

        def reduce():
            partials = [add_halves(gp, oth, core, "add_" + n) for n, gp, oth in zip(ns, gps, from_sibling)]
            pieces.update(zip(ns, sequencer_exchange(partials, True, ids[1], "scatter_" + tag)))

        return reduce

    def late_grads(g):
        gfull = {"ab_w_out": [g["Wo0a"], g["Wo0b"]], "ssd_w_in": [g["W1z"], g["W1x"], g["W1dt"][:, :n_heads1]],
                 "ssd_w_out": [g["Wo1"]]}
        return reduce_group(late, gfull, "late_g", (5, 2))

    def early_grads(g):
        gfull = {"ab_w_in": [g["W0a"], g["W0xb"], g["W0zb"]], "s5_glu_w": [g["glu"]]}
        return reduce_group(early, gfull, "early_g", (6, 3))

    loss_local, dh0, gbig, gs = _local_step(x, loss_target, bw, sp, late_weights, late_grads, early_grads)
    grad_x = dh0[:, N_META:N_META + x.shape[1]]

    out_g, out_d, out_m, out_v = {}, {}, {}, {}
    small_full_shapes = [sp[n].shape for n in small] + [(1, 1)]
    packed_gs = _pack([gs[n] for n in small] + [loss_local.reshape(1, 1)], F32, LANES, SUBLANES)
    rows_s = packed_gs.shape[0]
    all_gs = sequencer_exchange([jnp.broadcast_to(packed_gs[None], (N_CHIP,) + packed_gs.shape)], True, 4,
                                "gather_small_g")[0].reshape(N_DEV, rows_s, LANES)
    blocks = [all_gs[i] for i in range(N_DEV)]

    for n in late + early:
        pcs = pieces[n].reshape((N_CHIP,) + w_loc[n].shape[1:])
        out_g[n], out_d[n], out_m[n], out_v[n] = adam_big(w_loc[n], m_loc[n], v_loc[n], pcs, "adam_" + n)

    def sum8(i, *b):
        acc = b[0]
        for t in b[1:]:
            acc = acc + t
        return acc

    gsum = rowwise("sum_small_g", sum8, blocks, [], [(LANES, F32)], tr=_tile(rows_s, 512, 8))[0]
    summed = _unpack(gsum, small_full_shapes)
    loss = summed[-1].reshape(())
    g_small = dict(zip(small, summed[:-1]))
    g_loc = {}
    for n in small:
        g = g_small[n].reshape((1,) + g_small[n].shape) if w_loc[n].ndim >= 3 else g_small[n]
        if axis[n] is not None:
            size = w_loc[n].shape[axis[n]]
            g = lax.dynamic_slice_in_dim(g, chip * size, size, axis=axis[n])
        g_loc[n] = g.reshape(w_loc[n].shape)
    loc_shapes = [w_loc[n].shape for n in small]
    pw, pm, pv, pg = (_pack([d[n] for n in small], F32, LANES, SUBLANES) for d in (w_loc, m_loc, v_loc, g_loc))
    dl, mn, vn = rowwise("adam_small", lambda i, a, b, c_, d_: _adam_tile(a, b, c_, d_), [pw, pm, pv, pg], [],
                         [(LANES, F32)] * 3, tr=_tile(pw.shape[0], 512, 8))
    for d_out, flat in ((out_d, dl), (out_m, mn), (out_v, vn)):
        for n, a in zip(small, _unpack(flat, loc_shapes)):
            d_out[n] = a
    for n in small:
        out_g[n] = g_loc[n]

    return (loss, grad_x, *[out_g[n] for n in names], *[out_d[n] for n in names], *[out_m[n] for n in names],
            *[out_v[n] for n in names])
```

```python
import functools
import math

import jax
import jax.numpy as jnp
from jax import lax
from jax.experimental import pallas as pl
from jax.experimental.pallas import tpu as pltpu
from jax.experimental.pallas import tpu_sc as plsc

F32 = jnp.float32
BF16 = jnp.bfloat16
HI = lax.Precision.HIGHEST

D_MODEL = 2048
SEQ = 2048
N_META = 16
CHUNK = 128
NORM_EPS = 1e-6
HEAD_NORM_EPS = 1e-5
S5_GROUP_SIZE = 16
S5_STATE = 64
MLSTM_HEADS = 8
QKV_BLOCK = 4
SSD_HEAD_DIM = 64
SSD_STATE = 128
SSD_HPG = 8
ADAM_LR = 0.001
ADAM_B1 = 0.9
ADAM_B2 = 0.999
ADAM_EPS = 1e-08
ADAM_WD = 0.01
ADAM_STEP = 10

LANES = 128
SUBLANES = 8
VMEM_LIMIT = 56 * 1024 * 1024
MM_OPERAND_VMEM = 34 * 1024 * 1024


def _sigmoid(x):
    return 0.5 * jnp.tanh(0.5 * x) + 0.5


@jax.custom_vjp
def _silu(x):
    return x * _sigmoid(x)


def _silu_fwd(x):
    return x * _sigmoid(x), x


def _silu_bwd(x, ct):
    s = _sigmoid(x)
    return (ct * (s * (1.0 + x * (1.0 - s))),)


_silu.defvjp(_silu_fwd, _silu_bwd)


def _softplus(x):
    return jnp.maximum(x, 0.0) + jnp.log(1.0 + jnp.exp(-jnp.abs(x)))


def _log_sigmoid(x):
    return jnp.minimum(x, 0.0) - jnp.log(1.0 + jnp.exp(-jnp.abs(x)))


def _gelu(x):
    return 0.5 * x * (1.0 + jnp.tanh(math.sqrt(2.0 / math.pi) * (x + 0.044715 * (x * x * x))))


def _dot(a, b, dims, precision=None):
    return lax.dot_general(a, b, (dims, ((), ())), preferred_element_type=F32, precision=precision)


_NN, _NT, _TN = ((1,), (0,)), ((1,), (1,)), ((0,), (0,))


def _bf16_dot(dims, da_rule, db_rule):
    @jax.custom_vjp
    def f(a, b):
        return _dot(a.astype(BF16), b.astype(BF16), dims)

    def fwd(a, b):
        ab, bb = a.astype(BF16), b.astype(BF16)
        return _dot(ab, bb, dims), (ab, bb, jnp.zeros((), a.dtype), jnp.zeros((), b.dtype))

    def bwd(res, ct):
        ab, bb, a_like, b_like = res
        cb = ct.astype(BF16)
        return da_rule(ab, bb, cb).astype(a_like.dtype), db_rule(ab, bb, cb).astype(b_like.dtype)

    f.defvjp(fwd, bwd)
    return f


_dot_nn = _bf16_dot(_NN, lambda a, b, c: _dot(c, b, _NT), lambda a, b, c: _dot(a, c, _TN))
_dot_nt = _bf16_dot(_NT, lambda a, b, c: _dot(c, b, _NN), lambda a, b, c: _dot(c, a, _TN))
_dot_tn = _bf16_dot(_TN, lambda a, b, c: _dot(b, c, _NT), lambda a, b, c: _dot(a, c, _NN))


def _lane_pick(a, idx):
    sel = (lax.broadcasted_iota(jnp.int32, (1, a.shape[1]), 1) == idx).astype(a.dtype)
    return jnp.sum(a * sel, axis=1, keepdims=True)


def _row_pick(a, idx):
    sel = (lax.broadcasted_iota(jnp.int32, (a.shape[0], 1), 0) == idx).astype(a.dtype)
    return jnp.sum(a * sel, axis=0, keepdims=True)


def _tri(n, upper=False):
    r = lax.broadcasted_iota(jnp.int32, (n, n), 0)
    c = lax.broadcasted_iota(jnp.int32, (n, n), 1)
    return ((r <= c) if upper else (r >= c)).astype(F32)


def _tile(n, target, align):
    if n <= target:
        return n
    t = (target // align) * align
    while t >= align:
        if n % t == 0:
            return t
        t -= align
    return n


def _params(sem=None):
    return pltpu.CompilerParams(dimension_semantics=sem, vmem_limit_bytes=VMEM_LIMIT)


def mm(a, b, mode, name, resid=None, out_dtype=F32):
    if mode == "nn":
        (m, k), (k2, n) = a.shape, b.shape
    elif mode == "nt":
        (m, k), (n, k2) = a.shape, b.shape
    else:
        (k, m), (k2, n) = a.shape, b.shape
    assert k == k2, (a.shape, b.shape, mode)
    a_sz, b_sz = a.dtype.itemsize, b.dtype.itemsize
    if mode == "tn":
        tm, tn = _tile(m, 1024, LANES), _tile(n, 1024, LANES)
        tk = _tile(k, MM_OPERAND_VMEM // (2 * (tm * a_sz + tn * b_sz)), 16)
    else:
        tm, tn = _tile(m, 1088, 16), _tile(n, 512, LANES)
        tk = _tile(k, MM_OPERAND_VMEM // (2 * (tm * a_sz + tn * b_sz)), LANES)
    nk = k // tk
    dims = {"nn": ((1,), (0,)), "nt": ((1,), (1,)), "tn": ((0,), (0,))}[mode]
    has_resid = resid is not None

    def body(*refs):
        if has_resid:
            a_ref, b_ref, r_ref, o_ref = refs[:4]
        else:
            a_ref, b_ref, o_ref = refs[:3]
        part = _dot(a_ref[...].astype(BF16), b_ref[...].astype(BF16), dims)

        def finish(res):
            if has_resid:
                res = res + r_ref[...].astype(F32)
            o_ref[...] = res.astype(o_ref.dtype)

        if nk == 1:
            finish(part)
            return
        acc_ref = refs[-1]
        kk = pl.program_id(2)

        @pl.when(kk == 0)
        def _():
            acc_ref[...] = part

        @pl.when(jnp.logical_and(kk > 0, kk < nk - 1))
        def _():
            acc_ref[...] += part

        @pl.when(kk == nk - 1)
        def _():
            finish(acc_ref[...] + part)

    if mode == "tn":
        a_spec = pl.BlockSpec((tk, tm), lambda i, j, kk: (kk, i))
    else:
        a_spec = pl.BlockSpec((tm, tk), lambda i, j, kk: (i, kk))
    if mode == "nt":
        b_spec = pl.BlockSpec((tn, tk), lambda i, j, kk: (j, kk))
    else:
        b_spec = pl.BlockSpec((tk, tn), lambda i, j, kk: (kk, j))
    o_spec = pl.BlockSpec((tm, tn), lambda i, j, kk: (i, j))
    in_specs = [a_spec, b_spec] + ([o_spec] if has_resid else [])
    args = (a, b) + ((resid,) if has_resid else ())
    return pl.pallas_call(
        body, name=name, grid=(m // tm, n // tn, nk), in_specs=in_specs, out_specs=o_spec,
        out_shape=jax.ShapeDtypeStruct((m, n), out_dtype), scratch_shapes=[pltpu.VMEM((tm, tn), F32)] if nk > 1 else [],
        compiler_params=_params(("parallel", "parallel", "arbitrary")))(*args)


def rowwise(name, f, rows, params, outs, accs=(), tr=128):
    n_rows = rows[0].shape[0]
    assert n_rows % tr == 0
    n_r, n_p, n_o, n_a = len(rows), len(params), len(outs), len(accs)

    def body(*refs):
        i = pl.program_id(0)
        r_vals = [r[...] for r in refs[:n_r]]
        p_vals = [r[...] for r in refs[n_r:n_r + n_p]]
        o_refs = refs[n_r + n_p:n_r + n_p + n_o]
        a_refs = refs[n_r + n_p + n_o:]
        res = f(i, *r_vals, *p_vals)
        if not isinstance(res, (tuple, list)):
            res = (res,)
        assert len(res) == n_o + n_a, (name, len(res))
        for o_ref, val in zip(o_refs, res[:n_o]):
            o_ref[...] = val.astype(o_ref.dtype)
        if n_a:
            @pl.when(i == 0)
            def _():
                for a_ref in a_refs:
                    a_ref[...] = jnp.zeros_like(a_ref)

            for a_ref, val in zip(a_refs, res[n_o:]):
                a_ref[...] += val.astype(F32)

    in_specs = [pl.BlockSpec((tr, r.shape[1]), lambda i: (i, 0)) for r in rows]
    in_specs += [pl.BlockSpec(p.shape, lambda i: (0, 0)) for p in params]
    out_specs = [pl.BlockSpec((tr, w), lambda i: (i, 0)) for w, _ in outs]
    out_specs += [pl.BlockSpec(s, lambda i: (0, 0)) for s in accs]
    out_shape = [jax.ShapeDtypeStruct((n_rows, w), dt) for w, dt in outs]
    out_shape += [jax.ShapeDtypeStruct(s, F32) for s in accs]
    res = pl.pallas_call(
        body, name=name, grid=(n_rows // tr,), in_specs=in_specs, out_specs=out_specs, out_shape=out_shape,
        compiler_params=_params(("arbitrary",)))(*rows, *params)
    return res


def _rms(x, g, eps=NORM_EPS):
    return x * lax.rsqrt(jnp.mean(x * x, axis=-1, keepdims=True) + eps) * g


def norm_fwd(x, g, name):
    return rowwise(name, lambda i, xb, gb: _rms(xb, gb), [x], [g], [(x.shape[1], BF16)], tr=_tile(x.shape[0], 256, 16))[0]


def norm_bwd(x, g, dn, resid, name):
    def f(i, xb, dnb, rb, gb):
        _, vjp = jax.vjp(_rms, xb, gb)
        dx, dg = vjp(dnb)
        return dx + rb, dx + rb, dg

    return rowwise(name, f, [x, dn, resid], [g], [(x.shape[1], F32), (x.shape[1], BF16)], [g.shape],
                   tr=_tile(x.shape[0], 256, 16))


def conv_fwd(x, w, b, nb, name):
    rows, width = x.shape
    nc = rows // nb // CHUNK
    tw = _tile(width, 1024, LANES)
    ksz = w.shape[0]

    def body(x_ref, w_ref, b_ref, o_ref, ext_ref):
        c = pl.program_id(2)

        @pl.when(c == 0)
        def _():
            ext_ref[0:SUBLANES, :] = jnp.zeros((SUBLANES, tw), F32)

        taps = [w_ref[j:j + 1, :] for j in range(ksz)]
        bias = b_ref[...]
        row = lax.broadcasted_iota(jnp.int32, (SUBLANES, tw), 0)
        prev_rot = [pltpu.roll(ext_ref[0:SUBLANES, :], k, 0) for k in range(1, ksz)]
        for s in range(CHUNK // SUBLANES):
            r0 = s * SUBLANES
            cur = x_ref[r0:r0 + SUBLANES, :]
            cur_rot = [pltpu.roll(cur, k, 0) for k in range(1, ksz)]
            acc = bias + taps[ksz - 1] * cur
            for k in range(1, ksz):
                acc = acc + taps[ksz - 1 - k] * jnp.where(row >= k, cur_rot[k - 1], prev_rot[k - 1])
            o_ref[r0:r0 + SUBLANES, :] = acc
            prev_rot = cur_rot
        ext_ref[0:SUBLANES, :] = x_ref[CHUNK - SUBLANES:CHUNK, :]

    return pl.pallas_call(
        body, name=name, grid=(width // tw, nb, nc),
        in_specs=[pl.BlockSpec((CHUNK, tw), lambda j, bb, c: (bb * nc + c, j)),
                  pl.BlockSpec((ksz, tw), lambda j, bb, c: (0, j)),
                  pl.BlockSpec((1, tw), lambda j, bb, c: (0, j))],
        out_specs=pl.BlockSpec((CHUNK, tw), lambda j, bb, c: (bb * nc + c, j)),
        out_shape=jax.ShapeDtypeStruct((rows, width), F32),
        scratch_shapes=[pltpu.VMEM((2 * SUBLANES, tw), F32)],
        compiler_params=_params(("arbitrary", "arbitrary", "arbitrary")))(x, w, b)


def conv_bwd(dc, x, w, nb, name, resid=None, dx_dtype=BF16):
    rows, width = x.shape
    nc = rows // nb // CHUNK
    tw = _tile(width, 1024, LANES)
    ksz = w.shape[0]
    per = CHUNK // SUBLANES
    has_resid = resid is not None

    def body(*refs):
        if has_resid:
            dc_ref, x_ref, halo_ref, w_ref, r_ref, dx_ref, dw_ref, db_ref, extd_ref, extx_ref = refs
        else:
            dc_ref, x_ref, halo_ref, w_ref, dx_ref, dw_ref, db_ref, extd_ref, extx_ref = refs
        bb = pl.program_id(1)
        step = pl.program_id(2)
        c = nc - 1 - step

        @pl.when(jnp.logical_and(bb == 0, step == 0))
        def _():
            dw_ref[...] = jnp.zeros_like(dw_ref)
            db_ref[...] = jnp.zeros_like(db_ref)

        @pl.when(step == 0)
        def _():
            extd_ref[SUBLANES:2 * SUBLANES, :] = jnp.zeros((SUBLANES, tw), F32)

        nstrip = CHUNK // SUBLANES
        taps = [w_ref[j:j + 1, :] for j in range(ksz)]
        row = lax.broadcasted_iota(jnp.int32, (SUBLANES, tw), 0)
        x_prev_rot = [pltpu.roll(jnp.where(c == 0, 0.0, halo_ref[...]), k, 0) for k in range(1, ksz)]
        dcs = dc_ref[0:SUBLANES, :]
        dc_rot = [pltpu.roll(dcs, SUBLANES - k, 0) for k in range(1, ksz)]
        for s in range(nstrip):
            r0 = s * SUBLANES
            nxt = extd_ref[SUBLANES:2 * SUBLANES, :] if s == nstrip - 1 else dc_ref[r0 + SUBLANES:r0 + 2 * SUBLANES, :]
            nxt_rot = [pltpu.roll(nxt, SUBLANES - k, 0) for k in range(1, ksz)]
            xc = x_ref[r0:r0 + SUBLANES, :]
            x_rot = [pltpu.roll(xc, k, 0) for k in range(1, ksz)]
            dx = r_ref[r0:r0 + SUBLANES, :].astype(F32) if has_resid else jnp.zeros((SUBLANES, tw), F32)
            dx = dx + taps[ksz - 1] * dcs
            dw_ref[(ksz - 1) * SUBLANES:ksz * SUBLANES, :] += dcs * xc
            for k in range(1, ksz):
                j = ksz - 1 - k
                dx = dx + taps[j] * jnp.where(row < SUBLANES - k, dc_rot[k - 1], nxt_rot[k - 1])
                dw_ref[j * SUBLANES:(j + 1) * SUBLANES, :] += dcs * jnp.where(row >= k, x_rot[k - 1], x_prev_rot[k - 1])
            if s % 2 == 0:
                held = dx
            else:
                dx_ref[r0 - SUBLANES:r0 + SUBLANES, :] = jnp.concatenate([held, dx], axis=0).astype(dx_ref.dtype)
            db_ref[...] += dcs
            dcs, dc_rot, x_prev_rot = nxt, nxt_rot, x_rot
        extd_ref[SUBLANES:2 * SUBLANES, :] = dc_ref[0:SUBLANES, :]

    def blk(j, bb, step):
        return (bb * nc + nc - 1 - step, j)

    def halo(j, bb, step):
        return (jnp.maximum((bb * nc + nc - 1 - step) * per - 1, 0), j)

    in_specs = [pl.BlockSpec((CHUNK, tw), blk), pl.BlockSpec((CHUNK, tw), blk), pl.BlockSpec((SUBLANES, tw), halo),
                pl.BlockSpec((ksz, tw), lambda j, bb, step: (0, j))]
    args = [dc, x, x, w]
    if has_resid:
        in_specs.append(pl.BlockSpec((CHUNK, tw), blk))
        args.append(resid)
    dx, dw_raw, db_raw = pl.pallas_call(
        body, name=name, grid=(width // tw, nb, nc), in_specs=in_specs,
        out_specs=[pl.BlockSpec((CHUNK, tw), blk), pl.BlockSpec((ksz * SUBLANES, tw), lambda j, bb, step: (0, j)),
                   pl.BlockSpec((SUBLANES, tw), lambda j, bb, step: (0, j))],
        out_shape=[jax.ShapeDtypeStruct((rows, width), dx_dtype), jax.ShapeDtypeStruct((ksz * SUBLANES, width), F32),
                   jax.ShapeDtypeStruct((SUBLANES, width), F32)],
        scratch_shapes=[pltpu.VMEM((2 * SUBLANES, tw), F32), pltpu.VMEM((2 * SUBLANES, tw), F32)],
        compiler_params=_params(("arbitrary", "arbitrary", "arbitrary")))(*args)
    return dx, dw_raw.reshape(ksz, SUBLANES, width).sum(axis=1), db_raw.sum(axis=0, keepdims=True)


S5_Q = 4


def _s5_fill_bu(u, bre_ref, bim_ref, xr_ref, xi_ref, ns):
    for s in range(ns):
        ub = u[:, s * LANES:(s + 1) * LANES].astype(BF16)
        bur = _dot(ub, bre_ref[s], ((1,), (0,)))
        bui = _dot(ub, bim_ref[s], ((1,), (0,)))
        for q in range(S5_Q):
            xr_ref[q, pl.ds(s, CHUNK, stride=ns), :] = bur[:, q * LANES:(q + 1) * LANES]
            xi_ref[q, pl.ds(s, CHUNK, stride=ns), :] = bui[:, q * LANES:(q + 1) * LANES]


def _s5_scan(xr_ref, xi_ref, ar_ref, ai_ref, st_ref, ns):
    ar = [ar_ref[q] for q in range(S5_Q)]
    ai = [ai_ref[q] for q in range(S5_Q)]

    def step(t, carry):
        rows = pl.ds(pl.multiple_of(t * ns, ns), ns)
        out = []
        for q in range(S5_Q):
            pr, pi_ = carry[2 * q], carry[2 * q + 1]
            nr = ar[q] * pr - ai[q] * pi_ + xr_ref[q, rows, :]
            ni = ar[q] * pi_ + ai[q] * pr + xi_ref[q, rows, :]
            xr_ref[q, rows, :] = nr
            xi_ref[q, rows, :] = ni
            out += [nr, ni]
        return tuple(out)

    init = []
    for q in range(S5_Q):
        init += [st_ref[0, q], st_ref[1, q]]
    fin = lax.fori_loop(0, CHUNK, step, tuple(init), unroll=2)
    for q in range(S5_Q):
        st_ref[0, q] = fin[2 * q]
        st_ref[1, q] = fin[2 * q + 1]


def s5_fwd(pa, bre, bim, cre, cim, ar, ai, dvec, nb, name):
    rows = pa.shape[0]
    width = pa.shape[1] // 2
    ns = width // LANES
    nc = rows // nb // CHUNK

    def body(u_ref, bre_ref, bim_ref, cre_ref, cim_ref, ar_ref, ai_ref, d_ref, y_ref, g_ref, so_ref, xr_ref, xi_ref, st_ref):
        c = pl.program_id(1)

        @pl.when(c == 0)
        def _():
            st_ref[...] = jnp.zeros_like(st_ref)

        so_ref[...] = st_ref[...]
        u = u_ref[...]
        _s5_fill_bu(u, bre_ref, bim_ref, xr_ref, xi_ref, ns)
        _s5_scan(xr_ref, xi_ref, ar_ref, ai_ref, st_ref, ns)
        for s in range(ns):
            acc = jnp.zeros((CHUNK, LANES), F32)
            for q in range(S5_Q):
                xr = xr_ref[q, pl.ds(s, CHUNK, stride=ns), :].astype(BF16)
                xi = xi_ref[q, pl.ds(s, CHUNK, stride=ns), :].astype(BF16)
                acc = acc + _dot(xr, cre_ref[s, q * LANES:(q + 1) * LANES, :], ((1,), (0,)))
                acc = acc - _dot(xi, cim_ref[s, q * LANES:(q + 1) * LANES, :], ((1,), (0,)))
            cols = slice(s * LANES, (s + 1) * LANES)
            y = acc + d_ref[:, cols] * u[:, cols]
            y_ref[:, cols] = y
            g_ref[:, cols] = _gelu(y).astype(BF16)

    whole3 = lambda a: pl.BlockSpec(a.shape, lambda b_, c: (0, 0, 0))
    return pl.pallas_call(
        body, name=name, grid=(nb, nc),
        in_specs=[pl.BlockSpec((CHUNK, width), lambda b_, c: (b_ * nc + c, 0)), whole3(bre), whole3(bim), whole3(cre),
                  whole3(cim), whole3(ar), whole3(ai), pl.BlockSpec((1, width), lambda b_, c: (0, 0))],
        out_specs=[pl.BlockSpec((CHUNK, width), lambda b_, c: (b_ * nc + c, 0)),
                   pl.BlockSpec((CHUNK, width), lambda b_, c: (b_ * nc + c, 0)),
                   pl.BlockSpec((None, 2, S5_Q, ns, LANES), lambda b_, c: (b_ * nc + c, 0, 0, 0, 0))],
        out_shape=[jax.ShapeDtypeStruct((rows, width), F32), jax.ShapeDtypeStruct((rows, width), BF16),
                   jax.ShapeDtypeStruct((nb * nc, 2, S5_Q, ns, LANES), F32)],
        scratch_shapes=[pltpu.VMEM((S5_Q, CHUNK * ns, LANES), F32), pltpu.VMEM((S5_Q, CHUNK * ns, LANES), F32),
                        pltpu.VMEM((2, S5_Q, ns, LANES), F32)],
        compiler_params=_params(("arbitrary", "arbitrary")))(pa, bre, bim, cre, cim, ar, ai, dvec)


def s5_bwd(pa, dys, states, bre, bim, cre, cim, ar, ai, dvec, nb, name):
    rows = pa.shape[0]
    width = pa.shape[1] // 2
    ns = width // LANES
    nc = rows // nb // CHUNK

    def body(u_ref, dy_ref, sin_ref, bre_ref, bim_ref, cre_ref, cim_ref, ar_ref, ai_ref, d_ref,
             du_ref, dbre_ref, dbim_ref, dcre_ref, dcim_ref, dar_ref, dai_ref, dd_ref,
             xr_ref, xi_ref, lr_ref, li_ref, st_ref, lam_ref):
        bb = pl.program_id(0)
        step_i = pl.program_id(1)

        @pl.when(jnp.logical_and(bb == 0, step_i == 0))
        def _():
            for r in (dbre_ref, dbim_ref, dcre_ref, dcim_ref, dar_ref, dai_ref, dd_ref):
                r[...] = jnp.zeros_like(r)

        @pl.when(step_i == 0)
        def _():
            lam_ref[...] = jnp.zeros_like(lam_ref)

        u = u_ref[...]
        dy = dy_ref[...]
        st_ref[...] = sin_ref[...]
        _s5_fill_bu(u, bre_ref, bim_ref, xr_ref, xi_ref, ns)
        _s5_scan(xr_ref, xi_ref, ar_ref, ai_ref, st_ref, ns)
        dd_ref[...] += jnp.sum(dy * u, axis=0, keepdims=True)
        for s in range(ns):
            dyb = dy[:, s * LANES:(s + 1) * LANES].astype(BF16)
            gr = _dot(dyb, cre_ref[s], ((1,), (1,)))
            gi = -_dot(dyb, cim_ref[s], ((1,), (1,)))
            for q in range(S5_Q):
                lr_ref[q, pl.ds(s, CHUNK, stride=ns), :] = gr[:, q * LANES:(q + 1) * LANES]
                li_ref[q, pl.ds(s, CHUNK, stride=ns), :] = gi[:, q * LANES:(q + 1) * LANES]
                xr = xr_ref[q, pl.ds(s, CHUNK, stride=ns), :].astype(BF16)
                xi = xi_ref[q, pl.ds(s, CHUNK, stride=ns), :].astype(BF16)
                dcre_ref[s, q * LANES:(q + 1) * LANES, :] += _dot(xr, dyb, ((0,), (0,)))
                dcim_ref[s, q * LANES:(q + 1) * LANES, :] -= _dot(xi, dyb, ((0,), (0,)))
        ar = [ar_ref[q] for q in range(S5_Q)]
        ai = [ai_ref[q] for q in range(S5_Q)]

        def one(t_rows, p_r, p_i, carry):
            out = []
            for q in range(S5_Q):
                l_r, l_i, da_r, da_i = carry[4 * q:4 * q + 4]
                n_r = lr_ref[q, t_rows, :] + ar[q] * l_r + ai[q] * l_i
                n_i = li_ref[q, t_rows, :] + ar[q] * l_i - ai[q] * l_r
                lr_ref[q, t_rows, :] = n_r
                li_ref[q, t_rows, :] = n_i
                xpr, xpi = p_r(q), p_i(q)
                out += [n_r, n_i, da_r + n_r * xpr + n_i * xpi, da_i + n_i * xpr - n_r * xpi]
            return tuple(out)

        def step(k, carry):
            t = CHUNK - 1 - k
            t_rows = pl.ds(pl.multiple_of(t * ns, ns), ns)
            p_rows = pl.ds(pl.multiple_of((t - 1) * ns, ns), ns)
            return one(t_rows, lambda q: xr_ref[q, p_rows, :], lambda q: xi_ref[q, p_rows, :], carry)

        init = []
        zero = jnp.zeros((ns, LANES), F32)
        for q in range(S5_Q):
            init += [lam_ref[0, q], lam_ref[1, q], zero, zero]
        carry = lax.fori_loop(0, CHUNK - 1, step, tuple(init), unroll=2)
        carry = one(pl.ds(0, ns), lambda q: sin_ref[0, q], lambda q: sin_ref[1, q], carry)
        for q in range(S5_Q):
            lam_ref[0, q] = carry[4 * q]
            lam_ref[1, q] = carry[4 * q + 1]
            dar_ref[q] += carry[4 * q + 2]
            dai_ref[q] += carry[4 * q + 3]
        for s in range(ns):
            cols = slice(s * LANES, (s + 1) * LANES)
            ub = u[:, cols].astype(BF16)
            acc = d_ref[:, cols] * dy[:, cols]
            for q in range(S5_Q):
                qs = slice(q * LANES, (q + 1) * LANES)
                lr = lr_ref[q, pl.ds(s, CHUNK, stride=ns), :].astype(BF16)
                li = li_ref[q, pl.ds(s, CHUNK, stride=ns), :].astype(BF16)
                dbre_ref[s, :, qs] += _dot(ub, lr, ((0,), (0,)))
                dbim_ref[s, :, qs] += _dot(ub, li, ((0,), (0,)))
                acc = acc + _dot(lr, bre_ref[s, :, qs], ((1,), (1,))) + _dot(li, bim_ref[s, :, qs], ((1,), (1,)))
            du_ref[:, cols] = acc.astype(du_ref.dtype)

    whole3 = lambda a: pl.BlockSpec(a.shape, lambda b_, c: (0, 0, 0))
    rowblk = pl.BlockSpec((CHUNK, width), lambda b_, c: (b_ * nc + nc - 1 - c, 0))
    scr = pltpu.VMEM((S5_Q, CHUNK * ns, LANES), F32)
    return pl.pallas_call(
        body, name=name, grid=(nb, nc),
        in_specs=[rowblk, rowblk,
                  pl.BlockSpec((None, 2, S5_Q, ns, LANES), lambda b_, c: (b_ * nc + nc - 1 - c, 0, 0, 0, 0)),
                  whole3(bre), whole3(bim), whole3(cre), whole3(cim), whole3(ar), whole3(ai),
                  pl.BlockSpec((1, width), lambda b_, c: (0, 0))],
        out_specs=[rowblk, whole3(bre), whole3(bim), whole3(cre), whole3(cim), whole3(ar), whole3(ai),
                   pl.BlockSpec((1, width), lambda b_, c: (0, 0))],
        out_shape=[jax.ShapeDtypeStruct((rows, width), BF16), jax.ShapeDtypeStruct(bre.shape, F32),
                   jax.ShapeDtypeStruct(bim.shape, F32), jax.ShapeDtypeStruct(cre.shape, F32),
                   jax.ShapeDtypeStruct(cim.shape, F32), jax.ShapeDtypeStruct(ar.shape, F32),
                   jax.ShapeDtypeStruct(ai.shape, F32), jax.ShapeDtypeStruct((1, width), F32)],
        scratch_shapes=[scr, scr, scr, scr, pltpu.VMEM((2, S5_Q, ns, LANES), F32), pltpu.VMEM((2, S5_Q, ns, LANES), F32)],
        compiler_params=_params(("arbitrary", "arbitrary")))(pa, dys, states, bre, bim, cre, cim, ar, ai, dvec)


def _s5_discretize(lam_re, lam_im, log_dt, b_re, b_im):
    dt = jnp.exp(log_dt)[:, None]
    mag = jnp.exp(lam_re * dt)
    ar, ai = mag * jnp.cos(lam_im * dt), mag * jnp.sin(lam_im * dt)
    den = lam_re * lam_re + lam_im * lam_im
    qr = ((ar - 1.0) * lam_re + ai * lam_im) / den
    qi = (ai * lam_re - (ar - 1.0) * lam_im) / den
    bbr = qr[..., None] * b_re - qi[..., None] * b_im
    bbi = qr[..., None] * b_im + qi[..., None] * b_re
    return ar, ai, bbr, bbi


def _s5_expand(ar, ai, bbr, bbi, c_re, c_im):
    g, p, h = bbr.shape
    gps = LANES // h
    ns = g // gps
    eye = jnp.eye(gps, dtype=F32)

    def bexp(b):
        return jnp.einsum("sgph,gk->sghkp", b.reshape(ns, gps, p, h), eye).reshape(ns, gps * h, gps * p)

    def cexp(c):
        return jnp.einsum("sghp,gk->sgpkh", c.reshape(ns, gps, h, p), eye).reshape(ns, gps * p, gps * h)

    def aexp(a):
        return a.reshape(ns, S5_Q, LANES).transpose(1, 0, 2)

    return (bexp(bbr).astype(BF16), bexp(bbi).astype(BF16), cexp(c_re).astype(BF16), cexp(c_im).astype(BF16),
            aexp(ar), aexp(ai))


def _s5_contract(dbre, dbim, dcre, dcim, dar, dai, g, p, h):
    gps = LANES // h
    ns = g // gps
    eye = jnp.eye(gps, dtype=F32)
    bcon = lambda d: jnp.einsum("sghkp,gk->sgph", d.reshape(ns, gps, h, gps, p), eye).reshape(g, p, h)
    ccon = lambda d: jnp.einsum("sgpkh,gk->sghp", d.reshape(ns, gps, p, gps, h), eye).reshape(g, h, p)
    acon = lambda d: d.transpose(1, 0, 2).reshape(g, p)
    return bcon(dbre), bcon(dbim), ccon(dcre), ccon(dcim), acon(dar), acon(dai)


PROJ_BLOCK = 256


def _ml_proj_tile(cpre, xb, wq, wk, wv, gq, gk, gv):
    xc = _silu(cpre)
    q = _dot_nn(xc, wq)
    k = _dot_nn(xc, wk)
    v = _dot_nn(xb, wv)
    return q, k, v, _dot_nn(q, gq) + _dot_nn(k, gk) + _dot_nn(v, gv)


def ml_proj_fwd(cpre, xb, wq, wk, wv, gq, gk, gv, name):
    rows, width = cpre.shape
    pb = wq.shape[1]
    nblk = width // pb
    tr = _tile(rows, 1088, 16)

    def body(c_ref, x_ref, wq_ref, wk_ref, wv_ref, gq_ref, gk_ref, gv_ref, q_ref, k_ref, v_ref, g_ref):
        j = pl.program_id(1)
        q, k, v, g = _ml_proj_tile(c_ref[...], x_ref[...], wq_ref[...], wk_ref[...], wv_ref[...],
                                   gq_ref[...], gk_ref[...], gv_ref[...])
        q_ref[...] = q
        k_ref[...] = k
        v_ref[...] = v

        @pl.when(j == 0)
        def _():
            g_ref[...] = jnp.zeros_like(g_ref)

        g_ref[...] += g

    rb = pl.BlockSpec((tr, pb), lambda i, j: (i, j))
    wb = pl.BlockSpec((None, pb, pb), lambda i, j: (j, 0, 0))
    gwb = pl.BlockSpec((None, pb, LANES), lambda i, j: (j, 0, 0))
    return pl.pallas_call(
        body, name=name, grid=(rows // tr, nblk), in_specs=[rb, rb, wb, wb, wb, gwb, gwb, gwb],
        out_specs=[rb, rb, rb, pl.BlockSpec((tr, LANES), lambda i, j: (i, 0))],
        out_shape=[jax.ShapeDtypeStruct((rows, width), F32)] * 3 + [jax.ShapeDtypeStruct((rows, LANES), F32)],
        compiler_params=_params(("arbitrary", "arbitrary")))(cpre, xb, wq, wk, wv, gq, gk, gv)


def ml_proj_bwd(cpre, xb, wq, wk, wv, gq, gk, gv, dq, dk, dv, dg, dcp_extra, name):
    rows, width = cpre.shape
    pb = wq.shape[1]
    nblk = width // pb
    tr = _tile(rows, 1088, 16)

    def body(c_ref, x_ref, wq_ref, wk_ref, wv_ref, gq_ref, gk_ref, gv_ref, dq_ref, dk_ref, dv_ref, dg_ref, e_ref,
             dc_ref, dx_ref, *dw_refs):
        i = pl.program_id(1)
        _, vjp = jax.vjp(_ml_proj_tile, c_ref[...], x_ref[...], wq_ref[...], wk_ref[...], wv_ref[...],
                         gq_ref[...], gk_ref[...], gv_ref[...])
        grads = vjp((dq_ref[...], dk_ref[...], dv_ref[...], dg_ref[...]))
        dc_ref[...] = grads[0] + e_ref[...]
        dx_ref[...] = grads[1]

        @pl.when(i == 0)
        def _():
            for r in dw_refs:
                r[...] = jnp.zeros_like(r)

        for r, gval in zip(dw_refs, grads[2:]):
            r[...] += gval

    rb = pl.BlockSpec((tr, pb), lambda j, i: (i, j))
    wb = pl.BlockSpec((None, pb, pb), lambda j, i: (j, 0, 0))
    gwb = pl.BlockSpec((None, pb, LANES), lambda j, i: (j, 0, 0))
    gb = pl.BlockSpec((tr, LANES), lambda j, i: (i, 0))
    wshape = jax.ShapeDtypeStruct((nblk, pb, pb), F32)
    gshape = jax.ShapeDtypeStruct((nblk, pb, LANES), F32)
    return pl.pallas_call(
        body, name=name, grid=(nblk, rows // tr), in_specs=[rb, rb, wb, wb, wb, gwb, gwb, gwb, rb, rb, rb, gb, rb],
        out_specs=[rb, rb] + [wb] * 3 + [gwb] * 3,
        out_shape=[jax.ShapeDtypeStruct((rows, width), F32)] * 2 + [wshape] * 3 + [gshape] * 3,
        compiler_params=_params(("arbitrary", "arbitrary")))(cpre, xb, wq, wk, wv, gq, gk, gv, dq, dk, dv, dg, dcp_extra)


def _ml_gates_tile(gl, bg, nh):
    x = gl + bg
    bcum = _dot(_tri(CHUNK), _log_sigmoid(x), ((1,), (0,)), precision=HI)
    lane = lax.broadcasted_iota(jnp.int32, x.shape, 1)
    return jnp.where(lane < nh, x, jnp.where(lane < 2 * nh, bcum, 0.0))


def _ml_core_tile(q, k, v, colg, rowg, cpre, zb, nw, sk, cst, nst, m_prev):
    c, dh = q.shape
    igc, bc = _lane_pick(colg, 0), _lane_pick(colg, 1)
    igr, br = _row_pick(rowg, 0), _row_pick(rowg, 1)
    causal = _tri(c) > 0
    dmat = jnp.where(causal, bc - br + igr, -jnp.inf)
    inter = bc + m_prev
    mt = lax.stop_gradient(jnp.maximum(inter, jnp.max(dmat, axis=1, keepdims=True)))
    wt = jnp.exp(dmat - mt)
    w_prev = jnp.exp(inter - mt)
    qs = q * (dh ** -0.5)
    s = _dot_nt(qs, k) * wt
    num = _dot_nn(s, v) + w_prev * _dot_nn(qs, cst)
    den = jnp.sum(s, axis=1, keepdims=True) + w_prev * jnp.sum(qs * nst, axis=1, keepdims=True)
    h = num * (1.0 / jnp.maximum(jnp.abs(den), jnp.exp(-mt)))
    last = (lax.broadcasted_iota(jnp.int32, (c, 1), 0) == c - 1).astype(F32)
    blast = jnp.sum(bc * last, axis=0, keepdims=True)
    g = blast - bc + igc
    m_new = lax.stop_gradient(jnp.maximum(blast + m_prev, jnp.max(g, axis=0, keepdims=True)))
    decay = jnp.exp(blast + m_prev - m_new)
    wk = jnp.exp(g - m_new) * k
    c_new = decay * cst + _dot_tn(wk, v)
    n_new = decay * nst + jnp.sum(wk, axis=0, keepdims=True)
    mu = jnp.mean(h, axis=1, keepdims=True)
    hc = h - mu
    var = jnp.mean(hc * hc, axis=1, keepdims=True)
    out = hc * lax.rsqrt(var + HEAD_NORM_EPS) * nw + sk * _silu(cpre)
    return out * _silu(zb), c_new, n_new, m_new


def _ml_core_specs(nc, dh, rev):
    ch = (lambda c: nc - 1 - c) if rev else (lambda c: c)
    rb = pl.BlockSpec((CHUNK, dh), lambda b_, c, h: (b_ * nc + ch(c), h))
    colb = pl.BlockSpec((None, CHUNK, 2), lambda b_, c, h: (h, b_ * nc + ch(c), 0))
    rowb = pl.BlockSpec((None, None, 2, CHUNK), lambda b_, c, h: (b_ * nc + ch(c), h, 0, 0))
    pb = pl.BlockSpec((1, dh), lambda b_, c, h: (0, h))
    cb = pl.BlockSpec((None, None, dh, dh), lambda b_, c, h: (b_ * nc + ch(c), h, 0, 0))
    nb_ = pl.BlockSpec((None, None, 1, dh), lambda b_, c, h: (b_ * nc + ch(c), h, 0, 0))
    mb = pl.BlockSpec((None, None, 1, 1), lambda b_, c, h: (b_ * nc + ch(c), h, 0, 0))
    return rb, colb, rowb, pb, cb, nb_, mb


def ml_core_fwd(q, k, v, colg, rowg, cpre, zb, nw, sk, nb, nh, name):
    rows, width = q.shape
    dh = width // nh
    nc = rows // nb // CHUNK
    rb, colb, rowb, pb, cb, nb_, mb = _ml_core_specs(nc, dh, False)

    def body(q_ref, k_ref, v_ref, col_ref, row_ref, c_ref, z_ref, nw_ref, sk_ref, y_ref, cs_ref, ns_ref, ms_ref,
             cst_ref, nst_ref, mst_ref):
        c = pl.program_id(1)
        h = pl.program_id(2)

        @pl.when(c == 0)
        def _():
            cst_ref[h] = jnp.zeros((dh, dh), F32)
            nst_ref[h] = jnp.zeros((1, dh), F32)
            mst_ref[h] = jnp.zeros((1, 1), F32)

        cst, nst, m_prev = cst_ref[h], nst_ref[h], mst_ref[h]
        cs_ref[...] = cst
        ns_ref[...] = nst
        ms_ref[...] = m_prev
        y, c_new, n_new, m_new = _ml_core_tile(q_ref[...], k_ref[...], v_ref[...], col_ref[...], row_ref[...],
                                               c_ref[...], z_ref[...], nw_ref[...], sk_ref[...], cst, nst, m_prev)
        y_ref[...] = y.astype(BF16)
        cst_ref[h] = c_new
        nst_ref[h] = n_new
        mst_ref[h] = m_new

    nbc = nb * nc
    return pl.pallas_call(
        body, name=name, grid=(nb, nc, nh), in_specs=[rb, rb, rb, colb, rowb, rb, rb, pb, pb],
        out_specs=[rb, cb, nb_, mb],
        out_shape=[jax.ShapeDtypeStruct((rows, width), BF16), jax.ShapeDtypeStruct((nbc, nh, dh, dh), F32),
                   jax.ShapeDtypeStruct((nbc, nh, 1, dh), F32), jax.ShapeDtypeStruct((nbc, nh, 1, 1), F32)],
        scratch_shapes=[pltpu.VMEM((nh, dh, dh), F32), pltpu.VMEM((nh, 1, dh), F32), pltpu.VMEM((nh, 1, 1), F32)],
        compiler_params=_params(("arbitrary", "arbitrary", "arbitrary")))(q, k, v, colg, rowg, cpre, zb, nw, sk)


def ml_core_bwd(q, k, v, colg, rowg, cpre, zb, nw, sk, cs, ns, ms, dy, nb, nh, name):
    rows, width = q.shape
    dh = width // nh
    nc = rows // nb // CHUNK
    rb, colb, rowb, pb, cb, nb_, mb = _ml_core_specs(nc, dh, True)

    def body(q_ref, k_ref, v_ref, col_ref, row_ref, c_ref, z_ref, nw_ref, sk_ref, cs_ref, ns_ref, ms_ref, dy_ref,
             dq_ref, dk_ref, dv_ref, dc_ref, dz_ref, dcol_ref, drow_ref, dnw_ref, dsk_ref, dcst_ref, dnst_ref):
        bb = pl.program_id(0)
        step = pl.program_id(1)
        h = pl.program_id(2)

        @pl.when(jnp.logical_and(bb == 0, jnp.logical_and(step == 0, h == 0)))
        def _():
            dnw_ref[...] = jnp.zeros_like(dnw_ref)
            dsk_ref[...] = jnp.zeros_like(dsk_ref)

        @pl.when(step == 0)
        def _():
            dcst_ref[h] = jnp.zeros((dh, dh), F32)
            dnst_ref[h] = jnp.zeros((1, dh), F32)

        m_prev = ms_ref[...]

        def f(*a):
            return _ml_core_tile(*a, m_prev)[:3]

        _, vjp = jax.vjp(f, q_ref[...], k_ref[...], v_ref[...], col_ref[...], row_ref[...], c_ref[...], z_ref[...],
                         nw_ref[...], sk_ref[...], cs_ref[...], ns_ref[...])
        g = vjp((dy_ref[...], dcst_ref[h], dnst_ref[h]))
        dq_ref[...] = g[0]
        dk_ref[...] = g[1]
        dv_ref[...] = g[2]
        dcol_ref[...] = g[3]
        drow_ref[...] = g[4]
        dc_ref[...] = g[5]
        dz_ref[...] = g[6].astype(dz_ref.dtype)
        dnw_ref[h] += g[7]
        dsk_ref[h] += g[8]
        dcst_ref[h] = g[9]
        dnst_ref[h] = g[10]

    nbc = nb * nc
    accb = pl.BlockSpec((nh, 1, dh), lambda b_, c, h: (0, 0, 0))
    return pl.pallas_call(
        body, name=name, grid=(nb, nc, nh), in_specs=[rb, rb, rb, colb, rowb, rb, rb, pb, pb, cb, nb_, mb, rb],
        out_specs=[rb, rb, rb, rb, rb, colb, rowb, accb, accb],
        out_shape=[jax.ShapeDtypeStruct((rows, width), F32)] * 4 + [jax.ShapeDtypeStruct((rows, width), BF16)]
        + [jax.ShapeDtypeStruct(colg.shape, F32), jax.ShapeDtypeStruct(rowg.shape, F32),
           jax.ShapeDtypeStruct((nh, 1, dh), F32), jax.ShapeDtypeStruct((nh, 1, dh), F32)],
        scratch_shapes=[pltpu.VMEM((nh, dh, dh), F32), pltpu.VMEM((nh, 1, dh), F32)],
        compiler_params=_params(("arbitrary", "arbitrary", "arbitrary")))(
            q, k, v, colg, rowg, cpre, zb, nw, sk, cs, ns, ms, dy)


def _ssd_dt_tile(dtr, bias, alog):
    dt = _softplus(dtr + bias)
    cum = _dot(_tri(CHUNK), dt * (-jnp.exp(alog)), ((1,), (0,)), precision=HI)
    return dt, cum


def _ssd_tile(xcs, bmc, cmc, cols, rows_, z, dvec, gn, states, hpg):
    npair = hpg // 2
    hd = SSD_HEAD_DIM
    xs = [_silu(x) for x in xcs]
    bm, cm = _silu(bmc), _silu(cmc)
    cb = _dot_nt(cm, bm)
    causal = _tri(CHUNK) > 0
    lane_lo = lax.broadcasted_iota(jnp.int32, (1, 2 * hd), 1) < hd
    lastsel = (lax.broadcasted_iota(jnp.int32, (CHUNK, 1), 0) == CHUNK - 1).astype(F32)
    heads = []
    for r in range(hpg):
        dtc, cumc = _lane_pick(cols, r), _lane_pick(cols, hpg + r)
        dtrow, cumr = _row_pick(rows_, r), _row_pick(rows_, hpg + r)
        w = cb * jnp.exp(jnp.where(causal, cumc - cumr, -jnp.inf)) * dtrow
        last = jnp.sum(cumc * lastsel, axis=0, keepdims=True)
        heads.append((w, jnp.exp(cumc), jnp.exp(last - cumc) * dtc, jnp.exp(last)))
    ys, new_states = [], []
    for j in range(npair):
        (wa, ea, da, la), (wb, eb, db, lb) = heads[2 * j], heads[2 * j + 1]
        yi = jnp.where(lane_lo, _dot_nn(wa, xs[j]), _dot_nn(wb, xs[j]))
        ys.append(yi + jnp.where(lane_lo, ea, eb) * _dot_nn(cm, states[j]))
        xd = xs[j] * jnp.where(lane_lo, da, db)
        new_states.append(jnp.where(lane_lo, la, lb) * states[j] + _dot_tn(bm, xd))
    y = jnp.concatenate(ys, axis=1) + dvec * jnp.concatenate(xs, axis=1)
    yg = y * _silu(z)
    yn = yg * lax.rsqrt(jnp.mean(yg * yg, axis=1, keepdims=True) + NORM_EPS) * gn
    return yn, new_states


def _ssd_specs(nc, hpg, ng, rev):
    npair = hpg // 2
    gw = hpg * SSD_HEAD_DIM
    xblocks = ng * npair
    ch = (lambda c: nc - 1 - c) if rev else (lambda c: c)
    xs = [pl.BlockSpec((CHUNK, LANES), functools.partial(lambda b_, c, g, jj: (b_ * nc + ch(c), g * npair + jj), jj=j))
          for j in range(npair)]
    bmb = pl.BlockSpec((CHUNK, SSD_STATE), lambda b_, c, g: (b_ * nc + ch(c), xblocks + g))
    cmb = pl.BlockSpec((CHUNK, SSD_STATE), lambda b_, c, g: (b_ * nc + ch(c), xblocks + ng + g))
    colb = pl.BlockSpec((None, CHUNK, 2 * hpg), lambda b_, c, g: (g, b_ * nc + ch(c), 0))
    rowb = pl.BlockSpec((None, None, 2 * hpg, CHUNK), lambda b_, c, g: (b_ * nc + ch(c), g, 0, 0))
    zb = pl.BlockSpec((CHUNK, gw), lambda b_, c, g: (b_ * nc + ch(c), g))
    pb = pl.BlockSpec((1, gw), lambda b_, c, g: (0, g))
    sb = pl.BlockSpec((None, None, npair, SSD_STATE, 2 * SSD_HEAD_DIM), lambda b_, c, g: (b_ * nc + ch(c), g, 0, 0, 0))
    return xs, bmb, cmb, colb, rowb, zb, pb, sb


def ssd_core_fwd(cpre, cols, rows_, z, dvec, gn, nb, hpg, name):
    rows = cpre.shape[0]
    inner = z.shape[1]
    ng = inner // (hpg * SSD_HEAD_DIM)
    npair = hpg // 2
    nc = rows // nb // CHUNK
    xs, bmb, cmb, colb, rowb, zb, pb, sb = _ssd_specs(nc, hpg, ng, False)

    def body(*refs):
        x_refs = refs[:npair]
        bm_ref, cm_ref, col_ref, row_ref, z_ref, d_ref, gn_ref, y_ref, so_ref, st_ref = refs[npair:]
        c = pl.program_id(1)
        g = pl.program_id(2)

        @pl.when(c == 0)
        def _():
            st_ref[g] = jnp.zeros((npair, SSD_STATE, 2 * SSD_HEAD_DIM), F32)

        so_ref[...] = st_ref[g]
        states = [st_ref[g, j] for j in range(npair)]
        yn, new_states = _ssd_tile([r[...] for r in x_refs], bm_ref[...], cm_ref[...], col_ref[...], row_ref[...],
                                   z_ref[...], d_ref[...], gn_ref[...], states, hpg)
        y_ref[...] = yn.astype(BF16)
        for j in range(npair):
            st_ref[g, j] = new_states[j]

    return pl.pallas_call(
        body, name=name, grid=(nb, nc, ng), in_specs=xs + [bmb, cmb, colb, rowb, zb, pb, pb],
        out_specs=[zb, sb],
        out_shape=[jax.ShapeDtypeStruct((rows, inner), BF16),
                   jax.ShapeDtypeStruct((nb * nc, ng, npair, SSD_STATE, 2 * SSD_HEAD_DIM), F32)],
        scratch_shapes=[pltpu.VMEM((ng, npair, SSD_STATE, 2 * SSD_HEAD_DIM), F32)],
        compiler_params=_params(("arbitrary", "arbitrary", "arbitrary")))(
            *([cpre] * npair), cpre, cpre, cols, rows_, z, dvec, gn)


def ssd_core_bwd(cpre, cols, rows_, z, dvec, gn, states, dyn, nb, hpg, name):
    rows = cpre.shape[0]
    inner = z.shape[1]
    gw = hpg * SSD_HEAD_DIM
    ng = inner // gw
    npair = hpg // 2
    nc = rows // nb // CHUNK
    xs, bmb, cmb, colb, rowb, zb, pb, sb = _ssd_specs(nc, hpg, ng, True)

    def body(*refs):
        x_refs = refs[:npair]
        (bm_ref, cm_ref, col_ref, row_ref, z_ref, d_ref, gn_ref, s_ref, dy_ref,
         dx_ref, dbm_ref, dcm_ref, dcol_ref, drow_ref, dz_ref, dd_ref, dgn_ref, dst_ref) = refs[npair:]
        bb = pl.program_id(0)
        step = pl.program_id(1)
        g = pl.program_id(2)

        @pl.when(jnp.logical_and(bb == 0, jnp.logical_and(step == 0, g == 0)))
        def _():
            dd_ref[...] = jnp.zeros_like(dd_ref)
            dgn_ref[...] = jnp.zeros_like(dgn_ref)

        @pl.when(step == 0)
        def _():
            dst_ref[g] = jnp.zeros((npair, SSD_STATE, 2 * SSD_HEAD_DIM), F32)

        def f(xcs, bmc, cmc, cv, rv, zv, dv_, gv, sts):
            return _ssd_tile(xcs, bmc, cmc, cv, rv, zv, dv_, gv, sts, hpg)

        _, vjp = jax.vjp(f, [r[...] for r in x_refs], bm_ref[...], cm_ref[...], col_ref[...], row_ref[...], z_ref[...],
                         d_ref[...], gn_ref[...], [s_ref[j] for j in range(npair)])
        gr = vjp((dy_ref[...], [dst_ref[g, j] for j in range(npair)]))
        dx_ref[...] = jnp.concatenate(gr[0], axis=1)
        dbm_ref[...] = gr[1]
        dcm_ref[...] = gr[2]
        dcol_ref[...] = gr[3]
        drow_ref[...] = gr[4]
        dz_ref[...] = gr[5].astype(dz_ref.dtype)
        dd_ref[g] += gr[6]
        dgn_ref[g] += gr[7]
        for j in range(npair):
            dst_ref[g, j] = gr[8][j]

    ch = lambda c: nc - 1 - c
    nblk = pl.BlockSpec((CHUNK, SSD_STATE), lambda b_, c, g: (b_ * nc + ch(c), g))
    accb = pl.BlockSpec((ng, 1, gw), lambda b_, c, g: (0, 0, 0))
    return pl.pallas_call(
        body, name=name, grid=(nb, nc, ng), in_specs=xs + [bmb, cmb, colb, rowb, zb, pb, pb, sb, zb],
        out_specs=[zb, nblk, nblk, colb, rowb, zb, accb, accb],
        out_shape=[jax.ShapeDtypeStruct((rows, inner), F32), jax.ShapeDtypeStruct((rows, ng * SSD_STATE), F32),
                   jax.ShapeDtypeStruct((rows, ng * SSD_STATE), F32), jax.ShapeDtypeStruct(cols.shape, F32),
                   jax.ShapeDtypeStruct(rows_.shape, F32), jax.ShapeDtypeStruct((rows, inner), BF16),
                   jax.ShapeDtypeStruct((ng, 1, gw), F32), jax.ShapeDtypeStruct((ng, 1, gw), F32)],
        scratch_shapes=[pltpu.VMEM((ng, npair, SSD_STATE, 2 * SSD_HEAD_DIM), F32)],
        compiler_params=_params(("arbitrary", "arbitrary", "arbitrary")))(
            *([cpre] * npair), cpre, cpre, cols, rows_, z, dvec, gn, states, dyn)


def _hw_expand(w):
    n, bi, _ = w.shape
    per = PROJ_BLOCK // bi
    eye = jnp.eye(per, dtype=F32)
    return jnp.einsum("jbio,bc->jbico", w.reshape(n // per, per, bi, bi), eye).reshape(n // per, PROJ_BLOCK, PROJ_BLOCK)


def _hw_contract(d, bi=QKV_BLOCK):
    per = PROJ_BLOCK // bi
    eye = jnp.eye(per, dtype=F32)
    return jnp.einsum("jbico,bc->jbio", d.reshape(d.shape[0], per, bi, per, bi), eye).reshape(-1, bi, bi)


def _wg_expand(wg, width):
    pad = jnp.pad(wg, ((0, 0), (0, LANES - wg.shape[1])))
    return [pad[i * width:(i + 1) * width].reshape(width // PROJ_BLOCK, PROJ_BLOCK, LANES) for i in range(3)]


def _wg_contract(dgs, ngate):
    return jnp.concatenate([d[:, :, :ngate].reshape(-1, ngate) for d in dgs], axis=0)


def _pad_lanes(a):
    return jnp.pad(a, ((0, 0), (0, LANES - a.shape[1])))


def _pairs_to_layouts(first, second, ngrp, per, nbc):
    rows = first.shape[0]
    both = jnp.concatenate([first.reshape(rows, ngrp, per), second.reshape(rows, ngrp, per)], axis=2)
    return both.transpose(1, 0, 2), both.reshape(nbc, CHUNK, ngrp, 2 * per).transpose(0, 2, 3, 1)


def _layouts_to_pairs(dcols, drows, ngrp, per):
    rows = dcols.shape[1]
    both = dcols.transpose(1, 0, 2) + drows.transpose(0, 3, 1, 2).reshape(rows, ngrp, 2 * per)
    return both[:, :, :per].reshape(rows, ngrp * per), both[:, :, per:].reshape(rows, ngrp * per)


_EARLY = ("W0a", "W0xb", "W0zb", "glu")
_LATE = ("Wo0a", "Wo0b", "W1z", "W1x", "W1dt", "Wo1")


def _local_step(x, target, bw, sp, late_weights=None, late_grads=None, early_grads=None):
    nb, seq, d = x.shape
    nh, hpg = MLSTM_HEADS, SSD_HPG
    t_len = N_META + seq
    nc = -(-t_len // CHUNK)
    tp = nc * CHUNK
    rows = nb * tp
    nbc = nb * nc
    meta = sp["meta_tokens"]
    h0 = jnp.concatenate([jnp.broadcast_to(meta[None], (nb, N_META, d)), x, jnp.zeros((nb, tp - t_len, d), F32)], axis=1)
    h0 = h0.reshape(rows, d)
    tgt = jnp.pad(target, ((0, 0), (N_META, tp - t_len), (0, 0))).reshape(rows, d)

    n0 = norm_fwd(h0, sp["ab_norm"], "norm0")
    pa = mm(n0, bw["W0a"], "nn", "mm_pa")
    xb = mm(n0, bw["W0xb"], "nn", "mm_xb")
    zb = mm(n0, bw["W0zb"], "nn", "mm_zb")
    s5w = pa.shape[1] // 2
    mlw = xb.shape[1]
    s5_args = (sp["s5_lambda_re"], sp["s5_lambda_im"], sp["s5_log_dt"].reshape(-1), sp["s5_b_re"], sp["s5_b_im"])
    (ar, ai, bbr, bbi), s5_disc_vjp = jax.vjp(_s5_discretize, *s5_args)
    sg, spn, shh = bbr.shape
    bre, bim, cre, cim, are, aie = _s5_expand(ar, ai, bbr, bbi, sp["s5_c_re"], sp["s5_c_im"])
    ys5, gb, s5st = s5_fwd(pa, bre, bim, cre, cim, are, aie, sp["s5_d"], nb, "s5_fwd")
    tglu = mm(gb, bw["glu"], "nn", "mm_glu")

    def glu_tile(ys, tt, za, gbias):
        return _gelu(ys) * _sigmoid(tt + gbias) * _silu(za)

    ya = rowwise("glu_fwd", lambda i, ys, tt, pab, gbias: glu_tile(ys, tt, pab[:, s5w:], gbias),
                 [ys5, tglu, pa], [sp["s5_glu_b"]], [(s5w, BF16)], tr=_tile(rows, 256, 16))[0]

    cpre0 = conv_fwd(xb, sp["ml_conv_w"], sp["ml_conv_b"], nb, "ml_conv_fwd")
    wq_e, wk_e, wv_e = _hw_expand(sp["ml_wq"]), _hw_expand(sp["ml_wk"]), _hw_expand(sp["ml_wv"])
    gq, gk, gv = _wg_expand(sp["ml_w_gate"], mlw)
    q, k, v, gl = ml_proj_fwd(cpre0, xb, wq_e, wk_e, wv_e, gq, gk, gv, "ml_proj_fwd")
    bgate = _pad_lanes(sp["ml_b_gate"])
    gout = rowwise("ml_gates_fwd", lambda i, g_, b_: _ml_gates_tile(g_, b_, nh), [gl], [bgate], [(LANES, F32)], tr=CHUNK)[0]
    colg, rowg = _pairs_to_layouts(gout[:, :nh], gout[:, nh:2 * nh], nh, 1, nbc)
    yb, ml_cs, ml_ns, ml_ms = ml_core_fwd(q, k, v, colg, rowg, cpre0, zb, sp["ml_norm"], sp["ml_skip"], nb, nh, "ml_core_fwd")
    if late_weights is not None:
        bw = {**bw, **late_weights()}
    h1 = mm(ya, bw["Wo0a"], "nn", "mm_out0a", resid=h0)
    h1 = mm(yb, bw["Wo0b"], "nn", "mm_out0b", resid=h1)

    n1 = norm_fwd(h1, sp["ssd_norm"], "norm1")
    z1 = mm(n1, bw["W1z"], "nn", "mm_z1")
    xbc = mm(n1, bw["W1x"], "nn", "mm_xbc")
    dtr = mm(n1, bw["W1dt"], "nn", "mm_dt")
    inner = z1.shape[1]
    ng = inner // (hpg * SSD_HEAD_DIM)
    nhd = ng * hpg
    cpre1 = conv_fwd(xbc, sp["ssd_conv_w"], sp["ssd_conv_b"], nb, "ssd_conv_fwd")
    dt_bias, a_log = _pad_lanes(sp["ssd_dt_bias"]), _pad_lanes(sp["ssd_a_log"])
    dt, cum = rowwise("ssd_dt_fwd", lambda i, r_, b_, a_: _ssd_dt_tile(r_, b_, a_), [dtr], [dt_bias, a_log],
                      [(LANES, F32), (LANES, F32)], tr=CHUNK)
    cols, rws = _pairs_to_layouts(dt[:, :nhd], cum[:, :nhd], ng, hpg, nbc)
    dvec = jnp.repeat(sp["ssd_d"], SSD_HEAD_DIM, axis=1)
    yn, ssd_st = ssd_core_fwd(cpre1, cols, rws, z1, dvec, sp["ssd_gnorm"], nb, hpg, "ssd_core_fwd")
    h2 = mm(yn, bw["Wo1"], "nn", "mm_out1", resid=h1)

    tr_l = _tile(tp, 256, 16)
    per_ex = tp // tr_l

    def loss_tile(i, hb, tb, gfn):
        tpos = (i % per_ex) * tr_l + lax.broadcasted_iota(jnp.int32, (tr_l, 1), 0)
        mask = jnp.logical_and(tpos >= N_META, tpos < t_len).astype(F32)

        def lf(hh, gg):
            e = (_rms(hh, gg) - tb) * mask
            return 0.5 * jnp.sum(e * e) / d

        lval, (dh, dg) = jax.value_and_grad(lf, (0, 1))(hb, gfn)
        return dh, dh, jnp.full((1, LANES), lval, F32), dg

    fn = sp["final_norm"].reshape(1, d)
    dh2, dh2b, loss_acc, dfn = rowwise("loss", loss_tile, [h2, tgt], [fn], [(d, F32), (d, BF16)], [(1, LANES), (1, d)], tr=tr_l)

    gbig, gs = {}, {}
    gs["final_norm"] = dfn.reshape(sp["final_norm"].shape)
    dyn = mm(dh2b, bw["Wo1"], "nt", "mm_dyn")
    gbig["Wo1"] = mm(yn, dh2b, "tn", "mm_dWo1", out_dtype=BF16)
    dxs, dbm, dcm, dcols, drws, dz1, ddvec, dgn = ssd_core_bwd(cpre1, cols, rws, z1, dvec, sp["ssd_gnorm"], ssd_st, dyn,
                                                              nb, hpg, "ssd_core_bwd")
    gs["ssd_d"] = ddvec.reshape(1, nhd, SSD_HEAD_DIM).sum(axis=2)
    gs["ssd_gnorm"] = dgn.reshape(1, inner)
    ddt, dcum = _layouts_to_pairs(dcols, drws, ng, hpg)

    def ssd_dt_bwd_tile(i, r_, ddt_, dcum_, b_, a_):
        _, vjp = jax.vjp(_ssd_dt_tile, r_, b_, a_)
        return vjp((ddt_, dcum_))

    ddtr, dbias, dalog = rowwise("ssd_dt_bwd", ssd_dt_bwd_tile, [dtr, _pad_lanes(ddt), _pad_lanes(dcum)], [dt_bias, a_log],
                                 [(LANES, BF16)], [(1, LANES), (1, LANES)], tr=CHUNK)
    gs["ssd_dt_bias"] = dbias[:, :nhd]
    gs["ssd_a_log"] = dalog[:, :nhd]
    dcpre1 = jnp.concatenate([dxs, dbm, dcm], axis=1)
    dxbc, dcw1, dcb1 = conv_bwd(dcpre1, xbc, sp["ssd_conv_w"], nb, "ssd_conv_bwd")
    gs["ssd_conv_w"] = dcw1
    gs["ssd_conv_b"] = dcb1
    dn1 = mm(dz1, bw["W1z"], "nt", "mm_dn1z")
    dn1 = mm(dxbc, bw["W1x"], "nt", "mm_dn1x", resid=dn1)
    dn1 = mm(ddtr, bw["W1dt"], "nt", "mm_dn1dt", resid=dn1)
    gbig["W1z"] = mm(n1, dz1, "tn", "mm_dW1z", out_dtype=BF16)
    gbig["W1x"] = mm(n1, dxbc, "tn", "mm_dW1x", out_dtype=BF16)
    gbig["W1dt"] = mm(n1, ddtr, "tn", "mm_dW1dt", out_dtype=BF16)
    dh1, dh1b, dg1 = norm_bwd(h1, sp["ssd_norm"], dn1, dh2, "norm1_bwd")
    gs["ssd_norm"] = dg1

    gbig["Wo0a"] = mm(ya, dh1b, "tn", "mm_dWo0a", out_dtype=BF16)
    gbig["Wo0b"] = mm(yb, dh1b, "tn", "mm_dWo0b", out_dtype=BF16)
    if late_grads is not None:
        late_grads({n: gbig[n] for n in _LATE})
    dya = mm(dh1b, bw["Wo0a"], "nt", "mm_dya")
    dyb = mm(dh1b, bw["Wo0b"], "nt", "mm_dyb")
    (dq, dk, dv, dcp_skip, dzb, dcolg, drowg, dnw, dsk) = ml_core_bwd(
        q, k, v, colg, rowg, cpre0, zb, sp["ml_norm"], sp["ml_skip"], ml_cs, ml_ns, ml_ms, dyb, nb, nh, "ml_core_bwd")
    gs["ml_norm"] = dnw.reshape(1, mlw)
    gs["ml_skip"] = dsk.reshape(1, mlw)
    dig, dbcum = _layouts_to_pairs(dcolg, drowg, nh, 1)
    dgout = _pad_lanes(jnp.concatenate([dig, dbcum], axis=1))

    def ml_gates_bwd_tile(i, g_, dgo, b_):
        _, vjp = jax.vjp(lambda a, b: _ml_gates_tile(a, b, nh), g_, b_)
        return vjp(dgo)

    dgl, dbg = rowwise("ml_gates_bwd", ml_gates_bwd_tile, [gl, dgout], [bgate], [(LANES, F32)], [(1, LANES)], tr=CHUNK)
    gs["ml_b_gate"] = dbg[:, :2 * nh]
    dcpre0, dxb_v, dwq, dwk, dwv, dgq, dgk, dgv = ml_proj_bwd(cpre0, xb, wq_e, wk_e, wv_e, gq, gk, gv, dq, dk, dv, dgl,
                                                            dcp_skip, "ml_proj_bwd")
    gs["ml_wq"], gs["ml_wk"], gs["ml_wv"] = _hw_contract(dwq), _hw_contract(dwk), _hw_contract(dwv)
    gs["ml_w_gate"] = _wg_contract([dgq, dgk, dgv], 2 * nh)
    dxb, dcw0, dcb0 = conv_bwd(dcpre0, xb, sp["ml_conv_w"], nb, "ml_conv_bwd", resid=dxb_v)
    gs["ml_conv_w"] = dcw0
    gs["ml_conv_b"] = dcb0

    def glu_bwd_tile(i, ys, tt, pab, dy_, gbias):
        _, vjp = jax.vjp(glu_tile, ys, tt, pab[:, s5w:], gbias)
        return vjp(dy_)

    dys_direct, dtglu, dza, dglub = rowwise("glu_bwd", glu_bwd_tile, [ys5, tglu, pa, dya], [sp["s5_glu_b"]],
                                            [(s5w, F32), (s5w, BF16), (s5w, BF16)], [(1, s5w)], tr=_tile(rows, 256, 16))
    gs["s5_glu_b"] = dglub
    dgb = mm(dtglu, bw["glu"], "nt", "mm_dgb")
    gbig["glu"] = mm(gb, dtglu, "tn", "mm_dglu", out_dtype=BF16)

    def gelu_bwd_tile(i, ys, dg_, direct):
        _, vjp = jax.vjp(_gelu, ys)
        return vjp(dg_)[0] + direct

    dys5 = rowwise("gelu_bwd", gelu_bwd_tile, [ys5, dgb, dys_direct], [], [(s5w, F32)], tr=_tile(rows, 256, 16))[0]
    du, dbre, dbim, dcre, dcim, dare, daie, dd5 = s5_bwd(pa, dys5, s5st, bre, bim, cre, cim, are, aie, sp["s5_d"], nb, "s5_bwd")
    gs["s5_d"] = dd5
    dbbr, dbbi, dcr, dci, dar, dai = _s5_contract(dbre, dbim, dcre, dcim, dare, daie, sg, spn, shh)
    gs["s5_c_re"], gs["s5_c_im"] = dcr, dci
    (gs["s5_lambda_re"], gs["s5_lambda_im"], dlogdt, gs["s5_b_re"], gs["s5_b_im"]) = s5_disc_vjp((dar, dai, dbbr, dbbi))
    gs["s5_log_dt"] = dlogdt.reshape(1, -1)
    dpa = jnp.concatenate([du, dza], axis=1)
    gbig["W0a"] = mm(n0, dpa, "tn", "mm_dW0a", out_dtype=BF16)
    gbig["W0xb"] = mm(n0, dxb, "tn", "mm_dW0xb", out_dtype=BF16)
    gbig["W0zb"] = mm(n0, dzb, "tn", "mm_dW0zb", out_dtype=BF16)
    if early_grads is not None:
        early_grads({n: gbig[n] for n in _EARLY})
    dn0 = mm(dpa, bw["W0a"], "nt", "mm_dn0a")
    dn0 = mm(dxb, bw["W0xb"], "nt", "mm_dn0xb", resid=dn0)
    dn0 = mm(dzb, bw["W0zb"], "nt", "mm_dn0zb", resid=dn0)
    dh0, _, dg0 = norm_bwd(h0, sp["ab_norm"], dn0, dh1, "norm0_bwd")
    gs["ab_norm"] = dg0
    dh0 = dh0.reshape(nb, tp, d)
    gs["meta_tokens"] = jnp.sum(dh0[:, :N_META], axis=0)
    return loss_acc[0, 0], dh0, gbig, gs


N_DEV = 8
N_CHIP = 4
N_PEER_CHIPS = N_CHIP - 1
MESH = pl.DeviceIdType.MESH
_HBM = pl.BlockSpec(memory_space=pltpu.HBM)


def _place():
    x, y, c = lax.axis_index("x"), lax.axis_index("y"), lax.axis_index("c")
    return x, y, c, [(1 - x, y), (x, 1 - y), (1 - x, 1 - y)]


def all_gather8(v, name):
    m_per, n = v.shape

    def body(x_ref, out_ref, send_sems, recv_sems, local_sem):
        x, y, c, chips = _place()
        me, sibling = (x, y, c), (x, y, 1 - c)

        def rows(px, py, pc):
            return out_ref.at[pl.ds((4 * px + 2 * py + pc) * m_per, m_per), :]

        def copy(kk, block, to, src=None):
            return pltpu.make_async_remote_copy(
                src_ref=rows(*block) if src is None else src, dst_ref=rows(*block), send_sem=send_sems.at[kk],
                recv_sem=recv_sems.at[kk], device_id=to, device_id_type=MESH)

        mine = pltpu.make_async_copy(x_ref, rows(*me), local_sem)
        mine.start()
        first = [copy(0, me, sibling, src=x_ref)]
        first += [copy(1 + j, me, (*chip, c), src=x_ref) for j, chip in enumerate(chips)]
        for cp in first:
            cp.start()
        passed = [copy(4 + j, (*chip, c), sibling) for j, chip in enumerate(chips)]
        for j, chip in enumerate(chips):
            copy(1 + j, (*chip, c), me).wait_recv()
            passed[j].start()
        copy(0, sibling, me).wait_recv()
        for j, chip in enumerate(chips):
            copy(4 + j, (*chip, 1 - c), me).wait_recv()
        for cp in first + passed:
            cp.wait_send()
        mine.wait()

    return pl.pallas_call(
        body, name=name, out_shape=jax.ShapeDtypeStruct((N_DEV * m_per, n), v.dtype),
        in_specs=[pl.BlockSpec(memory_space=pltpu.VMEM)], out_specs=pl.BlockSpec(memory_space=pltpu.VMEM),
        scratch_shapes=[pltpu.SemaphoreType.DMA((7,)), pltpu.SemaphoreType.DMA((7,)), pltpu.SemaphoreType.DMA],
        compiler_params=pltpu.CompilerParams(vmem_limit_bytes=VMEM_LIMIT))(v)


def gather_chips(vs, name):
    na = len(vs)

    def body(*refs):
        x_refs, out_refs = refs[:na], refs[na:2 * na]
        send_sems, recv_sems, local_sems = refs[2 * na:]
        x, y, c, chips = _place()
        k = 2 * x + y
        sibling = (x, y, 1 - c)

        def copy(i, kk, src, chip_k, half, to):
            return pltpu.make_async_remote_copy(
                src_ref=src, dst_ref=out_refs[i].at[chip_k, half], send_sem=send_sems.at[6 * i + kk],
                recv_sem=recv_sems.at[6 * i + kk], device_id=to, device_id_type=MESH)

        mine = [pltpu.make_async_copy(x_refs[i], out_refs[i].at[k], local_sems.at[i]) for i in range(na)]
        for cp in mine:
            cp.start()
        first = [copy(i, j, x_refs[i].at[c], k, c, (*chip, c)) for j, chip in enumerate(chips) for i in range(na)]
        for cp in first:
            cp.start()
        passed = []
        for j, (cx, cy) in enumerate(chips):
            kj = 2 * cx + cy
            for i in range(na):
                copy(i, j, out_refs[i].at[kj, c], kj, c, (cx, cy, c)).wait_recv()
                fwd = copy(i, 3 + j, out_refs[i].at[kj, c], kj, c, sibling)
                fwd.start()
                passed.append(fwd)
        for j, (cx, cy) in enumerate(chips):
            kj = 2 * cx + cy
            for i in range(na):
                copy(i, 3 + j, out_refs[i].at[kj, 1 - c], kj, 1 - c, sibling).wait_recv()
        for cp in first + passed:
            cp.wait_send()
        for cp in mine:
            cp.wait()

    return pl.pallas_call(
        body, name=name, out_shape=[jax.ShapeDtypeStruct((N_CHIP,) + v.shape, v.dtype) for v in vs],
        in_specs=[_HBM] * na, out_specs=[_HBM] * na,
        scratch_shapes=[pltpu.SemaphoreType.DMA((6 * na,)), pltpu.SemaphoreType.DMA((6 * na,)),
                        pltpu.SemaphoreType.DMA((na,))])(*vs)


def swap_halves(gs_, name):
    na = len(gs_)

    def body(*refs):
        g_refs, out_refs = refs[:na], refs[na:2 * na]
        send_sems, recv_sems = refs[2 * na:]
        x, y, c, _ = _place()
        cps = [pltpu.make_async_remote_copy(
            src_ref=g_refs[i].at[kk, 1 - c], dst_ref=out_refs[i].at[kk], send_sem=send_sems.at[N_CHIP * i + kk],
            recv_sem=recv_sems.at[N_CHIP * i + kk], device_id=(x, y, 1 - c), device_id_type=MESH)
            for i in range(na) for kk in range(N_CHIP)]
        for cp in cps:
            cp.start()
        for cp in cps:
            cp.wait()

    return pl.pallas_call(
        body, name=name, out_shape=[jax.ShapeDtypeStruct((N_CHIP,) + g.shape[2:], g.dtype) for g in gs_],
        in_specs=[_HBM] * na, out_specs=[_HBM] * na,
        scratch_shapes=[pltpu.SemaphoreType.DMA((N_CHIP * na,)), pltpu.SemaphoreType.DMA((N_CHIP * na,))])(*gs_)


def add_halves(g, other, core, name):
    _, _, m, n = g.shape
    tr = _tile(m, 256, 16)

    def body(core_ref, g_ref, o_ref, out_ref):
        out_ref[...] = (g_ref[...].astype(F32) + o_ref[...].astype(F32)).astype(out_ref.dtype)

    grid_spec = pltpu.PrefetchScalarGridSpec(
        num_scalar_prefetch=1, grid=(N_CHIP, m // tr),
        in_specs=[pl.BlockSpec((None, None, tr, n), lambda kk, i, core_ref: (kk, core_ref[0], i, 0)),
                  pl.BlockSpec((None, tr, n), lambda kk, i, core_ref: (kk, i, 0))],
        out_specs=pl.BlockSpec((None, tr, n), lambda kk, i, core_ref: (kk, i, 0)))
    return pl.pallas_call(body, name=name, grid_spec=grid_spec, out_shape=jax.ShapeDtypeStruct((N_CHIP, m, n), g.dtype),
                          compiler_params=_params(("arbitrary", "arbitrary")))(core.reshape(1).astype(jnp.int32), g, other)


def sequencer_exchange(srcs, scatter, collective_id, name):
    na = len(srcs)
    per = 2 * N_PEER_CHIPS + (1 if scatter else 0)
    hbm = pltpu.MemorySpace.HBM
    src_refs = [jax.new_ref(a, memory_space=hbm) for a in srcs]
    out_refs = [jax.empty_ref(jax.ShapeDtypeStruct((N_CHIP, 2) + a.shape[1:], a.dtype), memory_space=hbm) for a in srcs]

    @pl.kernel(mesh=plsc.ScalarSubcoreMesh(axis_name="seq", num_cores=1), name=name,
               scratch_types=(pltpu.SemaphoreType.DMA((per * na,)), pltpu.SemaphoreType.DMA((per * na,)),
                              pltpu.SemaphoreType.DMA((na,))),
               compiler_params=pltpu.CompilerParams(collective_id=collective_id))
    def launch(send_sems, recv_sems, local_sems):
        x, y, c, chips = _place()
        k = 2 * x + y
        sibling = (x, y, 1 - c)
        barrier = pltpu.get_barrier_semaphore()
        for cx, cy in chips:
            pl.semaphore_signal(barrier, inc=1, device_id=(cx, cy, c), device_id_type=MESH)
        pl.semaphore_signal(barrier, inc=1, device_id=sibling, device_id_type=MESH)
        pl.semaphore_wait(barrier, N_CHIP)

        def copy(i, kk, src, chip_k, half, to):
            return pltpu.make_async_remote_copy(
                src_ref=src, dst_ref=out_refs[i].at[chip_k, half], send_sem=send_sems.at[per * i + kk],
                recv_sem=recv_sems.at[per * i + kk], device_id=to, device_id_type=MESH)

        if scatter:
            mine = [pltpu.make_async_copy(src_refs[i].at[k], out_refs[i].at[k, c], local_sems.at[i]) for i in range(na)]
        else:
            mine = [pltpu.make_async_copy(src_refs[i], out_refs[i].at[k], local_sems.at[i]) for i in range(na)]
        for cp in mine:
            cp.start()
        first = []
        for j, (cx, cy) in enumerate(chips):
            for i in range(na):
                src = src_refs[i].at[2 * cx + cy] if scatter else src_refs[i].at[c]
                first.append(copy(i, j, src, k, c, (cx, cy, c)))
        if scatter:
            first += [copy(i, 2 * N_PEER_CHIPS, src_refs[i].at[k], k, c, sibling) for i in range(na)]
        for cp in first:
            cp.start()
        passed = []
        for j, (cx, cy) in enumerate(chips):
            kj = 2 * cx + cy
            for i in range(na):
                copy(i, j, out_refs[i].at[kj, c], kj, c, (cx, cy, c)).wait_recv()
                fwd = copy(i, N_PEER_CHIPS + j, out_refs[i].at[kj, c], kj, c, sibling)
                fwd.start()
                passed.append(fwd)
        if scatter:
            for i in range(na):
                copy(i, 2 * N_PEER_CHIPS, out_refs[i].at[k, 1 - c], k, 1 - c, sibling).wait_recv()
        for j, (cx, cy) in enumerate(chips):
            kj = 2 * cx + cy
            for i in range(na):
                copy(i, N_PEER_CHIPS + j, out_refs[i].at[kj, 1 - c], kj, 1 - c, sibling).wait_recv()
        for cp in first + passed:
            cp.wait_send()
        for cp in mine:
            cp.wait()

    launch()
    return [r[...] for r in out_refs]


PACK_LANES = 512


def _pack(arrs, dtype, lanes, row_align):
    flat = jnp.concatenate([a.reshape(-1).astype(dtype) for a in arrs])
    unit = lanes * row_align
    total = -(-flat.shape[0] // unit) * unit
    return jnp.pad(flat, (0, total - flat.shape[0])).reshape(total // lanes, lanes)


def _unpack(flat, shapes):
    flat = flat.reshape(-1)
    out, off = [], 0
    for s in shapes:
        n = math.prod(s)
        out.append(flat[off:off + n].reshape(s))
        off += n
    return out


def _adam_tile(w, m, v, g):
    m2 = ADAM_B1 * m + (1.0 - ADAM_B1) * g
    v2 = ADAM_B2 * v + (1.0 - ADAM_B2) * (g * g)
    m_hat = m2 / (1.0 - ADAM_B1 ** ADAM_STEP)
    v_hat = v2 / (1.0 - ADAM_B2 ** ADAM_STEP)
    delta = -ADAM_LR * (m_hat / (jnp.sqrt(v_hat) + ADAM_EPS) + ADAM_WD * w)
    return delta, m2, v2


def adam_big(w, m, v, pieces, name):
    _, r, c = w.shape
    tr = _tile(r, 128, 16)

    def body(w_ref, m_ref, v_ref, p0, p1, p2, p3, g_ref, d_ref, mo_ref, vo_ref):
        g = ((p0[...].astype(F32) + p1[...].astype(F32)) + p2[...].astype(F32)) + p3[...].astype(F32)
        delta, m2, v2 = _adam_tile(w_ref[...], m_ref[...], v_ref[...], g)
        g_ref[...] = g
        d_ref[...] = delta
        mo_ref[...] = m2
        vo_ref[...] = v2

    wspec = pl.BlockSpec((None, tr, c), lambda i: (0, i, 0))
    pspecs = [pl.BlockSpec((None, tr, c), functools.partial(lambda i, kk: (kk, i, 0), kk=kk)) for kk in range(N_CHIP)]
    return pl.pallas_call(
        body, name=name, grid=(r // tr,), in_specs=[wspec] * 3 + pspecs, out_specs=[wspec] * 4,
        out_shape=[jax.ShapeDtypeStruct(w.shape, F32)] * 4, compiler_params=_params(("parallel",)))(
            w, m, v, pieces, pieces, pieces, pieces)


_WEIGHTS = (
    ("meta_tokens", "small", 1), ("ab_norm", "small", None), ("ab_w_in", "big", 2), ("s5_lambda_re", "small", None),
    ("s5_lambda_im", "small", None), ("s5_log_dt", "small", None), ("s5_b_re", "small", None), ("s5_b_im", "small", None),
    ("s5_c_re", "small", None), ("s5_c_im", "small", None), ("s5_d", "small", None), ("s5_glu_w", "big", 1),
    ("s5_glu_b", "small", None), ("ml_conv_w", "small", 2), ("ml_conv_b", "small", None), ("ml_wq", "small", 1),
    ("ml_wk", "small", 1), ("ml_wv", "small", 1), ("ml_w_gate", "small", 1), ("ml_b_gate", "small", None),
    ("ml_norm", "small", None), ("ml_skip", "small", None), ("ab_w_out", "big", 1), ("ssd_norm", "small", 1),
    ("ssd_w_in", "big", 2), ("ssd_conv_w", "small", 2), ("ssd_conv_b", "small", 1), ("ssd_dt_bias", "small", None),
    ("ssd_a_log", "small", None), ("ssd_d", "small", None), ("ssd_gnorm", "small", 1), ("ssd_w_out", "big", 1),
    ("final_norm", "small", None),
)


def _squeeze(a):
    return a[0] if a.ndim >= 3 else a


def kernel(x, meta_tokens, ab_norm, ab_w_in, s5_lambda_re, s5_lambda_im, s5_log_dt, s5_b_re, s5_b_im, s5_c_re, s5_c_im, s5_d, s5_glu_w, s5_glu_b, ml_conv_w, ml_conv_b, ml_wq, ml_wk, ml_wv, ml_w_gate, ml_b_gate, ml_norm, ml_skip, ab_w_out, ssd_norm, ssd_w_in, ssd_conv_w, ssd_conv_b, ssd_dt_bias, ssd_a_log, ssd_d, ssd_gnorm, ssd_w_out, final_norm, loss_target, m_meta_tokens, m_ab_norm, m_ab_w_in, m_s5_lambda_re, m_s5_lambda_im, m_s5_log_dt, m_s5_b_re, m_s5_b_im, m_s5_c_re, m_s5_c_im, m_s5_d, m_s5_glu_w, m_s5_glu_b, m_ml_conv_w, m_ml_conv_b, m_ml_wq, m_ml_wk, m_ml_wv, m_ml_w_gate, m_ml_b_gate, m_ml_norm, m_ml_skip, m_ab_w_out, m_ssd_norm, m_ssd_w_in, m_ssd_conv_w, m_ssd_conv_b, m_ssd_dt_bias, m_ssd_a_log, m_ssd_d, m_ssd_gnorm, m_ssd_w_out, m_final_norm, v_meta_tokens, v_ab_norm, v_ab_w_in, v_s5_lambda_re, v_s5_lambda_im, v_s5_log_dt, v_s5_b_re, v_s5_b_im, v_s5_c_re, v_s5_c_im, v_s5_d, v_s5_glu_w, v_s5_glu_b, v_ml_conv_w, v_ml_conv_b, v_ml_wq, v_ml_wk, v_ml_wv, v_ml_w_gate, v_ml_b_gate, v_ml_norm, v_ml_skip, v_ab_w_out, v_ssd_norm, v_ssd_w_in, v_ssd_conv_w, v_ssd_conv_b, v_ssd_dt_bias, v_ssd_a_log, v_ssd_d, v_ssd_gnorm, v_ssd_w_out, v_final_norm):
    args = (meta_tokens, ab_norm, ab_w_in, s5_lambda_re, s5_lambda_im, s5_log_dt, s5_b_re, s5_b_im, s5_c_re, s5_c_im, s5_d, s5_glu_w, s5_glu_b, ml_conv_w, ml_conv_b, ml_wq, ml_wk, ml_wv, ml_w_gate, ml_b_gate, ml_norm, ml_skip, ab_w_out, ssd_norm, ssd_w_in, ssd_conv_w, ssd_conv_b, ssd_dt_bias, ssd_a_log, ssd_d, ssd_gnorm, ssd_w_out, final_norm)
    m_args = (m_meta_tokens, m_ab_norm, m_ab_w_in, m_s5_lambda_re, m_s5_lambda_im, m_s5_log_dt, m_s5_b_re, m_s5_b_im, m_s5_c_re, m_s5_c_im, m_s5_d, m_s5_glu_w, m_s5_glu_b, m_ml_conv_w, m_ml_conv_b, m_ml_wq, m_ml_wk, m_ml_wv, m_ml_w_gate, m_ml_b_gate, m_ml_norm, m_ml_skip, m_ab_w_out, m_ssd_norm, m_ssd_w_in, m_ssd_conv_w, m_ssd_conv_b, m_ssd_dt_bias, m_ssd_a_log, m_ssd_d, m_ssd_gnorm, m_ssd_w_out, m_final_norm)
    v_args = (v_meta_tokens, v_ab_norm, v_ab_w_in, v_s5_lambda_re, v_s5_lambda_im, v_s5_log_dt, v_s5_b_re, v_s5_b_im, v_s5_c_re, v_s5_c_im, v_s5_d, v_s5_glu_w, v_s5_glu_b, v_ml_conv_w, v_ml_conv_b, v_ml_wq, v_ml_wk, v_ml_wv, v_ml_w_gate, v_ml_b_gate, v_ml_norm, v_ml_skip, v_ab_w_out, v_ssd_norm, v_ssd_w_in, v_ssd_conv_w, v_ssd_conv_b, v_ssd_dt_bias, v_ssd_a_log, v_ssd_d, v_ssd_gnorm, v_ssd_w_out, v_final_norm)
    names = [w[0] for w in _WEIGHTS]
    kind = {w[0]: w[1] for w in _WEIGHTS}
    axis = {w[0]: w[2] for w in _WEIGHTS}
    w_loc = dict(zip(names, args))
    m_loc = dict(zip(names, m_args))
    v_loc = dict(zip(names, v_args))
    chip = 2 * lax.axis_index("x") + lax.axis_index("y")
    core = lax.axis_index("c")
    big = [n for n in names if kind[n] == "big"]
    small = [n for n in names if kind[n] == "small"]
    small_sh = [n for n in small if axis[n] is not None]

    def halves(a):
        return a.astype(BF16).reshape(2, a.shape[1] // 2, a.shape[2])

    def assemble(n, gth):
        shard = gth.reshape((N_CHIP,) + w_loc[n].shape[1:])
        if axis[n] == 1:
            return shard.reshape(-1, shard.shape[2])
        return jnp.concatenate([shard[kk] for kk in range(N_CHIP)], axis=1)

    early = ["ab_w_in", "s5_glu_w"]
    late = ["ab_w_out", "ssd_w_in", "ssd_w_out"]
    gathered = gather_chips([halves(w_loc[n]) for n in early], "gather_early_w")
    after_early = (gathered[0][0, 0, 0, 0] * 0).astype(BF16)
    late_gathered = sequencer_exchange([halves(w_loc[n]) + after_early for n in late], False, 1, "gather_late_w")
    w_in0_shards = gathered[0].reshape((N_CHIP,) + w_loc["ab_w_in"].shape[1:])
    glu_full = assemble("s5_glu_w", gathered[1])

    def columns(shards, lo, hi):
        cw = shards.shape[2]
        parts = [shards[kk][:, max(lo - kk * cw, 0):min(hi - kk * cw, cw)]
                 for kk in range(N_CHIP) if lo < (kk + 1) * cw and hi > kk * cw]
        return parts[0] if len(parts) == 1 else jnp.concatenate(parts, axis=1)

    small_sh_shapes = [w_loc[n].shape for n in small_sh]
    packed_s = _pack([w_loc[n] for n in small_sh], F32, LANES, SUBLANES)
    g8 = all_gather8(packed_s, "gather_small_w").reshape(N_CHIP, 2, -1)
    sp = {}
    for n in small:
        if axis[n] is None:
            sp[n] = _squeeze(w_loc[n])
    per_chip = [_unpack(g8[kk, 0], small_sh_shapes) for kk in range(N_CHIP)]
    for i, n in enumerate(small_sh):
        sp[n] = _squeeze(jnp.concatenate([per_chip[kk][i] for kk in range(N_CHIP)], axis=axis[n]))

    s5w = glu_full.shape[0]
    mlw = w_loc["ab_w_out"].shape[1] * N_CHIP - s5w
    inner = w_loc["ssd_w_out"].shape[1] * N_CHIP
    n_heads1 = sp["ssd_d"].shape[1]
    cdim = w_loc["ssd_w_in"].shape[2] * N_CHIP - inner - n_heads1
    bw = dict(W0a=columns(w_in0_shards, 0, 2 * s5w), W0xb=columns(w_in0_shards, 2 * s5w, 2 * s5w + mlw),
              W0zb=columns(w_in0_shards, 2 * s5w + mlw, 2 * (s5w + mlw)), glu=glu_full)

    def late_weights():
        fb = dict(zip(late, late_gathered))
        w_out0 = assemble("ab_w_out", fb["ab_w_out"])
        w1 = fb["ssd_w_in"].reshape((N_CHIP,) + w_loc["ssd_w_in"].shape[1:])
        return dict(Wo0a=w_out0[:s5w], Wo0b=w_out0[s5w:], W1z=columns(w1, 0, inner),
                    W1x=columns(w1, inner, inner + cdim), W1dt=_pad_lanes(columns(w1, inner + cdim, inner + cdim + n_heads1)),
                    Wo1=assemble("ssd_w_out", fb["ssd_w_out"]))

    def piece_columns(parts, lo, hi):
        out, off = [], 0
        for p in parts:
            a, b = max(lo - off, 0), min(hi - off, p.shape[1])
            if a < b:
                out.append(p[:, a:b])
            off += p.shape[1]
        return out[0] if len(out) == 1 else jnp.concatenate(out, axis=1)

    def chip_halves(n, parts):
        _, r, c_ = w_loc[n].shape
        if axis[n] == 1:
            whole = parts[0] if len(parts) == 1 else jnp.concatenate(parts, axis=0)
            return whole.reshape(N_CHIP, 2, r // 2, c_)
        shards = [piece_columns(parts, kk * c_, (kk + 1) * c_) for kk in range(N_CHIP)]
        return jnp.stack(shards).reshape(N_CHIP, 2, r // 2, c_)

    pieces = {}

    def reduce_group(ns, gfull, tag, collective_id):
        gps = [chip_halves(n, gfull[n]) for n in ns]
        from_sibling = swap_halves(gps, "swap_" + tag)
        partials = [add_halves(gp, oth, core, "add_" + n) for n, gp, oth in zip(ns, gps, from_sibling)]
        pieces.update(zip(ns, sequencer_exchange(partials, True, collective_id, "scatter_" + tag)))

    def late_grads(g):
        gfull = {"ab_w_out": [g["Wo0a"], g["Wo0b"]], "ssd_w_in": [g["W1z"], g["W1x"], g["W1dt"][:, :n_heads1]],
                 "ssd_w_out": [g["Wo1"]]}
        reduce_group(late, gfull, "late_g", 2)

    def early_grads(g):
        gfull = {"ab_w_in": [g["W0a"], g["W0xb"], g["W0zb"]], "s5_glu_w": [g["glu"]]}
        reduce_group(early, gfull, "early_g", 3)

    loss_local, dh0, gbig, gs = _local_step(x, loss_target, bw, sp, late_weights, late_grads, early_grads)
    grad_x = dh0[:, N_META:N_META + x.shape[1]]

    out_g, out_d, out_m, out_v = {}, {}, {}, {}
    small_full_shapes = [sp[n].shape for n in small] + [(1, 1)]
    packed_gs = _pack([gs[n] for n in small] + [loss_local.reshape(1, 1)], F32, LANES, SUBLANES)
    rows_s = packed_gs.shape[0]
    all_gs = sequencer_exchange([jnp.broadcast_to(packed_gs[None], (N_CHIP,) + packed_gs.shape)], True, 4,
                                "gather_small_g")[0].reshape(N_DEV, rows_s, LANES)
    blocks = [all_gs[i] for i in range(N_DEV)]

    for n in late + early:
        pcs = pieces[n].reshape((N_CHIP,) + w_loc[n].shape[1:])
        out_g[n], out_d[n], out_m[n], out_v[n] = adam_big(w_loc[n], m_loc[n], v_loc[n], pcs, "adam_" + n)

    def sum8(i, *b):
        acc = b[0]
        for t in b[1:]:
            acc = acc + t
        return acc

    gsum = rowwise("sum_small_g", sum8, blocks, [], [(LANES, F32)], tr=_tile(rows_s, 512, 8))[0]
    summed = _unpack(gsum, small_full_shapes)
    loss = summed[-1].reshape(())
    g_small = dict(zip(small, summed[:-1]))
    g_loc = {}
    for n in small:
        g = g_small[n].reshape((1,) + g_small[n].shape) if w_loc[n].ndim >= 3 else g_small[n]
        if axis[n] is not None:
            size = w_loc[n].shape[axis[n]]
            g = lax.dynamic_slice_in_dim(g, chip * size, size, axis=axis[n])
        g_loc[n] = g.reshape(w_loc[n].shape)
    loc_shapes = [w_loc[n].shape for n in small]
    pw, pm, pv, pg = (_pack([d[n] for n in small], F32, LANES, SUBLANES) for d in (w_loc, m_loc, v_loc, g_loc))
    dl, mn, vn = rowwise("adam_small", lambda i, a, b, c_, d_: _adam_tile(a, b, c_, d_), [pw, pm, pv, pg], [],
                         [(LANES, F32)] * 3, tr=_tile(pw.shape[0], 512, 8))
    for d_out, flat in ((out_d, dl), (out_m, mn), (out_v, vn)):
        for n, a in zip(small, _unpack(flat, loc_shapes)):
            d_out[n] = a
    for n in small:
        out_g[n] = g_loc[n]

    return (loss, grad_x, *[out_g[n] for n in names], *[out_d[n] for n in names], *[out_m[n] for n in names],
            *[out_v[n] for n in names])
```

```python
import functools
import math

import jax
import jax.numpy as jnp
from jax import lax
from jax.experimental import pallas as pl
from jax.experimental.pallas import tpu as pltpu
from jax.experimental.pallas import tpu_sc as plsc

F32 = jnp.float32
BF16 = jnp.bfloat16
HI = lax.Precision.HIGHEST

D_MODEL = 2048
SEQ = 2048
N_META = 16
CHUNK = 128
NORM_EPS = 1e-6
HEAD_NORM_EPS = 1e-5
S5_GROUP_SIZE = 16
S5_STATE = 64
MLSTM_HEADS = 8
QKV_BLOCK = 4
SSD_HEAD_DIM = 64
SSD_STATE = 128
SSD_HPG = 8
ADAM_LR = 0.001
ADAM_B1 = 0.9
ADAM_B2 = 0.999
ADAM_EPS = 1e-08
ADAM_WD = 0.01
ADAM_STEP = 10

LANES = 128
SUBLANES = 8
VMEM_LIMIT = 56 * 1024 * 1024
MM_OPERAND_VMEM = 34 * 1024 * 1024


def _sigmoid(x):
    return 0.5 * jnp.tanh(0.5 * x) + 0.5


@jax.custom_vjp
def _silu(x):
    return x * _sigmoid(x)


def _silu_fwd(x):
    return x * _sigmoid(x), x


def _silu_bwd(x, ct):
    s = _sigmoid(x)
    return (ct * (s * (1.0 + x * (1.0 - s))),)


_silu.defvjp(_silu_fwd, _silu_bwd)


def _softplus(x):
    return jnp.maximum(x, 0.0) + jnp.log(1.0 + jnp.exp(-jnp.abs(x)))


def _log_sigmoid(x):
    return jnp.minimum(x, 0.0) - jnp.log(1.0 + jnp.exp(-jnp.abs(x)))


def _gelu(x):
    return 0.5 * x * (1.0 + jnp.tanh(math.sqrt(2.0 / math.pi) * (x + 0.044715 * (x * x * x))))


def _dot(a, b, dims, precision=None):
    return lax.dot_general(a, b, (dims, ((), ())), preferred_element_type=F32, precision=precision)


_NN, _NT, _TN = ((1,), (0,)), ((1,), (1,)), ((0,), (0,))


def _bf16_dot(dims, da_rule, db_rule):
    @jax.custom_vjp
    def f(a, b):
        return _dot(a.astype(BF16), b.astype(BF16), dims)

    def fwd(a, b):
        ab, bb = a.astype(BF16), b.astype(BF16)
        return _dot(ab, bb, dims), (ab, bb, jnp.zeros((), a.dtype), jnp.zeros((), b.dtype))

    def bwd(res, ct):
        ab, bb, a_like, b_like = res
        cb = ct.astype(BF16)
        return da_rule(ab, bb, cb).astype(a_like.dtype), db_rule(ab, bb, cb).astype(b_like.dtype)

    f.defvjp(fwd, bwd)
    return f


_dot_nn = _bf16_dot(_NN, lambda a, b, c: _dot(c, b, _NT), lambda a, b, c: _dot(a, c, _TN))
_dot_nt = _bf16_dot(_NT, lambda a, b, c: _dot(c, b, _NN), lambda a, b, c: _dot(c, a, _TN))
_dot_tn = _bf16_dot(_TN, lambda a, b, c: _dot(b, c, _NT), lambda a, b, c: _dot(a, c, _NN))


def _lane_pick(a, idx):
    sel = (lax.broadcasted_iota(jnp.int32, (1, a.shape[1]), 1) == idx).astype(a.dtype)
    return jnp.sum(a * sel, axis=1, keepdims=True)


def _row_pick(a, idx):
    sel = (lax.broadcasted_iota(jnp.int32, (a.shape[0], 1), 0) == idx).astype(a.dtype)
    return jnp.sum(a * sel, axis=0, keepdims=True)


def _tri(n, upper=False):
    r = lax.broadcasted_iota(jnp.int32, (n, n), 0)
    c = lax.broadcasted_iota(jnp.int32, (n, n), 1)
    return ((r <= c) if upper else (r >= c)).astype(F32)


def _tile(n, target, align):
    if n <= target:
        return n
    t = (target // align) * align
    while t >= align:
        if n % t == 0:
            return t
        t -= align
    return n


def _params(sem=None):
    return pltpu.CompilerParams(dimension_semantics=sem, vmem_limit_bytes=VMEM_LIMIT)


def mm(a, b, mode, name, resid=None, out_dtype=F32):
    if mode == "nn":
        (m, k), (k2, n) = a.shape, b.shape
    elif mode == "nt":
        (m, k), (n, k2) = a.shape, b.shape
    else:
        (k, m), (k2, n) = a.shape, b.shape
    assert k == k2, (a.shape, b.shape, mode)
    a_sz, b_sz = a.dtype.itemsize, b.dtype.itemsize
    if mode == "tn":
        tm, tn = _tile(m, 1024, LANES), _tile(n, 1024, LANES)
        tk = _tile(k, MM_OPERAND_VMEM // (2 * (tm * a_sz + tn * b_sz)), 16)
    else:
        tm, tn = _tile(m, 1088, 16), _tile(n, 512, LANES)
        tk = _tile(k, MM_OPERAND_VMEM // (2 * (tm * a_sz + tn * b_sz)), LANES)
    nk = k // tk
    dims = {"nn": ((1,), (0,)), "nt": ((1,), (1,)), "tn": ((0,), (0,))}[mode]
    has_resid = resid is not None

    def body(*refs):
        if has_resid:
            a_ref, b_ref, r_ref, o_ref = refs[:4]
        else:
            a_ref, b_ref, o_ref = refs[:3]
        part = _dot(a_ref[...].astype(BF16), b_ref[...].astype(BF16), dims)

        def finish(res):
            if has_resid:
                res = res + r_ref[...].astype(F32)
            o_ref[...] = res.astype(o_ref.dtype)

        if nk == 1:
            finish(part)
            return
        acc_ref = refs[-1]
        kk = pl.program_id(2)

        @pl.when(kk == 0)
        def _():
            acc_ref[...] = part

        @pl.when(jnp.logical_and(kk > 0, kk < nk - 1))
        def _():
            acc_ref[...] += part

        @pl.when(kk == nk - 1)
        def _():
            finish(acc_ref[...] + part)

    if mode == "tn":
        a_spec = pl.BlockSpec((tk, tm), lambda i, j, kk: (kk, i))
    else:
        a_spec = pl.BlockSpec((tm, tk), lambda i, j, kk: (i, kk))
    if mode == "nt":
        b_spec = pl.BlockSpec((tn, tk), lambda i, j, kk: (j, kk))
    else:
        b_spec = pl.BlockSpec((tk, tn), lambda i, j, kk: (kk, j))
    o_spec = pl.BlockSpec((tm, tn), lambda i, j, kk: (i, j))
    in_specs = [a_spec, b_spec] + ([o_spec] if has_resid else [])
    args = (a, b) + ((resid,) if has_resid else ())
    return pl.pallas_call(
        body, name=name, grid=(m // tm, n // tn, nk), in_specs=in_specs, out_specs=o_spec,
        out_shape=jax.ShapeDtypeStruct((m, n), out_dtype), scratch_shapes=[pltpu.VMEM((tm, tn), F32)] if nk > 1 else [],
        compiler_params=_params(("parallel", "parallel", "arbitrary")))(*args)


def rowwise(name, f, rows, params, outs, accs=(), tr=128):
    n_rows = rows[0].shape[0]
    assert n_rows % tr == 0
    n_r, n_p, n_o, n_a = len(rows), len(params), len(outs), len(accs)

    def body(*refs):
        i = pl.program_id(0)
        r_vals = [r[...] for r in refs[:n_r]]
        p_vals = [r[...] for r in refs[n_r:n_r + n_p]]
        o_refs = refs[n_r + n_p:n_r + n_p + n_o]
        a_refs = refs[n_r + n_p + n_o:]
        res = f(i, *r_vals, *p_vals)
        if not isinstance(res, (tuple, list)):
            res = (res,)
        assert len(res) == n_o + n_a, (name, len(res))
        for o_ref, val in zip(o_refs, res[:n_o]):
            o_ref[...] = val.astype(o_ref.dtype)
        if n_a:
            @pl.when(i == 0)
            def _():
                for a_ref in a_refs:
                    a_ref[...] = jnp.zeros_like(a_ref)

            for a_ref, val in zip(a_refs, res[n_o:]):
                a_ref[...] += val.astype(F32)

    in_specs = [pl.BlockSpec((tr, r.shape[1]), lambda i: (i, 0)) for r in rows]
    in_specs += [pl.BlockSpec(p.shape, lambda i: (0, 0)) for p in params]
    out_specs = [pl.BlockSpec((tr, w), lambda i: (i, 0)) for w, _ in outs]
    out_specs += [pl.BlockSpec(s, lambda i: (0, 0)) for s in accs]
    out_shape = [jax.ShapeDtypeStruct((n_rows, w), dt) for w, dt in outs]
    out_shape += [jax.ShapeDtypeStruct(s, F32) for s in accs]
    res = pl.pallas_call(
        body, name=name, grid=(n_rows // tr,), in_specs=in_specs, out_specs=out_specs, out_shape=out_shape,
        compiler_params=_params(("arbitrary",)))(*rows, *params)
    return res


def _rms(x, g, eps=NORM_EPS):
    return x * lax.rsqrt(jnp.mean(x * x, axis=-1, keepdims=True) + eps) * g


def norm_fwd(x, g, name):
    return rowwise(name, lambda i, xb, gb: _rms(xb, gb), [x], [g], [(x.shape[1], BF16)], tr=_tile(x.shape[0], 256, 16))[0]


def norm_bwd(x, g, dn, resid, name):
    def f(i, xb, dnb, rb, gb):
        _, vjp = jax.vjp(_rms, xb, gb)
        dx, dg = vjp(dnb)
        return dx + rb, dx + rb, dg

    return rowwise(name, f, [x, dn, resid], [g], [(x.shape[1], F32), (x.shape[1], BF16)], [g.shape],
                   tr=_tile(x.shape[0], 256, 16))


def conv_fwd(x, w, b, nb, name):
    rows, width = x.shape
    nc = rows // nb // CHUNK
    tw = _tile(width, 1024, LANES)
    ksz = w.shape[0]

    def body(x_ref, w_ref, b_ref, o_ref, ext_ref):
        c = pl.program_id(2)

        @pl.when(c == 0)
        def _():
            ext_ref[0:SUBLANES, :] = jnp.zeros((SUBLANES, tw), F32)

        taps = [w_ref[j:j + 1, :] for j in range(ksz)]
        bias = b_ref[...]
        row = lax.broadcasted_iota(jnp.int32, (SUBLANES, tw), 0)
        prev_rot = [pltpu.roll(ext_ref[0:SUBLANES, :], k, 0) for k in range(1, ksz)]
        for s in range(CHUNK // SUBLANES):
            r0 = s * SUBLANES
            cur = x_ref[r0:r0 + SUBLANES, :]
            cur_rot = [pltpu.roll(cur, k, 0) for k in range(1, ksz)]
            acc = bias + taps[ksz - 1] * cur
            for k in range(1, ksz):
                acc = acc + taps[ksz - 1 - k] * jnp.where(row >= k, cur_rot[k - 1], prev_rot[k - 1])
            o_ref[r0:r0 + SUBLANES, :] = acc
            prev_rot = cur_rot
        ext_ref[0:SUBLANES, :] = x_ref[CHUNK - SUBLANES:CHUNK, :]

    return pl.pallas_call(
        body, name=name, grid=(width // tw, nb, nc),
        in_specs=[pl.BlockSpec((CHUNK, tw), lambda j, bb, c: (bb * nc + c, j)),
                  pl.BlockSpec((ksz, tw), lambda j, bb, c: (0, j)),
                  pl.BlockSpec((1, tw), lambda j, bb, c: (0, j))],
        out_specs=pl.BlockSpec((CHUNK, tw), lambda j, bb, c: (bb * nc + c, j)),
        out_shape=jax.ShapeDtypeStruct((rows, width), F32),
        scratch_shapes=[pltpu.VMEM((2 * SUBLANES, tw), F32)],
        compiler_params=_params(("arbitrary", "arbitrary", "arbitrary")))(x, w, b)


def conv_bwd(dc, x, w, nb, name, resid=None, dx_dtype=BF16):
    rows, width = x.shape
    nc = rows // nb // CHUNK
    tw = _tile(width, 1024, LANES)
    ksz = w.shape[0]
    per = CHUNK // SUBLANES
    has_resid = resid is not None

    def body(*refs):
        if has_resid:
            dc_ref, x_ref, halo_ref, w_ref, r_ref, dx_ref, dw_ref, db_ref, extd_ref, extx_ref = refs
        else:
            dc_ref, x_ref, halo_ref, w_ref, dx_ref, dw_ref, db_ref, extd_ref, extx_ref = refs
        bb = pl.program_id(1)
        step = pl.program_id(2)
        c = nc - 1 - step

        @pl.when(jnp.logical_and(bb == 0, step == 0))
        def _():
            dw_ref[...] = jnp.zeros_like(dw_ref)
            db_ref[...] = jnp.zeros_like(db_ref)

        @pl.when(step == 0)
        def _():
            extd_ref[SUBLANES:2 * SUBLANES, :] = jnp.zeros((SUBLANES, tw), F32)

        nstrip = CHUNK // SUBLANES
        taps = [w_ref[j:j + 1, :] for j in range(ksz)]
        row = lax.broadcasted_iota(jnp.int32, (SUBLANES, tw), 0)
        x_prev_rot = [pltpu.roll(jnp.where(c == 0, 0.0, halo_ref[...]), k, 0) for k in range(1, ksz)]
        dcs = dc_ref[0:SUBLANES, :]
        dc_rot = [pltpu.roll(dcs, SUBLANES - k, 0) for k in range(1, ksz)]
        for s in range(nstrip):
            r0 = s * SUBLANES
            nxt = extd_ref[SUBLANES:2 * SUBLANES, :] if s == nstrip - 1 else dc_ref[r0 + SUBLANES:r0 + 2 * SUBLANES, :]
            nxt_rot = [pltpu.roll(nxt, SUBLANES - k, 0) for k in range(1, ksz)]
            xc = x_ref[r0:r0 + SUBLANES, :]
            x_rot = [pltpu.roll(xc, k, 0) for k in range(1, ksz)]
            dx = r_ref[r0:r0 + SUBLANES, :].astype(F32) if has_resid else jnp.zeros((SUBLANES, tw), F32)
            dx = dx + taps[ksz - 1] * dcs
            dw_ref[(ksz - 1) * SUBLANES:ksz * SUBLANES, :] += dcs * xc
            for k in range(1, ksz):
                j = ksz - 1 - k
                dx = dx + taps[j] * jnp.where(row < SUBLANES - k, dc_rot[k - 1], nxt_rot[k - 1])
                dw_ref[j * SUBLANES:(j + 1) * SUBLANES, :] += dcs * jnp.where(row >= k, x_rot[k - 1], x_prev_rot[k - 1])
            if s % 2 == 0:
                held = dx
            else:
                dx_ref[r0 - SUBLANES:r0 + SUBLANES, :] = jnp.concatenate([held, dx], axis=0).astype(dx_ref.dtype)
            db_ref[...] += dcs
            dcs, dc_rot, x_prev_rot = nxt, nxt_rot, x_rot
        extd_ref[SUBLANES:2 * SUBLANES, :] = dc_ref[0:SUBLANES, :]

    def blk(j, bb, step):
        return (bb * nc + nc - 1 - step, j)

    def halo(j, bb, step):
        return (jnp.maximum((bb * nc + nc - 1 - step) * per - 1, 0), j)

    in_specs = [pl.BlockSpec((CHUNK, tw), blk), pl.BlockSpec((CHUNK, tw), blk), pl.BlockSpec((SUBLANES, tw), halo),
                pl.BlockSpec((ksz, tw), lambda j, bb, step: (0, j))]
    args = [dc, x, x, w]
    if has_resid:
        in_specs.append(pl.BlockSpec((CHUNK, tw), blk))
        args.append(resid)
    dx, dw_raw, db_raw = pl.pallas_call(
        body, name=name, grid=(width // tw, nb, nc), in_specs=in_specs,
        out_specs=[pl.BlockSpec((CHUNK, tw), blk), pl.BlockSpec((ksz * SUBLANES, tw), lambda j, bb, step: (0, j)),
                   pl.BlockSpec((SUBLANES, tw), lambda j, bb, step: (0, j))],
        out_shape=[jax.ShapeDtypeStruct((rows, width), dx_dtype), jax.ShapeDtypeStruct((ksz * SUBLANES, width), F32),
                   jax.ShapeDtypeStruct((SUBLANES, width), F32)],
        scratch_shapes=[pltpu.VMEM((2 * SUBLANES, tw), F32), pltpu.VMEM((2 * SUBLANES, tw), F32)],
        compiler_params=_params(("arbitrary", "arbitrary", "arbitrary")))(*args)
    return dx, dw_raw.reshape(ksz, SUBLANES, width).sum(axis=1), db_raw.sum(axis=0, keepdims=True)


S5_Q = 4


def _s5_fill_bu(u, bre_ref, bim_ref, xr_ref, xi_ref, ns):
    for s in range(ns):
        ub = u[:, s * LANES:(s + 1) * LANES].astype(BF16)
        bur = _dot(ub, bre_ref[s], ((1,), (0,)))
        bui = _dot(ub, bim_ref[s], ((1,), (0,)))
        for q in range(S5_Q):
            xr_ref[q, pl.ds(s, CHUNK, stride=ns), :] = bur[:, q * LANES:(q + 1) * LANES]
            xi_ref[q, pl.ds(s, CHUNK, stride=ns), :] = bui[:, q * LANES:(q + 1) * LANES]


def _s5_scan(xr_ref, xi_ref, ar_ref, ai_ref, st_ref, ns):
    ar = [ar_ref[q] for q in range(S5_Q)]
    ai = [ai_ref[q] for q in range(S5_Q)]

    def step(t, carry):
        rows = pl.ds(pl.multiple_of(t * ns, ns), ns)
        out = []
        for q in range(S5_Q):
            pr, pi_ = carry[2 * q], carry[2 * q + 1]
            nr = ar[q] * pr - ai[q] * pi_ + xr_ref[q, rows, :]
            ni = ar[q] * pi_ + ai[q] * pr + xi_ref[q, rows, :]
            xr_ref[q, rows, :] = nr
            xi_ref[q, rows, :] = ni
            out += [nr, ni]
        return tuple(out)

    init = []
    for q in range(S5_Q):
        init += [st_ref[0, q], st_ref[1, q]]
    fin = lax.fori_loop(0, CHUNK, step, tuple(init), unroll=2)
    for q in range(S5_Q):
        st_ref[0, q] = fin[2 * q]
        st_ref[1, q] = fin[2 * q + 1]


def s5_fwd(pa, bre, bim, cre, cim, ar, ai, dvec, nb, name):
    rows = pa.shape[0]
    width = pa.shape[1] // 2
    ns = width // LANES
    nc = rows // nb // CHUNK

    def body(u_ref, bre_ref, bim_ref, cre_ref, cim_ref, ar_ref, ai_ref, d_ref, y_ref, g_ref, so_ref, xr_ref, xi_ref, st_ref):
        c = pl.program_id(1)

        @pl.when(c == 0)
        def _():
            st_ref[...] = jnp.zeros_like(st_ref)

        so_ref[...] = st_ref[...]
        u = u_ref[...]
        _s5_fill_bu(u, bre_ref, bim_ref, xr_ref, xi_ref, ns)
        _s5_scan(xr_ref, xi_ref, ar_ref, ai_ref, st_ref, ns)
        for s in range(ns):
            acc = jnp.zeros((CHUNK, LANES), F32)
            for q in range(S5_Q):
                xr = xr_ref[q, pl.ds(s, CHUNK, stride=ns), :].astype(BF16)
                xi = xi_ref[q, pl.ds(s, CHUNK, stride=ns), :].astype(BF16)
                acc = acc + _dot(xr, cre_ref[s, q * LANES:(q + 1) * LANES, :], ((1,), (0,)))
                acc = acc - _dot(xi, cim_ref[s, q * LANES:(q + 1) * LANES, :], ((1,), (0,)))
            cols = slice(s * LANES, (s + 1) * LANES)
            y = acc + d_ref[:, cols] * u[:, cols]
            y_ref[:, cols] = y
            g_ref[:, cols] = _gelu(y).astype(BF16)

    whole3 = lambda a: pl.BlockSpec(a.shape, lambda b_, c: (0, 0, 0))
    return pl.pallas_call(
        body, name=name, grid=(nb, nc),
        in_specs=[pl.BlockSpec((CHUNK, width), lambda b_, c: (b_ * nc + c, 0)), whole3(bre), whole3(bim), whole3(cre),
                  whole3(cim), whole3(ar), whole3(ai), pl.BlockSpec((1, width), lambda b_, c: (0, 0))],
        out_specs=[pl.BlockSpec((CHUNK, width), lambda b_, c: (b_ * nc + c, 0)),
                   pl.BlockSpec((CHUNK, width), lambda b_, c: (b_ * nc + c, 0)),
                   pl.BlockSpec((None, 2, S5_Q, ns, LANES), lambda b_, c: (b_ * nc + c, 0, 0, 0, 0))],
        out_shape=[jax.ShapeDtypeStruct((rows, width), F32), jax.ShapeDtypeStruct((rows, width), BF16),
                   jax.ShapeDtypeStruct((nb * nc, 2, S5_Q, ns, LANES), F32)],
        scratch_shapes=[pltpu.VMEM((S5_Q, CHUNK * ns, LANES), F32), pltpu.VMEM((S5_Q, CHUNK * ns, LANES), F32),
                        pltpu.VMEM((2, S5_Q, ns, LANES), F32)],
        compiler_params=_params(("arbitrary", "arbitrary")))(pa, bre, bim, cre, cim, ar, ai, dvec)


def s5_bwd(pa, dys, states, bre, bim, cre, cim, ar, ai, dvec, nb, name):
    rows = pa.shape[0]
    width = pa.shape[1] // 2
    ns = width // LANES
    nc = rows // nb // CHUNK

    def body(u_ref, dy_ref, sin_ref, bre_ref, bim_ref, cre_ref, cim_ref, ar_ref, ai_ref, d_ref,
             du_ref, dbre_ref, dbim_ref, dcre_ref, dcim_ref, dar_ref, dai_ref, dd_ref,
             xr_ref, xi_ref, lr_ref, li_ref, st_ref, lam_ref):
        bb = pl.program_id(0)
        step_i = pl.program_id(1)

        @pl.when(jnp.logical_and(bb == 0, step_i == 0))
        def _():
            for r in (dbre_ref, dbim_ref, dcre_ref, dcim_ref, dar_ref, dai_ref, dd_ref):
                r[...] = jnp.zeros_like(r)

        @pl.when(step_i == 0)
        def _():
            lam_ref[...] = jnp.zeros_like(lam_ref)

        u = u_ref[...]
        dy = dy_ref[...]
        st_ref[...] = sin_ref[...]
        _s5_fill_bu(u, bre_ref, bim_ref, xr_ref, xi_ref, ns)
        _s5_scan(xr_ref, xi_ref, ar_ref, ai_ref, st_ref, ns)
        dd_ref[...] += jnp.sum(dy * u, axis=0, keepdims=True)
        for s in range(ns):
            dyb = dy[:, s * LANES:(s + 1) * LANES].astype(BF16)
            gr = _dot(dyb, cre_ref[s], ((1,), (1,)))
            gi = -_dot(dyb, cim_ref[s], ((1,), (1,)))
            for q in range(S5_Q):
                lr_ref[q, pl.ds(s, CHUNK, stride=ns), :] = gr[:, q * LANES:(q + 1) * LANES]
                li_ref[q, pl.ds(s, CHUNK, stride=ns), :] = gi[:, q * LANES:(q + 1) * LANES]
                xr = xr_ref[q, pl.ds(s, CHUNK, stride=ns), :].astype(BF16)
                xi = xi_ref[q, pl.ds(s, CHUNK, stride=ns), :].astype(BF16)
                dcre_ref[s, q * LANES:(q + 1) * LANES, :] += _dot(xr, dyb, ((0,), (0,)))
                dcim_ref[s, q * LANES:(q + 1) * LANES, :] -= _dot(xi, dyb, ((0,), (0,)))
        ar = [ar_ref[q] for q in range(S5_Q)]
        ai = [ai_ref[q] for q in range(S5_Q)]

        def one(t_rows, p_r, p_i, carry):
            out = []
            for q in range(S5_Q):
                l_r, l_i, da_r, da_i = carry[4 * q:4 * q + 4]
                n_r = lr_ref[q, t_rows, :] + ar[q] * l_r + ai[q] * l_i
                n_i = li_ref[q, t_rows, :] + ar[q] * l_i - ai[q] * l_r
                lr_ref[q, t_rows, :] = n_r
                li_ref[q, t_rows, :] = n_i
                xpr, xpi = p_r(q), p_i(q)
                out += [n_r, n_i, da_r + n_r * xpr + n_i * xpi, da_i + n_i * xpr - n_r * xpi]
            return tuple(out)

        def step(k, carry):
            t = CHUNK - 1 - k
            t_rows = pl.ds(pl.multiple_of(t * ns, ns), ns)
            p_rows = pl.ds(pl.multiple_of((t - 1) * ns, ns), ns)
            return one(t_rows, lambda q: xr_ref[q, p_rows, :], lambda q: xi_ref[q, p_rows, :], carry)

        init = []
        zero = jnp.zeros((ns, LANES), F32)
        for q in range(S5_Q):
            init += [lam_ref[0, q], lam_ref[1, q], zero, zero]
        carry = lax.fori_loop(0, CHUNK - 1, step, tuple(init), unroll=2)
        carry = one(pl.ds(0, ns), lambda q: sin_ref[0, q], lambda q: sin_ref[1, q], carry)
        for q in range(S5_Q):
            lam_ref[0, q] = carry[4 * q]
            lam_ref[1, q] = carry[4 * q + 1]
            dar_ref[q] += carry[4 * q + 2]
            dai_ref[q] += carry[4 * q + 3]
        for s in range(ns):
            cols = slice(s * LANES, (s + 1) * LANES)
            ub = u[:, cols].astype(BF16)
            acc = d_ref[:, cols] * dy[:, cols]
            for q in range(S5_Q):
                qs = slice(q * LANES, (q + 1) * LANES)
                lr = lr_ref[q, pl.ds(s, CHUNK, stride=ns), :].astype(BF16)
                li = li_ref[q, pl.ds(s, CHUNK, stride=ns), :].astype(BF16)
                dbre_ref[s, :, qs] += _dot(ub, lr, ((0,), (0,)))
                dbim_ref[s, :, qs] += _dot(ub, li, ((0,), (0,)))
                acc = acc + _dot(lr, bre_ref[s, :, qs], ((1,), (1,))) + _dot(li, bim_ref[s, :, qs], ((1,), (1,)))
            du_ref[:, cols] = acc.astype(du_ref.dtype)

    whole3 = lambda a: pl.BlockSpec(a.shape, lambda b_, c: (0, 0, 0))
    rowblk = pl.BlockSpec((CHUNK, width), lambda b_, c: (b_ * nc + nc - 1 - c, 0))
    scr = pltpu.VMEM((S5_Q, CHUNK * ns, LANES), F32)
    return pl.pallas_call(
        body, name=name, grid=(nb, nc),
        in_specs=[rowblk, rowblk,
                  pl.BlockSpec((None, 2, S5_Q, ns, LANES), lambda b_, c: (b_ * nc + nc - 1 - c, 0, 0, 0, 0)),
                  whole3(bre), whole3(bim), whole3(cre), whole3(cim), whole3(ar), whole3(ai),
                  pl.BlockSpec((1, width), lambda b_, c: (0, 0))],
        out_specs=[rowblk, whole3(bre), whole3(bim), whole3(cre), whole3(cim), whole3(ar), whole3(ai),
                   pl.BlockSpec((1, width), lambda b_, c: (0, 0))],
        out_shape=[jax.ShapeDtypeStruct((rows, width), BF16), jax.ShapeDtypeStruct(bre.shape, F32),
                   jax.ShapeDtypeStruct(bim.shape, F32), jax.ShapeDtypeStruct(cre.shape, F32),
                   jax.ShapeDtypeStruct(cim.shape, F32), jax.ShapeDtypeStruct(ar.shape, F32),
                   jax.ShapeDtypeStruct(ai.shape, F32), jax.ShapeDtypeStruct((1, width), F32)],
        scratch_shapes=[scr, scr, scr, scr, pltpu.VMEM((2, S5_Q, ns, LANES), F32), pltpu.VMEM((2, S5_Q, ns, LANES), F32)],
        compiler_params=_params(("arbitrary", "arbitrary")))(pa, dys, states, bre, bim, cre, cim, ar, ai, dvec)


def _s5_discretize(lam_re, lam_im, log_dt, b_re, b_im):
    dt = jnp.exp(log_dt)[:, None]
    mag = jnp.exp(lam_re * dt)
    ar, ai = mag * jnp.cos(lam_im * dt), mag * jnp.sin(lam_im * dt)
    den = lam_re * lam_re + lam_im * lam_im
    qr = ((ar - 1.0) * lam_re + ai * lam_im) / den
    qi = (ai * lam_re - (ar - 1.0) * lam_im) / den
    bbr = qr[..., None] * b_re - qi[..., None] * b_im
    bbi = qr[..., None] * b_im + qi[..., None] * b_re
    return ar, ai, bbr, bbi


def _s5_expand(ar, ai, bbr, bbi, c_re, c_im):
    g, p, h = bbr.shape
    gps = LANES // h
    ns = g // gps
    eye = jnp.eye(gps, dtype=F32)

    def bexp(b):
        return jnp.einsum("sgph,gk->sghkp", b.reshape(ns, gps, p, h), eye).reshape(ns, gps * h, gps * p)

    def cexp(c):
        return jnp.einsum("sghp,gk->sgpkh", c.reshape(ns, gps, h, p), eye).reshape(ns, gps * p, gps * h)

    def aexp(a):
        return a.reshape(ns, S5_Q, LANES).transpose(1, 0, 2)

    return (bexp(bbr).astype(BF16), bexp(bbi).astype(BF16), cexp(c_re).astype(BF16), cexp(c_im).astype(BF16),
            aexp(ar), aexp(ai))


def _s5_contract(dbre, dbim, dcre, dcim, dar, dai, g, p, h):
    gps = LANES // h
    ns = g // gps
    eye = jnp.eye(gps, dtype=F32)
    bcon = lambda d: jnp.einsum("sghkp,gk->sgph", d.reshape(ns, gps, h, gps, p), eye).reshape(g, p, h)
    ccon = lambda d: jnp.einsum("sgpkh,gk->sghp", d.reshape(ns, gps, p, gps, h), eye).reshape(g, h, p)
    acon = lambda d: d.transpose(1, 0, 2).reshape(g, p)
    return bcon(dbre), bcon(dbim), ccon(dcre), ccon(dcim), acon(dar), acon(dai)


PROJ_BLOCK = 256


def _ml_proj_tile(cpre, xb, wq, wk, wv, gq, gk, gv):
    xc = _silu(cpre)
    q = _dot_nn(xc, wq)
    k = _dot_nn(xc, wk)
    v = _dot_nn(xb, wv)
    return q, k, v, _dot_nn(q, gq) + _dot_nn(k, gk) + _dot_nn(v, gv)


def ml_proj_fwd(cpre, xb, wq, wk, wv, gq, gk, gv, name):
    rows, width = cpre.shape
    pb = wq.shape[1]
    nblk = width // pb
    tr = _tile(rows, 1088, 16)

    def body(c_ref, x_ref, wq_ref, wk_ref, wv_ref, gq_ref, gk_ref, gv_ref, q_ref, k_ref, v_ref, g_ref):
        j = pl.program_id(1)
        q, k, v, g = _ml_proj_tile(c_ref[...], x_ref[...], wq_ref[...], wk_ref[...], wv_ref[...],
                                   gq_ref[...], gk_ref[...], gv_ref[...])
        q_ref[...] = q
        k_ref[...] = k
        v_ref[...] = v

        @pl.when(j == 0)
        def _():
            g_ref[...] = jnp.zeros_like(g_ref)

        g_ref[...] += g

    rb = pl.BlockSpec((tr, pb), lambda i, j: (i, j))
    wb = pl.BlockSpec((None, pb, pb), lambda i, j: (j, 0, 0))
    gwb = pl.BlockSpec((None, pb, LANES), lambda i, j: (j, 0, 0))
    return pl.pallas_call(
        body, name=name, grid=(rows // tr, nblk), in_specs=[rb, rb, wb, wb, wb, gwb, gwb, gwb],
        out_specs=[rb, rb, rb, pl.BlockSpec((tr, LANES), lambda i, j: (i, 0))],
        out_shape=[jax.ShapeDtypeStruct((rows, width), F32)] * 3 + [jax.ShapeDtypeStruct((rows, LANES), F32)],
        compiler_params=_params(("arbitrary", "arbitrary")))(cpre, xb, wq, wk, wv, gq, gk, gv)


def ml_proj_bwd(cpre, xb, wq, wk, wv, gq, gk, gv, dq, dk, dv, dg, dcp_extra, name):
    rows, width = cpre.shape
    pb = wq.shape[1]
    nblk = width // pb
    tr = _tile(rows, 1088, 16)

    def body(c_ref, x_ref, wq_ref, wk_ref, wv_ref, gq_ref, gk_ref, gv_ref, dq_ref, dk_ref, dv_ref, dg_ref, e_ref,
             dc_ref, dx_ref, *dw_refs):
        i = pl.program_id(1)
        _, vjp = jax.vjp(_ml_proj_tile, c_ref[...], x_ref[...], wq_ref[...], wk_ref[...], wv_ref[...],
                         gq_ref[...], gk_ref[...], gv_ref[...])
        grads = vjp((dq_ref[...], dk_ref[...], dv_ref[...], dg_ref[...]))
        dc_ref[...] = grads[0] + e_ref[...]
        dx_ref[...] = grads[1]

        @pl.when(i == 0)
        def _():
            for r in dw_refs:
                r[...] = jnp.zeros_like(r)

        for r, gval in zip(dw_refs, grads[2:]):
            r[...] += gval

    rb = pl.BlockSpec((tr, pb), lambda j, i: (i, j))
    wb = pl.BlockSpec((None, pb, pb), lambda j, i: (j, 0, 0))
    gwb = pl.BlockSpec((None, pb, LANES), lambda j, i: (j, 0, 0))
    gb = pl.BlockSpec((tr, LANES), lambda j, i: (i, 0))
    wshape = jax.ShapeDtypeStruct((nblk, pb, pb), F32)
    gshape = jax.ShapeDtypeStruct((nblk, pb, LANES), F32)
    return pl.pallas_call(
        body, name=name, grid=(nblk, rows // tr), in_specs=[rb, rb, wb, wb, wb, gwb, gwb, gwb, rb, rb, rb, gb, rb],
        out_specs=[rb, rb] + [wb] * 3 + [gwb] * 3,
        out_shape=[jax.ShapeDtypeStruct((rows, width), F32)] * 2 + [wshape] * 3 + [gshape] * 3,
        compiler_params=_params(("arbitrary", "arbitrary")))(cpre, xb, wq, wk, wv, gq, gk, gv, dq, dk, dv, dg, dcp_extra)


def _ml_gates_tile(gl, bg, nh):
    x = gl + bg
    bcum = _dot(_tri(CHUNK), _log_sigmoid(x), ((1,), (0,)), precision=HI)
    lane = lax.broadcasted_iota(jnp.int32, x.shape, 1)
    return jnp.where(lane < nh, x, jnp.where(lane < 2 * nh, bcum, 0.0))


def _ml_core_tile(q, k, v, colg, rowg, cpre, zb, nw, sk, cst, nst, m_prev):
    c, dh = q.shape
    igc, bc = _lane_pick(colg, 0), _lane_pick(colg, 1)
    igr, br = _row_pick(rowg, 0), _row_pick(rowg, 1)
    causal = _tri(c) > 0
    dmat = jnp.where(causal, bc - br + igr, -jnp.inf)
    inter = bc + m_prev
    mt = lax.stop_gradient(jnp.maximum(inter, jnp.max(dmat, axis=1, keepdims=True)))
    wt = jnp.exp(dmat - mt)
    w_prev = jnp.exp(inter - mt)
    qs = q * (dh ** -0.5)
    s = _dot_nt(qs, k) * wt
    num = _dot_nn(s, v) + w_prev * _dot_nn(qs, cst)
    den = jnp.sum(s, axis=1, keepdims=True) + w_prev * jnp.sum(qs * nst, axis=1, keepdims=True)
    h = num * (1.0 / jnp.maximum(jnp.abs(den), jnp.exp(-mt)))
    last = (lax.broadcasted_iota(jnp.int32, (c, 1), 0) == c - 1).astype(F32)
    blast = jnp.sum(bc * last, axis=0, keepdims=True)
    g = blast - bc + igc
    m_new = lax.stop_gradient(jnp.maximum(blast + m_prev, jnp.max(g, axis=0, keepdims=True)))
    decay = jnp.exp(blast + m_prev - m_new)
    wk = jnp.exp(g - m_new) * k
    c_new = decay * cst + _dot_tn(wk, v)
    n_new = decay * nst + jnp.sum(wk, axis=0, keepdims=True)
    mu = jnp.mean(h, axis=1, keepdims=True)
    hc = h - mu
    var = jnp.mean(hc * hc, axis=1, keepdims=True)
    out = hc * lax.rsqrt(var + HEAD_NORM_EPS) * nw + sk * _silu(cpre)
    return out * _silu(zb), c_new, n_new, m_new


def _ml_core_specs(nc, dh, rev):
    ch = (lambda c: nc - 1 - c) if rev else (lambda c: c)
    rb = pl.BlockSpec((CHUNK, dh), lambda b_, c, h: (b_ * nc + ch(c), h))
    colb = pl.BlockSpec((None, CHUNK, 2), lambda b_, c, h: (h, b_ * nc + ch(c), 0))
    rowb = pl.BlockSpec((None, None, 2, CHUNK), lambda b_, c, h: (b_ * nc + ch(c), h, 0, 0))
    pb = pl.BlockSpec((1, dh), lambda b_, c, h: (0, h))
    cb = pl.BlockSpec((None, None, dh, dh), lambda b_, c, h: (b_ * nc + ch(c), h, 0, 0))
    nb_ = pl.BlockSpec((None, None, 1, dh), lambda b_, c, h: (b_ * nc + ch(c), h, 0, 0))
    mb = pl.BlockSpec((None, None, 1, 1), lambda b_, c, h: (b_ * nc + ch(c), h, 0, 0))
    return rb, colb, rowb, pb, cb, nb_, mb


def ml_core_fwd(q, k, v, colg, rowg, cpre, zb, nw, sk, nb, nh, name):
    rows, width = q.shape
    dh = width // nh
    nc = rows // nb // CHUNK
    rb, colb, rowb, pb, cb, nb_, mb = _ml_core_specs(nc, dh, False)

    def body(q_ref, k_ref, v_ref, col_ref, row_ref, c_ref, z_ref, nw_ref, sk_ref, y_ref, cs_ref, ns_ref, ms_ref,
             cst_ref, nst_ref, mst_ref):
        c = pl.program_id(1)
        h = pl.program_id(2)

        @pl.when(c == 0)
        def _():
            cst_ref[h] = jnp.zeros((dh, dh), F32)
            nst_ref[h] = jnp.zeros((1, dh), F32)
            mst_ref[h] = jnp.zeros((1, 1), F32)

        cst, nst, m_prev = cst_ref[h], nst_ref[h], mst_ref[h]
        cs_ref[...] = cst
        ns_ref[...] = nst
        ms_ref[...] = m_prev
        y, c_new, n_new, m_new = _ml_core_tile(q_ref[...], k_ref[...], v_ref[...], col_ref[...], row_ref[...],
                                               c_ref[...], z_ref[...], nw_ref[...], sk_ref[...], cst, nst, m_prev)
        y_ref[...] = y.astype(BF16)
        cst_ref[h] = c_new
        nst_ref[h] = n_new
        mst_ref[h] = m_new

    nbc = nb * nc
    return pl.pallas_call(
        body, name=name, grid=(nb, nc, nh), in_specs=[rb, rb, rb, colb, rowb, rb, rb, pb, pb],
        out_specs=[rb, cb, nb_, mb],
        out_shape=[jax.ShapeDtypeStruct((rows, width), BF16), jax.ShapeDtypeStruct((nbc, nh, dh, dh), F32),
                   jax.ShapeDtypeStruct((nbc, nh, 1, dh), F32), jax.ShapeDtypeStruct((nbc, nh, 1, 1), F32)],
        scratch_shapes=[pltpu.VMEM((nh, dh, dh), F32), pltpu.VMEM((nh, 1, dh), F32), pltpu.VMEM((nh, 1, 1), F32)],
        compiler_params=_params(("arbitrary", "arbitrary", "arbitrary")))(q, k, v, colg, rowg, cpre, zb, nw, sk)


def ml_core_bwd(q, k, v, colg, rowg, cpre, zb, nw, sk, cs, ns, ms, dy, nb, nh, name):
    rows, width = q.shape
    dh = width // nh
    nc = rows // nb // CHUNK
    rb, colb, rowb, pb, cb, nb_, mb = _ml_core_specs(nc, dh, True)

    def body(q_ref, k_ref, v_ref, col_ref, row_ref, c_ref, z_ref, nw_ref, sk_ref, cs_ref, ns_ref, ms_ref, dy_ref,
             dq_ref, dk_ref, dv_ref, dc_ref, dz_ref, dcol_ref, drow_ref, dnw_ref, dsk_ref, dcst_ref, dnst_ref):
        bb = pl.program_id(0)
        step = pl.program_id(1)
        h = pl.program_id(2)

        @pl.when(jnp.logical_and(bb == 0, jnp.logical_and(step == 0, h == 0)))
        def _():
            dnw_ref[...] = jnp.zeros_like(dnw_ref)
            dsk_ref[...] = jnp.zeros_like(dsk_ref)

        @pl.when(step == 0)
        def _():
            dcst_ref[h] = jnp.zeros((dh, dh), F32)
            dnst_ref[h] = jnp.zeros((1, dh), F32)

        m_prev = ms_ref[...]

        def f(*a):
            return _ml_core_tile(*a, m_prev)[:3]

        _, vjp = jax.vjp(f, q_ref[...], k_ref[...], v_ref[...], col_ref[...], row_ref[...], c_ref[...], z_ref[...],
                         nw_ref[...], sk_ref[...], cs_ref[...], ns_ref[...])
        g = vjp((dy_ref[...], dcst_ref[h], dnst_ref[h]))
        dq_ref[...] = g[0]
        dk_ref[...] = g[1]
        dv_ref[...] = g[2]
        dcol_ref[...] = g[3]
        drow_ref[...] = g[4]
        dc_ref[...] = g[5]
        dz_ref[...] = g[6].astype(dz_ref.dtype)
        dnw_ref[h] += g[7]
        dsk_ref[h] += g[8]
        dcst_ref[h] = g[9]
        dnst_ref[h] = g[10]

    nbc = nb * nc
    accb = pl.BlockSpec((nh, 1, dh), lambda b_, c, h: (0, 0, 0))
    return pl.pallas_call(
        body, name=name, grid=(nb, nc, nh), in_specs=[rb, rb, rb, colb, rowb, rb, rb, pb, pb, cb, nb_, mb, rb],
        out_specs=[rb, rb, rb, rb, rb, colb, rowb, accb, accb],
        out_shape=[jax.ShapeDtypeStruct((rows, width), F32)] * 4 + [jax.ShapeDtypeStruct((rows, width), BF16)]
        + [jax.ShapeDtypeStruct(colg.shape, F32), jax.ShapeDtypeStruct(rowg.shape, F32),
           jax.ShapeDtypeStruct((nh, 1, dh), F32), jax.ShapeDtypeStruct((nh, 1, dh), F32)],
        scratch_shapes=[pltpu.VMEM((nh, dh, dh), F32), pltpu.VMEM((nh, 1, dh), F32)],
        compiler_params=_params(("arbitrary", "arbitrary", "arbitrary")))(
            q, k, v, colg, rowg, cpre, zb, nw, sk, cs, ns, ms, dy)


def _ssd_dt_tile(dtr, bias, alog):
    dt = _softplus(dtr + bias)
    cum = _dot(_tri(CHUNK), dt * (-jnp.exp(alog)), ((1,), (0,)), precision=HI)
    return dt, cum


def _ssd_tile(xcs, bmc, cmc, cols, rows_, z, dvec, gn, states, hpg):
    npair = hpg // 2
    hd = SSD_HEAD_DIM
    xs = [_silu(x) for x in xcs]
    bm, cm = _silu(bmc), _silu(cmc)
    cb = _dot_nt(cm, bm)
    causal = _tri(CHUNK) > 0
    lane_lo = lax.broadcasted_iota(jnp.int32, (1, 2 * hd), 1) < hd
    lastsel = (lax.broadcasted_iota(jnp.int32, (CHUNK, 1), 0) == CHUNK - 1).astype(F32)
    heads = []
    for r in range(hpg):
        dtc, cumc = _lane_pick(cols, r), _lane_pick(cols, hpg + r)
        dtrow, cumr = _row_pick(rows_, r), _row_pick(rows_, hpg + r)
        w = cb * jnp.exp(jnp.where(causal, cumc - cumr, -jnp.inf)) * dtrow
        last = jnp.sum(cumc * lastsel, axis=0, keepdims=True)
        heads.append((w, jnp.exp(cumc), jnp.exp(last - cumc) * dtc, jnp.exp(last)))
    ys, new_states = [], []
    for j in range(npair):
        (wa, ea, da, la), (wb, eb, db, lb) = heads[2 * j], heads[2 * j + 1]
        yi = jnp.where(lane_lo, _dot_nn(wa, xs[j]), _dot_nn(wb, xs[j]))
        ys.append(yi + jnp.where(lane_lo, ea, eb) * _dot_nn(cm, states[j]))
        xd = xs[j] * jnp.where(lane_lo, da, db)
        new_states.append(jnp.where(lane_lo, la, lb) * states[j] + _dot_tn(bm, xd))
    y = jnp.concatenate(ys, axis=1) + dvec * jnp.concatenate(xs, axis=1)
    yg = y * _silu(z)
    yn = yg * lax.rsqrt(jnp.mean(yg * yg, axis=1, keepdims=True) + NORM_EPS) * gn
    return yn, new_states


def _ssd_specs(nc, hpg, ng, rev):
    npair = hpg // 2
    gw = hpg * SSD_HEAD_DIM
    xblocks = ng * npair
    ch = (lambda c: nc - 1 - c) if rev else (lambda c: c)
    xs = [pl.BlockSpec((CHUNK, LANES), functools.partial(lambda b_, c, g, jj: (b_ * nc + ch(c), g * npair + jj), jj=j))
          for j in range(npair)]
    bmb = pl.BlockSpec((CHUNK, SSD_STATE), lambda b_, c, g: (b_ * nc + ch(c), xblocks + g))
    cmb = pl.BlockSpec((CHUNK, SSD_STATE), lambda b_, c, g: (b_ * nc + ch(c), xblocks + ng + g))
    colb = pl.BlockSpec((None, CHUNK, 2 * hpg), lambda b_, c, g: (g, b_ * nc + ch(c), 0))
    rowb = pl.BlockSpec((None, None, 2 * hpg, CHUNK), lambda b_, c, g: (b_ * nc + ch(c), g, 0, 0))
    zb = pl.BlockSpec((CHUNK, gw), lambda b_, c, g: (b_ * nc + ch(c), g))
    pb = pl.BlockSpec((1, gw), lambda b_, c, g: (0, g))
    sb = pl.BlockSpec((None, None, npair, SSD_STATE, 2 * SSD_HEAD_DIM), lambda b_, c, g: (b_ * nc + ch(c), g, 0, 0, 0))
    return xs, bmb, cmb, colb, rowb, zb, pb, sb


def ssd_core_fwd(cpre, cols, rows_, z, dvec, gn, nb, hpg, name):
    rows = cpre.shape[0]
    inner = z.shape[1]
    ng = inner // (hpg * SSD_HEAD_DIM)
    npair = hpg // 2
    nc = rows // nb // CHUNK
    xs, bmb, cmb, colb, rowb, zb, pb, sb = _ssd_specs(nc, hpg, ng, False)

    def body(*refs):
        x_refs = refs[:npair]
        bm_ref, cm_ref, col_ref, row_ref, z_ref, d_ref, gn_ref, y_ref, so_ref, st_ref = refs[npair:]
        c = pl.program_id(1)
        g = pl.program_id(2)

        @pl.when(c == 0)
        def _():
            st_ref[g] = jnp.zeros((npair, SSD_STATE, 2 * SSD_HEAD_DIM), F32)

        so_ref[...] = st_ref[g]
        states = [st_ref[g, j] for j in range(npair)]
        yn, new_states = _ssd_tile([r[...] for r in x_refs], bm_ref[...], cm_ref[...], col_ref[...], row_ref[...],
                                   z_ref[...], d_ref[...], gn_ref[...], states, hpg)
        y_ref[...] = yn.astype(BF16)
        for j in range(npair):
            st_ref[g, j] = new_states[j]

    return pl.pallas_call(
        body, name=name, grid=(nb, nc, ng), in_specs=xs + [bmb, cmb, colb, rowb, zb, pb, pb],
        out_specs=[zb, sb],
        out_shape=[jax.ShapeDtypeStruct((rows, inner), BF16),
                   jax.ShapeDtypeStruct((nb * nc, ng, npair, SSD_STATE, 2 * SSD_HEAD_DIM), F32)],
        scratch_shapes=[pltpu.VMEM((ng, npair, SSD_STATE, 2 * SSD_HEAD_DIM), F32)],
        compiler_params=_params(("arbitrary", "arbitrary", "arbitrary")))(
            *([cpre] * npair), cpre, cpre, cols, rows_, z, dvec, gn)


def ssd_core_bwd(cpre, cols, rows_, z, dvec, gn, states, dyn, nb, hpg, name):
    rows = cpre.shape[0]
    inner = z.shape[1]
    gw = hpg * SSD_HEAD_DIM
    ng = inner // gw
    npair = hpg // 2
    nc = rows // nb // CHUNK
    xs, bmb, cmb, colb, rowb, zb, pb, sb = _ssd_specs(nc, hpg, ng, True)

    def body(*refs):
        x_refs = refs[:npair]
        (bm_ref, cm_ref, col_ref, row_ref, z_ref, d_ref, gn_ref, s_ref, dy_ref,
         dx_ref, dbm_ref, dcm_ref, dcol_ref, drow_ref, dz_ref, dd_ref, dgn_ref, dst_ref) = refs[npair:]
        bb = pl.program_id(0)
        step = pl.program_id(1)
        g = pl.program_id(2)

        @pl.when(jnp.logical_and(bb == 0, jnp.logical_and(step == 0, g == 0)))
        def _():
            dd_ref[...] = jnp.zeros_like(dd_ref)
            dgn_ref[...] = jnp.zeros_like(dgn_ref)

        @pl.when(step == 0)
        def _():
            dst_ref[g] = jnp.zeros((npair, SSD_STATE, 2 * SSD_HEAD_DIM), F32)

        def f(xcs, bmc, cmc, cv, rv, zv, dv_, gv, sts):
            return _ssd_tile(xcs, bmc, cmc, cv, rv, zv, dv_, gv, sts, hpg)

        _, vjp = jax.vjp(f, [r[...] for r in x_refs], bm_ref[...], cm_ref[...], col_ref[...], row_ref[...], z_ref[...],
                         d_ref[...], gn_ref[...], [s_ref[j] for j in range(npair)])
        gr = vjp((dy_ref[...], [dst_ref[g, j] for j in range(npair)]))
        dx_ref[...] = jnp.concatenate(gr[0], axis=1)
        dbm_ref[...] = gr[1]
        dcm_ref[...] = gr[2]
        dcol_ref[...] = gr[3]
        drow_ref[...] = gr[4]
        dz_ref[...] = gr[5].astype(dz_ref.dtype)
        dd_ref[g] += gr[6]
        dgn_ref[g] += gr[7]
        for j in range(npair):
            dst_ref[g, j] = gr[8][j]

    ch = lambda c: nc - 1 - c
    nblk = pl.BlockSpec((CHUNK, SSD_STATE), lambda b_, c, g: (b_ * nc + ch(c), g))
    accb = pl.BlockSpec((ng, 1, gw), lambda b_, c, g: (0, 0, 0))
    return pl.pallas_call(
        body, name=name, grid=(nb, nc, ng), in_specs=xs + [bmb, cmb, colb, rowb, zb, pb, pb, sb, zb],
        out_specs=[zb, nblk, nblk, colb, rowb, zb, accb, accb],
        out_shape=[jax.ShapeDtypeStruct((rows, inner), F32), jax.ShapeDtypeStruct((rows, ng * SSD_STATE), F32),
                   jax.ShapeDtypeStruct((rows, ng * SSD_STATE), F32), jax.ShapeDtypeStruct(cols.shape, F32),
                   jax.ShapeDtypeStruct(rows_.shape, F32), jax.ShapeDtypeStruct((rows, inner), BF16),
                   jax.ShapeDtypeStruct((ng, 1, gw), F32), jax.ShapeDtypeStruct((ng, 1, gw), F32)],
        scratch_shapes=[pltpu.VMEM((ng, npair, SSD_STATE, 2 * SSD_HEAD_DIM), F32)],
        compiler_params=_params(("arbitrary", "arbitrary", "arbitrary")))(
            *([cpre] * npair), cpre, cpre, cols, rows_, z, dvec, gn, states, dyn)


def _hw_expand(w):
    n, bi, _ = w.shape
    per = PROJ_BLOCK // bi
    tiled = jnp.tile(w.reshape(n // per, PROJ_BLOCK, bi), (1, 1, per))
    return jnp.where(_hw_mask(bi), tiled, 0.0)


def _hw_mask(bi):
    r = lax.broadcasted_iota(jnp.int32, (PROJ_BLOCK, PROJ_BLOCK), 0) // bi
    c = lax.broadcasted_iota(jnp.int32, (PROJ_BLOCK, PROJ_BLOCK), 1) // bi
    return r == c


def _hw_contract(d, bi=QKV_BLOCK):
    per = PROJ_BLOCK // bi
    kept = jnp.where(_hw_mask(bi), d, 0.0)
    return kept.reshape(d.shape[0], PROJ_BLOCK, per, bi).sum(axis=2).reshape(-1, bi, bi)


def _wg_expand(wg, width):
    pad = jnp.pad(wg, ((0, 0), (0, LANES - wg.shape[1])))
    return [pad[i * width:(i + 1) * width].reshape(width // PROJ_BLOCK, PROJ_BLOCK, LANES) for i in range(3)]


def _wg_contract(dgs, ngate):
    return jnp.concatenate([d[:, :, :ngate].reshape(-1, ngate) for d in dgs], axis=0)


def _pad_lanes(a):
    return jnp.pad(a, ((0, 0), (0, LANES - a.shape[1])))


def _pairs_to_layouts(first, second, ngrp, per, nbc):
    rows = first.shape[0]
    both = jnp.concatenate([first.reshape(rows, ngrp, per), second.reshape(rows, ngrp, per)], axis=2)
    return both.transpose(1, 0, 2), both.reshape(nbc, CHUNK, ngrp, 2 * per).transpose(0, 2, 3, 1)


def _layouts_to_pairs(dcols, drows, ngrp, per):
    rows = dcols.shape[1]
    both = dcols.transpose(1, 0, 2) + drows.transpose(0, 3, 1, 2).reshape(rows, ngrp, 2 * per)
    return both[:, :, :per].reshape(rows, ngrp * per), both[:, :, per:].reshape(rows, ngrp * per)


_EARLY = ("W0a", "W0xb", "W0zb", "glu")
_LATE = ("Wo0a", "Wo0b", "W1z", "W1x", "W1dt", "Wo1")


def _local_step(x, target, bw, sp, late_weights=None, late_grads=None, early_grads=None):
    nb, seq, d = x.shape
    nh, hpg = MLSTM_HEADS, SSD_HPG
    t_len = N_META + seq
    nc = -(-t_len // CHUNK)
    tp = nc * CHUNK
    rows = nb * tp
    nbc = nb * nc
    meta = sp["meta_tokens"]
    h0 = jnp.concatenate([jnp.broadcast_to(meta[None], (nb, N_META, d)), x, jnp.zeros((nb, tp - t_len, d), F32)], axis=1)
    h0 = h0.reshape(rows, d)
    tgt = jnp.pad(target, ((0, 0), (N_META, tp - t_len), (0, 0))).reshape(rows, d)

    n0 = norm_fwd(h0, sp["ab_norm"], "norm0")
    pa = mm(n0, bw["W0a"], "nn", "mm_pa")
    xb = mm(n0, bw["W0xb"], "nn", "mm_xb")
    zb = mm(n0, bw["W0zb"], "nn", "mm_zb")
    s5w = pa.shape[1] // 2
    mlw = xb.shape[1]
    s5_args = (sp["s5_lambda_re"], sp["s5_lambda_im"], sp["s5_log_dt"].reshape(-1), sp["s5_b_re"], sp["s5_b_im"])
    (ar, ai, bbr, bbi), s5_disc_vjp = jax.vjp(_s5_discretize, *s5_args)
    sg, spn, shh = bbr.shape
    bre, bim, cre, cim, are, aie = _s5_expand(ar, ai, bbr, bbi, sp["s5_c_re"], sp["s5_c_im"])
    ys5, gb, s5st = s5_fwd(pa, bre, bim, cre, cim, are, aie, sp["s5_d"], nb, "s5_fwd")
    tglu = mm(gb, bw["glu"], "nn", "mm_glu")

    def glu_tile(ys, tt, za, gbias):
        return _gelu(ys) * _sigmoid(tt + gbias) * _silu(za)

    ya = rowwise("glu_fwd", lambda i, ys, tt, pab, gbias: glu_tile(ys, tt, pab[:, s5w:], gbias),
                 [ys5, tglu, pa], [sp["s5_glu_b"]], [(s5w, BF16)], tr=_tile(rows, 256, 16))[0]

    cpre0 = conv_fwd(xb, sp["ml_conv_w"], sp["ml_conv_b"], nb, "ml_conv_fwd")
    wq_e, wk_e, wv_e = _hw_expand(sp["ml_wq"]), _hw_expand(sp["ml_wk"]), _hw_expand(sp["ml_wv"])
    gq, gk, gv = _wg_expand(sp["ml_w_gate"], mlw)
    q, k, v, gl = ml_proj_fwd(cpre0, xb, wq_e, wk_e, wv_e, gq, gk, gv, "ml_proj_fwd")
    bgate = _pad_lanes(sp["ml_b_gate"])
    gout = rowwise("ml_gates_fwd", lambda i, g_, b_: _ml_gates_tile(g_, b_, nh), [gl], [bgate], [(LANES, F32)], tr=CHUNK)[0]
    colg, rowg = _pairs_to_layouts(gout[:, :nh], gout[:, nh:2 * nh], nh, 1, nbc)
    yb, ml_cs, ml_ns, ml_ms = ml_core_fwd(q, k, v, colg, rowg, cpre0, zb, sp["ml_norm"], sp["ml_skip"], nb, nh, "ml_core_fwd")
    if late_weights is not None:
        bw = {**bw, **late_weights()}
    h1 = mm(ya, bw["Wo0a"], "nn", "mm_out0a", resid=h0)
    h1 = mm(yb, bw["Wo0b"], "nn", "mm_out0b", resid=h1)

    n1 = norm_fwd(h1, sp["ssd_norm"], "norm1")
    z1 = mm(n1, bw["W1z"], "nn", "mm_z1")
    xbc = mm(n1, bw["W1x"], "nn", "mm_xbc")
    dtr = mm(n1, bw["W1dt"], "nn", "mm_dt")
    inner = z1.shape[1]
    ng = inner // (hpg * SSD_HEAD_DIM)
    nhd = ng * hpg
    cpre1 = conv_fwd(xbc, sp["ssd_conv_w"], sp["ssd_conv_b"], nb, "ssd_conv_fwd")
    dt_bias, a_log = _pad_lanes(sp["ssd_dt_bias"]), _pad_lanes(sp["ssd_a_log"])
    dt, cum = rowwise("ssd_dt_fwd", lambda i, r_, b_, a_: _ssd_dt_tile(r_, b_, a_), [dtr], [dt_bias, a_log],
                      [(LANES, F32), (LANES, F32)], tr=CHUNK)
    cols, rws = _pairs_to_layouts(dt[:, :nhd], cum[:, :nhd], ng, hpg, nbc)
    dvec = jnp.repeat(sp["ssd_d"], SSD_HEAD_DIM, axis=1)
    yn, ssd_st = ssd_core_fwd(cpre1, cols, rws, z1, dvec, sp["ssd_gnorm"], nb, hpg, "ssd_core_fwd")
    h2 = mm(yn, bw["Wo1"], "nn", "mm_out1", resid=h1)

    tr_l = _tile(tp, 256, 16)
    per_ex = tp // tr_l

    def loss_tile(i, hb, tb, gfn):
        tpos = (i % per_ex) * tr_l + lax.broadcasted_iota(jnp.int32, (tr_l, 1), 0)
        mask = jnp.logical_and(tpos >= N_META, tpos < t_len).astype(F32)

        def lf(hh, gg):
            e = (_rms(hh, gg) - tb) * mask
            return 0.5 * jnp.sum(e * e) / d

        lval, (dh, dg) = jax.value_and_grad(lf, (0, 1))(hb, gfn)
        return dh, dh, jnp.full((1, LANES), lval, F32), dg

    fn = sp["final_norm"].reshape(1, d)
    dh2, dh2b, loss_acc, dfn = rowwise("loss", loss_tile, [h2, tgt], [fn], [(d, F32), (d, BF16)], [(1, LANES), (1, d)], tr=tr_l)

    gbig, gs = {}, {}
    gs["final_norm"] = dfn.reshape(sp["final_norm"].shape)
    dyn = mm(dh2b, bw["Wo1"], "nt", "mm_dyn")
    gbig["Wo1"] = mm(yn, dh2b, "tn", "mm_dWo1", out_dtype=BF16)
    dxs, dbm, dcm, dcols, drws, dz1, ddvec, dgn = ssd_core_bwd(cpre1, cols, rws, z1, dvec, sp["ssd_gnorm"], ssd_st, dyn,
                                                              nb, hpg, "ssd_core_bwd")
    gs["ssd_d"] = ddvec.reshape(1, nhd, SSD_HEAD_DIM).sum(axis=2)
    gs["ssd_gnorm"] = dgn.reshape(1, inner)
    ddt, dcum = _layouts_to_pairs(dcols, drws, ng, hpg)

    def ssd_dt_bwd_tile(i, r_, ddt_, dcum_, b_, a_):
        _, vjp = jax.vjp(_ssd_dt_tile, r_, b_, a_)
        return vjp((ddt_, dcum_))

    ddtr, dbias, dalog = rowwise("ssd_dt_bwd", ssd_dt_bwd_tile, [dtr, _pad_lanes(ddt), _pad_lanes(dcum)], [dt_bias, a_log],
                                 [(LANES, BF16)], [(1, LANES), (1, LANES)], tr=CHUNK)
    gs["ssd_dt_bias"] = dbias[:, :nhd]
    gs["ssd_a_log"] = dalog[:, :nhd]
    dcpre1 = jnp.concatenate([dxs, dbm, dcm], axis=1)
    dxbc, dcw1, dcb1 = conv_bwd(dcpre1, xbc, sp["ssd_conv_w"], nb, "ssd_conv_bwd")
    gs["ssd_conv_w"] = dcw1
    gs["ssd_conv_b"] = dcb1
    dn1 = mm(dz1, bw["W1z"], "nt", "mm_dn1z")
    dn1 = mm(dxbc, bw["W1x"], "nt", "mm_dn1x", resid=dn1)
    dn1 = mm(ddtr, bw["W1dt"], "nt", "mm_dn1dt", resid=dn1)
    gbig["W1z"] = mm(n1, dz1, "tn", "mm_dW1z", out_dtype=BF16)
    gbig["W1x"] = mm(n1, dxbc, "tn", "mm_dW1x", out_dtype=BF16)
    gbig["W1dt"] = mm(n1, ddtr, "tn", "mm_dW1dt", out_dtype=BF16)
    dh1, dh1b, dg1 = norm_bwd(h1, sp["ssd_norm"], dn1, dh2, "norm1_bwd")
    gs["ssd_norm"] = dg1

    gbig["Wo0a"] = mm(ya, dh1b, "tn", "mm_dWo0a", out_dtype=BF16)
    gbig["Wo0b"] = mm(yb, dh1b, "tn", "mm_dWo0b", out_dtype=BF16)
    if late_grads is not None:
        late_grads({n: gbig[n] for n in _LATE})
    dya = mm(dh1b, bw["Wo0a"], "nt", "mm_dya")
    dyb = mm(dh1b, bw["Wo0b"], "nt", "mm_dyb")
    (dq, dk, dv, dcp_skip, dzb, dcolg, drowg, dnw, dsk) = ml_core_bwd(
        q, k, v, colg, rowg, cpre0, zb, sp["ml_norm"], sp["ml_skip"], ml_cs, ml_ns, ml_ms, dyb, nb, nh, "ml_core_bwd")
    gs["ml_norm"] = dnw.reshape(1, mlw)
    gs["ml_skip"] = dsk.reshape(1, mlw)
    dig, dbcum = _layouts_to_pairs(dcolg, drowg, nh, 1)
    dgout = _pad_lanes(jnp.concatenate([dig, dbcum], axis=1))

    def ml_gates_bwd_tile(i, g_, dgo, b_):
        _, vjp = jax.vjp(lambda a, b: _ml_gates_tile(a, b, nh), g_, b_)
        return vjp(dgo)

    dgl, dbg = rowwise("ml_gates_bwd", ml_gates_bwd_tile, [gl, dgout], [bgate], [(LANES, F32)], [(1, LANES)], tr=CHUNK)
    gs["ml_b_gate"] = dbg[:, :2 * nh]
    dcpre0, dxb_v, dwq, dwk, dwv, dgq, dgk, dgv = ml_proj_bwd(cpre0, xb, wq_e, wk_e, wv_e, gq, gk, gv, dq, dk, dv, dgl,
                                                            dcp_skip, "ml_proj_bwd")
    gs["ml_wq"], gs["ml_wk"], gs["ml_wv"] = _hw_contract(dwq), _hw_contract(dwk), _hw_contract(dwv)
    gs["ml_w_gate"] = _wg_contract([dgq, dgk, dgv], 2 * nh)
    dxb, dcw0, dcb0 = conv_bwd(dcpre0, xb, sp["ml_conv_w"], nb, "ml_conv_bwd", resid=dxb_v)
    gs["ml_conv_w"] = dcw0
    gs["ml_conv_b"] = dcb0

    def glu_bwd_tile(i, ys, tt, pab, dy_, gbias):
        _, vjp = jax.vjp(glu_tile, ys, tt, pab[:, s5w:], gbias)
        return vjp(dy_)

    dys_direct, dtglu, dza, dglub = rowwise("glu_bwd", glu_bwd_tile, [ys5, tglu, pa, dya], [sp["s5_glu_b"]],
                                            [(s5w, F32), (s5w, BF16), (s5w, BF16)], [(1, s5w)], tr=_tile(rows, 256, 16))
    gs["s5_glu_b"] = dglub
    dgb = mm(dtglu, bw["glu"], "nt", "mm_dgb")
    gbig["glu"] = mm(gb, dtglu, "tn", "mm_dglu", out_dtype=BF16)

    def gelu_bwd_tile(i, ys, dg_, direct):
        _, vjp = jax.vjp(_gelu, ys)
        return vjp(dg_)[0] + direct

    dys5 = rowwise("gelu_bwd", gelu_bwd_tile, [ys5, dgb, dys_direct], [], [(s5w, F32)], tr=_tile(rows, 256, 16))[0]
    du, dbre, dbim, dcre, dcim, dare, daie, dd5 = s5_bwd(pa, dys5, s5st, bre, bim, cre, cim, are, aie, sp["s5_d"], nb, "s5_bwd")
    gs["s5_d"] = dd5
    dbbr, dbbi, dcr, dci, dar, dai = _s5_contract(dbre, dbim, dcre, dcim, dare, daie, sg, spn, shh)
    gs["s5_c_re"], gs["s5_c_im"] = dcr, dci
    (gs["s5_lambda_re"], gs["s5_lambda_im"], dlogdt, gs["s5_b_re"], gs["s5_b_im"]) = s5_disc_vjp((dar, dai, dbbr, dbbi))
    gs["s5_log_dt"] = dlogdt.reshape(1, -1)
    dpa = jnp.concatenate([du, dza], axis=1)
    gbig["W0a"] = mm(n0, dpa, "tn", "mm_dW0a", out_dtype=BF16)
    gbig["W0xb"] = mm(n0, dxb, "tn", "mm_dW0xb", out_dtype=BF16)
    gbig["W0zb"] = mm(n0, dzb, "tn", "mm_dW0zb", out_dtype=BF16)
    if early_grads is not None:
        early_grads({n: gbig[n] for n in _EARLY})
    dn0 = mm(dpa, bw["W0a"], "nt", "mm_dn0a")
    dn0 = mm(dxb, bw["W0xb"], "nt", "mm_dn0xb", resid=dn0)
    dn0 = mm(dzb, bw["W0zb"], "nt", "mm_dn0zb", resid=dn0)
    dh0, _, dg0 = norm_bwd(h0, sp["ab_norm"], dn0, dh1, "norm0_bwd")
    gs["ab_norm"] = dg0
    dh0 = dh0.reshape(nb, tp, d)
    gs["meta_tokens"] = jnp.sum(dh0[:, :N_META], axis=0)
    return loss_acc[0, 0], dh0, gbig, gs


N_DEV = 8
N_CHIP = 4
N_PEER_CHIPS = N_CHIP - 1
MESH = pl.DeviceIdType.MESH
_HBM = pl.BlockSpec(memory_space=pltpu.HBM)


def _place():
    x, y, c = lax.axis_index("x"), lax.axis_index("y"), lax.axis_index("c")
    return x, y, c, [(1 - x, y), (x, 1 - y), (1 - x, 1 - y)]


def all_gather8(v, name):
    m_per, n = v.shape

    def body(x_ref, out_ref, send_sems, recv_sems, local_sem):
        x, y, c, chips = _place()
        me, sibling = (x, y, c), (x, y, 1 - c)

        def rows(px, py, pc):
            return out_ref.at[pl.ds((4 * px + 2 * py + pc) * m_per, m_per), :]

        def copy(kk, block, to, src=None):
            return pltpu.make_async_remote_copy(
                src_ref=rows(*block) if src is None else src, dst_ref=rows(*block), send_sem=send_sems.at[kk],
                recv_sem=recv_sems.at[kk], device_id=to, device_id_type=MESH)

        mine = pltpu.make_async_copy(x_ref, rows(*me), local_sem)
        mine.start()
        first = [copy(0, me, sibling, src=x_ref)]
        first += [copy(1 + j, me, (*chip, c), src=x_ref) for j, chip in enumerate(chips)]
        for cp in first:
            cp.start()
        passed = [copy(4 + j, (*chip, c), sibling) for j, chip in enumerate(chips)]
        for j, chip in enumerate(chips):
            copy(1 + j, (*chip, c), me).wait_recv()
            passed[j].start()
        copy(0, sibling, me).wait_recv()
        for j, chip in enumerate(chips):
            copy(4 + j, (*chip, 1 - c), me).wait_recv()
        for cp in first + passed:
            cp.wait_send()
        mine.wait()

    return pl.pallas_call(
        body, name=name, out_shape=jax.ShapeDtypeStruct((N_DEV * m_per, n), v.dtype),
        in_specs=[pl.BlockSpec(memory_space=pltpu.VMEM)], out_specs=pl.BlockSpec(memory_space=pltpu.VMEM),
        scratch_shapes=[pltpu.SemaphoreType.DMA((7,)), pltpu.SemaphoreType.DMA((7,)), pltpu.SemaphoreType.DMA],
        compiler_params=pltpu.CompilerParams(vmem_limit_bytes=VMEM_LIMIT))(v)


def gather_chips(vs, name):
    na = len(vs)

    def body(*refs):
        x_refs, out_refs = refs[:na], refs[na:2 * na]
        send_sems, recv_sems, local_sems = refs[2 * na:]
        x, y, c, chips = _place()
        k = 2 * x + y
        sibling = (x, y, 1 - c)

        def copy(i, kk, src, chip_k, half, to):
            return pltpu.make_async_remote_copy(
                src_ref=src, dst_ref=out_refs[i].at[chip_k, half], send_sem=send_sems.at[6 * i + kk],
                recv_sem=recv_sems.at[6 * i + kk], device_id=to, device_id_type=MESH)

        mine = [pltpu.make_async_copy(x_refs[i], out_refs[i].at[k], local_sems.at[i]) for i in range(na)]
        for cp in mine:
            cp.start()
        first = [copy(i, j, x_refs[i].at[c], k, c, (*chip, c)) for j, chip in enumerate(chips) for i in range(na)]
        for cp in first:
            cp.start()
        passed = []
        for j, (cx, cy) in enumerate(chips):
            kj = 2 * cx + cy
            for i in range(na):
                copy(i, j, out_refs[i].at[kj, c], kj, c, (cx, cy, c)).wait_recv()
                fwd = copy(i, 3 + j, out_refs[i].at[kj, c], kj, c, sibling)
                fwd.start()
                passed.append(fwd)
        for j, (cx, cy) in enumerate(chips):
            kj = 2 * cx + cy
            for i in range(na):
                copy(i, 3 + j, out_refs[i].at[kj, 1 - c], kj, 1 - c, sibling).wait_recv()
        for cp in first + passed:
            cp.wait_send()
        for cp in mine:
            cp.wait()

    return pl.pallas_call(
        body, name=name, out_shape=[jax.ShapeDtypeStruct((N_CHIP,) + v.shape, v.dtype) for v in vs],
        in_specs=[_HBM] * na, out_specs=[_HBM] * na,
        scratch_shapes=[pltpu.SemaphoreType.DMA((6 * na,)), pltpu.SemaphoreType.DMA((6 * na,)),
                        pltpu.SemaphoreType.DMA((na,))])(*vs)


def swap_halves(gs_, name):
    na = len(gs_)

    def body(*refs):
        g_refs, out_refs = refs[:na], refs[na:2 * na]
        send_sems, recv_sems = refs[2 * na:]
        x, y, c, _ = _place()
        cps = [pltpu.make_async_remote_copy(
            src_ref=g_refs[i].at[kk, 1 - c], dst_ref=out_refs[i].at[kk], send_sem=send_sems.at[N_CHIP * i + kk],
            recv_sem=recv_sems.at[N_CHIP * i + kk], device_id=(x, y, 1 - c), device_id_type=MESH)
            for i in range(na) for kk in range(N_CHIP)]
        for cp in cps:
            cp.start()
        for cp in cps:
            cp.wait()

    return pl.pallas_call(
        body, name=name, out_shape=[jax.ShapeDtypeStruct((N_CHIP,) + g.shape[2:], g.dtype) for g in gs_],
        in_specs=[_HBM] * na, out_specs=[_HBM] * na,
        scratch_shapes=[pltpu.SemaphoreType.DMA((N_CHIP * na,)), pltpu.SemaphoreType.DMA((N_CHIP * na,))])(*gs_)


def add_halves(g, other, core, name):
    _, _, m, n = g.shape
    tr = _tile(m, 256, 16)

    def body(core_ref, g_ref, o_ref, out_ref):
        out_ref[...] = (g_ref[...].astype(F32) + o_ref[...].astype(F32)).astype(out_ref.dtype)

    grid_spec = pltpu.PrefetchScalarGridSpec(
        num_scalar_prefetch=1, grid=(N_CHIP, m // tr),
        in_specs=[pl.BlockSpec((None, None, tr, n), lambda kk, i, core_ref: (kk, core_ref[0], i, 0)),
                  pl.BlockSpec((None, tr, n), lambda kk, i, core_ref: (kk, i, 0))],
        out_specs=pl.BlockSpec((None, tr, n), lambda kk, i, core_ref: (kk, i, 0)))
    return pl.pallas_call(body, name=name, grid_spec=grid_spec, out_shape=jax.ShapeDtypeStruct((N_CHIP, m, n), g.dtype),
                          compiler_params=_params(("arbitrary", "arbitrary")))(core.reshape(1).astype(jnp.int32), g, other)


def sequencer_exchange(srcs, scatter, collective_id, name):
    na = len(srcs)
    per = 2 * N_PEER_CHIPS + (1 if scatter else 0)
    hbm = pltpu.MemorySpace.HBM
    src_refs = [jax.new_ref(a, memory_space=hbm) for a in srcs]
    out_refs = [jax.empty_ref(jax.ShapeDtypeStruct((N_CHIP, 2) + a.shape[1:], a.dtype), memory_space=hbm) for a in srcs]

    @pl.kernel(mesh=plsc.ScalarSubcoreMesh(axis_name="seq", num_cores=1), name=name,
               scratch_types=(pltpu.SemaphoreType.DMA((per * na,)), pltpu.SemaphoreType.DMA((per * na,)),
                              pltpu.SemaphoreType.DMA((na,))),
               compiler_params=pltpu.CompilerParams(collective_id=collective_id))
    def launch(send_sems, recv_sems, local_sems):
        x, y, c, chips = _place()
        k = 2 * x + y
        sibling = (x, y, 1 - c)
        barrier = pltpu.get_barrier_semaphore()
        for cx, cy in chips:
            pl.semaphore_signal(barrier, inc=1, device_id=(cx, cy, c), device_id_type=MESH)
        pl.semaphore_signal(barrier, inc=1, device_id=sibling, device_id_type=MESH)
        pl.semaphore_wait(barrier, N_CHIP)

        def copy(i, kk, src, chip_k, half, to):
            return pltpu.make_async_remote_copy(
                src_ref=src, dst_ref=out_refs[i].at[chip_k, half], send_sem=send_sems.at[per * i + kk],
                recv_sem=recv_sems.at[per * i + kk], device_id=to, device_id_type=MESH)

        if scatter:
            mine = [pltpu.make_async_copy(src_refs[i].at[k], out_refs[i].at[k, c], local_sems.at[i]) for i in range(na)]
        else:
            mine = [pltpu.make_async_copy(src_refs[i], out_refs[i].at[k], local_sems.at[i]) for i in range(na)]
        for cp in mine:
            cp.start()
        first = []
        for j, (cx, cy) in enumerate(chips):
            for i in range(na):
                src = src_refs[i].at[2 * cx + cy] if scatter else src_refs[i].at[c]
                first.append(copy(i, j, src, k, c, (cx, cy, c)))
        if scatter:
            first += [copy(i, 2 * N_PEER_CHIPS, src_refs[i].at[k], k, c, sibling) for i in range(na)]
        for cp in first:
            cp.start()
        passed = []
        for j, (cx, cy) in enumerate(chips):
            kj = 2 * cx + cy
            for i in range(na):
                copy(i, j, out_refs[i].at[kj, c], kj, c, (cx, cy, c)).wait_recv()
                fwd = copy(i, N_PEER_CHIPS + j, out_refs[i].at[kj, c], kj, c, sibling)
                fwd.start()
                passed.append(fwd)
        if scatter:
            for i in range(na):
                copy(i, 2 * N_PEER_CHIPS, out_refs[i].at[k, 1 - c], k, 1 - c, sibling).wait_recv()
        for j, (cx, cy) in enumerate(chips):
            kj = 2 * cx + cy
            for i in range(na):
                copy(i, N_PEER_CHIPS + j, out_refs[i].at[kj, 1 - c], kj, 1 - c, sibling).wait_recv()
        for cp in first + passed:
            cp.wait_send()
        for cp in mine:
            cp.wait()

    launch()
    return [r[...] for r in out_refs]


PACK_LANES = 512


def _pack(arrs, dtype, lanes, row_align):
    flat = jnp.concatenate([a.reshape(-1).astype(dtype) for a in arrs])
    unit = lanes * row_align
    total = -(-flat.shape[0] // unit) * unit
    return jnp.pad(flat, (0, total - flat.shape[0])).reshape(total // lanes, lanes)


def _unpack(flat, shapes):
    flat = flat.reshape(-1)
    out, off = [], 0
    for s in shapes:
        n = math.prod(s)
        out.append(flat[off:off + n].reshape(s))
        off += n
    return out


def _adam_tile(w, m, v, g):
    m2 = ADAM_B1 * m + (1.0 - ADAM_B1) * g
    v2 = ADAM_B2 * v + (1.0 - ADAM_B2) * (g * g)
    m_hat = m2 / (1.0 - ADAM_B1 ** ADAM_STEP)
    v_hat = v2 / (1.0 - ADAM_B2 ** ADAM_STEP)
    delta = -ADAM_LR * (m_hat / (jnp.sqrt(v_hat) + ADAM_EPS) + ADAM_WD * w)
    return delta, m2, v2


def adam_big(w, m, v, pieces, name):
    _, r, c = w.shape
    tr = _tile(r, 128, 16)

    def body(w_ref, m_ref, v_ref, p0, p1, p2, p3, g_ref, d_ref, mo_ref, vo_ref):
        g = ((p0[...].astype(F32) + p1[...].astype(F32)) + p2[...].astype(F32)) + p3[...].astype(F32)
        delta, m2, v2 = _adam_tile(w_ref[...], m_ref[...], v_ref[...], g)
        g_ref[...] = g
        d_ref[...] = delta
        mo_ref[...] = m2
        vo_ref[...] = v2

    wspec = pl.BlockSpec((None, tr, c), lambda i: (0, i, 0))
    pspecs = [pl.BlockSpec((None, tr, c), functools.partial(lambda i, kk: (kk, i, 0), kk=kk)) for kk in range(N_CHIP)]
    return pl.pallas_call(
        body, name=name, grid=(r // tr,), in_specs=[wspec] * 3 + pspecs, out_specs=[wspec] * 4,
        out_shape=[jax.ShapeDtypeStruct(w.shape, F32)] * 4, compiler_params=_params(("parallel",)))(
            w, m, v, pieces, pieces, pieces, pieces)


_WEIGHTS = (
    ("meta_tokens", "small", 1), ("ab_norm", "small", None), ("ab_w_in", "big", 2), ("s5_lambda_re", "small", None),
    ("s5_lambda_im", "small", None), ("s5_log_dt", "small", None), ("s5_b_re", "small", None), ("s5_b_im", "small", None),
    ("s5_c_re", "small", None), ("s5_c_im", "small", None), ("s5_d", "small", None), ("s5_glu_w", "big", 1),
    ("s5_glu_b", "small", None), ("ml_conv_w", "small", 2), ("ml_conv_b", "small", None), ("ml_wq", "small", 1),
    ("ml_wk", "small", 1), ("ml_wv", "small", 1), ("ml_w_gate", "small", 1), ("ml_b_gate", "small", None),
    ("ml_norm", "small", None), ("ml_skip", "small", None), ("ab_w_out", "big", 1), ("ssd_norm", "small", 1),
    ("ssd_w_in", "big", 2), ("ssd_conv_w", "small", 2), ("ssd_conv_b", "small", 1), ("ssd_dt_bias", "small", None),
    ("ssd_a_log", "small", None), ("ssd_d", "small", None), ("ssd_gnorm", "small", 1), ("ssd_w_out", "big", 1),
    ("final_norm", "small", None),
)


def _squeeze(a):
    return a[0] if a.ndim >= 3 else a


def kernel(x, meta_tokens, ab_norm, ab_w_in, s5_lambda_re, s5_lambda_im, s5_log_dt, s5_b_re, s5_b_im, s5_c_re, s5_c_im, s5_d, s5_glu_w, s5_glu_b, ml_conv_w, ml_conv_b, ml_wq, ml_wk, ml_wv, ml_w_gate, ml_b_gate, ml_norm, ml_skip, ab_w_out, ssd_norm, ssd_w_in, ssd_conv_w, ssd_conv_b, ssd_dt_bias, ssd_a_log, ssd_d, ssd_gnorm, ssd_w_out, final_norm, loss_target, m_meta_tokens, m_ab_norm, m_ab_w_in, m_s5_lambda_re, m_s5_lambda_im, m_s5_log_dt, m_s5_b_re, m_s5_b_im, m_s5_c_re, m_s5_c_im, m_s5_d, m_s5_glu_w, m_s5_glu_b, m_ml_conv_w, m_ml_conv_b, m_ml_wq, m_ml_wk, m_ml_wv, m_ml_w_gate, m_ml_b_gate, m_ml_norm, m_ml_skip, m_ab_w_out, m_ssd_norm, m_ssd_w_in, m_ssd_conv_w, m_ssd_conv_b, m_ssd_dt_bias, m_ssd_a_log, m_ssd_d, m_ssd_gnorm, m_ssd_w_out, m_final_norm, v_meta_tokens, v_ab_norm, v_ab_w_in, v_s5_lambda_re, v_s5_lambda_im, v_s5_log_dt, v_s5_b_re, v_s5_b_im, v_s5_c_re, v_s5_c_im, v_s5_d, v_s5_glu_w, v_s5_glu_b, v_ml_conv_w, v_ml_conv_b, v_ml_wq, v_ml_wk, v_ml_wv, v_ml_w_gate, v_ml_b_gate, v_ml_norm, v_ml_skip, v_ab_w_out, v_ssd_norm, v_ssd_w_in, v_ssd_conv_w, v_ssd_conv_b, v_ssd_dt_bias, v_ssd_a_log, v_ssd_d, v_ssd_gnorm, v_ssd_w_out, v_final_norm):
    args = (meta_tokens, ab_norm, ab_w_in, s5_lambda_re, s5_lambda_im, s5_log_dt, s5_b_re, s5_b_im, s5_c_re, s5_c_im, s5_d, s5_glu_w, s5_glu_b, ml_conv_w, ml_conv_b, ml_wq, ml_wk, ml_wv, ml_w_gate, ml_b_gate, ml_norm, ml_skip, ab_w_out, ssd_norm, ssd_w_in, ssd_conv_w, ssd_conv_b, ssd_dt_bias, ssd_a_log, ssd_d, ssd_gnorm, ssd_w_out, final_norm)
    m_args = (m_meta_tokens, m_ab_norm, m_ab_w_in, m_s5_lambda_re, m_s5_lambda_im, m_s5_log_dt, m_s5_b_re, m_s5_b_im, m_s5_c_re, m_s5_c_im, m_s5_d, m_s5_glu_w, m_s5_glu_b, m_ml_conv_w, m_ml_conv_b, m_ml_wq, m_ml_wk, m_ml_wv, m_ml_w_gate, m_ml_b_gate, m_ml_norm, m_ml_skip, m_ab_w_out, m_ssd_norm, m_ssd_w_in, m_ssd_conv_w, m_ssd_conv_b, m_ssd_dt_bias, m_ssd_a_log, m_ssd_d, m_ssd_gnorm, m_ssd_w_out, m_final_norm)
    v_args = (v_meta_tokens, v_ab_norm, v_ab_w_in, v_s5_lambda_re, v_s5_lambda_im, v_s5_log_dt, v_s5_b_re, v_s5_b_im, v_s5_c_re, v_s5_c_im, v_s5_d, v_s5_glu_w, v_s5_glu_b, v_ml_conv_w, v_ml_conv_b, v_ml_wq, v_ml_wk, v_ml_wv, v_ml_w_gate, v_ml_b_gate, v_ml_norm, v_ml_skip, v_ab_w_out, v_ssd_norm, v_ssd_w_in, v_ssd_conv_w, v_ssd_conv_b, v_ssd_dt_bias, v_ssd_a_log, v_ssd_d, v_ssd_gnorm, v_ssd_w_out, v_final_norm)
    names = [w[0] for w in _WEIGHTS]
    kind = {w[0]: w[1] for w in _WEIGHTS}
    axis = {w[0]: w[2] for w in _WEIGHTS}
    w_loc = dict(zip(names, args))
    m_loc = dict(zip(names, m_args))
    v_loc = dict(zip(names, v_args))
    chip = 2 * lax.axis_index("x") + lax.axis_index("y")
    core = lax.axis_index("c")
    big = [n for n in names if kind[n] == "big"]
    small = [n for n in names if kind[n] == "small"]
    small_sh = [n for n in small if axis[n] is not None]

    def halves(a):
        return a.astype(BF16).reshape(2, a.shape[1] // 2, a.shape[2])

    def assemble(n, gth):
        shard = gth.reshape((N_CHIP,) + w_loc[n].shape[1:])
        if axis[n] == 1:
            return shard.reshape(-1, shard.shape[2])
        return jnp.concatenate([shard[kk] for kk in range(N_CHIP)], axis=1)

    early = ["ab_w_in", "s5_glu_w"]
    late = ["ab_w_out", "ssd_w_in", "ssd_w_out"]
    gathered = gather_chips([halves(w_loc[n]) for n in early], "gather_early_w")
    after_early = (gathered[0][0, 0, 0, 0] * 0).astype(BF16)
    late_gathered = sequencer_exchange([halves(w_loc[n]) + after_early for n in late], False, 1, "gather_late_w")
    w_in0_shards = gathered[0].reshape((N_CHIP,) + w_loc["ab_w_in"].shape[1:])
    glu_full = assemble("s5_glu_w", gathered[1])

    def columns(shards, lo, hi):
        cw = shards.shape[2]
        parts = [shards[kk][:, max(lo - kk * cw, 0):min(hi - kk * cw, cw)]
                 for kk in range(N_CHIP) if lo < (kk + 1) * cw and hi > kk * cw]
        return parts[0] if len(parts) == 1 else jnp.concatenate(parts, axis=1)

    small_sh_shapes = [w_loc[n].shape for n in small_sh]
    packed_s = _pack([w_loc[n] for n in small_sh], F32, LANES, SUBLANES)
    g8 = all_gather8(packed_s, "gather_small_w").reshape(N_CHIP, 2, -1)
    sp = {}
    for n in small:
        if axis[n] is None:
            sp[n] = _squeeze(w_loc[n])
    per_chip = [_unpack(g8[kk, 0], small_sh_shapes) for kk in range(N_CHIP)]
    for i, n in enumerate(small_sh):
        sp[n] = _squeeze(jnp.concatenate([per_chip[kk][i] for kk in range(N_CHIP)], axis=axis[n]))

    s5w = glu_full.shape[0]
    mlw = w_loc["ab_w_out"].shape[1] * N_CHIP - s5w
    inner = w_loc["ssd_w_out"].shape[1] * N_CHIP
    n_heads1 = sp["ssd_d"].shape[1]
    cdim = w_loc["ssd_w_in"].shape[2] * N_CHIP - inner - n_heads1
    bw = dict(W0a=columns(w_in0_shards, 0, 2 * s5w), W0xb=columns(w_in0_shards, 2 * s5w, 2 * s5w + mlw),
              W0zb=columns(w_in0_shards, 2 * s5w + mlw, 2 * (s5w + mlw)), glu=glu_full)

    def late_weights():
        fb = dict(zip(late, late_gathered))
        w_out0 = assemble("ab_w_out", fb["ab_w_out"])
        w1 = fb["ssd_w_in"].reshape((N_CHIP,) + w_loc["ssd_w_in"].shape[1:])
        return dict(Wo0a=w_out0[:s5w], Wo0b=w_out0[s5w:], W1z=columns(w1, 0, inner),
                    W1x=columns(w1, inner, inner + cdim), W1dt=_pad_lanes(columns(w1, inner + cdim, inner + cdim + n_heads1)),
                    Wo1=assemble("ssd_w_out", fb["ssd_w_out"]))

    def piece_columns(parts, lo, hi):
        out, off = [], 0
        for p in parts:
            a, b = max(lo - off, 0), min(hi - off, p.shape[1])
            if a < b:
                out.append(p[:, a:b])
            off += p.shape[1]
        return out[0] if len(out) == 1 else jnp.concatenate(out, axis=1)

    def chip_halves(n, parts):
        _, r, c_ = w_loc[n].shape
        if axis[n] == 1:
            whole = parts[0] if len(parts) == 1 else jnp.concatenate(parts, axis=0)
            return whole.reshape(N_CHIP, 2, r // 2, c_)
        shards = [piece_columns(parts, kk * c_, (kk + 1) * c_) for kk in range(N_CHIP)]
        return jnp.stack(shards).reshape(N_CHIP, 2, r // 2, c_)

    pieces = {}

    def reduce_group(ns, gfull, tag, collective_id):
        gps = [chip_halves(n, gfull[n]) for n in ns]
        from_sibling = swap_halves(gps, "swap_" + tag)
        partials = [add_halves(gp, oth, core, "add_" + n) for n, gp, oth in zip(ns, gps, from_sibling)]
        pieces.update(zip(ns, sequencer_exchange(partials, True, collective_id, "scatter_" + tag)))

    def late_grads(g):
        gfull = {"ab_w_out": [g["Wo0a"], g["Wo0b"]], "ssd_w_in": [g["W1z"], g["W1x"], g["W1dt"][:, :n_heads1]],
                 "ssd_w_out": [g["Wo1"]]}
        reduce_group(late, gfull, "late_g", 2)

    def early_grads(g):
        gfull = {"ab_w_in": [g["W0a"], g["W0xb"], g["W0zb"]], "s5_glu_w": [g["glu"]]}
        reduce_group(early, gfull, "early_g", 3)

    loss_local, dh0, gbig, gs = _local_step(x, loss_target, bw, sp, late_weights, late_grads, early_grads)
    grad_x = dh0[:, N_META:N_META + x.shape[1]]

    out_g, out_d, out_m, out_v = {}, {}, {}, {}
    small_full_shapes = [sp[n].shape for n in small] + [(1, 1)]
    packed_gs = _pack([gs[n] for n in small] + [loss_local.reshape(1, 1)], F32, LANES, SUBLANES)
    rows_s = packed_gs.shape[0]
    all_gs = sequencer_exchange([jnp.broadcast_to(packed_gs[None], (N_CHIP,) + packed_gs.shape)], True, 4,
                                "gather_small_g")[0].reshape(N_DEV, rows_s, LANES)
    blocks = [all_gs[i] for i in range(N_DEV)]

    for n in late + early:
        pcs = pieces[n].reshape((N_CHIP,) + w_loc[n].shape[1:])
        out_g[n], out_d[n], out_m[n], out_v[n] = adam_big(w_loc[n], m_loc[n], v_loc[n], pcs, "adam_" + n)

    def sum8(i, *b):
        acc = b[0]
        for t in b[1:]:
            acc = acc + t
        return acc

    gsum = rowwise("sum_small_g", sum8, blocks, [], [(LANES, F32)], tr=_tile(rows_s, 512, 8))[0]
    summed = _unpack(gsum, small_full_shapes)
    loss = summed[-1].reshape(())
    g_small = dict(zip(small, summed[:-1]))
    g_loc = {}
    for n in small:
        g = g_small[n].reshape((1,) + g_small[n].shape) if w_loc[n].ndim >= 3 else g_small[n]
        if axis[n] is not None:
            size = w_loc[n].shape[axis[n]]
            g = lax.dynamic_slice_in_dim(g, chip * size, size, axis=axis[n])
        g_loc[n] = g.reshape(w_loc[n].shape)
    loc_shapes = [w_loc[n].shape for n in small]
    pw, pm, pv, pg = (_pack([d[n] for n in small], F32, LANES, SUBLANES) for d in (w_loc, m_loc, v_loc, g_loc))
    dl, mn, vn = rowwise("adam_small", lambda i, a, b, c_, d_: _adam_tile(a, b, c_, d_), [pw, pm, pv, pg], [],
                         [(LANES, F32)] * 3, tr=_tile(pw.shape[0], 512, 8))
    for d_out, flat in ((out_d, dl), (out_m, mn), (out_v, vn)):
        for n, a in zip(small, _unpack(flat, loc_shapes)):
            d_out[n] = a
    for n in small:
        out_g[n] = g_loc[n]

    return (loss, grad_x, *[out_g[n] for n in names], *[out_d[n] for n in names], *[out_m[n] for n in names],
            *[out_v[n] for n in names])
```

```python
import functools
import math

import jax
import jax.numpy as jnp
from jax import lax
from jax.experimental import pallas as pl
from jax.experimental.pallas import tpu as pltpu
from jax.experimental.pallas import tpu_sc as plsc

F32 = jnp.float32
BF16 = jnp.bfloat16
HI = lax.Precision.HIGHEST

D_MODEL = 2048
SEQ = 2048
N_META = 16
CHUNK = 128
NORM_EPS = 1e-6
HEAD_NORM_EPS = 1e-5
S5_GROUP_SIZE = 16
S5_STATE = 64
MLSTM_HEADS = 8
QKV_BLOCK = 4
SSD_HEAD_DIM = 64
SSD_STATE = 128
SSD_HPG = 8
ADAM_LR = 0.001
ADAM_B1 = 0.9
ADAM_B2 = 0.999
ADAM_EPS = 1e-08
ADAM_WD = 0.01
ADAM_STEP = 10

LANES = 128
SUBLANES = 8
VMEM_LIMIT = 56 * 1024 * 1024
MM_OPERAND_VMEM = 34 * 1024 * 1024


def _sigmoid(x):
    return 0.5 * jnp.tanh(0.5 * x) + 0.5


@jax.custom_vjp
def _silu(x):
    return x * _sigmoid(x)


def _silu_fwd(x):
    return x * _sigmoid(x), x


def _silu_bwd(x, ct):
    s = _sigmoid(x)
    return (ct * (s * (1.0 + x * (1.0 - s))),)


_silu.defvjp(_silu_fwd, _silu_bwd)


def _softplus(x):
    return jnp.maximum(x, 0.0) + jnp.log(1.0 + jnp.exp(-jnp.abs(x)))


def _log_sigmoid(x):
    return jnp.minimum(x, 0.0) - jnp.log(1.0 + jnp.exp(-jnp.abs(x)))


def _gelu(x):
    return 0.5 * x * (1.0 + jnp.tanh(math.sqrt(2.0 / math.pi) * (x + 0.044715 * (x * x * x))))


def _dot(a, b, dims, precision=None):
    return lax.dot_general(a, b, (dims, ((), ())), preferred_element_type=F32, precision=precision)


_NN, _NT, _TN = ((1,), (0,)), ((1,), (1,)), ((0,), (0,))


def _bf16_dot(dims, da_rule, db_rule):
    @jax.custom_vjp
    def f(a, b):
        return _dot(a.astype(BF16), b.astype(BF16), dims)

    def fwd(a, b):
        ab, bb = a.astype(BF16), b.astype(BF16)
        return _dot(ab, bb, dims), (ab, bb, jnp.zeros((), a.dtype), jnp.zeros((), b.dtype))

    def bwd(res, ct):
        ab, bb, a_like, b_like = res
        cb = ct.astype(BF16)
        return da_rule(ab, bb, cb).astype(a_like.dtype), db_rule(ab, bb, cb).astype(b_like.dtype)

    f.defvjp(fwd, bwd)
    return f


_dot_nn = _bf16_dot(_NN, lambda a, b, c: _dot(c, b, _NT), lambda a, b, c: _dot(a, c, _TN))
_dot_nt = _bf16_dot(_NT, lambda a, b, c: _dot(c, b, _NN), lambda a, b, c: _dot(c, a, _TN))
_dot_tn = _bf16_dot(_TN, lambda a, b, c: _dot(b, c, _NT), lambda a, b, c: _dot(a, c, _NN))


def _lane_pick(a, idx):
    sel = (lax.broadcasted_iota(jnp.int32, (1, a.shape[1]), 1) == idx).astype(a.dtype)
    return jnp.sum(a * sel, axis=1, keepdims=True)


def _row_pick(a, idx):
    sel = (lax.broadcasted_iota(jnp.int32, (a.shape[0], 1), 0) == idx).astype(a.dtype)
    return jnp.sum(a * sel, axis=0, keepdims=True)


def _tri(n, upper=False):
    r = lax.broadcasted_iota(jnp.int32, (n, n), 0)
    c = lax.broadcasted_iota(jnp.int32, (n, n), 1)
    return ((r <= c) if upper else (r >= c)).astype(F32)


def _tile(n, target, align):
    if n <= target:
        return n
    t = (target // align) * align
    while t >= align:
        if n % t == 0:
            return t
        t -= align
    return n


def _params(sem=None):
    return pltpu.CompilerParams(dimension_semantics=sem, vmem_limit_bytes=VMEM_LIMIT)


def mm(a, b, mode, name, resid=None, out_dtype=F32):
    if mode == "nn":
        (m, k), (k2, n) = a.shape, b.shape
    elif mode == "nt":
        (m, k), (n, k2) = a.shape, b.shape
    else:
        (k, m), (k2, n) = a.shape, b.shape
    assert k == k2, (a.shape, b.shape, mode)
    a_sz, b_sz = a.dtype.itemsize, b.dtype.itemsize
    if mode == "tn":
        tm, tn = _tile(m, 1024, LANES), _tile(n, 1024, LANES)
        tk = _tile(k, MM_OPERAND_VMEM // (2 * (tm * a_sz + tn * b_sz)), 16)
    else:
        tm, tn = _tile(m, 1088, 16), _tile(n, 512, LANES)
        tk = _tile(k, MM_OPERAND_VMEM // (2 * (tm * a_sz + tn * b_sz)), LANES)
    nk = k // tk
    dims = {"nn": ((1,), (0,)), "nt": ((1,), (1,)), "tn": ((0,), (0,))}[mode]
    has_resid = resid is not None

    def body(*refs):
        if has_resid:
            a_ref, b_ref, r_ref, o_ref = refs[:4]
        else:
            a_ref, b_ref, o_ref = refs[:3]
        part = _dot(a_ref[...].astype(BF16), b_ref[...].astype(BF16), dims)

        def finish(res):
            if has_resid:
                res = res + r_ref[...].astype(F32)
            o_ref[...] = res.astype(o_ref.dtype)

        if nk == 1:
            finish(part)
            return
        acc_ref = refs[-1]
        kk = pl.program_id(2)

        @pl.when(kk == 0)
        def _():
            acc_ref[...] = part

        @pl.when(jnp.logical_and(kk > 0, kk < nk - 1))
        def _():
            acc_ref[...] += part

        @pl.when(kk == nk - 1)
        def _():
            finish(acc_ref[...] + part)

    if mode == "tn":
        a_spec = pl.BlockSpec((tk, tm), lambda i, j, kk: (kk, i))
    else:
        a_spec = pl.BlockSpec((tm, tk), lambda i, j, kk: (i, kk))
    if mode == "nt":
        b_spec = pl.BlockSpec((tn, tk), lambda i, j, kk: (j, kk))
    else:
        b_spec = pl.BlockSpec((tk, tn), lambda i, j, kk: (kk, j))
    o_spec = pl.BlockSpec((tm, tn), lambda i, j, kk: (i, j))
    in_specs = [a_spec, b_spec] + ([o_spec] if has_resid else [])
    args = (a, b) + ((resid,) if has_resid else ())
    return pl.pallas_call(
        body, name=name, grid=(m // tm, n // tn, nk), in_specs=in_specs, out_specs=o_spec,
        out_shape=jax.ShapeDtypeStruct((m, n), out_dtype), scratch_shapes=[pltpu.VMEM((tm, tn), F32)] if nk > 1 else [],
        compiler_params=_params(("parallel", "parallel", "arbitrary")))(*args)


def rowwise(name, f, rows, params, outs, accs=(), tr=128):
    n_rows = rows[0].shape[0]
    assert n_rows % tr == 0
    n_r, n_p, n_o, n_a = len(rows), len(params), len(outs), len(accs)

    def body(*refs):
        i = pl.program_id(0)
        r_vals = [r[...] for r in refs[:n_r]]
        p_vals = [r[...] for r in refs[n_r:n_r + n_p]]
        o_refs = refs[n_r + n_p:n_r + n_p + n_o]
        a_refs = refs[n_r + n_p + n_o:]
        res = f(i, *r_vals, *p_vals)
        if not isinstance(res, (tuple, list)):
            res = (res,)
        assert len(res) == n_o + n_a, (name, len(res))
        for o_ref, val in zip(o_refs, res[:n_o]):
            o_ref[...] = val.astype(o_ref.dtype)
        if n_a:
            @pl.when(i == 0)
            def _():
                for a_ref in a_refs:
                    a_ref[...] = jnp.zeros_like(a_ref)

            for a_ref, val in zip(a_refs, res[n_o:]):
                a_ref[...] += val.astype(F32)

    in_specs = [pl.BlockSpec((tr, r.shape[1]), lambda i: (i, 0)) for r in rows]
    in_specs += [pl.BlockSpec(p.shape, lambda i: (0, 0)) for p in params]
    out_specs = [pl.BlockSpec((tr, w), lambda i: (i, 0)) for w, _ in outs]
    out_specs += [pl.BlockSpec(s, lambda i: (0, 0)) for s in accs]
    out_shape = [jax.ShapeDtypeStruct((n_rows, w), dt) for w, dt in outs]
    out_shape += [jax.ShapeDtypeStruct(s, F32) for s in accs]
    res = pl.pallas_call(
        body, name=name, grid=(n_rows // tr,), in_specs=in_specs, out_specs=out_specs, out_shape=out_shape,
        compiler_params=_params(("arbitrary",)))(*rows, *params)
    return res


def _rms(x, g, eps=NORM_EPS):
    return x * lax.rsqrt(jnp.mean(x * x, axis=-1, keepdims=True) + eps) * g


def norm_fwd(x, g, name):
    return rowwise(name, lambda i, xb, gb: _rms(xb, gb), [x], [g], [(x.shape[1], BF16)], tr=_tile(x.shape[0], 256, 16))[0]


def norm_bwd(x, g, dn, resid, name):
    def f(i, xb, dnb, rb, gb):
        _, vjp = jax.vjp(_rms, xb, gb)
        dx, dg = vjp(dnb)
        return dx + rb, dx + rb, dg

    return rowwise(name, f, [x, dn, resid], [g], [(x.shape[1], F32), (x.shape[1], BF16)], [g.shape],
                   tr=_tile(x.shape[0], 256, 16))


def conv_fwd(x, w, b, nb, name):
    rows, width = x.shape
    nc = rows // nb // CHUNK
    tw = _tile(width, 1024, LANES)
    ksz = w.shape[0]

    def body(x_ref, w_ref, b_ref, o_ref, ext_ref):
        c = pl.program_id(2)

        @pl.when(c == 0)
        def _():
            ext_ref[0:SUBLANES, :] = jnp.zeros((SUBLANES, tw), F32)

        taps = [w_ref[j:j + 1, :] for j in range(ksz)]
        bias = b_ref[...]
        row = lax.broadcasted_iota(jnp.int32, (SUBLANES, tw), 0)
        prev_rot = [pltpu.roll(ext_ref[0:SUBLANES, :], k, 0) for k in range(1, ksz)]
        for s in range(CHUNK // SUBLANES):
            r0 = s * SUBLANES
            cur = x_ref[r0:r0 + SUBLANES, :]
            cur_rot = [pltpu.roll(cur, k, 0) for k in range(1, ksz)]
            acc = bias + taps[ksz - 1] * cur
            for k in range(1, ksz):
                acc = acc + taps[ksz - 1 - k] * jnp.where(row >= k, cur_rot[k - 1], prev_rot[k - 1])
            o_ref[r0:r0 + SUBLANES, :] = acc
            prev_rot = cur_rot
        ext_ref[0:SUBLANES, :] = x_ref[CHUNK - SUBLANES:CHUNK, :]

    return pl.pallas_call(
        body, name=name, grid=(width // tw, nb, nc),
        in_specs=[pl.BlockSpec((CHUNK, tw), lambda j, bb, c: (bb * nc + c, j)),
                  pl.BlockSpec((ksz, tw), lambda j, bb, c: (0, j)),
                  pl.BlockSpec((1, tw), lambda j, bb, c: (0, j))],
        out_specs=pl.BlockSpec((CHUNK, tw), lambda j, bb, c: (bb * nc + c, j)),
        out_shape=jax.ShapeDtypeStruct((rows, width), F32),
        scratch_shapes=[pltpu.VMEM((2 * SUBLANES, tw), F32)],
        compiler_params=_params(("arbitrary", "arbitrary", "arbitrary")))(x, w, b)


def conv_bwd(dc, x, w, nb, name, resid=None, dx_dtype=BF16):
    rows, width = x.shape
    nc = rows // nb // CHUNK
    tw = _tile(width, 1024, LANES)
    ksz = w.shape[0]
    per = CHUNK // SUBLANES
    has_resid = resid is not None

    def body(*refs):
        if has_resid:
            dc_ref, x_ref, halo_ref, w_ref, r_ref, dx_ref, dw_ref, db_ref, extd_ref, extx_ref = refs
        else:
            dc_ref, x_ref, halo_ref, w_ref, dx_ref, dw_ref, db_ref, extd_ref, extx_ref = refs
        bb = pl.program_id(1)
        step = pl.program_id(2)
        c = nc - 1 - step

        @pl.when(jnp.logical_and(bb == 0, step == 0))
        def _():
            dw_ref[...] = jnp.zeros_like(dw_ref)
            db_ref[...] = jnp.zeros_like(db_ref)

        @pl.when(step == 0)
        def _():
            extd_ref[SUBLANES:2 * SUBLANES, :] = jnp.zeros((SUBLANES, tw), F32)

        nstrip = CHUNK // SUBLANES
        taps = [w_ref[j:j + 1, :] for j in range(ksz)]
        row = lax.broadcasted_iota(jnp.int32, (SUBLANES, tw), 0)
        x_prev_rot = [pltpu.roll(jnp.where(c == 0, 0.0, halo_ref[...]), k, 0) for k in range(1, ksz)]
        dcs = dc_ref[0:SUBLANES, :]
        dc_rot = [pltpu.roll(dcs, SUBLANES - k, 0) for k in range(1, ksz)]
        for s in range(nstrip):
            r0 = s * SUBLANES
            nxt = extd_ref[SUBLANES:2 * SUBLANES, :] if s == nstrip - 1 else dc_ref[r0 + SUBLANES:r0 + 2 * SUBLANES, :]
            nxt_rot = [pltpu.roll(nxt, SUBLANES - k, 0) for k in range(1, ksz)]
            xc = x_ref[r0:r0 + SUBLANES, :]
            x_rot = [pltpu.roll(xc, k, 0) for k in range(1, ksz)]
            dx = r_ref[r0:r0 + SUBLANES, :].astype(F32) if has_resid else jnp.zeros((SUBLANES, tw), F32)
            dx = dx + taps[ksz - 1] * dcs
            dw_ref[(ksz - 1) * SUBLANES:ksz * SUBLANES, :] += dcs * xc
            for k in range(1, ksz):
                j = ksz - 1 - k
                dx = dx + taps[j] * jnp.where(row < SUBLANES - k, dc_rot[k - 1], nxt_rot[k - 1])
                dw_ref[j * SUBLANES:(j + 1) * SUBLANES, :] += dcs * jnp.where(row >= k, x_rot[k - 1], x_prev_rot[k - 1])
            if s % 2 == 0:
                held = dx
            else:
                dx_ref[r0 - SUBLANES:r0 + SUBLANES, :] = jnp.concatenate([held, dx], axis=0).astype(dx_ref.dtype)
            db_ref[...] += dcs
            dcs, dc_rot, x_prev_rot = nxt, nxt_rot, x_rot
        extd_ref[SUBLANES:2 * SUBLANES, :] = dc_ref[0:SUBLANES, :]

    def blk(j, bb, step):
        return (bb * nc + nc - 1 - step, j)

    def halo(j, bb, step):
        return (jnp.maximum((bb * nc + nc - 1 - step) * per - 1, 0), j)

    in_specs = [pl.BlockSpec((CHUNK, tw), blk), pl.BlockSpec((CHUNK, tw), blk), pl.BlockSpec((SUBLANES, tw), halo),
                pl.BlockSpec((ksz, tw), lambda j, bb, step: (0, j))]
    args = [dc, x, x, w]
    if has_resid:
        in_specs.append(pl.BlockSpec((CHUNK, tw), blk))
        args.append(resid)
    dx, dw_raw, db_raw = pl.pallas_call(
        body, name=name, grid=(width // tw, nb, nc), in_specs=in_specs,
        out_specs=[pl.BlockSpec((CHUNK, tw), blk), pl.BlockSpec((ksz * SUBLANES, tw), lambda j, bb, step: (0, j)),
                   pl.BlockSpec((SUBLANES, tw), lambda j, bb, step: (0, j))],
        out_shape=[jax.ShapeDtypeStruct((rows, width), dx_dtype), jax.ShapeDtypeStruct((ksz * SUBLANES, width), F32),
                   jax.ShapeDtypeStruct((SUBLANES, width), F32)],
        scratch_shapes=[pltpu.VMEM((2 * SUBLANES, tw), F32), pltpu.VMEM((2 * SUBLANES, tw), F32)],
        compiler_params=_params(("arbitrary", "arbitrary", "arbitrary")))(*args)
    return dx, dw_raw.reshape(ksz, SUBLANES, width).sum(axis=1), db_raw.sum(axis=0, keepdims=True)


S5_Q = 4


def _s5_fill_bu(u, bre_ref, bim_ref, xr_ref, xi_ref, ns):
    for s in range(ns):
        ub = u[:, s * LANES:(s + 1) * LANES].astype(BF16)
        bur = _dot(ub, bre_ref[s], ((1,), (0,)))
        bui = _dot(ub, bim_ref[s], ((1,), (0,)))
        for q in range(S5_Q):
            xr_ref[q, pl.ds(s, CHUNK, stride=ns), :] = bur[:, q * LANES:(q + 1) * LANES]
            xi_ref[q, pl.ds(s, CHUNK, stride=ns), :] = bui[:, q * LANES:(q + 1) * LANES]


def _s5_scan(xr_ref, xi_ref, ar_ref, ai_ref, st_ref, ns):
    ar = [ar_ref[q] for q in range(S5_Q)]
    ai = [ai_ref[q] for q in range(S5_Q)]

    def step(t, carry):
        rows = pl.ds(pl.multiple_of(t * ns, ns), ns)
        out = []
        for q in range(S5_Q):
            pr, pi_ = carry[2 * q], carry[2 * q + 1]
            nr = ar[q] * pr - ai[q] * pi_ + xr_ref[q, rows, :]
            ni = ar[q] * pi_ + ai[q] * pr + xi_ref[q, rows, :]
            xr_ref[q, rows, :] = nr
            xi_ref[q, rows, :] = ni
            out += [nr, ni]
        return tuple(out)

    init = []
    for q in range(S5_Q):
        init += [st_ref[0, q], st_ref[1, q]]
    fin = lax.fori_loop(0, CHUNK, step, tuple(init), unroll=2)
    for q in range(S5_Q):
        st_ref[0, q] = fin[2 * q]
        st_ref[1, q] = fin[2 * q + 1]


def s5_fwd(pa, bre, bim, cre, cim, ar, ai, dvec, nb, name):
    rows = pa.shape[0]
    width = pa.shape[1] // 2
    ns = width // LANES
    nc = rows // nb // CHUNK

    def body(u_ref, bre_ref, bim_ref, cre_ref, cim_ref, ar_ref, ai_ref, d_ref, y_ref, g_ref, so_ref, xr_ref, xi_ref, st_ref):
        c = pl.program_id(1)

        @pl.when(c == 0)
        def _():
            st_ref[...] = jnp.zeros_like(st_ref)

        so_ref[...] = st_ref[...]
        u = u_ref[...]
        _s5_fill_bu(u, bre_ref, bim_ref, xr_ref, xi_ref, ns)
        _s5_scan(xr_ref, xi_ref, ar_ref, ai_ref, st_ref, ns)
        for s in range(ns):
            acc = jnp.zeros((CHUNK, LANES), F32)
            for q in range(S5_Q):
                xr = xr_ref[q, pl.ds(s, CHUNK, stride=ns), :].astype(BF16)
                xi = xi_ref[q, pl.ds(s, CHUNK, stride=ns), :].astype(BF16)
                acc = acc + _dot(xr, cre_ref[s, q * LANES:(q + 1) * LANES, :], ((1,), (0,)))
                acc = acc - _dot(xi, cim_ref[s, q * LANES:(q + 1) * LANES, :], ((1,), (0,)))
            cols = slice(s * LANES, (s + 1) * LANES)
            y = acc + d_ref[:, cols] * u[:, cols]
            y_ref[:, cols] = y
            g_ref[:, cols] = _gelu(y).astype(BF16)

    whole3 = lambda a: pl.BlockSpec(a.shape, lambda b_, c: (0, 0, 0))
    return pl.pallas_call(
        body, name=name, grid=(nb, nc),
        in_specs=[pl.BlockSpec((CHUNK, width), lambda b_, c: (b_ * nc + c, 0)), whole3(bre), whole3(bim), whole3(cre),
                  whole3(cim), whole3(ar), whole3(ai), pl.BlockSpec((1, width), lambda b_, c: (0, 0))],
        out_specs=[pl.BlockSpec((CHUNK, width), lambda b_, c: (b_ * nc + c, 0)),
                   pl.BlockSpec((CHUNK, width), lambda b_, c: (b_ * nc + c, 0)),
                   pl.BlockSpec((None, 2, S5_Q, ns, LANES), lambda b_, c: (b_ * nc + c, 0, 0, 0, 0))],
        out_shape=[jax.ShapeDtypeStruct((rows, width), F32), jax.ShapeDtypeStruct((rows, width), BF16),
                   jax.ShapeDtypeStruct((nb * nc, 2, S5_Q, ns, LANES), F32)],
        scratch_shapes=[pltpu.VMEM((S5_Q, CHUNK * ns, LANES), F32), pltpu.VMEM((S5_Q, CHUNK * ns, LANES), F32),
                        pltpu.VMEM((2, S5_Q, ns, LANES), F32)],
        compiler_params=_params(("arbitrary", "arbitrary")))(pa, bre, bim, cre, cim, ar, ai, dvec)


def s5_bwd(pa, dys, states, bre, bim, cre, cim, ar, ai, dvec, nb, name):
    rows = pa.shape[0]
    width = pa.shape[1] // 2
    ns = width // LANES
    nc = rows // nb // CHUNK

    def body(u_ref, dy_ref, sin_ref, bre_ref, bim_ref, cre_ref, cim_ref, ar_ref, ai_ref, d_ref,
             du_ref, dbre_ref, dbim_ref, dcre_ref, dcim_ref, dar_ref, dai_ref, dd_ref,
             xr_ref, xi_ref, lr_ref, li_ref, st_ref, lam_ref):
        bb = pl.program_id(0)
        step_i = pl.program_id(1)

        @pl.when(jnp.logical_and(bb == 0, step_i == 0))
        def _():
            for r in (dbre_ref, dbim_ref, dcre_ref, dcim_ref, dar_ref, dai_ref, dd_ref):
                r[...] = jnp.zeros_like(r)

        @pl.when(step_i == 0)
        def _():
            lam_ref[...] = jnp.zeros_like(lam_ref)

        u = u_ref[...]
        dy = dy_ref[...]
        st_ref[...] = sin_ref[...]
        _s5_fill_bu(u, bre_ref, bim_ref, xr_ref, xi_ref, ns)
        _s5_scan(xr_ref, xi_ref, ar_ref, ai_ref, st_ref, ns)
        dd_ref[...] += jnp.sum(dy * u, axis=0, keepdims=True)
        for s in range(ns):
            dyb = dy[:, s * LANES:(s + 1) * LANES].astype(BF16)
            gr = _dot(dyb, cre_ref[s], ((1,), (1,)))
            gi = -_dot(dyb, cim_ref[s], ((1,), (1,)))
            for q in range(S5_Q):
                lr_ref[q, pl.ds(s, CHUNK, stride=ns), :] = gr[:, q * LANES:(q + 1) * LANES]
                li_ref[q, pl.ds(s, CHUNK, stride=ns), :] = gi[:, q * LANES:(q + 1) * LANES]
                xr = xr_ref[q, pl.ds(s, CHUNK, stride=ns), :].astype(BF16)
                xi = xi_ref[q, pl.ds(s, CHUNK, stride=ns), :].astype(BF16)
                dcre_ref[s, q * LANES:(q + 1) * LANES, :] += _dot(xr, dyb, ((0,), (0,)))
                dcim_ref[s, q * LANES:(q + 1) * LANES, :] -= _dot(xi, dyb, ((0,), (0,)))
        ar = [ar_ref[q] for q in range(S5_Q)]
        ai = [ai_ref[q] for q in range(S5_Q)]

        def one(t_rows, p_r, p_i, carry):
            out = []
            for q in range(S5_Q):
                l_r, l_i, da_r, da_i = carry[4 * q:4 * q + 4]
                n_r = lr_ref[q, t_rows, :] + ar[q] * l_r + ai[q] * l_i
                n_i = li_ref[q, t_rows, :] + ar[q] * l_i - ai[q] * l_r
                lr_ref[q, t_rows, :] = n_r
                li_ref[q, t_rows, :] = n_i
                xpr, xpi = p_r(q), p_i(q)
                out += [n_r, n_i, da_r + n_r * xpr + n_i * xpi, da_i + n_i * xpr - n_r * xpi]
            return tuple(out)

        def step(k, carry):
            t = CHUNK - 1 - k
            t_rows = pl.ds(pl.multiple_of(t * ns, ns), ns)
            p_rows = pl.ds(pl.multiple_of((t - 1) * ns, ns), ns)
            return one(t_rows, lambda q: xr_ref[q, p_rows, :], lambda q: xi_ref[q, p_rows, :], carry)

        init = []
        zero = jnp.zeros((ns, LANES), F32)
        for q in range(S5_Q):
            init += [lam_ref[0, q], lam_ref[1, q], zero, zero]
        carry = lax.fori_loop(0, CHUNK - 1, step, tuple(init), unroll=2)
        carry = one(pl.ds(0, ns), lambda q: sin_ref[0, q], lambda q: sin_ref[1, q], carry)
        for q in range(S5_Q):
            lam_ref[0, q] = carry[4 * q]
            lam_ref[1, q] = carry[4 * q + 1]
            dar_ref[q] += carry[4 * q + 2]
            dai_ref[q] += carry[4 * q + 3]
        for s in range(ns):
            cols = slice(s * LANES, (s + 1) * LANES)
            ub = u[:, cols].astype(BF16)
            acc = d_ref[:, cols] * dy[:, cols]
            for q in range(S5_Q):
                qs = slice(q * LANES, (q + 1) * LANES)
                lr = lr_ref[q, pl.ds(s, CHUNK, stride=ns), :].astype(BF16)
                li = li_ref[q, pl.ds(s, CHUNK, stride=ns), :].astype(BF16)
                dbre_ref[s, :, qs] += _dot(ub, lr, ((0,), (0,)))
                dbim_ref[s, :, qs] += _dot(ub, li, ((0,), (0,)))
                acc = acc + _dot(lr, bre_ref[s, :, qs], ((1,), (1,))) + _dot(li, bim_ref[s, :, qs], ((1,), (1,)))
            du_ref[:, cols] = acc.astype(du_ref.dtype)

    whole3 = lambda a: pl.BlockSpec(a.shape, lambda b_, c: (0, 0, 0))
    rowblk = pl.BlockSpec((CHUNK, width), lambda b_, c: (b_ * nc + nc - 1 - c, 0))
    scr = pltpu.VMEM((S5_Q, CHUNK * ns, LANES), F32)
    return pl.pallas_call(
        body, name=name, grid=(nb, nc),
        in_specs=[rowblk, rowblk,
                  pl.BlockSpec((None, 2, S5_Q, ns, LANES), lambda b_, c: (b_ * nc + nc - 1 - c, 0, 0, 0, 0)),
                  whole3(bre), whole3(bim), whole3(cre), whole3(cim), whole3(ar), whole3(ai),
                  pl.BlockSpec((1, width), lambda b_, c: (0, 0))],
        out_specs=[rowblk, whole3(bre), whole3(bim), whole3(cre), whole3(cim), whole3(ar), whole3(ai),
                   pl.BlockSpec((1, width), lambda b_, c: (0, 0))],
        out_shape=[jax.ShapeDtypeStruct((rows, width), BF16), jax.ShapeDtypeStruct(bre.shape, F32),
                   jax.ShapeDtypeStruct(bim.shape, F32), jax.ShapeDtypeStruct(cre.shape, F32),
                   jax.ShapeDtypeStruct(cim.shape, F32), jax.ShapeDtypeStruct(ar.shape, F32),
                   jax.ShapeDtypeStruct(ai.shape, F32), jax.ShapeDtypeStruct((1, width), F32)],
        scratch_shapes=[scr, scr, scr, scr, pltpu.VMEM((2, S5_Q, ns, LANES), F32), pltpu.VMEM((2, S5_Q, ns, LANES), F32)],
        compiler_params=_params(("arbitrary", "arbitrary")))(pa, dys, states, bre, bim, cre, cim, ar, ai, dvec)


def _s5_discretize(lam_re, lam_im, log_dt, b_re, b_im):
    dt = jnp.exp(log_dt)[:, None]
    mag = jnp.exp(lam_re * dt)
    ar, ai = mag * jnp.cos(lam_im * dt), mag * jnp.sin(lam_im * dt)
    den = lam_re * lam_re + lam_im * lam_im
    qr = ((ar - 1.0) * lam_re + ai * lam_im) / den
    qi = (ai * lam_re - (ar - 1.0) * lam_im) / den
    bbr = qr[..., None] * b_re - qi[..., None] * b_im
    bbi = qr[..., None] * b_im + qi[..., None] * b_re
    return ar, ai, bbr, bbi


def _s5_expand(ar, ai, bbr, bbi, c_re, c_im):
    g, p, h = bbr.shape
    gps = LANES // h
    ns = g // gps
    eye = jnp.eye(gps, dtype=F32)

    def bexp(b):
        return jnp.einsum("sgph,gk->sghkp", b.reshape(ns, gps, p, h), eye).reshape(ns, gps * h, gps * p)

    def cexp(c):
        return jnp.einsum("sghp,gk->sgpkh", c.reshape(ns, gps, h, p), eye).reshape(ns, gps * p, gps * h)

    def aexp(a):
        return a.reshape(ns, S5_Q, LANES).transpose(1, 0, 2)

    return (bexp(bbr).astype(BF16), bexp(bbi).astype(BF16), cexp(c_re).astype(BF16), cexp(c_im).astype(BF16),
            aexp(ar), aexp(ai))


def _s5_contract(dbre, dbim, dcre, dcim, dar, dai, g, p, h):
    gps = LANES // h
    ns = g // gps
    eye = jnp.eye(gps, dtype=F32)
    bcon = lambda d: jnp.einsum("sghkp,gk->sgph", d.reshape(ns, gps, h, gps, p), eye).reshape(g, p, h)
    ccon = lambda d: jnp.einsum("sgpkh,gk->sghp", d.reshape(ns, gps, p, gps, h), eye).reshape(g, h, p)
    acon = lambda d: d.transpose(1, 0, 2).reshape(g, p)
    return bcon(dbre), bcon(dbim), ccon(dcre), ccon(dcim), acon(dar), acon(dai)


PROJ_BLOCK = 256


def _ml_proj_tile(cpre, xb, wq, wk, wv, gq, gk, gv):
    xc = _silu(cpre)
    q = _dot_nn(xc, wq)
    k = _dot_nn(xc, wk)
    v = _dot_nn(xb, wv)
    return q, k, v, _dot_nn(q, gq) + _dot_nn(k, gk) + _dot_nn(v, gv)


def ml_proj_fwd(cpre, xb, wq, wk, wv, gq, gk, gv, name):
    rows, width = cpre.shape
    pb = wq.shape[1]
    nblk = width // pb
    tr = _tile(rows, 1088, 16)

    def body(c_ref, x_ref, wq_ref, wk_ref, wv_ref, gq_ref, gk_ref, gv_ref, q_ref, k_ref, v_ref, g_ref):
        j = pl.program_id(1)
        q, k, v, g = _ml_proj_tile(c_ref[...], x_ref[...], wq_ref[...], wk_ref[...], wv_ref[...],
                                   gq_ref[...], gk_ref[...], gv_ref[...])
        q_ref[...] = q
        k_ref[...] = k
        v_ref[...] = v

        @pl.when(j == 0)
        def _():
            g_ref[...] = jnp.zeros_like(g_ref)

        g_ref[...] += g

    rb = pl.BlockSpec((tr, pb), lambda i, j: (i, j))
    wb = pl.BlockSpec((None, pb, pb), lambda i, j: (j, 0, 0))
    gwb = pl.BlockSpec((None, pb, LANES), lambda i, j: (j, 0, 0))
    return pl.pallas_call(
        body, name=name, grid=(rows // tr, nblk), in_specs=[rb, rb, wb, wb, wb, gwb, gwb, gwb],
        out_specs=[rb, rb, rb, pl.BlockSpec((tr, LANES), lambda i, j: (i, 0))],
        out_shape=[jax.ShapeDtypeStruct((rows, width), F32)] * 3 + [jax.ShapeDtypeStruct((rows, LANES), F32)],
        compiler_params=_params(("arbitrary", "arbitrary")))(cpre, xb, wq, wk, wv, gq, gk, gv)


def ml_proj_bwd(cpre, xb, wq, wk, wv, gq, gk, gv, dq, dk, dv, dg, dcp_extra, name):
    rows, width = cpre.shape
    pb = wq.shape[1]
    nblk = width // pb
    tr = _tile(rows, 1088, 16)

    def body(c_ref, x_ref, wq_ref, wk_ref, wv_ref, gq_ref, gk_ref, gv_ref, dq_ref, dk_ref, dv_ref, dg_ref, e_ref,
             dc_ref, dx_ref, *dw_refs):
        i = pl.program_id(1)
        _, vjp = jax.vjp(_ml_proj_tile, c_ref[...], x_ref[...], wq_ref[...], wk_ref[...], wv_ref[...],
                         gq_ref[...], gk_ref[...], gv_ref[...])
        grads = vjp((dq_ref[...], dk_ref[...], dv_ref[...], dg_ref[...]))
        dc_ref[...] = grads[0] + e_ref[...]
        dx_ref[...] = grads[1]

        @pl.when(i == 0)
        def _():
            for r in dw_refs:
                r[...] = jnp.zeros_like(r)

        for r, gval in zip(dw_refs, grads[2:]):
            r[...] += gval

    rb = pl.BlockSpec((tr, pb), lambda j, i: (i, j))
    wb = pl.BlockSpec((None, pb, pb), lambda j, i: (j, 0, 0))
    gwb = pl.BlockSpec((None, pb, LANES), lambda j, i: (j, 0, 0))
    gb = pl.BlockSpec((tr, LANES), lambda j, i: (i, 0))
    wshape = jax.ShapeDtypeStruct((nblk, pb, pb), F32)
    gshape = jax.ShapeDtypeStruct((nblk, pb, LANES), F32)
    return pl.pallas_call(
        body, name=name, grid=(nblk, rows // tr), in_specs=[rb, rb, wb, wb, wb, gwb, gwb, gwb, rb, rb, rb, gb, rb],
        out_specs=[rb, rb] + [wb] * 3 + [gwb] * 3,
        out_shape=[jax.ShapeDtypeStruct((rows, width), F32)] * 2 + [wshape] * 3 + [gshape] * 3,
        compiler_params=_params(("arbitrary", "arbitrary")))(cpre, xb, wq, wk, wv, gq, gk, gv, dq, dk, dv, dg, dcp_extra)


def _ml_gates_tile(gl, bg, nh):
    x = gl + bg
    bcum = _dot(_tri(CHUNK), _log_sigmoid(x), ((1,), (0,)), precision=HI)
    lane = lax.broadcasted_iota(jnp.int32, x.shape, 1)
    return jnp.where(lane < nh, x, jnp.where(lane < 2 * nh, bcum, 0.0))


def _ml_core_tile(q, k, v, colg, rowg, cpre, zb, nw, sk, cst, nst, m_prev):
    c, dh = q.shape
    igc, bc = _lane_pick(colg, 0), _lane_pick(colg, 1)
    igr, br = _row_pick(rowg, 0), _row_pick(rowg, 1)
    causal = _tri(c) > 0
    dmat = jnp.where(causal, bc - br + igr, -jnp.inf)
    inter = bc + m_prev
    mt = lax.stop_gradient(jnp.maximum(inter, jnp.max(dmat, axis=1, keepdims=True)))
    wt = jnp.exp(dmat - mt)
    w_prev = jnp.exp(inter - mt)
    qs = q * (dh ** -0.5)
    s = _dot_nt(qs, k) * wt
    num = _dot_nn(s, v) + w_prev * _dot_nn(qs, cst)
    den = jnp.sum(s, axis=1, keepdims=True) + w_prev * jnp.sum(qs * nst, axis=1, keepdims=True)
    h = num * (1.0 / jnp.maximum(jnp.abs(den), jnp.exp(-mt)))
    last = (lax.broadcasted_iota(jnp.int32, (c, 1), 0) == c - 1).astype(F32)
    blast = jnp.sum(bc * last, axis=0, keepdims=True)
    g = blast - bc + igc
    m_new = lax.stop_gradient(jnp.maximum(blast + m_prev, jnp.max(g, axis=0, keepdims=True)))
    decay = jnp.exp(blast + m_prev - m_new)
    wk = jnp.exp(g - m_new) * k
    c_new = decay * cst + _dot_tn(wk, v)
    n_new = decay * nst + jnp.sum(wk, axis=0, keepdims=True)
    mu = jnp.mean(h, axis=1, keepdims=True)
    hc = h - mu
    var = jnp.mean(hc * hc, axis=1, keepdims=True)
    out = hc * lax.rsqrt(var + HEAD_NORM_EPS) * nw + sk * _silu(cpre)
    return out * _silu(zb), c_new, n_new, m_new


def _ml_core_specs(nc, dh, rev):
    ch = (lambda c: nc - 1 - c) if rev else (lambda c: c)
    rb = pl.BlockSpec((CHUNK, dh), lambda b_, c, h: (b_ * nc + ch(c), h))
    colb = pl.BlockSpec((None, CHUNK, 2), lambda b_, c, h: (h, b_ * nc + ch(c), 0))
    rowb = pl.BlockSpec((None, None, 2, CHUNK), lambda b_, c, h: (b_ * nc + ch(c), h, 0, 0))
    pb = pl.BlockSpec((1, dh), lambda b_, c, h: (0, h))
    cb = pl.BlockSpec((None, None, dh, dh), lambda b_, c, h: (b_ * nc + ch(c), h, 0, 0))
    nb_ = pl.BlockSpec((None, None, 1, dh), lambda b_, c, h: (b_ * nc + ch(c), h, 0, 0))
    mb = pl.BlockSpec((None, None, 1, 1), lambda b_, c, h: (b_ * nc + ch(c), h, 0, 0))
    return rb, colb, rowb, pb, cb, nb_, mb


def ml_core_fwd(q, k, v, colg, rowg, cpre, zb, nw, sk, nb, nh, name):
    rows, width = q.shape
    dh = width // nh
    nc = rows // nb // CHUNK
    rb, colb, rowb, pb, cb, nb_, mb = _ml_core_specs(nc, dh, False)

    def body(q_ref, k_ref, v_ref, col_ref, row_ref, c_ref, z_ref, nw_ref, sk_ref, y_ref, cs_ref, ns_ref, ms_ref,
             cst_ref, nst_ref, mst_ref):
        c = pl.program_id(1)
        h = pl.program_id(2)

        @pl.when(c == 0)
        def _():
            cst_ref[h] = jnp.zeros((dh, dh), F32)
            nst_ref[h] = jnp.zeros((1, dh), F32)
            mst_ref[h] = jnp.zeros((1, 1), F32)

        cst, nst, m_prev = cst_ref[h], nst_ref[h], mst_ref[h]
        cs_ref[...] = cst
        ns_ref[...] = nst
        ms_ref[...] = m_prev
        y, c_new, n_new, m_new = _ml_core_tile(q_ref[...], k_ref[...], v_ref[...], col_ref[...], row_ref[...],
                                               c_ref[...], z_ref[...], nw_ref[...], sk_ref[...], cst, nst, m_prev)
        y_ref[...] = y.astype(BF16)
        cst_ref[h] = c_new
        nst_ref[h] = n_new
        mst_ref[h] = m_new

    nbc = nb * nc
    return pl.pallas_call(
        body, name=name, grid=(nb, nc, nh), in_specs=[rb, rb, rb, colb, rowb, rb, rb, pb, pb],
        out_specs=[rb, cb, nb_, mb],
        out_shape=[jax.ShapeDtypeStruct((rows, width), BF16), jax.ShapeDtypeStruct((nbc, nh, dh, dh), F32),
                   jax.ShapeDtypeStruct((nbc, nh, 1, dh), F32), jax.ShapeDtypeStruct((nbc, nh, 1, 1), F32)],
        scratch_shapes=[pltpu.VMEM((nh, dh, dh), F32), pltpu.VMEM((nh, 1, dh), F32), pltpu.VMEM((nh, 1, 1), F32)],
        compiler_params=_params(("arbitrary", "arbitrary", "arbitrary")))(q, k, v, colg, rowg, cpre, zb, nw, sk)


def ml_core_bwd(q, k, v, colg, rowg, cpre, zb, nw, sk, cs, ns, ms, dy, nb, nh, name):
    rows, width = q.shape
    dh = width // nh
    nc = rows // nb // CHUNK
    rb, colb, rowb, pb, cb, nb_, mb = _ml_core_specs(nc, dh, True)

    def body(q_ref, k_ref, v_ref, col_ref, row_ref, c_ref, z_ref, nw_ref, sk_ref, cs_ref, ns_ref, ms_ref, dy_ref,
             dq_ref, dk_ref, dv_ref, dc_ref, dz_ref, dcol_ref, drow_ref, dnw_ref, dsk_ref, dcst_ref, dnst_ref):
        bb = pl.program_id(0)
        step = pl.program_id(1)
        h = pl.program_id(2)

        @pl.when(jnp.logical_and(bb == 0, jnp.logical_and(step == 0, h == 0)))
        def _():
            dnw_ref[...] = jnp.zeros_like(dnw_ref)
            dsk_ref[...] = jnp.zeros_like(dsk_ref)

        @pl.when(step == 0)
        def _():
            dcst_ref[h] = jnp.zeros((dh, dh), F32)
            dnst_ref[h] = jnp.zeros((1, dh), F32)

        m_prev = ms_ref[...]

        def f(*a):
            return _ml_core_tile(*a, m_prev)[:3]

        _, vjp = jax.vjp(f, q_ref[...], k_ref[...], v_ref[...], col_ref[...], row_ref[...], c_ref[...], z_ref[...],
                         nw_ref[...], sk_ref[...], cs_ref[...], ns_ref[...])
        g = vjp((dy_ref[...], dcst_ref[h], dnst_ref[h]))
        dq_ref[...] = g[0]
        dk_ref[...] = g[1]
        dv_ref[...] = g[2]
        dcol_ref[...] = g[3]
        drow_ref[...] = g[4]
        dc_ref[...] = g[5]
        dz_ref[...] = g[6].astype(dz_ref.dtype)
        dnw_ref[h] += g[7]
        dsk_ref[h] += g[8]
        dcst_ref[h] = g[9]
        dnst_ref[h] = g[10]

    nbc = nb * nc
    accb = pl.BlockSpec((nh, 1, dh), lambda b_, c, h: (0, 0, 0))
    return pl.pallas_call(
        body, name=name, grid=(nb, nc, nh), in_specs=[rb, rb, rb, colb, rowb, rb, rb, pb, pb, cb, nb_, mb, rb],
        out_specs=[rb, rb, rb, rb, rb, colb, rowb, accb, accb],
        out_shape=[jax.ShapeDtypeStruct((rows, width), F32)] * 4 + [jax.ShapeDtypeStruct((rows, width), BF16)]
        + [jax.ShapeDtypeStruct(colg.shape, F32), jax.ShapeDtypeStruct(rowg.shape, F32),
           jax.ShapeDtypeStruct((nh, 1, dh), F32), jax.ShapeDtypeStruct((nh, 1, dh), F32)],
        scratch_shapes=[pltpu.VMEM((nh, dh, dh), F32), pltpu.VMEM((nh, 1, dh), F32)],
        compiler_params=_params(("arbitrary", "arbitrary", "arbitrary")))(
            q, k, v, colg, rowg, cpre, zb, nw, sk, cs, ns, ms, dy)


def _ssd_dt_tile(dtr, bias, alog):
    dt = _softplus(dtr + bias)
    cum = _dot(_tri(CHUNK), dt * (-jnp.exp(alog)), ((1,), (0,)), precision=HI)
    return dt, cum


def _ssd_tile(xcs, bmc, cmc, cols, rows_, z, dvec, gn, states, hpg):
    npair = hpg // 2
    hd = SSD_HEAD_DIM
    xs = [_silu(x) for x in xcs]
    bm, cm = _silu(bmc), _silu(cmc)
    cb = _dot_nt(cm, bm)
    causal = _tri(CHUNK) > 0
    lane_lo = lax.broadcasted_iota(jnp.int32, (1, 2 * hd), 1) < hd
    lastsel = (lax.broadcasted_iota(jnp.int32, (CHUNK, 1), 0) == CHUNK - 1).astype(F32)
    heads = []
    for r in range(hpg):
        dtc, cumc = _lane_pick(cols, r), _lane_pick(cols, hpg + r)
        dtrow, cumr = _row_pick(rows_, r), _row_pick(rows_, hpg + r)
        w = cb * jnp.exp(jnp.where(causal, cumc - cumr, -jnp.inf)) * dtrow
        last = jnp.sum(cumc * lastsel, axis=0, keepdims=True)
        heads.append((w, jnp.exp(cumc), jnp.exp(last - cumc) * dtc, jnp.exp(last)))
    ys, new_states = [], []
    for j in range(npair):
        (wa, ea, da, la), (wb, eb, db, lb) = heads[2 * j], heads[2 * j + 1]
        yi = jnp.where(lane_lo, _dot_nn(wa, xs[j]), _dot_nn(wb, xs[j]))
        ys.append(yi + jnp.where(lane_lo, ea, eb) * _dot_nn(cm, states[j]))
        xd = xs[j] * jnp.where(lane_lo, da, db)
        new_states.append(jnp.where(lane_lo, la, lb) * states[j] + _dot_tn(bm, xd))
    y = jnp.concatenate(ys, axis=1) + dvec * jnp.concatenate(xs, axis=1)
    yg = y * _silu(z)
    yn = yg * lax.rsqrt(jnp.mean(yg * yg, axis=1, keepdims=True) + NORM_EPS) * gn
    return yn, new_states


def _ssd_specs(nc, hpg, ng, rev):
    npair = hpg // 2
    gw = hpg * SSD_HEAD_DIM
    xblocks = ng * npair
    ch = (lambda c: nc - 1 - c) if rev else (lambda c: c)
    xs = [pl.BlockSpec((CHUNK, LANES), functools.partial(lambda b_, c, g, jj: (b_ * nc + ch(c), g * npair + jj), jj=j))
          for j in range(npair)]
    bmb = pl.BlockSpec((CHUNK, SSD_STATE), lambda b_, c, g: (b_ * nc + ch(c), xblocks + g))
    cmb = pl.BlockSpec((CHUNK, SSD_STATE), lambda b_, c, g: (b_ * nc + ch(c), xblocks + ng + g))
    colb = pl.BlockSpec((None, CHUNK, 2 * hpg), lambda b_, c, g: (g, b_ * nc + ch(c), 0))
    rowb = pl.BlockSpec((None, None, 2 * hpg, CHUNK), lambda b_, c, g: (b_ * nc + ch(c), g, 0, 0))
    zb = pl.BlockSpec((CHUNK, gw), lambda b_, c, g: (b_ * nc + ch(c), g))
    pb = pl.BlockSpec((1, gw), lambda b_, c, g: (0, g))
    sb = pl.BlockSpec((None, None, npair, SSD_STATE, 2 * SSD_HEAD_DIM), lambda b_, c, g: (b_ * nc + ch(c), g, 0, 0, 0))
    return xs, bmb, cmb, colb, rowb, zb, pb, sb


def ssd_core_fwd(cpre, cols, rows_, z, dvec, gn, nb, hpg, name):
    rows = cpre.shape[0]
    inner = z.shape[1]
    ng = inner // (hpg * SSD_HEAD_DIM)
    npair = hpg // 2
    nc = rows // nb // CHUNK
    xs, bmb, cmb, colb, rowb, zb, pb, sb = _ssd_specs(nc, hpg, ng, False)

    def body(*refs):
        x_refs = refs[:npair]
        bm_ref, cm_ref, col_ref, row_ref, z_ref, d_ref, gn_ref, y_ref, so_ref, st_ref = refs[npair:]
        c = pl.program_id(1)
        g = pl.program_id(2)

        @pl.when(c == 0)
        def _():
            st_ref[g] = jnp.zeros((npair, SSD_STATE, 2 * SSD_HEAD_DIM), F32)

        so_ref[...] = st_ref[g]
        states = [st_ref[g, j] for j in range(npair)]
        yn, new_states = _ssd_tile([r[...] for r in x_refs], bm_ref[...], cm_ref[...], col_ref[...], row_ref[...],
                                   z_ref[...], d_ref[...], gn_ref[...], states, hpg)
        y_ref[...] = yn.astype(BF16)
        for j in range(npair):
            st_ref[g, j] = new_states[j]

    return pl.pallas_call(
        body, name=name, grid=(nb, nc, ng), in_specs=xs + [bmb, cmb, colb, rowb, zb, pb, pb],
        out_specs=[zb, sb],
        out_shape=[jax.ShapeDtypeStruct((rows, inner), BF16),
                   jax.ShapeDtypeStruct((nb * nc, ng, npair, SSD_STATE, 2 * SSD_HEAD_DIM), F32)],
        scratch_shapes=[pltpu.VMEM((ng, npair, SSD_STATE, 2 * SSD_HEAD_DIM), F32)],
        compiler_params=_params(("arbitrary", "arbitrary", "arbitrary")))(
            *([cpre] * npair), cpre, cpre, cols, rows_, z, dvec, gn)


def ssd_core_bwd(cpre, cols, rows_, z, dvec, gn, states, dyn, nb, hpg, name):
    rows = cpre.shape[0]
    inner = z.shape[1]
    gw = hpg * SSD_HEAD_DIM
    ng = inner // gw
    npair = hpg // 2
    nc = rows // nb // CHUNK
    xs, bmb, cmb, colb, rowb, zb, pb, sb = _ssd_specs(nc, hpg, ng, True)

    def body(*refs):
        x_refs = refs[:npair]
        (bm_ref, cm_ref, col_ref, row_ref, z_ref, d_ref, gn_ref, s_ref, dy_ref,
         dx_ref, dbm_ref, dcm_ref, dcol_ref, drow_ref, dz_ref, dd_ref, dgn_ref, dst_ref) = refs[npair:]
        bb = pl.program_id(0)
        step = pl.program_id(1)
        g = pl.program_id(2)

        @pl.when(jnp.logical_and(bb == 0, jnp.logical_and(step == 0, g == 0)))
        def _():
            dd_ref[...] = jnp.zeros_like(dd_ref)
            dgn_ref[...] = jnp.zeros_like(dgn_ref)

        @pl.when(step == 0)
        def _():
            dst_ref[g] = jnp.zeros((npair, SSD_STATE, 2 * SSD_HEAD_DIM), F32)

        def f(xcs, bmc, cmc, cv, rv, zv, dv_, gv, sts):
            return _ssd_tile(xcs, bmc, cmc, cv, rv, zv, dv_, gv, sts, hpg)

        _, vjp = jax.vjp(f, [r[...] for r in x_refs], bm_ref[...], cm_ref[...], col_ref[...], row_ref[...], z_ref[...],
                         d_ref[...], gn_ref[...], [s_ref[j] for j in range(npair)])
        gr = vjp((dy_ref[...], [dst_ref[g, j] for j in range(npair)]))
        dx_ref[...] = jnp.concatenate(gr[0], axis=1)
        dbm_ref[...] = gr[1]
        dcm_ref[...] = gr[2]
        dcol_ref[...] = gr[3]
        drow_ref[...] = gr[4]
        dz_ref[...] = gr[5].astype(dz_ref.dtype)
        dd_ref[g] += gr[6]
        dgn_ref[g] += gr[7]
        for j in range(npair):
            dst_ref[g, j] = gr[8][j]

    ch = lambda c: nc - 1 - c
    nblk = pl.BlockSpec((CHUNK, SSD_STATE), lambda b_, c, g: (b_ * nc + ch(c), g))
    accb = pl.BlockSpec((ng, 1, gw), lambda b_, c, g: (0, 0, 0))
    return pl.pallas_call(
        body, name=name, grid=(nb, nc, ng), in_specs=xs + [bmb, cmb, colb, rowb, zb, pb, pb, sb, zb],
        out_specs=[zb, nblk, nblk, colb, rowb, zb, accb, accb],
        out_shape=[jax.ShapeDtypeStruct((rows, inner), F32), jax.ShapeDtypeStruct((rows, ng * SSD_STATE), F32),
                   jax.ShapeDtypeStruct((rows, ng * SSD_STATE), F32), jax.ShapeDtypeStruct(cols.shape, F32),
                   jax.ShapeDtypeStruct(rows_.shape, F32), jax.ShapeDtypeStruct((rows, inner), BF16),
                   jax.ShapeDtypeStruct((ng, 1, gw), F32), jax.ShapeDtypeStruct((ng, 1, gw), F32)],
        scratch_shapes=[pltpu.VMEM((ng, npair, SSD_STATE, 2 * SSD_HEAD_DIM), F32)],
        compiler_params=_params(("arbitrary", "arbitrary", "arbitrary")))(
            *([cpre] * npair), cpre, cpre, cols, rows_, z, dvec, gn, states, dyn)


def _hw_expand(w):
    n, bi, _ = w.shape
    per = PROJ_BLOCK // bi
    tiled = jnp.tile(w.reshape(n // per, PROJ_BLOCK, bi), (1, 1, per))
    return jnp.where(_hw_mask(bi), tiled, 0.0)


def _hw_mask(bi):
    r = lax.broadcasted_iota(jnp.int32, (PROJ_BLOCK, PROJ_BLOCK), 0) // bi
    c = lax.broadcasted_iota(jnp.int32, (PROJ_BLOCK, PROJ_BLOCK), 1) // bi
    return r == c


def _hw_contract(d, bi=QKV_BLOCK):
    per = PROJ_BLOCK // bi
    kept = jnp.where(_hw_mask(bi), d, 0.0)
    return kept.reshape(d.shape[0], PROJ_BLOCK, per, bi).sum(axis=2).reshape(-1, bi, bi)


def _wg_expand(wg, width):
    pad = jnp.pad(wg, ((0, 0), (0, LANES - wg.shape[1])))
    return [pad[i * width:(i + 1) * width].reshape(width // PROJ_BLOCK, PROJ_BLOCK, LANES) for i in range(3)]


def _wg_contract(dgs, ngate):
    return jnp.concatenate([d[:, :, :ngate].reshape(-1, ngate) for d in dgs], axis=0)


def _pad_lanes(a):
    return jnp.pad(a, ((0, 0), (0, LANES - a.shape[1])))


def _pairs_to_layouts(first, second, ngrp, per, nbc):
    rows = first.shape[0]
    both = jnp.concatenate([first.reshape(rows, ngrp, per), second.reshape(rows, ngrp, per)], axis=2)
    return both.transpose(1, 0, 2), both.reshape(nbc, CHUNK, ngrp, 2 * per).transpose(0, 2, 3, 1)


def _layouts_to_pairs(dcols, drows, ngrp, per):
    rows = dcols.shape[1]
    both = dcols.transpose(1, 0, 2) + drows.transpose(0, 3, 1, 2).reshape(rows, ngrp, 2 * per)
    return both[:, :, :per].reshape(rows, ngrp * per), both[:, :, per:].reshape(rows, ngrp * per)


_EARLY = ("W0a", "W0xb", "W0zb", "glu")
_LATE = ("Wo0a", "Wo0b", "W1z", "W1x", "W1dt", "Wo1")


def _local_step(x, target, bw, sp, late_weights=None, late_grads=None, early_grads=None):
    nb, seq, d = x.shape
    nh, hpg = MLSTM_HEADS, SSD_HPG
    t_len = N_META + seq
    nc = -(-t_len // CHUNK)
    tp = nc * CHUNK
    rows = nb * tp
    nbc = nb * nc
    meta = sp["meta_tokens"]
    h0 = jnp.concatenate([jnp.broadcast_to(meta[None], (nb, N_META, d)), x, jnp.zeros((nb, tp - t_len, d), F32)], axis=1)
    h0 = h0.reshape(rows, d)
    tgt = jnp.pad(target, ((0, 0), (N_META, tp - t_len), (0, 0))).reshape(rows, d)

    n0 = norm_fwd(h0, sp["ab_norm"], "norm0")
    pa = mm(n0, bw["W0a"], "nn", "mm_pa")
    xb = mm(n0, bw["W0xb"], "nn", "mm_xb")
    zb = mm(n0, bw["W0zb"], "nn", "mm_zb")
    s5w = pa.shape[1] // 2
    mlw = xb.shape[1]
    s5_args = (sp["s5_lambda_re"], sp["s5_lambda_im"], sp["s5_log_dt"].reshape(-1), sp["s5_b_re"], sp["s5_b_im"])
    (ar, ai, bbr, bbi), s5_disc_vjp = jax.vjp(_s5_discretize, *s5_args)
    sg, spn, shh = bbr.shape
    bre, bim, cre, cim, are, aie = _s5_expand(ar, ai, bbr, bbi, sp["s5_c_re"], sp["s5_c_im"])
    ys5, gb, s5st = s5_fwd(pa, bre, bim, cre, cim, are, aie, sp["s5_d"], nb, "s5_fwd")
    tglu = mm(gb, bw["glu"], "nn", "mm_glu")

    def glu_tile(ys, tt, za, gbias):
        return _gelu(ys) * _sigmoid(tt + gbias) * _silu(za)

    ya = rowwise("glu_fwd", lambda i, ys, tt, pab, gbias: glu_tile(ys, tt, pab[:, s5w:], gbias),
                 [ys5, tglu, pa], [sp["s5_glu_b"]], [(s5w, BF16)], tr=_tile(rows, 256, 16))[0]

    cpre0 = conv_fwd(xb, sp["ml_conv_w"], sp["ml_conv_b"], nb, "ml_conv_fwd")
    wq_e, wk_e, wv_e = _hw_expand(sp["ml_wq"]), _hw_expand(sp["ml_wk"]), _hw_expand(sp["ml_wv"])
    gq, gk, gv = _wg_expand(sp["ml_w_gate"], mlw)
    q, k, v, gl = ml_proj_fwd(cpre0, xb, wq_e, wk_e, wv_e, gq, gk, gv, "ml_proj_fwd")
    bgate = _pad_lanes(sp["ml_b_gate"])
    gout = rowwise("ml_gates_fwd", lambda i, g_, b_: _ml_gates_tile(g_, b_, nh), [gl], [bgate], [(LANES, F32)], tr=CHUNK)[0]
    colg, rowg = _pairs_to_layouts(gout[:, :nh], gout[:, nh:2 * nh], nh, 1, nbc)
    yb, ml_cs, ml_ns, ml_ms = ml_core_fwd(q, k, v, colg, rowg, cpre0, zb, sp["ml_norm"], sp["ml_skip"], nb, nh, "ml_core_fwd")
    if late_weights is not None:
        bw = {**bw, **late_weights()}
    h1 = mm(ya, bw["Wo0a"], "nn", "mm_out0a", resid=h0)
    h1 = mm(yb, bw["Wo0b"], "nn", "mm_out0b", resid=h1)

    n1 = norm_fwd(h1, sp["ssd_norm"], "norm1")
    z1 = mm(n1, bw["W1z"], "nn", "mm_z1")
    xbc = mm(n1, bw["W1x"], "nn", "mm_xbc")
    dtr = mm(n1, bw["W1dt"], "nn", "mm_dt")
    inner = z1.shape[1]
    ng = inner // (hpg * SSD_HEAD_DIM)
    nhd = ng * hpg
    cpre1 = conv_fwd(xbc, sp["ssd_conv_w"], sp["ssd_conv_b"], nb, "ssd_conv_fwd")
    dt_bias, a_log = _pad_lanes(sp["ssd_dt_bias"]), _pad_lanes(sp["ssd_a_log"])
    dt, cum = rowwise("ssd_dt_fwd", lambda i, r_, b_, a_: _ssd_dt_tile(r_, b_, a_), [dtr], [dt_bias, a_log],
                      [(LANES, F32), (LANES, F32)], tr=CHUNK)
    cols, rws = _pairs_to_layouts(dt[:, :nhd], cum[:, :nhd], ng, hpg, nbc)
    dvec = jnp.repeat(sp["ssd_d"], SSD_HEAD_DIM, axis=1)
    yn, ssd_st = ssd_core_fwd(cpre1, cols, rws, z1, dvec, sp["ssd_gnorm"], nb, hpg, "ssd_core_fwd")
    h2 = mm(yn, bw["Wo1"], "nn", "mm_out1", resid=h1)

    tr_l = _tile(tp, 256, 16)
    per_ex = tp // tr_l

    def loss_tile(i, hb, tb, gfn):
        tpos = (i % per_ex) * tr_l + lax.broadcasted_iota(jnp.int32, (tr_l, 1), 0)
        mask = jnp.logical_and(tpos >= N_META, tpos < t_len).astype(F32)

        def lf(hh, gg):
            e = (_rms(hh, gg) - tb) * mask
            return 0.5 * jnp.sum(e * e) / d

        lval, (dh, dg) = jax.value_and_grad(lf, (0, 1))(hb, gfn)
        return dh, dh, jnp.full((1, LANES), lval, F32), dg

    fn = sp["final_norm"].reshape(1, d)
    dh2, dh2b, loss_acc, dfn = rowwise("loss", loss_tile, [h2, tgt], [fn], [(d, F32), (d, BF16)], [(1, LANES), (1, d)], tr=tr_l)

    gbig, gs = {}, {}
    gs["final_norm"] = dfn.reshape(sp["final_norm"].shape)
    dyn = mm(dh2b, bw["Wo1"], "nt", "mm_dyn")
    gbig["Wo1"] = mm(yn, dh2b, "tn", "mm_dWo1", out_dtype=BF16)
    dxs, dbm, dcm, dcols, drws, dz1, ddvec, dgn = ssd_core_bwd(cpre1, cols, rws, z1, dvec, sp["ssd_gnorm"], ssd_st, dyn,
                                                              nb, hpg, "ssd_core_bwd")
    gs["ssd_d"] = ddvec.reshape(1, nhd, SSD_HEAD_DIM).sum(axis=2)
    gs["ssd_gnorm"] = dgn.reshape(1, inner)
    ddt, dcum = _layouts_to_pairs(dcols, drws, ng, hpg)

    def ssd_dt_bwd_tile(i, r_, ddt_, dcum_, b_, a_):
        _, vjp = jax.vjp(_ssd_dt_tile, r_, b_, a_)
        return vjp((ddt_, dcum_))

    ddtr, dbias, dalog = rowwise("ssd_dt_bwd", ssd_dt_bwd_tile, [dtr, _pad_lanes(ddt), _pad_lanes(dcum)], [dt_bias, a_log],
                                 [(LANES, BF16)], [(1, LANES), (1, LANES)], tr=CHUNK)
    gs["ssd_dt_bias"] = dbias[:, :nhd]
    gs["ssd_a_log"] = dalog[:, :nhd]
    dcpre1 = jnp.concatenate([dxs, dbm, dcm], axis=1)
    dxbc, dcw1, dcb1 = conv_bwd(dcpre1, xbc, sp["ssd_conv_w"], nb, "ssd_conv_bwd")
    gs["ssd_conv_w"] = dcw1
    gs["ssd_conv_b"] = dcb1
    dn1 = mm(dz1, bw["W1z"], "nt", "mm_dn1z")
    dn1 = mm(dxbc, bw["W1x"], "nt", "mm_dn1x", resid=dn1)
    dn1 = mm(ddtr, bw["W1dt"], "nt", "mm_dn1dt", resid=dn1)
    gbig["W1z"] = mm(n1, dz1, "tn", "mm_dW1z", out_dtype=BF16)
    gbig["W1x"] = mm(n1, dxbc, "tn", "mm_dW1x", out_dtype=BF16)
    gbig["W1dt"] = mm(n1, ddtr, "tn", "mm_dW1dt", out_dtype=BF16)
    dh1, dh1b, dg1 = norm_bwd(h1, sp["ssd_norm"], dn1, dh2, "norm1_bwd")
    gs["ssd_norm"] = dg1

    gbig["Wo0a"] = mm(ya, dh1b, "tn", "mm_dWo0a", out_dtype=BF16)
    gbig["Wo0b"] = mm(yb, dh1b, "tn", "mm_dWo0b", out_dtype=BF16)
    if late_grads is not None:
        late_grads({n: gbig[n] for n in _LATE})
    dya = mm(dh1b, bw["Wo0a"], "nt", "mm_dya")
    dyb = mm(dh1b, bw["Wo0b"], "nt", "mm_dyb")
    (dq, dk, dv, dcp_skip, dzb, dcolg, drowg, dnw, dsk) = ml_core_bwd(
        q, k, v, colg, rowg, cpre0, zb, sp["ml_norm"], sp["ml_skip"], ml_cs, ml_ns, ml_ms, dyb, nb, nh, "ml_core_bwd")
    gs["ml_norm"] = dnw.reshape(1, mlw)
    gs["ml_skip"] = dsk.reshape(1, mlw)
    dig, dbcum = _layouts_to_pairs(dcolg, drowg, nh, 1)
    dgout = _pad_lanes(jnp.concatenate([dig, dbcum], axis=1))

    def ml_gates_bwd_tile(i, g_, dgo, b_):
        _, vjp = jax.vjp(lambda a, b: _ml_gates_tile(a, b, nh), g_, b_)
        return vjp(dgo)

    dgl, dbg = rowwise("ml_gates_bwd", ml_gates_bwd_tile, [gl, dgout], [bgate], [(LANES, F32)], [(1, LANES)], tr=CHUNK)
    gs["ml_b_gate"] = dbg[:, :2 * nh]
    dcpre0, dxb_v, dwq, dwk, dwv, dgq, dgk, dgv = ml_proj_bwd(cpre0, xb, wq_e, wk_e, wv_e, gq, gk, gv, dq, dk, dv, dgl,
                                                            dcp_skip, "ml_proj_bwd")
    gs["ml_wq"], gs["ml_wk"], gs["ml_wv"] = _hw_contract(dwq), _hw_contract(dwk), _hw_contract(dwv)
    gs["ml_w_gate"] = _wg_contract([dgq, dgk, dgv], 2 * nh)
    dxb, dcw0, dcb0 = conv_bwd(dcpre0, xb, sp["ml_conv_w"], nb, "ml_conv_bwd", resid=dxb_v)
    gs["ml_conv_w"] = dcw0
    gs["ml_conv_b"] = dcb0

    def glu_bwd_tile(i, ys, tt, pab, dy_, gbias):
        _, vjp = jax.vjp(glu_tile, ys, tt, pab[:, s5w:], gbias)
        return vjp(dy_)

    dys_direct, dtglu, dza, dglub = rowwise("glu_bwd", glu_bwd_tile, [ys5, tglu, pa, dya], [sp["s5_glu_b"]],
                                            [(s5w, F32), (s5w, BF16), (s5w, BF16)], [(1, s5w)], tr=_tile(rows, 256, 16))
    gs["s5_glu_b"] = dglub
    dgb = mm(dtglu, bw["glu"], "nt", "mm_dgb")
    gbig["glu"] = mm(gb, dtglu, "tn", "mm_dglu", out_dtype=BF16)

    def gelu_bwd_tile(i, ys, dg_, direct):
        _, vjp = jax.vjp(_gelu, ys)
        return vjp(dg_)[0] + direct

    dys5 = rowwise("gelu_bwd", gelu_bwd_tile, [ys5, dgb, dys_direct], [], [(s5w, F32)], tr=_tile(rows, 256, 16))[0]
    du, dbre, dbim, dcre, dcim, dare, daie, dd5 = s5_bwd(pa, dys5, s5st, bre, bim, cre, cim, are, aie, sp["s5_d"], nb, "s5_bwd")
    gs["s5_d"] = dd5
    dbbr, dbbi, dcr, dci, dar, dai = _s5_contract(dbre, dbim, dcre, dcim, dare, daie, sg, spn, shh)
    gs["s5_c_re"], gs["s5_c_im"] = dcr, dci
    (gs["s5_lambda_re"], gs["s5_lambda_im"], dlogdt, gs["s5_b_re"], gs["s5_b_im"]) = s5_disc_vjp((dar, dai, dbbr, dbbi))
    gs["s5_log_dt"] = dlogdt.reshape(1, -1)
    dpa = jnp.concatenate([du, dza], axis=1)
    gbig["W0a"] = mm(n0, dpa, "tn", "mm_dW0a", out_dtype=BF16)
    gbig["W0xb"] = mm(n0, dxb, "tn", "mm_dW0xb", out_dtype=BF16)
    gbig["W0zb"] = mm(n0, dzb, "tn", "mm_dW0zb", out_dtype=BF16)
    if early_grads is not None:
        early_grads({n: gbig[n] for n in _EARLY})
    dn0 = mm(dpa, bw["W0a"], "nt", "mm_dn0a")
    dn0 = mm(dxb, bw["W0xb"], "nt", "mm_dn0xb", resid=dn0)
    dn0 = mm(dzb, bw["W0zb"], "nt", "mm_dn0zb", resid=dn0)
    dh0, _, dg0 = norm_bwd(h0, sp["ab_norm"], dn0, dh1, "norm0_bwd")
    gs["ab_norm"] = dg0
    dh0 = dh0.reshape(nb, tp, d)
    gs["meta_tokens"] = jnp.sum(dh0[:, :N_META], axis=0)
    return loss_acc[0, 0], dh0, gbig, gs


N_DEV = 8
N_CHIP = 4
N_PEER_CHIPS = N_CHIP - 1
MESH = pl.DeviceIdType.MESH
_HBM = pl.BlockSpec(memory_space=pltpu.HBM)


def _place():
    x, y, c = lax.axis_index("x"), lax.axis_index("y"), lax.axis_index("c")
    return x, y, c, [(1 - x, y), (x, 1 - y), (1 - x, 1 - y)]


def all_gather8(v, name):
    m_per, n = v.shape

    def body(x_ref, out_ref, send_sems, recv_sems, local_sem):
        x, y, c, chips = _place()
        me, sibling = (x, y, c), (x, y, 1 - c)

        def rows(px, py, pc):
            return out_ref.at[pl.ds((4 * px + 2 * py + pc) * m_per, m_per), :]

        def copy(kk, block, to, src=None):
            return pltpu.make_async_remote_copy(
                src_ref=rows(*block) if src is None else src, dst_ref=rows(*block), send_sem=send_sems.at[kk],
                recv_sem=recv_sems.at[kk], device_id=to, device_id_type=MESH)

        mine = pltpu.make_async_copy(x_ref, rows(*me), local_sem)
        mine.start()
        first = [copy(0, me, sibling, src=x_ref)]
        first += [copy(1 + j, me, (*chip, c), src=x_ref) for j, chip in enumerate(chips)]
        for cp in first:
            cp.start()
        passed = [copy(4 + j, (*chip, c), sibling) for j, chip in enumerate(chips)]
        for j, chip in enumerate(chips):
            copy(1 + j, (*chip, c), me).wait_recv()
            passed[j].start()
        copy(0, sibling, me).wait_recv()
        for j, chip in enumerate(chips):
            copy(4 + j, (*chip, 1 - c), me).wait_recv()
        for cp in first + passed:
            cp.wait_send()
        mine.wait()

    return pl.pallas_call(
        body, name=name, out_shape=jax.ShapeDtypeStruct((N_DEV * m_per, n), v.dtype),
        in_specs=[pl.BlockSpec(memory_space=pltpu.VMEM)], out_specs=pl.BlockSpec(memory_space=pltpu.VMEM),
        scratch_shapes=[pltpu.SemaphoreType.DMA((7,)), pltpu.SemaphoreType.DMA((7,)), pltpu.SemaphoreType.DMA],
        compiler_params=pltpu.CompilerParams(vmem_limit_bytes=VMEM_LIMIT))(v)


def gather_chips(vs, name):
    na = len(vs)

    def body(*refs):
        x_refs, out_refs = refs[:na], refs[na:2 * na]
        send_sems, recv_sems, local_sems = refs[2 * na:]
        x, y, c, chips = _place()
        k = 2 * x + y
        sibling = (x, y, 1 - c)

        def copy(i, kk, src, chip_k, half, to):
            return pltpu.make_async_remote_copy(
                src_ref=src, dst_ref=out_refs[i].at[chip_k, half], send_sem=send_sems.at[6 * i + kk],
                recv_sem=recv_sems.at[6 * i + kk], device_id=to, device_id_type=MESH)

        mine = [pltpu.make_async_copy(x_refs[i], out_refs[i].at[k], local_sems.at[i]) for i in range(na)]
        for cp in mine:
            cp.start()
        first = [copy(i, j, x_refs[i].at[c], k, c, (*chip, c)) for j, chip in enumerate(chips) for i in range(na)]
        for cp in first:
            cp.start()
        passed = []
        for j, (cx, cy) in enumerate(chips):
            kj = 2 * cx + cy
            for i in range(na):
                copy(i, j, out_refs[i].at[kj, c], kj, c, (cx, cy, c)).wait_recv()
                fwd = copy(i, 3 + j, out_refs[i].at[kj, c], kj, c, sibling)
                fwd.start()
                passed.append(fwd)
        for j, (cx, cy) in enumerate(chips):
            kj = 2 * cx + cy
            for i in range(na):
                copy(i, 3 + j, out_refs[i].at[kj, 1 - c], kj, 1 - c, sibling).wait_recv()
        for cp in first + passed:
            cp.wait_send()
        for cp in mine:
            cp.wait()

    return pl.pallas_call(
        body, name=name, out_shape=[jax.ShapeDtypeStruct((N_CHIP,) + v.shape, v.dtype) for v in vs],
        in_specs=[_HBM] * na, out_specs=[_HBM] * na,
        scratch_shapes=[pltpu.SemaphoreType.DMA((6 * na,)), pltpu.SemaphoreType.DMA((6 * na,)),
                        pltpu.SemaphoreType.DMA((na,))])(*vs)


def swap_halves(gs_, name):
    na = len(gs_)

    def body(*refs):
        g_refs, out_refs = refs[:na], refs[na:2 * na]
        send_sems, recv_sems = refs[2 * na:]
        x, y, c, _ = _place()
        cps = [pltpu.make_async_remote_copy(
            src_ref=g_refs[i].at[kk, 1 - c], dst_ref=out_refs[i].at[kk], send_sem=send_sems.at[N_CHIP * i + kk],
            recv_sem=recv_sems.at[N_CHIP * i + kk], device_id=(x, y, 1 - c), device_id_type=MESH)
            for i in range(na) for kk in range(N_CHIP)]
        for cp in cps:
            cp.start()
        for cp in cps:
            cp.wait()

    return pl.pallas_call(
        body, name=name, out_shape=[jax.ShapeDtypeStruct((N_CHIP,) + g.shape[2:], g.dtype) for g in gs_],
        in_specs=[_HBM] * na, out_specs=[_HBM] * na,
        scratch_shapes=[pltpu.SemaphoreType.DMA((N_CHIP * na,)), pltpu.SemaphoreType.DMA((N_CHIP * na,))])(*gs_)


def add_halves(g, other, core, name):
    _, _, m, n = g.shape
    tr = _tile(m, 256, 16)

    def body(core_ref, g_ref, o_ref, out_ref):
        out_ref[...] = (g_ref[...].astype(F32) + o_ref[...].astype(F32)).astype(out_ref.dtype)

    grid_spec = pltpu.PrefetchScalarGridSpec(
        num_scalar_prefetch=1, grid=(N_CHIP, m // tr),
        in_specs=[pl.BlockSpec((None, None, tr, n), lambda kk, i, core_ref: (kk, core_ref[0], i, 0)),
                  pl.BlockSpec((None, tr, n), lambda kk, i, core_ref: (kk, i, 0))],
        out_specs=pl.BlockSpec((None, tr, n), lambda kk, i, core_ref: (kk, i, 0)))
    return pl.pallas_call(body, name=name, grid_spec=grid_spec, out_shape=jax.ShapeDtypeStruct((N_CHIP, m, n), g.dtype),
                          compiler_params=_params(("arbitrary", "arbitrary")))(core.reshape(1).astype(jnp.int32), g, other)


def sequencer_exchange(srcs, scatter, collective_id, name):
    na = len(srcs)
    per = 2 * N_PEER_CHIPS + (1 if scatter else 0)
    hbm = pltpu.MemorySpace.HBM
    src_refs = [jax.new_ref(a, memory_space=hbm) for a in srcs]
    out_refs = [jax.empty_ref(jax.ShapeDtypeStruct((N_CHIP, 2) + a.shape[1:], a.dtype), memory_space=hbm) for a in srcs]

    @pl.kernel(mesh=plsc.ScalarSubcoreMesh(axis_name="seq", num_cores=1), name=name,
               scratch_types=(pltpu.SemaphoreType.DMA((per * na,)), pltpu.SemaphoreType.DMA((per * na,)),
                              pltpu.SemaphoreType.DMA((na,))),
               compiler_params=pltpu.CompilerParams(collective_id=collective_id))
    def launch(send_sems, recv_sems, local_sems):
        x, y, c, chips = _place()
        k = 2 * x + y
        sibling = (x, y, 1 - c)
        barrier = pltpu.get_barrier_semaphore()
        for cx, cy in chips:
            pl.semaphore_signal(barrier, inc=1, device_id=(cx, cy, c), device_id_type=MESH)
        pl.semaphore_signal(barrier, inc=1, device_id=sibling, device_id_type=MESH)
        pl.semaphore_wait(barrier, N_CHIP)

        def copy(i, kk, src, chip_k, half, to):
            return pltpu.make_async_remote_copy(
                src_ref=src, dst_ref=out_refs[i].at[chip_k, half], send_sem=send_sems.at[per * i + kk],
                recv_sem=recv_sems.at[per * i + kk], device_id=to, device_id_type=MESH)

        if scatter:
            mine = [pltpu.make_async_copy(src_refs[i].at[k], out_refs[i].at[k, c], local_sems.at[i]) for i in range(na)]
        else:
            mine = [pltpu.make_async_copy(src_refs[i], out_refs[i].at[k], local_sems.at[i]) for i in range(na)]
        for cp in mine:
            cp.start()
        first = []
        for j, (cx, cy) in enumerate(chips):
            for i in range(na):
                src = src_refs[i].at[2 * cx + cy] if scatter else src_refs[i].at[c]
                first.append(copy(i, j, src, k, c, (cx, cy, c)))
        if scatter:
            first += [copy(i, 2 * N_PEER_CHIPS, src_refs[i].at[k], k, c, sibling) for i in range(na)]
        for cp in first:
            cp.start()
        passed = []
        for j, (cx, cy) in enumerate(chips):
            kj = 2 * cx + cy
            for i in range(na):
                copy(i, j, out_refs[i].at[kj, c], kj, c, (cx, cy, c)).wait_recv()
                fwd = copy(i, N_PEER_CHIPS + j, out_refs[i].at[kj, c], kj, c, sibling)
                fwd.start()
                passed.append(fwd)
        if scatter:
            for i in range(na):
                copy(i, 2 * N_PEER_CHIPS, out_refs[i].at[k, 1 - c], k, 1 - c, sibling).wait_recv()
        for j, (cx, cy) in enumerate(chips):
            kj = 2 * cx + cy
            for i in range(na):
                copy(i, N_PEER_CHIPS + j, out_refs[i].at[kj, 1 - c], kj, 1 - c, sibling).wait_recv()
        for cp in first + passed:
            cp.wait_send()
        for cp in mine:
            cp.wait()

    launch()
    return [r[...] for r in out_refs]


PACK_LANES = 512


def _pack(arrs, dtype, lanes, row_align):
    flat = jnp.concatenate([a.reshape(-1).astype(dtype) for a in arrs])
    unit = lanes * row_align
    total = -(-flat.shape[0] // unit) * unit
    return jnp.pad(flat, (0, total - flat.shape[0])).reshape(total // lanes, lanes)


def _unpack(flat, shapes):
    flat = flat.reshape(-1)
    out, off = [], 0
    for s in shapes:
        n = math.prod(s)
        out.append(flat[off:off + n].reshape(s))
        off += n
    return out


def _adam_tile(w, m, v, g):
    m2 = ADAM_B1 * m + (1.0 - ADAM_B1) * g
    v2 = ADAM_B2 * v + (1.0 - ADAM_B2) * (g * g)
    m_hat = m2 / (1.0 - ADAM_B1 ** ADAM_STEP)
    v_hat = v2 / (1.0 - ADAM_B2 ** ADAM_STEP)
    delta = -ADAM_LR * (m_hat / (jnp.sqrt(v_hat) + ADAM_EPS) + ADAM_WD * w)
    return delta, m2, v2


def adam_big(w, m, v, pieces, name):
    _, r, c = w.shape
    tr = _tile(r, 128, 16)

    def body(w_ref, m_ref, v_ref, p0, p1, p2, p3, g_ref, d_ref, mo_ref, vo_ref):
        g = ((p0[...].astype(F32) + p1[...].astype(F32)) + p2[...].astype(F32)) + p3[...].astype(F32)
        delta, m2, v2 = _adam_tile(w_ref[...], m_ref[...], v_ref[...], g)
        g_ref[...] = g
        d_ref[...] = delta
        mo_ref[...] = m2
        vo_ref[...] = v2

    wspec = pl.BlockSpec((None, tr, c), lambda i: (0, i, 0))
    pspecs = [pl.BlockSpec((None, tr, c), functools.partial(lambda i, kk: (kk, i, 0), kk=kk)) for kk in range(N_CHIP)]
    return pl.pallas_call(
        body, name=name, grid=(r // tr,), in_specs=[wspec] * 3 + pspecs, out_specs=[wspec] * 4,
        out_shape=[jax.ShapeDtypeStruct(w.shape, F32)] * 4, compiler_params=_params(("parallel",)))(
            w, m, v, pieces, pieces, pieces, pieces)


_WEIGHTS = (
    ("meta_tokens", "small", 1), ("ab_norm", "small", None), ("ab_w_in", "big", 2), ("s5_lambda_re", "small", None),
    ("s5_lambda_im", "small", None), ("s5_log_dt", "small", None), ("s5_b_re", "small", None), ("s5_b_im", "small", None),
    ("s5_c_re", "small", None), ("s5_c_im", "small", None), ("s5_d", "small", None), ("s5_glu_w", "big", 1),
    ("s5_glu_b", "small", None), ("ml_conv_w", "small", 2), ("ml_conv_b", "small", None), ("ml_wq", "small", 1),
    ("ml_wk", "small", 1), ("ml_wv", "small", 1), ("ml_w_gate", "small", 1), ("ml_b_gate", "small", None),
    ("ml_norm", "small", None), ("ml_skip", "small", None), ("ab_w_out", "big", 1), ("ssd_norm", "small", 1),
    ("ssd_w_in", "big", 2), ("ssd_conv_w", "small", 2), ("ssd_conv_b", "small", 1), ("ssd_dt_bias", "small", None),
    ("ssd_a_log", "small", None), ("ssd_d", "small", None), ("ssd_gnorm", "small", 1), ("ssd_w_out", "big", 1),
    ("final_norm", "small", None),
)


def _squeeze(a):
    return a[0] if a.ndim >= 3 else a


def kernel(x, meta_tokens, ab_norm, ab_w_in, s5_lambda_re, s5_lambda_im, s5_log_dt, s5_b_re, s5_b_im, s5_c_re, s5_c_im, s5_d, s5_glu_w, s5_glu_b, ml_conv_w, ml_conv_b, ml_wq, ml_wk, ml_wv, ml_w_gate, ml_b_gate, ml_norm, ml_skip, ab_w_out, ssd_norm, ssd_w_in, ssd_conv_w, ssd_conv_b, ssd_dt_bias, ssd_a_log, ssd_d, ssd_gnorm, ssd_w_out, final_norm, loss_target, m_meta_tokens, m_ab_norm, m_ab_w_in, m_s5_lambda_re, m_s5_lambda_im, m_s5_log_dt, m_s5_b_re, m_s5_b_im, m_s5_c_re, m_s5_c_im, m_s5_d, m_s5_glu_w, m_s5_glu_b, m_ml_conv_w, m_ml_conv_b, m_ml_wq, m_ml_wk, m_ml_wv, m_ml_w_gate, m_ml_b_gate, m_ml_norm, m_ml_skip, m_ab_w_out, m_ssd_norm, m_ssd_w_in, m_ssd_conv_w, m_ssd_conv_b, m_ssd_dt_bias, m_ssd_a_log, m_ssd_d, m_ssd_gnorm, m_ssd_w_out, m_final_norm, v_meta_tokens, v_ab_norm, v_ab_w_in, v_s5_lambda_re, v_s5_lambda_im, v_s5_log_dt, v_s5_b_re, v_s5_b_im, v_s5_c_re, v_s5_c_im, v_s5_d, v_s5_glu_w, v_s5_glu_b, v_ml_conv_w, v_ml_conv_b, v_ml_wq, v_ml_wk, v_ml_wv, v_ml_w_gate, v_ml_b_gate, v_ml_norm, v_ml_skip, v_ab_w_out, v_ssd_norm, v_ssd_w_in, v_ssd_conv_w, v_ssd_conv_b, v_ssd_dt_bias, v_ssd_a_log, v_ssd_d, v_ssd_gnorm, v_ssd_w_out, v_final_norm):
    args = (meta_tokens, ab_norm, ab_w_in, s5_lambda_re, s5_lambda_im, s5_log_dt, s5_b_re, s5_b_im, s5_c_re, s5_c_im, s5_d, s5_glu_w, s5_glu_b, ml_conv_w, ml_conv_b, ml_wq, ml_wk, ml_wv, ml_w_gate, ml_b_gate, ml_norm, ml_skip, ab_w_out, ssd_norm, ssd_w_in, ssd_conv_w, ssd_conv_b, ssd_dt_bias, ssd_a_log, ssd_d, ssd_gnorm, ssd_w_out, final_norm)
    m_args = (m_meta_tokens, m_ab_norm, m_ab_w_in, m_s5_lambda_re, m_s5_lambda_im, m_s5_log_dt, m_s5_b_re, m_s5_b_im, m_s5_c_re, m_s5_c_im, m_s5_d, m_s5_glu_w, m_s5_glu_b, m_ml_conv_w, m_ml_conv_b, m_ml_wq, m_ml_wk, m_ml_wv, m_ml_w_gate, m_ml_b_gate, m_ml_norm, m_ml_skip, m_ab_w_out, m_ssd_norm, m_ssd_w_in, m_ssd_conv_w, m_ssd_conv_b, m_ssd_dt_bias, m_ssd_a_log, m_ssd_d, m_ssd_gnorm, m_ssd_w_out, m_final_norm)
    v_args = (v_meta_tokens, v_ab_norm, v_ab_w_in, v_s5_lambda_re, v_s5_lambda_im, v_s5_log_dt, v_s5_b_re, v_s5_b_im, v_s5_c_re, v_s5_c_im, v_s5_d, v_s5_glu_w, v_s5_glu_b, v_ml_conv_w, v_ml_conv_b, v_ml_wq, v_ml_wk, v_ml_wv, v_ml_w_gate, v_ml_b_gate, v_ml_norm, v_ml_skip, v_ab_w_out, v_ssd_norm, v_ssd_w_in, v_ssd_conv_w, v_ssd_conv_b, v_ssd_dt_bias, v_ssd_a_log, v_ssd_d, v_ssd_gnorm, v_ssd_w_out, v_final_norm)
    names = [w[0] for w in _WEIGHTS]
    kind = {w[0]: w[1] for w in _WEIGHTS}
    axis = {w[0]: w[2] for w in _WEIGHTS}
    w_loc = dict(zip(names, args))
    m_loc = dict(zip(names, m_args))
    v_loc = dict(zip(names, v_args))
    chip = 2 * lax.axis_index("x") + lax.axis_index("y")
    core = lax.axis_index("c")
    big = [n for n in names if kind[n] == "big"]
    small = [n for n in names if kind[n] == "small"]
    small_sh = [n for n in small if axis[n] is not None]

    def halves(a):
        return a.astype(BF16).reshape(2, a.shape[1] // 2, a.shape[2])

    def assemble(n, gth):
        shard = gth.reshape((N_CHIP,) + w_loc[n].shape[1:])
        if axis[n] == 1:
            return shard.reshape(-1, shard.shape[2])
        return jnp.concatenate([shard[kk] for kk in range(N_CHIP)], axis=1)

    early = ["ab_w_in", "s5_glu_w"]
    late = ["ab_w_out", "ssd_w_in", "ssd_w_out"]
    gathered = gather_chips([halves(w_loc[n]) for n in early], "gather_early_w")
    after_early = (gathered[0][0, 0, 0, 0] * 0).astype(BF16)
    late_gathered = sequencer_exchange([halves(w_loc[n]) + after_early for n in late], False, 1, "gather_late_w")
    w_in0_shards = gathered[0].reshape((N_CHIP,) + w_loc["ab_w_in"].shape[1:])
    glu_full = assemble("s5_glu_w", gathered[1])

    def columns(shards, lo, hi):
        cw = shards.shape[2]
        parts = [shards[kk][:, max(lo - kk * cw, 0):min(hi - kk * cw, cw)]
                 for kk in range(N_CHIP) if lo < (kk + 1) * cw and hi > kk * cw]
        return parts[0] if len(parts) == 1 else jnp.concatenate(parts, axis=1)

    small_sh_shapes = [w_loc[n].shape for n in small_sh]
    packed_s = _pack([w_loc[n] for n in small_sh], F32, LANES, SUBLANES)
    g8 = all_gather8(packed_s, "gather_small_w").reshape(N_CHIP, 2, -1)
    sp = {}
    for n in small:
        if axis[n] is None:
            sp[n] = _squeeze(w_loc[n])
    per_chip = [_unpack(g8[kk, 0], small_sh_shapes) for kk in range(N_CHIP)]
    for i, n in enumerate(small_sh):
        sp[n] = _squeeze(jnp.concatenate([per_chip[kk][i] for kk in range(N_CHIP)], axis=axis[n]))

    s5w = glu_full.shape[0]
    mlw = w_loc["ab_w_out"].shape[1] * N_CHIP - s5w
    inner = w_loc["ssd_w_out"].shape[1] * N_CHIP
    n_heads1 = sp["ssd_d"].shape[1]
    cdim = w_loc["ssd_w_in"].shape[2] * N_CHIP - inner - n_heads1
    bw = dict(W0a=columns(w_in0_shards, 0, 2 * s5w), W0xb=columns(w_in0_shards, 2 * s5w, 2 * s5w + mlw),
              W0zb=columns(w_in0_shards, 2 * s5w + mlw, 2 * (s5w + mlw)), glu=glu_full)

    def late_weights():
        fb = dict(zip(late, late_gathered))
        w_out0 = assemble("ab_w_out", fb["ab_w_out"])
        w1 = fb["ssd_w_in"].reshape((N_CHIP,) + w_loc["ssd_w_in"].shape[1:])
        return dict(Wo0a=w_out0[:s5w], Wo0b=w_out0[s5w:], W1z=columns(w1, 0, inner),
                    W1x=columns(w1, inner, inner + cdim), W1dt=_pad_lanes(columns(w1, inner + cdim, inner + cdim + n_heads1)),
                    Wo1=assemble("ssd_w_out", fb["ssd_w_out"]))

    def piece_columns(parts, lo, hi):
        out, off = [], 0
        for p in parts:
            a, b = max(lo - off, 0), min(hi - off, p.shape[1])
            if a < b:
                out.append(p[:, a:b])
            off += p.shape[1]
        return out[0] if len(out) == 1 else jnp.concatenate(out, axis=1)

    def chip_halves(n, parts):
        _, r, c_ = w_loc[n].shape
        if axis[n] == 1:
            whole = parts[0] if len(parts) == 1 else jnp.concatenate(parts, axis=0)
            return whole.reshape(N_CHIP, 2, r // 2, c_)
        shards = [piece_columns(parts, kk * c_, (kk + 1) * c_) for kk in range(N_CHIP)]
        return jnp.stack(shards).reshape(N_CHIP, 2, r // 2, c_)

    pieces = {}

    def reduce_group(ns, gfull, tag, collective_id):
        gps = [chip_halves(n, gfull[n]) for n in ns]
        from_sibling = swap_halves(gps, "swap_" + tag)
        partials = [add_halves(gp, oth, core, "add_" + n) for n, gp, oth in zip(ns, gps, from_sibling)]
        pieces.update(zip(ns, sequencer_exchange(partials, True, collective_id, "scatter_" + tag)))

    def late_grads(g):
        gfull = {"ab_w_out": [g["Wo0a"], g["Wo0b"]], "ssd_w_in": [g["W1z"], g["W1x"], g["W1dt"][:, :n_heads1]],
                 "ssd_w_out": [g["Wo1"]]}
        reduce_group(late, gfull, "late_g", 2)

    def early_grads(g):
        gfull = {"ab_w_in": [g["W0a"], g["W0xb"], g["W0zb"]], "s5_glu_w": [g["glu"]]}
        reduce_group(early, gfull, "early_g", 3)

    loss_local, dh0, gbig, gs = _local_step(x, loss_target, bw, sp, late_weights, late_grads, early_grads)
    grad_x = dh0[:, N_META:N_META + x.shape[1]]

    out_g, out_d, out_m, out_v = {}, {}, {}, {}
    small_full_shapes = [sp[n].shape for n in small] + [(1, 1)]
    packed_gs = _pack([gs[n] for n in small] + [loss_local.reshape(1, 1)], F32, LANES, SUBLANES)
    rows_s = packed_gs.shape[0]
    all_gs = sequencer_exchange([jnp.broadcast_to(packed_gs[None], (N_CHIP,) + packed_gs.shape)], True, 4,
                                "gather_small_g")[0].reshape(N_DEV, rows_s, LANES)
    blocks = [all_gs[i] for i in range(N_DEV)]

    for n in late + early:
        w, m, v = w_loc[n], m_loc[n], v_loc[n]
        pcs = pieces[n].reshape((N_CHIP,) + w.shape[1:])
        if w.shape[2] % LANES and w.shape[1] % LANES == 0:
            outs = adam_big(*(jnp.swapaxes(a, 1, 2) for a in (w, m, v, pcs)), "adam_" + n)
            outs = [jnp.swapaxes(o, 1, 2) for o in outs]
        else:
            outs = adam_big(w, m, v, pcs, "adam_" + n)
        out_g[n], out_d[n], out_m[n], out_v[n] = outs

    def sum8(i, *b):
        acc = b[0]
        for t in b[1:]:
            acc = acc + t
        return acc

    gsum = rowwise("sum_small_g", sum8, blocks, [], [(LANES, F32)], tr=_tile(rows_s, 512, 8))[0]
    summed = _unpack(gsum, small_full_shapes)
    loss = summed[-1].reshape(())
    g_small = dict(zip(small, summed[:-1]))
    g_loc = {}
    for n in small:
        g = g_small[n].reshape((1,) + g_small[n].shape) if w_loc[n].ndim >= 3 else g_small[n]
        if axis[n] is not None:
            size = w_loc[n].shape[axis[n]]
            g = lax.dynamic_slice_in_dim(g, chip * size, size, axis=axis[n])
        g_loc[n] = g.reshape(w_loc[n].shape)
    loc_shapes = [w_loc[n].shape for n in small]
    pw, pm, pv, pg = (_pack([d[n] for n in small], F32, LANES, SUBLANES) for d in (w_loc, m_loc, v_loc, g_loc))
    dl, mn, vn = rowwise("adam_small", lambda i, a, b, c_, d_: _adam_tile(a, b, c_, d_), [pw, pm, pv, pg], [],
                         [(LANES, F32)] * 3, tr=_tile(pw.shape[0], 512, 8))
    for d_out, flat in ((out_d, dl), (out_m, mn), (out_v, vn)):
        for n, a in zip(small, _unpack(flat, loc_shapes)):
            d_out[n] = a
    for n in small:
        out_g[n] = g_loc[n]

    return (loss, grad_x, *[out_g[n] for n in names], *[out_d[n] for n in names], *[out_m[n] for n in names],
            *[out_v[n] for n in names])
```

```python
import functools
import math

import jax
import jax.numpy as jnp
from jax import lax
from jax.experimental import pallas as pl
from jax.experimental.pallas import tpu as pltpu
from jax.experimental.pallas import tpu_sc as plsc

F32 = jnp.float32
BF16 = jnp.bfloat16
HI = lax.Precision.HIGHEST

D_MODEL = 2048
SEQ = 2048
N_META = 16
CHUNK = 128
NORM_EPS = 1e-6
HEAD_NORM_EPS = 1e-5
S5_GROUP_SIZE = 16
S5_STATE = 64
MLSTM_HEADS = 8
QKV_BLOCK = 4
SSD_HEAD_DIM = 64
SSD_STATE = 128
SSD_HPG = 8
ADAM_LR = 0.001
ADAM_B1 = 0.9
ADAM_B2 = 0.999
ADAM_EPS = 1e-08
ADAM_WD = 0.01
ADAM_STEP = 10

LANES = 128
SUBLANES = 8
VMEM_LIMIT = 56 * 1024 * 1024
MM_OPERAND_VMEM = 34 * 1024 * 1024


def _sigmoid(x):
    return 0.5 * jnp.tanh(0.5 * x) + 0.5


@jax.custom_vjp
def _silu(x):
    return x * _sigmoid(x)


def _silu_fwd(x):
    return x * _sigmoid(x), x


def _silu_bwd(x, ct):
    s = _sigmoid(x)
    return (ct * (s * (1.0 + x * (1.0 - s))),)


_silu.defvjp(_silu_fwd, _silu_bwd)


def _softplus(x):
    return jnp.maximum(x, 0.0) + jnp.log(1.0 + jnp.exp(-jnp.abs(x)))


def _log_sigmoid(x):
    return jnp.minimum(x, 0.0) - jnp.log(1.0 + jnp.exp(-jnp.abs(x)))


def _gelu(x):
    return 0.5 * x * (1.0 + jnp.tanh(math.sqrt(2.0 / math.pi) * (x + 0.044715 * (x * x * x))))


def _dot(a, b, dims, precision=None):
    return lax.dot_general(a, b, (dims, ((), ())), preferred_element_type=F32, precision=precision)


_NN, _NT, _TN = ((1,), (0,)), ((1,), (1,)), ((0,), (0,))


def _bf16_dot(dims, da_rule, db_rule):
    @jax.custom_vjp
    def f(a, b):
        return _dot(a.astype(BF16), b.astype(BF16), dims)

    def fwd(a, b):
        ab, bb = a.astype(BF16), b.astype(BF16)
        return _dot(ab, bb, dims), (ab, bb, jnp.zeros((), a.dtype), jnp.zeros((), b.dtype))

    def bwd(res, ct):
        ab, bb, a_like, b_like = res
        cb = ct.astype(BF16)
        return da_rule(ab, bb, cb).astype(a_like.dtype), db_rule(ab, bb, cb).astype(b_like.dtype)

    f.defvjp(fwd, bwd)
    return f


_dot_nn = _bf16_dot(_NN, lambda a, b, c: _dot(c, b, _NT), lambda a, b, c: _dot(a, c, _TN))
_dot_nt = _bf16_dot(_NT, lambda a, b, c: _dot(c, b, _NN), lambda a, b, c: _dot(c, a, _TN))
_dot_tn = _bf16_dot(_TN, lambda a, b, c: _dot(b, c, _NT), lambda a, b, c: _dot(a, c, _NN))


def _lane_pick(a, idx):
    sel = (lax.broadcasted_iota(jnp.int32, (1, a.shape[1]), 1) == idx).astype(a.dtype)
    return jnp.sum(a * sel, axis=1, keepdims=True)


def _row_pick(a, idx):
    sel = (lax.broadcasted_iota(jnp.int32, (a.shape[0], 1), 0) == idx).astype(a.dtype)
    return jnp.sum(a * sel, axis=0, keepdims=True)


def _tri(n, upper=False):
    r = lax.broadcasted_iota(jnp.int32, (n, n), 0)
    c = lax.broadcasted_iota(jnp.int32, (n, n), 1)
    return ((r <= c) if upper else (r >= c)).astype(F32)


def _tile(n, target, align):
    if n <= target:
        return n
    t = (target // align) * align
    while t >= align:
        if n % t == 0:
            return t
        t -= align
    return n


def _params(sem=None):
    return pltpu.CompilerParams(dimension_semantics=sem, vmem_limit_bytes=VMEM_LIMIT)


def mm(a, b, mode, name, resid=None, out_dtype=F32):
    if mode == "nn":
        (m, k), (k2, n) = a.shape, b.shape
    elif mode == "nt":
        (m, k), (n, k2) = a.shape, b.shape
    else:
        (k, m), (k2, n) = a.shape, b.shape
    assert k == k2, (a.shape, b.shape, mode)
    a_sz, b_sz = a.dtype.itemsize, b.dtype.itemsize
    if mode == "tn":
        tm, tn = _tile(m, 1024, LANES), _tile(n, 1024, LANES)
        tk = _tile(k, MM_OPERAND_VMEM // (2 * (tm * a_sz + tn * b_sz)), 16)
    else:
        tm, tn = _tile(m, 1088, 16), _tile(n, 512, LANES)
        tk = _tile(k, MM_OPERAND_VMEM // (2 * (tm * a_sz + tn * b_sz)), LANES)
    nk = k // tk
    dims = {"nn": ((1,), (0,)), "nt": ((1,), (1,)), "tn": ((0,), (0,))}[mode]
    has_resid = resid is not None

    def body(*refs):
        if has_resid:
            a_ref, b_ref, r_ref, o_ref = refs[:4]
        else:
            a_ref, b_ref, o_ref = refs[:3]
        part = _dot(a_ref[...].astype(BF16), b_ref[...].astype(BF16), dims)

        def finish(res):
            if has_resid:
                res = res + r_ref[...].astype(F32)
            o_ref[...] = res.astype(o_ref.dtype)

        if nk == 1:
            finish(part)
            return
        acc_ref = refs[-1]
        kk = pl.program_id(2)

        @pl.when(kk == 0)
        def _():
            acc_ref[...] = part

        @pl.when(jnp.logical_and(kk > 0, kk < nk - 1))
        def _():
            acc_ref[...] += part

        @pl.when(kk == nk - 1)
        def _():
            finish(acc_ref[...] + part)

    if mode == "tn":
        a_spec = pl.BlockSpec((tk, tm), lambda i, j, kk: (kk, i))
    else:
        a_spec = pl.BlockSpec((tm, tk), lambda i, j, kk: (i, kk))
    if mode == "nt":
        b_spec = pl.BlockSpec((tn, tk), lambda i, j, kk: (j, kk))
    else:
        b_spec = pl.BlockSpec((tk, tn), lambda i, j, kk: (kk, j))
    o_spec = pl.BlockSpec((tm, tn), lambda i, j, kk: (i, j))
    in_specs = [a_spec, b_spec] + ([o_spec] if has_resid else [])
    args = (a, b) + ((resid,) if has_resid else ())
    return pl.pallas_call(
        body, name=name, grid=(m // tm, n // tn, nk), in_specs=in_specs, out_specs=o_spec,
        out_shape=jax.ShapeDtypeStruct((m, n), out_dtype), scratch_shapes=[pltpu.VMEM((tm, tn), F32)] if nk > 1 else [],
        compiler_params=_params(("parallel", "parallel", "arbitrary")))(*args)


def rowwise(name, f, rows, params, outs, accs=(), tr=128):
    n_rows = rows[0].shape[0]
    assert n_rows % tr == 0
    n_r, n_p, n_o, n_a = len(rows), len(params), len(outs), len(accs)

    def body(*refs):
        i = pl.program_id(0)
        r_vals = [r[...] for r in refs[:n_r]]
        p_vals = [r[...] for r in refs[n_r:n_r + n_p]]
        o_refs = refs[n_r + n_p:n_r + n_p + n_o]
        a_refs = refs[n_r + n_p + n_o:]
        res = f(i, *r_vals, *p_vals)
        if not isinstance(res, (tuple, list)):
            res = (res,)
        assert len(res) == n_o + n_a, (name, len(res))
        for o_ref, val in zip(o_refs, res[:n_o]):
            o_ref[...] = val.astype(o_ref.dtype)
        if n_a:
            @pl.when(i == 0)
            def _():
                for a_ref in a_refs:
                    a_ref[...] = jnp.zeros_like(a_ref)

            for a_ref, val in zip(a_refs, res[n_o:]):
                a_ref[...] += val.astype(F32)

    in_specs = [pl.BlockSpec((tr, r.shape[1]), lambda i: (i, 0)) for r in rows]
    in_specs += [pl.BlockSpec(p.shape, lambda i: (0, 0)) for p in params]
    out_specs = [pl.BlockSpec((tr, w), lambda i: (i, 0)) for w, _ in outs]
    out_specs += [pl.BlockSpec(s, lambda i: (0, 0)) for s in accs]
    out_shape = [jax.ShapeDtypeStruct((n_rows, w), dt) for w, dt in outs]
    out_shape += [jax.ShapeDtypeStruct(s, F32) for s in accs]
    res = pl.pallas_call(
        body, name=name, grid=(n_rows // tr,), in_specs=in_specs, out_specs=out_specs, out_shape=out_shape,
        compiler_params=_params(("arbitrary",)))(*rows, *params)
    return res


def _rms(x, g, eps=NORM_EPS):
    return x * lax.rsqrt(jnp.mean(x * x, axis=-1, keepdims=True) + eps) * g


def norm_fwd(x, g, name):
    return rowwise(name, lambda i, xb, gb: _rms(xb, gb), [x], [g], [(x.shape[1], BF16)], tr=_tile(x.shape[0], 256, 16))[0]


def norm_bwd(x, g, dn, resid, name):
    def f(i, xb, dnb, rb, gb):
        _, vjp = jax.vjp(_rms, xb, gb)
        dx, dg = vjp(dnb)
        return dx + rb, dx + rb, dg

    return rowwise(name, f, [x, dn, resid], [g], [(x.shape[1], F32), (x.shape[1], BF16)], [g.shape],
                   tr=_tile(x.shape[0], 256, 16))


def conv_fwd(x, w, b, nb, name):
    rows, width = x.shape
    nc = rows // nb // CHUNK
    tw = _tile(width, 1024, LANES)
    ksz = w.shape[0]

    def body(x_ref, w_ref, b_ref, o_ref, ext_ref):
        c = pl.program_id(2)

        @pl.when(c == 0)
        def _():
            ext_ref[0:SUBLANES, :] = jnp.zeros((SUBLANES, tw), F32)

        taps = [w_ref[j:j + 1, :] for j in range(ksz)]
        bias = b_ref[...]
        row = lax.broadcasted_iota(jnp.int32, (SUBLANES, tw), 0)
        prev_rot = [pltpu.roll(ext_ref[0:SUBLANES, :], k, 0) for k in range(1, ksz)]
        for s in range(CHUNK // SUBLANES):
            r0 = s * SUBLANES
            cur = x_ref[r0:r0 + SUBLANES, :]
            cur_rot = [pltpu.roll(cur, k, 0) for k in range(1, ksz)]
            acc = bias + taps[ksz - 1] * cur
            for k in range(1, ksz):
                acc = acc + taps[ksz - 1 - k] * jnp.where(row >= k, cur_rot[k - 1], prev_rot[k - 1])
            o_ref[r0:r0 + SUBLANES, :] = acc
            prev_rot = cur_rot
        ext_ref[0:SUBLANES, :] = x_ref[CHUNK - SUBLANES:CHUNK, :]

    return pl.pallas_call(
        body, name=name, grid=(width // tw, nb, nc),
        in_specs=[pl.BlockSpec((CHUNK, tw), lambda j, bb, c: (bb * nc + c, j)),
                  pl.BlockSpec((ksz, tw), lambda j, bb, c: (0, j)),
                  pl.BlockSpec((1, tw), lambda j, bb, c: (0, j))],
        out_specs=pl.BlockSpec((CHUNK, tw), lambda j, bb, c: (bb * nc + c, j)),
        out_shape=jax.ShapeDtypeStruct((rows, width), F32),
        scratch_shapes=[pltpu.VMEM((2 * SUBLANES, tw), F32)],
        compiler_params=_params(("arbitrary", "arbitrary", "arbitrary")))(x, w, b)


def conv_bwd(dc, x, w, nb, name, resid=None, dx_dtype=BF16):
    rows, width = x.shape
    nc = rows // nb // CHUNK
    tw = _tile(width, 1024, LANES)
    ksz = w.shape[0]
    per = CHUNK // SUBLANES
    has_resid = resid is not None

    def body(*refs):
        if has_resid:
            dc_ref, x_ref, halo_ref, w_ref, r_ref, dx_ref, dw_ref, db_ref, extd_ref = refs
        else:
            dc_ref, x_ref, halo_ref, w_ref, dx_ref, dw_ref, db_ref, extd_ref = refs
        bb = pl.program_id(1)
        step = pl.program_id(2)
        c = nc - 1 - step

        @pl.when(jnp.logical_and(bb == 0, step == 0))
        def _():
            dw_ref[...] = jnp.zeros_like(dw_ref)
            db_ref[...] = jnp.zeros_like(db_ref)

        @pl.when(step == 0)
        def _():
            extd_ref[SUBLANES:2 * SUBLANES, :] = jnp.zeros((SUBLANES, tw), F32)

        nstrip = CHUNK // SUBLANES
        taps = [w_ref[j:j + 1, :] for j in range(ksz)]
        row = lax.broadcasted_iota(jnp.int32, (SUBLANES, tw), 0)
        x_prev_rot = [pltpu.roll(jnp.where(c == 0, 0.0, halo_ref[...]), k, 0) for k in range(1, ksz)]
        dcs = dc_ref[0:SUBLANES, :]
        dc_rot = [pltpu.roll(dcs, SUBLANES - k, 0) for k in range(1, ksz)]
        for s in range(nstrip):
            r0 = s * SUBLANES
            nxt = extd_ref[SUBLANES:2 * SUBLANES, :] if s == nstrip - 1 else dc_ref[r0 + SUBLANES:r0 + 2 * SUBLANES, :]
            nxt_rot = [pltpu.roll(nxt, SUBLANES - k, 0) for k in range(1, ksz)]
            xc = x_ref[r0:r0 + SUBLANES, :]
            x_rot = [pltpu.roll(xc, k, 0) for k in range(1, ksz)]
            dx = r_ref[r0:r0 + SUBLANES, :].astype(F32) if has_resid else jnp.zeros((SUBLANES, tw), F32)
            dx = dx + taps[ksz - 1] * dcs
            dw_ref[(ksz - 1) * SUBLANES:ksz * SUBLANES, :] += dcs * xc
            for k in range(1, ksz):
                j = ksz - 1 - k
                dx = dx + taps[j] * jnp.where(row < SUBLANES - k, dc_rot[k - 1], nxt_rot[k - 1])
                dw_ref[j * SUBLANES:(j + 1) * SUBLANES, :] += dcs * jnp.where(row >= k, x_rot[k - 1], x_prev_rot[k - 1])
            if s % 2 == 0:
                held = dx
            else:
                dx_ref[r0 - SUBLANES:r0 + SUBLANES, :] = jnp.concatenate([held, dx], axis=0).astype(dx_ref.dtype)
            db_ref[...] += dcs
            dcs, dc_rot, x_prev_rot = nxt, nxt_rot, x_rot
        extd_ref[SUBLANES:2 * SUBLANES, :] = dc_ref[0:SUBLANES, :]

    def blk(j, bb, step):
        return (bb * nc + nc - 1 - step, j)

    def halo(j, bb, step):
        return (jnp.maximum((bb * nc + nc - 1 - step) * per - 1, 0), j)

    in_specs = [pl.BlockSpec((CHUNK, tw), blk), pl.BlockSpec((CHUNK, tw), blk), pl.BlockSpec((SUBLANES, tw), halo),
                pl.BlockSpec((ksz, tw), lambda j, bb, step: (0, j))]
    args = [dc, x, x, w]
    if has_resid:
        in_specs.append(pl.BlockSpec((CHUNK, tw), blk))
        args.append(resid)
    dx, dw_raw, db_raw = pl.pallas_call(
        body, name=name, grid=(width // tw, nb, nc), in_specs=in_specs,
        out_specs=[pl.BlockSpec((CHUNK, tw), blk), pl.BlockSpec((ksz * SUBLANES, tw), lambda j, bb, step: (0, j)),
                   pl.BlockSpec((SUBLANES, tw), lambda j, bb, step: (0, j))],
        out_shape=[jax.ShapeDtypeStruct((rows, width), dx_dtype), jax.ShapeDtypeStruct((ksz * SUBLANES, width), F32),
                   jax.ShapeDtypeStruct((SUBLANES, width), F32)],
        scratch_shapes=[pltpu.VMEM((2 * SUBLANES, tw), F32)],
        compiler_params=_params(("arbitrary", "arbitrary", "arbitrary")))(*args)
    return dx, dw_raw.reshape(ksz, SUBLANES, width).sum(axis=1), db_raw.sum(axis=0, keepdims=True)


S5_Q = 4


def _s5_fill_bu(u, bre_ref, bim_ref, xr_ref, xi_ref, ns):
    for s in range(ns):
        ub = u[:, s * LANES:(s + 1) * LANES].astype(BF16)
        bur = _dot(ub, bre_ref[s], ((1,), (0,)))
        bui = _dot(ub, bim_ref[s], ((1,), (0,)))
        for q in range(S5_Q):
            xr_ref[q, pl.ds(s, CHUNK, stride=ns), :] = bur[:, q * LANES:(q + 1) * LANES]
            xi_ref[q, pl.ds(s, CHUNK, stride=ns), :] = bui[:, q * LANES:(q + 1) * LANES]


def _s5_scan(xr_ref, xi_ref, ar_ref, ai_ref, st_ref, ns):
    ar = [ar_ref[q] for q in range(S5_Q)]
    ai = [ai_ref[q] for q in range(S5_Q)]

    def step(t, carry):
        rows = pl.ds(pl.multiple_of(t * ns, ns), ns)
        out = []
        for q in range(S5_Q):
            pr, pi_ = carry[2 * q], carry[2 * q + 1]
            nr = ar[q] * pr - ai[q] * pi_ + xr_ref[q, rows, :]
            ni = ar[q] * pi_ + ai[q] * pr + xi_ref[q, rows, :]
            xr_ref[q, rows, :] = nr
            xi_ref[q, rows, :] = ni
            out += [nr, ni]
        return tuple(out)

    init = []
    for q in range(S5_Q):
        init += [st_ref[0, q], st_ref[1, q]]
    fin = lax.fori_loop(0, CHUNK, step, tuple(init), unroll=2)
    for q in range(S5_Q):
        st_ref[0, q] = fin[2 * q]
        st_ref[1, q] = fin[2 * q + 1]


def s5_fwd(pa, bre, bim, cre, cim, ar, ai, dvec, nb, name):
    rows = pa.shape[0]
    width = pa.shape[1] // 2
    ns = width // LANES
    nc = rows // nb // CHUNK

    def body(u_ref, bre_ref, bim_ref, cre_ref, cim_ref, ar_ref, ai_ref, d_ref, y_ref, g_ref, so_ref, xr_ref, xi_ref, st_ref):
        c = pl.program_id(1)

        @pl.when(c == 0)
        def _():
            st_ref[...] = jnp.zeros_like(st_ref)

        so_ref[...] = st_ref[...]
        u = u_ref[...]
        _s5_fill_bu(u, bre_ref, bim_ref, xr_ref, xi_ref, ns)
        _s5_scan(xr_ref, xi_ref, ar_ref, ai_ref, st_ref, ns)
        for s in range(ns):
            acc = jnp.zeros((CHUNK, LANES), F32)
            for q in range(S5_Q):
                xr = xr_ref[q, pl.ds(s, CHUNK, stride=ns), :].astype(BF16)
                xi = xi_ref[q, pl.ds(s, CHUNK, stride=ns), :].astype(BF16)
                acc = acc + _dot(xr, cre_ref[s, q * LANES:(q + 1) * LANES, :], ((1,), (0,)))
                acc = acc - _dot(xi, cim_ref[s, q * LANES:(q + 1) * LANES, :], ((1,), (0,)))
            cols = slice(s * LANES, (s + 1) * LANES)
            y = acc + d_ref[:, cols] * u[:, cols]
            y_ref[:, cols] = y
            g_ref[:, cols] = _gelu(y).astype(BF16)

    whole3 = lambda a: pl.BlockSpec(a.shape, lambda b_, c: (0, 0, 0))
    return pl.pallas_call(
        body, name=name, grid=(nb, nc),
        in_specs=[pl.BlockSpec((CHUNK, width), lambda b_, c: (b_ * nc + c, 0)), whole3(bre), whole3(bim), whole3(cre),
                  whole3(cim), whole3(ar), whole3(ai), pl.BlockSpec((1, width), lambda b_, c: (0, 0))],
        out_specs=[pl.BlockSpec((CHUNK, width), lambda b_, c: (b_ * nc + c, 0)),
                   pl.BlockSpec((CHUNK, width), lambda b_, c: (b_ * nc + c, 0)),
                   pl.BlockSpec((None, 2, S5_Q, ns, LANES), lambda b_, c: (b_ * nc + c, 0, 0, 0, 0))],
        out_shape=[jax.ShapeDtypeStruct((rows, width), F32), jax.ShapeDtypeStruct((rows, width), BF16),
                   jax.ShapeDtypeStruct((nb * nc, 2, S5_Q, ns, LANES), F32)],
        scratch_shapes=[pltpu.VMEM((S5_Q, CHUNK * ns, LANES), F32), pltpu.VMEM((S5_Q, CHUNK * ns, LANES), F32),
                        pltpu.VMEM((2, S5_Q, ns, LANES), F32)],
        compiler_params=_params(("arbitrary", "arbitrary")))(pa, bre, bim, cre, cim, ar, ai, dvec)


def s5_bwd(pa, dys, states, bre, bim, cre, cim, ar, ai, dvec, nb, name):
    rows = pa.shape[0]
    width = pa.shape[1] // 2
    ns = width // LANES
    nc = rows // nb // CHUNK

    def body(u_ref, dy_ref, sin_ref, bre_ref, bim_ref, cre_ref, cim_ref, ar_ref, ai_ref, d_ref,
             du_ref, dbre_ref, dbim_ref, dcre_ref, dcim_ref, dar_ref, dai_ref, dd_ref,
             xr_ref, xi_ref, lr_ref, li_ref, st_ref, lam_ref):
        bb = pl.program_id(0)
        step_i = pl.program_id(1)

        @pl.when(jnp.logical_and(bb == 0, step_i == 0))
        def _():
            for r in (dbre_ref, dbim_ref, dcre_ref, dcim_ref, dar_ref, dai_ref, dd_ref):
                r[...] = jnp.zeros_like(r)

        @pl.when(step_i == 0)
        def _():
            lam_ref[...] = jnp.zeros_like(lam_ref)

        u = u_ref[...]
        dy = dy_ref[...]
        st_ref[...] = sin_ref[...]
        _s5_fill_bu(u, bre_ref, bim_ref, xr_ref, xi_ref, ns)
        _s5_scan(xr_ref, xi_ref, ar_ref, ai_ref, st_ref, ns)
        dd_ref[...] += jnp.sum(dy * u, axis=0, keepdims=True)
        for s in range(ns):
            dyb = dy[:, s * LANES:(s + 1) * LANES].astype(BF16)
            gr = _dot(dyb, cre_ref[s], ((1,), (1,)))
            gi = -_dot(dyb, cim_ref[s], ((1,), (1,)))
            for q in range(S5_Q):
                lr_ref[q, pl.ds(s, CHUNK, stride=ns), :] = gr[:, q * LANES:(q + 1) * LANES]
                li_ref[q, pl.ds(s, CHUNK, stride=ns), :] = gi[:, q * LANES:(q + 1) * LANES]
                xr = xr_ref[q, pl.ds(s, CHUNK, stride=ns), :].astype(BF16)
                xi = xi_ref[q, pl.ds(s, CHUNK, stride=ns), :].astype(BF16)
                dcre_ref[s, q * LANES:(q + 1) * LANES, :] += _dot(xr, dyb, ((0,), (0,)))
                dcim_ref[s, q * LANES:(q + 1) * LANES, :] -= _dot(xi, dyb, ((0,), (0,)))
        ar = [ar_ref[q] for q in range(S5_Q)]
        ai = [ai_ref[q] for q in range(S5_Q)]

        def one(t_rows, p_r, p_i, carry):
            out = []
            for q in range(S5_Q):
                l_r, l_i, da_r, da_i = carry[4 * q:4 * q + 4]
                n_r = lr_ref[q, t_rows, :] + ar[q] * l_r + ai[q] * l_i
                n_i = li_ref[q, t_rows, :] + ar[q] * l_i - ai[q] * l_r
                lr_ref[q, t_rows, :] = n_r
                li_ref[q, t_rows, :] = n_i
                xpr, xpi = p_r(q), p_i(q)
                out += [n_r, n_i, da_r + n_r * xpr + n_i * xpi, da_i + n_i * xpr - n_r * xpi]
            return tuple(out)

        def step(k, carry):
            t = CHUNK - 1 - k
            t_rows = pl.ds(pl.multiple_of(t * ns, ns), ns)
            p_rows = pl.ds(pl.multiple_of((t - 1) * ns, ns), ns)
            return one(t_rows, lambda q: xr_ref[q, p_rows, :], lambda q: xi_ref[q, p_rows, :], carry)

        init = []
        zero = jnp.zeros((ns, LANES), F32)
        for q in range(S5_Q):
            init += [lam_ref[0, q], lam_ref[1, q], zero, zero]
        carry = lax.fori_loop(0, CHUNK - 1, step, tuple(init), unroll=2)
        carry = one(pl.ds(0, ns), lambda q: sin_ref[0, q], lambda q: sin_ref[1, q], carry)
        for q in range(S5_Q):
            lam_ref[0, q] = carry[4 * q]
            lam_ref[1, q] = carry[4 * q + 1]
            dar_ref[q] += carry[4 * q + 2]
            dai_ref[q] += carry[4 * q + 3]
        for s in range(ns):
            cols = slice(s * LANES, (s + 1) * LANES)
            ub = u[:, cols].astype(BF16)
            acc = d_ref[:, cols] * dy[:, cols]
            for q in range(S5_Q):
                qs = slice(q * LANES, (q + 1) * LANES)
                lr = lr_ref[q, pl.ds(s, CHUNK, stride=ns), :].astype(BF16)
                li = li_ref[q, pl.ds(s, CHUNK, stride=ns), :].astype(BF16)
                dbre_ref[s, :, qs] += _dot(ub, lr, ((0,), (0,)))
                dbim_ref[s, :, qs] += _dot(ub, li, ((0,), (0,)))
                acc = acc + _dot(lr, bre_ref[s, :, qs], ((1,), (1,))) + _dot(li, bim_ref[s, :, qs], ((1,), (1,)))
            du_ref[:, cols] = acc.astype(du_ref.dtype)

    whole3 = lambda a: pl.BlockSpec(a.shape, lambda b_, c: (0, 0, 0))
    rowblk = pl.BlockSpec((CHUNK, width), lambda b_, c: (b_ * nc + nc - 1 - c, 0))
    scr = pltpu.VMEM((S5_Q, CHUNK * ns, LANES), F32)
    return pl.pallas_call(
        body, name=name, grid=(nb, nc),
        in_specs=[rowblk, rowblk,
                  pl.BlockSpec((None, 2, S5_Q, ns, LANES), lambda b_, c: (b_ * nc + nc - 1 - c, 0, 0, 0, 0)),
                  whole3(bre), whole3(bim), whole3(cre), whole3(cim), whole3(ar), whole3(ai),
                  pl.BlockSpec((1, width), lambda b_, c: (0, 0))],
        out_specs=[rowblk, whole3(bre), whole3(bim), whole3(cre), whole3(cim), whole3(ar), whole3(ai),
                   pl.BlockSpec((1, width), lambda b_, c: (0, 0))],
        out_shape=[jax.ShapeDtypeStruct((rows, width), BF16), jax.ShapeDtypeStruct(bre.shape, F32),
                   jax.ShapeDtypeStruct(bim.shape, F32), jax.ShapeDtypeStruct(cre.shape, F32),
                   jax.ShapeDtypeStruct(cim.shape, F32), jax.ShapeDtypeStruct(ar.shape, F32),
                   jax.ShapeDtypeStruct(ai.shape, F32), jax.ShapeDtypeStruct((1, width), F32)],
        scratch_shapes=[scr, scr, scr, scr, pltpu.VMEM((2, S5_Q, ns, LANES), F32), pltpu.VMEM((2, S5_Q, ns, LANES), F32)],
        compiler_params=_params(("arbitrary", "arbitrary")))(pa, dys, states, bre, bim, cre, cim, ar, ai, dvec)


def _s5_discretize(lam_re, lam_im, log_dt, b_re, b_im):
    dt = jnp.exp(log_dt)[:, None]
    mag = jnp.exp(lam_re * dt)
    ar, ai = mag * jnp.cos(lam_im * dt), mag * jnp.sin(lam_im * dt)
    den = lam_re * lam_re + lam_im * lam_im
    qr = ((ar - 1.0) * lam_re + ai * lam_im) / den
    qi = (ai * lam_re - (ar - 1.0) * lam_im) / den
    bbr = qr[..., None] * b_re - qi[..., None] * b_im
    bbi = qr[..., None] * b_im + qi[..., None] * b_re
    return ar, ai, bbr, bbi


def _s5_expand(ar, ai, bbr, bbi, c_re, c_im):
    g, p, h = bbr.shape
    gps = LANES // h
    ns = g // gps
    eye = jnp.eye(gps, dtype=F32)

    def bexp(b):
        return jnp.einsum("sgph,gk->sghkp", b.reshape(ns, gps, p, h), eye).reshape(ns, gps * h, gps * p)

    def cexp(c):
        return jnp.einsum("sghp,gk->sgpkh", c.reshape(ns, gps, h, p), eye).reshape(ns, gps * p, gps * h)

    def aexp(a):
        return a.reshape(ns, S5_Q, LANES).transpose(1, 0, 2)

    return (bexp(bbr).astype(BF16), bexp(bbi).astype(BF16), cexp(c_re).astype(BF16), cexp(c_im).astype(BF16),
            aexp(ar), aexp(ai))


def _s5_contract(dbre, dbim, dcre, dcim, dar, dai, g, p, h):
    gps = LANES // h
    ns = g // gps
    eye = jnp.eye(gps, dtype=F32)
    bcon = lambda d: jnp.einsum("sghkp,gk->sgph", d.reshape(ns, gps, h, gps, p), eye).reshape(g, p, h)
    ccon = lambda d: jnp.einsum("sgpkh,gk->sghp", d.reshape(ns, gps, p, gps, h), eye).reshape(g, h, p)
    acon = lambda d: d.transpose(1, 0, 2).reshape(g, p)
    return bcon(dbre), bcon(dbim), ccon(dcre), ccon(dcim), acon(dar), acon(dai)


PROJ_BLOCK = 256


def _ml_proj_tile(cpre, xb, wq, wk, wv, gq, gk, gv):
    xc = _silu(cpre)
    q = _dot_nn(xc, wq)
    k = _dot_nn(xc, wk)
    v = _dot_nn(xb, wv)
    return q, k, v, _dot_nn(q, gq) + _dot_nn(k, gk) + _dot_nn(v, gv)


def ml_proj_fwd(cpre, xb, wq, wk, wv, gq, gk, gv, name):
    rows, width = cpre.shape
    pb = wq.shape[1]
    nblk = width // pb
    tr = _tile(rows, 1088, 16)

    def body(c_ref, x_ref, wq_ref, wk_ref, wv_ref, gq_ref, gk_ref, gv_ref, q_ref, k_ref, v_ref, g_ref):
        j = pl.program_id(1)
        q, k, v, g = _ml_proj_tile(c_ref[...], x_ref[...], wq_ref[...], wk_ref[...], wv_ref[...],
                                   gq_ref[...], gk_ref[...], gv_ref[...])
        q_ref[...] = q
        k_ref[...] = k
        v_ref[...] = v

        @pl.when(j == 0)
        def _():
            g_ref[...] = jnp.zeros_like(g_ref)

        g_ref[...] += g

    rb = pl.BlockSpec((tr, pb), lambda i, j: (i, j))
    wb = pl.BlockSpec((None, pb, pb), lambda i, j: (j, 0, 0))
    gwb = pl.BlockSpec((None, pb, LANES), lambda i, j: (j, 0, 0))
    return pl.pallas_call(
        body, name=name, grid=(rows // tr, nblk), in_specs=[rb, rb, wb, wb, wb, gwb, gwb, gwb],
        out_specs=[rb, rb, rb, pl.BlockSpec((tr, LANES), lambda i, j: (i, 0))],
        out_shape=[jax.ShapeDtypeStruct((rows, width), F32)] * 3 + [jax.ShapeDtypeStruct((rows, LANES), F32)],
        compiler_params=_params(("arbitrary", "arbitrary")))(cpre, xb, wq, wk, wv, gq, gk, gv)


def ml_proj_bwd(cpre, xb, wq, wk, wv, gq, gk, gv, dq, dk, dv, dg, dcp_extra, name):
    rows, width = cpre.shape
    pb = wq.shape[1]
    nblk = width // pb
    tr = _tile(rows, 1088, 16)

    def body(c_ref, x_ref, wq_ref, wk_ref, wv_ref, gq_ref, gk_ref, gv_ref, dq_ref, dk_ref, dv_ref, dg_ref, e_ref,
             dc_ref, dx_ref, *dw_refs):
        i = pl.program_id(1)
        _, vjp = jax.vjp(_ml_proj_tile, c_ref[...], x_ref[...], wq_ref[...], wk_ref[...], wv_ref[...],
                         gq_ref[...], gk_ref[...], gv_ref[...])
        grads = vjp((dq_ref[...], dk_ref[...], dv_ref[...], dg_ref[...]))
        dc_ref[...] = grads[0] + e_ref[...]
        dx_ref[...] = grads[1]

        @pl.when(i == 0)
        def _():
            for r in dw_refs:
                r[...] = jnp.zeros_like(r)

        for r, gval in zip(dw_refs, grads[2:]):
            r[...] += gval

    rb = pl.BlockSpec((tr, pb), lambda j, i: (i, j))
    wb = pl.BlockSpec((None, pb, pb), lambda j, i: (j, 0, 0))
    gwb = pl.BlockSpec((None, pb, LANES), lambda j, i: (j, 0, 0))
    gb = pl.BlockSpec((tr, LANES), lambda j, i: (i, 0))
    wshape = jax.ShapeDtypeStruct((nblk, pb, pb), F32)
    gshape = jax.ShapeDtypeStruct((nblk, pb, LANES), F32)
    return pl.pallas_call(
        body, name=name, grid=(nblk, rows // tr), in_specs=[rb, rb, wb, wb, wb, gwb, gwb, gwb, rb, rb, rb, gb, rb],
        out_specs=[rb, rb] + [wb] * 3 + [gwb] * 3,
        out_shape=[jax.ShapeDtypeStruct((rows, width), F32)] * 2 + [wshape] * 3 + [gshape] * 3,
        compiler_params=_params(("arbitrary", "arbitrary")))(cpre, xb, wq, wk, wv, gq, gk, gv, dq, dk, dv, dg, dcp_extra)


def _ml_gates_tile(gl, bg, nh):
    x = gl + bg
    bcum = _dot(_tri(CHUNK), _log_sigmoid(x), ((1,), (0,)), precision=HI)
    lane = lax.broadcasted_iota(jnp.int32, x.shape, 1)
    return jnp.where(lane < nh, x, jnp.where(lane < 2 * nh, bcum, 0.0))


def _ml_core_tile(q, k, v, colg, rowg, cpre, zb, nw, sk, cst, nst, m_prev):
    c, dh = q.shape
    igc, bc = _lane_pick(colg, 0), _lane_pick(colg, 1)
    igr, br = _row_pick(rowg, 0), _row_pick(rowg, 1)
    causal = _tri(c) > 0
    dmat = jnp.where(causal, bc - br + igr, -jnp.inf)
    inter = bc + m_prev
    mt = lax.stop_gradient(jnp.maximum(inter, jnp.max(dmat, axis=1, keepdims=True)))
    wt = jnp.exp(dmat - mt)
    w_prev = jnp.exp(inter - mt)
    qs = q * (dh ** -0.5)
    s = _dot_nt(qs, k) * wt
    num = _dot_nn(s, v) + w_prev * _dot_nn(qs, cst)
    den = jnp.sum(s, axis=1, keepdims=True) + w_prev * jnp.sum(qs * nst, axis=1, keepdims=True)
    h = num * (1.0 / jnp.maximum(jnp.abs(den), jnp.exp(-mt)))
    last = (lax.broadcasted_iota(jnp.int32, (c, 1), 0) == c - 1).astype(F32)
    blast = jnp.sum(bc * last, axis=0, keepdims=True)
    g = blast - bc + igc
    m_new = lax.stop_gradient(jnp.maximum(blast + m_prev, jnp.max(g, axis=0, keepdims=True)))
    decay = jnp.exp(blast + m_prev - m_new)
    wk = jnp.exp(g - m_new) * k
    c_new = decay * cst + _dot_tn(wk, v)
    n_new = decay * nst + jnp.sum(wk, axis=0, keepdims=True)
    mu = jnp.mean(h, axis=1, keepdims=True)
    hc = h - mu
    var = jnp.mean(hc * hc, axis=1, keepdims=True)
    out = hc * lax.rsqrt(var + HEAD_NORM_EPS) * nw + sk * _silu(cpre)
    return out * _silu(zb), c_new, n_new, m_new


def _ml_core_specs(nc, dh, rev):
    ch = (lambda c: nc - 1 - c) if rev else (lambda c: c)
    rb = pl.BlockSpec((CHUNK, dh), lambda b_, c, h: (b_ * nc + ch(c), h))
    colb = pl.BlockSpec((None, CHUNK, 2), lambda b_, c, h: (h, b_ * nc + ch(c), 0))
    rowb = pl.BlockSpec((None, None, 2, CHUNK), lambda b_, c, h: (b_ * nc + ch(c), h, 0, 0))
    pb = pl.BlockSpec((1, dh), lambda b_, c, h: (0, h))
    cb = pl.BlockSpec((None, None, dh, dh), lambda b_, c, h: (b_ * nc + ch(c), h, 0, 0))
    nb_ = pl.BlockSpec((None, None, 1, dh), lambda b_, c, h: (b_ * nc + ch(c), h, 0, 0))
    mb = pl.BlockSpec((None, None, 1, 1), lambda b_, c, h: (b_ * nc + ch(c), h, 0, 0))
    return rb, colb, rowb, pb, cb, nb_, mb


def ml_core_fwd(q, k, v, colg, rowg, cpre, zb, nw, sk, nb, nh, name):
    rows, width = q.shape
    dh = width // nh
    nc = rows // nb // CHUNK
    rb, colb, rowb, pb, cb, nb_, mb = _ml_core_specs(nc, dh, False)

    def body(q_ref, k_ref, v_ref, col_ref, row_ref, c_ref, z_ref, nw_ref, sk_ref, y_ref, cs_ref, ns_ref, ms_ref,
             cst_ref, nst_ref, mst_ref):
        c = pl.program_id(1)
        h = pl.program_id(2)

        @pl.when(c == 0)
        def _():
            cst_ref[h] = jnp.zeros((dh, dh), F32)
            nst_ref[h] = jnp.zeros((1, dh), F32)
            mst_ref[h] = jnp.zeros((1, 1), F32)

        cst, nst, m_prev = cst_ref[h], nst_ref[h], mst_ref[h]
        cs_ref[...] = cst
        ns_ref[...] = nst
        ms_ref[...] = m_prev
        y, c_new, n_new, m_new = _ml_core_tile(q_ref[...], k_ref[...], v_ref[...], col_ref[...], row_ref[...],
                                               c_ref[...], z_ref[...], nw_ref[...], sk_ref[...], cst, nst, m_prev)
        y_ref[...] = y.astype(BF16)
        cst_ref[h] = c_new
        nst_ref[h] = n_new
        mst_ref[h] = m_new

    nbc = nb * nc
    return pl.pallas_call(
        body, name=name, grid=(nb, nc, nh), in_specs=[rb, rb, rb, colb, rowb, rb, rb, pb, pb],
        out_specs=[rb, cb, nb_, mb],
        out_shape=[jax.ShapeDtypeStruct((rows, width), BF16), jax.ShapeDtypeStruct((nbc, nh, dh, dh), F32),
                   jax.ShapeDtypeStruct((nbc, nh, 1, dh), F32), jax.ShapeDtypeStruct((nbc, nh, 1, 1), F32)],
        scratch_shapes=[pltpu.VMEM((nh, dh, dh), F32), pltpu.VMEM((nh, 1, dh), F32), pltpu.VMEM((nh, 1, 1), F32)],
        compiler_params=_params(("arbitrary", "arbitrary", "arbitrary")))(q, k, v, colg, rowg, cpre, zb, nw, sk)


def ml_core_bwd(q, k, v, colg, rowg, cpre, zb, nw, sk, cs, ns, ms, dy, nb, nh, name):
    rows, width = q.shape
    dh = width // nh
    nc = rows // nb // CHUNK
    rb, colb, rowb, pb, cb, nb_, mb = _ml_core_specs(nc, dh, True)

    def body(q_ref, k_ref, v_ref, col_ref, row_ref, c_ref, z_ref, nw_ref, sk_ref, cs_ref, ns_ref, ms_ref, dy_ref,
             dq_ref, dk_ref, dv_ref, dc_ref, dz_ref, dcol_ref, drow_ref, dnw_ref, dsk_ref, dcst_ref, dnst_ref):
        bb = pl.program_id(0)
        step = pl.program_id(1)
        h = pl.program_id(2)

        @pl.when(jnp.logical_and(bb == 0, jnp.logical_and(step == 0, h == 0)))
        def _():
            dnw_ref[...] = jnp.zeros_like(dnw_ref)
            dsk_ref[...] = jnp.zeros_like(dsk_ref)

        @pl.when(step == 0)
        def _():
            dcst_ref[h] = jnp.zeros((dh, dh), F32)
            dnst_ref[h] = jnp.zeros((1, dh), F32)

        m_prev = ms_ref[...]

        def f(*a):
            return _ml_core_tile(*a, m_prev)[:3]

        _, vjp = jax.vjp(f, q_ref[...], k_ref[...], v_ref[...], col_ref[...], row_ref[...], c_ref[...], z_ref[...],
                         nw_ref[...], sk_ref[...], cs_ref[...], ns_ref[...])
        g = vjp((dy_ref[...], dcst_ref[h], dnst_ref[h]))
        dq_ref[...] = g[0]
        dk_ref[...] = g[1]
        dv_ref[...] = g[2]
        dcol_ref[...] = g[3]
        drow_ref[...] = g[4]
        dc_ref[...] = g[5]
        dz_ref[...] = g[6].astype(dz_ref.dtype)
        dnw_ref[h] += g[7]
        dsk_ref[h] += g[8]
        dcst_ref[h] = g[9]
        dnst_ref[h] = g[10]

    nbc = nb * nc
    accb = pl.BlockSpec((nh, 1, dh), lambda b_, c, h: (0, 0, 0))
    return pl.pallas_call(
        body, name=name, grid=(nb, nc, nh), in_specs=[rb, rb, rb, colb, rowb, rb, rb, pb, pb, cb, nb_, mb, rb],
        out_specs=[rb, rb, rb, rb, rb, colb, rowb, accb, accb],
        out_shape=[jax.ShapeDtypeStruct((rows, width), F32)] * 4 + [jax.ShapeDtypeStruct((rows, width), BF16)]
        + [jax.ShapeDtypeStruct(colg.shape, F32), jax.ShapeDtypeStruct(rowg.shape, F32),
           jax.ShapeDtypeStruct((nh, 1, dh), F32), jax.ShapeDtypeStruct((nh, 1, dh), F32)],
        scratch_shapes=[pltpu.VMEM((nh, dh, dh), F32), pltpu.VMEM((nh, 1, dh), F32)],
        compiler_params=_params(("arbitrary", "arbitrary", "arbitrary")))(
            q, k, v, colg, rowg, cpre, zb, nw, sk, cs, ns, ms, dy)


def _ssd_dt_tile(dtr, bias, alog):
    dt = _softplus(dtr + bias)
    cum = _dot(_tri(CHUNK), dt * (-jnp.exp(alog)), ((1,), (0,)), precision=HI)
    return dt, cum


def _ssd_tile(xcs, bmc, cmc, cols, rows_, z, dvec, gn, states, hpg):
    npair = hpg // 2
    hd = SSD_HEAD_DIM
    xs = [_silu(x) for x in xcs]
    bm, cm = _silu(bmc), _silu(cmc)
    cb = _dot_nt(cm, bm)
    causal = _tri(CHUNK) > 0
    lane_lo = lax.broadcasted_iota(jnp.int32, (1, 2 * hd), 1) < hd
    lastsel = (lax.broadcasted_iota(jnp.int32, (CHUNK, 1), 0) == CHUNK - 1).astype(F32)
    heads = []
    for r in range(hpg):
        dtc, cumc = _lane_pick(cols, r), _lane_pick(cols, hpg + r)
        dtrow, cumr = _row_pick(rows_, r), _row_pick(rows_, hpg + r)
        w = cb * jnp.exp(jnp.where(causal, cumc - cumr, -jnp.inf)) * dtrow
        last = jnp.sum(cumc * lastsel, axis=0, keepdims=True)
        heads.append((w, jnp.exp(cumc), jnp.exp(last - cumc) * dtc, jnp.exp(last)))
    ys, new_states = [], []
    for j in range(npair):
        (wa, ea, da, la), (wb, eb, db, lb) = heads[2 * j], heads[2 * j + 1]
        yi = jnp.where(lane_lo, _dot_nn(wa, xs[j]), _dot_nn(wb, xs[j]))
        ys.append(yi + jnp.where(lane_lo, ea, eb) * _dot_nn(cm, states[j]))
        xd = xs[j] * jnp.where(lane_lo, da, db)
        new_states.append(jnp.where(lane_lo, la, lb) * states[j] + _dot_tn(bm, xd))
    y = jnp.concatenate(ys, axis=1) + dvec * jnp.concatenate(xs, axis=1)
    yg = y * _silu(z)
    yn = yg * lax.rsqrt(jnp.mean(yg * yg, axis=1, keepdims=True) + NORM_EPS) * gn
    return yn, new_states


def _ssd_specs(nc, hpg, ng, rev):
    npair = hpg // 2
    gw = hpg * SSD_HEAD_DIM
    xblocks = ng * npair
    ch = (lambda c: nc - 1 - c) if rev else (lambda c: c)
    xs = [pl.BlockSpec((CHUNK, LANES), functools.partial(lambda b_, c, g, jj: (b_ * nc + ch(c), g * npair + jj), jj=j))
          for j in range(npair)]
    bmb = pl.BlockSpec((CHUNK, SSD_STATE), lambda b_, c, g: (b_ * nc + ch(c), xblocks + g))
    cmb = pl.BlockSpec((CHUNK, SSD_STATE), lambda b_, c, g: (b_ * nc + ch(c), xblocks + ng + g))
    colb = pl.BlockSpec((None, CHUNK, 2 * hpg), lambda b_, c, g: (g, b_ * nc + ch(c), 0))
    rowb = pl.BlockSpec((None, None, 2 * hpg, CHUNK), lambda b_, c, g: (b_ * nc + ch(c), g, 0, 0))
    zb = pl.BlockSpec((CHUNK, gw), lambda b_, c, g: (b_ * nc + ch(c), g))
    pb = pl.BlockSpec((1, gw), lambda b_, c, g: (0, g))
    sb = pl.BlockSpec((None, None, npair, SSD_STATE, 2 * SSD_HEAD_DIM), lambda b_, c, g: (b_ * nc + ch(c), g, 0, 0, 0))
    return xs, bmb, cmb, colb, rowb, zb, pb, sb


def ssd_core_fwd(cpre, cols, rows_, z, dvec, gn, nb, hpg, name):
    rows = cpre.shape[0]
    inner = z.shape[1]
    ng = inner // (hpg * SSD_HEAD_DIM)
    npair = hpg // 2
    nc = rows // nb // CHUNK
    xs, bmb, cmb, colb, rowb, zb, pb, sb = _ssd_specs(nc, hpg, ng, False)

    def body(*refs):
        x_refs = refs[:npair]
        bm_ref, cm_ref, col_ref, row_ref, z_ref, d_ref, gn_ref, y_ref, so_ref, st_ref = refs[npair:]
        c = pl.program_id(1)
        g = pl.program_id(2)

        @pl.when(c == 0)
        def _():
            st_ref[g] = jnp.zeros((npair, SSD_STATE, 2 * SSD_HEAD_DIM), F32)

        so_ref[...] = st_ref[g]
        states = [st_ref[g, j] for j in range(npair)]
        yn, new_states = _ssd_tile([r[...] for r in x_refs], bm_ref[...], cm_ref[...], col_ref[...], row_ref[...],
                                   z_ref[...], d_ref[...], gn_ref[...], states, hpg)
        y_ref[...] = yn.astype(BF16)
        for j in range(npair):
            st_ref[g, j] = new_states[j]

    return pl.pallas_call(
        body, name=name, grid=(nb, nc, ng), in_specs=xs + [bmb, cmb, colb, rowb, zb, pb, pb],
        out_specs=[zb, sb],
        out_shape=[jax.ShapeDtypeStruct((rows, inner), BF16),
                   jax.ShapeDtypeStruct((nb * nc, ng, npair, SSD_STATE, 2 * SSD_HEAD_DIM), F32)],
        scratch_shapes=[pltpu.VMEM((ng, npair, SSD_STATE, 2 * SSD_HEAD_DIM), F32)],
        compiler_params=_params(("arbitrary", "arbitrary", "arbitrary")))(
            *([cpre] * npair), cpre, cpre, cols, rows_, z, dvec, gn)


def ssd_core_bwd(cpre, cols, rows_, z, dvec, gn, states, dyn, nb, hpg, name):
    rows = cpre.shape[0]
    inner = z.shape[1]
    gw = hpg * SSD_HEAD_DIM
    ng = inner // gw
    npair = hpg // 2
    nc = rows // nb // CHUNK
    xs, bmb, cmb, colb, rowb, zb, pb, sb = _ssd_specs(nc, hpg, ng, True)

    def body(*refs):
        x_refs = refs[:npair]
        (bm_ref, cm_ref, col_ref, row_ref, z_ref, d_ref, gn_ref, s_ref, dy_ref,
         dx_ref, dbm_ref, dcm_ref, dcol_ref, drow_ref, dz_ref, dd_ref, dgn_ref, dst_ref) = refs[npair:]
        bb = pl.program_id(0)
        step = pl.program_id(1)
        g = pl.program_id(2)

        @pl.when(jnp.logical_and(bb == 0, jnp.logical_and(step == 0, g == 0)))
        def _():
            dd_ref[...] = jnp.zeros_like(dd_ref)
            dgn_ref[...] = jnp.zeros_like(dgn_ref)

        @pl.when(step == 0)
        def _():
            dst_ref[g] = jnp.zeros((npair, SSD_STATE, 2 * SSD_HEAD_DIM), F32)

        def f(xcs, bmc, cmc, cv, rv, zv, dv_, gv, sts):
            return _ssd_tile(xcs, bmc, cmc, cv, rv, zv, dv_, gv, sts, hpg)

        _, vjp = jax.vjp(f, [r[...] for r in x_refs], bm_ref[...], cm_ref[...], col_ref[...], row_ref[...], z_ref[...],
                         d_ref[...], gn_ref[...], [s_ref[j] for j in range(npair)])
        gr = vjp((dy_ref[...], [dst_ref[g, j] for j in range(npair)]))
        dx_ref[...] = jnp.concatenate(gr[0], axis=1)
        dbm_ref[...] = gr[1]
        dcm_ref[...] = gr[2]
        dcol_ref[...] = gr[3]
        drow_ref[...] = gr[4]
        dz_ref[...] = gr[5].astype(dz_ref.dtype)
        dd_ref[g] += gr[6]
        dgn_ref[g] += gr[7]
        for j in range(npair):
            dst_ref[g, j] = gr[8][j]

    ch = lambda c: nc - 1 - c
    nblk = pl.BlockSpec((CHUNK, SSD_STATE), lambda b_, c, g: (b_ * nc + ch(c), g))
    accb = pl.BlockSpec((ng, 1, gw), lambda b_, c, g: (0, 0, 0))
    return pl.pallas_call(
        body, name=name, grid=(nb, nc, ng), in_specs=xs + [bmb, cmb, colb, rowb, zb, pb, pb, sb, zb],
        out_specs=[zb, nblk, nblk, colb, rowb, zb, accb, accb],
        out_shape=[jax.ShapeDtypeStruct((rows, inner), F32), jax.ShapeDtypeStruct((rows, ng * SSD_STATE), F32),
                   jax.ShapeDtypeStruct((rows, ng * SSD_STATE), F32), jax.ShapeDtypeStruct(cols.shape, F32),
                   jax.ShapeDtypeStruct(rows_.shape, F32), jax.ShapeDtypeStruct((rows, inner), BF16),
                   jax.ShapeDtypeStruct((ng, 1, gw), F32), jax.ShapeDtypeStruct((ng, 1, gw), F32)],
        scratch_shapes=[pltpu.VMEM((ng, npair, SSD_STATE, 2 * SSD_HEAD_DIM), F32)],
        compiler_params=_params(("arbitrary", "arbitrary", "arbitrary")))(
            *([cpre] * npair), cpre, cpre, cols, rows_, z, dvec, gn, states, dyn)


def _hw_expand(w):
    n, bi, _ = w.shape
    per = PROJ_BLOCK // bi
    tiled = jnp.tile(w.reshape(n // per, PROJ_BLOCK, bi), (1, 1, per))
    return jnp.where(_hw_mask(bi), tiled, 0.0)


def _hw_mask(bi):
    r = lax.broadcasted_iota(jnp.int32, (PROJ_BLOCK, PROJ_BLOCK), 0) // bi
    c = lax.broadcasted_iota(jnp.int32, (PROJ_BLOCK, PROJ_BLOCK), 1) // bi
    return r == c


def _hw_contract(d, bi=QKV_BLOCK):
    per = PROJ_BLOCK // bi
    kept = jnp.where(_hw_mask(bi), d, 0.0)
    return kept.reshape(d.shape[0], PROJ_BLOCK, per, bi).sum(axis=2).reshape(-1, bi, bi)


def _wg_expand(wg, width):
    pad = jnp.pad(wg, ((0, 0), (0, LANES - wg.shape[1])))
    return [pad[i * width:(i + 1) * width].reshape(width // PROJ_BLOCK, PROJ_BLOCK, LANES) for i in range(3)]


def _wg_contract(dgs, ngate):
    return jnp.concatenate([d[:, :, :ngate].reshape(-1, ngate) for d in dgs], axis=0)


def _pad_lanes(a):
    return jnp.pad(a, ((0, 0), (0, LANES - a.shape[1])))


def _pairs_to_layouts(first, second, ngrp, per, nbc):
    rows = first.shape[0]
    both = jnp.concatenate([first.reshape(rows, ngrp, per), second.reshape(rows, ngrp, per)], axis=2)
    return both.transpose(1, 0, 2), both.reshape(nbc, CHUNK, ngrp, 2 * per).transpose(0, 2, 3, 1)


def _layouts_to_pairs(dcols, drows, ngrp, per):
    rows = dcols.shape[1]
    both = dcols.transpose(1, 0, 2) + drows.transpose(0, 3, 1, 2).reshape(rows, ngrp, 2 * per)
    return both[:, :, :per].reshape(rows, ngrp * per), both[:, :, per:].reshape(rows, ngrp * per)


_EARLY = ("W0a", "W0xb", "W0zb", "glu")
_LATE = ("Wo0a", "Wo0b", "W1z", "W1x", "W1dt", "Wo1")


def _local_step(x, target, bw, sp, late_weights=None, late_grads=None, early_grads=None):
    nb, seq, d = x.shape
    nh, hpg = MLSTM_HEADS, SSD_HPG
    t_len = N_META + seq
    nc = -(-t_len // CHUNK)
    tp = nc * CHUNK
    rows = nb * tp
    nbc = nb * nc
    meta = sp["meta_tokens"]
    h0 = jnp.concatenate([jnp.broadcast_to(meta[None], (nb, N_META, d)), x, jnp.zeros((nb, tp - t_len, d), F32)], axis=1)
    h0 = h0.reshape(rows, d)
    tgt = jnp.pad(target, ((0, 0), (N_META, tp - t_len), (0, 0))).reshape(rows, d)

    n0 = norm_fwd(h0, sp["ab_norm"], "norm0")
    pa = mm(n0, bw["W0a"], "nn", "mm_pa")
    xb = mm(n0, bw["W0xb"], "nn", "mm_xb")
    zb = mm(n0, bw["W0zb"], "nn", "mm_zb")
    s5w = pa.shape[1] // 2
    mlw = xb.shape[1]
    s5_args = (sp["s5_lambda_re"], sp["s5_lambda_im"], sp["s5_log_dt"].reshape(-1), sp["s5_b_re"], sp["s5_b_im"])
    (ar, ai, bbr, bbi), s5_disc_vjp = jax.vjp(_s5_discretize, *s5_args)
    sg, spn, shh = bbr.shape
    bre, bim, cre, cim, are, aie = _s5_expand(ar, ai, bbr, bbi, sp["s5_c_re"], sp["s5_c_im"])
    ys5, gb, s5st = s5_fwd(pa, bre, bim, cre, cim, are, aie, sp["s5_d"], nb, "s5_fwd")
    tglu = mm(gb, bw["glu"], "nn", "mm_glu")

    def glu_tile(ys, tt, za, gbias):
        return _gelu(ys) * _sigmoid(tt + gbias) * _silu(za)

    ya = rowwise("glu_fwd", lambda i, ys, tt, pab, gbias: glu_tile(ys, tt, pab[:, s5w:], gbias),
                 [ys5, tglu, pa], [sp["s5_glu_b"]], [(s5w, BF16)], tr=_tile(rows, 256, 16))[0]

    cpre0 = conv_fwd(xb, sp["ml_conv_w"], sp["ml_conv_b"], nb, "ml_conv_fwd")
    wq_e, wk_e, wv_e = _hw_expand(sp["ml_wq"]), _hw_expand(sp["ml_wk"]), _hw_expand(sp["ml_wv"])
    gq, gk, gv = _wg_expand(sp["ml_w_gate"], mlw)
    q, k, v, gl = ml_proj_fwd(cpre0, xb, wq_e, wk_e, wv_e, gq, gk, gv, "ml_proj_fwd")
    bgate = _pad_lanes(sp["ml_b_gate"])
    gout = rowwise("ml_gates_fwd", lambda i, g_, b_: _ml_gates_tile(g_, b_, nh), [gl], [bgate], [(LANES, F32)], tr=CHUNK)[0]
    colg, rowg = _pairs_to_layouts(gout[:, :nh], gout[:, nh:2 * nh], nh, 1, nbc)
    yb, ml_cs, ml_ns, ml_ms = ml_core_fwd(q, k, v, colg, rowg, cpre0, zb, sp["ml_norm"], sp["ml_skip"], nb, nh, "ml_core_fwd")
    if late_weights is not None:
        bw = {**bw, **late_weights()}
    h1 = mm(ya, bw["Wo0a"], "nn", "mm_out0a", resid=h0)
    h1 = mm(yb, bw["Wo0b"], "nn", "mm_out0b", resid=h1)

    n1 = norm_fwd(h1, sp["ssd_norm"], "norm1")
    z1 = mm(n1, bw["W1z"], "nn", "mm_z1")
    xbc = mm(n1, bw["W1x"], "nn", "mm_xbc")
    dtr = mm(n1, bw["W1dt"], "nn", "mm_dt")
    inner = z1.shape[1]
    ng = inner // (hpg * SSD_HEAD_DIM)
    nhd = ng * hpg
    cpre1 = conv_fwd(xbc, sp["ssd_conv_w"], sp["ssd_conv_b"], nb, "ssd_conv_fwd")
    dt_bias, a_log = _pad_lanes(sp["ssd_dt_bias"]), _pad_lanes(sp["ssd_a_log"])
    dt, cum = rowwise("ssd_dt_fwd", lambda i, r_, b_, a_: _ssd_dt_tile(r_, b_, a_), [dtr], [dt_bias, a_log],
                      [(LANES, F32), (LANES, F32)], tr=CHUNK)
    cols, rws = _pairs_to_layouts(dt[:, :nhd], cum[:, :nhd], ng, hpg, nbc)
    dvec = jnp.repeat(sp["ssd_d"], SSD_HEAD_DIM, axis=1)
    yn, ssd_st = ssd_core_fwd(cpre1, cols, rws, z1, dvec, sp["ssd_gnorm"], nb, hpg, "ssd_core_fwd")
    h2 = mm(yn, bw["Wo1"], "nn", "mm_out1", resid=h1)

    tr_l = _tile(tp, 256, 16)
    per_ex = tp // tr_l

    def loss_tile(i, hb, tb, gfn):
        tpos = (i % per_ex) * tr_l + lax.broadcasted_iota(jnp.int32, (tr_l, 1), 0)
        mask = jnp.logical_and(tpos >= N_META, tpos < t_len).astype(F32)

        def lf(hh, gg):
            e = (_rms(hh, gg) - tb) * mask
            return 0.5 * jnp.sum(e * e) / d

        lval, (dh, dg) = jax.value_and_grad(lf, (0, 1))(hb, gfn)
        return dh, dh, jnp.full((1, LANES), lval, F32), dg

    fn = sp["final_norm"].reshape(1, d)
    dh2, dh2b, loss_acc, dfn = rowwise("loss", loss_tile, [h2, tgt], [fn], [(d, F32), (d, BF16)], [(1, LANES), (1, d)], tr=tr_l)

    gbig, gs = {}, {}
    gs["final_norm"] = dfn.reshape(sp["final_norm"].shape)
    dyn = mm(dh2b, bw["Wo1"], "nt", "mm_dyn")
    gbig["Wo1"] = mm(yn, dh2b, "tn", "mm_dWo1", out_dtype=BF16)
    dxs, dbm, dcm, dcols, drws, dz1, ddvec, dgn = ssd_core_bwd(cpre1, cols, rws, z1, dvec, sp["ssd_gnorm"], ssd_st, dyn,
                                                              nb, hpg, "ssd_core_bwd")
    gs["ssd_d"] = ddvec.reshape(1, nhd, SSD_HEAD_DIM).sum(axis=2)
    gs["ssd_gnorm"] = dgn.reshape(1, inner)
    ddt, dcum = _layouts_to_pairs(dcols, drws, ng, hpg)

    def ssd_dt_bwd_tile(i, r_, ddt_, dcum_, b_, a_):
        _, vjp = jax.vjp(_ssd_dt_tile, r_, b_, a_)
        return vjp((ddt_, dcum_))

    ddtr, dbias, dalog = rowwise("ssd_dt_bwd", ssd_dt_bwd_tile, [dtr, _pad_lanes(ddt), _pad_lanes(dcum)], [dt_bias, a_log],
                                 [(LANES, BF16)], [(1, LANES), (1, LANES)], tr=CHUNK)
    gs["ssd_dt_bias"] = dbias[:, :nhd]
    gs["ssd_a_log"] = dalog[:, :nhd]
    dcpre1 = jnp.concatenate([dxs, dbm, dcm], axis=1)
    dxbc, dcw1, dcb1 = conv_bwd(dcpre1, xbc, sp["ssd_conv_w"], nb, "ssd_conv_bwd")
    gs["ssd_conv_w"] = dcw1
    gs["ssd_conv_b"] = dcb1
    dn1 = mm(dz1, bw["W1z"], "nt", "mm_dn1z")
    dn1 = mm(dxbc, bw["W1x"], "nt", "mm_dn1x", resid=dn1)
    dn1 = mm(ddtr, bw["W1dt"], "nt", "mm_dn1dt", resid=dn1)
    gbig["W1z"] = mm(n1, dz1, "tn", "mm_dW1z", out_dtype=BF16)
    gbig["W1x"] = mm(n1, dxbc, "tn", "mm_dW1x", out_dtype=BF16)
    gbig["W1dt"] = mm(n1, ddtr, "tn", "mm_dW1dt", out_dtype=BF16)
    dh1, dh1b, dg1 = norm_bwd(h1, sp["ssd_norm"], dn1, dh2, "norm1_bwd")
    gs["ssd_norm"] = dg1

    gbig["Wo0a"] = mm(ya, dh1b, "tn", "mm_dWo0a", out_dtype=BF16)
    gbig["Wo0b"] = mm(yb, dh1b, "tn", "mm_dWo0b", out_dtype=BF16)
    if late_grads is not None:
        late_grads({n: gbig[n] for n in _LATE})
    dya = mm(dh1b, bw["Wo0a"], "nt", "mm_dya")
    dyb = mm(dh1b, bw["Wo0b"], "nt", "mm_dyb")
    (dq, dk, dv, dcp_skip, dzb, dcolg, drowg, dnw, dsk) = ml_core_bwd(
        q, k, v, colg, rowg, cpre0, zb, sp["ml_norm"], sp["ml_skip"], ml_cs, ml_ns, ml_ms, dyb, nb, nh, "ml_core_bwd")
    gs["ml_norm"] = dnw.reshape(1, mlw)
    gs["ml_skip"] = dsk.reshape(1, mlw)
    dig, dbcum = _layouts_to_pairs(dcolg, drowg, nh, 1)
    dgout = _pad_lanes(jnp.concatenate([dig, dbcum], axis=1))

    def ml_gates_bwd_tile(i, g_, dgo, b_):
        _, vjp = jax.vjp(lambda a, b: _ml_gates_tile(a, b, nh), g_, b_)
        return vjp(dgo)

    dgl, dbg = rowwise("ml_gates_bwd", ml_gates_bwd_tile, [gl, dgout], [bgate], [(LANES, F32)], [(1, LANES)], tr=CHUNK)
    gs["ml_b_gate"] = dbg[:, :2 * nh]
    dcpre0, dxb_v, dwq, dwk, dwv, dgq, dgk, dgv = ml_proj_bwd(cpre0, xb, wq_e, wk_e, wv_e, gq, gk, gv, dq, dk, dv, dgl,
                                                            dcp_skip, "ml_proj_bwd")
    gs["ml_wq"], gs["ml_wk"], gs["ml_wv"] = _hw_contract(dwq), _hw_contract(dwk), _hw_contract(dwv)
    gs["ml_w_gate"] = _wg_contract([dgq, dgk, dgv], 2 * nh)
    dxb, dcw0, dcb0 = conv_bwd(dcpre0, xb, sp["ml_conv_w"], nb, "ml_conv_bwd", resid=dxb_v)
    gs["ml_conv_w"] = dcw0
    gs["ml_conv_b"] = dcb0

    def glu_bwd_tile(i, ys, tt, pab, dy_, gbias):
        _, vjp = jax.vjp(glu_tile, ys, tt, pab[:, s5w:], gbias)
        return vjp(dy_)

    dys_direct, dtglu, dza, dglub = rowwise("glu_bwd", glu_bwd_tile, [ys5, tglu, pa, dya], [sp["s5_glu_b"]],
                                            [(s5w, F32), (s5w, BF16), (s5w, BF16)], [(1, s5w)], tr=_tile(rows, 256, 16))
    gs["s5_glu_b"] = dglub
    dgb = mm(dtglu, bw["glu"], "nt", "mm_dgb")
    gbig["glu"] = mm(gb, dtglu, "tn", "mm_dglu", out_dtype=BF16)

    def gelu_bwd_tile(i, ys, dg_, direct):
        _, vjp = jax.vjp(_gelu, ys)
        return vjp(dg_)[0] + direct

    dys5 = rowwise("gelu_bwd", gelu_bwd_tile, [ys5, dgb, dys_direct], [], [(s5w, F32)], tr=_tile(rows, 256, 16))[0]
    du, dbre, dbim, dcre, dcim, dare, daie, dd5 = s5_bwd(pa, dys5, s5st, bre, bim, cre, cim, are, aie, sp["s5_d"], nb, "s5_bwd")
    gs["s5_d"] = dd5
    dbbr, dbbi, dcr, dci, dar, dai = _s5_contract(dbre, dbim, dcre, dcim, dare, daie, sg, spn, shh)
    gs["s5_c_re"], gs["s5_c_im"] = dcr, dci
    (gs["s5_lambda_re"], gs["s5_lambda_im"], dlogdt, gs["s5_b_re"], gs["s5_b_im"]) = s5_disc_vjp((dar, dai, dbbr, dbbi))
    gs["s5_log_dt"] = dlogdt.reshape(1, -1)
    dpa = jnp.concatenate([du, dza], axis=1)
    gbig["W0a"] = mm(n0, dpa, "tn", "mm_dW0a", out_dtype=BF16)
    gbig["W0xb"] = mm(n0, dxb, "tn", "mm_dW0xb", out_dtype=BF16)
    gbig["W0zb"] = mm(n0, dzb, "tn", "mm_dW0zb", out_dtype=BF16)
    if early_grads is not None:
        early_grads({n: gbig[n] for n in _EARLY})
    dn0 = mm(dpa, bw["W0a"], "nt", "mm_dn0a")
    dn0 = mm(dxb, bw["W0xb"], "nt", "mm_dn0xb", resid=dn0)
    dn0 = mm(dzb, bw["W0zb"], "nt", "mm_dn0zb", resid=dn0)
    dh0, _, dg0 = norm_bwd(h0, sp["ab_norm"], dn0, dh1, "norm0_bwd")
    gs["ab_norm"] = dg0
    dh0 = dh0.reshape(nb, tp, d)
    gs["meta_tokens"] = jnp.sum(dh0[:, :N_META], axis=0)
    return loss_acc[0, 0], dh0, gbig, gs


N_DEV = 8
N_CHIP = 4
N_PEER_CHIPS = N_CHIP - 1
MESH = pl.DeviceIdType.MESH
_HBM = pl.BlockSpec(memory_space=pltpu.HBM)


def _place():
    x, y, c = lax.axis_index("x"), lax.axis_index("y"), lax.axis_index("c")
    return x, y, c, [(1 - x, y), (x, 1 - y), (1 - x, 1 - y)]


def gather_chips(vs, name):
    na = len(vs)

    def body(*refs):
        x_refs, out_refs = refs[:na], refs[na:2 * na]
        send_sems, recv_sems, local_sems = refs[2 * na:]
        x, y, c, chips = _place()
        k = 2 * x + y
        sibling = (x, y, 1 - c)

        def copy(i, kk, src, chip_k, half, to):
            return pltpu.make_async_remote_copy(
                src_ref=src, dst_ref=out_refs[i].at[chip_k, half], send_sem=send_sems.at[6 * i + kk],
                recv_sem=recv_sems.at[6 * i + kk], device_id=to, device_id_type=MESH)

        mine = [pltpu.make_async_copy(x_refs[i], out_refs[i].at[k], local_sems.at[i]) for i in range(na)]
        for cp in mine:
            cp.start()
        first = [copy(i, j, x_refs[i].at[c], k, c, (*chip, c)) for j, chip in enumerate(chips) for i in range(na)]
        for cp in first:
            cp.start()
        passed = []
        for j, (cx, cy) in enumerate(chips):
            kj = 2 * cx + cy
            for i in range(na):
                copy(i, j, out_refs[i].at[kj, c], kj, c, (cx, cy, c)).wait_recv()
                fwd = copy(i, 3 + j, out_refs[i].at[kj, c], kj, c, sibling)
                fwd.start()
                passed.append(fwd)
        for j, (cx, cy) in enumerate(chips):
            kj = 2 * cx + cy
            for i in range(na):
                copy(i, 3 + j, out_refs[i].at[kj, 1 - c], kj, 1 - c, sibling).wait_recv()
        for cp in first + passed:
            cp.wait_send()
        for cp in mine:
            cp.wait()

    return pl.pallas_call(
        body, name=name, out_shape=[jax.ShapeDtypeStruct((N_CHIP,) + v.shape, v.dtype) for v in vs],
        in_specs=[_HBM] * na, out_specs=[_HBM] * na,
        scratch_shapes=[pltpu.SemaphoreType.DMA((6 * na,)), pltpu.SemaphoreType.DMA((6 * na,)),
                        pltpu.SemaphoreType.DMA((na,))])(*vs)


def swap_halves(gs_, name):
    na = len(gs_)

    def body(*refs):
        g_refs, out_refs = refs[:na], refs[na:2 * na]
        send_sems, recv_sems = refs[2 * na:]
        x, y, c, _ = _place()
        cps = [pltpu.make_async_remote_copy(
            src_ref=g_refs[i].at[kk, 1 - c], dst_ref=out_refs[i].at[kk], send_sem=send_sems.at[N_CHIP * i + kk],
            recv_sem=recv_sems.at[N_CHIP * i + kk], device_id=(x, y, 1 - c), device_id_type=MESH)
            for i in range(na) for kk in range(N_CHIP)]
        for cp in cps:
            cp.start()
        for cp in cps:
            cp.wait()

    return pl.pallas_call(
        body, name=name, out_shape=[jax.ShapeDtypeStruct((N_CHIP,) + g.shape[2:], g.dtype) for g in gs_],
        in_specs=[_HBM] * na, out_specs=[_HBM] * na,
        scratch_shapes=[pltpu.SemaphoreType.DMA((N_CHIP * na,)), pltpu.SemaphoreType.DMA((N_CHIP * na,))])(*gs_)


def add_halves(g, other, core, name):
    _, _, m, n = g.shape
    tr = _tile(m, 256, 16)

    def body(core_ref, g_ref, o_ref, out_ref):
        out_ref[...] = (g_ref[...].astype(F32) + o_ref[...].astype(F32)).astype(out_ref.dtype)

    grid_spec = pltpu.PrefetchScalarGridSpec(
        num_scalar_prefetch=1, grid=(N_CHIP, m // tr),
        in_specs=[pl.BlockSpec((None, None, tr, n), lambda kk, i, core_ref: (kk, core_ref[0], i, 0)),
                  pl.BlockSpec((None, tr, n), lambda kk, i, core_ref: (kk, i, 0))],
        out_specs=pl.BlockSpec((None, tr, n), lambda kk, i, core_ref: (kk, i, 0)))
    return pl.pallas_call(body, name=name, grid_spec=grid_spec, out_shape=jax.ShapeDtypeStruct((N_CHIP, m, n), g.dtype),
                          compiler_params=_params(("arbitrary", "arbitrary")))(core.reshape(1).astype(jnp.int32), g, other)


def sequencer_exchange(srcs, scatter, collective_id, name):
    na = len(srcs)
    per = 2 * N_PEER_CHIPS + (1 if scatter else 0)
    hbm = pltpu.MemorySpace.HBM
    src_refs = [jax.new_ref(a, memory_space=hbm) for a in srcs]
    out_refs = [jax.empty_ref(jax.ShapeDtypeStruct((N_CHIP, 2) + a.shape[1:], a.dtype), memory_space=hbm) for a in srcs]

    @pl.kernel(mesh=plsc.ScalarSubcoreMesh(axis_name="seq", num_cores=1), name=name,
               scratch_types=(pltpu.SemaphoreType.DMA((per * na,)), pltpu.SemaphoreType.DMA((per * na,)),
                              pltpu.SemaphoreType.DMA((na,))),
               compiler_params=pltpu.CompilerParams(collective_id=collective_id))
    def launch(send_sems, recv_sems, local_sems):
        x, y, c, chips = _place()
        k = 2 * x + y
        sibling = (x, y, 1 - c)
        barrier = pltpu.get_barrier_semaphore()
        for cx, cy in chips:
            pl.semaphore_signal(barrier, inc=1, device_id=(cx, cy, c), device_id_type=MESH)
        pl.semaphore_signal(barrier, inc=1, device_id=sibling, device_id_type=MESH)
        pl.semaphore_wait(barrier, N_CHIP)

        def copy(i, kk, src, chip_k, half, to):
            return pltpu.make_async_remote_copy(
                src_ref=src, dst_ref=out_refs[i].at[chip_k, half], send_sem=send_sems.at[per * i + kk],
                recv_sem=recv_sems.at[per * i + kk], device_id=to, device_id_type=MESH)

        if scatter:
            mine = [pltpu.make_async_copy(src_refs[i].at[k], out_refs[i].at[k, c], local_sems.at[i]) for i in range(na)]
        else:
            mine = [pltpu.make_async_copy(src_refs[i], out_refs[i].at[k], local_sems.at[i]) for i in range(na)]
        for cp in mine:
            cp.start()
        first = []
        for j, (cx, cy) in enumerate(chips):
            for i in range(na):
                src = src_refs[i].at[2 * cx + cy] if scatter else src_refs[i].at[c]
                first.append(copy(i, j, src, k, c, (cx, cy, c)))
        if scatter:
            first += [copy(i, 2 * N_PEER_CHIPS, src_refs[i].at[k], k, c, sibling) for i in range(na)]
        for cp in first:
            cp.start()
        passed = []
        for j, (cx, cy) in enumerate(chips):
            kj = 2 * cx + cy
            for i in range(na):
                copy(i, j, out_refs[i].at[kj, c], kj, c, (cx, cy, c)).wait_recv()
                fwd = copy(i, N_PEER_CHIPS + j, out_refs[i].at[kj, c], kj, c, sibling)
                fwd.start()
                passed.append(fwd)
        if scatter:
            for i in range(na):
                copy(i, 2 * N_PEER_CHIPS, out_refs[i].at[k, 1 - c], k, 1 - c, sibling).wait_recv()
        for j, (cx, cy) in enumerate(chips):
            kj = 2 * cx + cy
            for i in range(na):
                copy(i, N_PEER_CHIPS + j, out_refs[i].at[kj, 1 - c], kj, 1 - c, sibling).wait_recv()
        for cp in first + passed:
            cp.wait_send()
        for cp in mine:
            cp.wait()

    launch()
    return [r[...] for r in out_refs]


def _pack(arrs, dtype, lanes, row_align):
    flat = jnp.concatenate([a.reshape(-1).astype(dtype) for a in arrs])
    unit = lanes * row_align
    total = -(-flat.shape[0] // unit) * unit
    return jnp.pad(flat, (0, total - flat.shape[0])).reshape(total // lanes, lanes)


def _unpack(flat, shapes):
    flat = flat.reshape(-1)
    out, off = [], 0
    for s in shapes:
        n = math.prod(s)
        out.append(flat[off:off + n].reshape(s))
        off += n
    return out


def _adam_tile(w, m, v, g):
    m2 = ADAM_B1 * m + (1.0 - ADAM_B1) * g
    v2 = ADAM_B2 * v + (1.0 - ADAM_B2) * (g * g)
    m_hat = m2 / (1.0 - ADAM_B1 ** ADAM_STEP)
    v_hat = v2 / (1.0 - ADAM_B2 ** ADAM_STEP)
    delta = -ADAM_LR * (m_hat / (jnp.sqrt(v_hat) + ADAM_EPS) + ADAM_WD * w)
    return delta, m2, v2


def adam_big(w, m, v, pieces, name):
    _, r, c = w.shape
    tr = _tile(r, 128, 16)

    def body(w_ref, m_ref, v_ref, p0, p1, p2, p3, g_ref, d_ref, mo_ref, vo_ref):
        g = ((p0[...].astype(F32) + p1[...].astype(F32)) + p2[...].astype(F32)) + p3[...].astype(F32)
        delta, m2, v2 = _adam_tile(w_ref[...], m_ref[...], v_ref[...], g)
        g_ref[...] = g
        d_ref[...] = delta
        mo_ref[...] = m2
        vo_ref[...] = v2

    wspec = pl.BlockSpec((None, tr, c), lambda i: (0, i, 0))
    pspecs = [pl.BlockSpec((None, tr, c), functools.partial(lambda i, kk: (kk, i, 0), kk=kk)) for kk in range(N_CHIP)]
    return pl.pallas_call(
        body, name=name, grid=(r // tr,), in_specs=[wspec] * 3 + pspecs, out_specs=[wspec] * 4,
        out_shape=[jax.ShapeDtypeStruct(w.shape, F32)] * 4, compiler_params=_params(("parallel",)))(
            w, m, v, pieces, pieces, pieces, pieces)


_WEIGHTS = (
    ("meta_tokens", "small", 1), ("ab_norm", "small", None), ("ab_w_in", "big", 2), ("s5_lambda_re", "small", None),
    ("s5_lambda_im", "small", None), ("s5_log_dt", "small", None), ("s5_b_re", "small", None), ("s5_b_im", "small", None),
    ("s5_c_re", "small", None), ("s5_c_im", "small", None), ("s5_d", "small", None), ("s5_glu_w", "big", 1),
    ("s5_glu_b", "small", None), ("ml_conv_w", "small", 2), ("ml_conv_b", "small", None), ("ml_wq", "small", 1),
    ("ml_wk", "small", 1), ("ml_wv", "small", 1), ("ml_w_gate", "small", 1), ("ml_b_gate", "small", None),
    ("ml_norm", "small", None), ("ml_skip", "small", None), ("ab_w_out", "big", 1), ("ssd_norm", "small", 1),
    ("ssd_w_in", "big", 2), ("ssd_conv_w", "small", 2), ("ssd_conv_b", "small", 1), ("ssd_dt_bias", "small", None),
    ("ssd_a_log", "small", None), ("ssd_d", "small", None), ("ssd_gnorm", "small", 1), ("ssd_w_out", "big", 1),
    ("final_norm", "small", None),
)


def _squeeze(a):
    return a[0] if a.ndim >= 3 else a


def kernel(x, meta_tokens, ab_norm, ab_w_in, s5_lambda_re, s5_lambda_im, s5_log_dt, s5_b_re, s5_b_im, s5_c_re, s5_c_im, s5_d, s5_glu_w, s5_glu_b, ml_conv_w, ml_conv_b, ml_wq, ml_wk, ml_wv, ml_w_gate, ml_b_gate, ml_norm, ml_skip, ab_w_out, ssd_norm, ssd_w_in, ssd_conv_w, ssd_conv_b, ssd_dt_bias, ssd_a_log, ssd_d, ssd_gnorm, ssd_w_out, final_norm, loss_target, m_meta_tokens, m_ab_norm, m_ab_w_in, m_s5_lambda_re, m_s5_lambda_im, m_s5_log_dt, m_s5_b_re, m_s5_b_im, m_s5_c_re, m_s5_c_im, m_s5_d, m_s5_glu_w, m_s5_glu_b, m_ml_conv_w, m_ml_conv_b, m_ml_wq, m_ml_wk, m_ml_wv, m_ml_w_gate, m_ml_b_gate, m_ml_norm, m_ml_skip, m_ab_w_out, m_ssd_norm, m_ssd_w_in, m_ssd_conv_w, m_ssd_conv_b, m_ssd_dt_bias, m_ssd_a_log, m_ssd_d, m_ssd_gnorm, m_ssd_w_out, m_final_norm, v_meta_tokens, v_ab_norm, v_ab_w_in, v_s5_lambda_re, v_s5_lambda_im, v_s5_log_dt, v_s5_b_re, v_s5_b_im, v_s5_c_re, v_s5_c_im, v_s5_d, v_s5_glu_w, v_s5_glu_b, v_ml_conv_w, v_ml_conv_b, v_ml_wq, v_ml_wk, v_ml_wv, v_ml_w_gate, v_ml_b_gate, v_ml_norm, v_ml_skip, v_ab_w_out, v_ssd_norm, v_ssd_w_in, v_ssd_conv_w, v_ssd_conv_b, v_ssd_dt_bias, v_ssd_a_log, v_ssd_d, v_ssd_gnorm, v_ssd_w_out, v_final_norm):
    args = (meta_tokens, ab_norm, ab_w_in, s5_lambda_re, s5_lambda_im, s5_log_dt, s5_b_re, s5_b_im, s5_c_re, s5_c_im, s5_d, s5_glu_w, s5_glu_b, ml_conv_w, ml_conv_b, ml_wq, ml_wk, ml_wv, ml_w_gate, ml_b_gate, ml_norm, ml_skip, ab_w_out, ssd_norm, ssd_w_in, ssd_conv_w, ssd_conv_b, ssd_dt_bias, ssd_a_log, ssd_d, ssd_gnorm, ssd_w_out, final_norm)
    m_args = (m_meta_tokens, m_ab_norm, m_ab_w_in, m_s5_lambda_re, m_s5_lambda_im, m_s5_log_dt, m_s5_b_re, m_s5_b_im, m_s5_c_re, m_s5_c_im, m_s5_d, m_s5_glu_w, m_s5_glu_b, m_ml_conv_w, m_ml_conv_b, m_ml_wq, m_ml_wk, m_ml_wv, m_ml_w_gate, m_ml_b_gate, m_ml_norm, m_ml_skip, m_ab_w_out, m_ssd_norm, m_ssd_w_in, m_ssd_conv_w, m_ssd_conv_b, m_ssd_dt_bias, m_ssd_a_log, m_ssd_d, m_ssd_gnorm, m_ssd_w_out, m_final_norm)
    v_args = (v_meta_tokens, v_ab_norm, v_ab_w_in, v_s5_lambda_re, v_s5_lambda_im, v_s5_log_dt, v_s5_b_re, v_s5_b_im, v_s5_c_re, v_s5_c_im, v_s5_d, v_s5_glu_w, v_s5_glu_b, v_ml_conv_w, v_ml_conv_b, v_ml_wq, v_ml_wk, v_ml_wv, v_ml_w_gate, v_ml_b_gate, v_ml_norm, v_ml_skip, v_ab_w_out, v_ssd_norm, v_ssd_w_in, v_ssd_conv_w, v_ssd_conv_b, v_ssd_dt_bias, v_ssd_a_log, v_ssd_d, v_ssd_gnorm, v_ssd_w_out, v_final_norm)
    names = [w[0] for w in _WEIGHTS]
    kind = {w[0]: w[1] for w in _WEIGHTS}
    axis = {w[0]: w[2] for w in _WEIGHTS}
    w_loc = dict(zip(names, args))
    m_loc = dict(zip(names, m_args))
    v_loc = dict(zip(names, v_args))
    chip = 2 * lax.axis_index("x") + lax.axis_index("y")
    core = lax.axis_index("c")
    big = [n for n in names if kind[n] == "big"]
    small = [n for n in names if kind[n] == "small"]
    small_sh = [n for n in small if axis[n] is not None]

    def halves(a):
        return a.astype(BF16).reshape(2, a.shape[1] // 2, a.shape[2])

    def assemble(n, gth):
        shard = gth.reshape((N_CHIP,) + w_loc[n].shape[1:])
        if axis[n] == 1:
            return shard.reshape(-1, shard.shape[2])
        return jnp.concatenate([shard[kk] for kk in range(N_CHIP)], axis=1)

    early = ["ab_w_in", "s5_glu_w"]
    late = ["ab_w_out", "ssd_w_in", "ssd_w_out"]
    small_sh_shapes = [w_loc[n].shape for n in small_sh]
    packed_s = _pack([w_loc[n] for n in small_sh], F32, LANES, 2 * SUBLANES)
    gathered = gather_chips([halves(w_loc[n]) for n in early] + [packed_s.reshape(2, -1, LANES)], "gather_early_w")
    after_early = (gathered[0][0, 0, 0, 0] * 0).astype(BF16)
    late_gathered = sequencer_exchange([halves(w_loc[n]) + after_early for n in late], False, 1, "gather_late_w")
    w_in0_shards = gathered[0].reshape((N_CHIP,) + w_loc["ab_w_in"].shape[1:])
    glu_full = assemble("s5_glu_w", gathered[1])

    def columns(shards, lo, hi):
        cw = shards.shape[2]
        parts = [shards[kk][:, max(lo - kk * cw, 0):min(hi - kk * cw, cw)]
                 for kk in range(N_CHIP) if lo < (kk + 1) * cw and hi > kk * cw]
        return parts[0] if len(parts) == 1 else jnp.concatenate(parts, axis=1)

    small_by_chip = gathered[2].reshape(N_CHIP, -1)
    sp = {}
    for n in small:
        if axis[n] is None:
            sp[n] = _squeeze(w_loc[n])
    per_chip = [_unpack(small_by_chip[kk], small_sh_shapes) for kk in range(N_CHIP)]
    for i, n in enumerate(small_sh):
        sp[n] = _squeeze(jnp.concatenate([per_chip[kk][i] for kk in range(N_CHIP)], axis=axis[n]))

    s5w = glu_full.shape[0]
    mlw = w_loc["ab_w_out"].shape[1] * N_CHIP - s5w
    inner = w_loc["ssd_w_out"].shape[1] * N_CHIP
    n_heads1 = sp["ssd_d"].shape[1]
    cdim = w_loc["ssd_w_in"].shape[2] * N_CHIP - inner - n_heads1
    bw = dict(W0a=columns(w_in0_shards, 0, 2 * s5w), W0xb=columns(w_in0_shards, 2 * s5w, 2 * s5w + mlw),
              W0zb=columns(w_in0_shards, 2 * s5w + mlw, 2 * (s5w + mlw)), glu=glu_full)

    def late_weights():
        fb = dict(zip(late, late_gathered))
        w_out0 = assemble("ab_w_out", fb["ab_w_out"])
        w1 = fb["ssd_w_in"].reshape((N_CHIP,) + w_loc["ssd_w_in"].shape[1:])
        return dict(Wo0a=w_out0[:s5w], Wo0b=w_out0[s5w:], W1z=columns(w1, 0, inner),
                    W1x=columns(w1, inner, inner + cdim), W1dt=_pad_lanes(columns(w1, inner + cdim, inner + cdim + n_heads1)),
                    Wo1=assemble("ssd_w_out", fb["ssd_w_out"]))

    def piece_columns(parts, lo, hi):
        out, off = [], 0
        for p in parts:
            a, b = max(lo - off, 0), min(hi - off, p.shape[1])
            if a < b:
                out.append(p[:, a:b])
            off += p.shape[1]
        return out[0] if len(out) == 1 else jnp.concatenate(out, axis=1)

    def chip_halves(n, parts):
        _, r, c_ = w_loc[n].shape
        if axis[n] == 1:
            whole = parts[0] if len(parts) == 1 else jnp.concatenate(parts, axis=0)
            return whole.reshape(N_CHIP, 2, r // 2, c_)
        shards = [piece_columns(parts, kk * c_, (kk + 1) * c_) for kk in range(N_CHIP)]
        return jnp.stack(shards).reshape(N_CHIP, 2, r // 2, c_)

    pieces = {}

    def reduce_group(ns, gfull, tag, collective_id):
        gps = [chip_halves(n, gfull[n]) for n in ns]
        from_sibling = swap_halves(gps, "swap_" + tag)
        partials = [add_halves(gp, oth, core, "add_" + n) for n, gp, oth in zip(ns, gps, from_sibling)]
        pieces.update(zip(ns, sequencer_exchange(partials, True, collective_id, "scatter_" + tag)))

    def late_grads(g):
        gfull = {"ab_w_out": [g["Wo0a"], g["Wo0b"]], "ssd_w_in": [g["W1z"], g["W1x"], g["W1dt"][:, :n_heads1]],
                 "ssd_w_out": [g["Wo1"]]}
        reduce_group(late, gfull, "late_g", 2)

    def early_grads(g):
        gfull = {"ab_w_in": [g["W0a"], g["W0xb"], g["W0zb"]], "s5_glu_w": [g["glu"]]}
        reduce_group(early, gfull, "early_g", 3)

    loss_local, dh0, gbig, gs = _local_step(x, loss_target, bw, sp, late_weights, late_grads, early_grads)
    grad_x = dh0[:, N_META:N_META + x.shape[1]]

    out_g, out_d, out_m, out_v = {}, {}, {}, {}
    small_full_shapes = [sp[n].shape for n in small] + [(1, 1)]
    packed_gs = _pack([gs[n] for n in small] + [loss_local.reshape(1, 1)], F32, LANES, SUBLANES)
    rows_s = packed_gs.shape[0]
    all_gs = sequencer_exchange([jnp.broadcast_to(packed_gs[None], (N_CHIP,) + packed_gs.shape)], True, 4,
                                "gather_small_g")[0].reshape(N_DEV, rows_s, LANES)
    blocks = [all_gs[i] for i in range(N_DEV)]

    for n in late + early:
        w, m, v = w_loc[n], m_loc[n], v_loc[n]
        pcs = pieces[n].reshape((N_CHIP,) + w.shape[1:])
        if w.shape[2] % LANES and w.shape[1] % LANES == 0:
            outs = adam_big(*(jnp.swapaxes(a, 1, 2) for a in (w, m, v, pcs)), "adam_" + n)
            outs = [jnp.swapaxes(o, 1, 2) for o in outs]
        else:
            outs = adam_big(w, m, v, pcs, "adam_" + n)
        out_g[n], out_d[n], out_m[n], out_v[n] = outs

    def sum8(i, *b):
        acc = b[0]
        for t in b[1:]:
            acc = acc + t
        return acc

    gsum = rowwise("sum_small_g", sum8, blocks, [], [(LANES, F32)], tr=_tile(rows_s, 512, 8))[0]
    summed = _unpack(gsum, small_full_shapes)
    loss = summed[-1].reshape(())
    g_small = dict(zip(small, summed[:-1]))
    g_loc = {}
    for n in small:
        g = g_small[n].reshape((1,) + g_small[n].shape) if w_loc[n].ndim >= 3 else g_small[n]
        if axis[n] is not None:
            size = w_loc[n].shape[axis[n]]
            g = lax.dynamic_slice_in_dim(g, chip * size, size, axis=axis[n])
        g_loc[n] = g.reshape(w_loc[n].shape)
    loc_shapes = [w_loc[n].shape for n in small]
    pw, pm, pv, pg = (_pack([d[n] for n in small], F32, LANES, SUBLANES) for d in (w_loc, m_loc, v_loc, g_loc))
    dl, mn, vn = rowwise("adam_small", lambda i, a, b, c_, d_: _adam_tile(a, b, c_, d_), [pw, pm, pv, pg], [],
                         [(LANES, F32)] * 3, tr=_tile(pw.shape[0], 512, 8))
    for d_out, flat in ((out_d, dl), (out_m, mn), (out_v, vn)):
        for n, a in zip(small, _unpack(flat, loc_shapes)):
            d_out[n] = a
    for n in small:
        out_g[n] = g_loc[n]

    return (loss, grad_x, *[out_g[n] for n in names], *[out_d[n] for n in names], *[out_m[n] for n in names],
            *[out_v[n] for n in names])
```

```python
import functools
import math

import jax
import jax.numpy as jnp
from jax import lax
from jax.experimental import pallas as pl
from jax.experimental.pallas import tpu as pltpu
from jax.experimental.pallas import tpu_sc as plsc

F32 = jnp.float32
BF16 = jnp.bfloat16
HI = lax.Precision.HIGHEST

D_MODEL = 2048
SEQ = 2048
N_META = 16
CHUNK = 128
NORM_EPS = 1e-6
HEAD_NORM_EPS = 1e-5
S5_GROUP_SIZE = 16
S5_STATE = 64
MLSTM_HEADS = 8
QKV_BLOCK = 4
SSD_HEAD_DIM = 64
SSD_STATE = 128
SSD_HPG = 8
ADAM_LR = 0.001
ADAM_B1 = 0.9
ADAM_B2 = 0.999
ADAM_EPS = 1e-08
ADAM_WD = 0.01
ADAM_STEP = 10

LANES = 128
SUBLANES = 8
VMEM_LIMIT = 56 * 1024 * 1024
MM_OPERAND_VMEM = 34 * 1024 * 1024


def _sigmoid(x):
    return 0.5 * jnp.tanh(0.5 * x) + 0.5


@jax.custom_vjp
def _silu(x):
    return x * _sigmoid(x)


def _silu_fwd(x):
    return x * _sigmoid(x), x


def _silu_bwd(x, ct):
    s = _sigmoid(x)
    return (ct * (s * (1.0 + x * (1.0 - s))),)


_silu.defvjp(_silu_fwd, _silu_bwd)


def _softplus(x):
    return jnp.maximum(x, 0.0) + jnp.log(1.0 + jnp.exp(-jnp.abs(x)))


def _log_sigmoid(x):
    return jnp.minimum(x, 0.0) - jnp.log(1.0 + jnp.exp(-jnp.abs(x)))


def _gelu(x):
    return 0.5 * x * (1.0 + jnp.tanh(math.sqrt(2.0 / math.pi) * (x + 0.044715 * (x * x * x))))


def _dot(a, b, dims, precision=None):
    return lax.dot_general(a, b, (dims, ((), ())), preferred_element_type=F32, precision=precision)


_NN, _NT, _TN = ((1,), (0,)), ((1,), (1,)), ((0,), (0,))


def _bf16_dot(dims, da_rule, db_rule):
    @jax.custom_vjp
    def f(a, b):
        return _dot(a.astype(BF16), b.astype(BF16), dims)

    def fwd(a, b):
        ab, bb = a.astype(BF16), b.astype(BF16)
        return _dot(ab, bb, dims), (ab, bb, jnp.zeros((), a.dtype), jnp.zeros((), b.dtype))

    def bwd(res, ct):
        ab, bb, a_like, b_like = res
        cb = ct.astype(BF16)
        return da_rule(ab, bb, cb).astype(a_like.dtype), db_rule(ab, bb, cb).astype(b_like.dtype)

    f.defvjp(fwd, bwd)
    return f


_dot_nn = _bf16_dot(_NN, lambda a, b, c: _dot(c, b, _NT), lambda a, b, c: _dot(a, c, _TN))
_dot_nt = _bf16_dot(_NT, lambda a, b, c: _dot(c, b, _NN), lambda a, b, c: _dot(c, a, _TN))
_dot_tn = _bf16_dot(_TN, lambda a, b, c: _dot(b, c, _NT), lambda a, b, c: _dot(a, c, _NN))


def _lane_pick(a, idx):
    sel = (lax.broadcasted_iota(jnp.int32, (1, a.shape[1]), 1) == idx).astype(a.dtype)
    return jnp.sum(a * sel, axis=1, keepdims=True)


def _row_pick(a, idx):
    sel = (lax.broadcasted_iota(jnp.int32, (a.shape[0], 1), 0) == idx).astype(a.dtype)
    return jnp.sum(a * sel, axis=0, keepdims=True)


def _tri(n, upper=False):
    r = lax.broadcasted_iota(jnp.int32, (n, n), 0)
    c = lax.broadcasted_iota(jnp.int32, (n, n), 1)
    return ((r <= c) if upper else (r >= c)).astype(F32)


def _tile(n, target, align):
    if n <= target:
        return n
    t = (target // align) * align
    while t >= align:
        if n % t == 0:
            return t
        t -= align
    return n


def _params(sem=None):
    return pltpu.CompilerParams(dimension_semantics=sem, vmem_limit_bytes=VMEM_LIMIT)


def mm(a, b, mode, name, resid=None, out_dtype=F32):
    if mode == "nn":
        (m, k), (k2, n) = a.shape, b.shape
    elif mode == "nt":
        (m, k), (n, k2) = a.shape, b.shape
    else:
        (k, m), (k2, n) = a.shape, b.shape
    assert k == k2, (a.shape, b.shape, mode)
    a_sz, b_sz = a.dtype.itemsize, b.dtype.itemsize
    if mode == "tn":
        tm, tn = _tile(m, 1024, LANES), _tile(n, 1024, LANES)
        tk = _tile(k, MM_OPERAND_VMEM // (2 * (tm * a_sz + tn * b_sz)), 16)
    else:
        tm, tn = _tile(m, 1088, 16), _tile(n, 512, LANES)
        tk = _tile(k, MM_OPERAND_VMEM // (2 * (tm * a_sz + tn * b_sz)), LANES)
    nk = k // tk
    dims = {"nn": ((1,), (0,)), "nt": ((1,), (1,)), "tn": ((0,), (0,))}[mode]
    has_resid = resid is not None

    def body(*refs):
        if has_resid:
            a_ref, b_ref, r_ref, o_ref = refs[:4]
        else:
            a_ref, b_ref, o_ref = refs[:3]
        part = _dot(a_ref[...].astype(BF16), b_ref[...].astype(BF16), dims)

        def finish(res):
            if has_resid:
                res = res + r_ref[...].astype(F32)
            o_ref[...] = res.astype(o_ref.dtype)

        if nk == 1:
            finish(part)
            return
        acc_ref = refs[-1]
        kk = pl.program_id(2)

        @pl.when(kk == 0)
        def _():
            acc_ref[...] = part

        @pl.when(jnp.logical_and(kk > 0, kk < nk - 1))
        def _():
            acc_ref[...] += part

        @pl.when(kk == nk - 1)
        def _():
            finish(acc_ref[...] + part)

    if mode == "tn":
        a_spec = pl.BlockSpec((tk, tm), lambda i, j, kk: (kk, i))
    else:
        a_spec = pl.BlockSpec((tm, tk), lambda i, j, kk: (i, kk))
    if mode == "nt":
        b_spec = pl.BlockSpec((tn, tk), lambda i, j, kk: (j, kk))
    else:
        b_spec = pl.BlockSpec((tk, tn), lambda i, j, kk: (kk, j))
    o_spec = pl.BlockSpec((tm, tn), lambda i, j, kk: (i, j))
    in_specs = [a_spec, b_spec] + ([o_spec] if has_resid else [])
    args = (a, b) + ((resid,) if has_resid else ())
    return pl.pallas_call(
        body, name=name, grid=(m // tm, n // tn, nk), in_specs=in_specs, out_specs=o_spec,
        out_shape=jax.ShapeDtypeStruct((m, n), out_dtype), scratch_shapes=[pltpu.VMEM((tm, tn), F32)] if nk > 1 else [],
        compiler_params=_params(("parallel", "parallel", "arbitrary")))(*args)


def rowwise(name, f, rows, params, outs, accs=(), tr=128):
    n_rows = rows[0].shape[0]
    assert n_rows % tr == 0
    n_r, n_p, n_o, n_a = len(rows), len(params), len(outs), len(accs)

    def body(*refs):
        i = pl.program_id(0)
        r_vals = [r[...] for r in refs[:n_r]]
        p_vals = [r[...] for r in refs[n_r:n_r + n_p]]
        o_refs = refs[n_r + n_p:n_r + n_p + n_o]
        a_refs = refs[n_r + n_p + n_o:]
        res = f(i, *r_vals, *p_vals)
        if not isinstance(res, (tuple, list)):
            res = (res,)
        assert len(res) == n_o + n_a, (name, len(res))
        for o_ref, val in zip(o_refs, res[:n_o]):
            o_ref[...] = val.astype(o_ref.dtype)
        if n_a:
            @pl.when(i == 0)
            def _():
                for a_ref in a_refs:
                    a_ref[...] = jnp.zeros_like(a_ref)

            for a_ref, val in zip(a_refs, res[n_o:]):
                a_ref[...] += val.astype(F32)

    in_specs = [pl.BlockSpec((tr, r.shape[1]), lambda i: (i, 0)) for r in rows]
    in_specs += [pl.BlockSpec(p.shape, lambda i: (0, 0)) for p in params]
    out_specs = [pl.BlockSpec((tr, w), lambda i: (i, 0)) for w, _ in outs]
    out_specs += [pl.BlockSpec(s, lambda i: (0, 0)) for s in accs]
    out_shape = [jax.ShapeDtypeStruct((n_rows, w), dt) for w, dt in outs]
    out_shape += [jax.ShapeDtypeStruct(s, F32) for s in accs]
    res = pl.pallas_call(
        body, name=name, grid=(n_rows // tr,), in_specs=in_specs, out_specs=out_specs, out_shape=out_shape,
        compiler_params=_params(("arbitrary",)))(*rows, *params)
    return res


def _rms(x, g, eps=NORM_EPS):
    return x * lax.rsqrt(jnp.mean(x * x, axis=-1, keepdims=True) + eps) * g


def norm_fwd(x, g, name):
    return rowwise(name, lambda i, xb, gb: _rms(xb, gb), [x], [g], [(x.shape[1], BF16)], tr=_tile(x.shape[0], 256, 16))[0]


def norm_bwd(x, g, dn, resid, name):
    def f(i, xb, dnb, rb, gb):
        _, vjp = jax.vjp(_rms, xb, gb)
        dx, dg = vjp(dnb)
        return dx + rb, dx + rb, dg

    return rowwise(name, f, [x, dn, resid], [g], [(x.shape[1], F32), (x.shape[1], BF16)], [g.shape],
                   tr=_tile(x.shape[0], 256, 16))


def conv_fwd(x, w, b, nb, name):
    rows, width = x.shape
    nc = rows // nb // CHUNK
    tw = _tile(width, 1024, LANES)
    ksz = w.shape[0]

    def body(x_ref, w_ref, b_ref, o_ref, ext_ref):
        c = pl.program_id(2)

        @pl.when(c == 0)
        def _():
            ext_ref[0:SUBLANES, :] = jnp.zeros((SUBLANES, tw), F32)

        taps = [w_ref[j:j + 1, :] for j in range(ksz)]
        bias = b_ref[...]
        row = lax.broadcasted_iota(jnp.int32, (SUBLANES, tw), 0)
        prev_rot = [pltpu.roll(ext_ref[0:SUBLANES, :], k, 0) for k in range(1, ksz)]
        for s in range(CHUNK // SUBLANES):
            r0 = s * SUBLANES
            cur = x_ref[r0:r0 + SUBLANES, :]
            cur_rot = [pltpu.roll(cur, k, 0) for k in range(1, ksz)]
            acc = bias + taps[ksz - 1] * cur
            for k in range(1, ksz):
                acc = acc + taps[ksz - 1 - k] * jnp.where(row >= k, cur_rot[k - 1], prev_rot[k - 1])
            o_ref[r0:r0 + SUBLANES, :] = acc
            prev_rot = cur_rot
        ext_ref[0:SUBLANES, :] = x_ref[CHUNK - SUBLANES:CHUNK, :]

    return pl.pallas_call(
        body, name=name, grid=(width // tw, nb, nc),
        in_specs=[pl.BlockSpec((CHUNK, tw), lambda j, bb, c: (bb * nc + c, j)),
                  pl.BlockSpec((ksz, tw), lambda j, bb, c: (0, j)),
                  pl.BlockSpec((1, tw), lambda j, bb, c: (0, j))],
        out_specs=pl.BlockSpec((CHUNK, tw), lambda j, bb, c: (bb * nc + c, j)),
        out_shape=jax.ShapeDtypeStruct((rows, width), F32),
        scratch_shapes=[pltpu.VMEM((2 * SUBLANES, tw), F32)],
        compiler_params=_params(("arbitrary", "arbitrary", "arbitrary")))(x, w, b)


def conv_bwd(dc, x, w, nb, name, resid=None, dx_dtype=BF16):
    rows, width = x.shape
    nc = rows // nb // CHUNK
    tw = _tile(width, 1024, LANES)
    ksz = w.shape[0]
    per = CHUNK // SUBLANES
    has_resid = resid is not None

    def body(*refs):
        if has_resid:
            dc_ref, x_ref, halo_ref, w_ref, r_ref, dx_ref, dw_ref, db_ref, extd_ref = refs
        else:
            dc_ref, x_ref, halo_ref, w_ref, dx_ref, dw_ref, db_ref, extd_ref = refs
        bb = pl.program_id(1)
        step = pl.program_id(2)
        c = nc - 1 - step

        @pl.when(jnp.logical_and(bb == 0, step == 0))
        def _():
            dw_ref[...] = jnp.zeros_like(dw_ref)
            db_ref[...] = jnp.zeros_like(db_ref)

        @pl.when(step == 0)
        def _():
            extd_ref[SUBLANES:2 * SUBLANES, :] = jnp.zeros((SUBLANES, tw), F32)

        nstrip = CHUNK // SUBLANES
        taps = [w_ref[j:j + 1, :] for j in range(ksz)]
        row = lax.broadcasted_iota(jnp.int32, (SUBLANES, tw), 0)
        x_prev_rot = [pltpu.roll(jnp.where(c == 0, 0.0, halo_ref[...]), k, 0) for k in range(1, ksz)]
        dcs = dc_ref[0:SUBLANES, :]
        dc_rot = [pltpu.roll(dcs, SUBLANES - k, 0) for k in range(1, ksz)]
        for s in range(nstrip):
            r0 = s * SUBLANES
            nxt = extd_ref[SUBLANES:2 * SUBLANES, :] if s == nstrip - 1 else dc_ref[r0 + SUBLANES:r0 + 2 * SUBLANES, :]
            nxt_rot = [pltpu.roll(nxt, SUBLANES - k, 0) for k in range(1, ksz)]
            xc = x_ref[r0:r0 + SUBLANES, :]
            x_rot = [pltpu.roll(xc, k, 0) for k in range(1, ksz)]
            dx = r_ref[r0:r0 + SUBLANES, :].astype(F32) if has_resid else jnp.zeros((SUBLANES, tw), F32)
            dx = dx + taps[ksz - 1] * dcs
            dw_ref[(ksz - 1) * SUBLANES:ksz * SUBLANES, :] += dcs * xc
            for k in range(1, ksz):
                j = ksz - 1 - k
                dx = dx + taps[j] * jnp.where(row < SUBLANES - k, dc_rot[k - 1], nxt_rot[k - 1])
                dw_ref[j * SUBLANES:(j + 1) * SUBLANES, :] += dcs * jnp.where(row >= k, x_rot[k - 1], x_prev_rot[k - 1])
            if s % 2 == 0:
                held = dx
            else:
                dx_ref[r0 - SUBLANES:r0 + SUBLANES, :] = jnp.concatenate([held, dx], axis=0).astype(dx_ref.dtype)
            db_ref[...] += dcs
            dcs, dc_rot, x_prev_rot = nxt, nxt_rot, x_rot
        extd_ref[SUBLANES:2 * SUBLANES, :] = dc_ref[0:SUBLANES, :]

    def blk(j, bb, step):
        return (bb * nc + nc - 1 - step, j)

    def halo(j, bb, step):
        return (jnp.maximum((bb * nc + nc - 1 - step) * per - 1, 0), j)

    in_specs = [pl.BlockSpec((CHUNK, tw), blk), pl.BlockSpec((CHUNK, tw), blk), pl.BlockSpec((SUBLANES, tw), halo),
                pl.BlockSpec((ksz, tw), lambda j, bb, step: (0, j))]
    args = [dc, x, x, w]
    if has_resid:
        in_specs.append(pl.BlockSpec((CHUNK, tw), blk))
        args.append(resid)
    dx, dw_raw, db_raw = pl.pallas_call(
        body, name=name, grid=(width // tw, nb, nc), in_specs=in_specs,
        out_specs=[pl.BlockSpec((CHUNK, tw), blk), pl.BlockSpec((ksz * SUBLANES, tw), lambda j, bb, step: (0, j)),
                   pl.BlockSpec((SUBLANES, tw), lambda j, bb, step: (0, j))],
        out_shape=[jax.ShapeDtypeStruct((rows, width), dx_dtype), jax.ShapeDtypeStruct((ksz * SUBLANES, width), F32),
                   jax.ShapeDtypeStruct((SUBLANES, width), F32)],
        scratch_shapes=[pltpu.VMEM((2 * SUBLANES, tw), F32)],
        compiler_params=_params(("arbitrary", "arbitrary", "arbitrary")))(*args)
    return dx, dw_raw.reshape(ksz, SUBLANES, width).sum(axis=1), db_raw.sum(axis=0, keepdims=True)


S5_Q = 4


def _s5_fill_bu(u, bre_ref, bim_ref, xr_ref, xi_ref, ns):
    for s in range(ns):
        ub = u[:, s * LANES:(s + 1) * LANES].astype(BF16)
        bur = _dot(ub, bre_ref[s], ((1,), (0,)))
        bui = _dot(ub, bim_ref[s], ((1,), (0,)))
        for q in range(S5_Q):
            xr_ref[q, pl.ds(s, CHUNK, stride=ns), :] = bur[:, q * LANES:(q + 1) * LANES]
            xi_ref[q, pl.ds(s, CHUNK, stride=ns), :] = bui[:, q * LANES:(q + 1) * LANES]


def _s5_scan(xr_ref, xi_ref, ar_ref, ai_ref, st_ref, ns):
    ar = [ar_ref[q] for q in range(S5_Q)]
    ai = [ai_ref[q] for q in range(S5_Q)]

    def step(t, carry):
        rows = pl.ds(pl.multiple_of(t * ns, ns), ns)
        out = []
        for q in range(S5_Q):
            pr, pi_ = carry[2 * q], carry[2 * q + 1]
            nr = ar[q] * pr - ai[q] * pi_ + xr_ref[q, rows, :]
            ni = ar[q] * pi_ + ai[q] * pr + xi_ref[q, rows, :]
            xr_ref[q, rows, :] = nr
            xi_ref[q, rows, :] = ni
            out += [nr, ni]
        return tuple(out)

    init = []
    for q in range(S5_Q):
        init += [st_ref[0, q], st_ref[1, q]]
    fin = lax.fori_loop(0, CHUNK, step, tuple(init), unroll=2)
    for q in range(S5_Q):
        st_ref[0, q] = fin[2 * q]
        st_ref[1, q] = fin[2 * q + 1]


def s5_fwd(pa, bre, bim, cre, cim, ar, ai, dvec, nb, name):
    rows = pa.shape[0]
    width = pa.shape[1] // 2
    ns = width // LANES
    nc = rows // nb // CHUNK

    def body(u_ref, bre_ref, bim_ref, cre_ref, cim_ref, ar_ref, ai_ref, d_ref, y_ref, g_ref, so_ref, xr_ref, xi_ref, st_ref):
        c = pl.program_id(1)

        @pl.when(c == 0)
        def _():
            st_ref[...] = jnp.zeros_like(st_ref)

        so_ref[...] = st_ref[...]
        u = u_ref[...]
        _s5_fill_bu(u, bre_ref, bim_ref, xr_ref, xi_ref, ns)
        _s5_scan(xr_ref, xi_ref, ar_ref, ai_ref, st_ref, ns)
        for s in range(ns):
            acc = jnp.zeros((CHUNK, LANES), F32)
            for q in range(S5_Q):
                xr = xr_ref[q, pl.ds(s, CHUNK, stride=ns), :].astype(BF16)
                xi = xi_ref[q, pl.ds(s, CHUNK, stride=ns), :].astype(BF16)
                acc = acc + _dot(xr, cre_ref[s, q * LANES:(q + 1) * LANES, :], ((1,), (0,)))
                acc = acc - _dot(xi, cim_ref[s, q * LANES:(q + 1) * LANES, :], ((1,), (0,)))
            cols = slice(s * LANES, (s + 1) * LANES)
            y = acc + d_ref[:, cols] * u[:, cols]
            y_ref[:, cols] = y
            g_ref[:, cols] = _gelu(y).astype(BF16)

    whole3 = lambda a: pl.BlockSpec(a.shape, lambda b_, c: (0, 0, 0))
    return pl.pallas_call(
        body, name=name, grid=(nb, nc),
        in_specs=[pl.BlockSpec((CHUNK, width), lambda b_, c: (b_ * nc + c, 0)), whole3(bre), whole3(bim), whole3(cre),
                  whole3(cim), whole3(ar), whole3(ai), pl.BlockSpec((1, width), lambda b_, c: (0, 0))],
        out_specs=[pl.BlockSpec((CHUNK, width), lambda b_, c: (b_ * nc + c, 0)),
                   pl.BlockSpec((CHUNK, width), lambda b_, c: (b_ * nc + c, 0)),
                   pl.BlockSpec((None, 2, S5_Q, ns, LANES), lambda b_, c: (b_ * nc + c, 0, 0, 0, 0))],
        out_shape=[jax.ShapeDtypeStruct((rows, width), F32), jax.ShapeDtypeStruct((rows, width), BF16),
                   jax.ShapeDtypeStruct((nb * nc, 2, S5_Q, ns, LANES), F32)],
        scratch_shapes=[pltpu.VMEM((S5_Q, CHUNK * ns, LANES), F32), pltpu.VMEM((S5_Q, CHUNK * ns, LANES), F32),
                        pltpu.VMEM((2, S5_Q, ns, LANES), F32)],
        compiler_params=_params(("arbitrary", "arbitrary")))(pa, bre, bim, cre, cim, ar, ai, dvec)


def s5_bwd(pa, dys, states, bre, bim, cre, cim, ar, ai, dvec, nb, name):
    rows = pa.shape[0]
    width = pa.shape[1] // 2
    ns = width // LANES
    nc = rows // nb // CHUNK

    def body(u_ref, dy_ref, sin_ref, bre_ref, bim_ref, cre_ref, cim_ref, ar_ref, ai_ref, d_ref,
             du_ref, dbre_ref, dbim_ref, dcre_ref, dcim_ref, dar_ref, dai_ref, dd_ref,
             xr_ref, xi_ref, lr_ref, li_ref, st_ref, lam_ref):
        bb = pl.program_id(0)
        step_i = pl.program_id(1)

        @pl.when(jnp.logical_and(bb == 0, step_i == 0))
        def _():
            for r in (dbre_ref, dbim_ref, dcre_ref, dcim_ref, dar_ref, dai_ref, dd_ref):
                r[...] = jnp.zeros_like(r)

        @pl.when(step_i == 0)
        def _():
            lam_ref[...] = jnp.zeros_like(lam_ref)

        u = u_ref[...]
        dy = dy_ref[...]
        st_ref[...] = sin_ref[...]
        _s5_fill_bu(u, bre_ref, bim_ref, xr_ref, xi_ref, ns)
        _s5_scan(xr_ref, xi_ref, ar_ref, ai_ref, st_ref, ns)
        dd_ref[...] += jnp.sum(dy * u, axis=0, keepdims=True)
        for s in range(ns):
            dyb = dy[:, s * LANES:(s + 1) * LANES].astype(BF16)
            gr = _dot(dyb, cre_ref[s], ((1,), (1,)))
            gi = -_dot(dyb, cim_ref[s], ((1,), (1,)))
            for q in range(S5_Q):
                lr_ref[q, pl.ds(s, CHUNK, stride=ns), :] = gr[:, q * LANES:(q + 1) * LANES]
                li_ref[q, pl.ds(s, CHUNK, stride=ns), :] = gi[:, q * LANES:(q + 1) * LANES]
                xr = xr_ref[q, pl.ds(s, CHUNK, stride=ns), :].astype(BF16)
                xi = xi_ref[q, pl.ds(s, CHUNK, stride=ns), :].astype(BF16)
                dcre_ref[s, q * LANES:(q + 1) * LANES, :] += _dot(xr, dyb, ((0,), (0,)))
                dcim_ref[s, q * LANES:(q + 1) * LANES, :] -= _dot(xi, dyb, ((0,), (0,)))
        ar = [ar_ref[q] for q in range(S5_Q)]
        ai = [ai_ref[q] for q in range(S5_Q)]

        def one(t_rows, p_r, p_i, carry):
            out = []
            for q in range(S5_Q):
                l_r, l_i, da_r, da_i = carry[4 * q:4 * q + 4]
                n_r = lr_ref[q, t_rows, :] + ar[q] * l_r + ai[q] * l_i
                n_i = li_ref[q, t_rows, :] + ar[q] * l_i - ai[q] * l_r
                lr_ref[q, t_rows, :] = n_r
                li_ref[q, t_rows, :] = n_i
                xpr, xpi = p_r(q), p_i(q)
                out += [n_r, n_i, da_r + n_r * xpr + n_i * xpi, da_i + n_i * xpr - n_r * xpi]
            return tuple(out)

        def step(k, carry):
            t = CHUNK - 1 - k
            t_rows = pl.ds(pl.multiple_of(t * ns, ns), ns)
            p_rows = pl.ds(pl.multiple_of((t - 1) * ns, ns), ns)
            return one(t_rows, lambda q: xr_ref[q, p_rows, :], lambda q: xi_ref[q, p_rows, :], carry)

        init = []
        zero = jnp.zeros((ns, LANES), F32)
        for q in range(S5_Q):
            init += [lam_ref[0, q], lam_ref[1, q], zero, zero]
        carry = lax.fori_loop(0, CHUNK - 1, step, tuple(init), unroll=2)
        carry = one(pl.ds(0, ns), lambda q: sin_ref[0, q], lambda q: sin_ref[1, q], carry)
        for q in range(S5_Q):
            lam_ref[0, q] = carry[4 * q]
            lam_ref[1, q] = carry[4 * q + 1]
            dar_ref[q] += carry[4 * q + 2]
            dai_ref[q] += carry[4 * q + 3]
        for s in range(ns):
            cols = slice(s * LANES, (s + 1) * LANES)
            ub = u[:, cols].astype(BF16)
            acc = d_ref[:, cols] * dy[:, cols]
            for q in range(S5_Q):
                qs = slice(q * LANES, (q + 1) * LANES)
                lr = lr_ref[q, pl.ds(s, CHUNK, stride=ns), :].astype(BF16)
                li = li_ref[q, pl.ds(s, CHUNK, stride=ns), :].astype(BF16)
                dbre_ref[s, :, qs] += _dot(ub, lr, ((0,), (0,)))
                dbim_ref[s, :, qs] += _dot(ub, li, ((0,), (0,)))
                acc = acc + _dot(lr, bre_ref[s, :, qs], ((1,), (1,))) + _dot(li, bim_ref[s, :, qs], ((1,), (1,)))
            du_ref[:, cols] = acc.astype(du_ref.dtype)

    whole3 = lambda a: pl.BlockSpec(a.shape, lambda b_, c: (0, 0, 0))
    rowblk = pl.BlockSpec((CHUNK, width), lambda b_, c: (b_ * nc + nc - 1 - c, 0))
    scr = pltpu.VMEM((S5_Q, CHUNK * ns, LANES), F32)
    return pl.pallas_call(
        body, name=name, grid=(nb, nc),
        in_specs=[rowblk, rowblk,
                  pl.BlockSpec((None, 2, S5_Q, ns, LANES), lambda b_, c: (b_ * nc + nc - 1 - c, 0, 0, 0, 0)),
                  whole3(bre), whole3(bim), whole3(cre), whole3(cim), whole3(ar), whole3(ai),
                  pl.BlockSpec((1, width), lambda b_, c: (0, 0))],
        out_specs=[rowblk, whole3(bre), whole3(bim), whole3(cre), whole3(cim), whole3(ar), whole3(ai),
                   pl.BlockSpec((1, width), lambda b_, c: (0, 0))],
        out_shape=[jax.ShapeDtypeStruct((rows, width), BF16), jax.ShapeDtypeStruct(bre.shape, F32),
                   jax.ShapeDtypeStruct(bim.shape, F32), jax.ShapeDtypeStruct(cre.shape, F32),
                   jax.ShapeDtypeStruct(cim.shape, F32), jax.ShapeDtypeStruct(ar.shape, F32),
                   jax.ShapeDtypeStruct(ai.shape, F32), jax.ShapeDtypeStruct((1, width), F32)],
        scratch_shapes=[scr, scr, scr, scr, pltpu.VMEM((2, S5_Q, ns, LANES), F32), pltpu.VMEM((2, S5_Q, ns, LANES), F32)],
        compiler_params=_params(("arbitrary", "arbitrary")))(pa, dys, states, bre, bim, cre, cim, ar, ai, dvec)


def _s5_discretize(lam_re, lam_im, log_dt, b_re, b_im):
    dt = jnp.exp(log_dt)[:, None]
    mag = jnp.exp(lam_re * dt)
    ar, ai = mag * jnp.cos(lam_im * dt), mag * jnp.sin(lam_im * dt)
    den = lam_re * lam_re + lam_im * lam_im
    qr = ((ar - 1.0) * lam_re + ai * lam_im) / den
    qi = (ai * lam_re - (ar - 1.0) * lam_im) / den
    bbr = qr[..., None] * b_re - qi[..., None] * b_im
    bbi = qr[..., None] * b_im + qi[..., None] * b_re
    return ar, ai, bbr, bbi


def _s5_expand(ar, ai, bbr, bbi, c_re, c_im):
    g, p, h = bbr.shape
    gps = LANES // h
    ns = g // gps
    eye = jnp.eye(gps, dtype=F32)

    def bexp(b):
        return jnp.einsum("sgph,gk->sghkp", b.reshape(ns, gps, p, h), eye).reshape(ns, gps * h, gps * p)

    def cexp(c):
        return jnp.einsum("sghp,gk->sgpkh", c.reshape(ns, gps, h, p), eye).reshape(ns, gps * p, gps * h)

    def aexp(a):
        return a.reshape(ns, S5_Q, LANES).transpose(1, 0, 2)

    return (bexp(bbr).astype(BF16), bexp(bbi).astype(BF16), cexp(c_re).astype(BF16), cexp(c_im).astype(BF16),
            aexp(ar), aexp(ai))


def _s5_contract(dbre, dbim, dcre, dcim, dar, dai, g, p, h):
    gps = LANES // h
    ns = g // gps
    eye = jnp.eye(gps, dtype=F32)
    bcon = lambda d: jnp.einsum("sghkp,gk->sgph", d.reshape(ns, gps, h, gps, p), eye).reshape(g, p, h)
    ccon = lambda d: jnp.einsum("sgpkh,gk->sghp", d.reshape(ns, gps, p, gps, h), eye).reshape(g, h, p)
    acon = lambda d: d.transpose(1, 0, 2).reshape(g, p)
    return bcon(dbre), bcon(dbim), ccon(dcre), ccon(dcim), acon(dar), acon(dai)


PROJ_BLOCK = 256


def _ml_proj_tile(cpre, xb, wq, wk, wv, gq, gk, gv):
    xc = _silu(cpre)
    q = _dot_nn(xc, wq)
    k = _dot_nn(xc, wk)
    v = _dot_nn(xb, wv)
    return q, k, v, _dot_nn(q, gq) + _dot_nn(k, gk) + _dot_nn(v, gv)


def ml_proj_fwd(cpre, xb, wq, wk, wv, gq, gk, gv, name):
    rows, width = cpre.shape
    pb = wq.shape[1]
    nblk = width // pb
    tr = _tile(rows, 1088, 16)

    def body(c_ref, x_ref, wq_ref, wk_ref, wv_ref, gq_ref, gk_ref, gv_ref, q_ref, k_ref, v_ref, g_ref):
        j = pl.program_id(1)
        q, k, v, g = _ml_proj_tile(c_ref[...], x_ref[...], wq_ref[...], wk_ref[...], wv_ref[...],
                                   gq_ref[...], gk_ref[...], gv_ref[...])
        q_ref[...] = q
        k_ref[...] = k
        v_ref[...] = v

        @pl.when(j == 0)
        def _():
            g_ref[...] = jnp.zeros_like(g_ref)

        g_ref[...] += g

    rb = pl.BlockSpec((tr, pb), lambda i, j: (i, j))
    wb = pl.BlockSpec((None, pb, pb), lambda i, j: (j, 0, 0))
    gwb = pl.BlockSpec((None, pb, LANES), lambda i, j: (j, 0, 0))
    return pl.pallas_call(
        body, name=name, grid=(rows // tr, nblk), in_specs=[rb, rb, wb, wb, wb, gwb, gwb, gwb],
        out_specs=[rb, rb, rb, pl.BlockSpec((tr, LANES), lambda i, j: (i, 0))],
        out_shape=[jax.ShapeDtypeStruct((rows, width), F32)] * 3 + [jax.ShapeDtypeStruct((rows, LANES), F32)],
        compiler_params=_params(("arbitrary", "arbitrary")))(cpre, xb, wq, wk, wv, gq, gk, gv)


def ml_proj_bwd(cpre, xb, wq, wk, wv, gq, gk, gv, dq, dk, dv, dg, dcp_extra, name):
    rows, width = cpre.shape
    pb = wq.shape[1]
    nblk = width // pb
    tr = _tile(rows, 1088, 16)

    def body(c_ref, x_ref, wq_ref, wk_ref, wv_ref, gq_ref, gk_ref, gv_ref, dq_ref, dk_ref, dv_ref, dg_ref, e_ref,
             dc_ref, dx_ref, *dw_refs):
        i = pl.program_id(1)
        _, vjp = jax.vjp(_ml_proj_tile, c_ref[...], x_ref[...], wq_ref[...], wk_ref[...], wv_ref[...],
                         gq_ref[...], gk_ref[...], gv_ref[...])
        grads = vjp((dq_ref[...], dk_ref[...], dv_ref[...], dg_ref[...]))
        dc_ref[...] = grads[0] + e_ref[...]
        dx_ref[...] = grads[1]

        @pl.when(i == 0)
        def _():
            for r in dw_refs:
                r[...] = jnp.zeros_like(r)

        for r, gval in zip(dw_refs, grads[2:]):
            r[...] += gval

    rb = pl.BlockSpec((tr, pb), lambda j, i: (i, j))
    wb = pl.BlockSpec((None, pb, pb), lambda j, i: (j, 0, 0))
    gwb = pl.BlockSpec((None, pb, LANES), lambda j, i: (j, 0, 0))
    gb = pl.BlockSpec((tr, LANES), lambda j, i: (i, 0))
    wshape = jax.ShapeDtypeStruct((nblk, pb, pb), F32)
    gshape = jax.ShapeDtypeStruct((nblk, pb, LANES), F32)
    return pl.pallas_call(
        body, name=name, grid=(nblk, rows // tr), in_specs=[rb, rb, wb, wb, wb, gwb, gwb, gwb, rb, rb, rb, gb, rb],
        out_specs=[rb, rb] + [wb] * 3 + [gwb] * 3,
        out_shape=[jax.ShapeDtypeStruct((rows, width), F32)] * 2 + [wshape] * 3 + [gshape] * 3,
        compiler_params=_params(("arbitrary", "arbitrary")))(cpre, xb, wq, wk, wv, gq, gk, gv, dq, dk, dv, dg, dcp_extra)


def _ml_gates_tile(gl, bg, nh):
    x = gl + bg
    bcum = _dot(_tri(CHUNK), _log_sigmoid(x), ((1,), (0,)), precision=HI)
    lane = lax.broadcasted_iota(jnp.int32, x.shape, 1)
    return jnp.where(lane < nh, x, jnp.where(lane < 2 * nh, bcum, 0.0))


def _ml_core_tile(q, k, v, colg, rowg, cpre, zb, nw, sk, cst, nst, m_prev):
    c, dh = q.shape
    igc, bc = _lane_pick(colg, 0), _lane_pick(colg, 1)
    igr, br = _row_pick(rowg, 0), _row_pick(rowg, 1)
    causal = _tri(c) > 0
    dmat = jnp.where(causal, bc - br + igr, -jnp.inf)
    inter = bc + m_prev
    mt = lax.stop_gradient(jnp.maximum(inter, jnp.max(dmat, axis=1, keepdims=True)))
    wt = jnp.exp(dmat - mt)
    w_prev = jnp.exp(inter - mt)
    qs = q * (dh ** -0.5)
    s = _dot_nt(qs, k) * wt
    num = _dot_nn(s, v) + w_prev * _dot_nn(qs, cst)
    den = jnp.sum(s, axis=1, keepdims=True) + w_prev * jnp.sum(qs * nst, axis=1, keepdims=True)
    h = num * (1.0 / jnp.maximum(jnp.abs(den), jnp.exp(-mt)))
    last = (lax.broadcasted_iota(jnp.int32, (c, 1), 0) == c - 1).astype(F32)
    blast = jnp.sum(bc * last, axis=0, keepdims=True)
    g = blast - bc + igc
    m_new = lax.stop_gradient(jnp.maximum(blast + m_prev, jnp.max(g, axis=0, keepdims=True)))
    decay = jnp.exp(blast + m_prev - m_new)
    wk = jnp.exp(g - m_new) * k
    c_new = decay * cst + _dot_tn(wk, v)
    n_new = decay * nst + jnp.sum(wk, axis=0, keepdims=True)
    mu = jnp.mean(h, axis=1, keepdims=True)
    hc = h - mu
    var = jnp.mean(hc * hc, axis=1, keepdims=True)
    out = hc * lax.rsqrt(var + HEAD_NORM_EPS) * nw + sk * _silu(cpre)
    return out * _silu(zb), c_new, n_new, m_new


def _ml_core_specs(nc, dh, rev):
    ch = (lambda c: nc - 1 - c) if rev else (lambda c: c)
    rb = pl.BlockSpec((CHUNK, dh), lambda b_, c, h: (b_ * nc + ch(c), h))
    colb = pl.BlockSpec((None, CHUNK, 2), lambda b_, c, h: (h, b_ * nc + ch(c), 0))
    rowb = pl.BlockSpec((None, None, 2, CHUNK), lambda b_, c, h: (b_ * nc + ch(c), h, 0, 0))
    pb = pl.BlockSpec((1, dh), lambda b_, c, h: (0, h))
    cb = pl.BlockSpec((None, None, dh, dh), lambda b_, c, h: (b_ * nc + ch(c), h, 0, 0))
    nb_ = pl.BlockSpec((None, None, 1, dh), lambda b_, c, h: (b_ * nc + ch(c), h, 0, 0))
    mb = pl.BlockSpec((None, None, 1, 1), lambda b_, c, h: (b_ * nc + ch(c), h, 0, 0))
    return rb, colb, rowb, pb, cb, nb_, mb


def ml_core_fwd(q, k, v, colg, rowg, cpre, zb, nw, sk, nb, nh, name):
    rows, width = q.shape
    dh = width // nh
    nc = rows // nb // CHUNK
    rb, colb, rowb, pb, cb, nb_, mb = _ml_core_specs(nc, dh, False)

    def body(q_ref, k_ref, v_ref, col_ref, row_ref, c_ref, z_ref, nw_ref, sk_ref, y_ref, cs_ref, ns_ref, ms_ref,
             cst_ref, nst_ref, mst_ref):
        c = pl.program_id(1)
        h = pl.program_id(2)

        @pl.when(c == 0)
        def _():
            cst_ref[h] = jnp.zeros((dh, dh), F32)
            nst_ref[h] = jnp.zeros((1, dh), F32)
            mst_ref[h] = jnp.zeros((1, 1), F32)

        cst, nst, m_prev = cst_ref[h], nst_ref[h], mst_ref[h]
        cs_ref[...] = cst
        ns_ref[...] = nst
        ms_ref[...] = m_prev
        y, c_new, n_new, m_new = _ml_core_tile(q_ref[...], k_ref[...], v_ref[...], col_ref[...], row_ref[...],
                                               c_ref[...], z_ref[...], nw_ref[...], sk_ref[...], cst, nst, m_prev)
        y_ref[...] = y.astype(BF16)
        cst_ref[h] = c_new
        nst_ref[h] = n_new
        mst_ref[h] = m_new

    nbc = nb * nc
    return pl.pallas_call(
        body, name=name, grid=(nb, nc, nh), in_specs=[rb, rb, rb, colb, rowb, rb, rb, pb, pb],
        out_specs=[rb, cb, nb_, mb],
        out_shape=[jax.ShapeDtypeStruct((rows, width), BF16), jax.ShapeDtypeStruct((nbc, nh, dh, dh), F32),
                   jax.ShapeDtypeStruct((nbc, nh, 1, dh), F32), jax.ShapeDtypeStruct((nbc, nh, 1, 1), F32)],
        scratch_shapes=[pltpu.VMEM((nh, dh, dh), F32), pltpu.VMEM((nh, 1, dh), F32), pltpu.VMEM((nh, 1, 1), F32)],
        compiler_params=_params(("arbitrary", "arbitrary", "arbitrary")))(q, k, v, colg, rowg, cpre, zb, nw, sk)


def ml_core_bwd(q, k, v, colg, rowg, cpre, zb, nw, sk, cs, ns, ms, dy, nb, nh, name):
    rows, width = q.shape
    dh = width // nh
    nc = rows // nb // CHUNK
    rb, colb, rowb, pb, cb, nb_, mb = _ml_core_specs(nc, dh, True)

    def body(q_ref, k_ref, v_ref, col_ref, row_ref, c_ref, z_ref, nw_ref, sk_ref, cs_ref, ns_ref, ms_ref, dy_ref,
             dq_ref, dk_ref, dv_ref, dc_ref, dz_ref, dcol_ref, drow_ref, dnw_ref, dsk_ref, dcst_ref, dnst_ref):
        bb = pl.program_id(0)
        step = pl.program_id(1)
        h = pl.program_id(2)

        @pl.when(jnp.logical_and(bb == 0, jnp.logical_and(step == 0, h == 0)))
        def _():
            dnw_ref[...] = jnp.zeros_like(dnw_ref)
            dsk_ref[...] = jnp.zeros_like(dsk_ref)

        @pl.when(step == 0)
        def _():
            dcst_ref[h] = jnp.zeros((dh, dh), F32)
            dnst_ref[h] = jnp.zeros((1, dh), F32)

        m_prev = ms_ref[...]

        def f(*a):
            return _ml_core_tile(*a, m_prev)[:3]

        _, vjp = jax.vjp(f, q_ref[...], k_ref[...], v_ref[...], col_ref[...], row_ref[...], c_ref[...], z_ref[...],
                         nw_ref[...], sk_ref[...], cs_ref[...], ns_ref[...])
        g = vjp((dy_ref[...], dcst_ref[h], dnst_ref[h]))
        dq_ref[...] = g[0]
        dk_ref[...] = g[1]
        dv_ref[...] = g[2]
        dcol_ref[...] = g[3]
        drow_ref[...] = g[4]
        dc_ref[...] = g[5]
        dz_ref[...] = g[6].astype(dz_ref.dtype)
        dnw_ref[h] += g[7]
        dsk_ref[h] += g[8]
        dcst_ref[h] = g[9]
        dnst_ref[h] = g[10]

    nbc = nb * nc
    accb = pl.BlockSpec((nh, 1, dh), lambda b_, c, h: (0, 0, 0))
    return pl.pallas_call(
        body, name=name, grid=(nb, nc, nh), in_specs=[rb, rb, rb, colb, rowb, rb, rb, pb, pb, cb, nb_, mb, rb],
        out_specs=[rb, rb, rb, rb, rb, colb, rowb, accb, accb],
        out_shape=[jax.ShapeDtypeStruct((rows, width), F32)] * 4 + [jax.ShapeDtypeStruct((rows, width), BF16)]
        + [jax.ShapeDtypeStruct(colg.shape, F32), jax.ShapeDtypeStruct(rowg.shape, F32),
           jax.ShapeDtypeStruct((nh, 1, dh), F32), jax.ShapeDtypeStruct((nh, 1, dh), F32)],
        scratch_shapes=[pltpu.VMEM((nh, dh, dh), F32), pltpu.VMEM((nh, 1, dh), F32)],
        compiler_params=_params(("arbitrary", "arbitrary", "arbitrary")))(
            q, k, v, colg, rowg, cpre, zb, nw, sk, cs, ns, ms, dy)


def _ssd_dt_tile(dtr, bias, alog):
    dt = _softplus(dtr + bias)
    cum = _dot(_tri(CHUNK), dt * (-jnp.exp(alog)), ((1,), (0,)), precision=HI)
    return dt, cum


def _ssd_tile(xcs, bmc, cmc, cols, rows_, z, dvec, gn, states, hpg):
    npair = hpg // 2
    hd = SSD_HEAD_DIM
    xs = [_silu(x) for x in xcs]
    bm, cm = _silu(bmc), _silu(cmc)
    cb = _dot_nt(cm, bm)
    causal = _tri(CHUNK) > 0
    lane_lo = lax.broadcasted_iota(jnp.int32, (1, 2 * hd), 1) < hd
    lastsel = (lax.broadcasted_iota(jnp.int32, (CHUNK, 1), 0) == CHUNK - 1).astype(F32)
    heads = []
    for r in range(hpg):
        dtc, cumc = _lane_pick(cols, r), _lane_pick(cols, hpg + r)
        dtrow, cumr = _row_pick(rows_, r), _row_pick(rows_, hpg + r)
        w = cb * jnp.exp(jnp.where(causal, cumc - cumr, -jnp.inf)) * dtrow
        last = jnp.sum(cumc * lastsel, axis=0, keepdims=True)
        heads.append((w, jnp.exp(cumc), jnp.exp(last - cumc) * dtc, jnp.exp(last)))
    ys, new_states = [], []
    for j in range(npair):
        (wa, ea, da, la), (wb, eb, db, lb) = heads[2 * j], heads[2 * j + 1]
        yi = jnp.where(lane_lo, _dot_nn(wa, xs[j]), _dot_nn(wb, xs[j]))
        ys.append(yi + jnp.where(lane_lo, ea, eb) * _dot_nn(cm, states[j]))
        xd = xs[j] * jnp.where(lane_lo, da, db)
        new_states.append(jnp.where(lane_lo, la, lb) * states[j] + _dot_tn(bm, xd))
    y = jnp.concatenate(ys, axis=1) + dvec * jnp.concatenate(xs, axis=1)
    yg = y * _silu(z)
    yn = yg * lax.rsqrt(jnp.mean(yg * yg, axis=1, keepdims=True) + NORM_EPS) * gn
    return yn, new_states


def _ssd_specs(nc, hpg, ng, rev):
    npair = hpg // 2
    gw = hpg * SSD_HEAD_DIM
    xblocks = ng * npair
    ch = (lambda c: nc - 1 - c) if rev else (lambda c: c)
    xs = [pl.BlockSpec((CHUNK, LANES), functools.partial(lambda b_, c, g, jj: (b_ * nc + ch(c), g * npair + jj), jj=j))
          for j in range(npair)]
    bmb = pl.BlockSpec((CHUNK, SSD_STATE), lambda b_, c, g: (b_ * nc + ch(c), xblocks + g))
    cmb = pl.BlockSpec((CHUNK, SSD_STATE), lambda b_, c, g: (b_ * nc + ch(c), xblocks + ng + g))
    colb = pl.BlockSpec((None, CHUNK, 2 * hpg), lambda b_, c, g: (g, b_ * nc + ch(c), 0))
    rowb = pl.BlockSpec((None, None, 2 * hpg, CHUNK), lambda b_, c, g: (b_ * nc + ch(c), g, 0, 0))
    zb = pl.BlockSpec((CHUNK, gw), lambda b_, c, g: (b_ * nc + ch(c), g))
    pb = pl.BlockSpec((1, gw), lambda b_, c, g: (0, g))
    sb = pl.BlockSpec((None, None, npair, SSD_STATE, 2 * SSD_HEAD_DIM), lambda b_, c, g: (b_ * nc + ch(c), g, 0, 0, 0))
    return xs, bmb, cmb, colb, rowb, zb, pb, sb


def ssd_core_fwd(cpre, cols, rows_, z, dvec, gn, nb, hpg, name):
    rows = cpre.shape[0]
    inner = z.shape[1]
    ng = inner // (hpg * SSD_HEAD_DIM)
    npair = hpg // 2
    nc = rows // nb // CHUNK
    xs, bmb, cmb, colb, rowb, zb, pb, sb = _ssd_specs(nc, hpg, ng, False)

    def body(*refs):
        x_refs = refs[:npair]
        bm_ref, cm_ref, col_ref, row_ref, z_ref, d_ref, gn_ref, y_ref, so_ref, st_ref = refs[npair:]
        c = pl.program_id(1)
        g = pl.program_id(2)

        @pl.when(c == 0)
        def _():
            st_ref[g] = jnp.zeros((npair, SSD_STATE, 2 * SSD_HEAD_DIM), F32)

        so_ref[...] = st_ref[g]
        states = [st_ref[g, j] for j in range(npair)]
        yn, new_states = _ssd_tile([r[...] for r in x_refs], bm_ref[...], cm_ref[...], col_ref[...], row_ref[...],
                                   z_ref[...], d_ref[...], gn_ref[...], states, hpg)
        y_ref[...] = yn.astype(BF16)
        for j in range(npair):
            st_ref[g, j] = new_states[j]

    return pl.pallas_call(
        body, name=name, grid=(nb, nc, ng), in_specs=xs + [bmb, cmb, colb, rowb, zb, pb, pb],
        out_specs=[zb, sb],
        out_shape=[jax.ShapeDtypeStruct((rows, inner), BF16),
                   jax.ShapeDtypeStruct((nb * nc, ng, npair, SSD_STATE, 2 * SSD_HEAD_DIM), F32)],
        scratch_shapes=[pltpu.VMEM((ng, npair, SSD_STATE, 2 * SSD_HEAD_DIM), F32)],
        compiler_params=_params(("arbitrary", "arbitrary", "arbitrary")))(
            *([cpre] * npair), cpre, cpre, cols, rows_, z, dvec, gn)


def ssd_core_bwd(cpre, cols, rows_, z, dvec, gn, states, dyn, nb, hpg, name):
    rows = cpre.shape[0]
    inner = z.shape[1]
    gw = hpg * SSD_HEAD_DIM
    ng = inner // gw
    npair = hpg // 2
    nc = rows // nb // CHUNK
    xs, bmb, cmb, colb, rowb, zb, pb, sb = _ssd_specs(nc, hpg, ng, True)

    def body(*refs):
        x_refs = refs[:npair]
        (bm_ref, cm_ref, col_ref, row_ref, z_ref, d_ref, gn_ref, s_ref, dy_ref,
         dx_ref, dbm_ref, dcm_ref, dcol_ref, drow_ref, dz_ref, dd_ref, dgn_ref, dst_ref) = refs[npair:]
        bb = pl.program_id(0)
        step = pl.program_id(1)
        g = pl.program_id(2)

        @pl.when(jnp.logical_and(bb == 0, jnp.logical_and(step == 0, g == 0)))
        def _():
            dd_ref[...] = jnp.zeros_like(dd_ref)
            dgn_ref[...] = jnp.zeros_like(dgn_ref)

        @pl.when(step == 0)
        def _():
            dst_ref[g] = jnp.zeros((npair, SSD_STATE, 2 * SSD_HEAD_DIM), F32)

        def f(xcs, bmc, cmc, cv, rv, zv, dv_, gv, sts):
            return _ssd_tile(xcs, bmc, cmc, cv, rv, zv, dv_, gv, sts, hpg)

        _, vjp = jax.vjp(f, [r[...] for r in x_refs], bm_ref[...], cm_ref[...], col_ref[...], row_ref[...], z_ref[...],
                         d_ref[...], gn_ref[...], [s_ref[j] for j in range(npair)])
        gr = vjp((dy_ref[...], [dst_ref[g, j] for j in range(npair)]))
        dx_ref[...] = jnp.concatenate(gr[0], axis=1)
        dbm_ref[...] = gr[1]
        dcm_ref[...] = gr[2]
        dcol_ref[...] = gr[3]
        drow_ref[...] = gr[4]
        dz_ref[...] = gr[5].astype(dz_ref.dtype)
        dd_ref[g] += gr[6]
        dgn_ref[g] += gr[7]
        for j in range(npair):
            dst_ref[g, j] = gr[8][j]

    ch = lambda c: nc - 1 - c
    nblk = pl.BlockSpec((CHUNK, SSD_STATE), lambda b_, c, g: (b_ * nc + ch(c), g))
    accb = pl.BlockSpec((ng, 1, gw), lambda b_, c, g: (0, 0, 0))
    return pl.pallas_call(
        body, name=name, grid=(nb, nc, ng), in_specs=xs + [bmb, cmb, colb, rowb, zb, pb, pb, sb, zb],
        out_specs=[zb, nblk, nblk, colb, rowb, zb, accb, accb],
        out_shape=[jax.ShapeDtypeStruct((rows, inner), F32), jax.ShapeDtypeStruct((rows, ng * SSD_STATE), F32),
                   jax.ShapeDtypeStruct((rows, ng * SSD_STATE), F32), jax.ShapeDtypeStruct(cols.shape, F32),
                   jax.ShapeDtypeStruct(rows_.shape, F32), jax.ShapeDtypeStruct((rows, inner), BF16),
                   jax.ShapeDtypeStruct((ng, 1, gw), F32), jax.ShapeDtypeStruct((ng, 1, gw), F32)],
        scratch_shapes=[pltpu.VMEM((ng, npair, SSD_STATE, 2 * SSD_HEAD_DIM), F32)],
        compiler_params=_params(("arbitrary", "arbitrary", "arbitrary")))(
            *([cpre] * npair), cpre, cpre, cols, rows_, z, dvec, gn, states, dyn)


def _hw_expand(w):
    n, bi, _ = w.shape
    per = PROJ_BLOCK // bi
    tiled = jnp.tile(w.reshape(n // per, PROJ_BLOCK, bi), (1, 1, per))
    return jnp.where(_hw_mask(bi), tiled, 0.0)


def _hw_mask(bi):
    r = lax.broadcasted_iota(jnp.int32, (PROJ_BLOCK, PROJ_BLOCK), 0) // bi
    c = lax.broadcasted_iota(jnp.int32, (PROJ_BLOCK, PROJ_BLOCK), 1) // bi
    return r == c


def _hw_contract(d, bi=QKV_BLOCK):
    per = PROJ_BLOCK // bi
    kept = jnp.where(_hw_mask(bi), d, 0.0)
    return kept.reshape(d.shape[0], PROJ_BLOCK, per, bi).sum(axis=2).reshape(-1, bi, bi)


def _wg_expand(wg, width):
    pad = jnp.pad(wg, ((0, 0), (0, LANES - wg.shape[1])))
    return [pad[i * width:(i + 1) * width].reshape(width // PROJ_BLOCK, PROJ_BLOCK, LANES) for i in range(3)]


def _wg_contract(dgs, ngate):
    return jnp.concatenate([d[:, :, :ngate].reshape(-1, ngate) for d in dgs], axis=0)


def _pad_lanes(a):
    return jnp.pad(a, ((0, 0), (0, LANES - a.shape[1])))


def _pairs_to_layouts(first, second, ngrp, per, nbc):
    rows = first.shape[0]
    both = jnp.concatenate([first.reshape(rows, ngrp, per), second.reshape(rows, ngrp, per)], axis=2)
    return both.transpose(1, 0, 2), both.reshape(nbc, CHUNK, ngrp, 2 * per).transpose(0, 2, 3, 1)


def _layouts_to_pairs(dcols, drows, ngrp, per):
    rows = dcols.shape[1]
    both = dcols.transpose(1, 0, 2) + drows.transpose(0, 3, 1, 2).reshape(rows, ngrp, 2 * per)
    return both[:, :, :per].reshape(rows, ngrp * per), both[:, :, per:].reshape(rows, ngrp * per)


_EARLY = ("W0a", "W0xb", "W0zb", "glu")
_LATE = ("Wo0a", "Wo0b", "W1z", "W1x", "W1dt", "Wo1")


def _local_step(x, target, bw, sp, late_weights=None, late_grads=None, early_grads=None):
    nb, seq, d = x.shape
    nh, hpg = MLSTM_HEADS, SSD_HPG
    t_len = N_META + seq
    nc = -(-t_len // CHUNK)
    tp = nc * CHUNK
    rows = nb * tp
    nbc = nb * nc
    meta = sp["meta_tokens"]
    h0 = jnp.concatenate([jnp.broadcast_to(meta[None], (nb, N_META, d)), x, jnp.zeros((nb, tp - t_len, d), F32)], axis=1)
    h0 = h0.reshape(rows, d)
    tgt = jnp.pad(target, ((0, 0), (N_META, tp - t_len), (0, 0))).reshape(rows, d)

    n0 = norm_fwd(h0, sp["ab_norm"], "norm0")
    pa = mm(n0, bw["W0a"], "nn", "mm_pa")
    xb = mm(n0, bw["W0xb"], "nn", "mm_xb")
    zb = mm(n0, bw["W0zb"], "nn", "mm_zb")
    s5w = pa.shape[1] // 2
    mlw = xb.shape[1]
    s5_args = (sp["s5_lambda_re"], sp["s5_lambda_im"], sp["s5_log_dt"].reshape(-1), sp["s5_b_re"], sp["s5_b_im"])
    (ar, ai, bbr, bbi), s5_disc_vjp = jax.vjp(_s5_discretize, *s5_args)
    sg, spn, shh = bbr.shape
    bre, bim, cre, cim, are, aie = _s5_expand(ar, ai, bbr, bbi, sp["s5_c_re"], sp["s5_c_im"])
    ys5, gb, s5st = s5_fwd(pa, bre, bim, cre, cim, are, aie, sp["s5_d"], nb, "s5_fwd")
    tglu = mm(gb, bw["glu"], "nn", "mm_glu")

    def glu_tile(ys, tt, za, gbias):
        return _gelu(ys) * _sigmoid(tt + gbias) * _silu(za)

    ya = rowwise("glu_fwd", lambda i, ys, tt, pab, gbias: glu_tile(ys, tt, pab[:, s5w:], gbias),
                 [ys5, tglu, pa], [sp["s5_glu_b"]], [(s5w, BF16)], tr=_tile(rows, 256, 16))[0]

    cpre0 = conv_fwd(xb, sp["ml_conv_w"], sp["ml_conv_b"], nb, "ml_conv_fwd")
    wq_e, wk_e, wv_e = _hw_expand(sp["ml_wq"]), _hw_expand(sp["ml_wk"]), _hw_expand(sp["ml_wv"])
    gq, gk, gv = _wg_expand(sp["ml_w_gate"], mlw)
    q, k, v, gl = ml_proj_fwd(cpre0, xb, wq_e, wk_e, wv_e, gq, gk, gv, "ml_proj_fwd")
    bgate = _pad_lanes(sp["ml_b_gate"])
    gout = rowwise("ml_gates_fwd", lambda i, g_, b_: _ml_gates_tile(g_, b_, nh), [gl], [bgate], [(LANES, F32)], tr=CHUNK)[0]
    colg, rowg = _pairs_to_layouts(gout[:, :nh], gout[:, nh:2 * nh], nh, 1, nbc)
    yb, ml_cs, ml_ns, ml_ms = ml_core_fwd(q, k, v, colg, rowg, cpre0, zb, sp["ml_norm"], sp["ml_skip"], nb, nh, "ml_core_fwd")
    if late_weights is not None:
        bw = {**bw, **late_weights()}
    h1 = mm(ya, bw["Wo0a"], "nn", "mm_out0a", resid=h0)
    h1 = mm(yb, bw["Wo0b"], "nn", "mm_out0b", resid=h1)

    n1 = norm_fwd(h1, sp["ssd_norm"], "norm1")
    z1 = mm(n1, bw["W1z"], "nn", "mm_z1")
    xbc = mm(n1, bw["W1x"], "nn", "mm_xbc")
    dtr = mm(n1, bw["W1dt"], "nn", "mm_dt")
    inner = z1.shape[1]
    ng = inner // (hpg * SSD_HEAD_DIM)
    nhd = ng * hpg
    cpre1 = conv_fwd(xbc, sp["ssd_conv_w"], sp["ssd_conv_b"], nb, "ssd_conv_fwd")
    dt_bias, a_log = _pad_lanes(sp["ssd_dt_bias"]), _pad_lanes(sp["ssd_a_log"])
    dt, cum = rowwise("ssd_dt_fwd", lambda i, r_, b_, a_: _ssd_dt_tile(r_, b_, a_), [dtr], [dt_bias, a_log],
                      [(LANES, F32), (LANES, F32)], tr=CHUNK)
    cols, rws = _pairs_to_layouts(dt[:, :nhd], cum[:, :nhd], ng, hpg, nbc)
    dvec = jnp.repeat(sp["ssd_d"], SSD_HEAD_DIM, axis=1)
    yn, ssd_st = ssd_core_fwd(cpre1, cols, rws, z1, dvec, sp["ssd_gnorm"], nb, hpg, "ssd_core_fwd")
    h2 = mm(yn, bw["Wo1"], "nn", "mm_out1", resid=h1)

    tr_l = _tile(tp, 256, 16)
    per_ex = tp // tr_l

    def loss_tile(i, hb, tb, gfn):
        tpos = (i % per_ex) * tr_l + lax.broadcasted_iota(jnp.int32, (tr_l, 1), 0)
        mask = jnp.logical_and(tpos >= N_META, tpos < t_len).astype(F32)

        def lf(hh, gg):
            e = (_rms(hh, gg) - tb) * mask
            return 0.5 * jnp.sum(e * e) / d

        lval, (dh, dg) = jax.value_and_grad(lf, (0, 1))(hb, gfn)
        return dh, dh, jnp.full((1, LANES), lval, F32), dg

    fn = sp["final_norm"].reshape(1, d)
    dh2, dh2b, loss_acc, dfn = rowwise("loss", loss_tile, [h2, tgt], [fn], [(d, F32), (d, BF16)], [(1, LANES), (1, d)], tr=tr_l)

    gbig, gs = {}, {}
    gs["final_norm"] = dfn.reshape(sp["final_norm"].shape)
    dyn = mm(dh2b, bw["Wo1"], "nt", "mm_dyn")
    gbig["Wo1"] = mm(yn, dh2b, "tn", "mm_dWo1", out_dtype=BF16)
    dxs, dbm, dcm, dcols, drws, dz1, ddvec, dgn = ssd_core_bwd(cpre1, cols, rws, z1, dvec, sp["ssd_gnorm"], ssd_st, dyn,
                                                              nb, hpg, "ssd_core_bwd")
    gs["ssd_d"] = ddvec.reshape(1, nhd, SSD_HEAD_DIM).sum(axis=2)
    gs["ssd_gnorm"] = dgn.reshape(1, inner)
    ddt, dcum = _layouts_to_pairs(dcols, drws, ng, hpg)

    def ssd_dt_bwd_tile(i, r_, ddt_, dcum_, b_, a_):
        _, vjp = jax.vjp(_ssd_dt_tile, r_, b_, a_)
        return vjp((ddt_, dcum_))

    ddtr, dbias, dalog = rowwise("ssd_dt_bwd", ssd_dt_bwd_tile, [dtr, _pad_lanes(ddt), _pad_lanes(dcum)], [dt_bias, a_log],
                                 [(LANES, BF16)], [(1, LANES), (1, LANES)], tr=CHUNK)
    gs["ssd_dt_bias"] = dbias[:, :nhd]
    gs["ssd_a_log"] = dalog[:, :nhd]
    dcpre1 = jnp.concatenate([dxs, dbm, dcm], axis=1)
    dxbc, dcw1, dcb1 = conv_bwd(dcpre1, xbc, sp["ssd_conv_w"], nb, "ssd_conv_bwd")
    gs["ssd_conv_w"] = dcw1
    gs["ssd_conv_b"] = dcb1
    dn1 = mm(dz1, bw["W1z"], "nt", "mm_dn1z")
    dn1 = mm(dxbc, bw["W1x"], "nt", "mm_dn1x", resid=dn1)
    dn1 = mm(ddtr, bw["W1dt"], "nt", "mm_dn1dt", resid=dn1)
    gbig["W1z"] = mm(n1, dz1, "tn", "mm_dW1z", out_dtype=BF16)
    gbig["W1x"] = mm(n1, dxbc, "tn", "mm_dW1x", out_dtype=BF16)
    gbig["W1dt"] = mm(n1, ddtr, "tn", "mm_dW1dt", out_dtype=BF16)
    dh1, dh1b, dg1 = norm_bwd(h1, sp["ssd_norm"], dn1, dh2, "norm1_bwd")
    gs["ssd_norm"] = dg1

    gbig["Wo0a"] = mm(ya, dh1b, "tn", "mm_dWo0a", out_dtype=BF16)
    gbig["Wo0b"] = mm(yb, dh1b, "tn", "mm_dWo0b", out_dtype=BF16)
    if late_grads is not None:
        late_grads({n: gbig[n] for n in _LATE})
    dya = mm(dh1b, bw["Wo0a"], "nt", "mm_dya")
    dyb = mm(dh1b, bw["Wo0b"], "nt", "mm_dyb")
    (dq, dk, dv, dcp_skip, dzb, dcolg, drowg, dnw, dsk) = ml_core_bwd(
        q, k, v, colg, rowg, cpre0, zb, sp["ml_norm"], sp["ml_skip"], ml_cs, ml_ns, ml_ms, dyb, nb, nh, "ml_core_bwd")
    gs["ml_norm"] = dnw.reshape(1, mlw)
    gs["ml_skip"] = dsk.reshape(1, mlw)
    dig, dbcum = _layouts_to_pairs(dcolg, drowg, nh, 1)
    dgout = _pad_lanes(jnp.concatenate([dig, dbcum], axis=1))

    def ml_gates_bwd_tile(i, g_, dgo, b_):
        _, vjp = jax.vjp(lambda a, b: _ml_gates_tile(a, b, nh), g_, b_)
        return vjp(dgo)

    dgl, dbg = rowwise("ml_gates_bwd", ml_gates_bwd_tile, [gl, dgout], [bgate], [(LANES, F32)], [(1, LANES)], tr=CHUNK)
    gs["ml_b_gate"] = dbg[:, :2 * nh]
    dcpre0, dxb_v, dwq, dwk, dwv, dgq, dgk, dgv = ml_proj_bwd(cpre0, xb, wq_e, wk_e, wv_e, gq, gk, gv, dq, dk, dv, dgl,
                                                            dcp_skip, "ml_proj_bwd")
    gs["ml_wq"], gs["ml_wk"], gs["ml_wv"] = _hw_contract(dwq), _hw_contract(dwk), _hw_contract(dwv)
    gs["ml_w_gate"] = _wg_contract([dgq, dgk, dgv], 2 * nh)
    dxb, dcw0, dcb0 = conv_bwd(dcpre0, xb, sp["ml_conv_w"], nb, "ml_conv_bwd", resid=dxb_v)
    gs["ml_conv_w"] = dcw0
    gs["ml_conv_b"] = dcb0

    def glu_bwd_tile(i, ys, tt, pab, dy_, gbias):
        _, vjp = jax.vjp(glu_tile, ys, tt, pab[:, s5w:], gbias)
        return vjp(dy_)

    dys_direct, dtglu, dza, dglub = rowwise("glu_bwd", glu_bwd_tile, [ys5, tglu, pa, dya], [sp["s5_glu_b"]],
                                            [(s5w, F32), (s5w, BF16), (s5w, BF16)], [(1, s5w)], tr=_tile(rows, 256, 16))
    gs["s5_glu_b"] = dglub
    dgb = mm(dtglu, bw["glu"], "nt", "mm_dgb")
    gbig["glu"] = mm(gb, dtglu, "tn", "mm_dglu", out_dtype=BF16)

    def gelu_bwd_tile(i, ys, dg_, direct):
        _, vjp = jax.vjp(_gelu, ys)
        return vjp(dg_)[0] + direct

    dys5 = rowwise("gelu_bwd", gelu_bwd_tile, [ys5, dgb, dys_direct], [], [(s5w, F32)], tr=_tile(rows, 256, 16))[0]
    du, dbre, dbim, dcre, dcim, dare, daie, dd5 = s5_bwd(pa, dys5, s5st, bre, bim, cre, cim, are, aie, sp["s5_d"], nb, "s5_bwd")
    gs["s5_d"] = dd5
    dbbr, dbbi, dcr, dci, dar, dai = _s5_contract(dbre, dbim, dcre, dcim, dare, daie, sg, spn, shh)
    gs["s5_c_re"], gs["s5_c_im"] = dcr, dci
    (gs["s5_lambda_re"], gs["s5_lambda_im"], dlogdt, gs["s5_b_re"], gs["s5_b_im"]) = s5_disc_vjp((dar, dai, dbbr, dbbi))
    gs["s5_log_dt"] = dlogdt.reshape(1, -1)
    dpa = jnp.concatenate([du, dza], axis=1)
    gbig["W0a"] = mm(n0, dpa, "tn", "mm_dW0a", out_dtype=BF16)
    gbig["W0xb"] = mm(n0, dxb, "tn", "mm_dW0xb", out_dtype=BF16)
    gbig["W0zb"] = mm(n0, dzb, "tn", "mm_dW0zb", out_dtype=BF16)
    if early_grads is not None:
        early_grads({n: gbig[n] for n in _EARLY})
    dn0 = mm(dpa, bw["W0a"], "nt", "mm_dn0a")
    dn0 = mm(dxb, bw["W0xb"], "nt", "mm_dn0xb", resid=dn0)
    dn0 = mm(dzb, bw["W0zb"], "nt", "mm_dn0zb", resid=dn0)
    dh0, _, dg0 = norm_bwd(h0, sp["ab_norm"], dn0, dh1, "norm0_bwd")
    gs["ab_norm"] = dg0
    dh0 = dh0.reshape(nb, tp, d)
    gs["meta_tokens"] = jnp.sum(dh0[:, :N_META], axis=0)
    return loss_acc[0, 0], dh0, gbig, gs


N_DEV = 8
N_CHIP = 4
N_PEER_CHIPS = N_CHIP - 1
MESH = pl.DeviceIdType.MESH
_HBM = pl.BlockSpec(memory_space=pltpu.HBM)


def _place():
    x, y, c = lax.axis_index("x"), lax.axis_index("y"), lax.axis_index("c")
    return x, y, c, [(1 - x, y), (x, 1 - y), (1 - x, 1 - y)]


def swap_halves(gs_, name):
    na = len(gs_)

    def body(*refs):
        g_refs, out_refs = refs[:na], refs[na:2 * na]
        send_sems, recv_sems = refs[2 * na:]
        x, y, c, _ = _place()
        cps = [pltpu.make_async_remote_copy(
            src_ref=g_refs[i].at[kk, 1 - c], dst_ref=out_refs[i].at[kk], send_sem=send_sems.at[N_CHIP * i + kk],
            recv_sem=recv_sems.at[N_CHIP * i + kk], device_id=(x, y, 1 - c), device_id_type=MESH)
            for i in range(na) for kk in range(N_CHIP)]
        for cp in cps:
            cp.start()
        for cp in cps:
            cp.wait()

    return pl.pallas_call(
        body, name=name, out_shape=[jax.ShapeDtypeStruct((N_CHIP,) + g.shape[2:], g.dtype) for g in gs_],
        in_specs=[_HBM] * na, out_specs=[_HBM] * na,
        scratch_shapes=[pltpu.SemaphoreType.DMA((N_CHIP * na,)), pltpu.SemaphoreType.DMA((N_CHIP * na,))])(*gs_)


def add_halves(g, other, core, name):
    _, _, m, n = g.shape
    tr = _tile(m, 256, 16)

    def body(core_ref, g_ref, o_ref, out_ref):
        out_ref[...] = (g_ref[...].astype(F32) + o_ref[...].astype(F32)).astype(out_ref.dtype)

    grid_spec = pltpu.PrefetchScalarGridSpec(
        num_scalar_prefetch=1, grid=(N_CHIP, m // tr),
        in_specs=[pl.BlockSpec((None, None, tr, n), lambda kk, i, core_ref: (kk, core_ref[0], i, 0)),
                  pl.BlockSpec((None, tr, n), lambda kk, i, core_ref: (kk, i, 0))],
        out_specs=pl.BlockSpec((None, tr, n), lambda kk, i, core_ref: (kk, i, 0)))
    return pl.pallas_call(body, name=name, grid_spec=grid_spec, out_shape=jax.ShapeDtypeStruct((N_CHIP, m, n), g.dtype),
                          compiler_params=_params(("arbitrary", "arbitrary")))(core.reshape(1).astype(jnp.int32), g, other)


def sequencer_exchange(srcs, scatter, collective_id, name):
    na = len(srcs)
    per = 2 * N_PEER_CHIPS + (1 if scatter else 0)
    hbm = pltpu.MemorySpace.HBM
    src_refs = [jax.new_ref(a, memory_space=hbm) for a in srcs]
    out_refs = [jax.empty_ref(jax.ShapeDtypeStruct((N_CHIP, 2) + a.shape[1:], a.dtype), memory_space=hbm) for a in srcs]

    @pl.kernel(mesh=plsc.ScalarSubcoreMesh(axis_name="seq", num_cores=1), name=name,
               scratch_types=(pltpu.SemaphoreType.DMA((per * na,)), pltpu.SemaphoreType.DMA((per * na,)),
                              pltpu.SemaphoreType.DMA((na,))),
               compiler_params=pltpu.CompilerParams(collective_id=collective_id))
    def launch(send_sems, recv_sems, local_sems):
        x, y, c, chips = _place()
        k = 2 * x + y
        sibling = (x, y, 1 - c)
        barrier = pltpu.get_barrier_semaphore()
        for cx, cy in chips:
            pl.semaphore_signal(barrier, inc=1, device_id=(cx, cy, c), device_id_type=MESH)
        pl.semaphore_signal(barrier, inc=1, device_id=sibling, device_id_type=MESH)
        pl.semaphore_wait(barrier, N_CHIP)

        def copy(i, kk, src, chip_k, half, to):
            return pltpu.make_async_remote_copy(
                src_ref=src, dst_ref=out_refs[i].at[chip_k, half], send_sem=send_sems.at[per * i + kk],
                recv_sem=recv_sems.at[per * i + kk], device_id=to, device_id_type=MESH)

        if scatter:
            mine = [pltpu.make_async_copy(src_refs[i].at[k], out_refs[i].at[k, c], local_sems.at[i]) for i in range(na)]
        else:
            mine = [pltpu.make_async_copy(src_refs[i], out_refs[i].at[k], local_sems.at[i]) for i in range(na)]
        for cp in mine:
            cp.start()
        first = []
        for j, (cx, cy) in enumerate(chips):
            for i in range(na):
                src = src_refs[i].at[2 * cx + cy] if scatter else src_refs[i].at[c]
                first.append(copy(i, j, src, k, c, (cx, cy, c)))
        if scatter:
            first += [copy(i, 2 * N_PEER_CHIPS, src_refs[i].at[k], k, c, sibling) for i in range(na)]
        for cp in first:
            cp.start()
        passed = []
        for j, (cx, cy) in enumerate(chips):
            kj = 2 * cx + cy
            for i in range(na):
                copy(i, j, out_refs[i].at[kj, c], kj, c, (cx, cy, c)).wait_recv()
                fwd = copy(i, N_PEER_CHIPS + j, out_refs[i].at[kj, c], kj, c, sibling)
                fwd.start()
                passed.append(fwd)
        if scatter:
            for i in range(na):
                copy(i, 2 * N_PEER_CHIPS, out_refs[i].at[k, 1 - c], k, 1 - c, sibling).wait_recv()
        for j, (cx, cy) in enumerate(chips):
            kj = 2 * cx + cy
            for i in range(na):
                copy(i, N_PEER_CHIPS + j, out_refs[i].at[kj, 1 - c], kj, 1 - c, sibling).wait_recv()
        for cp in first + passed:
            cp.wait_send()
        for cp in mine:
            cp.wait()

    launch()
    return [r[...] for r in out_refs]


def _pack(arrs, dtype, lanes, row_align):
    flat = jnp.concatenate([a.reshape(-1).astype(dtype) for a in arrs])
    unit = lanes * row_align
    total = -(-flat.shape[0] // unit) * unit
    return jnp.pad(flat, (0, total - flat.shape[0])).reshape(total // lanes, lanes)


def _unpack(flat, shapes):
    flat = flat.reshape(-1)
    out, off = [], 0
    for s in shapes:
        n = math.prod(s)
        out.append(flat[off:off + n].reshape(s))
        off += n
    return out


def _adam_tile(w, m, v, g):
    m2 = ADAM_B1 * m + (1.0 - ADAM_B1) * g
    v2 = ADAM_B2 * v + (1.0 - ADAM_B2) * (g * g)
    m_hat = m2 / (1.0 - ADAM_B1 ** ADAM_STEP)
    v_hat = v2 / (1.0 - ADAM_B2 ** ADAM_STEP)
    delta = -ADAM_LR * (m_hat / (jnp.sqrt(v_hat) + ADAM_EPS) + ADAM_WD * w)
    return delta, m2, v2


def adam_big(w, m, v, pieces, name):
    _, r, c = w.shape
    tr = _tile(r, 128, 16)

    def body(w_ref, m_ref, v_ref, p0, p1, p2, p3, g_ref, d_ref, mo_ref, vo_ref):
        g = ((p0[...].astype(F32) + p1[...].astype(F32)) + p2[...].astype(F32)) + p3[...].astype(F32)
        delta, m2, v2 = _adam_tile(w_ref[...], m_ref[...], v_ref[...], g)
        g_ref[...] = g
        d_ref[...] = delta
        mo_ref[...] = m2
        vo_ref[...] = v2

    wspec = pl.BlockSpec((None, tr, c), lambda i: (0, i, 0))
    pspecs = [pl.BlockSpec((None, tr, c), functools.partial(lambda i, kk: (kk, i, 0), kk=kk)) for kk in range(N_CHIP)]
    return pl.pallas_call(
        body, name=name, grid=(r // tr,), in_specs=[wspec] * 3 + pspecs, out_specs=[wspec] * 4,
        out_shape=[jax.ShapeDtypeStruct(w.shape, F32)] * 4, compiler_params=_params(("parallel",)))(
            w, m, v, pieces, pieces, pieces, pieces)


_WEIGHTS = (
    ("meta_tokens", "small", 1), ("ab_norm", "small", None), ("ab_w_in", "big", 2), ("s5_lambda_re", "small", None),
    ("s5_lambda_im", "small", None), ("s5_log_dt", "small", None), ("s5_b_re", "small", None), ("s5_b_im", "small", None),
    ("s5_c_re", "small", None), ("s5_c_im", "small", None), ("s5_d", "small", None), ("s5_glu_w", "big", 1),
    ("s5_glu_b", "small", None), ("ml_conv_w", "small", 2), ("ml_conv_b", "small", None), ("ml_wq", "small", 1),
    ("ml_wk", "small", 1), ("ml_wv", "small", 1), ("ml_w_gate", "small", 1), ("ml_b_gate", "small", None),
    ("ml_norm", "small", None), ("ml_skip", "small", None), ("ab_w_out", "big", 1), ("ssd_norm", "small", 1),
    ("ssd_w_in", "big", 2), ("ssd_conv_w", "small", 2), ("ssd_conv_b", "small", 1), ("ssd_dt_bias", "small", None),
    ("ssd_a_log", "small", None), ("ssd_d", "small", None), ("ssd_gnorm", "small", 1), ("ssd_w_out", "big", 1),
    ("final_norm", "small", None),
)


def _squeeze(a):
    return a[0] if a.ndim >= 3 else a


def kernel(x, meta_tokens, ab_norm, ab_w_in, s5_lambda_re, s5_lambda_im, s5_log_dt, s5_b_re, s5_b_im, s5_c_re, s5_c_im, s5_d, s5_glu_w, s5_glu_b, ml_conv_w, ml_conv_b, ml_wq, ml_wk, ml_wv, ml_w_gate, ml_b_gate, ml_norm, ml_skip, ab_w_out, ssd_norm, ssd_w_in, ssd_conv_w, ssd_conv_b, ssd_dt_bias, ssd_a_log, ssd_d, ssd_gnorm, ssd_w_out, final_norm, loss_target, m_meta_tokens, m_ab_norm, m_ab_w_in, m_s5_lambda_re, m_s5_lambda_im, m_s5_log_dt, m_s5_b_re, m_s5_b_im, m_s5_c_re, m_s5_c_im, m_s5_d, m_s5_glu_w, m_s5_glu_b, m_ml_conv_w, m_ml_conv_b, m_ml_wq, m_ml_wk, m_ml_wv, m_ml_w_gate, m_ml_b_gate, m_ml_norm, m_ml_skip, m_ab_w_out, m_ssd_norm, m_ssd_w_in, m_ssd_conv_w, m_ssd_conv_b, m_ssd_dt_bias, m_ssd_a_log, m_ssd_d, m_ssd_gnorm, m_ssd_w_out, m_final_norm, v_meta_tokens, v_ab_norm, v_ab_w_in, v_s5_lambda_re, v_s5_lambda_im, v_s5_log_dt, v_s5_b_re, v_s5_b_im, v_s5_c_re, v_s5_c_im, v_s5_d, v_s5_glu_w, v_s5_glu_b, v_ml_conv_w, v_ml_conv_b, v_ml_wq, v_ml_wk, v_ml_wv, v_ml_w_gate, v_ml_b_gate, v_ml_norm, v_ml_skip, v_ab_w_out, v_ssd_norm, v_ssd_w_in, v_ssd_conv_w, v_ssd_conv_b, v_ssd_dt_bias, v_ssd_a_log, v_ssd_d, v_ssd_gnorm, v_ssd_w_out, v_final_norm):
    args = (meta_tokens, ab_norm, ab_w_in, s5_lambda_re, s5_lambda_im, s5_log_dt, s5_b_re, s5_b_im, s5_c_re, s5_c_im, s5_d, s5_glu_w, s5_glu_b, ml_conv_w, ml_conv_b, ml_wq, ml_wk, ml_wv, ml_w_gate, ml_b_gate, ml_norm, ml_skip, ab_w_out, ssd_norm, ssd_w_in, ssd_conv_w, ssd_conv_b, ssd_dt_bias, ssd_a_log, ssd_d, ssd_gnorm, ssd_w_out, final_norm)
    m_args = (m_meta_tokens, m_ab_norm, m_ab_w_in, m_s5_lambda_re, m_s5_lambda_im, m_s5_log_dt, m_s5_b_re, m_s5_b_im, m_s5_c_re, m_s5_c_im, m_s5_d, m_s5_glu_w, m_s5_glu_b, m_ml_conv_w, m_ml_conv_b, m_ml_wq, m_ml_wk, m_ml_wv, m_ml_w_gate, m_ml_b_gate, m_ml_norm, m_ml_skip, m_ab_w_out, m_ssd_norm, m_ssd_w_in, m_ssd_conv_w, m_ssd_conv_b, m_ssd_dt_bias, m_ssd_a_log, m_ssd_d, m_ssd_gnorm, m_ssd_w_out, m_final_norm)
    v_args = (v_meta_tokens, v_ab_norm, v_ab_w_in, v_s5_lambda_re, v_s5_lambda_im, v_s5_log_dt, v_s5_b_re, v_s5_b_im, v_s5_c_re, v_s5_c_im, v_s5_d, v_s5_glu_w, v_s5_glu_b, v_ml_conv_w, v_ml_conv_b, v_ml_wq, v_ml_wk, v_ml_wv, v_ml_w_gate, v_ml_b_gate, v_ml_norm, v_ml_skip, v_ab_w_out, v_ssd_norm, v_ssd_w_in, v_ssd_conv_w, v_ssd_conv_b, v_ssd_dt_bias, v_ssd_a_log, v_ssd_d, v_ssd_gnorm, v_ssd_w_out, v_final_norm)
    names = [w[0] for w in _WEIGHTS]
    kind = {w[0]: w[1] for w in _WEIGHTS}
    axis = {w[0]: w[2] for w in _WEIGHTS}
    w_loc = dict(zip(names, args))
    m_loc = dict(zip(names, m_args))
    v_loc = dict(zip(names, v_args))
    chip = 2 * lax.axis_index("x") + lax.axis_index("y")
    core = lax.axis_index("c")
    big = [n for n in names if kind[n] == "big"]
    small = [n for n in names if kind[n] == "small"]
    small_sh = [n for n in small if axis[n] is not None]

    def halves(a):
        return a.astype(BF16).reshape(2, a.shape[1] // 2, a.shape[2])

    def assemble(n, gth):
        shard = gth.reshape((N_CHIP,) + w_loc[n].shape[1:])
        if axis[n] == 1:
            return shard.reshape(-1, shard.shape[2])
        return jnp.concatenate([shard[kk] for kk in range(N_CHIP)], axis=1)

    early = ["ab_w_in", "s5_glu_w"]
    late = ["ab_w_out", "ssd_w_in", "ssd_w_out"]
    small_sh_shapes = [w_loc[n].shape for n in small_sh]
    packed_s = _pack([w_loc[n] for n in small_sh], F32, LANES, 2 * SUBLANES)
    gathered = sequencer_exchange([halves(w_loc[n]) for n in early] + [packed_s.reshape(2, -1, LANES)], False, 5,
                                  "gather_early_w")
    after_early = (gathered[0][0, 0, 0, 0] * 0).astype(BF16)
    late_gathered = sequencer_exchange([halves(w_loc[n]) + after_early for n in late], False, 1, "gather_late_w")
    w_in0_shards = gathered[0].reshape((N_CHIP,) + w_loc["ab_w_in"].shape[1:])
    glu_full = assemble("s5_glu_w", gathered[1])

    def columns(shards, lo, hi):
        cw = shards.shape[2]
        parts = [shards[kk][:, max(lo - kk * cw, 0):min(hi - kk * cw, cw)]
                 for kk in range(N_CHIP) if lo < (kk + 1) * cw and hi > kk * cw]
        return parts[0] if len(parts) == 1 else jnp.concatenate(parts, axis=1)

    small_by_chip = gathered[2].reshape(N_CHIP, -1)
    sp = {}
    for n in small:
        if axis[n] is None:
            sp[n] = _squeeze(w_loc[n])
    per_chip = [_unpack(small_by_chip[kk], small_sh_shapes) for kk in range(N_CHIP)]
    for i, n in enumerate(small_sh):
        sp[n] = _squeeze(jnp.concatenate([per_chip[kk][i] for kk in range(N_CHIP)], axis=axis[n]))

    s5w = glu_full.shape[0]
    mlw = w_loc["ab_w_out"].shape[1] * N_CHIP - s5w
    inner = w_loc["ssd_w_out"].shape[1] * N_CHIP
    n_heads1 = sp["ssd_d"].shape[1]
    cdim = w_loc["ssd_w_in"].shape[2] * N_CHIP - inner - n_heads1
    bw = dict(W0a=columns(w_in0_shards, 0, 2 * s5w), W0xb=columns(w_in0_shards, 2 * s5w, 2 * s5w + mlw),
              W0zb=columns(w_in0_shards, 2 * s5w + mlw, 2 * (s5w + mlw)), glu=glu_full)

    def late_weights():
        fb = dict(zip(late, late_gathered))
        w_out0 = assemble("ab_w_out", fb["ab_w_out"])
        w1 = fb["ssd_w_in"].reshape((N_CHIP,) + w_loc["ssd_w_in"].shape[1:])
        return dict(Wo0a=w_out0[:s5w], Wo0b=w_out0[s5w:], W1z=columns(w1, 0, inner),
                    W1x=columns(w1, inner, inner + cdim), W1dt=_pad_lanes(columns(w1, inner + cdim, inner + cdim + n_heads1)),
                    Wo1=assemble("ssd_w_out", fb["ssd_w_out"]))

    def piece_columns(parts, lo, hi):
        out, off = [], 0
        for p in parts:
            a, b = max(lo - off, 0), min(hi - off, p.shape[1])
            if a < b:
                out.append(p[:, a:b])
            off += p.shape[1]
        return out[0] if len(out) == 1 else jnp.concatenate(out, axis=1)

    def chip_halves(n, parts):
        _, r, c_ = w_loc[n].shape
        if axis[n] == 1:
            whole = parts[0] if len(parts) == 1 else jnp.concatenate(parts, axis=0)
            return whole.reshape(N_CHIP, 2, r // 2, c_)
        shards = [piece_columns(parts, kk * c_, (kk + 1) * c_) for kk in range(N_CHIP)]
        return jnp.stack(shards).reshape(N_CHIP, 2, r // 2, c_)

    pieces = {}

    def reduce_group(ns, gfull, tag, collective_id):
        gps = [chip_halves(n, gfull[n]) for n in ns]
        from_sibling = swap_halves(gps, "swap_" + tag)
        partials = [add_halves(gp, oth, core, "add_" + n) for n, gp, oth in zip(ns, gps, from_sibling)]
        pieces.update(zip(ns, sequencer_exchange(partials, True, collective_id, "scatter_" + tag)))

    def late_grads(g):
        gfull = {"ab_w_out": [g["Wo0a"], g["Wo0b"]], "ssd_w_in": [g["W1z"], g["W1x"], g["W1dt"][:, :n_heads1]],
                 "ssd_w_out": [g["Wo1"]]}
        reduce_group(late, gfull, "late_g", 2)

    def early_grads(g):
        gfull = {"ab_w_in": [g["W0a"], g["W0xb"], g["W0zb"]], "s5_glu_w": [g["glu"]]}
        reduce_group(early, gfull, "early_g", 3)

    loss_local, dh0, gbig, gs = _local_step(x, loss_target, bw, sp, late_weights, late_grads, early_grads)
    grad_x = dh0[:, N_META:N_META + x.shape[1]]

    out_g, out_d, out_m, out_v = {}, {}, {}, {}
    small_full_shapes = [sp[n].shape for n in small] + [(1, 1)]
    packed_gs = _pack([gs[n] for n in small] + [loss_local.reshape(1, 1)], F32, LANES, SUBLANES)
    rows_s = packed_gs.shape[0]
    all_gs = sequencer_exchange([jnp.broadcast_to(packed_gs[None], (N_CHIP,) + packed_gs.shape)], True, 4,
                                "gather_small_g")[0].reshape(N_DEV, rows_s, LANES)
    blocks = [all_gs[i] for i in range(N_DEV)]

    for n in late + early:
        w, m, v = w_loc[n], m_loc[n], v_loc[n]
        pcs = pieces[n].reshape((N_CHIP,) + w.shape[1:])
        if w.shape[2] % LANES and w.shape[1] % LANES == 0:
            outs = adam_big(*(jnp.swapaxes(a, 1, 2) for a in (w, m, v, pcs)), "adam_" + n)
            outs = [jnp.swapaxes(o, 1, 2) for o in outs]
        else:
            outs = adam_big(w, m, v, pcs, "adam_" + n)
        out_g[n], out_d[n], out_m[n], out_v[n] = outs

    def sum8(i, *b):
        acc = b[0]
        for t in b[1:]:
            acc = acc + t
        return acc

    gsum = rowwise("sum_small_g", sum8, blocks, [], [(LANES, F32)], tr=_tile(rows_s, 512, 8))[0]
    summed = _unpack(gsum, small_full_shapes)
    loss = summed[-1].reshape(())
    g_small = dict(zip(small, summed[:-1]))
    g_loc = {}
    for n in small:
        g = g_small[n].reshape((1,) + g_small[n].shape) if w_loc[n].ndim >= 3 else g_small[n]
        if axis[n] is not None:
            size = w_loc[n].shape[axis[n]]
            g = lax.dynamic_slice_in_dim(g, chip * size, size, axis=axis[n])
        g_loc[n] = g.reshape(w_loc[n].shape)
    loc_shapes = [w_loc[n].shape for n in small]
    pw, pm, pv, pg = (_pack([d[n] for n in small], F32, LANES, SUBLANES) for d in (w_loc, m_loc, v_loc, g_loc))
    dl, mn, vn = rowwise("adam_small", lambda i, a, b, c_, d_: _adam_tile(a, b, c_, d_), [pw, pm, pv, pg], [],
                         [(LANES, F32)] * 3, tr=_tile(pw.shape[0], 512, 8))
    for d_out, flat in ((out_d, dl), (out_m, mn), (out_v, vn)):
        for n, a in zip(small, _unpack(flat, loc_shapes)):
            d_out[n] = a
    for n in small:
        out_g[n] = g_loc[n]

    return (loss, grad_x, *[out_g[n] for n in names], *[out_d[n] for n in names], *[out_m[n] for n in names],
            *[out_v[n] for n in names])
```
